```python
import math
import jax
import jax.numpy as jnp
from jax import lax
import numpy as np

D_MODEL = 1024
BATCH = 8
SEQ = 8192
DEPTH = 1

HEAD_DIM = 64
A_Q_HEADS = 8
A_KV_HEADS = 2
B_Q_HEADS = 8
B_KV_HEADS = 2
BRANCH_WIDTH = A_Q_HEADS * HEAD_DIM
D_FF = 4 * D_MODEL
GRID_W = 64
Q_BLOCK = 128
WINDOW = 128
BAND_BLOCK = WINDOW
ROPE_THETA = 10000.0
AXIAL_THETA = 10000.0
NORM_EPS = 1e-6
NEG_INF = -1e30

A_Q_W = A_Q_HEADS * HEAD_DIM
A_KV_W = A_KV_HEADS * HEAD_DIM
B_Q_W = B_Q_HEADS * HEAD_DIM
B_KV_W = B_KV_HEADS * HEAD_DIM
IN_SPLITS = [A_Q_W, A_KV_W, A_KV_W, B_Q_W, B_KV_W, B_KV_W, D_MODEL, D_MODEL]
IN_WIDTH = sum(IN_SPLITS)

kernel_name = "hybrid_gated_axial_window_attention_block"


def rms_norm(x, g):
    xf = x.astype(jnp.float32)
    y = xf * lax.rsqrt(jnp.mean(xf * xf, axis=-1, keepdims=True) + NORM_EPS)
    return (y * g.astype(jnp.float32)).astype(x.dtype)


def rope_cos_sin(pos, dim, theta):
    inv = theta ** (-jnp.arange(0, dim, 2, dtype=jnp.float32) / dim)
    ang = pos.astype(jnp.float32)[:, None] * inv[None, :]
    return jnp.cos(ang), jnp.sin(ang)


def apply_rope(x, cos, sin):
    xf = x.astype(jnp.float32)
    half = xf.shape[-1] // 2
    x1, x2 = xf[..., :half], xf[..., half:]
    c = cos[None, :, None, :]
    s = sin[None, :, None, :]
    return jnp.concatenate([x1 * c - x2 * s, x1 * s + x2 * c], axis=-1).astype(x.dtype)


def apply_axial_rope(x, row, col):
    half = x.shape[-1] // 2
    cr, sr = rope_cos_sin(row, half, AXIAL_THETA)
    cc, sc = rope_cos_sin(col, half, AXIAL_THETA)
    return jnp.concatenate([apply_rope(x[..., :half], cr, sr),
                            apply_rope(x[..., half:], cc, sc)], axis=-1)


def global_attention(q, k, v):
    b, s, hq, dh = q.shape
    hkv = k.shape[2]
    g = hq // hkv
    nb = s // Q_BLOCK
    scale = dh ** -0.5
    qb = q.reshape(b, nb, Q_BLOCK, hkv, g, dh).transpose(1, 0, 2, 3, 4, 5)

    def one_block(qblk):
        sc = jnp.einsum('bqkgd,bskd->bkgqs', qblk, k).astype(jnp.float32) * scale
        p = jax.nn.softmax(sc, axis=-1).astype(v.dtype)
        return jnp.einsum('bkgqs,bskd->bqkgd', p, v)

    o = lax.map(one_block, qb)
    return o.transpose(1, 0, 2, 3, 4, 5).reshape(b, s, hq * dh)


def window_sink_attention(q, k, v, sink):
    b, s, hq, dh = q.shape
    hkv = k.shape[2]
    g = hq // hkv
    nb = s // BAND_BLOCK
    scale = dh ** -0.5
    pad = ((0, 0), (BAND_BLOCK, BAND_BLOCK), (0, 0), (0, 0))
    kr = jnp.pad(k, pad).reshape(b, nb + 2, BAND_BLOCK, hkv, dh)
    vr = jnp.pad(v, pad).reshape(b, nb + 2, BAND_BLOCK, hkv, dh)
    kb = jnp.concatenate([kr[:, :-2], kr[:, 1:-1], kr[:, 2:]], axis=2)
    vb = jnp.concatenate([vr[:, :-2], vr[:, 1:-1], vr[:, 2:]], axis=2)
    qb = q.reshape(b, nb, BAND_BLOCK, hkv, g, dh)
    sc = jnp.einsum('bnqkgd,bnskd->bnkgqs', qb, kb).astype(jnp.float32) * scale
    blk = jnp.arange(nb, dtype=jnp.int32)[:, None] * BAND_BLOCK
    qpos = blk + jnp.arange(BAND_BLOCK, dtype=jnp.int32)[None, :]
    kpos = blk - BAND_BLOCK + jnp.arange(3 * BAND_BLOCK, dtype=jnp.int32)[None, :]
    valid = (jnp.abs(kpos[:, None, :] - qpos[:, :, None]) <= WINDOW) \
        & (kpos[:, None, :] >= 0) & (kpos[:, None, :] < s)
    sc = jnp.where(valid[None, :, None, None], sc, NEG_INF)
    sink_l = jnp.broadcast_to(sink.astype(jnp.float32).reshape(1, 1, hkv, g, 1, 1),
                              sc.shape[:-1] + (1,))
    p = jax.nn.softmax(jnp.concatenate([sc, sink_l], axis=-1), axis=-1)[..., :-1]
    o = jnp.einsum('bnkgqs,bnskd->bnqkgd', p.astype(v.dtype), vb)
    return o.reshape(b, s, hq * dh)


def _fwd_setup_inputs(seed: int = 0) -> dict:
    key = jax.random.key(seed)
    ks = jax.random.split(key, 16)
    f32 = jnp.float32
    d = D_MODEL

    def nrm(k, shape, scale):
        return jax.random.normal(k, shape, f32) * scale

    return {
        "x": nrm(ks[0], (BATCH, SEQ, d), 1.0),
        "c": nrm(ks[1], (BATCH, d), 1.0),
        "w_ada": nrm(ks[2], (DEPTH, d, 6 * d), 0.02),
        "b_ada": nrm(ks[3], (DEPTH, 6 * d), 0.02),
        "norm1_g": 1.0 + nrm(ks[4], (DEPTH, d), 0.02),
        "w_in": nrm(ks[5], (DEPTH, d, IN_WIDTH), d ** -0.5),
        "q_norm_a": 1.0 + nrm(ks[6], (DEPTH, HEAD_DIM), 0.02),
        "k_norm_a": 1.0 + nrm(ks[7], (DEPTH, HEAD_DIM), 0.02),
        "sink_b": nrm(ks[8], (DEPTH, B_Q_HEADS), 0.5),
        "w_branch": nrm(ks[9], (DEPTH, 2, BRANCH_WIDTH, d), BRANCH_WIDTH ** -0.5),
        "w_out": nrm(ks[10], (DEPTH, d, d), d ** -0.5),
        "norm2_g": 1.0 + nrm(ks[11], (DEPTH, d), 0.02),
        "w_mlp_in": nrm(ks[12], (DEPTH, d, D_FF), d ** -0.5),
        "w_mlp_out": nrm(ks[13], (DEPTH, D_FF, d), D_FF ** -0.5),
        "final_g": 1.0 + nrm(ks[14], (d,), 0.02),
    }


def _fwd_reference(x, c, w_ada, b_ada, norm1_g, w_in, q_norm_a, k_norm_a, sink_b,
              w_branch, w_out, norm2_g, w_mlp_in, w_mlp_out, final_g):
    b, s, d = x.shape
    rows = s // GRID_W
    t = jnp.arange(s, dtype=jnp.int32)
    row_ids = jnp.repeat(jnp.arange(rows, dtype=jnp.int32), GRID_W)
    col_ids = jnp.tile(jnp.arange(GRID_W, dtype=jnp.int32), rows)
    cos1, sin1 = rope_cos_sin(t, HEAD_DIM, ROPE_THETA)
    offsets = np.cumsum(IN_SPLITS)[:-1].tolist()

    for l in range(DEPTH):
        mod = jax.nn.silu(c) @ w_ada[l] + b_ada[l]
        shift1, scale1, gate1, shift2, scale2, gate2 = jnp.split(mod, 6, axis=-1)

        h = rms_norm(x, norm1_g[l]) * (1.0 + scale1[:, None]) + shift1[:, None]
        proj = h @ w_in[l]
        qa, ka, va, qb, kb, vb, ga, gb = jnp.split(proj, offsets, axis=-1)

        qa = rms_norm(qa.reshape(b, s, A_Q_HEADS, HEAD_DIM), q_norm_a[l])
        ka = rms_norm(ka.reshape(b, s, A_KV_HEADS, HEAD_DIM), k_norm_a[l])
        qa = apply_axial_rope(qa, row_ids, col_ids)
        ka = apply_axial_rope(ka, row_ids, col_ids)
        ya = global_attention(qa, ka, va.reshape(b, s, A_KV_HEADS, HEAD_DIM))

        qb = apply_rope(qb.reshape(b, s, B_Q_HEADS, HEAD_DIM), cos1, sin1)
        kb = apply_rope(kb.reshape(b, s, B_KV_HEADS, HEAD_DIM), cos1, sin1)
        yb = window_sink_attention(qb, kb, vb.reshape(b, s, B_KV_HEADS, HEAD_DIM), sink_b[l])

        ua = ya @ w_branch[l, 0]
        ub = yb @ w_branch[l, 1]
        merged = jax.nn.sigmoid(ga) * ua + jax.nn.sigmoid(gb) * ub
        x = x + gate1[:, None] * (merged @ w_out[l])

        h2 = rms_norm(x, norm2_g[l]) * (1.0 + scale2[:, None]) + shift2[:, None]
        hid = jnp.square(jax.nn.relu(h2 @ w_mlp_in[l]))
        x = x + gate2[:, None] * (hid @ w_mlp_out[l])

    return rms_norm(x, final_g)


import jax as _jax
import jax.numpy as _jnp

TWIN_FORMAT = 'train_step'
FWD_PARAMS = ['x', 'c', 'w_ada', 'b_ada', 'norm1_g', 'w_in', 'q_norm_a', 'k_norm_a', 'sink_b', 'w_branch', 'w_out', 'norm2_g', 'w_mlp_in', 'w_mlp_out', 'final_g']
TWIN_WEIGHTS = ['w_ada', 'b_ada', 'norm1_g', 'w_in', 'q_norm_a', 'k_norm_a', 'sink_b', 'w_branch', 'w_out', 'norm2_g', 'w_mlp_in', 'w_mlp_out', 'final_g']
TWIN_DIFF_INPUT = 'x'
TWIN_INPUTS = ['x', 'c', 'w_ada', 'b_ada', 'norm1_g', 'w_in', 'q_norm_a', 'k_norm_a', 'sink_b', 'w_branch', 'w_out', 'norm2_g', 'w_mlp_in', 'w_mlp_out', 'final_g', 'loss_target', 'm_w_ada', 'm_b_ada', 'm_norm1_g', 'm_w_in', 'm_q_norm_a', 'm_k_norm_a', 'm_sink_b', 'm_w_branch', 'm_w_out', 'm_norm2_g', 'm_w_mlp_in', 'm_w_mlp_out', 'm_final_g', 'v_w_ada', 'v_b_ada', 'v_norm1_g', 'v_w_in', 'v_q_norm_a', 'v_k_norm_a', 'v_sink_b', 'v_w_branch', 'v_w_out', 'v_norm2_g', 'v_w_mlp_in', 'v_w_mlp_out', 'v_final_g']
TWIN_OUTPUTS = ['loss', 'grad_x', 'grad_w_ada', 'grad_b_ada', 'grad_norm1_g', 'grad_w_in', 'grad_q_norm_a', 'grad_k_norm_a', 'grad_sink_b', 'grad_w_branch', 'grad_w_out', 'grad_norm2_g', 'grad_w_mlp_in', 'grad_w_mlp_out', 'grad_final_g', 'delta_w_ada', 'delta_b_ada', 'delta_norm1_g', 'delta_w_in', 'delta_q_norm_a', 'delta_k_norm_a', 'delta_sink_b', 'delta_w_branch', 'delta_w_out', 'delta_norm2_g', 'delta_w_mlp_in', 'delta_w_mlp_out', 'delta_final_g', 'new_m_w_ada', 'new_m_b_ada', 'new_m_norm1_g', 'new_m_w_in', 'new_m_q_norm_a', 'new_m_k_norm_a', 'new_m_sink_b', 'new_m_w_branch', 'new_m_w_out', 'new_m_norm2_g', 'new_m_w_mlp_in', 'new_m_w_mlp_out', 'new_m_final_g', 'new_v_w_ada', 'new_v_b_ada', 'new_v_norm1_g', 'new_v_w_in', 'new_v_q_norm_a', 'new_v_k_norm_a', 'new_v_sink_b', 'new_v_w_branch', 'new_v_w_out', 'new_v_norm2_g', 'new_v_w_mlp_in', 'new_v_w_mlp_out', 'new_v_final_g']
TWIN_LEAF_KINDS = {'loss': 'loss', 'grad_x': 'grad_x', 'grad_w_ada': 'grad_w', 'grad_b_ada': 'grad_w', 'grad_norm1_g': 'grad_w', 'grad_w_in': 'grad_w', 'grad_q_norm_a': 'grad_w', 'grad_k_norm_a': 'grad_w', 'grad_sink_b': 'grad_w', 'grad_w_branch': 'grad_w', 'grad_w_out': 'grad_w', 'grad_norm2_g': 'grad_w', 'grad_w_mlp_in': 'grad_w', 'grad_w_mlp_out': 'grad_w', 'grad_final_g': 'grad_w', 'delta_w_ada': 'delta_w', 'delta_b_ada': 'delta_w', 'delta_norm1_g': 'delta_w', 'delta_w_in': 'delta_w', 'delta_q_norm_a': 'delta_w', 'delta_k_norm_a': 'delta_w', 'delta_sink_b': 'delta_w', 'delta_w_branch': 'delta_w', 'delta_w_out': 'delta_w', 'delta_norm2_g': 'delta_w', 'delta_w_mlp_in': 'delta_w', 'delta_w_mlp_out': 'delta_w', 'delta_final_g': 'delta_w', 'new_m_w_ada': 'new_m', 'new_m_b_ada': 'new_m', 'new_m_norm1_g': 'new_m', 'new_m_w_in': 'new_m', 'new_m_q_norm_a': 'new_m', 'new_m_k_norm_a': 'new_m', 'new_m_sink_b': 'new_m', 'new_m_w_branch': 'new_m', 'new_m_w_out': 'new_m', 'new_m_norm2_g': 'new_m', 'new_m_w_mlp_in': 'new_m', 'new_m_w_mlp_out': 'new_m', 'new_m_final_g': 'new_m', 'new_v_w_ada': 'new_v', 'new_v_b_ada': 'new_v', 'new_v_norm1_g': 'new_v', 'new_v_w_in': 'new_v', 'new_v_q_norm_a': 'new_v', 'new_v_k_norm_a': 'new_v', 'new_v_sink_b': 'new_v', 'new_v_w_branch': 'new_v', 'new_v_w_out': 'new_v', 'new_v_norm2_g': 'new_v', 'new_v_w_mlp_in': 'new_v', 'new_v_w_mlp_out': 'new_v', 'new_v_final_g': 'new_v'}


def _forward(args):
    return _fwd_reference(*[args[k] for k in FWD_PARAMS])


def _output_shape():
    def fwd():
        inp = _fwd_setup_inputs(0)
        return _fwd_reference(*[inp[k] for k in FWD_PARAMS])
    out = _jax.eval_shape(fwd)
    return out.shape, out.dtype

N_MICROBATCH = 1
ADAM_LR = 0.001
ADAM_B1 = 0.9
ADAM_B2 = 0.999
ADAM_EPS = 1e-08
ADAM_WD = 0.01
ADAM_STEP = 10
PER_EXAMPLE_BATCH_AXIS = {'x': 0, 'c': 0, 'loss_target': 0}
SHARED_INPUTS = []
_WEIGHT_DTYPES = {'w_ada': _jnp.float32, 'b_ada': _jnp.float32, 'norm1_g': _jnp.float32, 'w_in': _jnp.float32, 'q_norm_a': _jnp.float32, 'k_norm_a': _jnp.float32, 'sink_b': _jnp.float32, 'w_branch': _jnp.float32, 'w_out': _jnp.float32, 'norm2_g': _jnp.float32, 'w_mlp_in': _jnp.float32, 'w_mlp_out': _jnp.float32, 'final_g': _jnp.float32}
MOMENT_SCALE = {'w_ada': 1.414729e-01, 'b_ada': 2.750959e-01, 'norm1_g': 2.340608e-02, 'w_in': 2.066660e-02, 'q_norm_a': 2.616040e-02, 'k_norm_a': 2.544528e-02, 'sink_b': 1.007738e-03, 'w_branch': 2.315872e-02, 'w_out': 3.339329e-02, 'norm2_g': 1.502583e-01, 'w_mlp_in': 7.718684e-02, 'w_mlp_out': 1.530555e-01, 'final_g': 6.491471e+01}


def _to_microbatches(a, axis):
    t = _jnp.moveaxis(a, axis, 0)
    t = t.reshape((N_MICROBATCH, t.shape[0] // N_MICROBATCH) + t.shape[1:])
    return _jnp.moveaxis(t, 1, axis + 1)


def setup_inputs(seed: int = 0) -> dict:
    inp = _fwd_setup_inputs(seed)
    key = _jax.random.fold_in(_jax.random.key(seed), 7919)
    shape, _ = _output_shape()
    out = dict(inp)
    out["loss_target"] = _jax.random.normal(_jax.random.fold_in(key, 0), shape, _jnp.float32)
    for i, name in enumerate(TWIN_WEIGHTS):
        w = inp[name].astype(_jnp.float32)
        if MOMENT_SCALE is None:
            s = _jnp.sqrt(_jnp.mean(_jnp.square(w)) + 1e-30)
        else:
            s = MOMENT_SCALE[name]
        km, kv = _jax.random.split(_jax.random.fold_in(key, i + 1))
        out[name] = w
        out["m_" + name] = s * _jax.random.normal(km, w.shape, _jnp.float32)
        out["v_" + name] = (s * s) * _jax.random.uniform(kv, w.shape, _jnp.float32, 0.5, 1.5)
    if N_MICROBATCH > 1:
        for name, axis in PER_EXAMPLE_BATCH_AXIS.items():
            out[name] = _to_microbatches(out[name], axis)
    return {'x': out['x'], 'c': out['c'], 'w_ada': out['w_ada'], 'b_ada': out['b_ada'], 'norm1_g': out['norm1_g'], 'w_in': out['w_in'], 'q_norm_a': out['q_norm_a'], 'k_norm_a': out['k_norm_a'], 'sink_b': out['sink_b'], 'w_branch': out['w_branch'], 'w_out': out['w_out'], 'norm2_g': out['norm2_g'], 'w_mlp_in': out['w_mlp_in'], 'w_mlp_out': out['w_mlp_out'], 'final_g': out['final_g'], 'loss_target': out['loss_target'], 'm_w_ada': out['m_w_ada'], 'm_b_ada': out['m_b_ada'], 'm_norm1_g': out['m_norm1_g'], 'm_w_in': out['m_w_in'], 'm_q_norm_a': out['m_q_norm_a'], 'm_k_norm_a': out['m_k_norm_a'], 'm_sink_b': out['m_sink_b'], 'm_w_branch': out['m_w_branch'], 'm_w_out': out['m_w_out'], 'm_norm2_g': out['m_norm2_g'], 'm_w_mlp_in': out['m_w_mlp_in'], 'm_w_mlp_out': out['m_w_mlp_out'], 'm_final_g': out['m_final_g'], 'v_w_ada': out['v_w_ada'], 'v_b_ada': out['v_b_ada'], 'v_norm1_g': out['v_norm1_g'], 'v_w_in': out['v_w_in'], 'v_q_norm_a': out['v_q_norm_a'], 'v_k_norm_a': out['v_k_norm_a'], 'v_sink_b': out['v_sink_b'], 'v_w_branch': out['v_w_branch'], 'v_w_out': out['v_w_out'], 'v_norm2_g': out['v_norm2_g'], 'v_w_mlp_in': out['v_w_mlp_in'], 'v_w_mlp_out': out['v_w_mlp_out'], 'v_final_g': out['v_final_g']}


def _loss(weights, diff, rest, loss_target):
    with _jax.named_scope("forward"):
        args = {**rest, TWIN_DIFF_INPUT: diff, **{k: w.astype(_WEIGHT_DTYPES[k]) for k, w in weights.items()}}
        y = _forward(args)
    with _jax.named_scope("loss_head"):
        err = _jnp.square(y.astype(_jnp.float32) - loss_target)
        return 0.5 * _jnp.sum(_jnp.mean(err, axis=-1)) if err.ndim else 0.5 * err


def _adamw(w, g, m, v):
    m = ADAM_B1 * m + (1.0 - ADAM_B1) * g
    v = ADAM_B2 * v + (1.0 - ADAM_B2) * _jnp.square(g)
    m_hat = m / (1.0 - ADAM_B1 ** ADAM_STEP)
    v_hat = v / (1.0 - ADAM_B2 ** ADAM_STEP)
    delta = -ADAM_LR * (m_hat / (_jnp.sqrt(v_hat) + ADAM_EPS) + ADAM_WD * w)
    return delta, m, v


def reference(x, c, w_ada, b_ada, norm1_g, w_in, q_norm_a, k_norm_a, sink_b, w_branch, w_out, norm2_g, w_mlp_in, w_mlp_out, final_g, loss_target, m_w_ada, m_b_ada, m_norm1_g, m_w_in, m_q_norm_a, m_k_norm_a, m_sink_b, m_w_branch, m_w_out, m_norm2_g, m_w_mlp_in, m_w_mlp_out, m_final_g, v_w_ada, v_b_ada, v_norm1_g, v_w_in, v_q_norm_a, v_k_norm_a, v_sink_b, v_w_branch, v_w_out, v_norm2_g, v_w_mlp_in, v_w_mlp_out, v_final_g):
    given = dict(x=x, c=c, w_ada=w_ada, b_ada=b_ada, norm1_g=norm1_g, w_in=w_in, q_norm_a=q_norm_a, k_norm_a=k_norm_a, sink_b=sink_b, w_branch=w_branch, w_out=w_out, norm2_g=norm2_g, w_mlp_in=w_mlp_in, w_mlp_out=w_mlp_out, final_g=final_g, loss_target=loss_target, m_w_ada=m_w_ada, m_b_ada=m_b_ada, m_norm1_g=m_norm1_g, m_w_in=m_w_in, m_q_norm_a=m_q_norm_a, m_k_norm_a=m_k_norm_a, m_sink_b=m_sink_b, m_w_branch=m_w_branch, m_w_out=m_w_out, m_norm2_g=m_norm2_g, m_w_mlp_in=m_w_mlp_in, m_w_mlp_out=m_w_mlp_out, m_final_g=m_final_g, v_w_ada=v_w_ada, v_b_ada=v_b_ada, v_norm1_g=v_norm1_g, v_w_in=v_w_in, v_q_norm_a=v_q_norm_a, v_k_norm_a=v_k_norm_a, v_sink_b=v_sink_b, v_w_branch=v_w_branch, v_w_out=v_w_out, v_norm2_g=v_norm2_g, v_w_mlp_in=v_w_mlp_in, v_w_mlp_out=v_w_mlp_out, v_final_g=v_final_g)
    weights = {n: given[n] for n in TWIN_WEIGHTS}
    shared = {n: given[n] for n in SHARED_INPUTS}
    per_example = {n: given[n] for n in ['x', 'c']}
    grad_fn = _jax.value_and_grad(_loss, argnums=(0, 1))

    def one_microbatch(ex, loss_target):
        ex = dict(ex)
        diff = ex.pop(TWIN_DIFF_INPUT)
        return grad_fn(weights, diff, {**shared, **ex}, loss_target)

    if N_MICROBATCH == 1:
        loss, (grad_w, grad_x) = one_microbatch(per_example, given["loss_target"])
    else:
        def body(carry, xs):
            loss_sum, grad_sum = carry
            l_k, (gw_k, gx_k) = one_microbatch(xs[0], xs[1])
            with _jax.named_scope("update"):
                return (loss_sum + l_k, _jax.tree.map(_jnp.add, grad_sum, gw_k)), gx_k

        init = (_jnp.zeros((), _jnp.float32), _jax.tree.map(_jnp.zeros_like, weights))
        (loss, grad_w), grad_x = _jax.lax.scan(body, init, (per_example, given["loss_target"]))
    with _jax.named_scope("update"):
        delta_w, new_m, new_v = {}, {}, {}
        for n in TWIN_WEIGHTS:
            delta_w[n], new_m[n], new_v[n] = _adamw(weights[n], grad_w[n], given["m_" + n], given["v_" + n])
    return (loss, grad_x, *[grad_w[n] for n in TWIN_WEIGHTS], *[delta_w[n] for n in TWIN_WEIGHTS],
            *[new_m[n] for n in TWIN_WEIGHTS], *[new_v[n] for n in TWIN_WEIGHTS])
```

```python
import functools

import jax
import jax.numpy as jnp
from jax import lax
from jax.experimental import pallas as pl
from jax.experimental.pallas import tpu as pltpu

F32 = jnp.float32
BF16 = jnp.bfloat16
MESH = pl.DeviceIdType.MESH
ANY = pl.BlockSpec(memory_space=pl.ANY)

D_MODEL = 1024
HEAD_DIM = 64
Q_HEADS = 8
KV_HEADS = 2
GROUP = Q_HEADS // KV_HEADS
BRANCH_W = Q_HEADS * HEAD_DIM
KV_W = KV_HEADS * HEAD_DIM
IN_W = 2 * (BRANCH_W + 2 * KV_W) + 2 * D_MODEL
QK_W = 2 * (BRANCH_W + 2 * KV_W)
D_FF = 4 * D_MODEL
GRID_W = 64
WINDOW = 128
ROPE_THETA = 10000.0
NORM_EPS = 1e-6
NEG_INF = -1e30
Q_SCALE = HEAD_DIM ** -0.5
N_SHARD = 4
N_DEV = 8
LANES = 128
VMEM_LIMIT = 56 * 1024 * 1024

ADAM_LR = 0.001
ADAM_B1 = 0.9
ADAM_B2 = 0.999
ADAM_EPS = 1e-08
ADAM_WD = 0.01
ADAM_STEP = 10

_call = pl.pallas_call


def _params(sem=None, vmem=VMEM_LIMIT):
    return pltpu.CompilerParams(dimension_semantics=sem, vmem_limit_bytes=vmem)


def _nt(a, b):
    return lax.dot_general(a, b, (((1,), (1,)), ((), ())), preferred_element_type=F32)


def _tn(a, b):
    return lax.dot_general(a, b, (((0,), (0,)), ((), ())), preferred_element_type=F32)


def _nn(a, b):
    return jnp.dot(a, b, preferred_element_type=F32)


def _sigmoid(z):
    return 1.0 / (1.0 + jnp.exp(-z))


def _rope_tables(s):
    t = jnp.arange(s, dtype=jnp.int32)
    lane = jnp.arange(LANES, dtype=jnp.int32)

    def cos_sin(pos, dim):
        inv = ROPE_THETA ** (-jnp.arange(0, dim, 2, dtype=F32) / dim)
        ang = pos.astype(F32)[:, None] * inv[None, :]
        return jnp.cos(ang), jnp.sin(ang)

    cr, sr = cos_sin(t // GRID_W, HEAD_DIM // 2)
    cc, sc = cos_sin(t % GRID_W, HEAD_DIM // 2)
    cos_a = jnp.tile(jnp.concatenate([cr, cr, cc, cc], axis=1), (1, 2))
    sin_a = jnp.tile(jnp.concatenate([sr, sr, sc, sc], axis=1), (1, 2))
    first_a = (lane % 32) < 16
    c1, s1 = cos_sin(t, HEAD_DIM)
    cos_b = jnp.tile(jnp.concatenate([c1, c1], axis=1), (1, 2))
    sin_b = jnp.tile(jnp.concatenate([s1, s1], axis=1), (1, 2))
    first_b = (lane % 64) < 32
    tabs_a = (cos_a, jnp.where(first_a, -sin_a, 0.0), jnp.where(first_a, 0.0, sin_a))
    tabs_b = (cos_b, jnp.where(first_b, -sin_b, 0.0), jnp.where(first_b, 0.0, sin_b))
    return tabs_a + tabs_b


def _rope(z, cos, s_lo, s_hi, half, sign=1.0):
    up = pltpu.roll(z, LANES - half, 1)
    dn = pltpu.roll(z, half, 1)
    return z * cos + sign * (up * s_lo + dn * s_hi)


def _head_mean(z2, bd):
    hi = z2.astype(BF16)
    lo = (z2 - hi.astype(F32)).astype(BF16)
    return _nn(hi, bd) + _nn(lo, bd)


def _block_diag():
    lane = jnp.arange(LANES)
    return jnp.where((lane[:, None] // HEAD_DIM) == (lane[None, :] // HEAD_DIM), 1.0 / HEAD_DIM, 0.0).astype(BF16)


def _row_spec(tm, width):
    return pl.BlockSpec((tm, width), lambda i: (i, 0))


def _heads_spec(heads, tm):
    return pl.BlockSpec((heads, tm, HEAD_DIM), lambda i: (0, i, 0))


def _full_spec(shape):
    nd = len(shape)
    return pl.BlockSpec(shape, lambda i: (0,) * nd)


def _in_proj(x, mod6, g1, w_in_s, gq, gk, bd, tabs, tm=256):
    s = x.shape[0]

    def body(x_ref, mod_ref, g1_ref, w_ref, gq_ref, gk_ref, bd_ref, ca, la, ha, cb, lb, hb,
             h_ref, qkraw_ref, qa_ref, ka_ref, va_ref, qb_ref, kb_ref, vb_ref, gate_ref):
        xt = x_ref[...]
        r = lax.rsqrt(jnp.mean(xt * xt, axis=-1, keepdims=True) + NORM_EPS)
        h = (xt * r * g1_ref[...]) * (1.0 + mod_ref[1:2, :]) + mod_ref[0:1, :]
        hb16 = h.astype(BF16)
        h_ref[...] = hb16
        proj = jnp.concatenate([_nn(hb16, w_ref[j]) for j in range(N_SHARD)], axis=1)
        qkraw_ref[...] = proj[:, :BRANCH_W + KV_W]
        bdm = bd_ref[...]
        tab_a = (ca[...], la[...], ha[...])
        tab_b = (cb[...], lb[...], hb[...])

        def norm_rope_a(z, gain):
            zn = z * lax.rsqrt(_head_mean(z * z, bdm) + NORM_EPS) * gain
            return _rope(zn, *tab_a, 16)

        def put(ref, first, z):
            zb = z.astype(BF16)
            ref[first] = zb[:, :HEAD_DIM]
            ref[first + 1] = zb[:, HEAD_DIM:]

        for i in range(Q_HEADS // 2):
            put(qa_ref, 2 * i, norm_rope_a(proj[:, LANES * i:LANES * (i + 1)], gq_ref[...]) * Q_SCALE)
        off = BRANCH_W
        put(ka_ref, 0, norm_rope_a(proj[:, off:off + LANES], gk_ref[...]))
        off += KV_W
        put(va_ref, 0, proj[:, off:off + LANES])
        off += KV_W
        for i in range(Q_HEADS // 2):
            put(qb_ref, 2 * i, _rope(proj[:, off + LANES * i:off + LANES * (i + 1)], *tab_b, 32) * Q_SCALE)
        off += BRANCH_W
        put(kb_ref, 0, _rope(proj[:, off:off + LANES], *tab_b, 32))
        off += KV_W
        put(vb_ref, 0, proj[:, off:off + LANES])
        gate_ref[...] = proj[:, QK_W:]

    tab_spec = _row_spec(tm, LANES)
    return _call(
        body, name="in_proj", grid=(s // tm,),
        in_specs=[_row_spec(tm, D_MODEL), _full_spec(mod6.shape), _full_spec(g1.shape), _full_spec(w_in_s.shape),
                  _full_spec(gq.shape), _full_spec(gk.shape), _full_spec(bd.shape)] + [tab_spec] * 6,
        out_specs=[_row_spec(tm, D_MODEL), _row_spec(tm, BRANCH_W + KV_W), _heads_spec(Q_HEADS, tm), _heads_spec(KV_HEADS, tm),
                   _heads_spec(KV_HEADS, tm), _heads_spec(Q_HEADS, tm), _heads_spec(KV_HEADS, tm), _heads_spec(KV_HEADS, tm),
                   _row_spec(tm, 2 * D_MODEL)],
        out_shape=[jax.ShapeDtypeStruct((s, D_MODEL), BF16), jax.ShapeDtypeStruct((s, BRANCH_W + KV_W), F32),
                   jax.ShapeDtypeStruct((Q_HEADS, s, HEAD_DIM), BF16), jax.ShapeDtypeStruct((KV_HEADS, s, HEAD_DIM), BF16),
                   jax.ShapeDtypeStruct((KV_HEADS, s, HEAD_DIM), BF16), jax.ShapeDtypeStruct((Q_HEADS, s, HEAD_DIM), BF16),
                   jax.ShapeDtypeStruct((KV_HEADS, s, HEAD_DIM), BF16), jax.ShapeDtypeStruct((KV_HEADS, s, HEAD_DIM), BF16),
                   jax.ShapeDtypeStruct((s, 2 * D_MODEL), F32)],
        compiler_params=_params(("parallel",)),
    )(x, mod6, g1, w_in_s, gq, gk, bd, *tabs)


def _group_specs(s, tq):
    q_spec = pl.BlockSpec((None, GROUP, tq, HEAD_DIM), lambda g, i: (g, 0, i, 0))
    kv_spec = pl.BlockSpec((None, s, HEAD_DIM), lambda g, i: (g, 0, 0))
    col_spec = pl.BlockSpec((None, GROUP, tq, 1), lambda g, i: (g, 0, i, 0))
    return q_spec, kv_spec, col_spec


def _attn_a_fwd(q, k, v, tq=256, tk=512):
    s = q.shape[1]
    tk = min(tk, s)
    rows = GROUP * tq

    def body(q_ref, k_ref, v_ref, o_ref, lse_ref):
        qq = q_ref[...].reshape(rows, HEAD_DIM)

        def step(i, carry):
            m, l, acc = carry
            at = pl.ds(pl.multiple_of(i * tk, tk), tk)
            sc = _nt(qq, k_ref[at, :])
            mn = jnp.maximum(m, jnp.max(sc, axis=-1, keepdims=True))
            p = jnp.exp(sc - mn)
            a = jnp.exp(m - mn)
            l = a * l + jnp.sum(p, axis=-1, keepdims=True)
            acc = a * acc + _nn(p.astype(BF16), v_ref[at, :])
            return mn, l, acc

        m, l, acc = lax.fori_loop(0, s // tk, step, (jnp.full((rows, 1), NEG_INF, F32), jnp.zeros((rows, 1), F32),
                                                     jnp.zeros((rows, HEAD_DIM), F32)))
        o = (acc / l).astype(BF16)
        for g in range(GROUP):
            o_ref[:, HEAD_DIM * g:HEAD_DIM * (g + 1)] = o[tq * g:tq * (g + 1)]
        lse_ref[...] = (m + jnp.log(l)).reshape(GROUP, tq, 1)

    q_spec, kv_spec, col_spec = _group_specs(s, tq)
    return _call(
        body, name="attn_a_fwd", grid=(KV_HEADS, s // tq),
        in_specs=[q_spec, kv_spec, kv_spec],
        out_specs=[pl.BlockSpec((tq, GROUP * HEAD_DIM), lambda g, i: (i, g)), col_spec],
        out_shape=[jax.ShapeDtypeStruct((s, BRANCH_W), BF16), jax.ShapeDtypeStruct((KV_HEADS, GROUP, s, 1), F32)],
        compiler_params=_params(("parallel", "parallel")),
    )(q.reshape(KV_HEADS, GROUP, s, HEAD_DIM), k, v)


def _attn_a_bwd(q, k, v, do, lse, delta, tq=256, tk=512):
    s = q.shape[1]
    tk = min(tk, s)
    rows = GROUP * tq

    def body(q_ref, k_ref, v_ref, do_ref, lse_ref, dl_ref, dq_ref, dk_ref, dv_ref):
        @pl.when(pl.program_id(1) == 0)
        def _():
            dk_ref[...] = jnp.zeros_like(dk_ref)
            dv_ref[...] = jnp.zeros_like(dv_ref)

        qq = q_ref[...].reshape(rows, HEAD_DIM)
        dd = do_ref[...].reshape(rows, HEAD_DIM)
        ls = lse_ref[...].reshape(rows, 1)
        dl = dl_ref[...].reshape(rows, 1)

        def step(i, dq):
            at = pl.ds(pl.multiple_of(i * tk, tk), tk)
            kk = k_ref[at, :]
            p = jnp.exp(_nt(qq, kk) - ls)
            ds = (p * (_nt(dd, v_ref[at, :]) - dl)).astype(BF16)
            dv_ref[at, :] += _tn(p.astype(BF16), dd)
            dk_ref[at, :] += _tn(ds, qq)
            return dq + _nn(ds, kk)

        dq = lax.fori_loop(0, s // tk, step, jnp.zeros((rows, HEAD_DIM), F32))
        dq_ref[...] = dq.reshape(GROUP, tq, HEAD_DIM)

    q_spec, kv_spec, col_spec = _group_specs(s, tq)
    shape4 = (KV_HEADS, GROUP, s, HEAD_DIM)
    return _call(
        body, name="attn_a_bwd", grid=(KV_HEADS, s // tq),
        in_specs=[q_spec, kv_spec, kv_spec, q_spec, col_spec, col_spec],
        out_specs=[q_spec, kv_spec, kv_spec],
        out_shape=[jax.ShapeDtypeStruct(shape4, F32), jax.ShapeDtypeStruct((KV_HEADS, s, HEAD_DIM), F32),
                   jax.ShapeDtypeStruct((KV_HEADS, s, HEAD_DIM), F32)],
        compiler_params=_params(("parallel", "arbitrary")),
    )(q.reshape(shape4), k, v, do.reshape(shape4), lse, delta)


WIN_KEYS = 4 * WINDOW


def _window_start(i, tq, s):
    return pl.multiple_of(jnp.clip(i * tq - WINDOW, 0, s - WIN_KEYS), WINDOW)


def _window_mask(i, start, tq):
    qpos = i * tq + (lax.broadcasted_iota(jnp.int32, (GROUP * tq, WIN_KEYS), 0) & (tq - 1))
    kpos = start + lax.broadcasted_iota(jnp.int32, (GROUP * tq, WIN_KEYS), 1)
    return jnp.abs(kpos - qpos) <= WINDOW


def _attn_b_fwd(q, k, v, sink_col, tq=2 * WINDOW):
    s = q.shape[1]
    rows = GROUP * tq

    def body(q_ref, k_ref, v_ref, sink_ref, o_ref, lse_ref):
        i = pl.program_id(1)
        start = _window_start(i, tq, s)
        at = pl.ds(start, WIN_KEYS)
        qq = q_ref[...].reshape(rows, HEAD_DIM)
        sc = jnp.where(_window_mask(i, start, tq), _nt(qq, k_ref[at, :]), NEG_INF)
        sk = sink_ref[...]
        m = jnp.maximum(jnp.max(sc, axis=-1, keepdims=True), sk)
        p = jnp.exp(sc - m)
        l = jnp.sum(p, axis=-1, keepdims=True) + jnp.exp(sk - m)
        o = (_nn(p.astype(BF16), v_ref[at, :]) / l).astype(BF16)
        for g in range(GROUP):
            o_ref[:, HEAD_DIM * g:HEAD_DIM * (g + 1)] = o[tq * g:tq * (g + 1)]
        lse_ref[...] = (m + jnp.log(l)).reshape(GROUP, tq, 1)

    q_spec, kv_spec, col_spec = _group_specs(s, tq)
    sink_spec = pl.BlockSpec((None, rows, 1), lambda g, i: (g, 0, 0))
    return _call(
        body, name="attn_b_fwd", grid=(KV_HEADS, s // tq),
        in_specs=[q_spec, kv_spec, kv_spec, sink_spec],
        out_specs=[pl.BlockSpec((tq, GROUP * HEAD_DIM), lambda g, i: (i, g)), col_spec],
        out_shape=[jax.ShapeDtypeStruct((s, BRANCH_W), BF16), jax.ShapeDtypeStruct((KV_HEADS, GROUP, s, 1), F32)],
        compiler_params=_params(("parallel", "parallel")),
    )(q.reshape(KV_HEADS, GROUP, s, HEAD_DIM), k, v, sink_col)


def _attn_b_bwd(q, k, v, do, lse, delta, sink_col, tq=2 * WINDOW):
    s = q.shape[1]
    rows = GROUP * tq

    def body(q_ref, k_ref, v_ref, do_ref, lse_ref, dl_ref, sink_ref, dq_ref, dk_ref, dv_ref, dsink_ref):
        i = pl.program_id(1)

        @pl.when(i == 0)
        def _():
            dk_ref[...] = jnp.zeros_like(dk_ref)
            dv_ref[...] = jnp.zeros_like(dv_ref)
            dsink_ref[...] = jnp.zeros_like(dsink_ref)

        start = _window_start(i, tq, s)
        at = pl.ds(start, WIN_KEYS)
        qq = q_ref[...].reshape(rows, HEAD_DIM)
        dd = do_ref[...].reshape(rows, HEAD_DIM)
        ls = lse_ref[...].reshape(rows, 1)
        dl = dl_ref[...].reshape(rows, 1)
        kk = k_ref[at, :]
        sc = jnp.where(_window_mask(i, start, tq), _nt(qq, kk), NEG_INF)
        p = jnp.exp(sc - ls)
        ds = (p * (_nt(dd, v_ref[at, :]) - dl)).astype(BF16)
        dv_ref[at, :] += _tn(p.astype(BF16), dd)
        dk_ref[at, :] += _tn(ds, qq)
        dq_ref[...] = _nn(ds, kk).reshape(GROUP, tq, HEAD_DIM)
        dsk = jnp.exp(sink_ref[...] - ls) * dl
        for g in range(GROUP):
            dsink_ref[g:g + 1, :] -= jnp.broadcast_to(jnp.sum(dsk[tq * g:tq * (g + 1)], axis=0, keepdims=True), (1, LANES))

    q_spec, kv_spec, col_spec = _group_specs(s, tq)
    sink_spec = pl.BlockSpec((None, rows, 1), lambda g, i: (g, 0, 0))
    dsink_spec = pl.BlockSpec((None, ACC_ROWS, LANES), lambda g, i: (g, 0, 0))
    shape4 = (KV_HEADS, GROUP, s, HEAD_DIM)
    return _call(
        body, name="attn_b_bwd", grid=(KV_HEADS, s // tq),
        in_specs=[q_spec, kv_spec, kv_spec, q_spec, col_spec, col_spec, sink_spec],
        out_specs=[q_spec, kv_spec, kv_spec, dsink_spec],
        out_shape=[jax.ShapeDtypeStruct(shape4, F32), jax.ShapeDtypeStruct((KV_HEADS, s, HEAD_DIM), F32),
                   jax.ShapeDtypeStruct((KV_HEADS, s, HEAD_DIM), F32), jax.ShapeDtypeStruct((KV_HEADS, ACC_ROWS, LANES), F32)],
        compiler_params=_params(("parallel", "arbitrary")),
    )(q.reshape(shape4), k, v, do.reshape(shape4), lse, delta, sink_col)


def _post_attn(ya, yb, gates, x, mod6, wbr_s, w_out, tm=256):
    s = x.shape[0]

    def body(ya_ref, yb_ref, g_ref, x_ref, mod_ref, wbr_ref, wo_ref, ua_ref, ub_ref, mg_ref, o_ref, x1_ref):
        ya_t, yb_t = ya_ref[...], yb_ref[...]
        ua = jnp.concatenate([_nn(ya_t, wbr_ref[j, 0]) for j in range(N_SHARD)], axis=1)
        ub = jnp.concatenate([_nn(yb_t, wbr_ref[j, 1]) for j in range(N_SHARD)], axis=1)
        merged = (_sigmoid(g_ref[:, :D_MODEL]) * ua + _sigmoid(g_ref[:, D_MODEL:]) * ub).astype(BF16)
        o = _nn(merged, wo_ref[...])
        ua_ref[...] = ua.astype(BF16)
        ub_ref[...] = ub.astype(BF16)
        mg_ref[...] = merged
        o_ref[...] = o.astype(BF16)
        x1_ref[...] = x_ref[...] + mod_ref[2:3, :] * o

    bf = jax.ShapeDtypeStruct((s, D_MODEL), BF16)
    return _call(
        body, name="post_attn", grid=(s // tm,),
        in_specs=[_row_spec(tm, BRANCH_W), _row_spec(tm, BRANCH_W), _row_spec(tm, 2 * D_MODEL), _row_spec(tm, D_MODEL),
                  _full_spec(mod6.shape), _full_spec(wbr_s.shape), _full_spec(w_out.shape)],
        out_specs=[_row_spec(tm, D_MODEL)] * 5,
        out_shape=[bf, bf, bf, bf, jax.ShapeDtypeStruct((s, D_MODEL), F32)],
        compiler_params=_params(("parallel",)),
    )(ya, yb, gates, x, mod6, wbr_s, w_out)


def _mlp_in(x1, mod6, g2, w_mi_s, tm=256):
    s = x1.shape[0]

    def body(x_ref, mod_ref, g_ref, w_ref, h2_ref, a_ref, hid_ref):
        xt = x_ref[...]
        r = lax.rsqrt(jnp.mean(xt * xt, axis=-1, keepdims=True) + NORM_EPS)
        h2 = ((xt * r * g_ref[...]) * (1.0 + mod_ref[4:5, :]) + mod_ref[3:4, :]).astype(BF16)
        h2_ref[...] = h2
        a = jnp.concatenate([_nn(h2, w_ref[j]) for j in range(N_SHARD)], axis=1)
        a_ref[...] = a.astype(BF16)
        hid_ref[...] = jnp.square(jnp.maximum(a, 0.0)).astype(BF16)

    return _call(
        body, name="mlp_in", grid=(s // tm,),
        in_specs=[_row_spec(tm, D_MODEL), _full_spec(mod6.shape), _full_spec(g2.shape), _full_spec(w_mi_s.shape)],
        out_specs=[_row_spec(tm, D_MODEL), _row_spec(tm, D_FF), _row_spec(tm, D_FF)],
        out_shape=[jax.ShapeDtypeStruct((s, D_MODEL), BF16), jax.ShapeDtypeStruct((s, D_FF), BF16),
                   jax.ShapeDtypeStruct((s, D_FF), BF16)],
        compiler_params=_params(("parallel",)),
    )(x1, mod6, g2, w_mi_s)


ACC_ROWS = 8


def _acc_spec():
    return pl.BlockSpec((ACC_ROWS, D_MODEL), lambda i: (0, 0))


def _acc_add(acc_ref, rows):
    @pl.when(pl.program_id(0) == 0)
    def _():
        acc_ref[...] = jnp.zeros_like(acc_ref)

    for r, val in enumerate(rows):
        acc_ref[r:r + 1, :] += jnp.sum(val, axis=0, keepdims=True)


def _mlp_out_loss(hid, x1, a, target, mod6, gf, w_mo, tm=256):
    s = x1.shape[0]

    def body(hid_ref, x_ref, a_ref, t_ref, mod_ref, gf_ref, w_ref, dx2_ref, dm_ref, da_ref, acc_ref):
        m = _nn(hid_ref[...], w_ref[...])
        gate2 = mod_ref[5:6, :]
        x2 = x_ref[...] + gate2 * m
        r = lax.rsqrt(jnp.mean(x2 * x2, axis=-1, keepdims=True) + NORM_EPS)
        xn = x2 * r
        err = xn * gf_ref[...] - t_ref[...]
        dy = err * (1.0 / D_MODEL)
        dxn = dy * gf_ref[...]
        dx2 = r * (dxn - xn * jnp.mean(dxn * xn, axis=-1, keepdims=True))
        dx2_ref[...] = dx2
        dm = (dx2 * gate2).astype(BF16)
        dm_ref[...] = dm
        da_ref[...] = (_nt(dm, w_ref[...]) * (2.0 * jnp.maximum(a_ref[...].astype(F32), 0.0))).astype(BF16)
        _acc_add(acc_ref, [err * err, dy * xn, dx2 * m])

    return _call(
        body, name="mlp_out_loss", grid=(s // tm,),
        in_specs=[_row_spec(tm, D_FF), _row_spec(tm, D_MODEL), _row_spec(tm, D_FF), _row_spec(tm, D_MODEL),
                  _full_spec(mod6.shape), _full_spec(gf.shape), _full_spec(w_mo.shape)],
        out_specs=[_row_spec(tm, D_MODEL), _row_spec(tm, D_MODEL), _row_spec(tm, D_FF), _acc_spec()],
        out_shape=[jax.ShapeDtypeStruct((s, D_MODEL), F32), jax.ShapeDtypeStruct((s, D_MODEL), BF16),
                   jax.ShapeDtypeStruct((s, D_FF), BF16), jax.ShapeDtypeStruct((ACC_ROWS, D_MODEL), F32)],
        compiler_params=_params(("arbitrary",)),
    )(hid, x1, a, target, mod6, gf, w_mo)


def _norm_bwd(dh, xt, gain, scale):
    r = lax.rsqrt(jnp.mean(xt * xt, axis=-1, keepdims=True) + NORM_EPS)
    xn = xt * r
    dxn = dh * (gain * (1.0 + scale))
    dx = r * (dxn - xn * jnp.mean(dxn * xn, axis=-1, keepdims=True))
    return dx, [dh, dh * xn * gain, dh * xn * (1.0 + scale)]


def _mlp_bwd(da, x1, dx2, o, mod6, g2, w_mi_s, tm=256):
    s = x1.shape[0]

    def body(da_ref, x_ref, dx2_ref, o_ref, mod_ref, g_ref, w_ref, dx1_ref, do_ref, acc_ref):
        dh2 = _nt(da_ref[:, :D_MODEL], w_ref[0])
        for j in range(1, N_SHARD):
            dh2 += _nt(da_ref[:, D_MODEL * j:D_MODEL * (j + 1)], w_ref[j])
        dx, sums = _norm_bwd(dh2, x_ref[...], g_ref[...], mod_ref[4:5, :])
        dx1 = dx2_ref[...] + dx
        dx1_ref[...] = dx1
        do_ref[...] = (dx1 * mod_ref[2:3, :]).astype(BF16)
        _acc_add(acc_ref, sums + [dx1 * o_ref[...].astype(F32)])

    return _call(
        body, name="mlp_bwd", grid=(s // tm,),
        in_specs=[_row_spec(tm, D_FF), _row_spec(tm, D_MODEL), _row_spec(tm, D_MODEL), _row_spec(tm, D_MODEL),
                  _full_spec(mod6.shape), _full_spec(g2.shape), _full_spec(w_mi_s.shape)],
        out_specs=[_row_spec(tm, D_MODEL), _row_spec(tm, D_MODEL), _acc_spec()],
        out_shape=[jax.ShapeDtypeStruct((s, D_MODEL), F32), jax.ShapeDtypeStruct((s, D_MODEL), BF16),
                   jax.ShapeDtypeStruct((ACC_ROWS, D_MODEL), F32)],
        compiler_params=_params(("arbitrary",)),
    )(da, x1, dx2, o, mod6, g2, w_mi_s)


def _merge_bwd(do, gates, ua, ub, ya, yb, w_out, wbr_s, tm=256):
    s = do.shape[0]

    def body(do_ref, g_ref, ua_ref, ub_ref, ya_ref, yb_ref, wo_ref, wbr_ref,
             dua_ref, dub_ref, dg_ref, doa_ref, dob_ref, dla_ref, dlb_ref):
        dmerged = _nt(do_ref[...], wo_ref[...])
        for b, (u_ref, y_ref, du_ref, dy_ref, dl_ref) in enumerate(
                ((ua_ref, ya_ref, dua_ref, doa_ref, dla_ref), (ub_ref, yb_ref, dub_ref, dob_ref, dlb_ref))):
            sg = _sigmoid(g_ref[:, D_MODEL * b:D_MODEL * (b + 1)])
            du = (dmerged * sg).astype(BF16)
            du_ref[...] = du
            dg_ref[:, D_MODEL * b:D_MODEL * (b + 1)] = (dmerged * u_ref[...].astype(F32) * sg * (1.0 - sg)).astype(BF16)
            w = BRANCH_W // 2
            dy = _nt(du[:, :w], wbr_ref[0, b])
            for j in range(1, N_SHARD):
                dy += _nt(du[:, w * j:w * (j + 1)], wbr_ref[j, b])
            prod = dy * y_ref[...].astype(F32)
            dyb = dy.astype(BF16)
            for h in range(Q_HEADS):
                at = slice(HEAD_DIM * h, HEAD_DIM * (h + 1))
                dy_ref[h] = dyb[:, at]
                dl_ref[h] = jnp.sum(prod[:, at], axis=-1, keepdims=True)

    bf = jax.ShapeDtypeStruct((s, D_MODEL), BF16)
    heads = jax.ShapeDtypeStruct((Q_HEADS, s, HEAD_DIM), BF16)
    cols = jax.ShapeDtypeStruct((Q_HEADS, s, 1), F32)
    col_spec = pl.BlockSpec((Q_HEADS, tm, 1), lambda i: (0, i, 0))
    return _call(
        body, name="merge_bwd", grid=(s // tm,),
        in_specs=[_row_spec(tm, D_MODEL), _row_spec(tm, 2 * D_MODEL), _row_spec(tm, D_MODEL), _row_spec(tm, D_MODEL),
                  _row_spec(tm, BRANCH_W), _row_spec(tm, BRANCH_W), _full_spec(w_out.shape), _full_spec(wbr_s.shape)],
        out_specs=[_row_spec(tm, D_MODEL), _row_spec(tm, D_MODEL), _row_spec(tm, 2 * D_MODEL),
                   _heads_spec(Q_HEADS, tm), _heads_spec(Q_HEADS, tm), col_spec, col_spec],
        out_shape=[bf, bf, jax.ShapeDtypeStruct((s, 2 * D_MODEL), BF16), heads, heads, cols, cols],
        compiler_params=_params(("parallel",)),
    )(do, gates, ua, ub, ya, yb, w_out, wbr_s)


def _qk_bwd(dqa, dka, dva, dqb, dkb, dvb, qkraw, dgates, gq, gk, bd, tabs, tm=256):
    s = qkraw.shape[0]

    def body(dqa_ref, dka_ref, dva_ref, dqb_ref, dkb_ref, dvb_ref, raw_ref, dg_ref, gq_ref, gk_ref, bd_ref,
             ca, la, ha, cb, lb, hb, dp_ref, acc_ref, pair_ref):
        bdm = bd_ref[...]
        tab_a = (ca[...], la[...], ha[...])
        tab_b = (cb[...], lb[...], hb[...])

        def pair(ref, first):
            pair_ref[:, :HEAD_DIM] = ref[first]
            pair_ref[:, HEAD_DIM:] = ref[first + 1]
            return pair_ref[...]

        def norm_rope_a_bwd(dz, raw, gain):
            dzn = _rope(dz, *tab_a, 16, sign=-1.0)
            rinv = lax.rsqrt(_head_mean(raw * raw, bdm) + NORM_EPS)
            zhat = raw * rinv
            dzhat = dzn * gain
            return rinv * (dzhat - zhat * _head_mean(dzhat * zhat, bdm)), dzn * zhat

        gq_rows = jnp.zeros((tm, LANES), F32)
        for i in range(Q_HEADS // 2):
            at = slice(LANES * i, LANES * (i + 1))
            draw, gsum = norm_rope_a_bwd(pair(dqa_ref, 2 * i) * Q_SCALE, raw_ref[:, at], gq_ref[...])
            dp_ref[:, at] = draw.astype(BF16)
            gq_rows += gsum
        off = BRANCH_W
        draw, gk_rows = norm_rope_a_bwd(pair(dka_ref, 0), raw_ref[:, off:off + LANES], gk_ref[...])
        dp_ref[:, off:off + LANES] = draw.astype(BF16)
        off += KV_W
        dp_ref[:, off:off + LANES] = pair(dva_ref, 0).astype(BF16)
        off += KV_W
        for i in range(Q_HEADS // 2):
            dz = _rope(pair(dqb_ref, 2 * i) * Q_SCALE, *tab_b, 32, sign=-1.0)
            dp_ref[:, off + LANES * i:off + LANES * (i + 1)] = dz.astype(BF16)
        off += BRANCH_W
        dp_ref[:, off:off + LANES] = _rope(pair(dkb_ref, 0), *tab_b, 32, sign=-1.0).astype(BF16)
        off += KV_W
        dp_ref[:, off:off + LANES] = pair(dvb_ref, 0).astype(BF16)
        dp_ref[:, QK_W:] = dg_ref[...]

        @pl.when(pl.program_id(0) == 0)
        def _():
            acc_ref[...] = jnp.zeros_like(acc_ref)

        acc_ref[0:1, :] += jnp.sum(gq_rows, axis=0, keepdims=True)
        acc_ref[1:2, :] += jnp.sum(gk_rows, axis=0, keepdims=True)

    tab_spec = _row_spec(tm, LANES)
    return _call(
        body, name="qk_bwd", grid=(s // tm,),
        in_specs=[_heads_spec(Q_HEADS, tm), _heads_spec(KV_HEADS, tm), _heads_spec(KV_HEADS, tm),
                  _heads_spec(Q_HEADS, tm), _heads_spec(KV_HEADS, tm), _heads_spec(KV_HEADS, tm),
                  _row_spec(tm, BRANCH_W + KV_W), _row_spec(tm, 2 * D_MODEL),
                  _full_spec(gq.shape), _full_spec(gk.shape), _full_spec(bd.shape)] + [tab_spec] * 6,
        out_specs=[_row_spec(tm, IN_W), pl.BlockSpec((ACC_ROWS, LANES), lambda i: (0, 0))],
        out_shape=[jax.ShapeDtypeStruct((s, IN_W), BF16), jax.ShapeDtypeStruct((ACC_ROWS, LANES), F32)],
        scratch_shapes=[pltpu.VMEM((tm, LANES), F32)],
        compiler_params=_params(("arbitrary",)),
    )(dqa, dka, dva, dqb, dkb, dvb, qkraw, dgates, gq, gk, bd, *tabs)


def _in_proj_bwd(dproj, x, dx1, mod6, g1, w_in_s, tm=256):
    s = x.shape[0]
    w = IN_W // N_SHARD

    def body(dp_ref, x_ref, dx1_ref, mod_ref, g_ref, w_ref, gx_ref, acc_ref):
        dh = _nt(dp_ref[:, :w], w_ref[0])
        for j in range(1, N_SHARD):
            dh += _nt(dp_ref[:, w * j:w * (j + 1)], w_ref[j])
        dx, sums = _norm_bwd(dh, x_ref[...], g_ref[...], mod_ref[1:2, :])
        gx_ref[...] = dx1_ref[...] + dx
        _acc_add(acc_ref, sums)

    return _call(
        body, name="in_proj_bwd", grid=(s // tm,),
        in_specs=[_row_spec(tm, IN_W), _row_spec(tm, D_MODEL), _row_spec(tm, D_MODEL),
                  _full_spec(mod6.shape), _full_spec(g1.shape), _full_spec(w_in_s.shape)],
        out_specs=[_row_spec(tm, D_MODEL), _acc_spec()],
        out_shape=[jax.ShapeDtypeStruct((s, D_MODEL), F32), jax.ShapeDtypeStruct((ACC_ROWS, D_MODEL), F32)],
        compiler_params=_params(("arbitrary",)),
    )(dproj, x, dx1, mod6, g1, w_in_s)


def _wgrad(name, a, b, out_shape, out_spec, tm, tn, tk=512):
    s, m = a.shape
    n = b.shape[1]
    tk = min(tk, s)
    nk = s // tk

    def body(a_ref, b_ref, o_ref, acc_ref):
        k = pl.program_id(2)

        @pl.when(k == 0)
        def _():
            acc_ref[...] = jnp.zeros_like(acc_ref)

        acc_ref[...] += _tn(a_ref[...], b_ref[...])

        @pl.when(k == nk - 1)
        def _():
            o_ref[...] = acc_ref[...].reshape(o_ref.shape)

    return _call(
        body, name=name, grid=(m // tm, n // tn, nk),
        in_specs=[pl.BlockSpec((tk, tm), lambda i, j, k: (k, i)), pl.BlockSpec((tk, tn), lambda i, j, k: (k, j))],
        out_specs=out_spec, out_shape=jax.ShapeDtypeStruct(out_shape, F32),
        scratch_shapes=[pltpu.VMEM((tm, tn), F32)],
        compiler_params=_params(("parallel", "parallel", "arbitrary")),
    )(a, b)


def _wgrad_branch(ya, yb, dua, dub, tk=512):
    s = ya.shape[0]
    tk = min(tk, s)
    nk = s // tk
    w = D_MODEL // N_SHARD

    def body(ya_ref, yb_ref, dua_ref, dub_ref, o_ref, acc_ref):
        b, k = pl.program_id(0), pl.program_id(2)

        @pl.when(k == 0)
        def _():
            acc_ref[...] = jnp.zeros_like(acc_ref)

        @pl.when(b == 0)
        def _():
            acc_ref[...] += _tn(ya_ref[...], dua_ref[...])

        @pl.when(b == 1)
        def _():
            acc_ref[...] += _tn(yb_ref[...], dub_ref[...])

        @pl.when(k == nk - 1)
        def _():
            o_ref[...] = acc_ref[...]

    y_spec = pl.BlockSpec((tk, BRANCH_W), lambda b, j, k: (k, 0))
    du_spec = pl.BlockSpec((tk, w), lambda b, j, k: (k, j))
    return _call(
        body, name="wgrad_branch", grid=(2, N_SHARD, nk),
        in_specs=[y_spec, y_spec, du_spec, du_spec],
        out_specs=pl.BlockSpec((None, None, BRANCH_W, w), lambda b, j, k: (j, b, 0, 0)),
        out_shape=jax.ShapeDtypeStruct((N_SHARD, 2, BRANCH_W, w), F32),
        scratch_shapes=[pltpu.VMEM((BRANCH_W, w), F32)],
        compiler_params=_params(("parallel", "parallel", "arbitrary")),
    )(ya, yb, dua, dub)


def _local_step(x, target, mod6, g1, g2, gf, gq2, gk2, sink, w_in_s, wbr_s, w_out, w_mi_s, w_mo):
    s = x.shape[0]
    tabs = _rope_tables(s)
    bd = _block_diag()
    tq_b = 2 * WINDOW
    sink_col = jnp.repeat(sink.reshape(KV_HEADS, GROUP, 1), tq_b, axis=1).reshape(KV_HEADS, GROUP * tq_b, 1)

    h, qkraw, qa, ka, va, qb, kb, vb, gates = _in_proj(x, mod6, g1, w_in_s, gq2, gk2, bd, tabs)
    ya, lse_a = _attn_a_fwd(qa, ka, va)
    yb, lse_b = _attn_b_fwd(qb, kb, vb, sink_col)
    ua, ub, merged, o, x1 = _post_attn(ya, yb, gates, x, mod6, wbr_s, w_out)
    h2, a, hid = _mlp_in(x1, mod6, g2, w_mi_s)
    dx2, dm, da, acc_out = _mlp_out_loss(hid, x1, a, target, mod6, gf, w_mo)

    shard = D_MODEL // N_SHARD
    g_w_mo = _wgrad("wgrad_mlp_out", hid, dm, (D_FF, D_MODEL), pl.BlockSpec((512, D_MODEL), lambda i, j, k: (i, 0)),
                    512, D_MODEL).reshape(N_SHARD, D_MODEL, D_MODEL)
    g_w_mi = _wgrad("wgrad_mlp_in", h2, da, (N_SHARD, D_MODEL, D_MODEL),
                    pl.BlockSpec((None, 512, D_MODEL), lambda i, j, k: (j, i, 0)), 512, D_MODEL)
    dx1, do, acc_mlp = _mlp_bwd(da, x1, dx2, o, mod6, g2, w_mi_s)
    dua, dub, dgates, doa, dob, dl_a, dl_b = _merge_bwd(do, gates, ua, ub, ya, yb, w_out, wbr_s)
    g_w_out = _wgrad("wgrad_out", merged, do, (D_MODEL, D_MODEL), pl.BlockSpec((512, D_MODEL), lambda i, j, k: (i, 0)),
                     512, D_MODEL).reshape(N_SHARD, shard, D_MODEL)
    g_wbr = _wgrad_branch(ya, yb, dua, dub)
    col4 = (KV_HEADS, GROUP, s, 1)
    dqa, dka, dva = _attn_a_bwd(qa, ka, va, doa, lse_a, dl_a.reshape(col4))
    dqb, dkb, dvb, dsink = _attn_b_bwd(qb, kb, vb, dob, lse_b, dl_b.reshape(col4), sink_col)
    heads = (Q_HEADS, s, HEAD_DIM)
    dproj, acc_qk = _qk_bwd(dqa.reshape(heads), dka, dva, dqb.reshape(heads), dkb, dvb, qkraw, dgates, gq2, gk2, bd, tabs)
    w = IN_W // N_SHARD
    g_w_in = _wgrad("wgrad_in", h, dproj, (N_SHARD, D_MODEL, w), pl.BlockSpec((None, 512, w), lambda i, j, k: (j, i, 0)),
                    512, w)
    grad_x, acc_in = _in_proj_bwd(dproj, x, dx1, mod6, g1, w_in_s)
    return grad_x, (g_w_in, g_wbr, g_w_out, g_w_mi, g_w_mo), (acc_out, acc_mlp, acc_in, acc_qk, dsink)


def _me():
    return lax.axis_index("x"), lax.axis_index("y"), lax.axis_index("c")


def _peer(d):
    x, y, c = _me()
    return (1 - x if d & 4 else x, 1 - y if d & 2 else y, 1 - c if d & 1 else c)


def _dev_index(p):
    return 4 * p[0] + 2 * p[1] + p[2]


def _chip_index(p):
    return 2 * p[0] + p[1]


def _remote(src, dst, send_sem, recv_sem, to):
    return pltpu.make_async_remote_copy(src_ref=src, dst_ref=dst, send_sem=send_sem, recv_sem=recv_sem,
                                        device_id=to, device_id_type=MESH)


SLOT_ROWS = 8


def _ada_fwd(c, w_ada, b4):
    cols = w_ada.shape[1]

    def body(c_ref, w_ref, b_ref, mod_ref, sc_ref, cbuf, pbuf, mbuf, send1, recv1, send2, recv2):
        me = _me()
        mine, chip = _dev_index(me), _chip_index(me)
        cbuf[mine] = jnp.broadcast_to(c_ref[...], (SLOT_ROWS, D_MODEL))
        gather = [_remote(cbuf.at[mine], cbuf.at[mine], send1.at[d - 1], recv1.at[d - 1], _peer(d)) for d in range(1, N_DEV)]
        for cp in gather:
            cp.start()
        for d in range(1, N_DEV):
            _remote(cbuf.at[mine], cbuf.at[_dev_index(_peer(d))], send1.at[d - 1], recv1.at[d - 1], _peer(d)).wait_recv()
        call = cbuf[...].reshape(N_DEV * SLOT_ROWS, D_MODEL)
        sc = call * _sigmoid(call)
        for s in range(N_DEV):
            sc_ref[s:s + 1, :] = sc[SLOT_ROWS * s:SLOT_ROWS * s + 1]
        part = _nn(sc.astype(BF16), w_ref[...].astype(BF16)) + b_ref[pl.ds(chip, 1), :]
        pbuf[...] = part.reshape(N_DEV, SLOT_ROWS, cols)
        mbuf[chip] = pbuf[mine]
        spread = [_remote(pbuf.at[_dev_index(_peer(d))], mbuf.at[chip], send2.at[d // 2 - 1], recv2.at[d // 2 - 1], _peer(d))
                  for d in (2, 4, 6)]
        for cp in spread:
            cp.start()
        for d in (2, 4, 6):
            _remote(pbuf.at[mine], mbuf.at[_chip_index(_peer(d))], send2.at[d // 2 - 1], recv2.at[d // 2 - 1],
                    _peer(d)).wait_recv()
        half = D_MODEL // 2
        for p in range(2 * 6):
            col = half * p
            mod_ref[p // 2:p // 2 + 1, half * (p % 2):half * (p % 2 + 1)] = mbuf[col // cols, 0:1, col % cols:col % cols + half]
        for cp in gather + spread:
            cp.wait_send()

    vm = pl.BlockSpec(memory_space=pltpu.VMEM)
    return _call(
        body, name="ada_fwd", in_specs=[vm, vm, vm], out_specs=[vm, vm],
        out_shape=[jax.ShapeDtypeStruct((6, D_MODEL), F32), jax.ShapeDtypeStruct((N_DEV, D_MODEL), F32)],
        scratch_shapes=[pltpu.VMEM((N_DEV, SLOT_ROWS, D_MODEL), F32), pltpu.VMEM((N_DEV, SLOT_ROWS, cols), F32),
                        pltpu.VMEM((N_SHARD, SLOT_ROWS, cols), F32),
                        pltpu.SemaphoreType.DMA((N_DEV - 1,)), pltpu.SemaphoreType.DMA((N_DEV - 1,)),
                        pltpu.SemaphoreType.DMA((N_SHARD - 1,)), pltpu.SemaphoreType.DMA((N_SHARD - 1,))],
        compiler_params=_params(),
    )(c, w_ada, b4)


PACK_ROWS = 16
PACK_W = 3 * D_MODEL


def _ada_bwd(acc_out, acc_mlp, acc_in, acc_qk, dsink, sc_all):
    cols = 6 * D_MODEL // N_SHARD

    def body(out_ref, mlp_ref, in_ref, qk_ref, dsink_ref, sc_ref,
             gwa_ref, gba_ref, gn1_ref, gn2_ref, gf_ref, gq_ref, gk_ref, gs_ref, blk, send, recv):
        me = _me()
        mine, chip = _dev_index(me), _chip_index(me)
        blk[mine] = jnp.zeros((PACK_ROWS, PACK_W), F32)
        dmod = (in_ref, 0), (in_ref, 1), (mlp_ref, 3), (mlp_ref, 0), (mlp_ref, 1), (out_ref, 2)
        half = D_MODEL // 2
        for p in range(2 * 6):
            ref, row = dmod[p // 2]
            col = half * p
            blk[mine, col // cols:col // cols + 1, col % cols:col % cols + half] = ref[row:row + 1, half * (p % 2):half * (p % 2 + 1)]
        blk[mine, 4:5, 0:D_MODEL] = in_ref[2:3, :]
        blk[mine, 4:5, D_MODEL:2 * D_MODEL] = mlp_ref[2:3, :]
        blk[mine, 4:5, 2 * D_MODEL:] = out_ref[1:2, :]
        blk[mine, 5:6, 0:LANES] = qk_ref[0:1, :]
        blk[mine, 5:6, LANES:2 * LANES] = qk_ref[1:2, :]
        for g in range(KV_HEADS):
            blk[mine, 8 + GROUP * g:8 + GROUP * (g + 1), 0:LANES] = dsink_ref[g, 0:GROUP, :]
        copies = [_remote(blk.at[mine], blk.at[mine], send.at[d - 1], recv.at[d - 1], _peer(d)) for d in range(1, N_DEV)]
        for cp in copies:
            cp.start()
        for d in range(1, N_DEV):
            _remote(blk.at[mine], blk.at[_dev_index(_peer(d))], send.at[d - 1], recv.at[d - 1], _peer(d)).wait_recv()
        tot = blk[0]
        for s in range(1, N_DEV):
            tot = tot + blk[s]
        for j in range(N_SHARD):
            gba_ref[:, cols * j:cols * (j + 1)] = tot[j:j + 1, :cols]
        gn1_ref[...] = tot[4:5, 0:D_MODEL]
        gn2_ref[...] = tot[4:5, D_MODEL:2 * D_MODEL]
        gf_ref[...] = tot[4:5, 2 * D_MODEL:]
        gq_ref[...] = tot[5:6, 0:HEAD_DIM] + tot[5:6, HEAD_DIM:2 * HEAD_DIM]
        gk_ref[...] = tot[5:6, LANES:LANES + HEAD_DIM] + tot[5:6, LANES + HEAD_DIM:2 * LANES]
        sq = tot[8:16, 0:Q_HEADS]
        diag = lax.broadcasted_iota(jnp.int32, sq.shape, 0) == lax.broadcasted_iota(jnp.int32, sq.shape, 1)
        gs_ref[...] = jnp.sum(jnp.where(diag, sq, 0.0), axis=0, keepdims=True)
        dm = jnp.concatenate([blk[s, pl.ds(chip, 1), pl.ds(0, cols)] for s in range(N_DEV)], axis=0)
        gwa_ref[...] = _tn(sc_ref[...], dm)
        for cp in copies:
            cp.wait_send()

    vm = pl.BlockSpec(memory_space=pltpu.VMEM)
    row = lambda n: jax.ShapeDtypeStruct((1, n), F32)
    return _call(
        body, name="ada_bwd", in_specs=[vm] * 6, out_specs=[vm] * 8,
        out_shape=[jax.ShapeDtypeStruct((D_MODEL, cols), F32), row(6 * D_MODEL), row(D_MODEL), row(D_MODEL), row(D_MODEL),
                   row(HEAD_DIM), row(HEAD_DIM), row(Q_HEADS)],
        scratch_shapes=[pltpu.VMEM((N_DEV, PACK_ROWS, PACK_W), F32),
                        pltpu.SemaphoreType.DMA((N_DEV - 1,)), pltpu.SemaphoreType.DMA((N_DEV - 1,))],
        compiler_params=_params(),
    )(acc_out, acc_mlp, acc_in, acc_qk, dsink, sc_all)


def _cast_weights(ws):
    def body(*refs):
        for src, dst in zip(refs[:len(ws)], refs[len(ws):]):
            dst[...] = src[...].astype(BF16)

    vm = pl.BlockSpec(memory_space=pltpu.VMEM)
    return _call(
        body, name="cast_weights", in_specs=[vm] * len(ws), out_specs=[vm] * len(ws),
        out_shape=[jax.ShapeDtypeStruct(w.shape, BF16) for w in ws], compiler_params=_params(),
    )(*ws)


def _half_rows(ref_rows, c):
    half = ref_rows // 2
    return pl.ds(pl.multiple_of(c * half, 8), half)


def _gather_weights(shards):
    n = len(shards)

    def body(*refs):
        src, out = refs[:n], refs[n:2 * n]
        local, send, recv, fsend, frecv = refs[2 * n:]
        me = _me()
        chip = _chip_index(me)
        sib = _peer(1)
        copies = []
        for a in range(n):
            cp = pltpu.make_async_copy(src[a], out[a].at[chip], local.at[a])
            cp.start()
            copies.append(cp)
        sends = []
        for a in range(n):
            rows = _half_rows(src[a].shape[0], me[2])
            for d in (2, 4, 6):
                k = 3 * a + d // 2 - 1
                cp = _remote(src[a].at[rows], out[a].at[chip, rows], send.at[k], recv.at[k], _peer(d))
                cp.start()
                sends.append(cp)
        for a in range(n):
            rows = _half_rows(src[a].shape[0], me[2])
            for d in (2, 4, 6):
                k = 3 * a + d // 2 - 1
                landed = out[a].at[_chip_index(_peer(d)), rows]
                _remote(src[a].at[rows], landed, send.at[k], recv.at[k], _peer(d)).wait_recv()
                cp = _remote(landed, landed, fsend.at[k], frecv.at[k], sib)
                cp.start()
                sends.append(cp)
        for a in range(n):
            rows = _half_rows(src[a].shape[0], 1 - me[2])
            for d in (2, 4, 6):
                k = 3 * a + d // 2 - 1
                passed = out[a].at[_chip_index(_peer(d)), rows]
                _remote(passed, passed, fsend.at[k], frecv.at[k], sib).wait_recv()
        for cp in sends:
            cp.wait_send()
        for cp in copies:
            cp.wait()

    dma = pltpu.SemaphoreType.DMA
    return _call(
        body, name="gather_weights", in_specs=[ANY] * n, out_specs=[ANY] * n,
        out_shape=[jax.ShapeDtypeStruct((N_SHARD,) + w.shape, w.dtype) for w in shards],
        scratch_shapes=[dma((n,)), dma((3 * n,)), dma((3 * n,)), dma((3 * n,)), dma((3 * n,))],
    )(*shards)


def _swap_halves(grads):
    n = len(grads)

    def body(*refs):
        src, out, send, recv = refs[:n], refs[n:2 * n], refs[2 * n], refs[2 * n + 1]
        me = _me()
        copies = []
        for a in range(n):
            rows = _half_rows(src[a].shape[1], 1 - me[2])
            cp = _remote(src[a].at[pl.ds(0, N_SHARD), rows], out[a], send.at[a], recv.at[a], _peer(1))
            cp.start()
            copies.append(cp)
        for cp in copies:
            cp.wait()

    dma = pltpu.SemaphoreType.DMA
    return _call(
        body, name="swap_halves", in_specs=[ANY] * n, out_specs=[ANY] * n,
        out_shape=[jax.ShapeDtypeStruct((N_SHARD, g.shape[1] // 2, g.shape[2]), F32) for g in grads],
        scratch_shapes=[dma((n,)), dma((n,))],
    )(*grads)


def _row_tile(rows):
    return min(rows, 256)


def _add_halves(name, g, got, cj):
    _, half, cols = got.shape
    tr = _row_tile(half)
    nt = half // tr

    def body(cj_ref, g_ref, got_ref, o_ref):
        o_ref[...] = (g_ref[...] + got_ref[...]).astype(BF16)

    spec = pl.BlockSpec((None, tr, cols), lambda i, s, cj: (s, i, 0))
    return _call(
        body, name=name,
        grid_spec=pltpu.PrefetchScalarGridSpec(
            num_scalar_prefetch=1, grid=(nt, N_SHARD),
            in_specs=[pl.BlockSpec((None, tr, cols), lambda i, s, cj: (s, cj[0] * nt + i, 0)), spec], out_specs=spec),
        out_shape=jax.ShapeDtypeStruct(got.shape, BF16), compiler_params=_params(("parallel", "parallel")),
    )(cj, g, got)


def _scatter_sums(sums):
    n = len(sums)

    def body(*refs):
        src, out, send, recv = refs[:n], refs[n:2 * n], refs[2 * n], refs[2 * n + 1]
        chip = _chip_index(_me())
        copies = []
        for a in range(n):
            for d in (2, 4, 6):
                k = 3 * a + d // 2 - 1
                cp = _remote(src[a].at[_chip_index(_peer(d))], out[a].at[chip], send.at[k], recv.at[k], _peer(d))
                cp.start()
                copies.append(cp)
        for a in range(n):
            for d in (2, 4, 6):
                k = 3 * a + d // 2 - 1
                _remote(src[a].at[chip], out[a].at[_chip_index(_peer(d))], send.at[k], recv.at[k], _peer(d)).wait_recv()
        for cp in copies:
            cp.wait_send()

    dma = pltpu.SemaphoreType.DMA
    return _call(
        body, name="scatter_sums", in_specs=[ANY] * n, out_specs=[ANY] * n,
        out_shape=[jax.ShapeDtypeStruct(v.shape, v.dtype) for v in sums],
        scratch_shapes=[dma((3 * n,)), dma((3 * n,))],
    )(*sums)


def _sum_chips(name, g, got, landed, cj):
    _, half, cols = got.shape
    tr = _row_tile(half)
    nt = half // tr

    def body(cj_ref, g_ref, got_ref, landed_ref, o_ref):
        own = g_ref[...] + got_ref[...]
        total = None
        for s in range(N_SHARD):
            term = jnp.where(cj_ref[1] == s, own, landed_ref[s].astype(F32))
            total = term if total is None else total + term
        o_ref[...] = total

    return _call(
        body, name=name,
        grid_spec=pltpu.PrefetchScalarGridSpec(
            num_scalar_prefetch=1, grid=(nt,),
            in_specs=[pl.BlockSpec((None, tr, cols), lambda i, cj: (cj[1], cj[0] * nt + i, 0)),
                      pl.BlockSpec((None, tr, cols), lambda i, cj: (cj[1], i, 0)),
                      pl.BlockSpec((N_SHARD, tr, cols), lambda i, cj: (0, i, 0))],
            out_specs=pl.BlockSpec((tr, cols), lambda i, cj: (i, 0))),
        out_shape=jax.ShapeDtypeStruct((half, cols), F32), compiler_params=_params(("parallel",)),
    )(cj, g, got, landed)


def _join_halves(halves):
    n = len(halves)

    def body(*refs):
        src, out, local, send, recv = refs[:n], refs[n:2 * n], refs[2 * n], refs[2 * n + 1], refs[2 * n + 2]
        me = _me()
        copies = []
        for a in range(n):
            rows = _half_rows(out[a].shape[0], me[2])
            cp = pltpu.make_async_copy(src[a], out[a].at[rows], local.at[a])
            cp.start()
            copies.append(cp)
            cp = _remote(src[a], out[a].at[rows], send.at[a], recv.at[a], _peer(1))
            cp.start()
            copies.append(cp)
        for a in range(n):
            rows = _half_rows(out[a].shape[0], 1 - me[2])
            _remote(src[a], out[a].at[rows], send.at[a], recv.at[a], _peer(1)).wait_recv()
        for a in range(n):
            copies[2 * a].wait()
            copies[2 * a + 1].wait_send()

    dma = pltpu.SemaphoreType.DMA
    return _call(
        body, name="join_halves", in_specs=[ANY] * n, out_specs=[ANY] * n,
        out_shape=[jax.ShapeDtypeStruct((2 * h.shape[0], h.shape[1]), F32) for h in halves],
        scratch_shapes=[dma((n,)), dma((n,)), dma((n,))],
    )(*halves)


def _reduce_gradients(grads, cj):
    names = ("in", "branch", "out", "mlp_in", "mlp_out")
    got = _swap_halves(grads)
    sums = [_add_halves("add_halves_" + nm, g, h, cj) for nm, g, h in zip(names, grads, got)]
    landed = _scatter_sums(sums)
    halves = [_sum_chips("sum_chips_" + nm, g, h, l, cj) for nm, g, h, l in zip(names, grads, got, landed)]
    return _join_halves(halves)


def _adamw_math(w, g, m, v):
    m = ADAM_B1 * m + (1.0 - ADAM_B1) * g
    v = ADAM_B2 * v + (1.0 - ADAM_B2) * jnp.square(g)
    m_hat = m / (1.0 - ADAM_B1 ** ADAM_STEP)
    v_hat = v / (1.0 - ADAM_B2 ** ADAM_STEP)
    return -ADAM_LR * (m_hat / (jnp.sqrt(v_hat) + ADAM_EPS) + ADAM_WD * w), m, v


def _adamw(name, w, g, m, v):
    rows, cols = w.shape
    tr = _row_tile(rows)

    def body(w_ref, g_ref, m_ref, v_ref, d_ref, nm_ref, nv_ref):
        d_ref[...], nm_ref[...], nv_ref[...] = _adamw_math(w_ref[...], g_ref[...], m_ref[...], v_ref[...])

    spec = pl.BlockSpec((tr, cols), lambda i: (i, 0))
    return _call(
        body, name=name, grid=(rows // tr,), in_specs=[spec] * 4, out_specs=[spec] * 3,
        out_shape=[jax.ShapeDtypeStruct(w.shape, F32)] * 3, compiler_params=_params(("parallel",)),
    )(w, g, m, v)


def _adamw_small(ws, gs, ms, vs):
    n = len(ws)

    def body(*refs):
        ins, outs = refs[:4 * n], refs[4 * n:]
        for a in range(n):
            w, g, m, v = (ins[k * n + a][...] for k in range(4))
            outs[a][...], outs[n + a][...], outs[2 * n + a][...] = _adamw_math(w, g, m, v)

    vm = pl.BlockSpec(memory_space=pltpu.VMEM)
    res = _call(
        body, name="adamw_small", in_specs=[vm] * (4 * n), out_specs=[vm] * (3 * n),
        out_shape=[jax.ShapeDtypeStruct(w.shape, F32) for w in ws] * 3, compiler_params=_params(),
    )(*ws, *gs, *ms, *vs)
    return res[:n], res[n:2 * n], res[2 * n:]


def kernel(x, c, w_ada, b_ada, norm1_g, w_in, q_norm_a, k_norm_a, sink_b, w_branch, w_out, norm2_g, w_mlp_in, w_mlp_out, final_g, loss_target, m_w_ada, m_b_ada, m_norm1_g, m_w_in, m_q_norm_a, m_k_norm_a, m_sink_b, m_w_branch, m_w_out, m_norm2_g, m_w_mlp_in, m_w_mlp_out, m_final_g, v_w_ada, v_b_ada, v_norm1_g, v_w_in, v_q_norm_a, v_k_norm_a, v_sink_b, v_w_branch, v_w_out, v_norm2_g, v_w_mlp_in, v_w_mlp_out, v_final_g):
    xi, yi, ci = _me()
    cj = jnp.stack([ci, 2 * xi + yi]).astype(jnp.int32)
    n_cols = 6 * D_MODEL // N_SHARD

    mod6, sc_all = _ada_fwd(c, w_ada[0], b_ada.reshape(N_SHARD, n_cols))

    def rows2d(a):
        return a.reshape(-1, a.shape[-1])

    big = (w_in, w_branch, w_out, w_mlp_in, w_mlp_out)
    w_in_s, wbr_s, w_out_s, w_mi_s, w_mo_s = _gather_weights(_cast_weights([rows2d(w) for w in big]))
    wbr_s = wbr_s.reshape(N_SHARD, 2, BRANCH_W, D_MODEL // N_SHARD)

    gq2 = jnp.tile(q_norm_a, (1, 2))
    gk2 = jnp.tile(k_norm_a, (1, 2))
    grad_x, grads, (acc_out, acc_mlp, acc_in, acc_qk, dsink) = _local_step(
        x[0], loss_target[0], mod6, norm1_g, norm2_g, final_g.reshape(1, D_MODEL), gq2, gk2, sink_b[0],
        w_in_s, wbr_s, w_out_s.reshape(D_MODEL, D_MODEL), w_mi_s, w_mo_s.reshape(D_FF, D_MODEL))

    loss = lax.psum(0.5 * jnp.sum(acc_out[0]) / D_MODEL, ("x", "y", "c"))

    g_in, g_br, g_out, g_mi, g_mo = grads
    g_big = _reduce_gradients([g_in, g_br.reshape(N_SHARD, 2 * BRANCH_W, D_MODEL // N_SHARD), g_out, g_mi, g_mo], cj)
    g_w_ada, g_b_ada, g_n1, g_n2, g_f, g_q, g_k, g_s = _ada_bwd(acc_out, acc_mlp, acc_in, acc_qk, dsink, sc_all)

    names = ("w_ada", "w_in", "w_branch", "w_out", "w_mlp_in", "w_mlp_out")
    big_w = [w_ada[0]] + [rows2d(w) for w in big]
    big_g = [g_w_ada] + list(g_big)
    big_m = [rows2d(m) for m in (m_w_ada, m_w_in, m_w_branch, m_w_out, m_w_mlp_in, m_w_mlp_out)]
    big_v = [rows2d(v) for v in (v_w_ada, v_w_in, v_w_branch, v_w_out, v_w_mlp_in, v_w_mlp_out)]
    big_res = {nm: _adamw("adamw_" + nm, w, g, m, v) for nm, w, g, m, v in zip(names, big_w, big_g, big_m, big_v)}

    small = ("b_ada", "norm1_g", "q_norm_a", "k_norm_a", "sink_b", "norm2_g", "final_g")
    row = lambda a: a.reshape(1, -1)
    small_w = [row(a) for a in (b_ada, norm1_g, q_norm_a, k_norm_a, sink_b, norm2_g, final_g)]
    small_g = [g_b_ada, g_n1, g_q, g_k, g_s, g_n2, g_f]
    small_m = [row(a) for a in (m_b_ada, m_norm1_g, m_q_norm_a, m_k_norm_a, m_sink_b, m_norm2_g, m_final_g)]
    small_v = [row(a) for a in (v_b_ada, v_norm1_g, v_q_norm_a, v_k_norm_a, v_sink_b, v_norm2_g, v_final_g)]
    s_d, s_m, s_v = _adamw_small(small_w, small_g, small_m, small_v)

    order = ("w_ada", "b_ada", "norm1_g", "w_in", "q_norm_a", "k_norm_a", "sink_b", "w_branch", "w_out", "norm2_g",
             "w_mlp_in", "w_mlp_out", "final_g")
    like = dict(w_ada=w_ada, b_ada=b_ada, norm1_g=norm1_g, w_in=w_in, q_norm_a=q_norm_a, k_norm_a=k_norm_a, sink_b=sink_b,
                w_branch=w_branch, w_out=w_out, norm2_g=norm2_g, w_mlp_in=w_mlp_in, w_mlp_out=w_mlp_out, final_g=final_g)
    grad, delta, new_m, new_v = {}, {}, {}, {}
    for nm, g in zip(names, big_g):
        grad[nm] = g
        delta[nm], new_m[nm], new_v[nm] = big_res[nm]
    for k, nm in enumerate(small):
        grad[nm], delta[nm], new_m[nm], new_v[nm] = small_g[k], s_d[k], s_m[k], s_v[k]
    outs = [loss, grad_x[None]]
    for group in (grad, delta, new_m, new_v):
        outs += [group[nm].reshape(like[nm].shape) for nm in order]
    return tuple(outs)
```

```python
import functools

import jax
import jax.numpy as jnp
from jax import lax
from jax.experimental import pallas as pl
from jax.experimental.pallas import tpu as pltpu

F32 = jnp.float32
BF16 = jnp.bfloat16
MESH = pl.DeviceIdType.MESH
ANY = pl.BlockSpec(memory_space=pl.ANY)

D_MODEL = 1024
HEAD_DIM = 64
Q_HEADS = 8
KV_HEADS = 2
GROUP = Q_HEADS // KV_HEADS
BRANCH_W = Q_HEADS * HEAD_DIM
KV_W = KV_HEADS * HEAD_DIM
IN_W = 2 * (BRANCH_W + 2 * KV_W) + 2 * D_MODEL
QK_W = 2 * (BRANCH_W + 2 * KV_W)
D_FF = 4 * D_MODEL
GRID_W = 64
WINDOW = 128
ROPE_THETA = 10000.0
NORM_EPS = 1e-6
NEG_INF = -1e30
Q_SCALE = HEAD_DIM ** -0.5
N_SHARD = 4
N_DEV = 8
LANES = 128
VMEM_LIMIT = 56 * 1024 * 1024

ADAM_LR = 0.001
ADAM_B1 = 0.9
ADAM_B2 = 0.999
ADAM_EPS = 1e-08
ADAM_WD = 0.01
ADAM_STEP = 10

_call = pl.pallas_call


def _params(sem=None, vmem=VMEM_LIMIT):
    return pltpu.CompilerParams(dimension_semantics=sem, vmem_limit_bytes=vmem)


def _nt(a, b):
    return lax.dot_general(a, b, (((1,), (1,)), ((), ())), preferred_element_type=F32)


def _tn(a, b):
    return lax.dot_general(a, b, (((0,), (0,)), ((), ())), preferred_element_type=F32)


def _nn(a, b):
    return jnp.dot(a, b, preferred_element_type=F32)


def _sigmoid(z):
    return 1.0 / (1.0 + jnp.exp(-z))


def _rope_tables(s):
    t = jnp.arange(s, dtype=jnp.int32)
    lane = jnp.arange(LANES, dtype=jnp.int32)

    def cos_sin(pos, dim):
        inv = ROPE_THETA ** (-jnp.arange(0, dim, 2, dtype=F32) / dim)
        ang = pos.astype(F32)[:, None] * inv[None, :]
        return jnp.cos(ang), jnp.sin(ang)

    cr, sr = cos_sin(t // GRID_W, HEAD_DIM // 2)
    cc, sc = cos_sin(t % GRID_W, HEAD_DIM // 2)
    cos_a = jnp.tile(jnp.concatenate([cr, cr, cc, cc], axis=1), (1, 2))
    sin_a = jnp.tile(jnp.concatenate([sr, sr, sc, sc], axis=1), (1, 2))
    first_a = (lane % 32) < 16
    c1, s1 = cos_sin(t, HEAD_DIM)
    cos_b = jnp.tile(jnp.concatenate([c1, c1], axis=1), (1, 2))
    sin_b = jnp.tile(jnp.concatenate([s1, s1], axis=1), (1, 2))
    first_b = (lane % 64) < 32
    tabs_a = (cos_a, jnp.where(first_a, -sin_a, 0.0), jnp.where(first_a, 0.0, sin_a))
    tabs_b = (cos_b, jnp.where(first_b, -sin_b, 0.0), jnp.where(first_b, 0.0, sin_b))
    return tabs_a + tabs_b


def _rope(z, cos, s_lo, s_hi, half, sign=1.0):
    up = pltpu.roll(z, LANES - half, 1)
    dn = pltpu.roll(z, half, 1)
    return z * cos + sign * (up * s_lo + dn * s_hi)


def _head_mean(z2, bd):
    hi = z2.astype(BF16)
    lo = (z2 - hi.astype(F32)).astype(BF16)
    return _nn(hi, bd) + _nn(lo, bd)


def _block_diag():
    lane = jnp.arange(LANES)
    return jnp.where((lane[:, None] // HEAD_DIM) == (lane[None, :] // HEAD_DIM), 1.0 / HEAD_DIM, 0.0).astype(BF16)


def _row_spec(tm, width):
    return pl.BlockSpec((tm, width), lambda i: (i, 0))


def _heads_spec(heads, tm):
    return pl.BlockSpec((heads, tm, HEAD_DIM), lambda i: (0, i, 0))


def _full_spec(shape):
    nd = len(shape)
    return pl.BlockSpec(shape, lambda i: (0,) * nd)


def _in_proj(x, mod6, g1, w_in_s, gq, gk, bd, tabs, tm=256):
    s = x.shape[0]

    def body(x_ref, mod_ref, g1_ref, w_ref, gq_ref, gk_ref, bd_ref, ca, la, ha, cb, lb, hb,
             h_ref, qkraw_ref, qa_ref, ka_ref, va_ref, qb_ref, kb_ref, vb_ref, gate_ref):
        xt = x_ref[...]
        r = lax.rsqrt(jnp.mean(xt * xt, axis=-1, keepdims=True) + NORM_EPS)
        h = (xt * r * g1_ref[...]) * (1.0 + mod_ref[1:2, :]) + mod_ref[0:1, :]
        hb16 = h.astype(BF16)
        h_ref[...] = hb16
        proj = jnp.concatenate([_nn(hb16, w_ref[j]) for j in range(N_SHARD)], axis=1)
        qkraw_ref[...] = proj[:, :BRANCH_W + KV_W]
        bdm = bd_ref[...]
        tab_a = (ca[...], la[...], ha[...])
        tab_b = (cb[...], lb[...], hb[...])

        def norm_rope_a(z, gain):
            zn = z * lax.rsqrt(_head_mean(z * z, bdm) + NORM_EPS) * gain
            return _rope(zn, *tab_a, 16)

        def put(ref, first, z):
            zb = z.astype(BF16)
            ref[first] = zb[:, :HEAD_DIM]
            ref[first + 1] = zb[:, HEAD_DIM:]

        for i in range(Q_HEADS // 2):
            put(qa_ref, 2 * i, norm_rope_a(proj[:, LANES * i:LANES * (i + 1)], gq_ref[...]) * Q_SCALE)
        off = BRANCH_W
        put(ka_ref, 0, norm_rope_a(proj[:, off:off + LANES], gk_ref[...]))
        off += KV_W
        def put_v(ref, z):
            zb = z.astype(BF16)
            for hd in range(KV_HEADS):
                ref[hd, :, :HEAD_DIM] = zb[:, HEAD_DIM * hd:HEAD_DIM * (hd + 1)]
                ref[hd, :, HEAD_DIM:] = jnp.ones((tm, HEAD_DIM), BF16)

        put_v(va_ref, proj[:, off:off + LANES])
        off += KV_W
        for i in range(Q_HEADS // 2):
            put(qb_ref, 2 * i, _rope(proj[:, off + LANES * i:off + LANES * (i + 1)], *tab_b, 32) * Q_SCALE)
        off += BRANCH_W
        put(kb_ref, 0, _rope(proj[:, off:off + LANES], *tab_b, 32))
        off += KV_W
        put_v(vb_ref, proj[:, off:off + LANES])
        gate_ref[...] = proj[:, QK_W:]

    tab_spec = _row_spec(tm, LANES)
    return _call(
        body, name="in_proj", grid=(s // tm,),
        in_specs=[_row_spec(tm, D_MODEL), _full_spec(mod6.shape), _full_spec(g1.shape), _full_spec(w_in_s.shape),
                  _full_spec(gq.shape), _full_spec(gk.shape), _full_spec(bd.shape)] + [tab_spec] * 6,
        out_specs=[_row_spec(tm, D_MODEL), _row_spec(tm, BRANCH_W + KV_W), _heads_spec(Q_HEADS, tm), _heads_spec(KV_HEADS, tm),
                   pl.BlockSpec((KV_HEADS, tm, LANES), lambda i: (0, i, 0)), _heads_spec(Q_HEADS, tm),
                   _heads_spec(KV_HEADS, tm), pl.BlockSpec((KV_HEADS, tm, LANES), lambda i: (0, i, 0)),
                   _row_spec(tm, 2 * D_MODEL)],
        out_shape=[jax.ShapeDtypeStruct((s, D_MODEL), BF16), jax.ShapeDtypeStruct((s, BRANCH_W + KV_W), F32),
                   jax.ShapeDtypeStruct((Q_HEADS, s, HEAD_DIM), BF16), jax.ShapeDtypeStruct((KV_HEADS, s, HEAD_DIM), BF16),
                   jax.ShapeDtypeStruct((KV_HEADS, s, LANES), BF16), jax.ShapeDtypeStruct((Q_HEADS, s, HEAD_DIM), BF16),
                   jax.ShapeDtypeStruct((KV_HEADS, s, HEAD_DIM), BF16), jax.ShapeDtypeStruct((KV_HEADS, s, LANES), BF16),
                   jax.ShapeDtypeStruct((s, 2 * D_MODEL), F32)],
        compiler_params=_params(("parallel",)),
    )(x, mod6, g1, w_in_s, gq, gk, bd, *tabs)


def _group_specs(s, tq):
    q_spec = pl.BlockSpec((None, GROUP, tq, HEAD_DIM), lambda g, i: (g, 0, i, 0))
    kv_spec = pl.BlockSpec((None, s, HEAD_DIM), lambda g, i: (g, 0, 0))
    col_spec = pl.BlockSpec((None, GROUP, tq, 1), lambda g, i: (g, 0, i, 0))
    return q_spec, kv_spec, col_spec


def _attn_a_fwd(q, k, v1, tq=256, tk=512):
    s = q.shape[1]
    tk = min(tk, s)
    rows = GROUP * tq

    n = s // tk
    assert n >= 2 and n % 2 == 0

    def body(q_ref, k_ref, v_ref, o_ref, lse_ref, s0_ref, s1_ref, p0_ref, p1_ref, m_ref, a_ref, acc_ref):
        s_ref, p_ref = (s0_ref, s1_ref), (p0_ref, p1_ref)
        qq = q_ref[...].reshape(rows, HEAD_DIM)
        m_ref[...] = jnp.full((rows, 1), NEG_INF, F32)
        acc_ref[...] = jnp.zeros((rows, LANES), F32)

        def keys(i):
            return pl.ds(pl.multiple_of(i * tk, tk), tk)

        def scores(i, slot):
            s_ref[slot][...] = _nt(qq, k_ref[keys(i), :])

        def softmax(slot):
            sc = s_ref[slot][...]
            m = m_ref[...]
            mn = jnp.maximum(m, jnp.max(sc, axis=-1, keepdims=True))
            m_ref[...] = mn
            a_ref[...] = jnp.exp(m - mn)
            p_ref[slot][...] = jnp.exp(sc - mn).astype(BF16)

        def weigh(i, slot):
            acc_ref[...] = a_ref[...] * acc_ref[...] + _nn(p_ref[slot][...], v_ref[keys(i), :])

        scores(0, 0)
        softmax(0)
        scores(1, 1)

        def two_steps(j, carry):
            i = 2 * j + 1
            weigh(i - 1, 0)
            softmax(1)
            scores(i + 1, 0)
            weigh(i, 1)
            softmax(0)
            scores(i + 2, 1)
            return carry

        lax.fori_loop(0, (n - 2) // 2, two_steps, 0)
        weigh(n - 2, 0)
        softmax(1)
        weigh(n - 1, 1)
        l = acc_ref[:, HEAD_DIM:HEAD_DIM + 1]
        o = (acc_ref[:, :HEAD_DIM] / l).astype(BF16)
        for g in range(GROUP):
            o_ref[:, HEAD_DIM * g:HEAD_DIM * (g + 1)] = o[tq * g:tq * (g + 1)]
        lse_ref[...] = (m_ref[...] + jnp.log(l)).reshape(GROUP, tq, 1)

    q_spec, kv_spec, col_spec = _group_specs(s, tq)
    v_spec = pl.BlockSpec((None, s, LANES), lambda g, i: (g, 0, 0))
    return _call(
        body, name="attn_a_fwd", grid=(KV_HEADS, s // tq),
        in_specs=[q_spec, kv_spec, v_spec],
        out_specs=[pl.BlockSpec((tq, GROUP * HEAD_DIM), lambda g, i: (i, g)), col_spec],
        out_shape=[jax.ShapeDtypeStruct((s, BRANCH_W), BF16), jax.ShapeDtypeStruct((KV_HEADS, GROUP, s, 1), F32)],
        scratch_shapes=[pltpu.VMEM((rows, tk), F32), pltpu.VMEM((rows, tk), F32), pltpu.VMEM((rows, tk), BF16),
                        pltpu.VMEM((rows, tk), BF16), pltpu.VMEM((rows, 1), F32), pltpu.VMEM((rows, 1), F32),
                        pltpu.VMEM((rows, LANES), F32)],
        compiler_params=_params(("parallel", "parallel")),
    )(q.reshape(KV_HEADS, GROUP, s, HEAD_DIM), k, v1)


def _attn_a_bwd(q, k, v, do, lse, delta, tq=256, tk=512):
    s = q.shape[1]
    tk = min(tk, s)
    rows = GROUP * tq

    n = s // tk
    assert n >= 2 and n % 2 == 0

    def body(q_ref, k_ref, v_ref, do_ref, lse_ref, dl_ref, dq_ref, dk_ref, dv_ref,
             s0_ref, s1_ref, dp0_ref, dp1_ref, p0_ref, p1_ref, ds0_ref, ds1_ref, dq_acc):
        s_ref, dp_ref, p_ref, ds_ref = (s0_ref, s1_ref), (dp0_ref, dp1_ref), (p0_ref, p1_ref), (ds0_ref, ds1_ref)

        @pl.when(pl.program_id(1) == 0)
        def _():
            dk_ref[...] = jnp.zeros_like(dk_ref)
            dv_ref[...] = jnp.zeros_like(dv_ref)

        qq = q_ref[...].reshape(rows, HEAD_DIM)
        dd = do_ref[...].reshape(rows, HEAD_DIM)
        ls = lse_ref[...].reshape(rows, 1)
        dl = dl_ref[...].reshape(rows, 1)
        dq_acc[...] = jnp.zeros((rows, HEAD_DIM), F32)

        def keys(i):
            return pl.ds(pl.multiple_of(i * tk, tk), tk)

        def scores(i, slot):
            s_ref[slot][...] = _nt(qq, k_ref[keys(i), :])
            dp_ref[slot][...] = _nt(dd, v_ref[keys(i), :HEAD_DIM])

        def weights(slot):
            p = jnp.exp(s_ref[slot][...] - ls)
            p_ref[slot][...] = p.astype(BF16)
            ds_ref[slot][...] = (p * (dp_ref[slot][...] - dl)).astype(BF16)

        def grads(i, slot):
            dv_ref[keys(i), :] += _tn(p_ref[slot][...], dd)
            dk_ref[keys(i), :] += _tn(ds_ref[slot][...], qq)
            dq_acc[...] += _nn(ds_ref[slot][...], k_ref[keys(i), :])

        scores(0, 0)
        weights(0)
        scores(1, 1)

        def two_steps(j, carry):
            i = 2 * j + 1
            grads(i - 1, 0)
            weights(1)
            scores(i + 1, 0)
            grads(i, 1)
            weights(0)
            scores(i + 2, 1)
            return carry

        lax.fori_loop(0, (n - 2) // 2, two_steps, 0)
        grads(n - 2, 0)
        weights(1)
        grads(n - 1, 1)
        dq_ref[...] = dq_acc[...].reshape(GROUP, tq, HEAD_DIM)

    q_spec, kv_spec, col_spec = _group_specs(s, tq)
    v_spec = pl.BlockSpec((None, s, LANES), lambda g, i: (g, 0, 0))
    shape4 = (KV_HEADS, GROUP, s, HEAD_DIM)
    tile32, tile16 = pltpu.VMEM((rows, tk), F32), pltpu.VMEM((rows, tk), BF16)
    return _call(
        body, name="attn_a_bwd", grid=(KV_HEADS, s // tq),
        in_specs=[q_spec, kv_spec, v_spec, q_spec, col_spec, col_spec],
        out_specs=[q_spec, kv_spec, kv_spec],
        out_shape=[jax.ShapeDtypeStruct(shape4, F32), jax.ShapeDtypeStruct((KV_HEADS, s, HEAD_DIM), F32),
                   jax.ShapeDtypeStruct((KV_HEADS, s, HEAD_DIM), F32)],
        scratch_shapes=[tile32] * 4 + [tile16] * 4 + [pltpu.VMEM((rows, HEAD_DIM), F32)],
        compiler_params=_params(("parallel", "arbitrary")),
    )(q.reshape(shape4), k, v, do.reshape(shape4), lse, delta)


WIN_KEYS = 4 * WINDOW


def _window_start(i, tq, s):
    return pl.multiple_of(jnp.clip(i * tq - WINDOW, 0, s - WIN_KEYS), WINDOW)


def _window_bias(tq):
    r = jnp.arange(tq, dtype=jnp.int32)[:, None]
    col = jnp.arange(WIN_KEYS, dtype=jnp.int32)[None, :]
    return jnp.stack([jnp.where(jnp.abs(r - col + WINDOW * b) <= WINDOW, 0.0, NEG_INF) for b in range(3)]).astype(F32)


def _bias_spec(tq, nq):
    return pl.BlockSpec((None, tq, WIN_KEYS), lambda g, i: (jnp.where(i == 0, 0, jnp.where(i == nq - 1, 2, 1)), 0, 0))


def _masked(sc, bias_ref, tq):
    return (sc.reshape(GROUP, tq, WIN_KEYS) + bias_ref[...][None]).reshape(GROUP * tq, WIN_KEYS)


def _attn_b_fwd(q, k, v1, sink_col, bias, tq=2 * WINDOW):
    s = q.shape[1]
    rows = GROUP * tq

    def body(q_ref, k_ref, v_ref, sink_ref, bias_ref, o_ref, lse_ref):
        at = pl.ds(_window_start(pl.program_id(1), tq, s), WIN_KEYS)
        qq = q_ref[...].reshape(rows, HEAD_DIM)
        sc = _masked(_nt(qq, k_ref[at, :]), bias_ref, tq)
        sk = sink_ref[...]
        m = jnp.maximum(jnp.max(sc, axis=-1, keepdims=True), sk)
        acc = _nn(jnp.exp(sc - m).astype(BF16), v_ref[at, :])
        l = acc[:, HEAD_DIM:HEAD_DIM + 1] + jnp.exp(sk - m)
        o = (acc[:, :HEAD_DIM] / l).astype(BF16)
        for g in range(GROUP):
            o_ref[:, HEAD_DIM * g:HEAD_DIM * (g + 1)] = o[tq * g:tq * (g + 1)]
        lse_ref[...] = (m + jnp.log(l)).reshape(GROUP, tq, 1)

    q_spec, kv_spec, col_spec = _group_specs(s, tq)
    v_spec = pl.BlockSpec((None, s, LANES), lambda g, i: (g, 0, 0))
    sink_spec = pl.BlockSpec((None, rows, 1), lambda g, i: (g, 0, 0))
    return _call(
        body, name="attn_b_fwd", grid=(KV_HEADS, s // tq),
        in_specs=[q_spec, kv_spec, v_spec, sink_spec, _bias_spec(tq, s // tq)],
        out_specs=[pl.BlockSpec((tq, GROUP * HEAD_DIM), lambda g, i: (i, g)), col_spec],
        out_shape=[jax.ShapeDtypeStruct((s, BRANCH_W), BF16), jax.ShapeDtypeStruct((KV_HEADS, GROUP, s, 1), F32)],
        compiler_params=_params(("parallel", "parallel")),
    )(q.reshape(KV_HEADS, GROUP, s, HEAD_DIM), k, v1, sink_col, bias)


def _attn_b_bwd(q, k, v1, do, lse, delta, sink_col, bias, tq=2 * WINDOW):
    s = q.shape[1]
    rows = GROUP * tq

    def body(q_ref, k_ref, v_ref, do_ref, lse_ref, dl_ref, sink_ref, bias_ref, dq_ref, dk_ref, dv_ref, dsink_ref):
        i = pl.program_id(1)

        @pl.when(i == 0)
        def _():
            dk_ref[...] = jnp.zeros_like(dk_ref)
            dv_ref[...] = jnp.zeros_like(dv_ref)
            dsink_ref[...] = jnp.zeros_like(dsink_ref)

        start = _window_start(i, tq, s)
        at = pl.ds(start, WIN_KEYS)
        qq = q_ref[...].reshape(rows, HEAD_DIM)
        dd = do_ref[...].reshape(rows, HEAD_DIM)
        ls = lse_ref[...].reshape(rows, 1)
        dl = dl_ref[...].reshape(rows, 1)
        kk = k_ref[at, :]
        p = jnp.exp(_masked(_nt(qq, kk), bias_ref, tq) - ls)
        ds = (p * (_nt(dd, v_ref[at, :HEAD_DIM]) - dl)).astype(BF16)
        dv_ref[at, :] += _tn(p.astype(BF16), dd)
        dk_ref[at, :] += _tn(ds, qq)
        dq_ref[...] = _nn(ds, kk).reshape(GROUP, tq, HEAD_DIM)
        dsk = jnp.exp(sink_ref[...] - ls) * dl
        for g in range(GROUP):
            dsink_ref[g:g + 1, :] -= jnp.broadcast_to(jnp.sum(dsk[tq * g:tq * (g + 1)], axis=0, keepdims=True), (1, LANES))

    q_spec, kv_spec, col_spec = _group_specs(s, tq)
    sink_spec = pl.BlockSpec((None, rows, 1), lambda g, i: (g, 0, 0))
    dsink_spec = pl.BlockSpec((None, ACC_ROWS, LANES), lambda g, i: (g, 0, 0))
    shape4 = (KV_HEADS, GROUP, s, HEAD_DIM)
    v_spec = pl.BlockSpec((None, s, LANES), lambda g, i: (g, 0, 0))
    return _call(
        body, name="attn_b_bwd", grid=(KV_HEADS, s // tq),
        in_specs=[q_spec, kv_spec, v_spec, q_spec, col_spec, col_spec, sink_spec, _bias_spec(tq, s // tq)],
        out_specs=[q_spec, kv_spec, kv_spec, dsink_spec],
        out_shape=[jax.ShapeDtypeStruct(shape4, F32), jax.ShapeDtypeStruct((KV_HEADS, s, HEAD_DIM), F32),
                   jax.ShapeDtypeStruct((KV_HEADS, s, HEAD_DIM), F32), jax.ShapeDtypeStruct((KV_HEADS, ACC_ROWS, LANES), F32)],
        compiler_params=_params(("parallel", "arbitrary")),
    )(q.reshape(shape4), k, v1, do.reshape(shape4), lse, delta, sink_col, bias)


def _post_attn(ya, yb, gates, x, mod6, wbr_s, w_out, tm=256):
    s = x.shape[0]

    def body(ya_ref, yb_ref, g_ref, x_ref, mod_ref, wbr_ref, wo_ref, ua_ref, ub_ref, mg_ref, o_ref, x1_ref):
        ya_t, yb_t = ya_ref[...], yb_ref[...]
        ua = jnp.concatenate([_nn(ya_t, wbr_ref[j, 0]) for j in range(N_SHARD)], axis=1)
        ub = jnp.concatenate([_nn(yb_t, wbr_ref[j, 1]) for j in range(N_SHARD)], axis=1)
        merged = (_sigmoid(g_ref[:, :D_MODEL]) * ua + _sigmoid(g_ref[:, D_MODEL:]) * ub).astype(BF16)
        o = _nn(merged, wo_ref[...])
        ua_ref[...] = ua.astype(BF16)
        ub_ref[...] = ub.astype(BF16)
        mg_ref[...] = merged
        o_ref[...] = o.astype(BF16)
        x1_ref[...] = x_ref[...] + mod_ref[2:3, :] * o

    bf = jax.ShapeDtypeStruct((s, D_MODEL), BF16)
    return _call(
        body, name="post_attn", grid=(s // tm,),
        in_specs=[_row_spec(tm, BRANCH_W), _row_spec(tm, BRANCH_W), _row_spec(tm, 2 * D_MODEL), _row_spec(tm, D_MODEL),
                  _full_spec(mod6.shape), _full_spec(wbr_s.shape), _full_spec(w_out.shape)],
        out_specs=[_row_spec(tm, D_MODEL)] * 5,
        out_shape=[bf, bf, bf, bf, jax.ShapeDtypeStruct((s, D_MODEL), F32)],
        compiler_params=_params(("parallel",)),
    )(ya, yb, gates, x, mod6, wbr_s, w_out)


def _mlp_in(x1, mod6, g2, w_mi_s, tm=256):
    s = x1.shape[0]

    def body(x_ref, mod_ref, g_ref, w_ref, h2_ref, a_ref, hid_ref):
        xt = x_ref[...]
        r = lax.rsqrt(jnp.mean(xt * xt, axis=-1, keepdims=True) + NORM_EPS)
        h2 = ((xt * r * g_ref[...]) * (1.0 + mod_ref[4:5, :]) + mod_ref[3:4, :]).astype(BF16)
        h2_ref[...] = h2
        a = jnp.concatenate([_nn(h2, w_ref[j]) for j in range(N_SHARD)], axis=1)
        a_ref[...] = a.astype(BF16)
        hid_ref[...] = jnp.square(jnp.maximum(a, 0.0)).astype(BF16)

    return _call(
        body, name="mlp_in", grid=(s // tm,),
        in_specs=[_row_spec(tm, D_MODEL), _full_spec(mod6.shape), _full_spec(g2.shape), _full_spec(w_mi_s.shape)],
        out_specs=[_row_spec(tm, D_MODEL), _row_spec(tm, D_FF), _row_spec(tm, D_FF)],
        out_shape=[jax.ShapeDtypeStruct((s, D_MODEL), BF16), jax.ShapeDtypeStruct((s, D_FF), BF16),
                   jax.ShapeDtypeStruct((s, D_FF), BF16)],
        compiler_params=_params(("parallel",)),
    )(x1, mod6, g2, w_mi_s)


ACC_ROWS = 8


def _acc_spec():
    return pl.BlockSpec((ACC_ROWS, D_MODEL), lambda i: (0, 0))


def _acc_add(acc_ref, rows):
    @pl.when(pl.program_id(0) == 0)
    def _():
        acc_ref[...] = jnp.zeros_like(acc_ref)

    for r, val in enumerate(rows):
        acc_ref[r:r + 1, :] += jnp.sum(val, axis=0, keepdims=True)


def _mlp_out_loss(hid, x1, a, target, mod6, gf, w_mo, tm=256):
    s = x1.shape[0]

    def body(hid_ref, x_ref, a_ref, t_ref, mod_ref, gf_ref, w_ref, dx2_ref, dm_ref, da_ref, acc_ref):
        m = _nn(hid_ref[...], w_ref[...])
        gate2 = mod_ref[5:6, :]
        x2 = x_ref[...] + gate2 * m
        r = lax.rsqrt(jnp.mean(x2 * x2, axis=-1, keepdims=True) + NORM_EPS)
        xn = x2 * r
        err = xn * gf_ref[...] - t_ref[...]
        dy = err * (1.0 / D_MODEL)
        dxn = dy * gf_ref[...]
        dx2 = r * (dxn - xn * jnp.mean(dxn * xn, axis=-1, keepdims=True))
        dx2_ref[...] = dx2
        dm = (dx2 * gate2).astype(BF16)
        dm_ref[...] = dm
        da_ref[...] = (_nt(dm, w_ref[...]) * (2.0 * jnp.maximum(a_ref[...].astype(F32), 0.0))).astype(BF16)
        _acc_add(acc_ref, [err * err, dy * xn, dx2 * m])

    return _call(
        body, name="mlp_out_loss", grid=(s // tm,),
        in_specs=[_row_spec(tm, D_FF), _row_spec(tm, D_MODEL), _row_spec(tm, D_FF), _row_spec(tm, D_MODEL),
                  _full_spec(mod6.shape), _full_spec(gf.shape), _full_spec(w_mo.shape)],
        out_specs=[_row_spec(tm, D_MODEL), _row_spec(tm, D_MODEL), _row_spec(tm, D_FF), _acc_spec()],
        out_shape=[jax.ShapeDtypeStruct((s, D_MODEL), F32), jax.ShapeDtypeStruct((s, D_MODEL), BF16),
                   jax.ShapeDtypeStruct((s, D_FF), BF16), jax.ShapeDtypeStruct((ACC_ROWS, D_MODEL), F32)],
        compiler_params=_params(("arbitrary",)),
    )(hid, x1, a, target, mod6, gf, w_mo)


def _norm_bwd(dh, xt, gain, scale):
    r = lax.rsqrt(jnp.mean(xt * xt, axis=-1, keepdims=True) + NORM_EPS)
    xn = xt * r
    dxn = dh * (gain * (1.0 + scale))
    dx = r * (dxn - xn * jnp.mean(dxn * xn, axis=-1, keepdims=True))
    return dx, [dh, dh * xn * gain, dh * xn * (1.0 + scale)]


def _mlp_bwd(da, x1, dx2, o, mod6, g2, w_mi_s, tm=256):
    s = x1.shape[0]

    def body(da_ref, x_ref, dx2_ref, o_ref, mod_ref, g_ref, w_ref, dx1_ref, do_ref, acc_ref):
        dh2 = _nt(da_ref[:, :D_MODEL], w_ref[0])
        for j in range(1, N_SHARD):
            dh2 += _nt(da_ref[:, D_MODEL * j:D_MODEL * (j + 1)], w_ref[j])
        dx, sums = _norm_bwd(dh2, x_ref[...], g_ref[...], mod_ref[4:5, :])
        dx1 = dx2_ref[...] + dx
        dx1_ref[...] = dx1
        do_ref[...] = (dx1 * mod_ref[2:3, :]).astype(BF16)
        _acc_add(acc_ref, sums + [dx1 * o_ref[...].astype(F32)])

    return _call(
        body, name="mlp_bwd", grid=(s // tm,),
        in_specs=[_row_spec(tm, D_FF), _row_spec(tm, D_MODEL), _row_spec(tm, D_MODEL), _row_spec(tm, D_MODEL),
                  _full_spec(mod6.shape), _full_spec(g2.shape), _full_spec(w_mi_s.shape)],
        out_specs=[_row_spec(tm, D_MODEL), _row_spec(tm, D_MODEL), _acc_spec()],
        out_shape=[jax.ShapeDtypeStruct((s, D_MODEL), F32), jax.ShapeDtypeStruct((s, D_MODEL), BF16),
                   jax.ShapeDtypeStruct((ACC_ROWS, D_MODEL), F32)],
        compiler_params=_params(("arbitrary",)),
    )(da, x1, dx2, o, mod6, g2, w_mi_s)


def _merge_bwd(do, gates, ua, ub, ya, yb, w_out, wbr_s, tm=256):
    s = do.shape[0]

    def body(do_ref, g_ref, ua_ref, ub_ref, ya_ref, yb_ref, wo_ref, wbr_ref,
             dua_ref, dub_ref, dg_ref, doa_ref, dob_ref, dla_ref, dlb_ref):
        dmerged = _nt(do_ref[...], wo_ref[...])
        for b, (u_ref, y_ref, du_ref, dy_ref, dl_ref) in enumerate(
                ((ua_ref, ya_ref, dua_ref, doa_ref, dla_ref), (ub_ref, yb_ref, dub_ref, dob_ref, dlb_ref))):
            sg = _sigmoid(g_ref[:, D_MODEL * b:D_MODEL * (b + 1)])
            du = (dmerged * sg).astype(BF16)
            du_ref[...] = du
            dg_ref[:, D_MODEL * b:D_MODEL * (b + 1)] = (dmerged * u_ref[...].astype(F32) * sg * (1.0 - sg)).astype(BF16)
            w = BRANCH_W // 2
            dy = _nt(du[:, :w], wbr_ref[0, b])
            for j in range(1, N_SHARD):
                dy += _nt(du[:, w * j:w * (j + 1)], wbr_ref[j, b])
            prod = dy * y_ref[...].astype(F32)
            dyb = dy.astype(BF16)
            for h in range(Q_HEADS):
                at = slice(HEAD_DIM * h, HEAD_DIM * (h + 1))
                dy_ref[h] = dyb[:, at]
                dl_ref[h] = jnp.sum(prod[:, at], axis=-1, keepdims=True)

    bf = jax.ShapeDtypeStruct((s, D_MODEL), BF16)
    heads = jax.ShapeDtypeStruct((Q_HEADS, s, HEAD_DIM), BF16)
    cols = jax.ShapeDtypeStruct((Q_HEADS, s, 1), F32)
    col_spec = pl.BlockSpec((Q_HEADS, tm, 1), lambda i: (0, i, 0))
    return _call(
        body, name="merge_bwd", grid=(s // tm,),
        in_specs=[_row_spec(tm, D_MODEL), _row_spec(tm, 2 * D_MODEL), _row_spec(tm, D_MODEL), _row_spec(tm, D_MODEL),
                  _row_spec(tm, BRANCH_W), _row_spec(tm, BRANCH_W), _full_spec(w_out.shape), _full_spec(wbr_s.shape)],
        out_specs=[_row_spec(tm, D_MODEL), _row_spec(tm, D_MODEL), _row_spec(tm, 2 * D_MODEL),
                   _heads_spec(Q_HEADS, tm), _heads_spec(Q_HEADS, tm), col_spec, col_spec],
        out_shape=[bf, bf, jax.ShapeDtypeStruct((s, 2 * D_MODEL), BF16), heads, heads, cols, cols],
        compiler_params=_params(("parallel",)),
    )(do, gates, ua, ub, ya, yb, w_out, wbr_s)


def _qk_bwd(dqa, dka, dva, dqb, dkb, dvb, qkraw, dgates, gq, gk, bd, tabs, tm=256):
    s = qkraw.shape[0]

    def body(dqa_ref, dka_ref, dva_ref, dqb_ref, dkb_ref, dvb_ref, raw_ref, dg_ref, gq_ref, gk_ref, bd_ref,
             ca, la, ha, cb, lb, hb, dp_ref, acc_ref, pair_ref):
        bdm = bd_ref[...]
        tab_a = (ca[...], la[...], ha[...])
        tab_b = (cb[...], lb[...], hb[...])

        def pair(ref, first):
            pair_ref[:, :HEAD_DIM] = ref[first]
            pair_ref[:, HEAD_DIM:] = ref[first + 1]
            return pair_ref[...]

        def norm_rope_a_bwd(dz, raw, gain):
            dzn = _rope(dz, *tab_a, 16, sign=-1.0)
            rinv = lax.rsqrt(_head_mean(raw * raw, bdm) + NORM_EPS)
            zhat = raw * rinv
            dzhat = dzn * gain
            return rinv * (dzhat - zhat * _head_mean(dzhat * zhat, bdm)), dzn * zhat

        gq_rows = jnp.zeros((tm, LANES), F32)
        for i in range(Q_HEADS // 2):
            at = slice(LANES * i, LANES * (i + 1))
            draw, gsum = norm_rope_a_bwd(pair(dqa_ref, 2 * i) * Q_SCALE, raw_ref[:, at], gq_ref[...])
            dp_ref[:, at] = draw.astype(BF16)
            gq_rows += gsum
        off = BRANCH_W
        draw, gk_rows = norm_rope_a_bwd(pair(dka_ref, 0), raw_ref[:, off:off + LANES], gk_ref[...])
        dp_ref[:, off:off + LANES] = draw.astype(BF16)
        off += KV_W
        dp_ref[:, off:off + LANES] = pair(dva_ref, 0).astype(BF16)
        off += KV_W
        for i in range(Q_HEADS // 2):
            dz = _rope(pair(dqb_ref, 2 * i) * Q_SCALE, *tab_b, 32, sign=-1.0)
            dp_ref[:, off + LANES * i:off + LANES * (i + 1)] = dz.astype(BF16)
        off += BRANCH_W
        dp_ref[:, off:off + LANES] = _rope(pair(dkb_ref, 0), *tab_b, 32, sign=-1.0).astype(BF16)
        off += KV_W
        dp_ref[:, off:off + LANES] = pair(dvb_ref, 0).astype(BF16)
        dp_ref[:, QK_W:] = dg_ref[...]

        @pl.when(pl.program_id(0) == 0)
        def _():
            acc_ref[...] = jnp.zeros_like(acc_ref)

        acc_ref[0:1, :] += jnp.sum(gq_rows, axis=0, keepdims=True)
        acc_ref[1:2, :] += jnp.sum(gk_rows, axis=0, keepdims=True)

    tab_spec = _row_spec(tm, LANES)
    return _call(
        body, name="qk_bwd", grid=(s // tm,),
        in_specs=[_heads_spec(Q_HEADS, tm), _heads_spec(KV_HEADS, tm), _heads_spec(KV_HEADS, tm),
                  _heads_spec(Q_HEADS, tm), _heads_spec(KV_HEADS, tm), _heads_spec(KV_HEADS, tm),
                  _row_spec(tm, BRANCH_W + KV_W), _row_spec(tm, 2 * D_MODEL),
                  _full_spec(gq.shape), _full_spec(gk.shape), _full_spec(bd.shape)] + [tab_spec] * 6,
        out_specs=[_row_spec(tm, IN_W), pl.BlockSpec((ACC_ROWS, LANES), lambda i: (0, 0))],
        out_shape=[jax.ShapeDtypeStruct((s, IN_W), BF16), jax.ShapeDtypeStruct((ACC_ROWS, LANES), F32)],
        scratch_shapes=[pltpu.VMEM((tm, LANES), F32)],
        compiler_params=_params(("arbitrary",)),
    )(dqa, dka, dva, dqb, dkb, dvb, qkraw, dgates, gq, gk, bd, *tabs)


def _in_proj_bwd(dproj, x, dx1, mod6, g1, w_in_s, tm=256):
    s = x.shape[0]
    w = IN_W // N_SHARD

    def body(dp_ref, x_ref, dx1_ref, mod_ref, g_ref, w_ref, gx_ref, acc_ref):
        dh = _nt(dp_ref[:, :w], w_ref[0])
        for j in range(1, N_SHARD):
            dh += _nt(dp_ref[:, w * j:w * (j + 1)], w_ref[j])
        dx, sums = _norm_bwd(dh, x_ref[...], g_ref[...], mod_ref[1:2, :])
        gx_ref[...] = dx1_ref[...] + dx
        _acc_add(acc_ref, sums)

    return _call(
        body, name="in_proj_bwd", grid=(s // tm,),
        in_specs=[_row_spec(tm, IN_W), _row_spec(tm, D_MODEL), _row_spec(tm, D_MODEL),
                  _full_spec(mod6.shape), _full_spec(g1.shape), _full_spec(w_in_s.shape)],
        out_specs=[_row_spec(tm, D_MODEL), _acc_spec()],
        out_shape=[jax.ShapeDtypeStruct((s, D_MODEL), F32), jax.ShapeDtypeStruct((ACC_ROWS, D_MODEL), F32)],
        compiler_params=_params(("arbitrary",)),
    )(dproj, x, dx1, mod6, g1, w_in_s)


def _wgrad(name, a, b, out_shape, out_spec, tm, tn, tk=2048):
    s, m = a.shape
    n = b.shape[1]
    tk = min(tk, s)
    nk = s // tk

    def body(a_ref, b_ref, o_ref, acc_ref):
        k = pl.program_id(2)

        @pl.when(k == 0)
        def _():
            acc_ref[...] = jnp.zeros_like(acc_ref)

        acc_ref[...] += _tn(a_ref[...], b_ref[...])

        @pl.when(k == nk - 1)
        def _():
            o_ref[...] = acc_ref[...].reshape(o_ref.shape)

    return _call(
        body, name=name, grid=(m // tm, n // tn, nk),
        in_specs=[pl.BlockSpec((tk, tm), lambda i, j, k: (k, i)), pl.BlockSpec((tk, tn), lambda i, j, k: (k, j))],
        out_specs=out_spec, out_shape=jax.ShapeDtypeStruct(out_shape, F32),
        scratch_shapes=[pltpu.VMEM((tm, tn), F32)],
        compiler_params=_params(("parallel", "parallel", "arbitrary")),
    )(a, b)


def _wgrad_branch(ya, yb, dua, dub, tk=2048):
    s = ya.shape[0]
    tk = min(tk, s)
    nk = s // tk
    w = D_MODEL // N_SHARD

    def body(ya_ref, yb_ref, dua_ref, dub_ref, o_ref, acc_ref):
        b, k = pl.program_id(0), pl.program_id(2)

        @pl.when(k == 0)
        def _():
            acc_ref[...] = jnp.zeros_like(acc_ref)

        @pl.when(b == 0)
        def _():
            acc_ref[...] += _tn(ya_ref[...], dua_ref[...])

        @pl.when(b == 1)
        def _():
            acc_ref[...] += _tn(yb_ref[...], dub_ref[...])

        @pl.when(k == nk - 1)
        def _():
            o_ref[...] = acc_ref[...]

    y_spec = pl.BlockSpec((tk, BRANCH_W), lambda b, j, k: (k, 0))
    du_spec = pl.BlockSpec((tk, w), lambda b, j, k: (k, j))
    return _call(
        body, name="wgrad_branch", grid=(2, N_SHARD, nk),
        in_specs=[y_spec, y_spec, du_spec, du_spec],
        out_specs=pl.BlockSpec((None, None, BRANCH_W, w), lambda b, j, k: (j, b, 0, 0)),
        out_shape=jax.ShapeDtypeStruct((N_SHARD, 2, BRANCH_W, w), F32),
        scratch_shapes=[pltpu.VMEM((BRANCH_W, w), F32)],
        compiler_params=_params(("parallel", "parallel", "arbitrary")),
    )(ya, yb, dua, dub)


def _local_step(x, target, mod6, g1, g2, gf, gq2, gk2, sink, w_in_s, wbr_s, w_out, w_mi_s, w_mo):
    s = x.shape[0]
    tabs = _rope_tables(s)
    bd = _block_diag()
    tq_b = 2 * WINDOW
    sink_col = jnp.repeat(sink.reshape(KV_HEADS, GROUP, 1), tq_b, axis=1).reshape(KV_HEADS, GROUP * tq_b, 1)

    h, qkraw, qa, ka, va, qb, kb, vb, gates = _in_proj(x, mod6, g1, w_in_s, gq2, gk2, bd, tabs)
    ya, lse_a = _attn_a_fwd(qa, ka, va)
    bias = _window_bias(tq_b)
    yb, lse_b = _attn_b_fwd(qb, kb, vb, sink_col, bias)
    ua, ub, merged, o, x1 = _post_attn(ya, yb, gates, x, mod6, wbr_s, w_out)
    h2, a, hid = _mlp_in(x1, mod6, g2, w_mi_s)
    dx2, dm, da, acc_out = _mlp_out_loss(hid, x1, a, target, mod6, gf, w_mo)

    shard = D_MODEL // N_SHARD
    g_w_mo = _wgrad("wgrad_mlp_out", hid, dm, (D_FF, D_MODEL), pl.BlockSpec((512, D_MODEL), lambda i, j, k: (i, 0)),
                    512, D_MODEL).reshape(N_SHARD, D_MODEL, D_MODEL)
    g_w_mi = _wgrad("wgrad_mlp_in", h2, da, (N_SHARD, D_MODEL, D_MODEL),
                    pl.BlockSpec((None, 512, D_MODEL), lambda i, j, k: (j, i, 0)), 512, D_MODEL)
    dx1, do, acc_mlp = _mlp_bwd(da, x1, dx2, o, mod6, g2, w_mi_s)
    dua, dub, dgates, doa, dob, dl_a, dl_b = _merge_bwd(do, gates, ua, ub, ya, yb, w_out, wbr_s)
    g_w_out = _wgrad("wgrad_out", merged, do, (D_MODEL, D_MODEL), pl.BlockSpec((512, D_MODEL), lambda i, j, k: (i, 0)),
                     512, D_MODEL).reshape(N_SHARD, shard, D_MODEL)
    g_wbr = _wgrad_branch(ya, yb, dua, dub)
    col4 = (KV_HEADS, GROUP, s, 1)
    dqa, dka, dva = _attn_a_bwd(qa, ka, va, doa, lse_a, dl_a.reshape(col4))
    dqb, dkb, dvb, dsink = _attn_b_bwd(qb, kb, vb, dob, lse_b, dl_b.reshape(col4), sink_col, bias)
    heads = (Q_HEADS, s, HEAD_DIM)
    dproj, acc_qk = _qk_bwd(dqa.reshape(heads), dka, dva, dqb.reshape(heads), dkb, dvb, qkraw, dgates, gq2, gk2, bd, tabs)
    w = IN_W // N_SHARD
    g_w_in = _wgrad("wgrad_in", h, dproj, (N_SHARD, D_MODEL, w), pl.BlockSpec((None, 512, w), lambda i, j, k: (j, i, 0)),
                    512, w)
    grad_x, acc_in = _in_proj_bwd(dproj, x, dx1, mod6, g1, w_in_s)
    return grad_x, (g_w_in, g_wbr, g_w_out, g_w_mi, g_w_mo), (acc_out, acc_mlp, acc_in, acc_qk, dsink)


def _me():
    return lax.axis_index("x"), lax.axis_index("y"), lax.axis_index("c")


def _peer(d):
    x, y, c = _me()
    return (1 - x if d & 4 else x, 1 - y if d & 2 else y, 1 - c if d & 1 else c)


def _dev_index(p):
    return 4 * p[0] + 2 * p[1] + p[2]


def _chip_index(p):
    return 2 * p[0] + p[1]


def _remote(src, dst, send_sem, recv_sem, to):
    return pltpu.make_async_remote_copy(src_ref=src, dst_ref=dst, send_sem=send_sem, recv_sem=recv_sem,
                                        device_id=to, device_id_type=MESH)


SLOT_ROWS = 8


def _ada_fwd(c, w_ada, b4):
    cols = w_ada.shape[1]

    def body(c_ref, w_ref, b_ref, mod_ref, sc_ref, cbuf, pbuf, mbuf, send1, recv1, send2, recv2):
        me = _me()
        mine, chip = _dev_index(me), _chip_index(me)
        cbuf[mine] = jnp.broadcast_to(c_ref[...], (SLOT_ROWS, D_MODEL))
        gather = [_remote(cbuf.at[mine], cbuf.at[mine], send1.at[d - 1], recv1.at[d - 1], _peer(d)) for d in range(1, N_DEV)]
        for cp in gather:
            cp.start()
        for d in range(1, N_DEV):
            _remote(cbuf.at[mine], cbuf.at[_dev_index(_peer(d))], send1.at[d - 1], recv1.at[d - 1], _peer(d)).wait_recv()
        call = cbuf[...].reshape(N_DEV * SLOT_ROWS, D_MODEL)
        sc = call * _sigmoid(call)
        for s in range(N_DEV):
            sc_ref[s:s + 1, :] = sc[SLOT_ROWS * s:SLOT_ROWS * s + 1]
        part = _nn(sc.astype(BF16), w_ref[...].astype(BF16)) + b_ref[pl.ds(chip, 1), :]
        pbuf[...] = part.reshape(N_DEV, SLOT_ROWS, cols)
        mbuf[chip] = pbuf[mine]
        spread = [_remote(pbuf.at[_dev_index(_peer(d))], mbuf.at[chip], send2.at[d // 2 - 1], recv2.at[d // 2 - 1], _peer(d))
                  for d in (2, 4, 6)]
        for cp in spread:
            cp.start()
        for d in (2, 4, 6):
            _remote(pbuf.at[mine], mbuf.at[_chip_index(_peer(d))], send2.at[d // 2 - 1], recv2.at[d // 2 - 1],
                    _peer(d)).wait_recv()
        half = D_MODEL // 2
        for p in range(2 * 6):
            col = half * p
            mod_ref[p // 2:p // 2 + 1, half * (p % 2):half * (p % 2 + 1)] = mbuf[col // cols, 0:1, col % cols:col % cols + half]
        for cp in gather + spread:
            cp.wait_send()

    vm = pl.BlockSpec(memory_space=pltpu.VMEM)
    return _call(
        body, name="ada_fwd", in_specs=[vm, vm, vm], out_specs=[vm, vm],
        out_shape=[jax.ShapeDtypeStruct((6, D_MODEL), F32), jax.ShapeDtypeStruct((N_DEV, D_MODEL), F32)],
        scratch_shapes=[pltpu.VMEM((N_DEV, SLOT_ROWS, D_MODEL), F32), pltpu.VMEM((N_DEV, SLOT_ROWS, cols), F32),
                        pltpu.VMEM((N_SHARD, SLOT_ROWS, cols), F32),
                        pltpu.SemaphoreType.DMA((N_DEV - 1,)), pltpu.SemaphoreType.DMA((N_DEV - 1,)),
                        pltpu.SemaphoreType.DMA((N_SHARD - 1,)), pltpu.SemaphoreType.DMA((N_SHARD - 1,))],
        compiler_params=_params(),
    )(c, w_ada, b4)


PACK_ROWS = 16
PACK_W = 3 * D_MODEL


def _ada_bwd(acc_out, acc_mlp, acc_in, acc_qk, dsink, sc_all):
    cols = 6 * D_MODEL // N_SHARD

    def body(out_ref, mlp_ref, in_ref, qk_ref, dsink_ref, sc_ref,
             gwa_ref, gba_ref, gn1_ref, gn2_ref, gf_ref, gq_ref, gk_ref, gs_ref, blk, send, recv):
        me = _me()
        mine, chip = _dev_index(me), _chip_index(me)
        blk[mine] = jnp.zeros((PACK_ROWS, PACK_W), F32)
        dmod = (in_ref, 0), (in_ref, 1), (mlp_ref, 3), (mlp_ref, 0), (mlp_ref, 1), (out_ref, 2)
        half = D_MODEL // 2
        for p in range(2 * 6):
            ref, row = dmod[p // 2]
            col = half * p
            blk[mine, col // cols:col // cols + 1, col % cols:col % cols + half] = ref[row:row + 1, half * (p % 2):half * (p % 2 + 1)]
        blk[mine, 4:5, 0:D_MODEL] = in_ref[2:3, :]
        blk[mine, 4:5, D_MODEL:2 * D_MODEL] = mlp_ref[2:3, :]
        blk[mine, 4:5, 2 * D_MODEL:] = out_ref[1:2, :]
        blk[mine, 5:6, 0:LANES] = qk_ref[0:1, :]
        blk[mine, 5:6, LANES:2 * LANES] = qk_ref[1:2, :]
        for g in range(KV_HEADS):
            blk[mine, 8 + GROUP * g:8 + GROUP * (g + 1), 0:LANES] = dsink_ref[g, 0:GROUP, :]
        copies = [_remote(blk.at[mine], blk.at[mine], send.at[d - 1], recv.at[d - 1], _peer(d)) for d in range(1, N_DEV)]
        for cp in copies:
            cp.start()
        for d in range(1, N_DEV):
            _remote(blk.at[mine], blk.at[_dev_index(_peer(d))], send.at[d - 1], recv.at[d - 1], _peer(d)).wait_recv()
        tot = blk[0]
        for s in range(1, N_DEV):
            tot = tot + blk[s]
        for j in range(N_SHARD):
            gba_ref[:, cols * j:cols * (j + 1)] = tot[j:j + 1, :cols]
        gn1_ref[...] = tot[4:5, 0:D_MODEL]
        gn2_ref[...] = tot[4:5, D_MODEL:2 * D_MODEL]
        gf_ref[...] = tot[4:5, 2 * D_MODEL:]
        gq_ref[...] = tot[5:6, 0:HEAD_DIM] + tot[5:6, HEAD_DIM:2 * HEAD_DIM]
        gk_ref[...] = tot[5:6, LANES:LANES + HEAD_DIM] + tot[5:6, LANES + HEAD_DIM:2 * LANES]
        sq = tot[8:16, 0:Q_HEADS]
        diag = lax.broadcasted_iota(jnp.int32, sq.shape, 0) == lax.broadcasted_iota(jnp.int32, sq.shape, 1)
        gs_ref[...] = jnp.sum(jnp.where(diag, sq, 0.0), axis=0, keepdims=True)
        dm = jnp.concatenate([blk[s, pl.ds(chip, 1), pl.ds(0, cols)] for s in range(N_DEV)], axis=0)
        gwa_ref[...] = _tn(sc_ref[...], dm)
        for cp in copies:
            cp.wait_send()

    vm = pl.BlockSpec(memory_space=pltpu.VMEM)
    row = lambda n: jax.ShapeDtypeStruct((1, n), F32)
    return _call(
        body, name="ada_bwd", in_specs=[vm] * 6, out_specs=[vm] * 8,
        out_shape=[jax.ShapeDtypeStruct((D_MODEL, cols), F32), row(6 * D_MODEL), row(D_MODEL), row(D_MODEL), row(D_MODEL),
                   row(HEAD_DIM), row(HEAD_DIM), row(Q_HEADS)],
        scratch_shapes=[pltpu.VMEM((N_DEV, PACK_ROWS, PACK_W), F32),
                        pltpu.SemaphoreType.DMA((N_DEV - 1,)), pltpu.SemaphoreType.DMA((N_DEV - 1,))],
        compiler_params=_params(),
    )(acc_out, acc_mlp, acc_in, acc_qk, dsink, sc_all)


def _cast_weights(ws):
    def body(*refs):
        for src, dst in zip(refs[:len(ws)], refs[len(ws):]):
            dst[...] = src[...].astype(BF16)

    vm = pl.BlockSpec(memory_space=pltpu.VMEM)
    return _call(
        body, name="cast_weights", in_specs=[vm] * len(ws), out_specs=[vm] * len(ws),
        out_shape=[jax.ShapeDtypeStruct(w.shape, BF16) for w in ws], compiler_params=_params(),
    )(*ws)


def _half_rows(ref_rows, c):
    half = ref_rows // 2
    return pl.ds(pl.multiple_of(c * half, 8), half)


def _gather_weights(shards):
    n = len(shards)

    def body(*refs):
        src, out = refs[:n], refs[n:2 * n]
        local, send, recv, fsend, frecv = refs[2 * n:]
        me = _me()
        chip = _chip_index(me)
        sib = _peer(1)
        copies = []
        for a in range(n):
            cp = pltpu.make_async_copy(src[a], out[a].at[chip], local.at[a])
            cp.start()
            copies.append(cp)
        sends = []
        for a in range(n):
            rows = _half_rows(src[a].shape[0], me[2])
            for d in (2, 4, 6):
                k = 3 * a + d // 2 - 1
                cp = _remote(src[a].at[rows], out[a].at[chip, rows], send.at[k], recv.at[k], _peer(d))
                cp.start()
                sends.append(cp)
        for a in range(n):
            rows = _half_rows(src[a].shape[0], me[2])
            for d in (2, 4, 6):
                k = 3 * a + d // 2 - 1
                landed = out[a].at[_chip_index(_peer(d)), rows]
                _remote(src[a].at[rows], landed, send.at[k], recv.at[k], _peer(d)).wait_recv()
                cp = _remote(landed, landed, fsend.at[k], frecv.at[k], sib)
                cp.start()
                sends.append(cp)
        for a in range(n):
            rows = _half_rows(src[a].shape[0], 1 - me[2])
            for d in (2, 4, 6):
                k = 3 * a + d // 2 - 1
                passed = out[a].at[_chip_index(_peer(d)), rows]
                _remote(passed, passed, fsend.at[k], frecv.at[k], sib).wait_recv()
        for cp in sends:
            cp.wait_send()
        for cp in copies:
            cp.wait()

    dma = pltpu.SemaphoreType.DMA
    return _call(
        body, name="gather_weights", in_specs=[ANY] * n, out_specs=[ANY] * n,
        out_shape=[jax.ShapeDtypeStruct((N_SHARD,) + w.shape, w.dtype) for w in shards],
        scratch_shapes=[dma((n,)), dma((3 * n,)), dma((3 * n,)), dma((3 * n,)), dma((3 * n,))],
    )(*shards)


def _swap_halves(grads):
    n = len(grads)

    def body(*refs):
        src, out, send, recv = refs[:n], refs[n:2 * n], refs[2 * n], refs[2 * n + 1]
        me = _me()
        copies = []
        for a in range(n):
            rows = _half_rows(src[a].shape[1], 1 - me[2])
            cp = _remote(src[a].at[pl.ds(0, N_SHARD), rows], out[a], send.at[a], recv.at[a], _peer(1))
            cp.start()
            copies.append(cp)
        for cp in copies:
            cp.wait()

    dma = pltpu.SemaphoreType.DMA
    return _call(
        body, name="swap_halves", in_specs=[ANY] * n, out_specs=[ANY] * n,
        out_shape=[jax.ShapeDtypeStruct((N_SHARD, g.shape[1] // 2, g.shape[2]), F32) for g in grads],
        scratch_shapes=[dma((n,)), dma((n,))],
    )(*grads)


def _row_tile(rows):
    return min(rows, 256)


def _add_halves(name, g, got, cj):
    _, half, cols = got.shape
    tr = _row_tile(half)
    nt = half // tr

    def body(cj_ref, g_ref, got_ref, o_ref):
        o_ref[...] = (g_ref[...] + got_ref[...]).astype(BF16)

    spec = pl.BlockSpec((None, tr, cols), lambda i, s, cj: (s, i, 0))
    return _call(
        body, name=name,
        grid_spec=pltpu.PrefetchScalarGridSpec(
            num_scalar_prefetch=1, grid=(nt, N_SHARD),
            in_specs=[pl.BlockSpec((None, tr, cols), lambda i, s, cj: (s, cj[0] * nt + i, 0)), spec], out_specs=spec),
        out_shape=jax.ShapeDtypeStruct(got.shape, BF16), compiler_params=_params(("parallel", "parallel")),
    )(cj, g, got)


def _scatter_sums(sums):
    n = len(sums)

    def body(*refs):
        src, out, send, recv = refs[:n], refs[n:2 * n], refs[2 * n], refs[2 * n + 1]
        chip = _chip_index(_me())
        copies = []
        for a in range(n):
            for d in (2, 4, 6):
                k = 3 * a + d // 2 - 1
                cp = _remote(src[a].at[_chip_index(_peer(d))], out[a].at[chip], send.at[k], recv.at[k], _peer(d))
                cp.start()
                copies.append(cp)
        for a in range(n):
            for d in (2, 4, 6):
                k = 3 * a + d // 2 - 1
                _remote(src[a].at[chip], out[a].at[_chip_index(_peer(d))], send.at[k], recv.at[k], _peer(d)).wait_recv()
        for cp in copies:
            cp.wait_send()

    dma = pltpu.SemaphoreType.DMA
    return _call(
        body, name="scatter_sums", in_specs=[ANY] * n, out_specs=[ANY] * n,
        out_shape=[jax.ShapeDtypeStruct(v.shape, v.dtype) for v in sums],
        scratch_shapes=[dma((3 * n,)), dma((3 * n,))],
    )(*sums)


def _sum_chips(name, g, got, landed, cj):
    _, half, cols = got.shape
    tr = _row_tile(half)
    nt = half // tr

    def body(cj_ref, g_ref, got_ref, landed_ref, o_ref):
        own = g_ref[...] + got_ref[...]
        total = None
        for s in range(N_SHARD):
            term = jnp.where(cj_ref[1] == s, own, landed_ref[s].astype(F32))
            total = term if total is None else total + term
        o_ref[...] = total

    return _call(
        body, name=name,
        grid_spec=pltpu.PrefetchScalarGridSpec(
            num_scalar_prefetch=1, grid=(nt,),
            in_specs=[pl.BlockSpec((None, tr, cols), lambda i, cj: (cj[1], cj[0] * nt + i, 0)),
                      pl.BlockSpec((None, tr, cols), lambda i, cj: (cj[1], i, 0)),
                      pl.BlockSpec((N_SHARD, tr, cols), lambda i, cj: (0, i, 0))],
            out_specs=pl.BlockSpec((tr, cols), lambda i, cj: (i, 0))),
        out_shape=jax.ShapeDtypeStruct((half, cols), F32), compiler_params=_params(("parallel",)),
    )(cj, g, got, landed)


def _join_halves(halves):
    n = len(halves)

    def body(*refs):
        src, out, local, send, recv = refs[:n], refs[n:2 * n], refs[2 * n], refs[2 * n + 1], refs[2 * n + 2]
        me = _me()
        copies = []
        for a in range(n):
            rows = _half_rows(out[a].shape[0], me[2])
            cp = pltpu.make_async_copy(src[a], out[a].at[rows], local.at[a])
            cp.start()
            copies.append(cp)
            cp = _remote(src[a], out[a].at[rows], send.at[a], recv.at[a], _peer(1))
            cp.start()
            copies.append(cp)
        for a in range(n):
            rows = _half_rows(out[a].shape[0], 1 - me[2])
            _remote(src[a], out[a].at[rows], send.at[a], recv.at[a], _peer(1)).wait_recv()
        for a in range(n):
            copies[2 * a].wait()
            copies[2 * a + 1].wait_send()

    dma = pltpu.SemaphoreType.DMA
    return _call(
        body, name="join_halves", in_specs=[ANY] * n, out_specs=[ANY] * n,
        out_shape=[jax.ShapeDtypeStruct((2 * h.shape[0], h.shape[1]), F32) for h in halves],
        scratch_shapes=[dma((n,)), dma((n,)), dma((n,))],
    )(*halves)


def _reduce_gradients(grads, cj):
    names = ("in", "branch", "out", "mlp_in", "mlp_out")
    got = _swap_halves(grads)
    sums = [_add_halves("add_halves_" + nm, g, h, cj) for nm, g, h in zip(names, grads, got)]
    landed = _scatter_sums(sums)
    halves = [_sum_chips("sum_chips_" + nm, g, h, l, cj) for nm, g, h, l in zip(names, grads, got, landed)]
    return _join_halves(halves)


def _adamw_math(w, g, m, v):
    m = ADAM_B1 * m + (1.0 - ADAM_B1) * g
    v = ADAM_B2 * v + (1.0 - ADAM_B2) * jnp.square(g)
    m_hat = m / (1.0 - ADAM_B1 ** ADAM_STEP)
    v_hat = v / (1.0 - ADAM_B2 ** ADAM_STEP)
    return -ADAM_LR * (m_hat / (jnp.sqrt(v_hat) + ADAM_EPS) + ADAM_WD * w), m, v


def _adamw(name, w, g, m, v):
    rows, cols = w.shape
    tr = _row_tile(rows)

    def body(w_ref, g_ref, m_ref, v_ref, d_ref, nm_ref, nv_ref):
        d_ref[...], nm_ref[...], nv_ref[...] = _adamw_math(w_ref[...], g_ref[...], m_ref[...], v_ref[...])

    spec = pl.BlockSpec((tr, cols), lambda i: (i, 0))
    return _call(
        body, name=name, grid=(rows // tr,), in_specs=[spec] * 4, out_specs=[spec] * 3,
        out_shape=[jax.ShapeDtypeStruct(w.shape, F32)] * 3, compiler_params=_params(("parallel",)),
    )(w, g, m, v)


def _adamw_small(ws, gs, ms, vs):
    n = len(ws)

    def body(*refs):
        ins, outs = refs[:4 * n], refs[4 * n:]
        for a in range(n):
            w, g, m, v = (ins[k * n + a][...] for k in range(4))
            outs[a][...], outs[n + a][...], outs[2 * n + a][...] = _adamw_math(w, g, m, v)

    vm = pl.BlockSpec(memory_space=pltpu.VMEM)
    res = _call(
        body, name="adamw_small", in_specs=[vm] * (4 * n), out_specs=[vm] * (3 * n),
        out_shape=[jax.ShapeDtypeStruct(w.shape, F32) for w in ws] * 3, compiler_params=_params(),
    )(*ws, *gs, *ms, *vs)
    return res[:n], res[n:2 * n], res[2 * n:]


def kernel(x, c, w_ada, b_ada, norm1_g, w_in, q_norm_a, k_norm_a, sink_b, w_branch, w_out, norm2_g, w_mlp_in, w_mlp_out, final_g, loss_target, m_w_ada, m_b_ada, m_norm1_g, m_w_in, m_q_norm_a, m_k_norm_a, m_sink_b, m_w_branch, m_w_out, m_norm2_g, m_w_mlp_in, m_w_mlp_out, m_final_g, v_w_ada, v_b_ada, v_norm1_g, v_w_in, v_q_norm_a, v_k_norm_a, v_sink_b, v_w_branch, v_w_out, v_norm2_g, v_w_mlp_in, v_w_mlp_out, v_final_g):
    xi, yi, ci = _me()
    cj = jnp.stack([ci, 2 * xi + yi]).astype(jnp.int32)
    n_cols = 6 * D_MODEL // N_SHARD

    mod6, sc_all = _ada_fwd(c, w_ada[0], b_ada.reshape(N_SHARD, n_cols))

    def rows2d(a):
        return a.reshape(-1, a.shape[-1])

    big = (w_in, w_branch, w_out, w_mlp_in, w_mlp_out)
    w_in_s, wbr_s, w_out_s, w_mi_s, w_mo_s = _gather_weights(_cast_weights([rows2d(w) for w in big]))
    wbr_s = wbr_s.reshape(N_SHARD, 2, BRANCH_W, D_MODEL // N_SHARD)

    gq2 = jnp.tile(q_norm_a, (1, 2))
    gk2 = jnp.tile(k_norm_a, (1, 2))
    grad_x, grads, (acc_out, acc_mlp, acc_in, acc_qk, dsink) = _local_step(
        x[0], loss_target[0], mod6, norm1_g, norm2_g, final_g.reshape(1, D_MODEL), gq2, gk2, sink_b[0],
        w_in_s, wbr_s, w_out_s.reshape(D_MODEL, D_MODEL), w_mi_s, w_mo_s.reshape(D_FF, D_MODEL))

    loss = lax.psum(0.5 * jnp.sum(acc_out[0]) / D_MODEL, ("x", "y", "c"))

    g_in, g_br, g_out, g_mi, g_mo = grads
    g_big = _reduce_gradients([g_in, g_br.reshape(N_SHARD, 2 * BRANCH_W, D_MODEL // N_SHARD), g_out, g_mi, g_mo], cj)
    g_w_ada, g_b_ada, g_n1, g_n2, g_f, g_q, g_k, g_s = _ada_bwd(acc_out, acc_mlp, acc_in, acc_qk, dsink, sc_all)

    names = ("w_ada", "w_in", "w_branch", "w_out", "w_mlp_in", "w_mlp_out")
    big_w = [w_ada[0]] + [rows2d(w) for w in big]
    big_g = [g_w_ada] + list(g_big)
    big_m = [rows2d(m) for m in (m_w_ada, m_w_in, m_w_branch, m_w_out, m_w_mlp_in, m_w_mlp_out)]
    big_v = [rows2d(v) for v in (v_w_ada, v_w_in, v_w_branch, v_w_out, v_w_mlp_in, v_w_mlp_out)]
    big_res = {nm: _adamw("adamw_" + nm, w, g, m, v) for nm, w, g, m, v in zip(names, big_w, big_g, big_m, big_v)}

    small = ("b_ada", "norm1_g", "q_norm_a", "k_norm_a", "sink_b", "norm2_g", "final_g")
    row = lambda a: a.reshape(1, -1)
    small_w = [row(a) for a in (b_ada, norm1_g, q_norm_a, k_norm_a, sink_b, norm2_g, final_g)]
    small_g = [g_b_ada, g_n1, g_q, g_k, g_s, g_n2, g_f]
    small_m = [row(a) for a in (m_b_ada, m_norm1_g, m_q_norm_a, m_k_norm_a, m_sink_b, m_norm2_g, m_final_g)]
    small_v = [row(a) for a in (v_b_ada, v_norm1_g, v_q_norm_a, v_k_norm_a, v_sink_b, v_norm2_g, v_final_g)]
    s_d, s_m, s_v = _adamw_small(small_w, small_g, small_m, small_v)

    order = ("w_ada", "b_ada", "norm1_g", "w_in", "q_norm_a", "k_norm_a", "sink_b", "w_branch", "w_out", "norm2_g",
             "w_mlp_in", "w_mlp_out", "final_g")
    like = dict(w_ada=w_ada, b_ada=b_ada, norm1_g=norm1_g, w_in=w_in, q_norm_a=q_norm_a, k_norm_a=k_norm_a, sink_b=sink_b,
                w_branch=w_branch, w_out=w_out, norm2_g=norm2_g, w_mlp_in=w_mlp_in, w_mlp_out=w_mlp_out, final_g=final_g)
    grad, delta, new_m, new_v = {}, {}, {}, {}
    for nm, g in zip(names, big_g):
        grad[nm] = g
        delta[nm], new_m[nm], new_v[nm] = big_res[nm]
    for k, nm in enumerate(small):
        grad[nm], delta[nm], new_m[nm], new_v[nm] = small_g[k], s_d[k], s_m[k], s_v[k]
    outs = [loss, grad_x[None]]
    for group in (grad, delta, new_m, new_v):
        outs += [group[nm].reshape(like[nm].shape) for nm in order]
    return tuple(outs)
```

```python
import functools

import jax
import jax.numpy as jnp
from jax import lax
from jax.experimental import pallas as pl
from jax.experimental.pallas import tpu as pltpu

F32 = jnp.float32
BF16 = jnp.bfloat16
MESH = pl.DeviceIdType.MESH
ANY = pl.BlockSpec(memory_space=pl.ANY)

D_MODEL = 1024
HEAD_DIM = 64
Q_HEADS = 8
KV_HEADS = 2
GROUP = Q_HEADS // KV_HEADS
BRANCH_W = Q_HEADS * HEAD_DIM
KV_W = KV_HEADS * HEAD_DIM
IN_W = 2 * (BRANCH_W + 2 * KV_W) + 2 * D_MODEL
QK_W = 2 * (BRANCH_W + 2 * KV_W)
D_FF = 4 * D_MODEL
GRID_W = 64
WINDOW = 128
ROPE_THETA = 10000.0
NORM_EPS = 1e-6
NEG_INF = -1e30
Q_SCALE = HEAD_DIM ** -0.5
N_SHARD = 4
N_DEV = 8
LANES = 128
VMEM_LIMIT = 56 * 1024 * 1024

ADAM_LR = 0.001
ADAM_B1 = 0.9
ADAM_B2 = 0.999
ADAM_EPS = 1e-08
ADAM_WD = 0.01
ADAM_STEP = 10

_call = pl.pallas_call


def _params(sem=None, vmem=VMEM_LIMIT):
    return pltpu.CompilerParams(dimension_semantics=sem, vmem_limit_bytes=vmem)


def _nt(a, b):
    return lax.dot_general(a, b, (((1,), (1,)), ((), ())), preferred_element_type=F32)


def _tn(a, b):
    return lax.dot_general(a, b, (((0,), (0,)), ((), ())), preferred_element_type=F32)


def _nn(a, b):
    return jnp.dot(a, b, preferred_element_type=F32)


def _sigmoid(z):
    return 1.0 / (1.0 + jnp.exp(-z))


def _rope_tables(s):
    t = jnp.arange(s, dtype=jnp.int32)
    lane = jnp.arange(LANES, dtype=jnp.int32)

    def cos_sin(pos, dim):
        inv = ROPE_THETA ** (-jnp.arange(0, dim, 2, dtype=F32) / dim)
        ang = pos.astype(F32)[:, None] * inv[None, :]
        return jnp.cos(ang), jnp.sin(ang)

    cr, sr = cos_sin(t // GRID_W, HEAD_DIM // 2)
    cc, sc = cos_sin(t % GRID_W, HEAD_DIM // 2)
    cos_a = jnp.tile(jnp.concatenate([cr, cr, cc, cc], axis=1), (1, 2))
    sin_a = jnp.tile(jnp.concatenate([sr, sr, sc, sc], axis=1), (1, 2))
    first_a = (lane % 32) < 16
    c1, s1 = cos_sin(t, HEAD_DIM)
    cos_b = jnp.tile(jnp.concatenate([c1, c1], axis=1), (1, 2))
    sin_b = jnp.tile(jnp.concatenate([s1, s1], axis=1), (1, 2))
    first_b = (lane % 64) < 32
    tabs_a = (cos_a, jnp.where(first_a, -sin_a, 0.0), jnp.where(first_a, 0.0, sin_a))
    tabs_b = (cos_b, jnp.where(first_b, -sin_b, 0.0), jnp.where(first_b, 0.0, sin_b))
    return tabs_a + tabs_b


def _rope(z, cos, s_lo, s_hi, half, sign=1.0):
    up = pltpu.roll(z, LANES - half, 1)
    dn = pltpu.roll(z, half, 1)
    return z * cos + sign * (up * s_lo + dn * s_hi)


def _head_mean(z2, bd):
    hi = z2.astype(BF16)
    lo = (z2 - hi.astype(F32)).astype(BF16)
    return _nn(hi, bd) + _nn(lo, bd)


def _block_diag():
    lane = jnp.arange(LANES)
    return jnp.where((lane[:, None] // HEAD_DIM) == (lane[None, :] // HEAD_DIM), 1.0 / HEAD_DIM, 0.0).astype(BF16)


def _row_spec(tm, width):
    return pl.BlockSpec((tm, width), lambda i: (i, 0))


def _heads_spec(heads, tm):
    return pl.BlockSpec((heads, tm, HEAD_DIM), lambda i: (0, i, 0))


def _full_spec(shape):
    nd = len(shape)
    return pl.BlockSpec(shape, lambda i: (0,) * nd)


def _in_proj(x, mod6, g1, w_in_s, gq, gk, bd, tabs, tm=256):
    s = x.shape[0]

    def body(x_ref, mod_ref, g1_ref, w_ref, gq_ref, gk_ref, bd_ref, ca, la, ha, cb, lb, hb,
             h_ref, qkraw_ref, qa_ref, ka_ref, va_ref, qb_ref, kb_ref, vb_ref, gate_ref):
        xt = x_ref[...]
        r = lax.rsqrt(jnp.mean(xt * xt, axis=-1, keepdims=True) + NORM_EPS)
        h = (xt * r * g1_ref[...]) * (1.0 + mod_ref[1:2, :]) + mod_ref[0:1, :]
        hb16 = h.astype(BF16)
        h_ref[...] = hb16
        proj = jnp.concatenate([_nn(hb16, w_ref[j]) for j in range(N_SHARD)], axis=1)
        qkraw_ref[...] = proj[:, :BRANCH_W + KV_W]
        bdm = bd_ref[...]
        tab_a = (ca[...], la[...], ha[...])
        tab_b = (cb[...], lb[...], hb[...])

        def norm_rope_a(z, gain):
            zn = z * lax.rsqrt(_head_mean(z * z, bdm) + NORM_EPS) * gain
            return _rope(zn, *tab_a, 16)

        def put(ref, first, z):
            zb = z.astype(BF16)
            ref[first] = zb[:, :HEAD_DIM]
            ref[first + 1] = zb[:, HEAD_DIM:]

        for i in range(Q_HEADS // 2):
            put(qa_ref, 2 * i, norm_rope_a(proj[:, LANES * i:LANES * (i + 1)], gq_ref[...]) * Q_SCALE)
        off = BRANCH_W
        put(ka_ref, 0, norm_rope_a(proj[:, off:off + LANES], gk_ref[...]))
        off += KV_W
        def put_v(ref, z):
            zb = z.astype(BF16)
            for hd in range(KV_HEADS):
                ref[hd, :, :HEAD_DIM] = zb[:, HEAD_DIM * hd:HEAD_DIM * (hd + 1)]
                ref[hd, :, HEAD_DIM:] = jnp.ones((tm, HEAD_DIM), BF16)

        put_v(va_ref, proj[:, off:off + LANES])
        off += KV_W
        for i in range(Q_HEADS // 2):
            put(qb_ref, 2 * i, _rope(proj[:, off + LANES * i:off + LANES * (i + 1)], *tab_b, 32) * Q_SCALE)
        off += BRANCH_W
        put(kb_ref, 0, _rope(proj[:, off:off + LANES], *tab_b, 32))
        off += KV_W
        put_v(vb_ref, proj[:, off:off + LANES])
        gate_ref[...] = proj[:, QK_W:]

    tab_spec = _row_spec(tm, LANES)
    return _call(
        body, name="in_proj", grid=(s // tm,),
        in_specs=[_row_spec(tm, D_MODEL), _full_spec(mod6.shape), _full_spec(g1.shape), _full_spec(w_in_s.shape),
                  _full_spec(gq.shape), _full_spec(gk.shape), _full_spec(bd.shape)] + [tab_spec] * 6,
        out_specs=[_row_spec(tm, D_MODEL), _row_spec(tm, BRANCH_W + KV_W), _heads_spec(Q_HEADS, tm), _heads_spec(KV_HEADS, tm),
                   pl.BlockSpec((KV_HEADS, tm, LANES), lambda i: (0, i, 0)), _heads_spec(Q_HEADS, tm),
                   _heads_spec(KV_HEADS, tm), pl.BlockSpec((KV_HEADS, tm, LANES), lambda i: (0, i, 0)),
                   _row_spec(tm, 2 * D_MODEL)],
        out_shape=[jax.ShapeDtypeStruct((s, D_MODEL), BF16), jax.ShapeDtypeStruct((s, BRANCH_W + KV_W), F32),
                   jax.ShapeDtypeStruct((Q_HEADS, s, HEAD_DIM), BF16), jax.ShapeDtypeStruct((KV_HEADS, s, HEAD_DIM), BF16),
                   jax.ShapeDtypeStruct((KV_HEADS, s, LANES), BF16), jax.ShapeDtypeStruct((Q_HEADS, s, HEAD_DIM), BF16),
                   jax.ShapeDtypeStruct((KV_HEADS, s, HEAD_DIM), BF16), jax.ShapeDtypeStruct((KV_HEADS, s, LANES), BF16),
                   jax.ShapeDtypeStruct((s, 2 * D_MODEL), F32)],
        compiler_params=_params(("parallel",)),
    )(x, mod6, g1, w_in_s, gq, gk, bd, *tabs)


def _group_specs(s, tq):
    q_spec = pl.BlockSpec((None, GROUP, tq, HEAD_DIM), lambda g, i: (g, 0, i, 0))
    kv_spec = pl.BlockSpec((None, s, HEAD_DIM), lambda g, i: (g, 0, 0))
    col_spec = pl.BlockSpec((None, GROUP, tq, 1), lambda g, i: (g, 0, i, 0))
    return q_spec, kv_spec, col_spec


def _attn_a_fwd(q, k, v1, tq=256, tk=512):
    s = q.shape[1]
    tk = min(tk, s)
    rows = GROUP * tq

    n = s // tk
    assert n >= 2 and n % 2 == 0

    def body(q_ref, k_ref, v_ref, o_ref, lse_ref, s0_ref, s1_ref, p0_ref, p1_ref, m_ref, a_ref, acc_ref):
        s_ref, p_ref = (s0_ref, s1_ref), (p0_ref, p1_ref)
        qq = q_ref[...].reshape(rows, HEAD_DIM)
        m_ref[...] = jnp.full((rows, 1), NEG_INF, F32)
        acc_ref[...] = jnp.zeros((rows, LANES), F32)

        def keys(i):
            return pl.ds(pl.multiple_of(i * tk, tk), tk)

        def scores(i, slot):
            s_ref[slot][...] = _nt(qq, k_ref[keys(i), :])

        def softmax(slot):
            sc = s_ref[slot][...]
            m = m_ref[...]
            mn = jnp.maximum(m, jnp.max(sc, axis=-1, keepdims=True))
            m_ref[...] = mn
            a_ref[...] = jnp.exp(m - mn)
            p_ref[slot][...] = jnp.exp(sc - mn).astype(BF16)

        def weigh(i, slot):
            acc_ref[...] = a_ref[...] * acc_ref[...] + _nn(p_ref[slot][...], v_ref[keys(i), :])

        scores(0, 0)
        softmax(0)
        scores(1, 1)

        def two_steps(j, carry):
            i = 2 * j + 1
            weigh(i - 1, 0)
            softmax(1)
            scores(i + 1, 0)
            weigh(i, 1)
            softmax(0)
            scores(i + 2, 1)
            return carry

        lax.fori_loop(0, (n - 2) // 2, two_steps, 0)
        weigh(n - 2, 0)
        softmax(1)
        weigh(n - 1, 1)
        l = acc_ref[:, HEAD_DIM:HEAD_DIM + 1]
        o = (acc_ref[:, :HEAD_DIM] / l).astype(BF16)
        for g in range(GROUP):
            o_ref[:, HEAD_DIM * g:HEAD_DIM * (g + 1)] = o[tq * g:tq * (g + 1)]
        lse_ref[...] = (m_ref[...] + jnp.log(l)).reshape(GROUP, tq, 1)

    q_spec, kv_spec, col_spec = _group_specs(s, tq)
    v_spec = pl.BlockSpec((None, s, LANES), lambda g, i: (g, 0, 0))
    return _call(
        body, name="attn_a_fwd", grid=(KV_HEADS, s // tq),
        in_specs=[q_spec, kv_spec, v_spec],
        out_specs=[pl.BlockSpec((tq, GROUP * HEAD_DIM), lambda g, i: (i, g)), col_spec],
        out_shape=[jax.ShapeDtypeStruct((s, BRANCH_W), BF16), jax.ShapeDtypeStruct((KV_HEADS, GROUP, s, 1), F32)],
        scratch_shapes=[pltpu.VMEM((rows, tk), F32), pltpu.VMEM((rows, tk), F32), pltpu.VMEM((rows, tk), BF16),
                        pltpu.VMEM((rows, tk), BF16), pltpu.VMEM((rows, 1), F32), pltpu.VMEM((rows, 1), F32),
                        pltpu.VMEM((rows, LANES), F32)],
        compiler_params=_params(("parallel", "parallel")),
    )(q.reshape(KV_HEADS, GROUP, s, HEAD_DIM), k, v1)


def _attn_a_bwd(q, k, v, do, lse, delta, tq=256, tk=512):
    s = q.shape[1]
    tk = min(tk, s)
    rows = GROUP * tq

    n = s // tk
    assert n >= 2 and n % 2 == 0

    def body(q_ref, k_ref, v_ref, do_ref, lse_ref, dl_ref, dq_ref, dk_ref, dv_ref,
             s0_ref, s1_ref, dp0_ref, dp1_ref, p0_ref, p1_ref, ds0_ref, ds1_ref, dq_acc):
        s_ref, dp_ref, p_ref, ds_ref = (s0_ref, s1_ref), (dp0_ref, dp1_ref), (p0_ref, p1_ref), (ds0_ref, ds1_ref)

        @pl.when(pl.program_id(1) == 0)
        def _():
            dk_ref[...] = jnp.zeros_like(dk_ref)
            dv_ref[...] = jnp.zeros_like(dv_ref)

        qq = q_ref[...].reshape(rows, HEAD_DIM)
        dd = do_ref[...].reshape(rows, HEAD_DIM)
        ls = lse_ref[...].reshape(rows, 1)
        dl = dl_ref[...].reshape(rows, 1)
        dq_acc[...] = jnp.zeros((rows, HEAD_DIM), F32)

        def keys(i):
            return pl.ds(pl.multiple_of(i * tk, tk), tk)

        def scores(i, slot):
            s_ref[slot][...] = _nt(qq, k_ref[keys(i), :])
            dp_ref[slot][...] = _nt(dd, v_ref[keys(i), :HEAD_DIM])

        def weights(slot):
            p = jnp.exp(s_ref[slot][...] - ls)
            p_ref[slot][...] = p.astype(BF16)
            ds_ref[slot][...] = (p * (dp_ref[slot][...] - dl)).astype(BF16)

        def grads(i, slot):
            dv_ref[keys(i), :] += _tn(p_ref[slot][...], dd)
            dk_ref[keys(i), :] += _tn(ds_ref[slot][...], qq)
            dq_acc[...] += _nn(ds_ref[slot][...], k_ref[keys(i), :])

        scores(0, 0)
        weights(0)
        scores(1, 1)

        def two_steps(j, carry):
            i = 2 * j + 1
            grads(i - 1, 0)
            weights(1)
            scores(i + 1, 0)
            grads(i, 1)
            weights(0)
            scores(i + 2, 1)
            return carry

        lax.fori_loop(0, (n - 2) // 2, two_steps, 0)
        grads(n - 2, 0)
        weights(1)
        grads(n - 1, 1)
        dq_ref[...] = dq_acc[...].reshape(GROUP, tq, HEAD_DIM)

    q_spec, kv_spec, col_spec = _group_specs(s, tq)
    v_spec = pl.BlockSpec((None, s, LANES), lambda g, i: (g, 0, 0))
    shape4 = (KV_HEADS, GROUP, s, HEAD_DIM)
    tile32, tile16 = pltpu.VMEM((rows, tk), F32), pltpu.VMEM((rows, tk), BF16)
    return _call(
        body, name="attn_a_bwd", grid=(KV_HEADS, s // tq),
        in_specs=[q_spec, kv_spec, v_spec, q_spec, col_spec, col_spec],
        out_specs=[q_spec, kv_spec, kv_spec],
        out_shape=[jax.ShapeDtypeStruct(shape4, F32), jax.ShapeDtypeStruct((KV_HEADS, s, HEAD_DIM), F32),
                   jax.ShapeDtypeStruct((KV_HEADS, s, HEAD_DIM), F32)],
        scratch_shapes=[tile32] * 4 + [tile16] * 4 + [pltpu.VMEM((rows, HEAD_DIM), F32)],
        compiler_params=_params(("parallel", "arbitrary")),
    )(q.reshape(shape4), k, v, do.reshape(shape4), lse, delta)


WIN_KEYS = 4 * WINDOW


def _window_start(i, tq, s):
    return pl.multiple_of(jnp.clip(i * tq - WINDOW, 0, s - WIN_KEYS), WINDOW)


def _window_bias(tq):
    r = jnp.arange(tq, dtype=jnp.int32)[:, None]
    col = jnp.arange(WIN_KEYS, dtype=jnp.int32)[None, :]
    return jnp.stack([jnp.where(jnp.abs(r - col + WINDOW * b) <= WINDOW, 0.0, NEG_INF) for b in range(3)]).astype(F32)


def _bias_spec(tq, nq):
    return pl.BlockSpec((None, tq, WIN_KEYS), lambda g, i: (jnp.where(i == 0, 0, jnp.where(i == nq - 1, 2, 1)), 0, 0))


def _masked(sc, bias_ref, tq):
    return (sc.reshape(GROUP, tq, WIN_KEYS) + bias_ref[...][None]).reshape(GROUP * tq, WIN_KEYS)


def _attn_b_fwd(q, k, v1, sink_col, bias, rider=None, tq=2 * WINDOW):
    s = q.shape[1]
    rows = GROUP * tq

    def body(q_ref, k_ref, v_ref, sink_ref, bias_ref, o_ref, lse_ref):
        at = pl.ds(_window_start(pl.program_id(1), tq, s), WIN_KEYS)
        qq = q_ref[...].reshape(rows, HEAD_DIM)
        sc = _masked(_nt(qq, k_ref[at, :]), bias_ref, tq)
        sk = sink_ref[...]
        m = jnp.maximum(jnp.max(sc, axis=-1, keepdims=True), sk)
        acc = _nn(jnp.exp(sc - m).astype(BF16), v_ref[at, :])
        l = acc[:, HEAD_DIM:HEAD_DIM + 1] + jnp.exp(sk - m)
        o = (acc[:, :HEAD_DIM] / l).astype(BF16)
        for g in range(GROUP):
            o_ref[:, HEAD_DIM * g:HEAD_DIM * (g + 1)] = o[tq * g:tq * (g + 1)]
        lse_ref[...] = (m + jnp.log(l)).reshape(GROUP, tq, 1)

    q_spec, kv_spec, col_spec = _group_specs(s, tq)
    v_spec = pl.BlockSpec((None, s, LANES), lambda g, i: (g, 0, 0))
    sink_spec = pl.BlockSpec((None, rows, 1), lambda g, i: (g, 0, 0))
    steps = KV_HEADS * (s // tq)
    return _hosted(
        body, rider, name="attn_b_fwd", grid=(KV_HEADS, s // tq),
        in_specs=[q_spec, kv_spec, v_spec, sink_spec, _bias_spec(tq, s // tq)],
        out_specs=[pl.BlockSpec((tq, GROUP * HEAD_DIM), lambda g, i: (i, g)), col_spec],
        out_shape=[jax.ShapeDtypeStruct((s, BRANCH_W), BF16), jax.ShapeDtypeStruct((KV_HEADS, GROUP, s, 1), F32)],
        args=(q.reshape(KV_HEADS, GROUP, s, HEAD_DIM), k, v1, sink_col, bias), middle_at=3 * steps // 4)


def _attn_b_bwd(q, k, v1, do, lse, delta, sink_col, bias, rider=None, tq=2 * WINDOW):
    s = q.shape[1]
    rows = GROUP * tq

    def body(q_ref, k_ref, v_ref, do_ref, lse_ref, dl_ref, sink_ref, bias_ref, dq_ref, dk_ref, dv_ref, dsink_ref):
        i = pl.program_id(1)

        @pl.when(i == 0)
        def _():
            dk_ref[...] = jnp.zeros_like(dk_ref)
            dv_ref[...] = jnp.zeros_like(dv_ref)
            dsink_ref[...] = jnp.zeros_like(dsink_ref)

        start = _window_start(i, tq, s)
        at = pl.ds(start, WIN_KEYS)
        qq = q_ref[...].reshape(rows, HEAD_DIM)
        dd = do_ref[...].reshape(rows, HEAD_DIM)
        ls = lse_ref[...].reshape(rows, 1)
        dl = dl_ref[...].reshape(rows, 1)
        kk = k_ref[at, :]
        p = jnp.exp(_masked(_nt(qq, kk), bias_ref, tq) - ls)
        ds = (p * (_nt(dd, v_ref[at, :HEAD_DIM]) - dl)).astype(BF16)
        dv_ref[at, :] += _tn(p.astype(BF16), dd)
        dk_ref[at, :] += _tn(ds, qq)
        dq_ref[...] = _nn(ds, kk).reshape(GROUP, tq, HEAD_DIM)
        dsk = jnp.exp(sink_ref[...] - ls) * dl
        for g in range(GROUP):
            dsink_ref[g:g + 1, :] -= jnp.broadcast_to(jnp.sum(dsk[tq * g:tq * (g + 1)], axis=0, keepdims=True), (1, LANES))

    q_spec, kv_spec, col_spec = _group_specs(s, tq)
    sink_spec = pl.BlockSpec((None, rows, 1), lambda g, i: (g, 0, 0))
    dsink_spec = pl.BlockSpec((None, ACC_ROWS, LANES), lambda g, i: (g, 0, 0))
    shape4 = (KV_HEADS, GROUP, s, HEAD_DIM)
    v_spec = pl.BlockSpec((None, s, LANES), lambda g, i: (g, 0, 0))
    return _hosted(
        body, rider, name="attn_b_bwd", grid=(KV_HEADS, s // tq),
        in_specs=[q_spec, kv_spec, v_spec, q_spec, col_spec, col_spec, sink_spec, _bias_spec(tq, s // tq)],
        out_specs=[q_spec, kv_spec, kv_spec, dsink_spec],
        out_shape=[jax.ShapeDtypeStruct(shape4, F32), jax.ShapeDtypeStruct((KV_HEADS, s, HEAD_DIM), F32),
                   jax.ShapeDtypeStruct((KV_HEADS, s, HEAD_DIM), F32), jax.ShapeDtypeStruct((KV_HEADS, ACC_ROWS, LANES), F32)],
        args=(q.reshape(shape4), k, v1, do.reshape(shape4), lse, delta, sink_col, bias))


def _post_attn(ya, yb, gates, x, mod6, wbr_s, w_out, tm=256):
    s = x.shape[0]

    def body(ya_ref, yb_ref, g_ref, x_ref, mod_ref, wbr_ref, wo_ref, ua_ref, ub_ref, mg_ref, o_ref, x1_ref):
        ya_t, yb_t = ya_ref[...], yb_ref[...]
        ua = jnp.concatenate([_nn(ya_t, wbr_ref[j, 0]) for j in range(N_SHARD)], axis=1)
        ub = jnp.concatenate([_nn(yb_t, wbr_ref[j, 1]) for j in range(N_SHARD)], axis=1)
        merged = (_sigmoid(g_ref[:, :D_MODEL]) * ua + _sigmoid(g_ref[:, D_MODEL:]) * ub).astype(BF16)
        o = _nn(merged, wo_ref[...])
        ua_ref[...] = ua.astype(BF16)
        ub_ref[...] = ub.astype(BF16)
        mg_ref[...] = merged
        o_ref[...] = o.astype(BF16)
        x1_ref[...] = x_ref[...] + mod_ref[2:3, :] * o

    bf = jax.ShapeDtypeStruct((s, D_MODEL), BF16)
    return _call(
        body, name="post_attn", grid=(s // tm,),
        in_specs=[_row_spec(tm, BRANCH_W), _row_spec(tm, BRANCH_W), _row_spec(tm, 2 * D_MODEL), _row_spec(tm, D_MODEL),
                  _full_spec(mod6.shape), _full_spec(wbr_s.shape), _full_spec(w_out.shape)],
        out_specs=[_row_spec(tm, D_MODEL)] * 5,
        out_shape=[bf, bf, bf, bf, jax.ShapeDtypeStruct((s, D_MODEL), F32)],
        compiler_params=_params(("parallel",)),
    )(ya, yb, gates, x, mod6, wbr_s, w_out)


def _mlp_in(x1, mod6, g2, w_mi_s, tm=256):
    s = x1.shape[0]

    def body(x_ref, mod_ref, g_ref, w_ref, h2_ref, a_ref, hid_ref):
        xt = x_ref[...]
        r = lax.rsqrt(jnp.mean(xt * xt, axis=-1, keepdims=True) + NORM_EPS)
        h2 = ((xt * r * g_ref[...]) * (1.0 + mod_ref[4:5, :]) + mod_ref[3:4, :]).astype(BF16)
        h2_ref[...] = h2
        a = jnp.concatenate([_nn(h2, w_ref[j]) for j in range(N_SHARD)], axis=1)
        a_ref[...] = a.astype(BF16)
        hid_ref[...] = jnp.square(jnp.maximum(a, 0.0)).astype(BF16)

    return _call(
        body, name="mlp_in", grid=(s // tm,),
        in_specs=[_row_spec(tm, D_MODEL), _full_spec(mod6.shape), _full_spec(g2.shape), _full_spec(w_mi_s.shape)],
        out_specs=[_row_spec(tm, D_MODEL), _row_spec(tm, D_FF), _row_spec(tm, D_FF)],
        out_shape=[jax.ShapeDtypeStruct((s, D_MODEL), BF16), jax.ShapeDtypeStruct((s, D_FF), BF16),
                   jax.ShapeDtypeStruct((s, D_FF), BF16)],
        compiler_params=_params(("parallel",)),
    )(x1, mod6, g2, w_mi_s)


ACC_ROWS = 8


def _acc_spec():
    return pl.BlockSpec((ACC_ROWS, D_MODEL), lambda i: (0, 0))


def _acc_add(acc_ref, rows):
    @pl.when(pl.program_id(0) == 0)
    def _():
        acc_ref[...] = jnp.zeros_like(acc_ref)

    for r, val in enumerate(rows):
        acc_ref[r:r + 1, :] += jnp.sum(val, axis=0, keepdims=True)


def _mlp_out_loss(hid, x1, a, target, mod6, gf, w_mo, tm=256):
    s = x1.shape[0]

    def body(hid_ref, x_ref, a_ref, t_ref, mod_ref, gf_ref, w_ref, dx2_ref, dm_ref, da_ref, acc_ref):
        m = _nn(hid_ref[...], w_ref[...])
        gate2 = mod_ref[5:6, :]
        x2 = x_ref[...] + gate2 * m
        r = lax.rsqrt(jnp.mean(x2 * x2, axis=-1, keepdims=True) + NORM_EPS)
        xn = x2 * r
        err = xn * gf_ref[...] - t_ref[...]
        dy = err * (1.0 / D_MODEL)
        dxn = dy * gf_ref[...]
        dx2 = r * (dxn - xn * jnp.mean(dxn * xn, axis=-1, keepdims=True))
        dx2_ref[...] = dx2
        dm = (dx2 * gate2).astype(BF16)
        dm_ref[...] = dm
        da_ref[...] = (_nt(dm, w_ref[...]) * (2.0 * jnp.maximum(a_ref[...].astype(F32), 0.0))).astype(BF16)
        _acc_add(acc_ref, [err * err, dy * xn, dx2 * m])

    return _call(
        body, name="mlp_out_loss", grid=(s // tm,),
        in_specs=[_row_spec(tm, D_FF), _row_spec(tm, D_MODEL), _row_spec(tm, D_FF), _row_spec(tm, D_MODEL),
                  _full_spec(mod6.shape), _full_spec(gf.shape), _full_spec(w_mo.shape)],
        out_specs=[_row_spec(tm, D_MODEL), _row_spec(tm, D_MODEL), _row_spec(tm, D_FF), _acc_spec()],
        out_shape=[jax.ShapeDtypeStruct((s, D_MODEL), F32), jax.ShapeDtypeStruct((s, D_MODEL), BF16),
                   jax.ShapeDtypeStruct((s, D_FF), BF16), jax.ShapeDtypeStruct((ACC_ROWS, D_MODEL), F32)],
        compiler_params=_params(("arbitrary",)),
    )(hid, x1, a, target, mod6, gf, w_mo)


def _norm_bwd(dh, xt, gain, scale):
    r = lax.rsqrt(jnp.mean(xt * xt, axis=-1, keepdims=True) + NORM_EPS)
    xn = xt * r
    dxn = dh * (gain * (1.0 + scale))
    dx = r * (dxn - xn * jnp.mean(dxn * xn, axis=-1, keepdims=True))
    return dx, [dh, dh * xn * gain, dh * xn * (1.0 + scale)]


def _mlp_bwd(da, x1, dx2, o, mod6, g2, w_mi_s, rider=None, tm=256):
    s = x1.shape[0]

    def body(da_ref, x_ref, dx2_ref, o_ref, mod_ref, g_ref, w_ref, dx1_ref, do_ref, acc_ref):
        dh2 = _nt(da_ref[:, :D_MODEL], w_ref[0])
        for j in range(1, N_SHARD):
            dh2 += _nt(da_ref[:, D_MODEL * j:D_MODEL * (j + 1)], w_ref[j])
        dx, sums = _norm_bwd(dh2, x_ref[...], g_ref[...], mod_ref[4:5, :])
        dx1 = dx2_ref[...] + dx
        dx1_ref[...] = dx1
        do_ref[...] = (dx1 * mod_ref[2:3, :]).astype(BF16)
        _acc_add(acc_ref, sums + [dx1 * o_ref[...].astype(F32)])

    return _hosted(
        body, rider, name="mlp_bwd", grid=(s // tm,),
        in_specs=[_row_spec(tm, D_FF), _row_spec(tm, D_MODEL), _row_spec(tm, D_MODEL), _row_spec(tm, D_MODEL),
                  _full_spec(mod6.shape), _full_spec(g2.shape), _full_spec(w_mi_s.shape)],
        out_specs=[_row_spec(tm, D_MODEL), _row_spec(tm, D_MODEL), _acc_spec()],
        out_shape=[jax.ShapeDtypeStruct((s, D_MODEL), F32), jax.ShapeDtypeStruct((s, D_MODEL), BF16),
                   jax.ShapeDtypeStruct((ACC_ROWS, D_MODEL), F32)],
        args=(da, x1, dx2, o, mod6, g2, w_mi_s))


def _merge_bwd(do, gates, ua, ub, ya, yb, w_out, wbr_s, rider=None, tm=256):
    s = do.shape[0]

    def body(do_ref, g_ref, ua_ref, ub_ref, ya_ref, yb_ref, wo_ref, wbr_ref,
             dua_ref, dub_ref, dg_ref, doa_ref, dob_ref, dla_ref, dlb_ref):
        dmerged = _nt(do_ref[...], wo_ref[...])
        for b, (u_ref, y_ref, du_ref, dy_ref, dl_ref) in enumerate(
                ((ua_ref, ya_ref, dua_ref, doa_ref, dla_ref), (ub_ref, yb_ref, dub_ref, dob_ref, dlb_ref))):
            sg = _sigmoid(g_ref[:, D_MODEL * b:D_MODEL * (b + 1)])
            du = (dmerged * sg).astype(BF16)
            du_ref[...] = du
            dg_ref[:, D_MODEL * b:D_MODEL * (b + 1)] = (dmerged * u_ref[...].astype(F32) * sg * (1.0 - sg)).astype(BF16)
            w = BRANCH_W // 2
            dy = _nt(du[:, :w], wbr_ref[0, b])
            for j in range(1, N_SHARD):
                dy += _nt(du[:, w * j:w * (j + 1)], wbr_ref[j, b])
            prod = dy * y_ref[...].astype(F32)
            dyb = dy.astype(BF16)
            for h in range(Q_HEADS):
                at = slice(HEAD_DIM * h, HEAD_DIM * (h + 1))
                dy_ref[h] = dyb[:, at]
                dl_ref[h] = jnp.sum(prod[:, at], axis=-1, keepdims=True)

    bf = jax.ShapeDtypeStruct((s, D_MODEL), BF16)
    heads = jax.ShapeDtypeStruct((Q_HEADS, s, HEAD_DIM), BF16)
    cols = jax.ShapeDtypeStruct((Q_HEADS, s, 1), F32)
    col_spec = pl.BlockSpec((Q_HEADS, tm, 1), lambda i: (0, i, 0))
    return _hosted(
        body, rider, name="merge_bwd", grid=(s // tm,),
        in_specs=[_row_spec(tm, D_MODEL), _row_spec(tm, 2 * D_MODEL), _row_spec(tm, D_MODEL), _row_spec(tm, D_MODEL),
                  _row_spec(tm, BRANCH_W), _row_spec(tm, BRANCH_W), _full_spec(w_out.shape), _full_spec(wbr_s.shape)],
        out_specs=[_row_spec(tm, D_MODEL), _row_spec(tm, D_MODEL), _row_spec(tm, 2 * D_MODEL),
                   _heads_spec(Q_HEADS, tm), _heads_spec(Q_HEADS, tm), col_spec, col_spec],
        out_shape=[bf, bf, jax.ShapeDtypeStruct((s, 2 * D_MODEL), BF16), heads, heads, cols, cols],
        args=(do, gates, ua, ub, ya, yb, w_out, wbr_s))


def _qk_bwd(dqa, dka, dva, dqb, dkb, dvb, qkraw, dgates, gq, gk, bd, tabs, rider=None, tm=256):
    s = qkraw.shape[0]

    def body(dqa_ref, dka_ref, dva_ref, dqb_ref, dkb_ref, dvb_ref, raw_ref, dg_ref, gq_ref, gk_ref, bd_ref,
             ca, la, ha, cb, lb, hb, dp_ref, acc_ref, pair_ref):
        bdm = bd_ref[...]
        tab_a = (ca[...], la[...], ha[...])
        tab_b = (cb[...], lb[...], hb[...])

        def pair(ref, first):
            pair_ref[:, :HEAD_DIM] = ref[first]
            pair_ref[:, HEAD_DIM:] = ref[first + 1]
            return pair_ref[...]

        def norm_rope_a_bwd(dz, raw, gain):
            dzn = _rope(dz, *tab_a, 16, sign=-1.0)
            rinv = lax.rsqrt(_head_mean(raw * raw, bdm) + NORM_EPS)
            zhat = raw * rinv
            dzhat = dzn * gain
            return rinv * (dzhat - zhat * _head_mean(dzhat * zhat, bdm)), dzn * zhat

        gq_rows = jnp.zeros((tm, LANES), F32)
        for i in range(Q_HEADS // 2):
            at = slice(LANES * i, LANES * (i + 1))
            draw, gsum = norm_rope_a_bwd(pair(dqa_ref, 2 * i) * Q_SCALE, raw_ref[:, at], gq_ref[...])
            dp_ref[:, at] = draw.astype(BF16)
            gq_rows += gsum
        off = BRANCH_W
        draw, gk_rows = norm_rope_a_bwd(pair(dka_ref, 0), raw_ref[:, off:off + LANES], gk_ref[...])
        dp_ref[:, off:off + LANES] = draw.astype(BF16)
        off += KV_W
        dp_ref[:, off:off + LANES] = pair(dva_ref, 0).astype(BF16)
        off += KV_W
        for i in range(Q_HEADS // 2):
            dz = _rope(pair(dqb_ref, 2 * i) * Q_SCALE, *tab_b, 32, sign=-1.0)
            dp_ref[:, off + LANES * i:off + LANES * (i + 1)] = dz.astype(BF16)
        off += BRANCH_W
        dp_ref[:, off:off + LANES] = _rope(pair(dkb_ref, 0), *tab_b, 32, sign=-1.0).astype(BF16)
        off += KV_W
        dp_ref[:, off:off + LANES] = pair(dvb_ref, 0).astype(BF16)
        dp_ref[:, QK_W:] = dg_ref[...]

        @pl.when(pl.program_id(0) == 0)
        def _():
            acc_ref[...] = jnp.zeros_like(acc_ref)

        acc_ref[0:1, :] += jnp.sum(gq_rows, axis=0, keepdims=True)
        acc_ref[1:2, :] += jnp.sum(gk_rows, axis=0, keepdims=True)

    tab_spec = _row_spec(tm, LANES)
    return _hosted(
        body, rider, name="qk_bwd", grid=(s // tm,),
        in_specs=[_heads_spec(Q_HEADS, tm), _heads_spec(KV_HEADS, tm), _heads_spec(KV_HEADS, tm),
                  _heads_spec(Q_HEADS, tm), _heads_spec(KV_HEADS, tm), _heads_spec(KV_HEADS, tm),
                  _row_spec(tm, BRANCH_W + KV_W), _row_spec(tm, 2 * D_MODEL),
                  _full_spec(gq.shape), _full_spec(gk.shape), _full_spec(bd.shape)] + [tab_spec] * 6,
        out_specs=[_row_spec(tm, IN_W), pl.BlockSpec((ACC_ROWS, LANES), lambda i: (0, 0))],
        out_shape=[jax.ShapeDtypeStruct((s, IN_W), BF16), jax.ShapeDtypeStruct((ACC_ROWS, LANES), F32)],
        scratch_shapes=[pltpu.VMEM((tm, LANES), F32)],
        args=(dqa, dka, dva, dqb, dkb, dvb, qkraw, dgates, gq, gk, bd, *tabs))


def _in_proj_bwd(dproj, x, dx1, mod6, g1, w_in_s, tm=256):
    s = x.shape[0]
    w = IN_W // N_SHARD

    def body(dp_ref, x_ref, dx1_ref, mod_ref, g_ref, w_ref, gx_ref, acc_ref):
        dh = _nt(dp_ref[:, :w], w_ref[0])
        for j in range(1, N_SHARD):
            dh += _nt(dp_ref[:, w * j:w * (j + 1)], w_ref[j])
        dx, sums = _norm_bwd(dh, x_ref[...], g_ref[...], mod_ref[1:2, :])
        gx_ref[...] = dx1_ref[...] + dx
        _acc_add(acc_ref, sums)

    return _call(
        body, name="in_proj_bwd", grid=(s // tm,),
        in_specs=[_row_spec(tm, IN_W), _row_spec(tm, D_MODEL), _row_spec(tm, D_MODEL),
                  _full_spec(mod6.shape), _full_spec(g1.shape), _full_spec(w_in_s.shape)],
        out_specs=[_row_spec(tm, D_MODEL), _acc_spec()],
        out_shape=[jax.ShapeDtypeStruct((s, D_MODEL), F32), jax.ShapeDtypeStruct((ACC_ROWS, D_MODEL), F32)],
        compiler_params=_params(("arbitrary",)),
    )(dproj, x, dx1, mod6, g1, w_in_s)


def _wgrad(name, a, b, out_shape, out_spec, tm, tn, tk=2048):
    s, m = a.shape
    n = b.shape[1]
    tk = min(tk, s)
    nk = s // tk

    def body(a_ref, b_ref, o_ref, acc_ref):
        k = pl.program_id(2)

        @pl.when(k == 0)
        def _():
            acc_ref[...] = jnp.zeros_like(acc_ref)

        acc_ref[...] += _tn(a_ref[...], b_ref[...])

        @pl.when(k == nk - 1)
        def _():
            o_ref[...] = acc_ref[...].reshape(o_ref.shape)

    return _call(
        body, name=name, grid=(m // tm, n // tn, nk),
        in_specs=[pl.BlockSpec((tk, tm), lambda i, j, k: (k, i)), pl.BlockSpec((tk, tn), lambda i, j, k: (k, j))],
        out_specs=out_spec, out_shape=jax.ShapeDtypeStruct(out_shape, F32),
        scratch_shapes=[pltpu.VMEM((tm, tn), F32)],
        compiler_params=_params(("parallel", "parallel", "arbitrary")),
    )(a, b)


def _wgrad_branch(ya, yb, dua, dub, tk=2048):
    s = ya.shape[0]
    tk = min(tk, s)
    nk = s // tk
    w = D_MODEL // N_SHARD

    def body(ya_ref, yb_ref, dua_ref, dub_ref, o_ref, acc_ref):
        b, k = pl.program_id(0), pl.program_id(2)

        @pl.when(k == 0)
        def _():
            acc_ref[...] = jnp.zeros_like(acc_ref)

        @pl.when(b == 0)
        def _():
            acc_ref[...] += _tn(ya_ref[...], dua_ref[...])

        @pl.when(b == 1)
        def _():
            acc_ref[...] += _tn(yb_ref[...], dub_ref[...])

        @pl.when(k == nk - 1)
        def _():
            o_ref[...] = acc_ref[...]

    y_spec = pl.BlockSpec((tk, BRANCH_W), lambda b, j, k: (k, 0))
    du_spec = pl.BlockSpec((tk, w), lambda b, j, k: (k, j))
    return _call(
        body, name="wgrad_branch", grid=(2, N_SHARD, nk),
        in_specs=[y_spec, y_spec, du_spec, du_spec],
        out_specs=pl.BlockSpec((None, None, BRANCH_W, w), lambda b, j, k: (j, b, 0, 0)),
        out_shape=jax.ShapeDtypeStruct((N_SHARD, 2, BRANCH_W, w), F32),
        scratch_shapes=[pltpu.VMEM((BRANCH_W, w), F32)],
        compiler_params=_params(("parallel", "parallel", "arbitrary")),
    )(ya, yb, dua, dub)


def _local_step(x, target, mod6, g1, g2, gf, gq2, gk2, sink, w_in_s, rest, cj=None):
    s = x.shape[0]
    dist = cj is not None
    tabs = _rope_tables(s)
    bd = _block_diag()
    tq_b = 2 * WINDOW
    sink_col = jnp.repeat(sink.reshape(KV_HEADS, GROUP, 1), tq_b, axis=1).reshape(KV_HEADS, GROUP * tq_b, 1)
    shard = D_MODEL // N_SHARD

    h, qkraw, qa, ka, va, qb, kb, vb, gates = _in_proj(x, mod6, g1, w_in_s, gq2, gk2, bd, tabs)
    bias = _window_bias(tq_b)
    (yb, lse_b), gathered = _attn_b_fwd(qb, kb, vb, sink_col, bias, rider=_gather_rider(rest) if dist else None)
    wbr_s, w_out, w_mi_s, w_mo = gathered if dist else rest
    wbr_s = wbr_s.reshape(N_SHARD, 2, BRANCH_W, shard)
    w_out = w_out.reshape(D_MODEL, D_MODEL)
    w_mo = w_mo.reshape(D_FF, D_MODEL)
    ya, lse_a = _attn_a_fwd(qa, ka, va)
    ua, ub, merged, o, x1 = _post_attn(ya, yb, gates, x, mod6, wbr_s, w_out)
    h2, a, hid = _mlp_in(x1, mod6, g2, w_mi_s)
    dx2, dm, da, acc_out = _mlp_out_loss(hid, x1, a, target, mod6, gf, w_mo)

    g_w_mo = _wgrad("wgrad_mlp_out", hid, dm, (D_FF, D_MODEL), pl.BlockSpec((512, D_MODEL), lambda i, j, k: (i, 0)),
                    512, D_MODEL).reshape(N_SHARD, D_MODEL, D_MODEL)
    g_w_mi = _wgrad("wgrad_mlp_in", h2, da, (N_SHARD, D_MODEL, D_MODEL),
                    pl.BlockSpec((None, 512, D_MODEL), lambda i, j, k: (j, i, 0)), 512, D_MODEL)
    mlp = _Reduction(("mlp_out", "mlp_in"), (g_w_mo, g_w_mi), cj)
    (dx1, do, acc_mlp), got = _mlp_bwd(da, x1, dx2, o, mod6, g2, w_mi_s, rider=mlp.swap() if dist else None)
    (dua, dub, dgates, doa, dob, dl_a, dl_b), landed = _merge_bwd(do, gates, ua, ub, ya, yb, w_out, wbr_s,
                                                                  rider=mlp.add(got) if dist else None)
    g_w_out = _wgrad("wgrad_out", merged, do, (D_MODEL, D_MODEL), pl.BlockSpec((512, D_MODEL), lambda i, j, k: (i, 0)),
                     512, D_MODEL).reshape(N_SHARD, shard, D_MODEL)
    g_wbr = _wgrad_branch(ya, yb, dua, dub)
    col4 = (KV_HEADS, GROUP, s, 1)
    out = _Reduction(("out", "branch"), (g_w_out, g_wbr.reshape(N_SHARD, 2 * BRANCH_W, shard)), cj)
    (dqb, dkb, dvb, dsink), landings = _attn_b_bwd(qb, kb, vb, dob, lse_b, dl_b.reshape(col4), sink_col, bias,
                                                   rider=_riders(out.swap(), mlp.total(landed)) if dist else None)
    dqa, dka, dva = _attn_a_bwd(qa, ka, va, doa, lse_a, dl_a.reshape(col4))
    heads = (Q_HEADS, s, HEAD_DIM)
    (dproj, acc_qk), landed = _qk_bwd(dqa.reshape(heads), dka, dva, dqb.reshape(heads), dkb, dvb, qkraw, dgates, gq2, gk2, bd,
                                      tabs, rider=out.add(landings[:2]) if dist else None)
    w = IN_W // N_SHARD
    g_w_in = _wgrad("wgrad_in", h, dproj, (N_SHARD, D_MODEL, w), pl.BlockSpec((None, 512, w), lambda i, j, k: (j, i, 0)),
                    512, w)
    grad_x, acc_in = _in_proj_bwd(dproj, x, dx1, mod6, g1, w_in_s)
    accs = (acc_out, acc_mlp, acc_in, acc_qk, dsink)
    if not dist:
        return grad_x, (g_w_in, g_wbr, g_w_out, g_w_mi, g_w_mo), accs
    first = _Reduction(("in",), (g_w_in,), cj)
    landed_in = _alone("scatter_in", first.add(_alone("swap_in", first.swap())))
    r_in, r_out, r_br = _alone("join_in_out_branch", _riders(first.total(landed_in), out.total(landed)))
    r_mo, r_mi = landings[2:]
    return grad_x, (r_in, r_br, r_out, r_mi, r_mo), accs


def _me():
    return lax.axis_index("x"), lax.axis_index("y"), lax.axis_index("c")


def _peer(d):
    x, y, c = _me()
    return (1 - x if d & 4 else x, 1 - y if d & 2 else y, 1 - c if d & 1 else c)


def _dev_index(p):
    return 4 * p[0] + 2 * p[1] + p[2]


def _chip_index(p):
    return 2 * p[0] + p[1]


def _remote(src, dst, send_sem, recv_sem, to):
    return pltpu.make_async_remote_copy(src_ref=src, dst_ref=dst, send_sem=send_sem, recv_sem=recv_sem,
                                        device_id=to, device_id_type=MESH)


SLOT_ROWS = 8


def _ada_fwd(c, w_ada, b4):
    cols = w_ada.shape[1]

    def body(c_ref, w_ref, b_ref, mod_ref, sc_ref, cbuf, pbuf, mbuf, send1, recv1, send2, recv2):
        me = _me()
        mine, chip = _dev_index(me), _chip_index(me)
        cbuf[mine] = jnp.broadcast_to(c_ref[...], (SLOT_ROWS, D_MODEL))
        gather = [_remote(cbuf.at[mine], cbuf.at[mine], send1.at[d - 1], recv1.at[d - 1], _peer(d)) for d in range(1, N_DEV)]
        for cp in gather:
            cp.start()
        for d in range(1, N_DEV):
            _remote(cbuf.at[mine], cbuf.at[_dev_index(_peer(d))], send1.at[d - 1], recv1.at[d - 1], _peer(d)).wait_recv()
        call = cbuf[...].reshape(N_DEV * SLOT_ROWS, D_MODEL)
        sc = call * _sigmoid(call)
        for s in range(N_DEV):
            sc_ref[s:s + 1, :] = sc[SLOT_ROWS * s:SLOT_ROWS * s + 1]
        part = _nn(sc.astype(BF16), w_ref[...].astype(BF16)) + b_ref[pl.ds(chip, 1), :]
        pbuf[...] = part.reshape(N_DEV, SLOT_ROWS, cols)
        mbuf[chip] = pbuf[mine]
        spread = [_remote(pbuf.at[_dev_index(_peer(d))], mbuf.at[chip], send2.at[d // 2 - 1], recv2.at[d // 2 - 1], _peer(d))
                  for d in (2, 4, 6)]
        for cp in spread:
            cp.start()
        for d in (2, 4, 6):
            _remote(pbuf.at[mine], mbuf.at[_chip_index(_peer(d))], send2.at[d // 2 - 1], recv2.at[d // 2 - 1],
                    _peer(d)).wait_recv()
        half = D_MODEL // 2
        for p in range(2 * 6):
            col = half * p
            mod_ref[p // 2:p // 2 + 1, half * (p % 2):half * (p % 2 + 1)] = mbuf[col // cols, 0:1, col % cols:col % cols + half]
        for cp in gather + spread:
            cp.wait_send()

    vm = pl.BlockSpec(memory_space=pltpu.VMEM)
    return _call(
        body, name="ada_fwd", in_specs=[vm, vm, vm], out_specs=[vm, vm],
        out_shape=[jax.ShapeDtypeStruct((6, D_MODEL), F32), jax.ShapeDtypeStruct((N_DEV, D_MODEL), F32)],
        scratch_shapes=[pltpu.VMEM((N_DEV, SLOT_ROWS, D_MODEL), F32), pltpu.VMEM((N_DEV, SLOT_ROWS, cols), F32),
                        pltpu.VMEM((N_SHARD, SLOT_ROWS, cols), F32),
                        pltpu.SemaphoreType.DMA((N_DEV - 1,)), pltpu.SemaphoreType.DMA((N_DEV - 1,)),
                        pltpu.SemaphoreType.DMA((N_SHARD - 1,)), pltpu.SemaphoreType.DMA((N_SHARD - 1,))],
        compiler_params=_params(),
    )(c, w_ada, b4)


PACK_ROWS = 16
PACK_W = 3 * D_MODEL


def _ada_bwd(acc_out, acc_mlp, acc_in, acc_qk, dsink, sc_all):
    cols = 6 * D_MODEL // N_SHARD

    def body(out_ref, mlp_ref, in_ref, qk_ref, dsink_ref, sc_ref,
             gwa_ref, gba_ref, gn1_ref, gn2_ref, gf_ref, gq_ref, gk_ref, gs_ref, blk, send, recv):
        me = _me()
        mine, chip = _dev_index(me), _chip_index(me)
        blk[mine] = jnp.zeros((PACK_ROWS, PACK_W), F32)
        dmod = (in_ref, 0), (in_ref, 1), (mlp_ref, 3), (mlp_ref, 0), (mlp_ref, 1), (out_ref, 2)
        half = D_MODEL // 2
        for p in range(2 * 6):
            ref, row = dmod[p // 2]
            col = half * p
            blk[mine, col // cols:col // cols + 1, col % cols:col % cols + half] = ref[row:row + 1, half * (p % 2):half * (p % 2 + 1)]
        blk[mine, 4:5, 0:D_MODEL] = in_ref[2:3, :]
        blk[mine, 4:5, D_MODEL:2 * D_MODEL] = mlp_ref[2:3, :]
        blk[mine, 4:5, 2 * D_MODEL:] = out_ref[1:2, :]
        blk[mine, 5:6, 0:LANES] = qk_ref[0:1, :]
        blk[mine, 5:6, LANES:2 * LANES] = qk_ref[1:2, :]
        for g in range(KV_HEADS):
            blk[mine, 8 + GROUP * g:8 + GROUP * (g + 1), 0:LANES] = dsink_ref[g, 0:GROUP, :]
        copies = [_remote(blk.at[mine], blk.at[mine], send.at[d - 1], recv.at[d - 1], _peer(d)) for d in range(1, N_DEV)]
        for cp in copies:
            cp.start()
        for d in range(1, N_DEV):
            _remote(blk.at[mine], blk.at[_dev_index(_peer(d))], send.at[d - 1], recv.at[d - 1], _peer(d)).wait_recv()
        tot = blk[0]
        for s in range(1, N_DEV):
            tot = tot + blk[s]
        for j in range(N_SHARD):
            gba_ref[:, cols * j:cols * (j + 1)] = tot[j:j + 1, :cols]
        gn1_ref[...] = tot[4:5, 0:D_MODEL]
        gn2_ref[...] = tot[4:5, D_MODEL:2 * D_MODEL]
        gf_ref[...] = tot[4:5, 2 * D_MODEL:]
        gq_ref[...] = tot[5:6, 0:HEAD_DIM] + tot[5:6, HEAD_DIM:2 * HEAD_DIM]
        gk_ref[...] = tot[5:6, LANES:LANES + HEAD_DIM] + tot[5:6, LANES + HEAD_DIM:2 * LANES]
        sq = tot[8:16, 0:Q_HEADS]
        diag = lax.broadcasted_iota(jnp.int32, sq.shape, 0) == lax.broadcasted_iota(jnp.int32, sq.shape, 1)
        gs_ref[...] = jnp.sum(jnp.where(diag, sq, 0.0), axis=0, keepdims=True)
        dm = jnp.concatenate([blk[s, pl.ds(chip, 1), pl.ds(0, cols)] for s in range(N_DEV)], axis=0)
        gwa_ref[...] = _tn(sc_ref[...], dm)
        for cp in copies:
            cp.wait_send()

    vm = pl.BlockSpec(memory_space=pltpu.VMEM)
    row = lambda n: jax.ShapeDtypeStruct((1, n), F32)
    return _call(
        body, name="ada_bwd", in_specs=[vm] * 6, out_specs=[vm] * 8,
        out_shape=[jax.ShapeDtypeStruct((D_MODEL, cols), F32), row(6 * D_MODEL), row(D_MODEL), row(D_MODEL), row(D_MODEL),
                   row(HEAD_DIM), row(HEAD_DIM), row(Q_HEADS)],
        scratch_shapes=[pltpu.VMEM((N_DEV, PACK_ROWS, PACK_W), F32),
                        pltpu.SemaphoreType.DMA((N_DEV - 1,)), pltpu.SemaphoreType.DMA((N_DEV - 1,))],
        compiler_params=_params(),
    )(acc_out, acc_mlp, acc_in, acc_qk, dsink, sc_all)


def _cast_weights(ws):
    n = len(ws)

    def body(*refs):
        src, out, tmp, sems = refs[:n], refs[n:2 * n], refs[2 * n:3 * n], refs[3 * n]
        chip = _chip_index(_me())
        copies = []
        for a in range(n):
            tmp[a][...] = src[a][...].astype(BF16)
            cp = pltpu.make_async_copy(tmp[a], out[a].at[chip], sems.at[a])
            cp.start()
            copies.append(cp)
        for cp in copies:
            cp.wait()

    vm = pl.BlockSpec(memory_space=pltpu.VMEM)
    return _call(
        body, name="cast_weights", in_specs=[vm] * n, out_specs=[ANY] * n,
        out_shape=[jax.ShapeDtypeStruct((N_SHARD,) + w.shape, BF16) for w in ws],
        scratch_shapes=[pltpu.VMEM(w.shape, BF16) for w in ws] + [pltpu.SemaphoreType.DMA((n,))],
        compiler_params=_params(),
    )(*ws)


def _half_rows(ref_rows, c):
    half = ref_rows // 2
    return pl.ds(pl.multiple_of(c * half, 8), half)


class _Rider:
    def __init__(self, inputs, out_shape, aliases, n_sems, start, finish, middle=None):
        self.inputs, self.out_shape, self.aliases, self.n_sems = list(inputs), list(out_shape), dict(aliases), n_sems
        self.start, self.finish, self.middle = start, finish, middle


def _riders(*rs):
    ins = [0]
    outs = [0]
    sems = [0]
    for r in rs:
        ins.append(ins[-1] + len(r.inputs))
        outs.append(outs[-1] + len(r.out_shape))
        sems.append(sems[-1] + r.n_sems)

    def phase(which):
        def run(in_refs, out_refs, sem):
            for k, r in enumerate(rs):
                fn = getattr(r, which)
                if fn is not None:
                    fn(in_refs[ins[k]:ins[k + 1]], out_refs[outs[k]:outs[k + 1]], lambda j, base=sems[k]: sem(base + j))
        return run

    aliases = {ins[k] + i: outs[k] + o for k, r in enumerate(rs) for i, o in r.aliases.items()}
    return _Rider([a for r in rs for a in r.inputs], [o for r in rs for o in r.out_shape], aliases, sems[-1],
                  phase("start"), phase("finish"), phase("middle") if any(r.middle for r in rs) else None)


def _hosted(body, rider, *, name, grid, in_specs, out_specs, out_shape, args, scratch_shapes=(), middle_at=None):
    sem = ("arbitrary",) * len(grid)
    if rider is None:
        res = _call(body, name=name, grid=grid, in_specs=in_specs, out_specs=out_specs, out_shape=out_shape,
                    scratch_shapes=list(scratch_shapes), compiler_params=_params(sem))(*args)
        return res, ()
    n_in, n_out, n_scr = len(in_specs), len(out_specs), len(scratch_shapes)
    r_in, r_out = len(rider.inputs), len(rider.out_shape)

    def riding(*refs):
        at = 0
        parts = []
        for size in (n_in, r_in, n_out, r_out, n_scr):
            parts.append(refs[at:at + size])
            at += size
        ins, rider_ins, outs, rider_outs, scratch = parts
        sems = refs[at]
        step = pl.program_id(0)
        for axis in range(1, len(grid)):
            step = step * grid[axis] + pl.program_id(axis)
        steps = 1
        for size in grid:
            steps *= size

        def sem_at(k):
            return sems.at[k]

        @pl.when(step == 0)
        def _():
            rider.start(rider_ins, rider_outs, sem_at)

        body(*ins, *outs, *scratch)
        if rider.middle is not None:
            @pl.when(step == middle_at)
            def _():
                rider.middle(rider_ins, rider_outs, sem_at)

        @pl.when(step == steps - 1)
        def _():
            rider.finish(rider_ins, rider_outs, sem_at)

    res = _call(
        riding, name=name, grid=grid, in_specs=list(in_specs) + [ANY] * r_in, out_specs=list(out_specs) + [ANY] * r_out,
        out_shape=list(out_shape) + rider.out_shape,
        input_output_aliases={n_in + i: n_out + o for i, o in rider.aliases.items()},
        scratch_shapes=list(scratch_shapes) + [pltpu.SemaphoreType.DMA((rider.n_sems,))],
        compiler_params=_params(sem),
    )(*args, *rider.inputs)
    return res[:n_out], res[n_out:]


def _alone(name, rider):
    n_in, n_out = len(rider.inputs), len(rider.out_shape)

    def body(*refs):
        ins, outs, sems = refs[:n_in], refs[n_in:n_in + n_out], refs[n_in + n_out]

        def sem_at(k):
            return sems.at[k]

        rider.start(ins, outs, sem_at)
        if rider.middle is not None:
            rider.middle(ins, outs, sem_at)
        rider.finish(ins, outs, sem_at)

    return _call(
        body, name=name, in_specs=[ANY] * n_in, out_specs=[ANY] * n_out, out_shape=rider.out_shape,
        input_output_aliases=rider.aliases, scratch_shapes=[pltpu.SemaphoreType.DMA((rider.n_sems,))],
    )(*rider.inputs)


OTHER_CHIPS = (2, 4, 6)


def _gather_rider(stacked):
    n = len(stacked)

    def flights(bufs, sem):
        me = _me()
        chip, sib = _chip_index(me), _peer(1)
        out = []
        for a in range(n):
            mine, theirs = (_half_rows(bufs[a].shape[1], c) for c in (me[2], 1 - me[2]))
            for j, d in enumerate(OTHER_CHIPS):
                k = 3 * a + j
                from_chip = _chip_index(_peer(d))
                own, landed, passed = bufs[a].at[chip, mine], bufs[a].at[from_chip, mine], bufs[a].at[from_chip, theirs]
                out.append((_remote(own, own, sem(k), sem(3 * n + k), _peer(d)),
                            _remote(own, landed, sem(k), sem(3 * n + k), _peer(d)),
                            _remote(landed, landed, sem(6 * n + k), sem(9 * n + k), sib),
                            _remote(passed, passed, sem(6 * n + k), sem(9 * n + k), sib)))
        return out

    def start(ins, outs, sem):
        for send, _, _, _ in flights(outs, sem):
            send.start()

    def middle(ins, outs, sem):
        for _, arrival, pass_on, _ in flights(outs, sem):
            arrival.wait_recv()
            pass_on.start()

    def finish(ins, outs, sem):
        every = flights(outs, sem)
        for _, _, _, passed_to_me in every:
            passed_to_me.wait_recv()
        for send, _, pass_on, _ in every:
            send.wait_send()
            pass_on.wait_send()

    return _Rider(stacked, [jax.ShapeDtypeStruct(w.shape, w.dtype) for w in stacked], {a: a for a in range(n)}, 12 * n,
                  start, finish, middle)


def _swap_rider(grads):
    n = len(grads)

    def copies(ins, outs, sem):
        c = _me()[2]
        return [_remote(ins[a].at[pl.ds(0, N_SHARD), _half_rows(ins[a].shape[1], 1 - c)], outs[a], sem(a), sem(n + a), _peer(1))
                for a in range(n)]

    def start(ins, outs, sem):
        for cp in copies(ins, outs, sem):
            cp.start()

    def finish(ins, outs, sem):
        for cp in copies(ins, outs, sem):
            cp.wait()

    return _Rider(grads, [jax.ShapeDtypeStruct((N_SHARD, g.shape[1] // 2, g.shape[2]), F32) for g in grads], {}, 2 * n,
                  start, finish)


def _row_tile(rows):
    return min(rows, 256)


def _add_halves(name, g, got, cj):
    _, half, cols = got.shape
    tr = _row_tile(half)
    nt = half // tr

    def body(cj_ref, g_ref, got_ref, o_ref):
        o_ref[...] = (g_ref[...] + got_ref[...]).astype(BF16)

    spec = pl.BlockSpec((None, tr, cols), lambda i, s, cj: (s, i, 0))
    return _call(
        body, name=name,
        grid_spec=pltpu.PrefetchScalarGridSpec(
            num_scalar_prefetch=1, grid=(nt, N_SHARD),
            in_specs=[pl.BlockSpec((None, tr, cols), lambda i, s, cj: (s, cj[0] * nt + i, 0)), spec], out_specs=spec),
        out_shape=jax.ShapeDtypeStruct(got.shape, BF16), compiler_params=_params(("parallel", "parallel")),
    )(cj, g, got)


def _scatter_rider(sums):
    n = len(sums)

    def flights(ins, outs, sem):
        chip = _chip_index(_me())
        out = []
        for a in range(n):
            for j, d in enumerate(OTHER_CHIPS):
                k = 3 * a + j
                other = _chip_index(_peer(d))
                out.append((_remote(ins[a].at[other], outs[a].at[chip], sem(k), sem(3 * n + k), _peer(d)),
                            _remote(ins[a].at[chip], outs[a].at[other], sem(k), sem(3 * n + k), _peer(d))))
        return out

    def start(ins, outs, sem):
        for send, _ in flights(ins, outs, sem):
            send.start()

    def finish(ins, outs, sem):
        every = flights(ins, outs, sem)
        for _, arrival in every:
            arrival.wait_recv()
        for send, _ in every:
            send.wait_send()

    return _Rider(sums, [jax.ShapeDtypeStruct(v.shape, v.dtype) for v in sums], {}, 6 * n, start, finish)


def _sum_chips(name, g, got, landed, cj):
    _, half, cols = got.shape
    tr = _row_tile(half)
    nt = half // tr

    def body(cj_ref, g_ref, got_ref, landed_ref, o_ref):
        own = g_ref[...] + got_ref[...]
        total = None
        for s in range(N_SHARD):
            term = jnp.where(cj_ref[1] == s, own, landed_ref[s].astype(F32))
            total = term if total is None else total + term
        o_ref[...] = total

    return _call(
        body, name=name,
        grid_spec=pltpu.PrefetchScalarGridSpec(
            num_scalar_prefetch=1, grid=(nt,),
            in_specs=[pl.BlockSpec((None, tr, cols), lambda i, cj: (cj[1], cj[0] * nt + i, 0)),
                      pl.BlockSpec((None, tr, cols), lambda i, cj: (cj[1], i, 0)),
                      pl.BlockSpec((N_SHARD, tr, cols), lambda i, cj: (0, i, 0))],
            out_specs=pl.BlockSpec((tr, cols), lambda i, cj: (cj[0] * nt + i, 0))),
        out_shape=jax.ShapeDtypeStruct((2 * half, cols), F32), compiler_params=_params(("parallel",)),
    )(cj, g, got, landed)


def _join_rider(shards):
    n = len(shards)

    def flights(bufs, sem):
        c = _me()[2]
        out = []
        for a in range(n):
            mine, theirs = (bufs[a].at[_half_rows(bufs[a].shape[0], cc)] for cc in (c, 1 - c))
            out.append((_remote(mine, mine, sem(a), sem(n + a), _peer(1)), _remote(theirs, theirs, sem(a), sem(n + a), _peer(1))))
        return out

    def start(ins, outs, sem):
        for send, _ in flights(outs, sem):
            send.start()

    def finish(ins, outs, sem):
        for send, arrival in flights(outs, sem):
            arrival.wait_recv()
            send.wait_send()

    return _Rider(shards, [jax.ShapeDtypeStruct(h.shape, F32) for h in shards], {a: a for a in range(n)}, 2 * n, start, finish)


class _Reduction:
    def __init__(self, names, grads, cj):
        self.names, self.grads, self.cj = names, list(grads), cj

    def swap(self):
        return _swap_rider(self.grads)

    def add(self, got):
        self.got = list(got)
        self.sums = [_add_halves("add_halves_" + nm, g, h, self.cj) for nm, g, h in zip(self.names, self.grads, self.got)]
        return _scatter_rider(self.sums)

    def total(self, landed):
        halves = [_sum_chips("sum_chips_" + nm, g, h, l, self.cj)
                  for nm, g, h, l in zip(self.names, self.grads, self.got, landed)]
        return _join_rider(halves)


def _adamw_math(w, g, m, v):
    m = ADAM_B1 * m + (1.0 - ADAM_B1) * g
    v = ADAM_B2 * v + (1.0 - ADAM_B2) * jnp.square(g)
    m_hat = m / (1.0 - ADAM_B1 ** ADAM_STEP)
    v_hat = v / (1.0 - ADAM_B2 ** ADAM_STEP)
    return -ADAM_LR * (m_hat / (jnp.sqrt(v_hat) + ADAM_EPS) + ADAM_WD * w), m, v


def _adamw(name, w, g, m, v):
    rows, cols = w.shape
    tr = _row_tile(rows)

    def body(w_ref, g_ref, m_ref, v_ref, d_ref, nm_ref, nv_ref):
        d_ref[...], nm_ref[...], nv_ref[...] = _adamw_math(w_ref[...], g_ref[...], m_ref[...], v_ref[...])

    spec = pl.BlockSpec((tr, cols), lambda i: (i, 0))
    return _call(
        body, name=name, grid=(rows // tr,), in_specs=[spec] * 4, out_specs=[spec] * 3,
        out_shape=[jax.ShapeDtypeStruct(w.shape, F32)] * 3, compiler_params=_params(("parallel",)),
    )(w, g, m, v)


def _adamw_small(ws, gs, ms, vs):
    n = len(ws)

    def body(*refs):
        ins, outs = refs[:4 * n], refs[4 * n:]
        for a in range(n):
            w, g, m, v = (ins[k * n + a][...] for k in range(4))
            outs[a][...], outs[n + a][...], outs[2 * n + a][...] = _adamw_math(w, g, m, v)

    vm = pl.BlockSpec(memory_space=pltpu.VMEM)
    res = _call(
        body, name="adamw_small", in_specs=[vm] * (4 * n), out_specs=[vm] * (3 * n),
        out_shape=[jax.ShapeDtypeStruct(w.shape, F32) for w in ws] * 3, compiler_params=_params(),
    )(*ws, *gs, *ms, *vs)
    return res[:n], res[n:2 * n], res[2 * n:]


def kernel(x, c, w_ada, b_ada, norm1_g, w_in, q_norm_a, k_norm_a, sink_b, w_branch, w_out, norm2_g, w_mlp_in, w_mlp_out, final_g, loss_target, m_w_ada, m_b_ada, m_norm1_g, m_w_in, m_q_norm_a, m_k_norm_a, m_sink_b, m_w_branch, m_w_out, m_norm2_g, m_w_mlp_in, m_w_mlp_out, m_final_g, v_w_ada, v_b_ada, v_norm1_g, v_w_in, v_q_norm_a, v_k_norm_a, v_sink_b, v_w_branch, v_w_out, v_norm2_g, v_w_mlp_in, v_w_mlp_out, v_final_g):
    xi, yi, ci = _me()
    cj = jnp.stack([ci, 2 * xi + yi]).astype(jnp.int32)
    n_cols = 6 * D_MODEL // N_SHARD

    mod6, sc_all = _ada_fwd(c, w_ada[0], b_ada.reshape(N_SHARD, n_cols))

    def rows2d(a):
        return a.reshape(-1, a.shape[-1])

    big = (w_in, w_branch, w_out, w_mlp_in, w_mlp_out)
    stacked = _cast_weights([rows2d(w) for w in big])
    w_in_s, = _alone("gather_w_in", _gather_rider(stacked[:1]))
    rest = stacked[1:]

    gq2 = jnp.tile(q_norm_a, (1, 2))
    gk2 = jnp.tile(k_norm_a, (1, 2))
    grad_x, g_big, (acc_out, acc_mlp, acc_in, acc_qk, dsink) = _local_step(
        x[0], loss_target[0], mod6, norm1_g, norm2_g, final_g.reshape(1, D_MODEL), gq2, gk2, sink_b[0], w_in_s, rest, cj)

    loss = lax.psum(0.5 * jnp.sum(acc_out[0]) / D_MODEL, ("x", "y", "c"))
    g_w_ada, g_b_ada, g_n1, g_n2, g_f, g_q, g_k, g_s = _ada_bwd(acc_out, acc_mlp, acc_in, acc_qk, dsink, sc_all)

    names = ("w_ada", "w_in", "w_branch", "w_out", "w_mlp_in", "w_mlp_out")
    big_w = [w_ada[0]] + [rows2d(w) for w in big]
    big_g = [g_w_ada] + list(g_big)
    big_m = [rows2d(m) for m in (m_w_ada, m_w_in, m_w_branch, m_w_out, m_w_mlp_in, m_w_mlp_out)]
    big_v = [rows2d(v) for v in (v_w_ada, v_w_in, v_w_branch, v_w_out, v_w_mlp_in, v_w_mlp_out)]
    big_res = {nm: _adamw("adamw_" + nm, w, g, m, v) for nm, w, g, m, v in zip(names, big_w, big_g, big_m, big_v)}

    small = ("b_ada", "norm1_g", "q_norm_a", "k_norm_a", "sink_b", "norm2_g", "final_g")
    row = lambda a: a.reshape(1, -1)
    small_w = [row(a) for a in (b_ada, norm1_g, q_norm_a, k_norm_a, sink_b, norm2_g, final_g)]
    small_g = [g_b_ada, g_n1, g_q, g_k, g_s, g_n2, g_f]
    small_m = [row(a) for a in (m_b_ada, m_norm1_g, m_q_norm_a, m_k_norm_a, m_sink_b, m_norm2_g, m_final_g)]
    small_v = [row(a) for a in (v_b_ada, v_norm1_g, v_q_norm_a, v_k_norm_a, v_sink_b, v_norm2_g, v_final_g)]
    s_d, s_m, s_v = _adamw_small(small_w, small_g, small_m, small_v)

    order = ("w_ada", "b_ada", "norm1_g", "w_in", "q_norm_a", "k_norm_a", "sink_b", "w_branch", "w_out", "norm2_g",
             "w_mlp_in", "w_mlp_out", "final_g")
    like = dict(w_ada=w_ada, b_ada=b_ada, norm1_g=norm1_g, w_in=w_in, q_norm_a=q_norm_a, k_norm_a=k_norm_a, sink_b=sink_b,
                w_branch=w_branch, w_out=w_out, norm2_g=norm2_g, w_mlp_in=w_mlp_in, w_mlp_out=w_mlp_out, final_g=final_g)
    grad, delta, new_m, new_v = {}, {}, {}, {}
    for nm, g in zip(names, big_g):
        grad[nm] = g
        delta[nm], new_m[nm], new_v[nm] = big_res[nm]
    for k, nm in enumerate(small):
        grad[nm], delta[nm], new_m[nm], new_v[nm] = small_g[k], s_d[k], s_m[k], s_v[k]
    outs = [loss, grad_x[None]]
    for group in (grad, delta, new_m, new_v):
        outs += [group[nm].reshape(like[nm].shape) for nm in order]
    return tuple(outs)
```

```python
import functools

import jax
import jax.numpy as jnp
from jax import lax
from jax.experimental import pallas as pl
from jax.experimental.pallas import tpu as pltpu

F32 = jnp.float32
BF16 = jnp.bfloat16
MESH = pl.DeviceIdType.MESH
ANY = pl.BlockSpec(memory_space=pl.ANY)

D_MODEL = 1024
HEAD_DIM = 64
Q_HEADS = 8
KV_HEADS = 2
GROUP = Q_HEADS // KV_HEADS
BRANCH_W = Q_HEADS * HEAD_DIM
KV_W = KV_HEADS * HEAD_DIM
IN_W = 2 * (BRANCH_W + 2 * KV_W) + 2 * D_MODEL
QK_W = 2 * (BRANCH_W + 2 * KV_W)
D_FF = 4 * D_MODEL
GRID_W = 64
WINDOW = 128
ROPE_THETA = 10000.0
NORM_EPS = 1e-6
NEG_INF = -1e30
Q_SCALE = HEAD_DIM ** -0.5
N_SHARD = 4
N_DEV = 8
LANES = 128
VMEM_LIMIT = 56 * 1024 * 1024

ADAM_LR = 0.001
ADAM_B1 = 0.9
ADAM_B2 = 0.999
ADAM_EPS = 1e-08
ADAM_WD = 0.01
ADAM_STEP = 10

_call = pl.pallas_call


def _params(sem=None, vmem=VMEM_LIMIT):
    return pltpu.CompilerParams(dimension_semantics=sem, vmem_limit_bytes=vmem)


def _nt(a, b):
    return lax.dot_general(a, b, (((1,), (1,)), ((), ())), preferred_element_type=F32)


def _tn(a, b):
    return lax.dot_general(a, b, (((0,), (0,)), ((), ())), preferred_element_type=F32)


def _nn(a, b):
    return jnp.dot(a, b, preferred_element_type=F32)


def _sigmoid(z):
    return 0.5 * jnp.tanh(0.5 * z) + 0.5


def _rope_tables(s):
    t = jnp.arange(s, dtype=jnp.int32)
    lane = jnp.arange(LANES, dtype=jnp.int32)

    def cos_sin(pos, dim):
        inv = ROPE_THETA ** (-jnp.arange(0, dim, 2, dtype=F32) / dim)
        ang = pos.astype(F32)[:, None] * inv[None, :]
        return jnp.cos(ang), jnp.sin(ang)

    cr, sr = cos_sin(t // GRID_W, HEAD_DIM // 2)
    cc, sc = cos_sin(t % GRID_W, HEAD_DIM // 2)
    cos_a = jnp.tile(jnp.concatenate([cr, cr, cc, cc], axis=1), (1, 2))
    sin_a = jnp.tile(jnp.concatenate([sr, sr, sc, sc], axis=1), (1, 2))
    first_a = (lane % 32) < 16
    c1, s1 = cos_sin(t, HEAD_DIM)
    cos_b = jnp.tile(jnp.concatenate([c1, c1], axis=1), (1, 2))
    sin_b = jnp.tile(jnp.concatenate([s1, s1], axis=1), (1, 2))
    first_b = (lane % 64) < 32
    tabs_a = (cos_a, jnp.where(first_a, -sin_a, 0.0), jnp.where(first_a, 0.0, sin_a))
    tabs_b = (cos_b, jnp.where(first_b, -sin_b, 0.0), jnp.where(first_b, 0.0, sin_b))
    return tabs_a + tabs_b


def _rope(z, cos, s_lo, s_hi, half, sign=1.0):
    up = pltpu.roll(z, LANES - half, 1)
    dn = pltpu.roll(z, half, 1)
    return z * cos + sign * (up * s_lo + dn * s_hi)


def _head_mean(z2, bd):
    hi = z2.astype(BF16)
    lo = (z2 - hi.astype(F32)).astype(BF16)
    return _nn(hi, bd) + _nn(lo, bd)


def _block_diag():
    lane = jnp.arange(LANES)
    return jnp.where((lane[:, None] // HEAD_DIM) == (lane[None, :] // HEAD_DIM), 1.0 / HEAD_DIM, 0.0).astype(BF16)


def _row_spec(tm, width):
    return pl.BlockSpec((tm, width), lambda i: (i, 0))


def _heads_spec(heads, tm):
    return pl.BlockSpec((heads, tm, HEAD_DIM), lambda i: (0, i, 0))


def _full_spec(shape):
    nd = len(shape)
    return pl.BlockSpec(shape, lambda i: (0,) * nd)


def _in_proj(x, mod6, g1, w_in_s, gq, gk, bd, tabs, tm=256):
    s = x.shape[0]

    def body(x_ref, mod_ref, g1_ref, w_ref, gq_ref, gk_ref, bd_ref, ca, la, ha, cb, lb, hb,
             h_ref, qkraw_ref, qa_ref, ka_ref, va_ref, qb_ref, kb_ref, vb_ref, gate_ref):
        xt = x_ref[...]
        r = lax.rsqrt(jnp.mean(xt * xt, axis=-1, keepdims=True) + NORM_EPS)
        h = (xt * r * g1_ref[...]) * (1.0 + mod_ref[1:2, :]) + mod_ref[0:1, :]
        hb16 = h.astype(BF16)
        h_ref[...] = hb16
        proj = jnp.concatenate([_nn(hb16, w_ref[j]) for j in range(N_SHARD)], axis=1)
        qkraw_ref[...] = proj[:, :BRANCH_W + KV_W]
        bdm = bd_ref[...]
        tab_a = (ca[...], la[...], ha[...])
        tab_b = (cb[...], lb[...], hb[...])

        def norm_rope_a(z, gain):
            zn = z * lax.rsqrt(_head_mean(z * z, bdm) + NORM_EPS) * gain
            return _rope(zn, *tab_a, 16)

        def put(ref, first, z):
            zb = z.astype(BF16)
            ref[first] = zb[:, :HEAD_DIM]
            ref[first + 1] = zb[:, HEAD_DIM:]

        for i in range(Q_HEADS // 2):
            put(qa_ref, 2 * i, norm_rope_a(proj[:, LANES * i:LANES * (i + 1)], gq_ref[...]) * Q_SCALE)
        off = BRANCH_W
        put(ka_ref, 0, norm_rope_a(proj[:, off:off + LANES], gk_ref[...]))
        off += KV_W
        def put_v(ref, z):
            zb = z.astype(BF16)
            for hd in range(KV_HEADS):
                ref[hd, :, :HEAD_DIM] = zb[:, HEAD_DIM * hd:HEAD_DIM * (hd + 1)]
                ref[hd, :, HEAD_DIM:] = jnp.ones((tm, HEAD_DIM), BF16)

        put_v(va_ref, proj[:, off:off + LANES])
        off += KV_W
        for i in range(Q_HEADS // 2):
            put(qb_ref, 2 * i, _rope(proj[:, off + LANES * i:off + LANES * (i + 1)], *tab_b, 32) * Q_SCALE)
        off += BRANCH_W
        put(kb_ref, 0, _rope(proj[:, off:off + LANES], *tab_b, 32))
        off += KV_W
        put_v(vb_ref, proj[:, off:off + LANES])
        gate_ref[...] = proj[:, QK_W:]

    tab_spec = _row_spec(tm, LANES)
    return _call(
        body, name="in_proj", grid=(s // tm,),
        in_specs=[_row_spec(tm, D_MODEL), _full_spec(mod6.shape), _full_spec(g1.shape), _full_spec(w_in_s.shape),
                  _full_spec(gq.shape), _full_spec(gk.shape), _full_spec(bd.shape)] + [tab_spec] * 6,
        out_specs=[_row_spec(tm, D_MODEL), _row_spec(tm, BRANCH_W + KV_W), _heads_spec(Q_HEADS, tm), _heads_spec(KV_HEADS, tm),
                   pl.BlockSpec((KV_HEADS, tm, LANES), lambda i: (0, i, 0)), _heads_spec(Q_HEADS, tm),
                   _heads_spec(KV_HEADS, tm), pl.BlockSpec((KV_HEADS, tm, LANES), lambda i: (0, i, 0)),
                   _row_spec(tm, 2 * D_MODEL)],
        out_shape=[jax.ShapeDtypeStruct((s, D_MODEL), BF16), jax.ShapeDtypeStruct((s, BRANCH_W + KV_W), F32),
                   jax.ShapeDtypeStruct((Q_HEADS, s, HEAD_DIM), BF16), jax.ShapeDtypeStruct((KV_HEADS, s, HEAD_DIM), BF16),
                   jax.ShapeDtypeStruct((KV_HEADS, s, LANES), BF16), jax.ShapeDtypeStruct((Q_HEADS, s, HEAD_DIM), BF16),
                   jax.ShapeDtypeStruct((KV_HEADS, s, HEAD_DIM), BF16), jax.ShapeDtypeStruct((KV_HEADS, s, LANES), BF16),
                   jax.ShapeDtypeStruct((s, 2 * D_MODEL), F32)],
        compiler_params=_params(("parallel",)),
    )(x, mod6, g1, w_in_s, gq, gk, bd, *tabs)


def _group_specs(s, tq):
    q_spec = pl.BlockSpec((None, GROUP, tq, HEAD_DIM), lambda g, i: (g, 0, i, 0))
    kv_spec = pl.BlockSpec((None, s, HEAD_DIM), lambda g, i: (g, 0, 0))
    col_spec = pl.BlockSpec((None, GROUP, tq, 1), lambda g, i: (g, 0, i, 0))
    return q_spec, kv_spec, col_spec


def _attn_a_fwd(q, k, v1, tq=256, tk=512):
    s = q.shape[1]
    tk = min(tk, s)
    rows = GROUP * tq

    n = s // tk
    assert n >= 2 and n % 2 == 0

    def body(q_ref, k_ref, v_ref, o_ref, oh_ref, lse_ref, s0_ref, s1_ref, p0_ref, p1_ref, m_ref, a_ref, acc_ref):
        s_ref, p_ref = (s0_ref, s1_ref), (p0_ref, p1_ref)
        qq = q_ref[...].reshape(rows, HEAD_DIM)
        m_ref[...] = jnp.full((rows, 1), NEG_INF, F32)
        acc_ref[...] = jnp.zeros((rows, LANES), F32)

        def keys(i):
            return pl.ds(pl.multiple_of(i * tk, tk), tk)

        def scores(i, slot):
            s_ref[slot][...] = _nt(qq, k_ref[keys(i), :])

        def softmax(slot):
            sc = s_ref[slot][...]
            m = m_ref[...]
            mn = jnp.maximum(m, jnp.max(sc, axis=-1, keepdims=True))
            m_ref[...] = mn
            a_ref[...] = jnp.exp(m - mn)
            p_ref[slot][...] = jnp.exp(sc - mn).astype(BF16)

        def weigh(i, slot):
            acc_ref[...] = a_ref[...] * acc_ref[...] + _nn(p_ref[slot][...], v_ref[keys(i), :])

        scores(0, 0)
        softmax(0)
        scores(1, 1)

        def two_steps(j, carry):
            i = 2 * j + 1
            weigh(i - 1, 0)
            softmax(1)
            scores(i + 1, 0)
            weigh(i, 1)
            softmax(0)
            scores(i + 2, 1)
            return carry

        lax.fori_loop(0, (n - 2) // 2, two_steps, 0)
        weigh(n - 2, 0)
        softmax(1)
        weigh(n - 1, 1)
        l = acc_ref[:, HEAD_DIM:HEAD_DIM + 1]
        o = (acc_ref[:, :HEAD_DIM] / l).astype(BF16)
        for g in range(GROUP):
            o_ref[:, HEAD_DIM * g:HEAD_DIM * (g + 1)] = o[tq * g:tq * (g + 1)]
        oh_ref[...] = o.reshape(GROUP, tq, HEAD_DIM)
        lse_ref[...] = (m_ref[...] + jnp.log(l)).reshape(GROUP, tq, 1)

    q_spec, kv_spec, col_spec = _group_specs(s, tq)
    v_spec = pl.BlockSpec((None, s, LANES), lambda g, i: (g, 0, 0))
    return _call(
        body, name="attn_a_fwd", grid=(KV_HEADS, s // tq),
        in_specs=[q_spec, kv_spec, v_spec],
        out_specs=[pl.BlockSpec((tq, GROUP * HEAD_DIM), lambda g, i: (i, g)), q_spec, col_spec],
        out_shape=[jax.ShapeDtypeStruct((s, BRANCH_W), BF16), jax.ShapeDtypeStruct((KV_HEADS, GROUP, s, HEAD_DIM), BF16),
                   jax.ShapeDtypeStruct((KV_HEADS, GROUP, s, 1), F32)],
        scratch_shapes=[pltpu.VMEM((rows, tk), F32), pltpu.VMEM((rows, tk), F32), pltpu.VMEM((rows, tk), BF16),
                        pltpu.VMEM((rows, tk), BF16), pltpu.VMEM((rows, 1), F32), pltpu.VMEM((rows, 1), F32),
                        pltpu.VMEM((rows, LANES), F32)],
        compiler_params=_params(("parallel", "parallel")),
    )(q.reshape(KV_HEADS, GROUP, s, HEAD_DIM), k, v1)


def _attn_a_bwd(q, k, v1, o, do, lse, tq=256, tk=512):
    v = v1
    s = q.shape[1]
    tk = min(tk, s)
    rows = GROUP * tq

    n = s // tk
    assert n >= 2 and n % 2 == 0

    def body(q_ref, k_ref, v_ref, o_ref, do_ref, lse_ref, dq_ref, dk_ref, dv_ref,
             s0_ref, s1_ref, dp0_ref, dp1_ref, p0_ref, p1_ref, ds0_ref, ds1_ref, dq_acc):
        s_ref, dp_ref, p_ref, ds_ref = (s0_ref, s1_ref), (dp0_ref, dp1_ref), (p0_ref, p1_ref), (ds0_ref, ds1_ref)

        @pl.when(pl.program_id(1) == 0)
        def _():
            dk_ref[...] = jnp.zeros_like(dk_ref)
            dv_ref[...] = jnp.zeros_like(dv_ref)

        qq = q_ref[...].reshape(rows, HEAD_DIM)
        dd = do_ref[...].reshape(rows, HEAD_DIM)
        ls = lse_ref[...].reshape(rows, 1)
        dl = jnp.sum(dd.astype(F32) * o_ref[...].reshape(rows, HEAD_DIM).astype(F32), axis=-1, keepdims=True)
        dq_acc[...] = jnp.zeros((rows, HEAD_DIM), F32)

        def keys(i):
            return pl.ds(pl.multiple_of(i * tk, tk), tk)

        def scores(i, slot):
            s_ref[slot][...] = _nt(qq, k_ref[keys(i), :])
            dp_ref[slot][...] = _nt(dd, v_ref[keys(i), :HEAD_DIM])

        def weights(slot):
            p = jnp.exp(s_ref[slot][...] - ls)
            p_ref[slot][...] = p.astype(BF16)
            ds_ref[slot][...] = (p * (dp_ref[slot][...] - dl)).astype(BF16)

        def grads(i, slot):
            dv_ref[keys(i), :] += _tn(p_ref[slot][...], dd)
            dk_ref[keys(i), :] += _tn(ds_ref[slot][...], qq)
            dq_acc[...] += _nn(ds_ref[slot][...], k_ref[keys(i), :])

        scores(0, 0)
        weights(0)
        scores(1, 1)

        def two_steps(j, carry):
            i = 2 * j + 1
            grads(i - 1, 0)
            weights(1)
            scores(i + 1, 0)
            grads(i, 1)
            weights(0)
            scores(i + 2, 1)
            return carry

        lax.fori_loop(0, (n - 2) // 2, two_steps, 0, unroll=True)
        grads(n - 2, 0)
        weights(1)
        grads(n - 1, 1)
        dq_ref[...] = dq_acc[...].reshape(GROUP, tq, HEAD_DIM)

    q_spec, kv_spec, col_spec = _group_specs(s, tq)
    v_spec = pl.BlockSpec((None, s, LANES), lambda g, i: (g, 0, 0))
    shape4 = (KV_HEADS, GROUP, s, HEAD_DIM)
    tile32, tile16 = pltpu.VMEM((rows, tk), F32), pltpu.VMEM((rows, tk), BF16)
    return _call(
        body, name="attn_a_bwd", grid=(KV_HEADS, s // tq),
        in_specs=[q_spec, kv_spec, v_spec, q_spec, q_spec, col_spec],
        out_specs=[q_spec, kv_spec, kv_spec],
        out_shape=[jax.ShapeDtypeStruct(shape4, F32), jax.ShapeDtypeStruct((KV_HEADS, s, HEAD_DIM), F32),
                   jax.ShapeDtypeStruct((KV_HEADS, s, HEAD_DIM), F32)],
        scratch_shapes=[tile32] * 4 + [tile16] * 4 + [pltpu.VMEM((rows, HEAD_DIM), F32)],
        compiler_params=_params(("parallel", "arbitrary")),
    )(q.reshape(shape4), k, v, o.reshape(shape4), do.reshape(shape4), lse)


WIN_KEYS = 4 * WINDOW


def _window_start(i, tq, s):
    return pl.multiple_of(jnp.clip(i * tq - WINDOW, 0, s - WIN_KEYS), WINDOW)


def _window_bias(tq):
    r = jnp.arange(tq, dtype=jnp.int32)[:, None]
    col = jnp.arange(WIN_KEYS, dtype=jnp.int32)[None, :]
    return jnp.stack([jnp.where(jnp.abs(r - col + WINDOW * b) <= WINDOW, 0.0, NEG_INF) for b in range(3)]).astype(F32)


def _bias_spec(tq, nq):
    return pl.BlockSpec((None, tq, WIN_KEYS), lambda g, i: (jnp.where(i == 0, 0, jnp.where(i == nq - 1, 2, 1)), 0, 0))


def _masked(sc, bias_ref, tq):
    return (sc.reshape(GROUP, tq, WIN_KEYS) + bias_ref[...][None]).reshape(GROUP * tq, WIN_KEYS)


def _attn_b_fwd(q, k, v1, sink_col, bias, rider=None, tq=2 * WINDOW):
    s = q.shape[1]
    rows = GROUP * tq

    def body(q_ref, k_ref, v_ref, sink_ref, bias_ref, o_ref, oh_ref, lse_ref):
        at = pl.ds(_window_start(pl.program_id(1), tq, s), WIN_KEYS)
        qq = q_ref[...].reshape(rows, HEAD_DIM)
        sc = _masked(_nt(qq, k_ref[at, :]), bias_ref, tq)
        sk = sink_ref[...]
        m = jnp.maximum(jnp.max(sc, axis=-1, keepdims=True), sk)
        acc = _nn(jnp.exp(sc - m).astype(BF16), v_ref[at, :])
        l = acc[:, HEAD_DIM:HEAD_DIM + 1] + jnp.exp(sk - m)
        o = (acc[:, :HEAD_DIM] / l).astype(BF16)
        for g in range(GROUP):
            o_ref[:, HEAD_DIM * g:HEAD_DIM * (g + 1)] = o[tq * g:tq * (g + 1)]
        oh_ref[...] = o.reshape(GROUP, tq, HEAD_DIM)
        lse_ref[...] = (m + jnp.log(l)).reshape(GROUP, tq, 1)

    q_spec, kv_spec, col_spec = _group_specs(s, tq)
    v_spec = pl.BlockSpec((None, s, LANES), lambda g, i: (g, 0, 0))
    sink_spec = pl.BlockSpec((None, rows, 1), lambda g, i: (g, 0, 0))
    steps = KV_HEADS * (s // tq)
    return _hosted(
        body, rider, name="attn_b_fwd", grid=(KV_HEADS, s // tq),
        in_specs=[q_spec, kv_spec, v_spec, sink_spec, _bias_spec(tq, s // tq)],
        out_specs=[pl.BlockSpec((tq, GROUP * HEAD_DIM), lambda g, i: (i, g)), q_spec, col_spec],
        out_shape=[jax.ShapeDtypeStruct((s, BRANCH_W), BF16), jax.ShapeDtypeStruct((KV_HEADS, GROUP, s, HEAD_DIM), BF16),
                   jax.ShapeDtypeStruct((KV_HEADS, GROUP, s, 1), F32)],
        args=(q.reshape(KV_HEADS, GROUP, s, HEAD_DIM), k, v1, sink_col, bias), middle_at=3 * steps // 4)


def _attn_b_bwd(q, k, v1, o, do, lse, sink_col, bias, rider=None, tq=2 * WINDOW):
    s = q.shape[1]
    rows = GROUP * tq

    def body(q_ref, k_ref, v_ref, o_ref, do_ref, lse_ref, sink_ref, bias_ref, dq_ref, dk_ref, dv_ref, dsink_ref):
        i = pl.program_id(1)

        @pl.when(i == 0)
        def _():
            dk_ref[...] = jnp.zeros_like(dk_ref)
            dv_ref[...] = jnp.zeros_like(dv_ref)
            dsink_ref[...] = jnp.zeros_like(dsink_ref)

        start = _window_start(i, tq, s)
        at = pl.ds(start, WIN_KEYS)
        qq = q_ref[...].reshape(rows, HEAD_DIM)
        dd = do_ref[...].reshape(rows, HEAD_DIM)
        ls = lse_ref[...].reshape(rows, 1)
        dl = jnp.sum(dd.astype(F32) * o_ref[...].reshape(rows, HEAD_DIM).astype(F32), axis=-1, keepdims=True)
        kk = k_ref[at, :]
        p = jnp.exp(_masked(_nt(qq, kk), bias_ref, tq) - ls)
        ds = (p * (_nt(dd, v_ref[at, :HEAD_DIM]) - dl)).astype(BF16)
        dv_ref[at, :] += _tn(p.astype(BF16), dd)
        dk_ref[at, :] += _tn(ds, qq)
        dq_ref[...] = _nn(ds, kk).reshape(GROUP, tq, HEAD_DIM)
        dsk = jnp.exp(sink_ref[...] - ls) * dl
        for g in range(GROUP):
            dsink_ref[g:g + 1, :] -= jnp.broadcast_to(jnp.sum(dsk[tq * g:tq * (g + 1)], axis=0, keepdims=True), (1, LANES))

    q_spec, kv_spec, col_spec = _group_specs(s, tq)
    sink_spec = pl.BlockSpec((None, rows, 1), lambda g, i: (g, 0, 0))
    dsink_spec = pl.BlockSpec((None, ACC_ROWS, LANES), lambda g, i: (g, 0, 0))
    shape4 = (KV_HEADS, GROUP, s, HEAD_DIM)
    v_spec = pl.BlockSpec((None, s, LANES), lambda g, i: (g, 0, 0))
    return _hosted(
        body, rider, name="attn_b_bwd", grid=(KV_HEADS, s // tq),
        in_specs=[q_spec, kv_spec, v_spec, q_spec, q_spec, col_spec, sink_spec, _bias_spec(tq, s // tq)],
        out_specs=[q_spec, kv_spec, kv_spec, dsink_spec],
        out_shape=[jax.ShapeDtypeStruct(shape4, F32), jax.ShapeDtypeStruct((KV_HEADS, s, HEAD_DIM), F32),
                   jax.ShapeDtypeStruct((KV_HEADS, s, HEAD_DIM), F32), jax.ShapeDtypeStruct((KV_HEADS, ACC_ROWS, LANES), F32)],
        args=(q.reshape(shape4), k, v1, o.reshape(shape4), do.reshape(shape4), lse, sink_col, bias))


def _post_attn(ya, yb, gates, x, mod6, wbr_s, w_out, tm=256):
    s = x.shape[0]

    def body(ya_ref, yb_ref, g_ref, x_ref, mod_ref, wbr_ref, wo_ref, ua_ref, ub_ref, mg_ref, o_ref, x1_ref):
        ya_t, yb_t = ya_ref[...], yb_ref[...]
        ua = jnp.concatenate([_nn(ya_t, wbr_ref[j, 0]) for j in range(N_SHARD)], axis=1)
        ub = jnp.concatenate([_nn(yb_t, wbr_ref[j, 1]) for j in range(N_SHARD)], axis=1)
        merged = (_sigmoid(g_ref[:, :D_MODEL]) * ua + _sigmoid(g_ref[:, D_MODEL:]) * ub).astype(BF16)
        o = _nn(merged, wo_ref[...])
        ua_ref[...] = ua.astype(BF16)
        ub_ref[...] = ub.astype(BF16)
        mg_ref[...] = merged
        o_ref[...] = o.astype(BF16)
        x1_ref[...] = x_ref[...] + mod_ref[2:3, :] * o

    bf = jax.ShapeDtypeStruct((s, D_MODEL), BF16)
    return _call(
        body, name="post_attn", grid=(s // tm,),
        in_specs=[_row_spec(tm, BRANCH_W), _row_spec(tm, BRANCH_W), _row_spec(tm, 2 * D_MODEL), _row_spec(tm, D_MODEL),
                  _full_spec(mod6.shape), _full_spec(wbr_s.shape), _full_spec(w_out.shape)],
        out_specs=[_row_spec(tm, D_MODEL)] * 5,
        out_shape=[bf, bf, bf, bf, jax.ShapeDtypeStruct((s, D_MODEL), F32)],
        compiler_params=_params(("parallel",)),
    )(ya, yb, gates, x, mod6, wbr_s, w_out)


def _mlp_in(x1, mod6, g2, w_mi_s, tm=256):
    s = x1.shape[0]

    def body(x_ref, mod_ref, g_ref, w_ref, h2_ref, a_ref, hid_ref):
        xt = x_ref[...]
        r = lax.rsqrt(jnp.mean(xt * xt, axis=-1, keepdims=True) + NORM_EPS)
        h2 = ((xt * r * g_ref[...]) * (1.0 + mod_ref[4:5, :]) + mod_ref[3:4, :]).astype(BF16)
        h2_ref[...] = h2
        a = jnp.concatenate([_nn(h2, w_ref[j]) for j in range(N_SHARD)], axis=1)
        a_ref[...] = a.astype(BF16)
        hid_ref[...] = jnp.square(jnp.maximum(a, 0.0)).astype(BF16)

    return _call(
        body, name="mlp_in", grid=(s // tm,),
        in_specs=[_row_spec(tm, D_MODEL), _full_spec(mod6.shape), _full_spec(g2.shape), _full_spec(w_mi_s.shape)],
        out_specs=[_row_spec(tm, D_MODEL), _row_spec(tm, D_FF), _row_spec(tm, D_FF)],
        out_shape=[jax.ShapeDtypeStruct((s, D_MODEL), BF16), jax.ShapeDtypeStruct((s, D_FF), BF16),
                   jax.ShapeDtypeStruct((s, D_FF), BF16)],
        compiler_params=_params(("parallel",)),
    )(x1, mod6, g2, w_mi_s)


ACC_ROWS = 8


def _acc_spec():
    return pl.BlockSpec((ACC_ROWS, D_MODEL), lambda i: (0, 0))


def _acc_add(acc_ref, rows):
    @pl.when(pl.program_id(0) == 0)
    def _():
        acc_ref[...] = jnp.zeros_like(acc_ref)

    for r, val in enumerate(rows):
        acc_ref[r:r + 1, :] += jnp.sum(val, axis=0, keepdims=True)


def _mlp_out_loss(hid, x1, a, target, mod6, gf, w_mo, tm=256):
    s = x1.shape[0]

    def body(hid_ref, x_ref, a_ref, t_ref, mod_ref, gf_ref, w_ref, dx2_ref, dm_ref, da_ref, acc_ref):
        m = _nn(hid_ref[...], w_ref[...])
        gate2 = mod_ref[5:6, :]
        x2 = x_ref[...] + gate2 * m
        r = lax.rsqrt(jnp.mean(x2 * x2, axis=-1, keepdims=True) + NORM_EPS)
        xn = x2 * r
        err = xn * gf_ref[...] - t_ref[...]
        dy = err * (1.0 / D_MODEL)
        dxn = dy * gf_ref[...]
        dx2 = r * (dxn - xn * jnp.mean(dxn * xn, axis=-1, keepdims=True))
        dx2_ref[...] = dx2
        dm = (dx2 * gate2).astype(BF16)
        dm_ref[...] = dm
        da_ref[...] = (_nt(dm, w_ref[...]) * (2.0 * jnp.maximum(a_ref[...].astype(F32), 0.0))).astype(BF16)
        _acc_add(acc_ref, [err * err, dy * xn, dx2 * m])

    return _call(
        body, name="mlp_out_loss", grid=(s // tm,),
        in_specs=[_row_spec(tm, D_FF), _row_spec(tm, D_MODEL), _row_spec(tm, D_FF), _row_spec(tm, D_MODEL),
                  _full_spec(mod6.shape), _full_spec(gf.shape), _full_spec(w_mo.shape)],
        out_specs=[_row_spec(tm, D_MODEL), _row_spec(tm, D_MODEL), _row_spec(tm, D_FF), _acc_spec()],
        out_shape=[jax.ShapeDtypeStruct((s, D_MODEL), F32), jax.ShapeDtypeStruct((s, D_MODEL), BF16),
                   jax.ShapeDtypeStruct((s, D_FF), BF16), jax.ShapeDtypeStruct((ACC_ROWS, D_MODEL), F32)],
        compiler_params=_params(("arbitrary",)),
    )(hid, x1, a, target, mod6, gf, w_mo)


def _norm_bwd(dh, xt, gain, scale):
    r = lax.rsqrt(jnp.mean(xt * xt, axis=-1, keepdims=True) + NORM_EPS)
    xn = xt * r
    dxn = dh * (gain * (1.0 + scale))
    dx = r * (dxn - xn * jnp.mean(dxn * xn, axis=-1, keepdims=True))
    return dx, [dh, dh * xn * gain, dh * xn * (1.0 + scale)]


def _mlp_bwd(da, x1, dx2, o, mod6, g2, w_mi_s, rider=None, tm=256):
    s = x1.shape[0]

    def body(da_ref, x_ref, dx2_ref, o_ref, mod_ref, g_ref, w_ref, dx1_ref, do_ref, acc_ref):
        dh2 = _nt(da_ref[:, :D_MODEL], w_ref[0])
        for j in range(1, N_SHARD):
            dh2 += _nt(da_ref[:, D_MODEL * j:D_MODEL * (j + 1)], w_ref[j])
        dx, sums = _norm_bwd(dh2, x_ref[...], g_ref[...], mod_ref[4:5, :])
        dx1 = dx2_ref[...] + dx
        dx1_ref[...] = dx1
        do_ref[...] = (dx1 * mod_ref[2:3, :]).astype(BF16)
        _acc_add(acc_ref, sums + [dx1 * o_ref[...].astype(F32)])

    return _hosted(
        body, rider, name="mlp_bwd", grid=(s // tm,),
        in_specs=[_row_spec(tm, D_FF), _row_spec(tm, D_MODEL), _row_spec(tm, D_MODEL), _row_spec(tm, D_MODEL),
                  _full_spec(mod6.shape), _full_spec(g2.shape), _full_spec(w_mi_s.shape)],
        out_specs=[_row_spec(tm, D_MODEL), _row_spec(tm, D_MODEL), _acc_spec()],
        out_shape=[jax.ShapeDtypeStruct((s, D_MODEL), F32), jax.ShapeDtypeStruct((s, D_MODEL), BF16),
                   jax.ShapeDtypeStruct((ACC_ROWS, D_MODEL), F32)],
        args=(da, x1, dx2, o, mod6, g2, w_mi_s))


def _merge_bwd(do, gates, ua, ub, w_out, wbr_s, rider=None, tm=256):
    s = do.shape[0]

    def body(do_ref, g_ref, ua_ref, ub_ref, wo_ref, wbr_ref, dua_ref, dub_ref, dg_ref, doa_ref, dob_ref):
        dmerged = _nt(do_ref[...], wo_ref[...])
        for b, (u_ref, du_ref, dy_ref) in enumerate(((ua_ref, dua_ref, doa_ref), (ub_ref, dub_ref, dob_ref))):
            sg = _sigmoid(g_ref[:, D_MODEL * b:D_MODEL * (b + 1)])
            du = (dmerged * sg).astype(BF16)
            du_ref[...] = du
            dg_ref[:, D_MODEL * b:D_MODEL * (b + 1)] = (dmerged * u_ref[...].astype(F32) * sg * (1.0 - sg)).astype(BF16)
            w = BRANCH_W // 2
            dy = _nt(du[:, :w], wbr_ref[0, b])
            for j in range(1, N_SHARD):
                dy += _nt(du[:, w * j:w * (j + 1)], wbr_ref[j, b])
            dyb = dy.astype(BF16)
            for h in range(Q_HEADS):
                dy_ref[h] = dyb[:, HEAD_DIM * h:HEAD_DIM * (h + 1)]

    bf = jax.ShapeDtypeStruct((s, D_MODEL), BF16)
    heads = jax.ShapeDtypeStruct((Q_HEADS, s, HEAD_DIM), BF16)
    return _hosted(
        body, rider, name="merge_bwd", grid=(s // tm,),
        in_specs=[_row_spec(tm, D_MODEL), _row_spec(tm, 2 * D_MODEL), _row_spec(tm, D_MODEL), _row_spec(tm, D_MODEL),
                  _full_spec(w_out.shape), _full_spec(wbr_s.shape)],
        out_specs=[_row_spec(tm, D_MODEL), _row_spec(tm, D_MODEL), _row_spec(tm, 2 * D_MODEL),
                   _heads_spec(Q_HEADS, tm), _heads_spec(Q_HEADS, tm)],
        out_shape=[bf, bf, jax.ShapeDtypeStruct((s, 2 * D_MODEL), BF16), heads, heads],
        args=(do, gates, ua, ub, w_out, wbr_s))


def _qk_bwd(dqa, dka, dva, dqb, dkb, dvb, qkraw, dgates, gq, gk, bd, tabs, rider=None, tm=256):
    s = qkraw.shape[0]

    def body(dqa_ref, dka_ref, dva_ref, dqb_ref, dkb_ref, dvb_ref, raw_ref, dg_ref, gq_ref, gk_ref, bd_ref,
             ca, la, ha, cb, lb, hb, dp_ref, acc_ref, pair_ref):
        bdm = bd_ref[...]
        tab_a = (ca[...], la[...], ha[...])
        tab_b = (cb[...], lb[...], hb[...])

        def pair(ref, first):
            pair_ref[:, :HEAD_DIM] = ref[first]
            pair_ref[:, HEAD_DIM:] = ref[first + 1]
            return pair_ref[...]

        def norm_rope_a_bwd(dz, raw, gain):
            dzn = _rope(dz, *tab_a, 16, sign=-1.0)
            rinv = lax.rsqrt(_head_mean(raw * raw, bdm) + NORM_EPS)
            zhat = raw * rinv
            dzhat = dzn * gain
            return rinv * (dzhat - zhat * _head_mean(dzhat * zhat, bdm)), dzn * zhat

        gq_rows = jnp.zeros((tm, LANES), F32)
        for i in range(Q_HEADS // 2):
            at = slice(LANES * i, LANES * (i + 1))
            draw, gsum = norm_rope_a_bwd(pair(dqa_ref, 2 * i) * Q_SCALE, raw_ref[:, at], gq_ref[...])
            dp_ref[:, at] = draw.astype(BF16)
            gq_rows += gsum
        off = BRANCH_W
        draw, gk_rows = norm_rope_a_bwd(pair(dka_ref, 0), raw_ref[:, off:off + LANES], gk_ref[...])
        dp_ref[:, off:off + LANES] = draw.astype(BF16)
        off += KV_W
        dp_ref[:, off:off + LANES] = pair(dva_ref, 0).astype(BF16)
        off += KV_W
        for i in range(Q_HEADS // 2):
            dz = _rope(pair(dqb_ref, 2 * i) * Q_SCALE, *tab_b, 32, sign=-1.0)
            dp_ref[:, off + LANES * i:off + LANES * (i + 1)] = dz.astype(BF16)
        off += BRANCH_W
        dp_ref[:, off:off + LANES] = _rope(pair(dkb_ref, 0), *tab_b, 32, sign=-1.0).astype(BF16)
        off += KV_W
        dp_ref[:, off:off + LANES] = pair(dvb_ref, 0).astype(BF16)
        dp_ref[:, QK_W:] = dg_ref[...]

        @pl.when(pl.program_id(0) == 0)
        def _():
            acc_ref[...] = jnp.zeros_like(acc_ref)

        acc_ref[0:1, :] += jnp.sum(gq_rows, axis=0, keepdims=True)
        acc_ref[1:2, :] += jnp.sum(gk_rows, axis=0, keepdims=True)

    tab_spec = _row_spec(tm, LANES)
    return _hosted(
        body, rider, name="qk_bwd", grid=(s // tm,),
        in_specs=[_heads_spec(Q_HEADS, tm), _heads_spec(KV_HEADS, tm), _heads_spec(KV_HEADS, tm),
                  _heads_spec(Q_HEADS, tm), _heads_spec(KV_HEADS, tm), _heads_spec(KV_HEADS, tm),
                  _row_spec(tm, BRANCH_W + KV_W), _row_spec(tm, 2 * D_MODEL),
                  _full_spec(gq.shape), _full_spec(gk.shape), _full_spec(bd.shape)] + [tab_spec] * 6,
        out_specs=[_row_spec(tm, IN_W), pl.BlockSpec((ACC_ROWS, LANES), lambda i: (0, 0))],
        out_shape=[jax.ShapeDtypeStruct((s, IN_W), BF16), jax.ShapeDtypeStruct((ACC_ROWS, LANES), F32)],
        scratch_shapes=[pltpu.VMEM((tm, LANES), F32)],
        args=(dqa, dka, dva, dqb, dkb, dvb, qkraw, dgates, gq, gk, bd, *tabs))


def _in_proj_bwd(dproj, x, dx1, mod6, g1, w_in_s, tm=256):
    s = x.shape[0]
    w = IN_W // N_SHARD

    def body(dp_ref, x_ref, dx1_ref, mod_ref, g_ref, w_ref, gx_ref, acc_ref):
        dh = _nt(dp_ref[:, :w], w_ref[0])
        for j in range(1, N_SHARD):
            dh += _nt(dp_ref[:, w * j:w * (j + 1)], w_ref[j])
        dx, sums = _norm_bwd(dh, x_ref[...], g_ref[...], mod_ref[1:2, :])
        gx_ref[...] = dx1_ref[...] + dx
        _acc_add(acc_ref, sums)

    return _call(
        body, name="in_proj_bwd", grid=(s // tm,),
        in_specs=[_row_spec(tm, IN_W), _row_spec(tm, D_MODEL), _row_spec(tm, D_MODEL),
                  _full_spec(mod6.shape), _full_spec(g1.shape), _full_spec(w_in_s.shape)],
        out_specs=[_row_spec(tm, D_MODEL), _acc_spec()],
        out_shape=[jax.ShapeDtypeStruct((s, D_MODEL), F32), jax.ShapeDtypeStruct((ACC_ROWS, D_MODEL), F32)],
        compiler_params=_params(("arbitrary",)),
    )(dproj, x, dx1, mod6, g1, w_in_s)


def _wgrad(name, a, b, out_shape, out_spec, tm, tn, tk=4096):
    s, m = a.shape
    n = b.shape[1]
    tk = min(tk, s)
    nk = s // tk

    def body(a_ref, b_ref, o_ref, acc_ref):
        k = pl.program_id(2)

        @pl.when(k == 0)
        def _():
            acc_ref[...] = jnp.zeros_like(acc_ref)

        acc_ref[...] += _tn(a_ref[...], b_ref[...])

        @pl.when(k == nk - 1)
        def _():
            o_ref[...] = acc_ref[...].reshape(o_ref.shape)

    return _call(
        body, name=name, grid=(m // tm, n // tn, nk),
        in_specs=[pl.BlockSpec((tk, tm), lambda i, j, k: (k, i)), pl.BlockSpec((tk, tn), lambda i, j, k: (k, j))],
        out_specs=out_spec, out_shape=jax.ShapeDtypeStruct(out_shape, F32),
        scratch_shapes=[pltpu.VMEM((tm, tn), F32)],
        compiler_params=_params(("parallel", "parallel", "arbitrary")),
    )(a, b)


def _wgrad_branch(ya, yb, dua, dub, tk=2048):
    s = ya.shape[0]
    tk = min(tk, s)
    nk = s // tk
    w = D_MODEL // N_SHARD

    def body(ya_ref, yb_ref, dua_ref, dub_ref, o_ref, acc_ref):
        b, k = pl.program_id(0), pl.program_id(2)

        @pl.when(k == 0)
        def _():
            acc_ref[...] = jnp.zeros_like(acc_ref)

        @pl.when(b == 0)
        def _():
            acc_ref[...] += _tn(ya_ref[...], dua_ref[...])

        @pl.when(b == 1)
        def _():
            acc_ref[...] += _tn(yb_ref[...], dub_ref[...])

        @pl.when(k == nk - 1)
        def _():
            o_ref[...] = acc_ref[...]

    y_spec = pl.BlockSpec((tk, BRANCH_W), lambda b, j, k: (k, 0))
    du_spec = pl.BlockSpec((tk, w), lambda b, j, k: (k, j))
    return _call(
        body, name="wgrad_branch", grid=(2, N_SHARD, nk),
        in_specs=[y_spec, y_spec, du_spec, du_spec],
        out_specs=pl.BlockSpec((None, None, BRANCH_W, w), lambda b, j, k: (j, b, 0, 0)),
        out_shape=jax.ShapeDtypeStruct((N_SHARD, 2, BRANCH_W, w), F32),
        scratch_shapes=[pltpu.VMEM((BRANCH_W, w), F32)],
        compiler_params=_params(("parallel", "parallel", "arbitrary")),
    )(ya, yb, dua, dub)


def _local_step(x, target, mod6, g1, g2, gf, gq2, gk2, sink, w_in_s, rest, cj=None):
    s = x.shape[0]
    dist = cj is not None
    tabs = _rope_tables(s)
    bd = _block_diag()
    tq_b = 2 * WINDOW
    sink_col = jnp.repeat(sink.reshape(KV_HEADS, GROUP, 1), tq_b, axis=1).reshape(KV_HEADS, GROUP * tq_b, 1)
    shard = D_MODEL // N_SHARD

    h, qkraw, qa, ka, va, qb, kb, vb, gates = _in_proj(x, mod6, g1, w_in_s, gq2, gk2, bd, tabs)
    bias = _window_bias(tq_b)
    (yb, yb_heads, lse_b), gathered = _attn_b_fwd(qb, kb, vb, sink_col, bias, rider=_gather_rider(rest) if dist else None)
    wbr_s, w_out, w_mi_s, w_mo = gathered if dist else rest
    wbr_s = wbr_s.reshape(N_SHARD, 2, BRANCH_W, shard)
    w_out = w_out.reshape(D_MODEL, D_MODEL)
    w_mo = w_mo.reshape(D_FF, D_MODEL)
    ya, ya_heads, lse_a = _attn_a_fwd(qa, ka, va)
    ua, ub, merged, o, x1 = _post_attn(ya, yb, gates, x, mod6, wbr_s, w_out)
    h2, a, hid = _mlp_in(x1, mod6, g2, w_mi_s)
    dx2, dm, da, acc_out = _mlp_out_loss(hid, x1, a, target, mod6, gf, w_mo)

    g_w_mo = _wgrad("wgrad_mlp_out", hid, dm, (D_FF, D_MODEL), pl.BlockSpec((D_MODEL, D_MODEL), lambda i, j, k: (i, 0)),
                    D_MODEL, D_MODEL).reshape(N_SHARD, D_MODEL, D_MODEL)
    g_w_mi = _wgrad("wgrad_mlp_in", h2, da, (N_SHARD, D_MODEL, D_MODEL),
                    pl.BlockSpec((None, D_MODEL, D_MODEL), lambda i, j, k: (j, i, 0)), D_MODEL, D_MODEL)
    mlp = _Reduction(("mlp_out", "mlp_in"), (g_w_mo, g_w_mi), cj)
    (dx1, do, acc_mlp), got = _mlp_bwd(da, x1, dx2, o, mod6, g2, w_mi_s, rider=mlp.swap() if dist else None)
    (dua, dub, dgates, doa, dob), landed = _merge_bwd(do, gates, ua, ub, w_out, wbr_s, rider=mlp.add(got) if dist else None)
    g_w_out = _wgrad("wgrad_out", merged, do, (D_MODEL, D_MODEL), pl.BlockSpec((D_MODEL, D_MODEL), lambda i, j, k: (i, 0)),
                     D_MODEL, D_MODEL).reshape(N_SHARD, shard, D_MODEL)
    g_wbr = _wgrad_branch(ya, yb, dua, dub)
    out = _Reduction(("out", "branch"), (g_w_out, g_wbr.reshape(N_SHARD, 2 * BRANCH_W, shard)), cj)
    (dqb, dkb, dvb, dsink), landings = _attn_b_bwd(qb, kb, vb, yb_heads, dob, lse_b, sink_col, bias,
                                                   rider=_riders(out.swap(), mlp.total(landed)) if dist else None)
    dqa, dka, dva = _attn_a_bwd(qa, ka, va, ya_heads, doa, lse_a)
    heads = (Q_HEADS, s, HEAD_DIM)
    (dproj, acc_qk), landed = _qk_bwd(dqa.reshape(heads), dka, dva, dqb.reshape(heads), dkb, dvb, qkraw, dgates, gq2, gk2, bd,
                                      tabs, rider=out.add(landings[:2]) if dist else None)
    w = IN_W // N_SHARD
    g_w_in = _wgrad("wgrad_in", h, dproj, (N_SHARD, D_MODEL, w), pl.BlockSpec((None, D_MODEL, w), lambda i, j, k: (j, i, 0)),
                    D_MODEL, w)
    grad_x, acc_in = _in_proj_bwd(dproj, x, dx1, mod6, g1, w_in_s)
    accs = (acc_out, acc_mlp, acc_in, acc_qk, dsink)
    if not dist:
        return grad_x, (g_w_in, g_wbr, g_w_out, g_w_mi, g_w_mo), accs
    first = _Reduction(("in",), (g_w_in,), cj)
    landed_in = _alone("scatter_in", first.add(_alone("swap_in", first.swap())))
    r_in, r_out, r_br = _alone("join_in_out_branch", _riders(first.total(landed_in), out.total(landed)))
    r_mo, r_mi = landings[2:]
    return grad_x, (r_in, r_br, r_out, r_mi, r_mo), accs


def _me():
    return lax.axis_index("x"), lax.axis_index("y"), lax.axis_index("c")


def _peer(d):
    x, y, c = _me()
    return (1 - x if d & 4 else x, 1 - y if d & 2 else y, 1 - c if d & 1 else c)


def _dev_index(p):
    return 4 * p[0] + 2 * p[1] + p[2]


def _chip_index(p):
    return 2 * p[0] + p[1]


def _remote(src, dst, send_sem, recv_sem, to):
    return pltpu.make_async_remote_copy(src_ref=src, dst_ref=dst, send_sem=send_sem, recv_sem=recv_sem,
                                        device_id=to, device_id_type=MESH)


SLOT_ROWS = 8


def _ada_fwd(c, w_ada, b4):
    cols = w_ada.shape[1]

    def body(c_ref, w_ref, b_ref, mod_ref, sc_ref, cbuf, pbuf, mbuf, send1, recv1, send2, recv2):
        me = _me()
        mine, chip = _dev_index(me), _chip_index(me)
        cbuf[mine] = jnp.broadcast_to(c_ref[...], (SLOT_ROWS, D_MODEL))
        gather = [_remote(cbuf.at[mine], cbuf.at[mine], send1.at[d - 1], recv1.at[d - 1], _peer(d)) for d in range(1, N_DEV)]
        for cp in gather:
            cp.start()
        for d in range(1, N_DEV):
            _remote(cbuf.at[mine], cbuf.at[_dev_index(_peer(d))], send1.at[d - 1], recv1.at[d - 1], _peer(d)).wait_recv()
        call = cbuf[...].reshape(N_DEV * SLOT_ROWS, D_MODEL)
        sc = call * _sigmoid(call)
        for s in range(N_DEV):
            sc_ref[s:s + 1, :] = sc[SLOT_ROWS * s:SLOT_ROWS * s + 1]
        part = _nn(sc.astype(BF16), w_ref[...].astype(BF16)) + b_ref[pl.ds(chip, 1), :]
        pbuf[...] = part.reshape(N_DEV, SLOT_ROWS, cols)
        mbuf[chip] = pbuf[mine]
        spread = [_remote(pbuf.at[_dev_index(_peer(d))], mbuf.at[chip], send2.at[d // 2 - 1], recv2.at[d // 2 - 1], _peer(d))
                  for d in (2, 4, 6)]
        for cp in spread:
            cp.start()
        for d in (2, 4, 6):
            _remote(pbuf.at[mine], mbuf.at[_chip_index(_peer(d))], send2.at[d // 2 - 1], recv2.at[d // 2 - 1],
                    _peer(d)).wait_recv()
        half = D_MODEL // 2
        for p in range(2 * 6):
            col = half * p
            mod_ref[p // 2:p // 2 + 1, half * (p % 2):half * (p % 2 + 1)] = mbuf[col // cols, 0:1, col % cols:col % cols + half]
        for cp in gather + spread:
            cp.wait_send()

    vm = pl.BlockSpec(memory_space=pltpu.VMEM)
    return _call(
        body, name="ada_fwd", in_specs=[vm, vm, vm], out_specs=[vm, vm],
        out_shape=[jax.ShapeDtypeStruct((6, D_MODEL), F32), jax.ShapeDtypeStruct((N_DEV, D_MODEL), F32)],
        scratch_shapes=[pltpu.VMEM((N_DEV, SLOT_ROWS, D_MODEL), F32), pltpu.VMEM((N_DEV, SLOT_ROWS, cols), F32),
                        pltpu.VMEM((N_SHARD, SLOT_ROWS, cols), F32),
                        pltpu.SemaphoreType.DMA((N_DEV - 1,)), pltpu.SemaphoreType.DMA((N_DEV - 1,)),
                        pltpu.SemaphoreType.DMA((N_SHARD - 1,)), pltpu.SemaphoreType.DMA((N_SHARD - 1,))],
        compiler_params=_params(),
    )(c, w_ada, b4)


PACK_ROWS = 16
PACK_W = 3 * D_MODEL


def _ada_bwd(acc_out, acc_mlp, acc_in, acc_qk, dsink, sc_all):
    cols = 6 * D_MODEL // N_SHARD

    def body(out_ref, mlp_ref, in_ref, qk_ref, dsink_ref, sc_ref,
             gwa_ref, gba_ref, gn1_ref, gn2_ref, gf_ref, gq_ref, gk_ref, gs_ref, blk, send, recv):
        me = _me()
        mine, chip = _dev_index(me), _chip_index(me)
        blk[mine] = jnp.zeros((PACK_ROWS, PACK_W), F32)
        dmod = (in_ref, 0), (in_ref, 1), (mlp_ref, 3), (mlp_ref, 0), (mlp_ref, 1), (out_ref, 2)
        half = D_MODEL // 2
        for p in range(2 * 6):
            ref, row = dmod[p // 2]
            col = half * p
            blk[mine, col // cols:col // cols + 1, col % cols:col % cols + half] = ref[row:row + 1, half * (p % 2):half * (p % 2 + 1)]
        blk[mine, 4:5, 0:D_MODEL] = in_ref[2:3, :]
        blk[mine, 4:5, D_MODEL:2 * D_MODEL] = mlp_ref[2:3, :]
        blk[mine, 4:5, 2 * D_MODEL:] = out_ref[1:2, :]
        blk[mine, 5:6, 0:LANES] = qk_ref[0:1, :]
        blk[mine, 5:6, LANES:2 * LANES] = qk_ref[1:2, :]
        for g in range(KV_HEADS):
            blk[mine, 8 + GROUP * g:8 + GROUP * (g + 1), 0:LANES] = dsink_ref[g, 0:GROUP, :]
        copies = [_remote(blk.at[mine], blk.at[mine], send.at[d - 1], recv.at[d - 1], _peer(d)) for d in range(1, N_DEV)]
        for cp in copies:
            cp.start()
        for d in range(1, N_DEV):
            _remote(blk.at[mine], blk.at[_dev_index(_peer(d))], send.at[d - 1], recv.at[d - 1], _peer(d)).wait_recv()
        tot = blk[0]
        for s in range(1, N_DEV):
            tot = tot + blk[s]
        for j in range(N_SHARD):
            gba_ref[:, cols * j:cols * (j + 1)] = tot[j:j + 1, :cols]
        gn1_ref[...] = tot[4:5, 0:D_MODEL]
        gn2_ref[...] = tot[4:5, D_MODEL:2 * D_MODEL]
        gf_ref[...] = tot[4:5, 2 * D_MODEL:]
        gq_ref[...] = tot[5:6, 0:HEAD_DIM] + tot[5:6, HEAD_DIM:2 * HEAD_DIM]
        gk_ref[...] = tot[5:6, LANES:LANES + HEAD_DIM] + tot[5:6, LANES + HEAD_DIM:2 * LANES]
        sq = tot[8:16, 0:Q_HEADS]
        diag = lax.broadcasted_iota(jnp.int32, sq.shape, 0) == lax.broadcasted_iota(jnp.int32, sq.shape, 1)
        gs_ref[...] = jnp.sum(jnp.where(diag, sq, 0.0), axis=0, keepdims=True)
        dm = jnp.concatenate([blk[s, pl.ds(chip, 1), pl.ds(0, cols)] for s in range(N_DEV)], axis=0)
        gwa_ref[...] = _tn(sc_ref[...], dm)
        for cp in copies:
            cp.wait_send()

    vm = pl.BlockSpec(memory_space=pltpu.VMEM)
    row = lambda n: jax.ShapeDtypeStruct((1, n), F32)
    return _call(
        body, name="ada_bwd", in_specs=[vm] * 6, out_specs=[vm] * 8,
        out_shape=[jax.ShapeDtypeStruct((D_MODEL, cols), F32), row(6 * D_MODEL), row(D_MODEL), row(D_MODEL), row(D_MODEL),
                   row(HEAD_DIM), row(HEAD_DIM), row(Q_HEADS)],
        scratch_shapes=[pltpu.VMEM((N_DEV, PACK_ROWS, PACK_W), F32),
                        pltpu.SemaphoreType.DMA((N_DEV - 1,)), pltpu.SemaphoreType.DMA((N_DEV - 1,))],
        compiler_params=_params(),
    )(acc_out, acc_mlp, acc_in, acc_qk, dsink, sc_all)


def _cast_weights(ws):
    n = len(ws)

    def body(*refs):
        src, out, tmp, sems = refs[:n], refs[n:2 * n], refs[2 * n:3 * n], refs[3 * n]
        chip = _chip_index(_me())
        copies = []
        for a in range(n):
            tmp[a][...] = src[a][...].astype(BF16)
            cp = pltpu.make_async_copy(tmp[a], out[a].at[chip], sems.at[a])
            cp.start()
            copies.append(cp)
        for cp in copies:
            cp.wait()

    vm = pl.BlockSpec(memory_space=pltpu.VMEM)
    return _call(
        body, name="cast_weights", in_specs=[vm] * n, out_specs=[ANY] * n,
        out_shape=[jax.ShapeDtypeStruct((N_SHARD,) + w.shape, BF16) for w in ws],
        scratch_shapes=[pltpu.VMEM(w.shape, BF16) for w in ws] + [pltpu.SemaphoreType.DMA((n,))],
        compiler_params=_params(),
    )(*ws)


def _half_rows(ref_rows, c):
    half = ref_rows // 2
    return pl.ds(pl.multiple_of(c * half, 8), half)


class _Rider:
    def __init__(self, inputs, out_shape, aliases, n_sems, start, finish, middle=None):
        self.inputs, self.out_shape, self.aliases, self.n_sems = list(inputs), list(out_shape), dict(aliases), n_sems
        self.start, self.finish, self.middle = start, finish, middle


def _riders(*rs):
    ins = [0]
    outs = [0]
    sems = [0]
    for r in rs:
        ins.append(ins[-1] + len(r.inputs))
        outs.append(outs[-1] + len(r.out_shape))
        sems.append(sems[-1] + r.n_sems)

    def phase(which):
        def run(in_refs, out_refs, sem):
            for k, r in enumerate(rs):
                fn = getattr(r, which)
                if fn is not None:
                    fn(in_refs[ins[k]:ins[k + 1]], out_refs[outs[k]:outs[k + 1]], lambda j, base=sems[k]: sem(base + j))
        return run

    aliases = {ins[k] + i: outs[k] + o for k, r in enumerate(rs) for i, o in r.aliases.items()}
    return _Rider([a for r in rs for a in r.inputs], [o for r in rs for o in r.out_shape], aliases, sems[-1],
                  phase("start"), phase("finish"), phase("middle") if any(r.middle for r in rs) else None)


def _hosted(body, rider, *, name, grid, in_specs, out_specs, out_shape, args, scratch_shapes=(), middle_at=None):
    sem = ("arbitrary",) * len(grid)
    if rider is None:
        res = _call(body, name=name, grid=grid, in_specs=in_specs, out_specs=out_specs, out_shape=out_shape,
                    scratch_shapes=list(scratch_shapes), compiler_params=_params(sem))(*args)
        return res, ()
    n_in, n_out, n_scr = len(in_specs), len(out_specs), len(scratch_shapes)
    r_in, r_out = len(rider.inputs), len(rider.out_shape)

    def riding(*refs):
        at = 0
        parts = []
        for size in (n_in, r_in, n_out, r_out, n_scr):
            parts.append(refs[at:at + size])
            at += size
        ins, rider_ins, outs, rider_outs, scratch = parts
        sems = refs[at]
        step = pl.program_id(0)
        for axis in range(1, len(grid)):
            step = step * grid[axis] + pl.program_id(axis)
        steps = 1
        for size in grid:
            steps *= size

        def sem_at(k):
            return sems.at[k]

        @pl.when(step == 0)
        def _():
            rider.start(rider_ins, rider_outs, sem_at)

        body(*ins, *outs, *scratch)
        if rider.middle is not None:
            @pl.when(step == middle_at)
            def _():
                rider.middle(rider_ins, rider_outs, sem_at)

        @pl.when(step == steps - 1)
        def _():
            rider.finish(rider_ins, rider_outs, sem_at)

    res = _call(
        riding, name=name, grid=grid, in_specs=list(in_specs) + [ANY] * r_in, out_specs=list(out_specs) + [ANY] * r_out,
        out_shape=list(out_shape) + rider.out_shape,
        input_output_aliases={n_in + i: n_out + o for i, o in rider.aliases.items()},
        scratch_shapes=list(scratch_shapes) + [pltpu.SemaphoreType.DMA((rider.n_sems,))],
        compiler_params=_params(sem),
    )(*args, *rider.inputs)
    return res[:n_out], res[n_out:]


def _alone(name, rider):
    n_in, n_out = len(rider.inputs), len(rider.out_shape)

    def body(*refs):
        ins, outs, sems = refs[:n_in], refs[n_in:n_in + n_out], refs[n_in + n_out]

        def sem_at(k):
            return sems.at[k]

        rider.start(ins, outs, sem_at)
        if rider.middle is not None:
            rider.middle(ins, outs, sem_at)
        rider.finish(ins, outs, sem_at)

    return _call(
        body, name=name, in_specs=[ANY] * n_in, out_specs=[ANY] * n_out, out_shape=rider.out_shape,
        input_output_aliases=rider.aliases, scratch_shapes=[pltpu.SemaphoreType.DMA((rider.n_sems,))],
    )(*rider.inputs)


OTHER_CHIPS = (2, 4, 6)


def _gather_rider(stacked):
    n = len(stacked)

    def flights(bufs, sem):
        me = _me()
        chip, sib = _chip_index(me), _peer(1)
        out = []
        for a in range(n):
            mine, theirs = (_half_rows(bufs[a].shape[1], c) for c in (me[2], 1 - me[2]))
            for j, d in enumerate(OTHER_CHIPS):
                k = 3 * a + j
                from_chip = _chip_index(_peer(d))
                own, landed, passed = bufs[a].at[chip, mine], bufs[a].at[from_chip, mine], bufs[a].at[from_chip, theirs]
                out.append((_remote(own, own, sem(k), sem(3 * n + k), _peer(d)),
                            _remote(own, landed, sem(k), sem(3 * n + k), _peer(d)),
                            _remote(landed, landed, sem(6 * n + k), sem(9 * n + k), sib),
                            _remote(passed, passed, sem(6 * n + k), sem(9 * n + k), sib)))
        return out

    def start(ins, outs, sem):
        for send, _, _, _ in flights(outs, sem):
            send.start()

    def middle(ins, outs, sem):
        for _, arrival, pass_on, _ in flights(outs, sem):
            arrival.wait_recv()
            pass_on.start()

    def finish(ins, outs, sem):
        every = flights(outs, sem)
        for _, _, _, passed_to_me in every:
            passed_to_me.wait_recv()
        for send, _, pass_on, _ in every:
            send.wait_send()
            pass_on.wait_send()

    return _Rider(stacked, [jax.ShapeDtypeStruct(w.shape, w.dtype) for w in stacked], {a: a for a in range(n)}, 12 * n,
                  start, finish, middle)


def _swap_rider(grads):
    n = len(grads)

    def copies(ins, outs, sem):
        c = _me()[2]
        return [_remote(ins[a].at[pl.ds(0, N_SHARD), _half_rows(ins[a].shape[1], 1 - c)], outs[a], sem(a), sem(n + a), _peer(1))
                for a in range(n)]

    def start(ins, outs, sem):
        for cp in copies(ins, outs, sem):
            cp.start()

    def finish(ins, outs, sem):
        for cp in copies(ins, outs, sem):
            cp.wait()

    return _Rider(grads, [jax.ShapeDtypeStruct((N_SHARD, g.shape[1] // 2, g.shape[2]), F32) for g in grads], {}, 2 * n,
                  start, finish)


def _row_tile(rows):
    return min(rows, 256)


def _add_halves(name, g, got, cj):
    _, half, cols = got.shape
    tr = _row_tile(half)
    nt = half // tr

    def body(cj_ref, g_ref, got_ref, o_ref):
        o_ref[...] = (g_ref[...] + got_ref[...]).astype(BF16)

    spec = pl.BlockSpec((None, tr, cols), lambda i, s, cj: (s, i, 0))
    return _call(
        body, name=name,
        grid_spec=pltpu.PrefetchScalarGridSpec(
            num_scalar_prefetch=1, grid=(nt, N_SHARD),
            in_specs=[pl.BlockSpec((None, tr, cols), lambda i, s, cj: (s, cj[0] * nt + i, 0)), spec], out_specs=spec),
        out_shape=jax.ShapeDtypeStruct(got.shape, BF16), compiler_params=_params(("parallel", "parallel")),
    )(cj, g, got)


def _scatter_rider(sums):
    n = len(sums)

    def flights(ins, outs, sem):
        chip = _chip_index(_me())
        out = []
        for a in range(n):
            for j, d in enumerate(OTHER_CHIPS):
                k = 3 * a + j
                other = _chip_index(_peer(d))
                out.append((_remote(ins[a].at[other], outs[a].at[chip], sem(k), sem(3 * n + k), _peer(d)),
                            _remote(ins[a].at[chip], outs[a].at[other], sem(k), sem(3 * n + k), _peer(d))))
        return out

    def start(ins, outs, sem):
        for send, _ in flights(ins, outs, sem):
            send.start()

    def finish(ins, outs, sem):
        every = flights(ins, outs, sem)
        for _, arrival in every:
            arrival.wait_recv()
        for send, _ in every:
            send.wait_send()

    return _Rider(sums, [jax.ShapeDtypeStruct(v.shape, v.dtype) for v in sums], {}, 6 * n, start, finish)


def _sum_chips(name, g, got, landed, cj):
    _, half, cols = got.shape
    tr = _row_tile(half)
    nt = half // tr

    def body(cj_ref, g_ref, got_ref, landed_ref, o_ref):
        own = g_ref[...] + got_ref[...]
        total = None
        for s in range(N_SHARD):
            term = jnp.where(cj_ref[1] == s, own, landed_ref[s].astype(F32))
            total = term if total is None else total + term
        o_ref[...] = total

    return _call(
        body, name=name,
        grid_spec=pltpu.PrefetchScalarGridSpec(
            num_scalar_prefetch=1, grid=(nt,),
            in_specs=[pl.BlockSpec((None, tr, cols), lambda i, cj: (cj[1], cj[0] * nt + i, 0)),
                      pl.BlockSpec((None, tr, cols), lambda i, cj: (cj[1], i, 0)),
                      pl.BlockSpec((N_SHARD, tr, cols), lambda i, cj: (0, i, 0))],
            out_specs=pl.BlockSpec((tr, cols), lambda i, cj: (cj[0] * nt + i, 0))),
        out_shape=jax.ShapeDtypeStruct((2 * half, cols), F32), compiler_params=_params(("parallel",)),
    )(cj, g, got, landed)


def _join_rider(shards):
    n = len(shards)

    def flights(bufs, sem):
        c = _me()[2]
        out = []
        for a in range(n):
            mine, theirs = (bufs[a].at[_half_rows(bufs[a].shape[0], cc)] for cc in (c, 1 - c))
            out.append((_remote(mine, mine, sem(a), sem(n + a), _peer(1)), _remote(theirs, theirs, sem(a), sem(n + a), _peer(1))))
        return out

    def start(ins, outs, sem):
        for send, _ in flights(outs, sem):
            send.start()

    def finish(ins, outs, sem):
        for send, arrival in flights(outs, sem):
            arrival.wait_recv()
            send.wait_send()

    return _Rider(shards, [jax.ShapeDtypeStruct(h.shape, F32) for h in shards], {a: a for a in range(n)}, 2 * n, start, finish)


class _Reduction:
    def __init__(self, names, grads, cj):
        self.names, self.grads, self.cj = names, list(grads), cj

    def swap(self):
        return _swap_rider(self.grads)

    def add(self, got):
        self.got = list(got)
        self.sums = [_add_halves("add_halves_" + nm, g, h, self.cj) for nm, g, h in zip(self.names, self.grads, self.got)]
        return _scatter_rider(self.sums)

    def total(self, landed):
        halves = [_sum_chips("sum_chips_" + nm, g, h, l, self.cj)
                  for nm, g, h, l in zip(self.names, self.grads, self.got, landed)]
        return _join_rider(halves)


def _adamw_math(w, g, m, v):
    m = ADAM_B1 * m + (1.0 - ADAM_B1) * g
    v = ADAM_B2 * v + (1.0 - ADAM_B2) * jnp.square(g)
    m_hat = m / (1.0 - ADAM_B1 ** ADAM_STEP)
    v_hat = v / (1.0 - ADAM_B2 ** ADAM_STEP)
    return -ADAM_LR * (m_hat / (jnp.sqrt(v_hat) + ADAM_EPS) + ADAM_WD * w), m, v


def _adamw(name, w, g, m, v):
    rows, cols = w.shape
    tr = _row_tile(rows)

    def body(w_ref, g_ref, m_ref, v_ref, d_ref, nm_ref, nv_ref):
        d_ref[...], nm_ref[...], nv_ref[...] = _adamw_math(w_ref[...], g_ref[...], m_ref[...], v_ref[...])

    spec = pl.BlockSpec((tr, cols), lambda i: (i, 0))
    return _call(
        body, name=name, grid=(rows // tr,), in_specs=[spec] * 4, out_specs=[spec] * 3,
        out_shape=[jax.ShapeDtypeStruct(w.shape, F32)] * 3, compiler_params=_params(("parallel",)),
    )(w, g, m, v)


def _adamw_small(ws, gs, ms, vs):
    n = len(ws)

    def body(*refs):
        ins, outs = refs[:4 * n], refs[4 * n:]
        for a in range(n):
            w, g, m, v = (ins[k * n + a][...] for k in range(4))
            outs[a][...], outs[n + a][...], outs[2 * n + a][...] = _adamw_math(w, g, m, v)

    vm = pl.BlockSpec(memory_space=pltpu.VMEM)
    res = _call(
        body, name="adamw_small", in_specs=[vm] * (4 * n), out_specs=[vm] * (3 * n),
        out_shape=[jax.ShapeDtypeStruct(w.shape, F32) for w in ws] * 3, compiler_params=_params(),
    )(*ws, *gs, *ms, *vs)
    return res[:n], res[n:2 * n], res[2 * n:]


def kernel(x, c, w_ada, b_ada, norm1_g, w_in, q_norm_a, k_norm_a, sink_b, w_branch, w_out, norm2_g, w_mlp_in, w_mlp_out, final_g, loss_target, m_w_ada, m_b_ada, m_norm1_g, m_w_in, m_q_norm_a, m_k_norm_a, m_sink_b, m_w_branch, m_w_out, m_norm2_g, m_w_mlp_in, m_w_mlp_out, m_final_g, v_w_ada, v_b_ada, v_norm1_g, v_w_in, v_q_norm_a, v_k_norm_a, v_sink_b, v_w_branch, v_w_out, v_norm2_g, v_w_mlp_in, v_w_mlp_out, v_final_g):
    xi, yi, ci = _me()
    cj = jnp.stack([ci, 2 * xi + yi]).astype(jnp.int32)
    n_cols = 6 * D_MODEL // N_SHARD

    mod6, sc_all = _ada_fwd(c, w_ada[0], b_ada.reshape(N_SHARD, n_cols))

    def rows2d(a):
        return a.reshape(-1, a.shape[-1])

    big = (w_in, w_branch, w_out, w_mlp_in, w_mlp_out)
    stacked = _cast_weights([rows2d(w) for w in big])
    w_in_s, = _alone("gather_w_in", _gather_rider(stacked[:1]))
    rest = stacked[1:]

    gq2 = jnp.tile(q_norm_a, (1, 2))
    gk2 = jnp.tile(k_norm_a, (1, 2))
    grad_x, g_big, (acc_out, acc_mlp, acc_in, acc_qk, dsink) = _local_step(
        x[0], loss_target[0], mod6, norm1_g, norm2_g, final_g.reshape(1, D_MODEL), gq2, gk2, sink_b[0], w_in_s, rest, cj)

    loss = lax.psum(0.5 * jnp.sum(acc_out[0]) / D_MODEL, ("x", "y", "c"))
    g_w_ada, g_b_ada, g_n1, g_n2, g_f, g_q, g_k, g_s = _ada_bwd(acc_out, acc_mlp, acc_in, acc_qk, dsink, sc_all)

    names = ("w_ada", "w_in", "w_branch", "w_out", "w_mlp_in", "w_mlp_out")
    big_w = [w_ada[0]] + [rows2d(w) for w in big]
    big_g = [g_w_ada] + list(g_big)
    big_m = [rows2d(m) for m in (m_w_ada, m_w_in, m_w_branch, m_w_out, m_w_mlp_in, m_w_mlp_out)]
    big_v = [rows2d(v) for v in (v_w_ada, v_w_in, v_w_branch, v_w_out, v_w_mlp_in, v_w_mlp_out)]
    big_res = {nm: _adamw("adamw_" + nm, w, g, m, v) for nm, w, g, m, v in zip(names, big_w, big_g, big_m, big_v)}

    small = ("b_ada", "norm1_g", "q_norm_a", "k_norm_a", "sink_b", "norm2_g", "final_g")
    row = lambda a: a.reshape(1, -1)
    small_w = [row(a) for a in (b_ada, norm1_g, q_norm_a, k_norm_a, sink_b, norm2_g, final_g)]
    small_g = [g_b_ada, g_n1, g_q, g_k, g_s, g_n2, g_f]
    small_m = [row(a) for a in (m_b_ada, m_norm1_g, m_q_norm_a, m_k_norm_a, m_sink_b, m_norm2_g, m_final_g)]
    small_v = [row(a) for a in (v_b_ada, v_norm1_g, v_q_norm_a, v_k_norm_a, v_sink_b, v_norm2_g, v_final_g)]
    s_d, s_m, s_v = _adamw_small(small_w, small_g, small_m, small_v)

    order = ("w_ada", "b_ada", "norm1_g", "w_in", "q_norm_a", "k_norm_a", "sink_b", "w_branch", "w_out", "norm2_g",
             "w_mlp_in", "w_mlp_out", "final_g")
    like = dict(w_ada=w_ada, b_ada=b_ada, norm1_g=norm1_g, w_in=w_in, q_norm_a=q_norm_a, k_norm_a=k_norm_a, sink_b=sink_b,
                w_branch=w_branch, w_out=w_out, norm2_g=norm2_g, w_mlp_in=w_mlp_in, w_mlp_out=w_mlp_out, final_g=final_g)
    grad, delta, new_m, new_v = {}, {}, {}, {}
    for nm, g in zip(names, big_g):
        grad[nm] = g
        delta[nm], new_m[nm], new_v[nm] = big_res[nm]
    for k, nm in enumerate(small):
        grad[nm], delta[nm], new_m[nm], new_v[nm] = small_g[k], s_d[k], s_m[k], s_v[k]
    outs = [loss, grad_x[None]]
    for group in (grad, delta, new_m, new_v):
        outs += [group[nm].reshape(like[nm].shape) for nm in order]
    return tuple(outs)
```

```python
import functools

import jax
import jax.numpy as jnp
from jax import lax
from jax.experimental import pallas as pl
from jax.experimental.pallas import tpu as pltpu

F32 = jnp.float32
BF16 = jnp.bfloat16
MESH = pl.DeviceIdType.MESH
ANY = pl.BlockSpec(memory_space=pl.ANY)

D_MODEL = 1024
HEAD_DIM = 64
Q_HEADS = 8
KV_HEADS = 2
GROUP = Q_HEADS // KV_HEADS
BRANCH_W = Q_HEADS * HEAD_DIM
KV_W = KV_HEADS * HEAD_DIM
IN_W = 2 * (BRANCH_W + 2 * KV_W) + 2 * D_MODEL
QK_W = 2 * (BRANCH_W + 2 * KV_W)
D_FF = 4 * D_MODEL
GRID_W = 64
WINDOW = 128
ROPE_THETA = 10000.0
NORM_EPS = 1e-6
NEG_INF = -1e30
Q_SCALE = HEAD_DIM ** -0.5
N_SHARD = 4
N_DEV = 8
LANES = 128
VMEM_LIMIT = 56 * 1024 * 1024

ADAM_LR = 0.001
ADAM_B1 = 0.9
ADAM_B2 = 0.999
ADAM_EPS = 1e-08
ADAM_WD = 0.01
ADAM_STEP = 10

_call = pl.pallas_call


def _params(sem=None, vmem=VMEM_LIMIT):
    return pltpu.CompilerParams(dimension_semantics=sem, vmem_limit_bytes=vmem)


def _nt(a, b):
    return lax.dot_general(a, b, (((1,), (1,)), ((), ())), preferred_element_type=F32)


def _tn(a, b):
    return lax.dot_general(a, b, (((0,), (0,)), ((), ())), preferred_element_type=F32)


def _nn(a, b):
    return jnp.dot(a, b, preferred_element_type=F32)


def _sigmoid(z):
    return 0.5 * jnp.tanh(0.5 * z) + 0.5


def _rope_tables(s):
    t = jnp.arange(s, dtype=jnp.int32)
    lane = jnp.arange(LANES, dtype=jnp.int32)

    def cos_sin(pos, dim):
        inv = ROPE_THETA ** (-jnp.arange(0, dim, 2, dtype=F32) / dim)
        ang = pos.astype(F32)[:, None] * inv[None, :]
        return jnp.cos(ang), jnp.sin(ang)

    cr, sr = cos_sin(t // GRID_W, HEAD_DIM // 2)
    cc, sc = cos_sin(t % GRID_W, HEAD_DIM // 2)
    cos_a = jnp.tile(jnp.concatenate([cr, cr, cc, cc], axis=1), (1, 2))
    sin_a = jnp.tile(jnp.concatenate([sr, sr, sc, sc], axis=1), (1, 2))
    first_a = (lane % 32) < 16
    c1, s1 = cos_sin(t, HEAD_DIM)
    cos_b = jnp.tile(jnp.concatenate([c1, c1], axis=1), (1, 2))
    sin_b = jnp.tile(jnp.concatenate([s1, s1], axis=1), (1, 2))
    first_b = (lane % 64) < 32
    tabs_a = (cos_a, jnp.where(first_a, -sin_a, 0.0), jnp.where(first_a, 0.0, sin_a))
    tabs_b = (cos_b, jnp.where(first_b, -sin_b, 0.0), jnp.where(first_b, 0.0, sin_b))
    return tabs_a + tabs_b


def _rope(z, cos, s_lo, s_hi, half, sign=1.0):
    up = pltpu.roll(z, LANES - half, 1)
    dn = pltpu.roll(z, half, 1)
    return z * cos + sign * (up * s_lo + dn * s_hi)


def _head_mean(z2, bd):
    hi = z2.astype(BF16)
    lo = (z2 - hi.astype(F32)).astype(BF16)
    return _nn(hi, bd) + _nn(lo, bd)


def _block_diag():
    lane = jnp.arange(LANES)
    return jnp.where((lane[:, None] // HEAD_DIM) == (lane[None, :] // HEAD_DIM), 1.0 / HEAD_DIM, 0.0).astype(BF16)


def _row_spec(tm, width):
    return pl.BlockSpec((tm, width), lambda i: (i, 0))


def _heads_spec(heads, tm):
    return pl.BlockSpec((heads, tm, HEAD_DIM), lambda i: (0, i, 0))


def _full_spec(shape):
    nd = len(shape)
    return pl.BlockSpec(shape, lambda i: (0,) * nd)


def _in_proj(x, mod6, g1, w_in_s, gq, gk, bd, tabs, tm=256):
    s = x.shape[0]

    def body(x_ref, mod_ref, g1_ref, w_ref, gq_ref, gk_ref, bd_ref, ca, la, ha, cb, lb, hb,
             h_ref, qkraw_ref, qa_ref, ka_ref, va_ref, qb_ref, kb_ref, vb_ref, gate_ref):
        xt = x_ref[...]
        r = lax.rsqrt(jnp.mean(xt * xt, axis=-1, keepdims=True) + NORM_EPS)
        h = (xt * r * g1_ref[...]) * (1.0 + mod_ref[1:2, :]) + mod_ref[0:1, :]
        hb16 = h.astype(BF16)
        h_ref[...] = hb16
        proj = jnp.concatenate([_nn(hb16, w_ref[j]) for j in range(N_SHARD)], axis=1)
        qkraw_ref[...] = proj[:, :BRANCH_W + KV_W]
        bdm = bd_ref[...]
        tab_a = (ca[...], la[...], ha[...])
        tab_b = (cb[...], lb[...], hb[...])

        def norm_rope_a(z, gain):
            zn = z * lax.rsqrt(_head_mean(z * z, bdm) + NORM_EPS) * gain
            return _rope(zn, *tab_a, 16)

        def put(ref, first, z):
            zb = z.astype(BF16)
            ref[first] = zb[:, :HEAD_DIM]
            ref[first + 1] = zb[:, HEAD_DIM:]

        for i in range(Q_HEADS // 2):
            put(qa_ref, 2 * i, norm_rope_a(proj[:, LANES * i:LANES * (i + 1)], gq_ref[...]) * Q_SCALE)
        off = BRANCH_W
        put(ka_ref, 0, norm_rope_a(proj[:, off:off + LANES], gk_ref[...]))
        off += KV_W
        def put_v(ref, z):
            zb = z.astype(BF16)
            for hd in range(KV_HEADS):
                ref[hd, :, :HEAD_DIM] = zb[:, HEAD_DIM * hd:HEAD_DIM * (hd + 1)]
                ref[hd, :, HEAD_DIM:] = jnp.ones((tm, HEAD_DIM), BF16)

        put_v(va_ref, proj[:, off:off + LANES])
        off += KV_W
        for i in range(Q_HEADS // 2):
            put(qb_ref, 2 * i, _rope(proj[:, off + LANES * i:off + LANES * (i + 1)], *tab_b, 32) * Q_SCALE)
        off += BRANCH_W
        put(kb_ref, 0, _rope(proj[:, off:off + LANES], *tab_b, 32))
        off += KV_W
        put_v(vb_ref, proj[:, off:off + LANES])
        gate_ref[...] = proj[:, QK_W:]

    tab_spec = _row_spec(tm, LANES)
    return _call(
        body, name="in_proj", grid=(s // tm,),
        in_specs=[_row_spec(tm, D_MODEL), _full_spec(mod6.shape), _full_spec(g1.shape), _full_spec(w_in_s.shape),
                  _full_spec(gq.shape), _full_spec(gk.shape), _full_spec(bd.shape)] + [tab_spec] * 6,
        out_specs=[_row_spec(tm, D_MODEL), _row_spec(tm, BRANCH_W + KV_W), _heads_spec(Q_HEADS, tm), _heads_spec(KV_HEADS, tm),
                   pl.BlockSpec((KV_HEADS, tm, LANES), lambda i: (0, i, 0)), _heads_spec(Q_HEADS, tm),
                   _heads_spec(KV_HEADS, tm), pl.BlockSpec((KV_HEADS, tm, LANES), lambda i: (0, i, 0)),
                   _row_spec(tm, 2 * D_MODEL)],
        out_shape=[jax.ShapeDtypeStruct((s, D_MODEL), BF16), jax.ShapeDtypeStruct((s, BRANCH_W + KV_W), F32),
                   jax.ShapeDtypeStruct((Q_HEADS, s, HEAD_DIM), BF16), jax.ShapeDtypeStruct((KV_HEADS, s, HEAD_DIM), BF16),
                   jax.ShapeDtypeStruct((KV_HEADS, s, LANES), BF16), jax.ShapeDtypeStruct((Q_HEADS, s, HEAD_DIM), BF16),
                   jax.ShapeDtypeStruct((KV_HEADS, s, HEAD_DIM), BF16), jax.ShapeDtypeStruct((KV_HEADS, s, LANES), BF16),
                   jax.ShapeDtypeStruct((s, 2 * D_MODEL), F32)],
        compiler_params=_params(("parallel",)),
    )(x, mod6, g1, w_in_s, gq, gk, bd, *tabs)


def _group_specs(s, tq):
    q_spec = pl.BlockSpec((None, GROUP, tq, HEAD_DIM), lambda g, i: (g, 0, i, 0))
    kv_spec = pl.BlockSpec((None, s, HEAD_DIM), lambda g, i: (g, 0, 0))
    col_spec = pl.BlockSpec((None, GROUP, tq, 1), lambda g, i: (g, 0, i, 0))
    return q_spec, kv_spec, col_spec


def _attn_a_fwd(q, k, v1, tq=256, tk=512):
    s = q.shape[1]
    tk = min(tk, s)
    rows = GROUP * tq

    n = s // tk
    assert n >= 2 and n % 2 == 0

    def body(q_ref, k_ref, v_ref, o_ref, oh_ref, lse_ref, s0_ref, s1_ref, p0_ref, p1_ref, m_ref, a_ref, acc_ref):
        s_ref, p_ref = (s0_ref, s1_ref), (p0_ref, p1_ref)
        qq = q_ref[...].reshape(rows, HEAD_DIM)
        m_ref[...] = jnp.full((rows, 1), NEG_INF, F32)
        acc_ref[...] = jnp.zeros((rows, LANES), F32)

        def keys(i):
            return pl.ds(pl.multiple_of(i * tk, tk), tk)

        def scores(i, slot):
            s_ref[slot][...] = _nt(qq, k_ref[keys(i), :])

        def softmax(slot):
            sc = s_ref[slot][...]
            m = m_ref[...]
            mn = jnp.maximum(m, jnp.max(sc, axis=-1, keepdims=True))
            m_ref[...] = mn
            a_ref[...] = jnp.exp(m - mn)
            p_ref[slot][...] = jnp.exp(sc - mn).astype(BF16)

        def weigh(i, slot):
            acc_ref[...] = a_ref[...] * acc_ref[...] + _nn(p_ref[slot][...], v_ref[keys(i), :])

        scores(0, 0)
        softmax(0)
        scores(1, 1)

        def two_steps(j, carry):
            i = 2 * j + 1
            weigh(i - 1, 0)
            softmax(1)
            scores(i + 1, 0)
            weigh(i, 1)
            softmax(0)
            scores(i + 2, 1)
            return carry

        lax.fori_loop(0, (n - 2) // 2, two_steps, 0)
        weigh(n - 2, 0)
        softmax(1)
        weigh(n - 1, 1)
        l = acc_ref[:, HEAD_DIM:HEAD_DIM + 1]
        o = (acc_ref[:, :HEAD_DIM] / l).astype(BF16)
        for g in range(GROUP):
            o_ref[:, HEAD_DIM * g:HEAD_DIM * (g + 1)] = o[tq * g:tq * (g + 1)]
        oh_ref[...] = o.reshape(GROUP, tq, HEAD_DIM)
        lse_ref[...] = (m_ref[...] + jnp.log(l)).reshape(GROUP, tq, 1)

    q_spec, kv_spec, col_spec = _group_specs(s, tq)
    v_spec = pl.BlockSpec((None, s, LANES), lambda g, i: (g, 0, 0))
    return _call(
        body, name="attn_a_fwd", grid=(KV_HEADS, s // tq),
        in_specs=[q_spec, kv_spec, v_spec],
        out_specs=[pl.BlockSpec((tq, GROUP * HEAD_DIM), lambda g, i: (i, g)), q_spec, col_spec],
        out_shape=[jax.ShapeDtypeStruct((s, BRANCH_W), BF16), jax.ShapeDtypeStruct((KV_HEADS, GROUP, s, HEAD_DIM), BF16),
                   jax.ShapeDtypeStruct((KV_HEADS, GROUP, s, 1), F32)],
        scratch_shapes=[pltpu.VMEM((rows, tk), F32), pltpu.VMEM((rows, tk), F32), pltpu.VMEM((rows, tk), BF16),
                        pltpu.VMEM((rows, tk), BF16), pltpu.VMEM((rows, 1), F32), pltpu.VMEM((rows, 1), F32),
                        pltpu.VMEM((rows, LANES), F32)],
        compiler_params=_params(("parallel", "parallel")),
    )(q.reshape(KV_HEADS, GROUP, s, HEAD_DIM), k, v1)


def _attn_a_bwd(q, k, v1, o, do, lse, tq=256, tk=512):
    v = v1
    s = q.shape[1]
    tk = min(tk, s)
    rows = GROUP * tq

    n = s // tk
    assert n >= 2 and n % 2 == 0

    def body(q_ref, k_ref, v_ref, o_ref, do_ref, lse_ref, dq_ref, dk_ref, dv_ref,
             s0_ref, s1_ref, dp0_ref, dp1_ref, p0_ref, p1_ref, ds0_ref, ds1_ref, dq_acc):
        s_ref, dp_ref, p_ref, ds_ref = (s0_ref, s1_ref), (dp0_ref, dp1_ref), (p0_ref, p1_ref), (ds0_ref, ds1_ref)

        @pl.when(pl.program_id(1) == 0)
        def _():
            dk_ref[...] = jnp.zeros_like(dk_ref)
            dv_ref[...] = jnp.zeros_like(dv_ref)

        qq = q_ref[...].reshape(rows, HEAD_DIM)
        dd = do_ref[...].reshape(rows, HEAD_DIM)
        ls = lse_ref[...].reshape(rows, 1)
        dl = jnp.sum(dd.astype(F32) * o_ref[...].reshape(rows, HEAD_DIM).astype(F32), axis=-1, keepdims=True)
        dq_acc[...] = jnp.zeros((rows, HEAD_DIM), F32)

        def keys(i):
            return pl.ds(pl.multiple_of(i * tk, tk), tk)

        def scores(i, slot):
            s_ref[slot][...] = _nt(qq, k_ref[keys(i), :])
            dp_ref[slot][...] = _nt(dd, v_ref[keys(i), :HEAD_DIM])

        def weights(slot):
            p = jnp.exp(s_ref[slot][...] - ls)
            p_ref[slot][...] = p.astype(BF16)
            ds_ref[slot][...] = (p * (dp_ref[slot][...] - dl)).astype(BF16)

        def grads(i, slot):
            dv_ref[keys(i), :] += _tn(p_ref[slot][...], dd)
            dk_ref[keys(i), :] += _tn(ds_ref[slot][...], qq)
            dq_acc[...] += _nn(ds_ref[slot][...], k_ref[keys(i), :])

        scores(0, 0)
        weights(0)
        scores(1, 1)

        def two_steps(j, carry):
            i = 2 * j + 1
            grads(i - 1, 0)
            weights(1)
            scores(i + 1, 0)
            grads(i, 1)
            weights(0)
            scores(i + 2, 1)
            return carry

        lax.fori_loop(0, (n - 2) // 2, two_steps, 0, unroll=True)
        grads(n - 2, 0)
        weights(1)
        grads(n - 1, 1)
        dq_ref[...] = dq_acc[...].reshape(GROUP, tq, HEAD_DIM)

    q_spec, kv_spec, col_spec = _group_specs(s, tq)
    v_spec = pl.BlockSpec((None, s, LANES), lambda g, i: (g, 0, 0))
    shape4 = (KV_HEADS, GROUP, s, HEAD_DIM)
    tile32, tile16 = pltpu.VMEM((rows, tk), F32), pltpu.VMEM((rows, tk), BF16)
    return _call(
        body, name="attn_a_bwd", grid=(KV_HEADS, s // tq),
        in_specs=[q_spec, kv_spec, v_spec, q_spec, q_spec, col_spec],
        out_specs=[q_spec, kv_spec, kv_spec],
        out_shape=[jax.ShapeDtypeStruct(shape4, F32), jax.ShapeDtypeStruct((KV_HEADS, s, HEAD_DIM), F32),
                   jax.ShapeDtypeStruct((KV_HEADS, s, HEAD_DIM), F32)],
        scratch_shapes=[tile32] * 4 + [tile16] * 4 + [pltpu.VMEM((rows, HEAD_DIM), F32)],
        compiler_params=_params(("parallel", "arbitrary")),
    )(q.reshape(shape4), k, v, o.reshape(shape4), do.reshape(shape4), lse)


TQ_B = WINDOW


def _win_keys(tq):
    return tq + 2 * WINDOW


def _window_bias(tq):
    r = jnp.arange(tq, dtype=jnp.int32)[:, None]
    col = jnp.arange(_win_keys(tq), dtype=jnp.int32)[None, :]
    return jnp.stack([jnp.where(jnp.abs(r - col + WINDOW * b) <= WINDOW, 0.0, NEG_INF) for b in range(3)]).astype(F32)


def _band(tq, s):
    win = _win_keys(tq)

    def window(e):
        return pl.ds(pl.multiple_of(jnp.clip(e * tq - WINDOW, 0, s - win), WINDOW), win)

    def bias_index(e):
        return jnp.where(e == 0, 0, jnp.where(e >= s // tq - 1, 2, 1))

    return window, bias_index


def _pair_specs(s, tq):
    pairs = s // (2 * tq)
    cur = lambda g, j: (g, 0, jnp.minimum(j, pairs - 1), 0)
    prev = lambda g, j: (g, 0, jnp.maximum(j - 1, 0), 0)
    tile = lambda width, index: pl.BlockSpec((None, GROUP, 2 * tq, width), index)
    kv_spec = pl.BlockSpec((None, s, HEAD_DIM), lambda g, j: (g, 0, 0))
    v_spec = pl.BlockSpec((None, s, LANES), lambda g, j: (g, 0, 0))
    sink_spec = pl.BlockSpec((None, GROUP * tq, 1), lambda g, j: (g, 0, 0))
    bias_spec = pl.BlockSpec((3, tq, _win_keys(tq)), lambda g, j: (0, 0, 0))
    return tile, cur, prev, kv_spec, v_spec, sink_spec, bias_spec


def _attn_b_fwd(q, k, v1, sink_col, bias, rider=None, tq=TQ_B):
    s = q.shape[1]
    rows = GROUP * tq
    win = _win_keys(tq)
    pairs = s // (2 * tq)
    window, bias_index = _band(tq, s)

    def body(q_ref, k_ref, v_ref, sink_ref, bias_ref, o_ref, oh_ref, lse_ref, s0_ref, s1_ref, p0_ref, p1_ref, m0_ref, m1_ref):
        s_ref, p_ref, m_ref = (s0_ref, s1_ref), (p0_ref, p1_ref), (m0_ref, m1_ref)
        j = pl.program_id(1)

        @pl.when(j == 0)
        def _():
            for ref in (s0_ref, s1_ref, p0_ref, p1_ref, m0_ref, m1_ref):
                ref[...] = jnp.zeros_like(ref)

        def scores(e, slot):
            qq = q_ref[:, pl.ds(slot * tq, tq), :].reshape(rows, HEAD_DIM)
            sc = _nt(qq, k_ref[window(e), :]).reshape(GROUP, tq, win) + bias_ref[bias_index(e)][None]
            s_ref[slot][...] = sc.reshape(rows, win)

        def softmax(slot):
            sc = s_ref[slot][...]
            m = jnp.maximum(jnp.max(sc, axis=-1, keepdims=True), sink_ref[...])
            m_ref[slot][...] = m
            p_ref[slot][...] = jnp.exp(sc - m).astype(BF16)

        def finish(e, slot):
            acc = _nn(p_ref[slot][...], v_ref[window(e), :])
            m = m_ref[slot][...]
            l = acc[:, HEAD_DIM:HEAD_DIM + 1] + jnp.exp(sink_ref[...] - m)
            o = (acc[:, :HEAD_DIM] / l).astype(BF16)
            at = pl.ds(slot * tq, tq)
            for g in range(GROUP):
                o_ref[at, HEAD_DIM * g:HEAD_DIM * (g + 1)] = o[tq * g:tq * (g + 1)]
            oh_ref[:, at, :] = o.reshape(GROUP, tq, HEAD_DIM)
            lse_ref[:, at, :] = (m + jnp.log(l)).reshape(GROUP, tq, 1)

        first = 2 * j
        finish(jnp.maximum(first - 2, 0), 0)
        softmax(1)
        scores(first, 0)
        finish(jnp.maximum(first - 1, 0), 1)
        softmax(0)
        scores(first + 1, 1)

    tile, cur, prev, kv_spec, v_spec, sink_spec, bias_spec = _pair_specs(s, tq)
    tile32, tile16, col = pltpu.VMEM((rows, win), F32), pltpu.VMEM((rows, win), BF16), pltpu.VMEM((rows, 1), F32)
    return _hosted(
        body, rider, name="attn_b_fwd", grid=(KV_HEADS, pairs + 1),
        in_specs=[tile(HEAD_DIM, cur), kv_spec, v_spec, sink_spec, bias_spec],
        out_specs=[pl.BlockSpec((2 * tq, GROUP * HEAD_DIM), lambda g, j: (jnp.maximum(j - 1, 0), g)),
                   tile(HEAD_DIM, prev), tile(1, prev)],
        out_shape=[jax.ShapeDtypeStruct((s, BRANCH_W), BF16), jax.ShapeDtypeStruct((KV_HEADS, GROUP, s, HEAD_DIM), BF16),
                   jax.ShapeDtypeStruct((KV_HEADS, GROUP, s, 1), F32)],
        scratch_shapes=[tile32, tile32, tile16, tile16, col, col],
        args=(q.reshape(KV_HEADS, GROUP, s, HEAD_DIM), k, v1, sink_col, bias), middle_at=3 * KV_HEADS * (pairs + 1) // 4)


def _attn_b_bwd(q, k, v1, o, do, lse, sink_col, bias, rider=None, tq=TQ_B):
    s = q.shape[1]
    rows = GROUP * tq
    win = _win_keys(tq)
    pairs = s // (2 * tq)
    window, bias_index = _band(tq, s)

    def body(q_ref, k_ref, v_ref, o_ref, do_ref, lse_ref, sink_ref, bias_ref, dq_ref, dk_ref, dv_ref, dsink_ref,
             s0, s1, dp0, dp1, p0, p1, ds0, ds1, q0, q1, d0, d1, ls0, ls1, dl0, dl1):
        s_ref, dp_ref, p_ref, ds_ref = (s0, s1), (dp0, dp1), (p0, p1), (ds0, ds1)
        q_keep, do_keep, lse_keep, delta_keep = (q0, q1), (d0, d1), (ls0, ls1), (dl0, dl1)
        j = pl.program_id(1)

        @pl.when(j == 0)
        def _():
            for ref in (dk_ref, dv_ref, dsink_ref, s0, s1, dp0, dp1, p0, p1, ds0, ds1, q0, q1, d0, d1, ls0, ls1, dl0, dl1):
                ref[...] = jnp.zeros_like(ref)

        def scores(e, slot):
            at = pl.ds(slot * tq, tq)
            qq = q_ref[:, at, :].reshape(rows, HEAD_DIM)
            dd = do_ref[:, at, :].reshape(rows, HEAD_DIM)
            q_keep[slot][...] = qq
            do_keep[slot][...] = dd
            lse_keep[slot][...] = lse_ref[:, at, :].reshape(rows, 1)
            delta_keep[slot][...] = jnp.sum(dd.astype(F32) * o_ref[:, at, :].reshape(rows, HEAD_DIM).astype(F32), axis=-1,
                                            keepdims=True)
            sc = _nt(qq, k_ref[window(e), :]).reshape(GROUP, tq, win) + bias_ref[bias_index(e)][None]
            s_ref[slot][...] = sc.reshape(rows, win)
            dp_ref[slot][...] = _nt(dd, v_ref[window(e), :HEAD_DIM])

        def weights(slot):
            p = jnp.exp(s_ref[slot][...] - lse_keep[slot][...])
            p_ref[slot][...] = p.astype(BF16)
            ds_ref[slot][...] = (p * (dp_ref[slot][...] - delta_keep[slot][...])).astype(BF16)

        def grads(e, slot, live):
            at = window(e)
            ds = ds_ref[slot][...]
            dv_ref[at, :] += _tn(p_ref[slot][...], do_keep[slot][...])
            dk_ref[at, :] += _tn(ds, q_keep[slot][...])
            dq_ref[:, pl.ds(slot * tq, tq), :] = _nn(ds, k_ref[at, :]).reshape(GROUP, tq, HEAD_DIM)
            dsk = jnp.exp(sink_ref[...] - lse_keep[slot][...]) * delta_keep[slot][...] * live
            for g in range(GROUP):
                dsink_ref[g:g + 1, :] -= jnp.broadcast_to(jnp.sum(dsk[tq * g:tq * (g + 1)], axis=0, keepdims=True), (1, LANES))

        first = 2 * j
        live = jnp.where(j > 0, 1.0, 0.0)
        grads(jnp.maximum(first - 2, 0), 0, live)
        weights(1)
        scores(first, 0)
        grads(jnp.maximum(first - 1, 0), 1, live)
        weights(0)
        scores(first + 1, 1)

    tile, cur, prev, kv_spec, v_spec, sink_spec, bias_spec = _pair_specs(s, tq)
    dsink_spec = pl.BlockSpec((None, ACC_ROWS, LANES), lambda g, j: (g, 0, 0))
    shape4 = (KV_HEADS, GROUP, s, HEAD_DIM)
    tile32, tile16 = pltpu.VMEM((rows, win), F32), pltpu.VMEM((rows, win), BF16)
    keep, col = pltpu.VMEM((rows, HEAD_DIM), BF16), pltpu.VMEM((rows, 1), F32)
    return _hosted(
        body, rider, name="attn_b_bwd", grid=(KV_HEADS, pairs + 1),
        in_specs=[tile(HEAD_DIM, cur), kv_spec, v_spec, tile(HEAD_DIM, cur), tile(HEAD_DIM, cur), tile(1, cur), sink_spec,
                  bias_spec],
        out_specs=[tile(HEAD_DIM, prev), kv_spec, kv_spec, dsink_spec],
        out_shape=[jax.ShapeDtypeStruct(shape4, F32), jax.ShapeDtypeStruct((KV_HEADS, s, HEAD_DIM), F32),
                   jax.ShapeDtypeStruct((KV_HEADS, s, HEAD_DIM), F32), jax.ShapeDtypeStruct((KV_HEADS, ACC_ROWS, LANES), F32)],
        scratch_shapes=[tile32] * 4 + [tile16] * 4 + [keep] * 4 + [col] * 4,
        args=(q.reshape(shape4), k, v1, o.reshape(shape4), do.reshape(shape4), lse, sink_col, bias))


def _post_attn(ya, yb, gates, x, mod6, wbr_s, w_out, tm=256):
    s = x.shape[0]

    def body(ya_ref, yb_ref, g_ref, x_ref, mod_ref, wbr_ref, wo_ref, ua_ref, ub_ref, mg_ref, o_ref, x1_ref):
        ya_t, yb_t = ya_ref[...], yb_ref[...]
        ua = jnp.concatenate([_nn(ya_t, wbr_ref[j, 0]) for j in range(N_SHARD)], axis=1)
        ub = jnp.concatenate([_nn(yb_t, wbr_ref[j, 1]) for j in range(N_SHARD)], axis=1)
        merged = (_sigmoid(g_ref[:, :D_MODEL]) * ua + _sigmoid(g_ref[:, D_MODEL:]) * ub).astype(BF16)
        o = _nn(merged, wo_ref[...])
        ua_ref[...] = ua.astype(BF16)
        ub_ref[...] = ub.astype(BF16)
        mg_ref[...] = merged
        o_ref[...] = o.astype(BF16)
        x1_ref[...] = x_ref[...] + mod_ref[2:3, :] * o

    bf = jax.ShapeDtypeStruct((s, D_MODEL), BF16)
    return _call(
        body, name="post_attn", grid=(s // tm,),
        in_specs=[_row_spec(tm, BRANCH_W), _row_spec(tm, BRANCH_W), _row_spec(tm, 2 * D_MODEL), _row_spec(tm, D_MODEL),
                  _full_spec(mod6.shape), _full_spec(wbr_s.shape), _full_spec(w_out.shape)],
        out_specs=[_row_spec(tm, D_MODEL)] * 5,
        out_shape=[bf, bf, bf, bf, jax.ShapeDtypeStruct((s, D_MODEL), F32)],
        compiler_params=_params(("parallel",)),
    )(ya, yb, gates, x, mod6, wbr_s, w_out)


def _mlp_in(x1, mod6, g2, w_mi_s, tm=256):
    s = x1.shape[0]

    def body(x_ref, mod_ref, g_ref, w_ref, h2_ref, a_ref, hid_ref):
        xt = x_ref[...]
        r = lax.rsqrt(jnp.mean(xt * xt, axis=-1, keepdims=True) + NORM_EPS)
        h2 = ((xt * r * g_ref[...]) * (1.0 + mod_ref[4:5, :]) + mod_ref[3:4, :]).astype(BF16)
        h2_ref[...] = h2
        a = jnp.concatenate([_nn(h2, w_ref[j]) for j in range(N_SHARD)], axis=1)
        a_ref[...] = a.astype(BF16)
        hid_ref[...] = jnp.square(jnp.maximum(a, 0.0)).astype(BF16)

    return _call(
        body, name="mlp_in", grid=(s // tm,),
        in_specs=[_row_spec(tm, D_MODEL), _full_spec(mod6.shape), _full_spec(g2.shape), _full_spec(w_mi_s.shape)],
        out_specs=[_row_spec(tm, D_MODEL), _row_spec(tm, D_FF), _row_spec(tm, D_FF)],
        out_shape=[jax.ShapeDtypeStruct((s, D_MODEL), BF16), jax.ShapeDtypeStruct((s, D_FF), BF16),
                   jax.ShapeDtypeStruct((s, D_FF), BF16)],
        compiler_params=_params(("parallel",)),
    )(x1, mod6, g2, w_mi_s)


ACC_ROWS = 8


def _acc_spec():
    return pl.BlockSpec((ACC_ROWS, D_MODEL), lambda i: (0, 0))


def _acc_add(acc_ref, rows):
    @pl.when(pl.program_id(0) == 0)
    def _():
        acc_ref[...] = jnp.zeros_like(acc_ref)

    for r, val in enumerate(rows):
        acc_ref[r:r + 1, :] += jnp.sum(val, axis=0, keepdims=True)


def _mlp_out_loss(hid, x1, a, target, mod6, gf, w_mo, tm=256):
    s = x1.shape[0]

    def body(hid_ref, x_ref, a_ref, t_ref, mod_ref, gf_ref, w_ref, dx2_ref, dm_ref, da_ref, acc_ref):
        m = _nn(hid_ref[...], w_ref[...])
        gate2 = mod_ref[5:6, :]
        x2 = x_ref[...] + gate2 * m
        r = lax.rsqrt(jnp.mean(x2 * x2, axis=-1, keepdims=True) + NORM_EPS)
        xn = x2 * r
        err = xn * gf_ref[...] - t_ref[...]
        dy = err * (1.0 / D_MODEL)
        dxn = dy * gf_ref[...]
        dx2 = r * (dxn - xn * jnp.mean(dxn * xn, axis=-1, keepdims=True))
        dx2_ref[...] = dx2
        dm = (dx2 * gate2).astype(BF16)
        dm_ref[...] = dm
        da_ref[...] = (_nt(dm, w_ref[...]) * (2.0 * jnp.maximum(a_ref[...].astype(F32), 0.0))).astype(BF16)
        _acc_add(acc_ref, [err * err, dy * xn, dx2 * m])

    return _call(
        body, name="mlp_out_loss", grid=(s // tm,),
        in_specs=[_row_spec(tm, D_FF), _row_spec(tm, D_MODEL), _row_spec(tm, D_FF), _row_spec(tm, D_MODEL),
                  _full_spec(mod6.shape), _full_spec(gf.shape), _full_spec(w_mo.shape)],
        out_specs=[_row_spec(tm, D_MODEL), _row_spec(tm, D_MODEL), _row_spec(tm, D_FF), _acc_spec()],
        out_shape=[jax.ShapeDtypeStruct((s, D_MODEL), F32), jax.ShapeDtypeStruct((s, D_MODEL), BF16),
                   jax.ShapeDtypeStruct((s, D_FF), BF16), jax.ShapeDtypeStruct((ACC_ROWS, D_MODEL), F32)],
        compiler_params=_params(("arbitrary",)),
    )(hid, x1, a, target, mod6, gf, w_mo)


def _norm_bwd(dh, xt, gain, scale):
    r = lax.rsqrt(jnp.mean(xt * xt, axis=-1, keepdims=True) + NORM_EPS)
    xn = xt * r
    dxn = dh * (gain * (1.0 + scale))
    dx = r * (dxn - xn * jnp.mean(dxn * xn, axis=-1, keepdims=True))
    return dx, [dh, dh * xn * gain, dh * xn * (1.0 + scale)]


def _mlp_bwd(da, x1, dx2, o, mod6, g2, w_mi_s, rider=None, tm=256):
    s = x1.shape[0]

    def body(da_ref, x_ref, dx2_ref, o_ref, mod_ref, g_ref, w_ref, dx1_ref, do_ref, acc_ref):
        dh2 = _nt(da_ref[:, :D_MODEL], w_ref[0])
        for j in range(1, N_SHARD):
            dh2 += _nt(da_ref[:, D_MODEL * j:D_MODEL * (j + 1)], w_ref[j])
        dx, sums = _norm_bwd(dh2, x_ref[...], g_ref[...], mod_ref[4:5, :])
        dx1 = dx2_ref[...] + dx
        dx1_ref[...] = dx1
        do_ref[...] = (dx1 * mod_ref[2:3, :]).astype(BF16)
        _acc_add(acc_ref, sums + [dx1 * o_ref[...].astype(F32)])

    return _hosted(
        body, rider, name="mlp_bwd", grid=(s // tm,),
        in_specs=[_row_spec(tm, D_FF), _row_spec(tm, D_MODEL), _row_spec(tm, D_MODEL), _row_spec(tm, D_MODEL),
                  _full_spec(mod6.shape), _full_spec(g2.shape), _full_spec(w_mi_s.shape)],
        out_specs=[_row_spec(tm, D_MODEL), _row_spec(tm, D_MODEL), _acc_spec()],
        out_shape=[jax.ShapeDtypeStruct((s, D_MODEL), F32), jax.ShapeDtypeStruct((s, D_MODEL), BF16),
                   jax.ShapeDtypeStruct((ACC_ROWS, D_MODEL), F32)],
        args=(da, x1, dx2, o, mod6, g2, w_mi_s))


def _merge_bwd(do, gates, ua, ub, w_out, wbr_s, rider=None, tm=256):
    s = do.shape[0]

    def body(do_ref, g_ref, ua_ref, ub_ref, wo_ref, wbr_ref, dua_ref, dub_ref, dg_ref, doa_ref, dob_ref):
        dmerged = _nt(do_ref[...], wo_ref[...])
        for b, (u_ref, du_ref, dy_ref) in enumerate(((ua_ref, dua_ref, doa_ref), (ub_ref, dub_ref, dob_ref))):
            sg = _sigmoid(g_ref[:, D_MODEL * b:D_MODEL * (b + 1)])
            du = (dmerged * sg).astype(BF16)
            du_ref[...] = du
            dg_ref[:, D_MODEL * b:D_MODEL * (b + 1)] = (dmerged * u_ref[...].astype(F32) * sg * (1.0 - sg)).astype(BF16)
            w = BRANCH_W // 2
            dy = _nt(du[:, :w], wbr_ref[0, b])
            for j in range(1, N_SHARD):
                dy += _nt(du[:, w * j:w * (j + 1)], wbr_ref[j, b])
            dyb = dy.astype(BF16)
            for h in range(Q_HEADS):
                dy_ref[h] = dyb[:, HEAD_DIM * h:HEAD_DIM * (h + 1)]

    bf = jax.ShapeDtypeStruct((s, D_MODEL), BF16)
    heads = jax.ShapeDtypeStruct((Q_HEADS, s, HEAD_DIM), BF16)
    return _hosted(
        body, rider, name="merge_bwd", grid=(s // tm,),
        in_specs=[_row_spec(tm, D_MODEL), _row_spec(tm, 2 * D_MODEL), _row_spec(tm, D_MODEL), _row_spec(tm, D_MODEL),
                  _full_spec(w_out.shape), _full_spec(wbr_s.shape)],
        out_specs=[_row_spec(tm, D_MODEL), _row_spec(tm, D_MODEL), _row_spec(tm, 2 * D_MODEL),
                   _heads_spec(Q_HEADS, tm), _heads_spec(Q_HEADS, tm)],
        out_shape=[bf, bf, jax.ShapeDtypeStruct((s, 2 * D_MODEL), BF16), heads, heads],
        args=(do, gates, ua, ub, w_out, wbr_s))


def _qk_bwd(dqa, dka, dva, dqb, dkb, dvb, qkraw, dgates, gq, gk, bd, tabs, rider=None, tm=256):
    s = qkraw.shape[0]

    def body(dqa_ref, dka_ref, dva_ref, dqb_ref, dkb_ref, dvb_ref, raw_ref, dg_ref, gq_ref, gk_ref, bd_ref,
             ca, la, ha, cb, lb, hb, dp_ref, acc_ref, pair_ref):
        bdm = bd_ref[...]
        tab_a = (ca[...], la[...], ha[...])
        tab_b = (cb[...], lb[...], hb[...])

        def pair(ref, first):
            pair_ref[:, :HEAD_DIM] = ref[first]
            pair_ref[:, HEAD_DIM:] = ref[first + 1]
            return pair_ref[...]

        def norm_rope_a_bwd(dz, raw, gain):
            dzn = _rope(dz, *tab_a, 16, sign=-1.0)
            rinv = lax.rsqrt(_head_mean(raw * raw, bdm) + NORM_EPS)
            zhat = raw * rinv
            dzhat = dzn * gain
            return rinv * (dzhat - zhat * _head_mean(dzhat * zhat, bdm)), dzn * zhat

        gq_rows = jnp.zeros((tm, LANES), F32)
        for i in range(Q_HEADS // 2):
            at = slice(LANES * i, LANES * (i + 1))
            draw, gsum = norm_rope_a_bwd(pair(dqa_ref, 2 * i) * Q_SCALE, raw_ref[:, at], gq_ref[...])
            dp_ref[:, at] = draw.astype(BF16)
            gq_rows += gsum
        off = BRANCH_W
        draw, gk_rows = norm_rope_a_bwd(pair(dka_ref, 0), raw_ref[:, off:off + LANES], gk_ref[...])
        dp_ref[:, off:off + LANES] = draw.astype(BF16)
        off += KV_W
        dp_ref[:, off:off + LANES] = pair(dva_ref, 0).astype(BF16)
        off += KV_W
        for i in range(Q_HEADS // 2):
            dz = _rope(pair(dqb_ref, 2 * i) * Q_SCALE, *tab_b, 32, sign=-1.0)
            dp_ref[:, off + LANES * i:off + LANES * (i + 1)] = dz.astype(BF16)
        off += BRANCH_W
        dp_ref[:, off:off + LANES] = _rope(pair(dkb_ref, 0), *tab_b, 32, sign=-1.0).astype(BF16)
        off += KV_W
        dp_ref[:, off:off + LANES] = pair(dvb_ref, 0).astype(BF16)
        dp_ref[:, QK_W:] = dg_ref[...]

        @pl.when(pl.program_id(0) == 0)
        def _():
            acc_ref[...] = jnp.zeros_like(acc_ref)

        acc_ref[0:1, :] += jnp.sum(gq_rows, axis=0, keepdims=True)
        acc_ref[1:2, :] += jnp.sum(gk_rows, axis=0, keepdims=True)

    tab_spec = _row_spec(tm, LANES)
    return _hosted(
        body, rider, name="qk_bwd", grid=(s // tm,),
        in_specs=[_heads_spec(Q_HEADS, tm), _heads_spec(KV_HEADS, tm), _heads_spec(KV_HEADS, tm),
                  _heads_spec(Q_HEADS, tm), _heads_spec(KV_HEADS, tm), _heads_spec(KV_HEADS, tm),
                  _row_spec(tm, BRANCH_W + KV_W), _row_spec(tm, 2 * D_MODEL),
                  _full_spec(gq.shape), _full_spec(gk.shape), _full_spec(bd.shape)] + [tab_spec] * 6,
        out_specs=[_row_spec(tm, IN_W), pl.BlockSpec((ACC_ROWS, LANES), lambda i: (0, 0))],
        out_shape=[jax.ShapeDtypeStruct((s, IN_W), BF16), jax.ShapeDtypeStruct((ACC_ROWS, LANES), F32)],
        scratch_shapes=[pltpu.VMEM((tm, LANES), F32)],
        args=(dqa, dka, dva, dqb, dkb, dvb, qkraw, dgates, gq, gk, bd, *tabs))


def _in_proj_bwd(dproj, x, dx1, mod6, g1, w_in_s, tm=256):
    s = x.shape[0]
    w = IN_W // N_SHARD

    def body(dp_ref, x_ref, dx1_ref, mod_ref, g_ref, w_ref, gx_ref, acc_ref):
        dh = _nt(dp_ref[:, :w], w_ref[0])
        for j in range(1, N_SHARD):
            dh += _nt(dp_ref[:, w * j:w * (j + 1)], w_ref[j])
        dx, sums = _norm_bwd(dh, x_ref[...], g_ref[...], mod_ref[1:2, :])
        gx_ref[...] = dx1_ref[...] + dx
        _acc_add(acc_ref, sums)

    return _call(
        body, name="in_proj_bwd", grid=(s // tm,),
        in_specs=[_row_spec(tm, IN_W), _row_spec(tm, D_MODEL), _row_spec(tm, D_MODEL),
                  _full_spec(mod6.shape), _full_spec(g1.shape), _full_spec(w_in_s.shape)],
        out_specs=[_row_spec(tm, D_MODEL), _acc_spec()],
        out_shape=[jax.ShapeDtypeStruct((s, D_MODEL), F32), jax.ShapeDtypeStruct((ACC_ROWS, D_MODEL), F32)],
        compiler_params=_params(("arbitrary",)),
    )(dproj, x, dx1, mod6, g1, w_in_s)


def _wgrad(name, a, b, out_shape, out_spec, tm, tn, tk=4096):
    s, m = a.shape
    n = b.shape[1]
    tk = min(tk, s)
    nk = s // tk

    def body(a_ref, b_ref, o_ref, acc_ref):
        k = pl.program_id(2)

        @pl.when(k == 0)
        def _():
            acc_ref[...] = jnp.zeros_like(acc_ref)

        acc_ref[...] += _tn(a_ref[...], b_ref[...])

        @pl.when(k == nk - 1)
        def _():
            o_ref[...] = acc_ref[...].reshape(o_ref.shape)

    return _call(
        body, name=name, grid=(m // tm, n // tn, nk),
        in_specs=[pl.BlockSpec((tk, tm), lambda i, j, k: (k, i)), pl.BlockSpec((tk, tn), lambda i, j, k: (k, j))],
        out_specs=out_spec, out_shape=jax.ShapeDtypeStruct(out_shape, F32),
        scratch_shapes=[pltpu.VMEM((tm, tn), F32)],
        compiler_params=_params(("parallel", "parallel", "arbitrary")),
    )(a, b)


def _wgrad_branch(ya, yb, dua, dub, tk=2048):
    s = ya.shape[0]
    tk = min(tk, s)
    nk = s // tk
    w = D_MODEL // N_SHARD

    def body(ya_ref, yb_ref, dua_ref, dub_ref, o_ref, acc_ref):
        b, k = pl.program_id(0), pl.program_id(2)

        @pl.when(k == 0)
        def _():
            acc_ref[...] = jnp.zeros_like(acc_ref)

        @pl.when(b == 0)
        def _():
            acc_ref[...] += _tn(ya_ref[...], dua_ref[...])

        @pl.when(b == 1)
        def _():
            acc_ref[...] += _tn(yb_ref[...], dub_ref[...])

        @pl.when(k == nk - 1)
        def _():
            o_ref[...] = acc_ref[...]

    y_spec = pl.BlockSpec((tk, BRANCH_W), lambda b, j, k: (k, 0))
    du_spec = pl.BlockSpec((tk, w), lambda b, j, k: (k, j))
    return _call(
        body, name="wgrad_branch", grid=(2, N_SHARD, nk),
        in_specs=[y_spec, y_spec, du_spec, du_spec],
        out_specs=pl.BlockSpec((None, None, BRANCH_W, w), lambda b, j, k: (j, b, 0, 0)),
        out_shape=jax.ShapeDtypeStruct((N_SHARD, 2, BRANCH_W, w), F32),
        scratch_shapes=[pltpu.VMEM((BRANCH_W, w), F32)],
        compiler_params=_params(("parallel", "parallel", "arbitrary")),
    )(ya, yb, dua, dub)


def _local_step(x, target, mod6, g1, g2, gf, gq2, gk2, sink, w_in_s, rest, cj=None):
    s = x.shape[0]
    dist = cj is not None
    tabs = _rope_tables(s)
    bd = _block_diag()
    sink_col = jnp.repeat(sink.reshape(KV_HEADS, GROUP, 1), TQ_B, axis=1).reshape(KV_HEADS, GROUP * TQ_B, 1)
    shard = D_MODEL // N_SHARD

    h, qkraw, qa, ka, va, qb, kb, vb, gates = _in_proj(x, mod6, g1, w_in_s, gq2, gk2, bd, tabs)
    bias = _window_bias(TQ_B)
    (yb, yb_heads, lse_b), gathered = _attn_b_fwd(qb, kb, vb, sink_col, bias, rider=_gather_rider(rest) if dist else None)
    wbr_s, w_out, w_mi_s, w_mo = gathered if dist else rest
    wbr_s = wbr_s.reshape(N_SHARD, 2, BRANCH_W, shard)
    w_out = w_out.reshape(D_MODEL, D_MODEL)
    w_mo = w_mo.reshape(D_FF, D_MODEL)
    ya, ya_heads, lse_a = _attn_a_fwd(qa, ka, va)
    ua, ub, merged, o, x1 = _post_attn(ya, yb, gates, x, mod6, wbr_s, w_out)
    h2, a, hid = _mlp_in(x1, mod6, g2, w_mi_s)
    dx2, dm, da, acc_out = _mlp_out_loss(hid, x1, a, target, mod6, gf, w_mo)

    g_w_mo = _wgrad("wgrad_mlp_out", hid, dm, (D_FF, D_MODEL), pl.BlockSpec((D_MODEL, D_MODEL), lambda i, j, k: (i, 0)),
                    D_MODEL, D_MODEL).reshape(N_SHARD, D_MODEL, D_MODEL)
    g_w_mi = _wgrad("wgrad_mlp_in", h2, da, (N_SHARD, D_MODEL, D_MODEL),
                    pl.BlockSpec((None, D_MODEL, D_MODEL), lambda i, j, k: (j, i, 0)), D_MODEL, D_MODEL)
    mlp = _Reduction(("mlp_out", "mlp_in"), (g_w_mo, g_w_mi), cj)
    (dx1, do, acc_mlp), got = _mlp_bwd(da, x1, dx2, o, mod6, g2, w_mi_s, rider=mlp.swap() if dist else None)
    (dua, dub, dgates, doa, dob), landed = _merge_bwd(do, gates, ua, ub, w_out, wbr_s, rider=mlp.add(got) if dist else None)
    g_w_out = _wgrad("wgrad_out", merged, do, (D_MODEL, D_MODEL), pl.BlockSpec((D_MODEL, D_MODEL), lambda i, j, k: (i, 0)),
                     D_MODEL, D_MODEL).reshape(N_SHARD, shard, D_MODEL)
    g_wbr = _wgrad_branch(ya, yb, dua, dub)
    out = _Reduction(("out", "branch"), (g_w_out, g_wbr.reshape(N_SHARD, 2 * BRANCH_W, shard)), cj)
    (dqb, dkb, dvb, dsink), landings = _attn_b_bwd(qb, kb, vb, yb_heads, dob, lse_b, sink_col, bias,
                                                   rider=_riders(out.swap(), mlp.total(landed)) if dist else None)
    dqa, dka, dva = _attn_a_bwd(qa, ka, va, ya_heads, doa, lse_a)
    heads = (Q_HEADS, s, HEAD_DIM)
    (dproj, acc_qk), landed = _qk_bwd(dqa.reshape(heads), dka, dva, dqb.reshape(heads), dkb, dvb, qkraw, dgates, gq2, gk2, bd,
                                      tabs, rider=out.add(landings[:2]) if dist else None)
    w = IN_W // N_SHARD
    g_w_in = _wgrad("wgrad_in", h, dproj, (N_SHARD, D_MODEL, w), pl.BlockSpec((None, D_MODEL, w), lambda i, j, k: (j, i, 0)),
                    D_MODEL, w)
    grad_x, acc_in = _in_proj_bwd(dproj, x, dx1, mod6, g1, w_in_s)
    accs = (acc_out, acc_mlp, acc_in, acc_qk, dsink)
    if not dist:
        return grad_x, (g_w_in, g_wbr, g_w_out, g_w_mi, g_w_mo), accs
    first = _Reduction(("in",), (g_w_in,), cj)
    landed_in = _alone("scatter_in", first.add(_alone("swap_in", first.swap())))
    r_in, r_out, r_br = _alone("join_in_out_branch", _riders(first.total(landed_in), out.total(landed)))
    r_mo, r_mi = landings[2:]
    return grad_x, (r_in, r_br, r_out, r_mi, r_mo), accs


def _me():
    return lax.axis_index("x"), lax.axis_index("y"), lax.axis_index("c")


def _peer(d):
    x, y, c = _me()
    return (1 - x if d & 4 else x, 1 - y if d & 2 else y, 1 - c if d & 1 else c)


def _dev_index(p):
    return 4 * p[0] + 2 * p[1] + p[2]


def _chip_index(p):
    return 2 * p[0] + p[1]


def _remote(src, dst, send_sem, recv_sem, to):
    return pltpu.make_async_remote_copy(src_ref=src, dst_ref=dst, send_sem=send_sem, recv_sem=recv_sem,
                                        device_id=to, device_id_type=MESH)


SLOT_ROWS = 8


def _ada_fwd(c, w_ada, b4):
    cols = w_ada.shape[1]

    def body(c_ref, w_ref, b_ref, mod_ref, sc_ref, cbuf, pbuf, mbuf, send1, recv1, send2, recv2):
        me = _me()
        mine, chip = _dev_index(me), _chip_index(me)
        cbuf[mine] = jnp.broadcast_to(c_ref[...], (SLOT_ROWS, D_MODEL))
        gather = [_remote(cbuf.at[mine], cbuf.at[mine], send1.at[d - 1], recv1.at[d - 1], _peer(d)) for d in range(1, N_DEV)]
        for cp in gather:
            cp.start()
        for d in range(1, N_DEV):
            _remote(cbuf.at[mine], cbuf.at[_dev_index(_peer(d))], send1.at[d - 1], recv1.at[d - 1], _peer(d)).wait_recv()
        call = cbuf[...].reshape(N_DEV * SLOT_ROWS, D_MODEL)
        sc = call * _sigmoid(call)
        for s in range(N_DEV):
            sc_ref[s:s + 1, :] = sc[SLOT_ROWS * s:SLOT_ROWS * s + 1]
        part = _nn(sc.astype(BF16), w_ref[...].astype(BF16)) + b_ref[pl.ds(chip, 1), :]
        pbuf[...] = part.reshape(N_DEV, SLOT_ROWS, cols)
        mbuf[chip] = pbuf[mine]
        spread = [_remote(pbuf.at[_dev_index(_peer(d))], mbuf.at[chip], send2.at[d // 2 - 1], recv2.at[d // 2 - 1], _peer(d))
                  for d in (2, 4, 6)]
        for cp in spread:
            cp.start()
        for d in (2, 4, 6):
            _remote(pbuf.at[mine], mbuf.at[_chip_index(_peer(d))], send2.at[d // 2 - 1], recv2.at[d // 2 - 1],
                    _peer(d)).wait_recv()
        half = D_MODEL // 2
        for p in range(2 * 6):
            col = half * p
            mod_ref[p // 2:p // 2 + 1, half * (p % 2):half * (p % 2 + 1)] = mbuf[col // cols, 0:1, col % cols:col % cols + half]
        for cp in gather + spread:
            cp.wait_send()

    vm = pl.BlockSpec(memory_space=pltpu.VMEM)
    return _call(
        body, name="ada_fwd", in_specs=[vm, vm, vm], out_specs=[vm, vm],
        out_shape=[jax.ShapeDtypeStruct((6, D_MODEL), F32), jax.ShapeDtypeStruct((N_DEV, D_MODEL), F32)],
        scratch_shapes=[pltpu.VMEM((N_DEV, SLOT_ROWS, D_MODEL), F32), pltpu.VMEM((N_DEV, SLOT_ROWS, cols), F32),
                        pltpu.VMEM((N_SHARD, SLOT_ROWS, cols), F32),
                        pltpu.SemaphoreType.DMA((N_DEV - 1,)), pltpu.SemaphoreType.DMA((N_DEV - 1,)),
                        pltpu.SemaphoreType.DMA((N_SHARD - 1,)), pltpu.SemaphoreType.DMA((N_SHARD - 1,))],
        compiler_params=_params(),
    )(c, w_ada, b4)


PACK_ROWS = 16
PACK_W = 3 * D_MODEL


def _ada_bwd(acc_out, acc_mlp, acc_in, acc_qk, dsink, sc_all):
    cols = 6 * D_MODEL // N_SHARD

    def body(out_ref, mlp_ref, in_ref, qk_ref, dsink_ref, sc_ref,
             gwa_ref, gba_ref, gn1_ref, gn2_ref, gf_ref, gq_ref, gk_ref, gs_ref, blk, send, recv):
        me = _me()
        mine, chip = _dev_index(me), _chip_index(me)
        blk[mine] = jnp.zeros((PACK_ROWS, PACK_W), F32)
        dmod = (in_ref, 0), (in_ref, 1), (mlp_ref, 3), (mlp_ref, 0), (mlp_ref, 1), (out_ref, 2)
        half = D_MODEL // 2
        for p in range(2 * 6):
            ref, row = dmod[p // 2]
            col = half * p
            blk[mine, col // cols:col // cols + 1, col % cols:col % cols + half] = ref[row:row + 1, half * (p % 2):half * (p % 2 + 1)]
        blk[mine, 4:5, 0:D_MODEL] = in_ref[2:3, :]
        blk[mine, 4:5, D_MODEL:2 * D_MODEL] = mlp_ref[2:3, :]
        blk[mine, 4:5, 2 * D_MODEL:] = out_ref[1:2, :]
        blk[mine, 5:6, 0:LANES] = qk_ref[0:1, :]
        blk[mine, 5:6, LANES:2 * LANES] = qk_ref[1:2, :]
        for g in range(KV_HEADS):
            blk[mine, 8 + GROUP * g:8 + GROUP * (g + 1), 0:LANES] = dsink_ref[g, 0:GROUP, :]
        copies = [_remote(blk.at[mine], blk.at[mine], send.at[d - 1], recv.at[d - 1], _peer(d)) for d in range(1, N_DEV)]
        for cp in copies:
            cp.start()
        for d in range(1, N_DEV):
            _remote(blk.at[mine], blk.at[_dev_index(_peer(d))], send.at[d - 1], recv.at[d - 1], _peer(d)).wait_recv()
        tot = blk[0]
        for s in range(1, N_DEV):
            tot = tot + blk[s]
        for j in range(N_SHARD):
            gba_ref[:, cols * j:cols * (j + 1)] = tot[j:j + 1, :cols]
        gn1_ref[...] = tot[4:5, 0:D_MODEL]
        gn2_ref[...] = tot[4:5, D_MODEL:2 * D_MODEL]
        gf_ref[...] = tot[4:5, 2 * D_MODEL:]
        gq_ref[...] = tot[5:6, 0:HEAD_DIM] + tot[5:6, HEAD_DIM:2 * HEAD_DIM]
        gk_ref[...] = tot[5:6, LANES:LANES + HEAD_DIM] + tot[5:6, LANES + HEAD_DIM:2 * LANES]
        sq = tot[8:16, 0:Q_HEADS]
        diag = lax.broadcasted_iota(jnp.int32, sq.shape, 0) == lax.broadcasted_iota(jnp.int32, sq.shape, 1)
        gs_ref[...] = jnp.sum(jnp.where(diag, sq, 0.0), axis=0, keepdims=True)
        dm = jnp.concatenate([blk[s, pl.ds(chip, 1), pl.ds(0, cols)] for s in range(N_DEV)], axis=0)
        gwa_ref[...] = _tn(sc_ref[...], dm)
        for cp in copies:
            cp.wait_send()

    vm = pl.BlockSpec(memory_space=pltpu.VMEM)
    row = lambda n: jax.ShapeDtypeStruct((1, n), F32)
    return _call(
        body, name="ada_bwd", in_specs=[vm] * 6, out_specs=[vm] * 8,
        out_shape=[jax.ShapeDtypeStruct((D_MODEL, cols), F32), row(6 * D_MODEL), row(D_MODEL), row(D_MODEL), row(D_MODEL),
                   row(HEAD_DIM), row(HEAD_DIM), row(Q_HEADS)],
        scratch_shapes=[pltpu.VMEM((N_DEV, PACK_ROWS, PACK_W), F32),
                        pltpu.SemaphoreType.DMA((N_DEV - 1,)), pltpu.SemaphoreType.DMA((N_DEV - 1,))],
        compiler_params=_params(),
    )(acc_out, acc_mlp, acc_in, acc_qk, dsink, sc_all)


def _cast_weights(ws):
    n = len(ws)

    def body(*refs):
        src, out, tmp, sems = refs[:n], refs[n:2 * n], refs[2 * n:3 * n], refs[3 * n]
        chip = _chip_index(_me())
        copies = []
        for a in range(n):
            tmp[a][...] = src[a][...].astype(BF16)
            cp = pltpu.make_async_copy(tmp[a], out[a].at[chip], sems.at[a])
            cp.start()
            copies.append(cp)
        for cp in copies:
            cp.wait()

    vm = pl.BlockSpec(memory_space=pltpu.VMEM)
    return _call(
        body, name="cast_weights", in_specs=[vm] * n, out_specs=[ANY] * n,
        out_shape=[jax.ShapeDtypeStruct((N_SHARD,) + w.shape, BF16) for w in ws],
        scratch_shapes=[pltpu.VMEM(w.shape, BF16) for w in ws] + [pltpu.SemaphoreType.DMA((n,))],
        compiler_params=_params(),
    )(*ws)


def _half_rows(ref_rows, c):
    half = ref_rows // 2
    return pl.ds(pl.multiple_of(c * half, 8), half)


class _Rider:
    def __init__(self, inputs, out_shape, aliases, n_sems, start, finish, middle=None):
        self.inputs, self.out_shape, self.aliases, self.n_sems = list(inputs), list(out_shape), dict(aliases), n_sems
        self.start, self.finish, self.middle = start, finish, middle


def _riders(*rs):
    ins = [0]
    outs = [0]
    sems = [0]
    for r in rs:
        ins.append(ins[-1] + len(r.inputs))
        outs.append(outs[-1] + len(r.out_shape))
        sems.append(sems[-1] + r.n_sems)

    def phase(which):
        def run(in_refs, out_refs, sem):
            for k, r in enumerate(rs):
                fn = getattr(r, which)
                if fn is not None:
                    fn(in_refs[ins[k]:ins[k + 1]], out_refs[outs[k]:outs[k + 1]], lambda j, base=sems[k]: sem(base + j))
        return run

    aliases = {ins[k] + i: outs[k] + o for k, r in enumerate(rs) for i, o in r.aliases.items()}
    return _Rider([a for r in rs for a in r.inputs], [o for r in rs for o in r.out_shape], aliases, sems[-1],
                  phase("start"), phase("finish"), phase("middle") if any(r.middle for r in rs) else None)


def _hosted(body, rider, *, name, grid, in_specs, out_specs, out_shape, args, scratch_shapes=(), middle_at=None):
    sem = ("arbitrary",) * len(grid)
    if rider is None:
        res = _call(body, name=name, grid=grid, in_specs=in_specs, out_specs=out_specs, out_shape=out_shape,
                    scratch_shapes=list(scratch_shapes), compiler_params=_params(sem))(*args)
        return res, ()
    n_in, n_out, n_scr = len(in_specs), len(out_specs), len(scratch_shapes)
    r_in, r_out = len(rider.inputs), len(rider.out_shape)

    def riding(*refs):
        at = 0
        parts = []
        for size in (n_in, r_in, n_out, r_out, n_scr):
            parts.append(refs[at:at + size])
            at += size
        ins, rider_ins, outs, rider_outs, scratch = parts
        sems = refs[at]
        step = pl.program_id(0)
        for axis in range(1, len(grid)):
            step = step * grid[axis] + pl.program_id(axis)
        steps = 1
        for size in grid:
            steps *= size

        def sem_at(k):
            return sems.at[k]

        @pl.when(step == 0)
        def _():
            rider.start(rider_ins, rider_outs, sem_at)

        body(*ins, *outs, *scratch)
        if rider.middle is not None:
            @pl.when(step == middle_at)
            def _():
                rider.middle(rider_ins, rider_outs, sem_at)

        @pl.when(step == steps - 1)
        def _():
            rider.finish(rider_ins, rider_outs, sem_at)

    res = _call(
        riding, name=name, grid=grid, in_specs=list(in_specs) + [ANY] * r_in, out_specs=list(out_specs) + [ANY] * r_out,
        out_shape=list(out_shape) + rider.out_shape,
        input_output_aliases={n_in + i: n_out + o for i, o in rider.aliases.items()},
        scratch_shapes=list(scratch_shapes) + [pltpu.SemaphoreType.DMA((rider.n_sems,))],
        compiler_params=_params(sem),
    )(*args, *rider.inputs)
    return res[:n_out], res[n_out:]


def _alone(name, rider):
    n_in, n_out = len(rider.inputs), len(rider.out_shape)

    def body(*refs):
        ins, outs, sems = refs[:n_in], refs[n_in:n_in + n_out], refs[n_in + n_out]

        def sem_at(k):
            return sems.at[k]

        rider.start(ins, outs, sem_at)
        if rider.middle is not None:
            rider.middle(ins, outs, sem_at)
        rider.finish(ins, outs, sem_at)

    return _call(
        body, name=name, in_specs=[ANY] * n_in, out_specs=[ANY] * n_out, out_shape=rider.out_shape,
        input_output_aliases=rider.aliases, scratch_shapes=[pltpu.SemaphoreType.DMA((rider.n_sems,))],
    )(*rider.inputs)


OTHER_CHIPS = (2, 4, 6)


def _gather_rider(stacked):
    n = len(stacked)

    def flights(bufs, sem):
        me = _me()
        chip, sib = _chip_index(me), _peer(1)
        out = []
        for a in range(n):
            mine, theirs = (_half_rows(bufs[a].shape[1], c) for c in (me[2], 1 - me[2]))
            for j, d in enumerate(OTHER_CHIPS):
                k = 3 * a + j
                from_chip = _chip_index(_peer(d))
                own, landed, passed = bufs[a].at[chip, mine], bufs[a].at[from_chip, mine], bufs[a].at[from_chip, theirs]
                out.append((_remote(own, own, sem(k), sem(3 * n + k), _peer(d)),
                            _remote(own, landed, sem(k), sem(3 * n + k), _peer(d)),
                            _remote(landed, landed, sem(6 * n + k), sem(9 * n + k), sib),
                            _remote(passed, passed, sem(6 * n + k), sem(9 * n + k), sib)))
        return out

    def start(ins, outs, sem):
        for send, _, _, _ in flights(outs, sem):
            send.start()

    def middle(ins, outs, sem):
        for _, arrival, pass_on, _ in flights(outs, sem):
            arrival.wait_recv()
            pass_on.start()

    def finish(ins, outs, sem):
        every = flights(outs, sem)
        for _, _, _, passed_to_me in every:
            passed_to_me.wait_recv()
        for send, _, pass_on, _ in every:
            send.wait_send()
            pass_on.wait_send()

    return _Rider(stacked, [jax.ShapeDtypeStruct(w.shape, w.dtype) for w in stacked], {a: a for a in range(n)}, 12 * n,
                  start, finish, middle)


def _swap_rider(grads):
    n = len(grads)

    def copies(ins, outs, sem):
        c = _me()[2]
        return [_remote(ins[a].at[pl.ds(0, N_SHARD), _half_rows(ins[a].shape[1], 1 - c)], outs[a], sem(a), sem(n + a), _peer(1))
                for a in range(n)]

    def start(ins, outs, sem):
        for cp in copies(ins, outs, sem):
            cp.start()

    def finish(ins, outs, sem):
        for cp in copies(ins, outs, sem):
            cp.wait()

    return _Rider(grads, [jax.ShapeDtypeStruct((N_SHARD, g.shape[1] // 2, g.shape[2]), F32) for g in grads], {}, 2 * n,
                  start, finish)


def _row_tile(rows):
    return min(rows, 256)


def _add_halves(name, g, got, cj):
    _, half, cols = got.shape
    tr = _row_tile(half)
    nt = half // tr

    def body(cj_ref, g_ref, got_ref, o_ref):
        o_ref[...] = (g_ref[...] + got_ref[...]).astype(BF16)

    spec = pl.BlockSpec((None, tr, cols), lambda i, s, cj: (s, i, 0))
    return _call(
        body, name=name,
        grid_spec=pltpu.PrefetchScalarGridSpec(
            num_scalar_prefetch=1, grid=(nt, N_SHARD),
            in_specs=[pl.BlockSpec((None, tr, cols), lambda i, s, cj: (s, cj[0] * nt + i, 0)), spec], out_specs=spec),
        out_shape=jax.ShapeDtypeStruct(got.shape, BF16), compiler_params=_params(("parallel", "parallel")),
    )(cj, g, got)


def _scatter_rider(sums):
    n = len(sums)

    def flights(ins, outs, sem):
        chip = _chip_index(_me())
        out = []
        for a in range(n):
            for j, d in enumerate(OTHER_CHIPS):
                k = 3 * a + j
                other = _chip_index(_peer(d))
                out.append((_remote(ins[a].at[other], outs[a].at[chip], sem(k), sem(3 * n + k), _peer(d)),
                            _remote(ins[a].at[chip], outs[a].at[other], sem(k), sem(3 * n + k), _peer(d))))
        return out

    def start(ins, outs, sem):
        for send, _ in flights(ins, outs, sem):
            send.start()

    def finish(ins, outs, sem):
        every = flights(ins, outs, sem)
        for _, arrival in every:
            arrival.wait_recv()
        for send, _ in every:
            send.wait_send()

    return _Rider(sums, [jax.ShapeDtypeStruct(v.shape, v.dtype) for v in sums], {}, 6 * n, start, finish)


def _sum_chips(name, g, got, landed, cj):
    _, half, cols = got.shape
    tr = _row_tile(half)
    nt = half // tr

    def body(cj_ref, g_ref, got_ref, landed_ref, o_ref):
        own = g_ref[...] + got_ref[...]
        total = None
        for s in range(N_SHARD):
            term = jnp.where(cj_ref[1] == s, own, landed_ref[s].astype(F32))
            total = term if total is None else total + term
        o_ref[...] = total

    return _call(
        body, name=name,
        grid_spec=pltpu.PrefetchScalarGridSpec(
            num_scalar_prefetch=1, grid=(nt,),
            in_specs=[pl.BlockSpec((None, tr, cols), lambda i, cj: (cj[1], cj[0] * nt + i, 0)),
                      pl.BlockSpec((None, tr, cols), lambda i, cj: (cj[1], i, 0)),
                      pl.BlockSpec((N_SHARD, tr, cols), lambda i, cj: (0, i, 0))],
            out_specs=pl.BlockSpec((tr, cols), lambda i, cj: (cj[0] * nt + i, 0))),
        out_shape=jax.ShapeDtypeStruct((2 * half, cols), F32), compiler_params=_params(("parallel",)),
    )(cj, g, got, landed)


def _join_rider(shards):
    n = len(shards)

    def flights(bufs, sem):
        c = _me()[2]
        out = []
        for a in range(n):
            mine, theirs = (bufs[a].at[_half_rows(bufs[a].shape[0], cc)] for cc in (c, 1 - c))
            out.append((_remote(mine, mine, sem(a), sem(n + a), _peer(1)), _remote(theirs, theirs, sem(a), sem(n + a), _peer(1))))
        return out

    def start(ins, outs, sem):
        for send, _ in flights(outs, sem):
            send.start()

    def finish(ins, outs, sem):
        for send, arrival in flights(outs, sem):
            arrival.wait_recv()
            send.wait_send()

    return _Rider(shards, [jax.ShapeDtypeStruct(h.shape, F32) for h in shards], {a: a for a in range(n)}, 2 * n, start, finish)


class _Reduction:
    def __init__(self, names, grads, cj):
        self.names, self.grads, self.cj = names, list(grads), cj

    def swap(self):
        return _swap_rider(self.grads)

    def add(self, got):
        self.got = list(got)
        self.sums = [_add_halves("add_halves_" + nm, g, h, self.cj) for nm, g, h in zip(self.names, self.grads, self.got)]
        return _scatter_rider(self.sums)

    def total(self, landed):
        halves = [_sum_chips("sum_chips_" + nm, g, h, l, self.cj)
                  for nm, g, h, l in zip(self.names, self.grads, self.got, landed)]
        return _join_rider(halves)


def _adamw_math(w, g, m, v):
    m = ADAM_B1 * m + (1.0 - ADAM_B1) * g
    v = ADAM_B2 * v + (1.0 - ADAM_B2) * jnp.square(g)
    m_hat = m / (1.0 - ADAM_B1 ** ADAM_STEP)
    v_hat = v / (1.0 - ADAM_B2 ** ADAM_STEP)
    return -ADAM_LR * (m_hat / (jnp.sqrt(v_hat) + ADAM_EPS) + ADAM_WD * w), m, v


def _adamw(name, w, g, m, v):
    rows, cols = w.shape
    tr = _row_tile(rows)

    def body(w_ref, g_ref, m_ref, v_ref, d_ref, nm_ref, nv_ref):
        d_ref[...], nm_ref[...], nv_ref[...] = _adamw_math(w_ref[...], g_ref[...], m_ref[...], v_ref[...])

    spec = pl.BlockSpec((tr, cols), lambda i: (i, 0))
    return _call(
        body, name=name, grid=(rows // tr,), in_specs=[spec] * 4, out_specs=[spec] * 3,
        out_shape=[jax.ShapeDtypeStruct(w.shape, F32)] * 3, compiler_params=_params(("parallel",)),
    )(w, g, m, v)


def _adamw_small(ws, gs, ms, vs):
    n = len(ws)

    def body(*refs):
        ins, outs = refs[:4 * n], refs[4 * n:]
        for a in range(n):
            w, g, m, v = (ins[k * n + a][...] for k in range(4))
            outs[a][...], outs[n + a][...], outs[2 * n + a][...] = _adamw_math(w, g, m, v)

    vm = pl.BlockSpec(memory_space=pltpu.VMEM)
    res = _call(
        body, name="adamw_small", in_specs=[vm] * (4 * n), out_specs=[vm] * (3 * n),
        out_shape=[jax.ShapeDtypeStruct(w.shape, F32) for w in ws] * 3, compiler_params=_params(),
    )(*ws, *gs, *ms, *vs)
    return res[:n], res[n:2 * n], res[2 * n:]


def kernel(x, c, w_ada, b_ada, norm1_g, w_in, q_norm_a, k_norm_a, sink_b, w_branch, w_out, norm2_g, w_mlp_in, w_mlp_out, final_g, loss_target, m_w_ada, m_b_ada, m_norm1_g, m_w_in, m_q_norm_a, m_k_norm_a, m_sink_b, m_w_branch, m_w_out, m_norm2_g, m_w_mlp_in, m_w_mlp_out, m_final_g, v_w_ada, v_b_ada, v_norm1_g, v_w_in, v_q_norm_a, v_k_norm_a, v_sink_b, v_w_branch, v_w_out, v_norm2_g, v_w_mlp_in, v_w_mlp_out, v_final_g):
    xi, yi, ci = _me()
    cj = jnp.stack([ci, 2 * xi + yi]).astype(jnp.int32)
    n_cols = 6 * D_MODEL // N_SHARD

    mod6, sc_all = _ada_fwd(c, w_ada[0], b_ada.reshape(N_SHARD, n_cols))

    def rows2d(a):
        return a.reshape(-1, a.shape[-1])

    big = (w_in, w_branch, w_out, w_mlp_in, w_mlp_out)
    stacked = _cast_weights([rows2d(w) for w in big])
    w_in_s, = _alone("gather_w_in", _gather_rider(stacked[:1]))
    rest = stacked[1:]

    gq2 = jnp.tile(q_norm_a, (1, 2))
    gk2 = jnp.tile(k_norm_a, (1, 2))
    grad_x, g_big, (acc_out, acc_mlp, acc_in, acc_qk, dsink) = _local_step(
        x[0], loss_target[0], mod6, norm1_g, norm2_g, final_g.reshape(1, D_MODEL), gq2, gk2, sink_b[0], w_in_s, rest, cj)

    loss = lax.psum(0.5 * jnp.sum(acc_out[0]) / D_MODEL, ("x", "y", "c"))
    g_w_ada, g_b_ada, g_n1, g_n2, g_f, g_q, g_k, g_s = _ada_bwd(acc_out, acc_mlp, acc_in, acc_qk, dsink, sc_all)

    names = ("w_ada", "w_in", "w_branch", "w_out", "w_mlp_in", "w_mlp_out")
    big_w = [w_ada[0]] + [rows2d(w) for w in big]
    big_g = [g_w_ada] + list(g_big)
    big_m = [rows2d(m) for m in (m_w_ada, m_w_in, m_w_branch, m_w_out, m_w_mlp_in, m_w_mlp_out)]
    big_v = [rows2d(v) for v in (v_w_ada, v_w_in, v_w_branch, v_w_out, v_w_mlp_in, v_w_mlp_out)]
    big_res = {nm: _adamw("adamw_" + nm, w, g, m, v) for nm, w, g, m, v in zip(names, big_w, big_g, big_m, big_v)}

    small = ("b_ada", "norm1_g", "q_norm_a", "k_norm_a", "sink_b", "norm2_g", "final_g")
    row = lambda a: a.reshape(1, -1)
    small_w = [row(a) for a in (b_ada, norm1_g, q_norm_a, k_norm_a, sink_b, norm2_g, final_g)]
    small_g = [g_b_ada, g_n1, g_q, g_k, g_s, g_n2, g_f]
    small_m = [row(a) for a in (m_b_ada, m_norm1_g, m_q_norm_a, m_k_norm_a, m_sink_b, m_norm2_g, m_final_g)]
    small_v = [row(a) for a in (v_b_ada, v_norm1_g, v_q_norm_a, v_k_norm_a, v_sink_b, v_norm2_g, v_final_g)]
    s_d, s_m, s_v = _adamw_small(small_w, small_g, small_m, small_v)

    order = ("w_ada", "b_ada", "norm1_g", "w_in", "q_norm_a", "k_norm_a", "sink_b", "w_branch", "w_out", "norm2_g",
             "w_mlp_in", "w_mlp_out", "final_g")
    like = dict(w_ada=w_ada, b_ada=b_ada, norm1_g=norm1_g, w_in=w_in, q_norm_a=q_norm_a, k_norm_a=k_norm_a, sink_b=sink_b,
                w_branch=w_branch, w_out=w_out, norm2_g=norm2_g, w_mlp_in=w_mlp_in, w_mlp_out=w_mlp_out, final_g=final_g)
    grad, delta, new_m, new_v = {}, {}, {}, {}
    for nm, g in zip(names, big_g):
        grad[nm] = g
        delta[nm], new_m[nm], new_v[nm] = big_res[nm]
    for k, nm in enumerate(small):
        grad[nm], delta[nm], new_m[nm], new_v[nm] = small_g[k], s_d[k], s_m[k], s_v[k]
    outs = [loss, grad_x[None]]
    for group in (grad, delta, new_m, new_v):
        outs += [group[nm].reshape(like[nm].shape) for nm in order]
    return tuple(outs)
```

```python
import functools

import jax
import jax.numpy as jnp
from jax import lax
from jax.experimental import pallas as pl
from jax.experimental.pallas import tpu as pltpu

F32 = jnp.float32
BF16 = jnp.bfloat16
MESH = pl.DeviceIdType.MESH
ANY = pl.BlockSpec(memory_space=pl.ANY)

D_MODEL = 1024
HEAD_DIM = 64
Q_HEADS = 8
KV_HEADS = 2
GROUP = Q_HEADS // KV_HEADS
BRANCH_W = Q_HEADS * HEAD_DIM
KV_W = KV_HEADS * HEAD_DIM
IN_W = 2 * (BRANCH_W + 2 * KV_W) + 2 * D_MODEL
QK_W = 2 * (BRANCH_W + 2 * KV_W)
D_FF = 4 * D_MODEL
GRID_W = 64
WINDOW = 128
ROPE_THETA = 10000.0
NORM_EPS = 1e-6
NEG_INF = -1e30
Q_SCALE = HEAD_DIM ** -0.5
N_SHARD = 4
N_DEV = 8
LANES = 128
VMEM_LIMIT = 56 * 1024 * 1024

ADAM_LR = 0.001
ADAM_B1 = 0.9
ADAM_B2 = 0.999
ADAM_EPS = 1e-08
ADAM_WD = 0.01
ADAM_STEP = 10

_call = pl.pallas_call


def _params(sem=None, vmem=VMEM_LIMIT):
    return pltpu.CompilerParams(dimension_semantics=sem, vmem_limit_bytes=vmem)


def _nt(a, b):
    return lax.dot_general(a, b, (((1,), (1,)), ((), ())), preferred_element_type=F32)


def _tn(a, b):
    return lax.dot_general(a, b, (((0,), (0,)), ((), ())), preferred_element_type=F32)


def _nn(a, b):
    return jnp.dot(a, b, preferred_element_type=F32)


def _sigmoid(z):
    return 0.5 * jnp.tanh(0.5 * z) + 0.5


def _rope_tables(s):
    t = jnp.arange(s, dtype=jnp.int32)
    lane = jnp.arange(LANES, dtype=jnp.int32)

    def cos_sin(pos, dim):
        inv = ROPE_THETA ** (-jnp.arange(0, dim, 2, dtype=F32) / dim)
        ang = pos.astype(F32)[:, None] * inv[None, :]
        return jnp.cos(ang), jnp.sin(ang)

    cr, sr = cos_sin(t // GRID_W, HEAD_DIM // 2)
    cc, sc = cos_sin(t % GRID_W, HEAD_DIM // 2)
    cos_a = jnp.tile(jnp.concatenate([cr, cr, cc, cc], axis=1), (1, 2))
    sin_a = jnp.tile(jnp.concatenate([sr, sr, sc, sc], axis=1), (1, 2))
    first_a = (lane % 32) < 16
    c1, s1 = cos_sin(t, HEAD_DIM)
    cos_b = jnp.tile(jnp.concatenate([c1, c1], axis=1), (1, 2))
    sin_b = jnp.tile(jnp.concatenate([s1, s1], axis=1), (1, 2))
    first_b = (lane % 64) < 32
    tabs_a = (cos_a, jnp.where(first_a, -sin_a, 0.0), jnp.where(first_a, 0.0, sin_a))
    tabs_b = (cos_b, jnp.where(first_b, -sin_b, 0.0), jnp.where(first_b, 0.0, sin_b))
    return tabs_a + tabs_b


def _rope(z, cos, s_lo, s_hi, half, sign=1.0):
    up = pltpu.roll(z, LANES - half, 1)
    dn = pltpu.roll(z, half, 1)
    return z * cos + sign * (up * s_lo + dn * s_hi)


def _head_mean(z2, bd):
    hi = z2.astype(BF16)
    lo = (z2 - hi.astype(F32)).astype(BF16)
    return _nn(hi, bd) + _nn(lo, bd)


def _block_diag():
    lane = jnp.arange(LANES)
    return jnp.where((lane[:, None] // HEAD_DIM) == (lane[None, :] // HEAD_DIM), 1.0 / HEAD_DIM, 0.0).astype(BF16)


def _row_spec(tm, width):
    return pl.BlockSpec((tm, width), lambda i: (i, 0))


def _heads_spec(heads, tm):
    return pl.BlockSpec((heads, tm, HEAD_DIM), lambda i: (0, i, 0))


def _full_spec(shape):
    nd = len(shape)
    return pl.BlockSpec(shape, lambda i: (0,) * nd)


def _in_proj(x, mod6, g1, w_in_s, gq, gk, bd, tabs, tm=256):
    s = x.shape[0]

    def body(x_ref, mod_ref, g1_ref, w_ref, gq_ref, gk_ref, bd_ref, ca, la, ha, cb, lb, hb,
             h_ref, qkraw_ref, qa_ref, ka_ref, va_ref, qb_ref, kb_ref, vb_ref, gate_ref):
        xt = x_ref[...]
        r = lax.rsqrt(jnp.mean(xt * xt, axis=-1, keepdims=True) + NORM_EPS)
        h = (xt * r * g1_ref[...]) * (1.0 + mod_ref[1:2, :]) + mod_ref[0:1, :]
        hb16 = h.astype(BF16)
        h_ref[...] = hb16
        proj = jnp.concatenate([_nn(hb16, w_ref[j]) for j in range(N_SHARD)], axis=1)
        qkraw_ref[...] = proj[:, :BRANCH_W + KV_W]
        bdm = bd_ref[...]
        tab_a = (ca[...], la[...], ha[...])
        tab_b = (cb[...], lb[...], hb[...])

        def norm_rope_a(z, gain):
            zn = z * lax.rsqrt(_head_mean(z * z, bdm) + NORM_EPS) * gain
            return _rope(zn, *tab_a, 16)

        def put(ref, first, z):
            zb = z.astype(BF16)
            ref[first] = zb[:, :HEAD_DIM]
            ref[first + 1] = zb[:, HEAD_DIM:]

        for i in range(Q_HEADS // 2):
            put(qa_ref, 2 * i, norm_rope_a(proj[:, LANES * i:LANES * (i + 1)], gq_ref[...]) * Q_SCALE)
        off = BRANCH_W
        put(ka_ref, 0, norm_rope_a(proj[:, off:off + LANES], gk_ref[...]))
        off += KV_W
        def put_v(ref, z):
            zb = z.astype(BF16)
            for hd in range(KV_HEADS):
                ref[hd, :, :HEAD_DIM] = zb[:, HEAD_DIM * hd:HEAD_DIM * (hd + 1)]
                ref[hd, :, HEAD_DIM:] = jnp.ones((tm, HEAD_DIM), BF16)

        put_v(va_ref, proj[:, off:off + LANES])
        off += KV_W
        for i in range(Q_HEADS // 2):
            put(qb_ref, 2 * i, _rope(proj[:, off + LANES * i:off + LANES * (i + 1)], *tab_b, 32) * Q_SCALE)
        off += BRANCH_W
        put(kb_ref, 0, _rope(proj[:, off:off + LANES], *tab_b, 32))
        off += KV_W
        put_v(vb_ref, proj[:, off:off + LANES])
        gate_ref[...] = proj[:, QK_W:]

    tab_spec = _row_spec(tm, LANES)
    return _call(
        body, name="in_proj", grid=(s // tm,),
        in_specs=[_row_spec(tm, D_MODEL), _full_spec(mod6.shape), _full_spec(g1.shape), _full_spec(w_in_s.shape),
                  _full_spec(gq.shape), _full_spec(gk.shape), _full_spec(bd.shape)] + [tab_spec] * 6,
        out_specs=[_row_spec(tm, D_MODEL), _row_spec(tm, BRANCH_W + KV_W), _heads_spec(Q_HEADS, tm), _heads_spec(KV_HEADS, tm),
                   pl.BlockSpec((KV_HEADS, tm, LANES), lambda i: (0, i, 0)), _heads_spec(Q_HEADS, tm),
                   _heads_spec(KV_HEADS, tm), pl.BlockSpec((KV_HEADS, tm, LANES), lambda i: (0, i, 0)),
                   _row_spec(tm, 2 * D_MODEL)],
        out_shape=[jax.ShapeDtypeStruct((s, D_MODEL), BF16), jax.ShapeDtypeStruct((s, BRANCH_W + KV_W), F32),
                   jax.ShapeDtypeStruct((Q_HEADS, s, HEAD_DIM), BF16), jax.ShapeDtypeStruct((KV_HEADS, s, HEAD_DIM), BF16),
                   jax.ShapeDtypeStruct((KV_HEADS, s, LANES), BF16), jax.ShapeDtypeStruct((Q_HEADS, s, HEAD_DIM), BF16),
                   jax.ShapeDtypeStruct((KV_HEADS, s, HEAD_DIM), BF16), jax.ShapeDtypeStruct((KV_HEADS, s, LANES), BF16),
                   jax.ShapeDtypeStruct((s, 2 * D_MODEL), F32)],
        compiler_params=_params(("parallel",)),
    )(x, mod6, g1, w_in_s, gq, gk, bd, *tabs)


def _group_specs(s, tq):
    q_spec = pl.BlockSpec((None, GROUP, tq, HEAD_DIM), lambda g, i: (g, 0, i, 0))
    kv_spec = pl.BlockSpec((None, s, HEAD_DIM), lambda g, i: (g, 0, 0))
    col_spec = pl.BlockSpec((None, GROUP, tq, 1), lambda g, i: (g, 0, i, 0))
    return q_spec, kv_spec, col_spec


def _attn_a_fwd(q, k, v1, rider=None, tq=256, tk=512):
    s = q.shape[1]
    tk = min(tk, s)
    rows = GROUP * tq

    n = s // tk
    assert n >= 2 and n % 2 == 0

    def body(q_ref, k_ref, v_ref, o_ref, oh_ref, lse_ref, s0_ref, s1_ref, p0_ref, p1_ref, m_ref, a_ref, acc_ref):
        s_ref, p_ref = (s0_ref, s1_ref), (p0_ref, p1_ref)
        qq = q_ref[...].reshape(rows, HEAD_DIM)
        m_ref[...] = jnp.full((rows, 1), NEG_INF, F32)
        acc_ref[...] = jnp.zeros((rows, LANES), F32)

        def keys(i):
            return pl.ds(pl.multiple_of(i * tk, tk), tk)

        def scores(i, slot):
            s_ref[slot][...] = _nt(qq, k_ref[keys(i), :])

        def softmax(slot):
            sc = s_ref[slot][...]
            m = m_ref[...]
            mn = jnp.maximum(m, jnp.max(sc, axis=-1, keepdims=True))
            m_ref[...] = mn
            a_ref[...] = jnp.exp(m - mn)
            p_ref[slot][...] = jnp.exp(sc - mn).astype(BF16)

        def weigh(i, slot):
            acc_ref[...] = a_ref[...] * acc_ref[...] + _nn(p_ref[slot][...], v_ref[keys(i), :])

        scores(0, 0)
        softmax(0)
        scores(1, 1)

        def two_steps(j, carry):
            i = 2 * j + 1
            weigh(i - 1, 0)
            softmax(1)
            scores(i + 1, 0)
            weigh(i, 1)
            softmax(0)
            scores(i + 2, 1)
            return carry

        lax.fori_loop(0, (n - 2) // 2, two_steps, 0)
        weigh(n - 2, 0)
        softmax(1)
        weigh(n - 1, 1)
        l = acc_ref[:, HEAD_DIM:HEAD_DIM + 1]
        o = (acc_ref[:, :HEAD_DIM] / l).astype(BF16)
        for g in range(GROUP):
            o_ref[:, HEAD_DIM * g:HEAD_DIM * (g + 1)] = o[tq * g:tq * (g + 1)]
        oh_ref[...] = o.reshape(GROUP, tq, HEAD_DIM)
        lse_ref[...] = (m_ref[...] + jnp.log(l)).reshape(GROUP, tq, 1)

    q_spec, kv_spec, col_spec = _group_specs(s, tq)
    v_spec = pl.BlockSpec((None, s, LANES), lambda g, i: (g, 0, 0))
    return _hosted(
        body, rider, name="attn_a_fwd", grid=(KV_HEADS, s // tq),
        in_specs=[q_spec, kv_spec, v_spec],
        out_specs=[pl.BlockSpec((tq, GROUP * HEAD_DIM), lambda g, i: (i, g)), q_spec, col_spec],
        out_shape=[jax.ShapeDtypeStruct((s, BRANCH_W), BF16), jax.ShapeDtypeStruct((KV_HEADS, GROUP, s, HEAD_DIM), BF16),
                   jax.ShapeDtypeStruct((KV_HEADS, GROUP, s, 1), F32)],
        scratch_shapes=[pltpu.VMEM((rows, tk), F32), pltpu.VMEM((rows, tk), F32), pltpu.VMEM((rows, tk), BF16),
                        pltpu.VMEM((rows, tk), BF16), pltpu.VMEM((rows, 1), F32), pltpu.VMEM((rows, 1), F32),
                        pltpu.VMEM((rows, LANES), F32)],
        args=(q.reshape(KV_HEADS, GROUP, s, HEAD_DIM), k, v1), middle_at=KV_HEADS * (s // tq) // 2)


def _attn_a_bwd(q, k, v1, o, do, lse, rider=None, tq=256, tk=512):
    v = v1
    s = q.shape[1]
    tk = min(tk, s)
    rows = GROUP * tq

    n = s // tk
    assert n >= 2 and n % 2 == 0

    def body(q_ref, k_ref, v_ref, o_ref, do_ref, lse_ref, dq_ref, dk_ref, dv_ref,
             s0_ref, s1_ref, dp0_ref, dp1_ref, p0_ref, p1_ref, ds0_ref, ds1_ref, dq_acc):
        s_ref, dp_ref, p_ref, ds_ref = (s0_ref, s1_ref), (dp0_ref, dp1_ref), (p0_ref, p1_ref), (ds0_ref, ds1_ref)

        @pl.when(pl.program_id(1) == 0)
        def _():
            dk_ref[...] = jnp.zeros_like(dk_ref)
            dv_ref[...] = jnp.zeros_like(dv_ref)

        qq = q_ref[...].reshape(rows, HEAD_DIM)
        dd = do_ref[...].reshape(rows, HEAD_DIM)
        ls = lse_ref[...].reshape(rows, 1)
        dl = jnp.sum(dd.astype(F32) * o_ref[...].reshape(rows, HEAD_DIM).astype(F32), axis=-1, keepdims=True)
        dq_acc[...] = jnp.zeros((rows, HEAD_DIM), F32)

        def keys(i):
            return pl.ds(pl.multiple_of(i * tk, tk), tk)

        def scores(i, slot):
            s_ref[slot][...] = _nt(qq, k_ref[keys(i), :])
            dp_ref[slot][...] = _nt(dd, v_ref[keys(i), :HEAD_DIM])

        def weights(slot):
            p = jnp.exp(s_ref[slot][...] - ls)
            p_ref[slot][...] = p.astype(BF16)
            ds_ref[slot][...] = (p * (dp_ref[slot][...] - dl)).astype(BF16)

        def grads(i, slot):
            dv_ref[keys(i), :] += _tn(p_ref[slot][...], dd)
            dk_ref[keys(i), :] += _tn(ds_ref[slot][...], qq)
            dq_acc[...] += _nn(ds_ref[slot][...], k_ref[keys(i), :])

        scores(0, 0)
        weights(0)
        scores(1, 1)

        def two_steps(j, carry):
            i = 2 * j + 1
            grads(i - 1, 0)
            weights(1)
            scores(i + 1, 0)
            grads(i, 1)
            weights(0)
            scores(i + 2, 1)
            return carry

        lax.fori_loop(0, (n - 2) // 2, two_steps, 0, unroll=True)
        grads(n - 2, 0)
        weights(1)
        grads(n - 1, 1)
        dq_ref[...] = dq_acc[...].reshape(GROUP, tq, HEAD_DIM)

    q_spec, kv_spec, col_spec = _group_specs(s, tq)
    v_spec = pl.BlockSpec((None, s, LANES), lambda g, i: (g, 0, 0))
    shape4 = (KV_HEADS, GROUP, s, HEAD_DIM)
    tile32, tile16 = pltpu.VMEM((rows, tk), F32), pltpu.VMEM((rows, tk), BF16)
    return _hosted(
        body, rider, name="attn_a_bwd", grid=(KV_HEADS, s // tq),
        in_specs=[q_spec, kv_spec, v_spec, q_spec, q_spec, col_spec],
        out_specs=[q_spec, kv_spec, kv_spec],
        out_shape=[jax.ShapeDtypeStruct(shape4, F32), jax.ShapeDtypeStruct((KV_HEADS, s, HEAD_DIM), F32),
                   jax.ShapeDtypeStruct((KV_HEADS, s, HEAD_DIM), F32)],
        scratch_shapes=[tile32] * 4 + [tile16] * 4 + [pltpu.VMEM((rows, HEAD_DIM), F32)],
        args=(q.reshape(shape4), k, v, o.reshape(shape4), do.reshape(shape4), lse))


TQ_B = WINDOW


def _win_keys(tq):
    return tq + 2 * WINDOW


def _window_bias(tq):
    r = jnp.arange(tq, dtype=jnp.int32)[:, None]
    col = jnp.arange(_win_keys(tq), dtype=jnp.int32)[None, :]
    return jnp.stack([jnp.where(jnp.abs(r - col + WINDOW * b) <= WINDOW, 0.0, NEG_INF) for b in range(3)]).astype(F32)


def _band(tq, s):
    win = _win_keys(tq)

    def window(e):
        return pl.ds(pl.multiple_of(jnp.clip(e * tq - WINDOW, 0, s - win), WINDOW), win)

    def bias_index(e):
        return jnp.where(e == 0, 0, jnp.where(e >= s // tq - 1, 2, 1))

    return window, bias_index


def _pair_specs(s, tq):
    pairs = s // (2 * tq)
    cur = lambda g, j: (g, 0, jnp.minimum(j, pairs - 1), 0)
    prev = lambda g, j: (g, 0, jnp.maximum(j - 1, 0), 0)
    tile = lambda width, index: pl.BlockSpec((None, GROUP, 2 * tq, width), index)
    kv_spec = pl.BlockSpec((None, s, HEAD_DIM), lambda g, j: (g, 0, 0))
    v_spec = pl.BlockSpec((None, s, LANES), lambda g, j: (g, 0, 0))
    sink_spec = pl.BlockSpec((None, GROUP * tq, 1), lambda g, j: (g, 0, 0))
    bias_spec = pl.BlockSpec((3, tq, _win_keys(tq)), lambda g, j: (0, 0, 0))
    return tile, cur, prev, kv_spec, v_spec, sink_spec, bias_spec


def _attn_b_fwd(q, k, v1, sink_col, bias, rider=None, tq=TQ_B):
    s = q.shape[1]
    rows = GROUP * tq
    win = _win_keys(tq)
    pairs = s // (2 * tq)
    window, bias_index = _band(tq, s)

    def body(q_ref, k_ref, v_ref, sink_ref, bias_ref, o_ref, oh_ref, lse_ref, s0_ref, s1_ref, p0_ref, p1_ref, m0_ref, m1_ref):
        s_ref, p_ref, m_ref = (s0_ref, s1_ref), (p0_ref, p1_ref), (m0_ref, m1_ref)
        j = pl.program_id(1)

        @pl.when(j == 0)
        def _():
            for ref in (s0_ref, s1_ref, p0_ref, p1_ref, m0_ref, m1_ref):
                ref[...] = jnp.zeros_like(ref)

        def scores(e, slot):
            qq = q_ref[:, pl.ds(slot * tq, tq), :].reshape(rows, HEAD_DIM)
            sc = _nt(qq, k_ref[window(e), :]).reshape(GROUP, tq, win) + bias_ref[bias_index(e)][None]
            s_ref[slot][...] = sc.reshape(rows, win)

        def softmax(slot):
            sc = s_ref[slot][...]
            m = jnp.maximum(jnp.max(sc, axis=-1, keepdims=True), sink_ref[...])
            m_ref[slot][...] = m
            p_ref[slot][...] = jnp.exp(sc - m).astype(BF16)

        def finish(e, slot):
            acc = _nn(p_ref[slot][...], v_ref[window(e), :])
            m = m_ref[slot][...]
            l = acc[:, HEAD_DIM:HEAD_DIM + 1] + jnp.exp(sink_ref[...] - m)
            o = (acc[:, :HEAD_DIM] / l).astype(BF16)
            at = pl.ds(slot * tq, tq)
            for g in range(GROUP):
                o_ref[at, HEAD_DIM * g:HEAD_DIM * (g + 1)] = o[tq * g:tq * (g + 1)]
            oh_ref[:, at, :] = o.reshape(GROUP, tq, HEAD_DIM)
            lse_ref[:, at, :] = (m + jnp.log(l)).reshape(GROUP, tq, 1)

        first = 2 * j
        finish(jnp.maximum(first - 2, 0), 0)
        softmax(1)
        scores(first, 0)
        finish(jnp.maximum(first - 1, 0), 1)
        softmax(0)
        scores(first + 1, 1)

    tile, cur, prev, kv_spec, v_spec, sink_spec, bias_spec = _pair_specs(s, tq)
    tile32, tile16, col = pltpu.VMEM((rows, win), F32), pltpu.VMEM((rows, win), BF16), pltpu.VMEM((rows, 1), F32)
    return _hosted(
        body, rider, name="attn_b_fwd", grid=(KV_HEADS, pairs + 1),
        in_specs=[tile(HEAD_DIM, cur), kv_spec, v_spec, sink_spec, bias_spec],
        out_specs=[pl.BlockSpec((2 * tq, GROUP * HEAD_DIM), lambda g, j: (jnp.maximum(j - 1, 0), g)),
                   tile(HEAD_DIM, prev), tile(1, prev)],
        out_shape=[jax.ShapeDtypeStruct((s, BRANCH_W), BF16), jax.ShapeDtypeStruct((KV_HEADS, GROUP, s, HEAD_DIM), BF16),
                   jax.ShapeDtypeStruct((KV_HEADS, GROUP, s, 1), F32)],
        scratch_shapes=[tile32, tile32, tile16, tile16, col, col],
        args=(q.reshape(KV_HEADS, GROUP, s, HEAD_DIM), k, v1, sink_col, bias))


def _attn_b_bwd(q, k, v1, o, do, lse, sink_col, bias, rider=None, tq=TQ_B):
    s = q.shape[1]
    rows = GROUP * tq
    win = _win_keys(tq)
    pairs = s // (2 * tq)
    window, bias_index = _band(tq, s)

    def body(q_ref, k_ref, v_ref, o_ref, do_ref, lse_ref, sink_ref, bias_ref, dq_ref, dk_ref, dv_ref, dsink_ref,
             s0, s1, dp0, dp1, p0, p1, ds0, ds1, q0, q1, d0, d1, ls0, ls1, dl0, dl1):
        s_ref, dp_ref, p_ref, ds_ref = (s0, s1), (dp0, dp1), (p0, p1), (ds0, ds1)
        q_keep, do_keep, lse_keep, delta_keep = (q0, q1), (d0, d1), (ls0, ls1), (dl0, dl1)
        j = pl.program_id(1)

        @pl.when(j == 0)
        def _():
            for ref in (dk_ref, dv_ref, dsink_ref, s0, s1, dp0, dp1, p0, p1, ds0, ds1, q0, q1, d0, d1, ls0, ls1, dl0, dl1):
                ref[...] = jnp.zeros_like(ref)

        def scores(e, slot):
            at = pl.ds(slot * tq, tq)
            qq = q_ref[:, at, :].reshape(rows, HEAD_DIM)
            dd = do_ref[:, at, :].reshape(rows, HEAD_DIM)
            q_keep[slot][...] = qq
            do_keep[slot][...] = dd
            lse_keep[slot][...] = lse_ref[:, at, :].reshape(rows, 1)
            delta_keep[slot][...] = jnp.sum(dd.astype(F32) * o_ref[:, at, :].reshape(rows, HEAD_DIM).astype(F32), axis=-1,
                                            keepdims=True)
            sc = _nt(qq, k_ref[window(e), :]).reshape(GROUP, tq, win) + bias_ref[bias_index(e)][None]
            s_ref[slot][...] = sc.reshape(rows, win)
            dp_ref[slot][...] = _nt(dd, v_ref[window(e), :HEAD_DIM])

        def weights(slot):
            p = jnp.exp(s_ref[slot][...] - lse_keep[slot][...])
            p_ref[slot][...] = p.astype(BF16)
            ds_ref[slot][...] = (p * (dp_ref[slot][...] - delta_keep[slot][...])).astype(BF16)

        def grads(e, slot, live):
            at = window(e)
            ds = ds_ref[slot][...]
            dv_ref[at, :] += _tn(p_ref[slot][...], do_keep[slot][...])
            dk_ref[at, :] += _tn(ds, q_keep[slot][...])
            dq_ref[:, pl.ds(slot * tq, tq), :] = _nn(ds, k_ref[at, :]).reshape(GROUP, tq, HEAD_DIM)
            dsk = jnp.exp(sink_ref[...] - lse_keep[slot][...]) * delta_keep[slot][...] * live
            for g in range(GROUP):
                dsink_ref[g:g + 1, :] -= jnp.broadcast_to(jnp.sum(dsk[tq * g:tq * (g + 1)], axis=0, keepdims=True), (1, LANES))

        first = 2 * j
        live = jnp.where(j > 0, 1.0, 0.0)
        grads(jnp.maximum(first - 2, 0), 0, live)
        weights(1)
        scores(first, 0)
        grads(jnp.maximum(first - 1, 0), 1, live)
        weights(0)
        scores(first + 1, 1)

    tile, cur, prev, kv_spec, v_spec, sink_spec, bias_spec = _pair_specs(s, tq)
    dsink_spec = pl.BlockSpec((None, ACC_ROWS, LANES), lambda g, j: (g, 0, 0))
    shape4 = (KV_HEADS, GROUP, s, HEAD_DIM)
    tile32, tile16 = pltpu.VMEM((rows, win), F32), pltpu.VMEM((rows, win), BF16)
    keep, col = pltpu.VMEM((rows, HEAD_DIM), BF16), pltpu.VMEM((rows, 1), F32)
    return _hosted(
        body, rider, name="attn_b_bwd", grid=(KV_HEADS, pairs + 1),
        in_specs=[tile(HEAD_DIM, cur), kv_spec, v_spec, tile(HEAD_DIM, cur), tile(HEAD_DIM, cur), tile(1, cur), sink_spec,
                  bias_spec],
        out_specs=[tile(HEAD_DIM, prev), kv_spec, kv_spec, dsink_spec],
        out_shape=[jax.ShapeDtypeStruct(shape4, F32), jax.ShapeDtypeStruct((KV_HEADS, s, HEAD_DIM), F32),
                   jax.ShapeDtypeStruct((KV_HEADS, s, HEAD_DIM), F32), jax.ShapeDtypeStruct((KV_HEADS, ACC_ROWS, LANES), F32)],
        scratch_shapes=[tile32] * 4 + [tile16] * 4 + [keep] * 4 + [col] * 4,
        args=(q.reshape(shape4), k, v1, o.reshape(shape4), do.reshape(shape4), lse, sink_col, bias))


def _post_attn(ya, yb, gates, x, mod6, wbr_s, w_out, tm=256):
    s = x.shape[0]

    def body(ya_ref, yb_ref, g_ref, x_ref, mod_ref, wbr_ref, wo_ref, ua_ref, ub_ref, mg_ref, o_ref, x1_ref):
        ya_t, yb_t = ya_ref[...], yb_ref[...]
        ua = jnp.concatenate([_nn(ya_t, wbr_ref[j, 0]) for j in range(N_SHARD)], axis=1)
        ub = jnp.concatenate([_nn(yb_t, wbr_ref[j, 1]) for j in range(N_SHARD)], axis=1)
        merged = (_sigmoid(g_ref[:, :D_MODEL]) * ua + _sigmoid(g_ref[:, D_MODEL:]) * ub).astype(BF16)
        o = _nn(merged, wo_ref[...])
        ua_ref[...] = ua.astype(BF16)
        ub_ref[...] = ub.astype(BF16)
        mg_ref[...] = merged
        o_ref[...] = o.astype(BF16)
        x1_ref[...] = x_ref[...] + mod_ref[2:3, :] * o

    bf = jax.ShapeDtypeStruct((s, D_MODEL), BF16)
    return _call(
        body, name="post_attn", grid=(s // tm,),
        in_specs=[_row_spec(tm, BRANCH_W), _row_spec(tm, BRANCH_W), _row_spec(tm, 2 * D_MODEL), _row_spec(tm, D_MODEL),
                  _full_spec(mod6.shape), _full_spec(wbr_s.shape), _full_spec(w_out.shape)],
        out_specs=[_row_spec(tm, D_MODEL)] * 5,
        out_shape=[bf, bf, bf, bf, jax.ShapeDtypeStruct((s, D_MODEL), F32)],
        compiler_params=_params(("parallel",)),
    )(ya, yb, gates, x, mod6, wbr_s, w_out)


def _mlp_in(x1, mod6, g2, w_mi_s, tm=256):
    s = x1.shape[0]

    def body(x_ref, mod_ref, g_ref, w_ref, h2_ref, a_ref, hid_ref):
        xt = x_ref[...]
        r = lax.rsqrt(jnp.mean(xt * xt, axis=-1, keepdims=True) + NORM_EPS)
        h2 = ((xt * r * g_ref[...]) * (1.0 + mod_ref[4:5, :]) + mod_ref[3:4, :]).astype(BF16)
        h2_ref[...] = h2
        a = jnp.concatenate([_nn(h2, w_ref[j]) for j in range(N_SHARD)], axis=1)
        a_ref[...] = a.astype(BF16)
        hid_ref[...] = jnp.square(jnp.maximum(a, 0.0)).astype(BF16)

    return _call(
        body, name="mlp_in", grid=(s // tm,),
        in_specs=[_row_spec(tm, D_MODEL), _full_spec(mod6.shape), _full_spec(g2.shape), _full_spec(w_mi_s.shape)],
        out_specs=[_row_spec(tm, D_MODEL), _row_spec(tm, D_FF), _row_spec(tm, D_FF)],
        out_shape=[jax.ShapeDtypeStruct((s, D_MODEL), BF16), jax.ShapeDtypeStruct((s, D_FF), BF16),
                   jax.ShapeDtypeStruct((s, D_FF), BF16)],
        compiler_params=_params(("parallel",)),
    )(x1, mod6, g2, w_mi_s)


ACC_ROWS = 8


def _acc_spec():
    return pl.BlockSpec((ACC_ROWS, D_MODEL), lambda i: (0, 0))


def _acc_add(acc_ref, rows):
    @pl.when(pl.program_id(0) == 0)
    def _():
        acc_ref[...] = jnp.zeros_like(acc_ref)

    for r, val in enumerate(rows):
        acc_ref[r:r + 1, :] += jnp.sum(val, axis=0, keepdims=True)


def _mlp_out_loss(hid, x1, a, target, mod6, gf, w_mo, tm=256):
    s = x1.shape[0]

    def body(hid_ref, x_ref, a_ref, t_ref, mod_ref, gf_ref, w_ref, dx2_ref, dm_ref, da_ref, acc_ref):
        m = _nn(hid_ref[...], w_ref[...])
        gate2 = mod_ref[5:6, :]
        x2 = x_ref[...] + gate2 * m
        r = lax.rsqrt(jnp.mean(x2 * x2, axis=-1, keepdims=True) + NORM_EPS)
        xn = x2 * r
        err = xn * gf_ref[...] - t_ref[...]
        dy = err * (1.0 / D_MODEL)
        dxn = dy * gf_ref[...]
        dx2 = r * (dxn - xn * jnp.mean(dxn * xn, axis=-1, keepdims=True))
        dx2_ref[...] = dx2
        dm = (dx2 * gate2).astype(BF16)
        dm_ref[...] = dm
        da_ref[...] = (_nt(dm, w_ref[...]) * (2.0 * jnp.maximum(a_ref[...].astype(F32), 0.0))).astype(BF16)
        _acc_add(acc_ref, [err * err, dy * xn, dx2 * m])

    return _call(
        body, name="mlp_out_loss", grid=(s // tm,),
        in_specs=[_row_spec(tm, D_FF), _row_spec(tm, D_MODEL), _row_spec(tm, D_FF), _row_spec(tm, D_MODEL),
                  _full_spec(mod6.shape), _full_spec(gf.shape), _full_spec(w_mo.shape)],
        out_specs=[_row_spec(tm, D_MODEL), _row_spec(tm, D_MODEL), _row_spec(tm, D_FF), _acc_spec()],
        out_shape=[jax.ShapeDtypeStruct((s, D_MODEL), F32), jax.ShapeDtypeStruct((s, D_MODEL), BF16),
                   jax.ShapeDtypeStruct((s, D_FF), BF16), jax.ShapeDtypeStruct((ACC_ROWS, D_MODEL), F32)],
        compiler_params=_params(("arbitrary",)),
    )(hid, x1, a, target, mod6, gf, w_mo)


def _norm_bwd(dh, xt, gain, scale):
    r = lax.rsqrt(jnp.mean(xt * xt, axis=-1, keepdims=True) + NORM_EPS)
    xn = xt * r
    dxn = dh * (gain * (1.0 + scale))
    dx = r * (dxn - xn * jnp.mean(dxn * xn, axis=-1, keepdims=True))
    return dx, [dh, dh * xn * gain, dh * xn * (1.0 + scale)]


def _mlp_bwd(da, x1, dx2, o, mod6, g2, w_mi_s, rider=None, tm=256):
    s = x1.shape[0]

    def body(da_ref, x_ref, dx2_ref, o_ref, mod_ref, g_ref, w_ref, dx1_ref, do_ref, acc_ref):
        dh2 = _nt(da_ref[:, :D_MODEL], w_ref[0])
        for j in range(1, N_SHARD):
            dh2 += _nt(da_ref[:, D_MODEL * j:D_MODEL * (j + 1)], w_ref[j])
        dx, sums = _norm_bwd(dh2, x_ref[...], g_ref[...], mod_ref[4:5, :])
        dx1 = dx2_ref[...] + dx
        dx1_ref[...] = dx1
        do_ref[...] = (dx1 * mod_ref[2:3, :]).astype(BF16)
        _acc_add(acc_ref, sums + [dx1 * o_ref[...].astype(F32)])

    return _hosted(
        body, rider, name="mlp_bwd", grid=(s // tm,),
        in_specs=[_row_spec(tm, D_FF), _row_spec(tm, D_MODEL), _row_spec(tm, D_MODEL), _row_spec(tm, D_MODEL),
                  _full_spec(mod6.shape), _full_spec(g2.shape), _full_spec(w_mi_s.shape)],
        out_specs=[_row_spec(tm, D_MODEL), _row_spec(tm, D_MODEL), _acc_spec()],
        out_shape=[jax.ShapeDtypeStruct((s, D_MODEL), F32), jax.ShapeDtypeStruct((s, D_MODEL), BF16),
                   jax.ShapeDtypeStruct((ACC_ROWS, D_MODEL), F32)],
        args=(da, x1, dx2, o, mod6, g2, w_mi_s))


def _merge_bwd(do, gates, ua, ub, w_out, wbr_s, rider=None, tm=256):
    s = do.shape[0]

    def body(do_ref, g_ref, ua_ref, ub_ref, wo_ref, wbr_ref, dua_ref, dub_ref, dg_ref, doa_ref, dob_ref):
        dmerged = _nt(do_ref[...], wo_ref[...])
        for b, (u_ref, du_ref, dy_ref) in enumerate(((ua_ref, dua_ref, doa_ref), (ub_ref, dub_ref, dob_ref))):
            sg = _sigmoid(g_ref[:, D_MODEL * b:D_MODEL * (b + 1)])
            du = (dmerged * sg).astype(BF16)
            du_ref[...] = du
            dg_ref[:, D_MODEL * b:D_MODEL * (b + 1)] = (dmerged * u_ref[...].astype(F32) * sg * (1.0 - sg)).astype(BF16)
            w = BRANCH_W // 2
            dy = _nt(du[:, :w], wbr_ref[0, b])
            for j in range(1, N_SHARD):
                dy += _nt(du[:, w * j:w * (j + 1)], wbr_ref[j, b])
            dyb = dy.astype(BF16)
            for h in range(Q_HEADS):
                dy_ref[h] = dyb[:, HEAD_DIM * h:HEAD_DIM * (h + 1)]

    bf = jax.ShapeDtypeStruct((s, D_MODEL), BF16)
    heads = jax.ShapeDtypeStruct((Q_HEADS, s, HEAD_DIM), BF16)
    return _hosted(
        body, rider, name="merge_bwd", grid=(s // tm,),
        in_specs=[_row_spec(tm, D_MODEL), _row_spec(tm, 2 * D_MODEL), _row_spec(tm, D_MODEL), _row_spec(tm, D_MODEL),
                  _full_spec(w_out.shape), _full_spec(wbr_s.shape)],
        out_specs=[_row_spec(tm, D_MODEL), _row_spec(tm, D_MODEL), _row_spec(tm, 2 * D_MODEL),
                   _heads_spec(Q_HEADS, tm), _heads_spec(Q_HEADS, tm)],
        out_shape=[bf, bf, jax.ShapeDtypeStruct((s, 2 * D_MODEL), BF16), heads, heads],
        args=(do, gates, ua, ub, w_out, wbr_s))


def _qk_bwd(dqa, dka, dva, dqb, dkb, dvb, qkraw, dgates, gq, gk, bd, tabs, rider=None, tm=256):
    s = qkraw.shape[0]

    def body(dqa_ref, dka_ref, dva_ref, dqb_ref, dkb_ref, dvb_ref, raw_ref, dg_ref, gq_ref, gk_ref, bd_ref,
             ca, la, ha, cb, lb, hb, dp_ref, acc_ref, pair_ref):
        bdm = bd_ref[...]
        tab_a = (ca[...], la[...], ha[...])
        tab_b = (cb[...], lb[...], hb[...])

        def pair(ref, first):
            pair_ref[:, :HEAD_DIM] = ref[first]
            pair_ref[:, HEAD_DIM:] = ref[first + 1]
            return pair_ref[...]

        def norm_rope_a_bwd(dz, raw, gain):
            dzn = _rope(dz, *tab_a, 16, sign=-1.0)
            rinv = lax.rsqrt(_head_mean(raw * raw, bdm) + NORM_EPS)
            zhat = raw * rinv
            dzhat = dzn * gain
            return rinv * (dzhat - zhat * _head_mean(dzhat * zhat, bdm)), dzn * zhat

        gq_rows = jnp.zeros((tm, LANES), F32)
        for i in range(Q_HEADS // 2):
            at = slice(LANES * i, LANES * (i + 1))
            draw, gsum = norm_rope_a_bwd(pair(dqa_ref, 2 * i) * Q_SCALE, raw_ref[:, at], gq_ref[...])
            dp_ref[:, at] = draw.astype(BF16)
            gq_rows += gsum
        off = BRANCH_W
        draw, gk_rows = norm_rope_a_bwd(pair(dka_ref, 0), raw_ref[:, off:off + LANES], gk_ref[...])
        dp_ref[:, off:off + LANES] = draw.astype(BF16)
        off += KV_W
        dp_ref[:, off:off + LANES] = pair(dva_ref, 0).astype(BF16)
        off += KV_W
        for i in range(Q_HEADS // 2):
            dz = _rope(pair(dqb_ref, 2 * i) * Q_SCALE, *tab_b, 32, sign=-1.0)
            dp_ref[:, off + LANES * i:off + LANES * (i + 1)] = dz.astype(BF16)
        off += BRANCH_W
        dp_ref[:, off:off + LANES] = _rope(pair(dkb_ref, 0), *tab_b, 32, sign=-1.0).astype(BF16)
        off += KV_W
        dp_ref[:, off:off + LANES] = pair(dvb_ref, 0).astype(BF16)
        dp_ref[:, QK_W:] = dg_ref[...]

        @pl.when(pl.program_id(0) == 0)
        def _():
            acc_ref[...] = jnp.zeros_like(acc_ref)

        acc_ref[0:1, :] += jnp.sum(gq_rows, axis=0, keepdims=True)
        acc_ref[1:2, :] += jnp.sum(gk_rows, axis=0, keepdims=True)

    tab_spec = _row_spec(tm, LANES)
    return _hosted(
        body, rider, name="qk_bwd", grid=(s // tm,),
        in_specs=[_heads_spec(Q_HEADS, tm), _heads_spec(KV_HEADS, tm), _heads_spec(KV_HEADS, tm),
                  _heads_spec(Q_HEADS, tm), _heads_spec(KV_HEADS, tm), _heads_spec(KV_HEADS, tm),
                  _row_spec(tm, BRANCH_W + KV_W), _row_spec(tm, 2 * D_MODEL),
                  _full_spec(gq.shape), _full_spec(gk.shape), _full_spec(bd.shape)] + [tab_spec] * 6,
        out_specs=[_row_spec(tm, IN_W), pl.BlockSpec((ACC_ROWS, LANES), lambda i: (0, 0))],
        out_shape=[jax.ShapeDtypeStruct((s, IN_W), BF16), jax.ShapeDtypeStruct((ACC_ROWS, LANES), F32)],
        scratch_shapes=[pltpu.VMEM((tm, LANES), F32)],
        args=(dqa, dka, dva, dqb, dkb, dvb, qkraw, dgates, gq, gk, bd, *tabs))


def _in_proj_bwd(dproj, x, dx1, mod6, g1, w_in_s, tm=256):
    s = x.shape[0]
    w = IN_W // N_SHARD

    def body(dp_ref, x_ref, dx1_ref, mod_ref, g_ref, w_ref, gx_ref, acc_ref):
        dh = _nt(dp_ref[:, :w], w_ref[0])
        for j in range(1, N_SHARD):
            dh += _nt(dp_ref[:, w * j:w * (j + 1)], w_ref[j])
        dx, sums = _norm_bwd(dh, x_ref[...], g_ref[...], mod_ref[1:2, :])
        gx_ref[...] = dx1_ref[...] + dx
        _acc_add(acc_ref, sums)

    return _call(
        body, name="in_proj_bwd", grid=(s // tm,),
        in_specs=[_row_spec(tm, IN_W), _row_spec(tm, D_MODEL), _row_spec(tm, D_MODEL),
                  _full_spec(mod6.shape), _full_spec(g1.shape), _full_spec(w_in_s.shape)],
        out_specs=[_row_spec(tm, D_MODEL), _acc_spec()],
        out_shape=[jax.ShapeDtypeStruct((s, D_MODEL), F32), jax.ShapeDtypeStruct((ACC_ROWS, D_MODEL), F32)],
        compiler_params=_params(("arbitrary",)),
    )(dproj, x, dx1, mod6, g1, w_in_s)


def _wgrad(name, a, b, out_shape, out_spec, tm, tn, tk=4096):
    s, m = a.shape
    n = b.shape[1]
    tk = min(tk, s)
    nk = s // tk

    def body(a_ref, b_ref, o_ref, acc_ref):
        k = pl.program_id(2)

        @pl.when(k == 0)
        def _():
            acc_ref[...] = jnp.zeros_like(acc_ref)

        acc_ref[...] += _tn(a_ref[...], b_ref[...])

        @pl.when(k == nk - 1)
        def _():
            o_ref[...] = acc_ref[...].reshape(o_ref.shape)

    return _call(
        body, name=name, grid=(m // tm, n // tn, nk),
        in_specs=[pl.BlockSpec((tk, tm), lambda i, j, k: (k, i)), pl.BlockSpec((tk, tn), lambda i, j, k: (k, j))],
        out_specs=out_spec, out_shape=jax.ShapeDtypeStruct(out_shape, F32),
        scratch_shapes=[pltpu.VMEM((tm, tn), F32)],
        compiler_params=_params(("parallel", "parallel", "arbitrary")),
    )(a, b)


def _wgrad_branch(ya, yb, dua, dub, tk=2048):
    s = ya.shape[0]
    tk = min(tk, s)
    nk = s // tk
    w = D_MODEL // N_SHARD

    def body(ya_ref, yb_ref, dua_ref, dub_ref, o_ref, acc_ref):
        b, k = pl.program_id(0), pl.program_id(2)

        @pl.when(k == 0)
        def _():
            acc_ref[...] = jnp.zeros_like(acc_ref)

        @pl.when(b == 0)
        def _():
            acc_ref[...] += _tn(ya_ref[...], dua_ref[...])

        @pl.when(b == 1)
        def _():
            acc_ref[...] += _tn(yb_ref[...], dub_ref[...])

        @pl.when(k == nk - 1)
        def _():
            o_ref[...] = acc_ref[...]

    y_spec = pl.BlockSpec((tk, BRANCH_W), lambda b, j, k: (k, 0))
    du_spec = pl.BlockSpec((tk, w), lambda b, j, k: (k, j))
    return _call(
        body, name="wgrad_branch", grid=(2, N_SHARD, nk),
        in_specs=[y_spec, y_spec, du_spec, du_spec],
        out_specs=pl.BlockSpec((None, None, BRANCH_W, w), lambda b, j, k: (j, b, 0, 0)),
        out_shape=jax.ShapeDtypeStruct((N_SHARD, 2, BRANCH_W, w), F32),
        scratch_shapes=[pltpu.VMEM((BRANCH_W, w), F32)],
        compiler_params=_params(("parallel", "parallel", "arbitrary")),
    )(ya, yb, dua, dub)


def _local_step(x, target, mod6, g1, g2, gf, gq2, gk2, sink, w_in_s, rest, cj=None):
    s = x.shape[0]
    dist = cj is not None
    tabs = _rope_tables(s)
    bd = _block_diag()
    sink_col = jnp.repeat(sink.reshape(KV_HEADS, GROUP, 1), TQ_B, axis=1).reshape(KV_HEADS, GROUP * TQ_B, 1)
    shard = D_MODEL // N_SHARD

    h, qkraw, qa, ka, va, qb, kb, vb, gates = _in_proj(x, mod6, g1, w_in_s, gq2, gk2, bd, tabs)
    bias = _window_bias(TQ_B)
    (yb, yb_heads, lse_b), _ = _attn_b_fwd(qb, kb, vb, sink_col, bias)
    (ya, ya_heads, lse_a), gathered = _attn_a_fwd(qa, ka, va, rider=_gather_rider(rest) if dist else None)
    wbr_s, w_out, w_mi_s, w_mo = gathered if dist else rest
    wbr_s = wbr_s.reshape(N_SHARD, 2, BRANCH_W, shard)
    w_out = w_out.reshape(D_MODEL, D_MODEL)
    w_mo = w_mo.reshape(D_FF, D_MODEL)
    ua, ub, merged, o, x1 = _post_attn(ya, yb, gates, x, mod6, wbr_s, w_out)
    h2, a, hid = _mlp_in(x1, mod6, g2, w_mi_s)
    dx2, dm, da, acc_out = _mlp_out_loss(hid, x1, a, target, mod6, gf, w_mo)

    g_w_mo = _wgrad("wgrad_mlp_out", hid, dm, (D_FF, D_MODEL), pl.BlockSpec((D_MODEL, D_MODEL), lambda i, j, k: (i, 0)),
                    D_MODEL, D_MODEL).reshape(N_SHARD, D_MODEL, D_MODEL)
    g_w_mi = _wgrad("wgrad_mlp_in", h2, da, (N_SHARD, D_MODEL, D_MODEL),
                    pl.BlockSpec((None, D_MODEL, D_MODEL), lambda i, j, k: (j, i, 0)), D_MODEL, D_MODEL)
    mlp = _Reduction(("mlp_out", "mlp_in"), (g_w_mo, g_w_mi), cj)
    (dx1, do, acc_mlp), got = _mlp_bwd(da, x1, dx2, o, mod6, g2, w_mi_s, rider=mlp.swap() if dist else None)
    (dua, dub, dgates, doa, dob), landed = _merge_bwd(do, gates, ua, ub, w_out, wbr_s, rider=mlp.add(got) if dist else None)
    g_w_out = _wgrad("wgrad_out", merged, do, (D_MODEL, D_MODEL), pl.BlockSpec((D_MODEL, D_MODEL), lambda i, j, k: (i, 0)),
                     D_MODEL, D_MODEL).reshape(N_SHARD, shard, D_MODEL)
    g_wbr = _wgrad_branch(ya, yb, dua, dub)
    out = _Reduction(("out", "branch"), (g_w_out, g_wbr.reshape(N_SHARD, 2 * BRANCH_W, shard)), cj)
    (dqa, dka, dva), landings = _attn_a_bwd(qa, ka, va, ya_heads, doa, lse_a,
                                            rider=_riders(out.swap(), mlp.total(landed)) if dist else None)
    (dqb, dkb, dvb, dsink), _ = _attn_b_bwd(qb, kb, vb, yb_heads, dob, lse_b, sink_col, bias)
    heads = (Q_HEADS, s, HEAD_DIM)
    (dproj, acc_qk), landed = _qk_bwd(dqa.reshape(heads), dka, dva, dqb.reshape(heads), dkb, dvb, qkraw, dgates, gq2, gk2, bd,
                                      tabs, rider=out.add(landings[:2]) if dist else None)
    w = IN_W // N_SHARD
    g_w_in = _wgrad("wgrad_in", h, dproj, (N_SHARD, D_MODEL, w), pl.BlockSpec((None, D_MODEL, w), lambda i, j, k: (j, i, 0)),
                    D_MODEL, w)
    grad_x, acc_in = _in_proj_bwd(dproj, x, dx1, mod6, g1, w_in_s)
    accs = (acc_out, acc_mlp, acc_in, acc_qk, dsink)
    if not dist:
        return grad_x, (g_w_in, g_wbr, g_w_out, g_w_mi, g_w_mo), accs
    first = _Reduction(("in",), (g_w_in,), cj)
    landed_in = _alone("scatter_in", first.add(_alone("swap_in", first.swap())))
    r_in, r_out, r_br = _alone("join_in_out_branch", _riders(first.total(landed_in), out.total(landed)))
    r_mo, r_mi = landings[2:]
    return grad_x, (r_in, r_br, r_out, r_mi, r_mo), accs


def _me():
    return lax.axis_index("x"), lax.axis_index("y"), lax.axis_index("c")


def _peer(d):
    x, y, c = _me()
    return (1 - x if d & 4 else x, 1 - y if d & 2 else y, 1 - c if d & 1 else c)


def _dev_index(p):
    return 4 * p[0] + 2 * p[1] + p[2]


def _chip_index(p):
    return 2 * p[0] + p[1]


def _remote(src, dst, send_sem, recv_sem, to):
    return pltpu.make_async_remote_copy(src_ref=src, dst_ref=dst, send_sem=send_sem, recv_sem=recv_sem,
                                        device_id=to, device_id_type=MESH)


SLOT_ROWS = 8


def _ada_fwd(c, w_ada, b4):
    cols = w_ada.shape[1]

    def body(c_ref, w_ref, b_ref, mod_ref, sc_ref, cbuf, pbuf, mbuf, send1, recv1, send2, recv2):
        me = _me()
        mine, chip = _dev_index(me), _chip_index(me)
        cbuf[mine] = jnp.broadcast_to(c_ref[...], (SLOT_ROWS, D_MODEL))
        gather = [_remote(cbuf.at[mine], cbuf.at[mine], send1.at[d - 1], recv1.at[d - 1], _peer(d)) for d in range(1, N_DEV)]
        for cp in gather:
            cp.start()
        for d in range(1, N_DEV):
            _remote(cbuf.at[mine], cbuf.at[_dev_index(_peer(d))], send1.at[d - 1], recv1.at[d - 1], _peer(d)).wait_recv()
        call = cbuf[...].reshape(N_DEV * SLOT_ROWS, D_MODEL)
        sc = call * _sigmoid(call)
        for s in range(N_DEV):
            sc_ref[s:s + 1, :] = sc[SLOT_ROWS * s:SLOT_ROWS * s + 1]
        part = _nn(sc.astype(BF16), w_ref[...].astype(BF16)) + b_ref[pl.ds(chip, 1), :]
        pbuf[...] = part.reshape(N_DEV, SLOT_ROWS, cols)
        mbuf[chip] = pbuf[mine]
        spread = [_remote(pbuf.at[_dev_index(_peer(d))], mbuf.at[chip], send2.at[d // 2 - 1], recv2.at[d // 2 - 1], _peer(d))
                  for d in (2, 4, 6)]
        for cp in spread:
            cp.start()
        for d in (2, 4, 6):
            _remote(pbuf.at[mine], mbuf.at[_chip_index(_peer(d))], send2.at[d // 2 - 1], recv2.at[d // 2 - 1],
                    _peer(d)).wait_recv()
        half = D_MODEL // 2
        for p in range(2 * 6):
            col = half * p
            mod_ref[p // 2:p // 2 + 1, half * (p % 2):half * (p % 2 + 1)] = mbuf[col // cols, 0:1, col % cols:col % cols + half]
        for cp in gather + spread:
            cp.wait_send()

    vm = pl.BlockSpec(memory_space=pltpu.VMEM)
    return _call(
        body, name="ada_fwd", in_specs=[vm, vm, vm], out_specs=[vm, vm],
        out_shape=[jax.ShapeDtypeStruct((6, D_MODEL), F32), jax.ShapeDtypeStruct((N_DEV, D_MODEL), F32)],
        scratch_shapes=[pltpu.VMEM((N_DEV, SLOT_ROWS, D_MODEL), F32), pltpu.VMEM((N_DEV, SLOT_ROWS, cols), F32),
                        pltpu.VMEM((N_SHARD, SLOT_ROWS, cols), F32),
                        pltpu.SemaphoreType.DMA((N_DEV - 1,)), pltpu.SemaphoreType.DMA((N_DEV - 1,)),
                        pltpu.SemaphoreType.DMA((N_SHARD - 1,)), pltpu.SemaphoreType.DMA((N_SHARD - 1,))],
        compiler_params=_params(),
    )(c, w_ada, b4)


PACK_ROWS = 16
PACK_W = 3 * D_MODEL


def _ada_bwd(acc_out, acc_mlp, acc_in, acc_qk, dsink, sc_all):
    cols = 6 * D_MODEL // N_SHARD

    def body(out_ref, mlp_ref, in_ref, qk_ref, dsink_ref, sc_ref,
             gwa_ref, gba_ref, gn1_ref, gn2_ref, gf_ref, gq_ref, gk_ref, gs_ref, blk, send, recv):
        me = _me()
        mine, chip = _dev_index(me), _chip_index(me)
        blk[mine] = jnp.zeros((PACK_ROWS, PACK_W), F32)
        dmod = (in_ref, 0), (in_ref, 1), (mlp_ref, 3), (mlp_ref, 0), (mlp_ref, 1), (out_ref, 2)
        half = D_MODEL // 2
        for p in range(2 * 6):
            ref, row = dmod[p // 2]
            col = half * p
            blk[mine, col // cols:col // cols + 1, col % cols:col % cols + half] = ref[row:row + 1, half * (p % 2):half * (p % 2 + 1)]
        blk[mine, 4:5, 0:D_MODEL] = in_ref[2:3, :]
        blk[mine, 4:5, D_MODEL:2 * D_MODEL] = mlp_ref[2:3, :]
        blk[mine, 4:5, 2 * D_MODEL:] = out_ref[1:2, :]
        blk[mine, 5:6, 0:LANES] = qk_ref[0:1, :]
        blk[mine, 5:6, LANES:2 * LANES] = qk_ref[1:2, :]
        for g in range(KV_HEADS):
            blk[mine, 8 + GROUP * g:8 + GROUP * (g + 1), 0:LANES] = dsink_ref[g, 0:GROUP, :]
        copies = [_remote(blk.at[mine], blk.at[mine], send.at[d - 1], recv.at[d - 1], _peer(d)) for d in range(1, N_DEV)]
        for cp in copies:
            cp.start()
        for d in range(1, N_DEV):
            _remote(blk.at[mine], blk.at[_dev_index(_peer(d))], send.at[d - 1], recv.at[d - 1], _peer(d)).wait_recv()
        tot = blk[0]
        for s in range(1, N_DEV):
            tot = tot + blk[s]
        for j in range(N_SHARD):
            gba_ref[:, cols * j:cols * (j + 1)] = tot[j:j + 1, :cols]
        gn1_ref[...] = tot[4:5, 0:D_MODEL]
        gn2_ref[...] = tot[4:5, D_MODEL:2 * D_MODEL]
        gf_ref[...] = tot[4:5, 2 * D_MODEL:]
        gq_ref[...] = tot[5:6, 0:HEAD_DIM] + tot[5:6, HEAD_DIM:2 * HEAD_DIM]
        gk_ref[...] = tot[5:6, LANES:LANES + HEAD_DIM] + tot[5:6, LANES + HEAD_DIM:2 * LANES]
        sq = tot[8:16, 0:Q_HEADS]
        diag = lax.broadcasted_iota(jnp.int32, sq.shape, 0) == lax.broadcasted_iota(jnp.int32, sq.shape, 1)
        gs_ref[...] = jnp.sum(jnp.where(diag, sq, 0.0), axis=0, keepdims=True)
        dm = jnp.concatenate([blk[s, pl.ds(chip, 1), pl.ds(0, cols)] for s in range(N_DEV)], axis=0)
        gwa_ref[...] = _tn(sc_ref[...], dm)
        for cp in copies:
            cp.wait_send()

    vm = pl.BlockSpec(memory_space=pltpu.VMEM)
    row = lambda n: jax.ShapeDtypeStruct((1, n), F32)
    return _call(
        body, name="ada_bwd", in_specs=[vm] * 6, out_specs=[vm] * 8,
        out_shape=[jax.ShapeDtypeStruct((D_MODEL, cols), F32), row(6 * D_MODEL), row(D_MODEL), row(D_MODEL), row(D_MODEL),
                   row(HEAD_DIM), row(HEAD_DIM), row(Q_HEADS)],
        scratch_shapes=[pltpu.VMEM((N_DEV, PACK_ROWS, PACK_W), F32),
                        pltpu.SemaphoreType.DMA((N_DEV - 1,)), pltpu.SemaphoreType.DMA((N_DEV - 1,))],
        compiler_params=_params(),
    )(acc_out, acc_mlp, acc_in, acc_qk, dsink, sc_all)


def _cast_weights(ws):
    n = len(ws)

    def body(*refs):
        src, out, tmp, sems = refs[:n], refs[n:2 * n], refs[2 * n:3 * n], refs[3 * n]
        chip = _chip_index(_me())
        copies = []
        for a in range(n):
            tmp[a][...] = src[a][...].astype(BF16)
            cp = pltpu.make_async_copy(tmp[a], out[a].at[chip], sems.at[a])
            cp.start()
            copies.append(cp)
        for cp in copies:
            cp.wait()

    vm = pl.BlockSpec(memory_space=pltpu.VMEM)
    return _call(
        body, name="cast_weights", in_specs=[vm] * n, out_specs=[ANY] * n,
        out_shape=[jax.ShapeDtypeStruct((N_SHARD,) + w.shape, BF16) for w in ws],
        scratch_shapes=[pltpu.VMEM(w.shape, BF16) for w in ws] + [pltpu.SemaphoreType.DMA((n,))],
        compiler_params=_params(),
    )(*ws)


def _half_rows(ref_rows, c):
    half = ref_rows // 2
    return pl.ds(pl.multiple_of(c * half, 8), half)


class _Rider:
    def __init__(self, inputs, out_shape, aliases, n_sems, start, finish, middle=None):
        self.inputs, self.out_shape, self.aliases, self.n_sems = list(inputs), list(out_shape), dict(aliases), n_sems
        self.start, self.finish, self.middle = start, finish, middle


def _riders(*rs):
    ins = [0]
    outs = [0]
    sems = [0]
    for r in rs:
        ins.append(ins[-1] + len(r.inputs))
        outs.append(outs[-1] + len(r.out_shape))
        sems.append(sems[-1] + r.n_sems)

    def phase(which):
        def run(in_refs, out_refs, sem):
            for k, r in enumerate(rs):
                fn = getattr(r, which)
                if fn is not None:
                    fn(in_refs[ins[k]:ins[k + 1]], out_refs[outs[k]:outs[k + 1]], lambda j, base=sems[k]: sem(base + j))
        return run

    aliases = {ins[k] + i: outs[k] + o for k, r in enumerate(rs) for i, o in r.aliases.items()}
    return _Rider([a for r in rs for a in r.inputs], [o for r in rs for o in r.out_shape], aliases, sems[-1],
                  phase("start"), phase("finish"), phase("middle") if any(r.middle for r in rs) else None)


def _hosted(body, rider, *, name, grid, in_specs, out_specs, out_shape, args, scratch_shapes=(), middle_at=None):
    sem = ("arbitrary",) * len(grid)
    if rider is None:
        res = _call(body, name=name, grid=grid, in_specs=in_specs, out_specs=out_specs, out_shape=out_shape,
                    scratch_shapes=list(scratch_shapes), compiler_params=_params(sem))(*args)
        return res, ()
    n_in, n_out, n_scr = len(in_specs), len(out_specs), len(scratch_shapes)
    r_in, r_out = len(rider.inputs), len(rider.out_shape)

    def riding(*refs):
        at = 0
        parts = []
        for size in (n_in, r_in, n_out, r_out, n_scr):
            parts.append(refs[at:at + size])
            at += size
        ins, rider_ins, outs, rider_outs, scratch = parts
        sems = refs[at]
        step = pl.program_id(0)
        for axis in range(1, len(grid)):
            step = step * grid[axis] + pl.program_id(axis)
        steps = 1
        for size in grid:
            steps *= size

        def sem_at(k):
            return sems.at[k]

        @pl.when(step == 0)
        def _():
            rider.start(rider_ins, rider_outs, sem_at)

        body(*ins, *outs, *scratch)
        if rider.middle is not None:
            @pl.when(step == middle_at)
            def _():
                rider.middle(rider_ins, rider_outs, sem_at)

        @pl.when(step == steps - 1)
        def _():
            rider.finish(rider_ins, rider_outs, sem_at)

    res = _call(
        riding, name=name, grid=grid, in_specs=list(in_specs) + [ANY] * r_in, out_specs=list(out_specs) + [ANY] * r_out,
        out_shape=list(out_shape) + rider.out_shape,
        input_output_aliases={n_in + i: n_out + o for i, o in rider.aliases.items()},
        scratch_shapes=list(scratch_shapes) + [pltpu.SemaphoreType.DMA((rider.n_sems,))],
        compiler_params=_params(sem),
    )(*args, *rider.inputs)
    return res[:n_out], res[n_out:]


def _alone(name, rider):
    n_in, n_out = len(rider.inputs), len(rider.out_shape)

    def body(*refs):
        ins, outs, sems = refs[:n_in], refs[n_in:n_in + n_out], refs[n_in + n_out]

        def sem_at(k):
            return sems.at[k]

        rider.start(ins, outs, sem_at)
        if rider.middle is not None:
            rider.middle(ins, outs, sem_at)
        rider.finish(ins, outs, sem_at)

    return _call(
        body, name=name, in_specs=[ANY] * n_in, out_specs=[ANY] * n_out, out_shape=rider.out_shape,
        input_output_aliases=rider.aliases, scratch_shapes=[pltpu.SemaphoreType.DMA((rider.n_sems,))],
    )(*rider.inputs)


OTHER_CHIPS = (2, 4, 6)


def _gather_rider(stacked):
    n = len(stacked)

    def flights(bufs, sem):
        me = _me()
        chip, sib = _chip_index(me), _peer(1)
        out = []
        for a in range(n):
            mine, theirs = (_half_rows(bufs[a].shape[1], c) for c in (me[2], 1 - me[2]))
            for j, d in enumerate(OTHER_CHIPS):
                k = 3 * a + j
                from_chip = _chip_index(_peer(d))
                own, landed, passed = bufs[a].at[chip, mine], bufs[a].at[from_chip, mine], bufs[a].at[from_chip, theirs]
                out.append((_remote(own, own, sem(k), sem(3 * n + k), _peer(d)),
                            _remote(own, landed, sem(k), sem(3 * n + k), _peer(d)),
                            _remote(landed, landed, sem(6 * n + k), sem(9 * n + k), sib),
                            _remote(passed, passed, sem(6 * n + k), sem(9 * n + k), sib)))
        return out

    def start(ins, outs, sem):
        for send, _, _, _ in flights(outs, sem):
            send.start()

    def middle(ins, outs, sem):
        for _, arrival, pass_on, _ in flights(outs, sem):
            arrival.wait_recv()
            pass_on.start()

    def finish(ins, outs, sem):
        every = flights(outs, sem)
        for _, _, _, passed_to_me in every:
            passed_to_me.wait_recv()
        for send, _, pass_on, _ in every:
            send.wait_send()
            pass_on.wait_send()

    return _Rider(stacked, [jax.ShapeDtypeStruct(w.shape, w.dtype) for w in stacked], {a: a for a in range(n)}, 12 * n,
                  start, finish, middle)


def _swap_rider(grads):
    n = len(grads)

    def copies(ins, outs, sem):
        c = _me()[2]
        return [_remote(ins[a].at[pl.ds(0, N_SHARD), _half_rows(ins[a].shape[1], 1 - c)], outs[a], sem(a), sem(n + a), _peer(1))
                for a in range(n)]

    def start(ins, outs, sem):
        for cp in copies(ins, outs, sem):
            cp.start()

    def finish(ins, outs, sem):
        for cp in copies(ins, outs, sem):
            cp.wait()

    return _Rider(grads, [jax.ShapeDtypeStruct((N_SHARD, g.shape[1] // 2, g.shape[2]), F32) for g in grads], {}, 2 * n,
                  start, finish)


def _row_tile(rows):
    return min(rows, 256)


def _add_halves(name, g, got, cj):
    _, half, cols = got.shape
    tr = _row_tile(half)
    nt = half // tr

    def body(cj_ref, g_ref, got_ref, o_ref):
        o_ref[...] = (g_ref[...] + got_ref[...]).astype(BF16)

    spec = pl.BlockSpec((None, tr, cols), lambda i, s, cj: (s, i, 0))
    return _call(
        body, name=name,
        grid_spec=pltpu.PrefetchScalarGridSpec(
            num_scalar_prefetch=1, grid=(nt, N_SHARD),
            in_specs=[pl.BlockSpec((None, tr, cols), lambda i, s, cj: (s, cj[0] * nt + i, 0)), spec], out_specs=spec),
        out_shape=jax.ShapeDtypeStruct(got.shape, BF16), compiler_params=_params(("parallel", "parallel")),
    )(cj, g, got)


def _scatter_rider(sums):
    n = len(sums)

    def flights(ins, outs, sem):
        chip = _chip_index(_me())
        out = []
        for a in range(n):
            for j, d in enumerate(OTHER_CHIPS):
                k = 3 * a + j
                other = _chip_index(_peer(d))
                out.append((_remote(ins[a].at[other], outs[a].at[chip], sem(k), sem(3 * n + k), _peer(d)),
                            _remote(ins[a].at[chip], outs[a].at[other], sem(k), sem(3 * n + k), _peer(d))))
        return out

    def start(ins, outs, sem):
        for send, _ in flights(ins, outs, sem):
            send.start()

    def finish(ins, outs, sem):
        every = flights(ins, outs, sem)
        for _, arrival in every:
            arrival.wait_recv()
        for send, _ in every:
            send.wait_send()

    return _Rider(sums, [jax.ShapeDtypeStruct(v.shape, v.dtype) for v in sums], {}, 6 * n, start, finish)


def _sum_chips(name, g, got, landed, cj):
    _, half, cols = got.shape
    tr = _row_tile(half)
    nt = half // tr

    def body(cj_ref, g_ref, got_ref, landed_ref, o_ref):
        own = g_ref[...] + got_ref[...]
        total = None
        for s in range(N_SHARD):
            term = jnp.where(cj_ref[1] == s, own, landed_ref[s].astype(F32))
            total = term if total is None else total + term
        o_ref[...] = total

    return _call(
        body, name=name,
        grid_spec=pltpu.PrefetchScalarGridSpec(
            num_scalar_prefetch=1, grid=(nt,),
            in_specs=[pl.BlockSpec((None, tr, cols), lambda i, cj: (cj[1], cj[0] * nt + i, 0)),
                      pl.BlockSpec((None, tr, cols), lambda i, cj: (cj[1], i, 0)),
                      pl.BlockSpec((N_SHARD, tr, cols), lambda i, cj: (0, i, 0))],
            out_specs=pl.BlockSpec((tr, cols), lambda i, cj: (cj[0] * nt + i, 0))),
        out_shape=jax.ShapeDtypeStruct((2 * half, cols), F32), compiler_params=_params(("parallel",)),
    )(cj, g, got, landed)


def _join_rider(shards):
    n = len(shards)

    def flights(bufs, sem):
        c = _me()[2]
        out = []
        for a in range(n):
            mine, theirs = (bufs[a].at[_half_rows(bufs[a].shape[0], cc)] for cc in (c, 1 - c))
            out.append((_remote(mine, mine, sem(a), sem(n + a), _peer(1)), _remote(theirs, theirs, sem(a), sem(n + a), _peer(1))))
        return out

    def start(ins, outs, sem):
        for send, _ in flights(outs, sem):
            send.start()

    def finish(ins, outs, sem):
        for send, arrival in flights(outs, sem):
            arrival.wait_recv()
            send.wait_send()

    return _Rider(shards, [jax.ShapeDtypeStruct(h.shape, F32) for h in shards], {a: a for a in range(n)}, 2 * n, start, finish)


class _Reduction:
    def __init__(self, names, grads, cj):
        self.names, self.grads, self.cj = names, list(grads), cj

    def swap(self):
        return _swap_rider(self.grads)

    def add(self, got):
        self.got = list(got)
        self.sums = [_add_halves("add_halves_" + nm, g, h, self.cj) for nm, g, h in zip(self.names, self.grads, self.got)]
        return _scatter_rider(self.sums)

    def total(self, landed):
        halves = [_sum_chips("sum_chips_" + nm, g, h, l, self.cj)
                  for nm, g, h, l in zip(self.names, self.grads, self.got, landed)]
        return _join_rider(halves)


def _adamw_math(w, g, m, v):
    m = ADAM_B1 * m + (1.0 - ADAM_B1) * g
    v = ADAM_B2 * v + (1.0 - ADAM_B2) * jnp.square(g)
    m_hat = m / (1.0 - ADAM_B1 ** ADAM_STEP)
    v_hat = v / (1.0 - ADAM_B2 ** ADAM_STEP)
    return -ADAM_LR * (m_hat / (jnp.sqrt(v_hat) + ADAM_EPS) + ADAM_WD * w), m, v


def _adamw(name, w, g, m, v):
    rows, cols = w.shape
    tr = _row_tile(rows)

    def body(w_ref, g_ref, m_ref, v_ref, d_ref, nm_ref, nv_ref):
        d_ref[...], nm_ref[...], nv_ref[...] = _adamw_math(w_ref[...], g_ref[...], m_ref[...], v_ref[...])

    spec = pl.BlockSpec((tr, cols), lambda i: (i, 0))
    return _call(
        body, name=name, grid=(rows // tr,), in_specs=[spec] * 4, out_specs=[spec] * 3,
        out_shape=[jax.ShapeDtypeStruct(w.shape, F32)] * 3, compiler_params=_params(("parallel",)),
    )(w, g, m, v)


def _adamw_small(ws, gs, ms, vs):
    n = len(ws)

    def body(*refs):
        ins, outs = refs[:4 * n], refs[4 * n:]
        for a in range(n):
            w, g, m, v = (ins[k * n + a][...] for k in range(4))
            outs[a][...], outs[n + a][...], outs[2 * n + a][...] = _adamw_math(w, g, m, v)

    vm = pl.BlockSpec(memory_space=pltpu.VMEM)
    res = _call(
        body, name="adamw_small", in_specs=[vm] * (4 * n), out_specs=[vm] * (3 * n),
        out_shape=[jax.ShapeDtypeStruct(w.shape, F32) for w in ws] * 3, compiler_params=_params(),
    )(*ws, *gs, *ms, *vs)
    return res[:n], res[n:2 * n], res[2 * n:]


def kernel(x, c, w_ada, b_ada, norm1_g, w_in, q_norm_a, k_norm_a, sink_b, w_branch, w_out, norm2_g, w_mlp_in, w_mlp_out, final_g, loss_target, m_w_ada, m_b_ada, m_norm1_g, m_w_in, m_q_norm_a, m_k_norm_a, m_sink_b, m_w_branch, m_w_out, m_norm2_g, m_w_mlp_in, m_w_mlp_out, m_final_g, v_w_ada, v_b_ada, v_norm1_g, v_w_in, v_q_norm_a, v_k_norm_a, v_sink_b, v_w_branch, v_w_out, v_norm2_g, v_w_mlp_in, v_w_mlp_out, v_final_g):
    xi, yi, ci = _me()
    cj = jnp.stack([ci, 2 * xi + yi]).astype(jnp.int32)
    n_cols = 6 * D_MODEL // N_SHARD

    mod6, sc_all = _ada_fwd(c, w_ada[0], b_ada.reshape(N_SHARD, n_cols))

    def rows2d(a):
        return a.reshape(-1, a.shape[-1])

    big = (w_in, w_branch, w_out, w_mlp_in, w_mlp_out)
    stacked = _cast_weights([rows2d(w) for w in big])
    w_in_s, = _alone("gather_w_in", _gather_rider(stacked[:1]))
    rest = stacked[1:]

    gq2 = jnp.tile(q_norm_a, (1, 2))
    gk2 = jnp.tile(k_norm_a, (1, 2))
    grad_x, g_big, (acc_out, acc_mlp, acc_in, acc_qk, dsink) = _local_step(
        x[0], loss_target[0], mod6, norm1_g, norm2_g, final_g.reshape(1, D_MODEL), gq2, gk2, sink_b[0], w_in_s, rest, cj)

    loss = lax.psum(0.5 * jnp.sum(acc_out[0]) / D_MODEL, ("x", "y", "c"))
    g_w_ada, g_b_ada, g_n1, g_n2, g_f, g_q, g_k, g_s = _ada_bwd(acc_out, acc_mlp, acc_in, acc_qk, dsink, sc_all)

    names = ("w_ada", "w_in", "w_branch", "w_out", "w_mlp_in", "w_mlp_out")
    big_w = [w_ada[0]] + [rows2d(w) for w in big]
    big_g = [g_w_ada] + list(g_big)
    big_m = [rows2d(m) for m in (m_w_ada, m_w_in, m_w_branch, m_w_out, m_w_mlp_in, m_w_mlp_out)]
    big_v = [rows2d(v) for v in (v_w_ada, v_w_in, v_w_branch, v_w_out, v_w_mlp_in, v_w_mlp_out)]
    big_res = {nm: _adamw("adamw_" + nm, w, g, m, v) for nm, w, g, m, v in zip(names, big_w, big_g, big_m, big_v)}

    small = ("b_ada", "norm1_g", "q_norm_a", "k_norm_a", "sink_b", "norm2_g", "final_g")
    row = lambda a: a.reshape(1, -1)
    small_w = [row(a) for a in (b_ada, norm1_g, q_norm_a, k_norm_a, sink_b, norm2_g, final_g)]
    small_g = [g_b_ada, g_n1, g_q, g_k, g_s, g_n2, g_f]
    small_m = [row(a) for a in (m_b_ada, m_norm1_g, m_q_norm_a, m_k_norm_a, m_sink_b, m_norm2_g, m_final_g)]
    small_v = [row(a) for a in (v_b_ada, v_norm1_g, v_q_norm_a, v_k_norm_a, v_sink_b, v_norm2_g, v_final_g)]
    s_d, s_m, s_v = _adamw_small(small_w, small_g, small_m, small_v)

    order = ("w_ada", "b_ada", "norm1_g", "w_in", "q_norm_a", "k_norm_a", "sink_b", "w_branch", "w_out", "norm2_g",
             "w_mlp_in", "w_mlp_out", "final_g")
    like = dict(w_ada=w_ada, b_ada=b_ada, norm1_g=norm1_g, w_in=w_in, q_norm_a=q_norm_a, k_norm_a=k_norm_a, sink_b=sink_b,
                w_branch=w_branch, w_out=w_out, norm2_g=norm2_g, w_mlp_in=w_mlp_in, w_mlp_out=w_mlp_out, final_g=final_g)
    grad, delta, new_m, new_v = {}, {}, {}, {}
    for nm, g in zip(names, big_g):
        grad[nm] = g
        delta[nm], new_m[nm], new_v[nm] = big_res[nm]
    for k, nm in enumerate(small):
        grad[nm], delta[nm], new_m[nm], new_v[nm] = small_g[k], s_d[k], s_m[k], s_v[k]
    outs = [loss, grad_x[None]]
    for group in (grad, delta, new_m, new_v):
        outs += [group[nm].reshape(like[nm].shape) for nm in order]
    return tuple(outs)
```

```python
import functools

import jax
import jax.numpy as jnp
from jax import lax
from jax.experimental import pallas as pl
from jax.experimental.pallas import tpu as pltpu

F32 = jnp.float32
BF16 = jnp.bfloat16
MESH = pl.DeviceIdType.MESH
ANY = pl.BlockSpec(memory_space=pl.ANY)

D_MODEL = 1024
HEAD_DIM = 64
Q_HEADS = 8
KV_HEADS = 2
GROUP = Q_HEADS // KV_HEADS
BRANCH_W = Q_HEADS * HEAD_DIM
KV_W = KV_HEADS * HEAD_DIM
IN_W = 2 * (BRANCH_W + 2 * KV_W) + 2 * D_MODEL
QK_W = 2 * (BRANCH_W + 2 * KV_W)
D_FF = 4 * D_MODEL
GRID_W = 64
WINDOW = 128
ROPE_THETA = 10000.0
NORM_EPS = 1e-6
NEG_INF = -1e30
Q_SCALE = HEAD_DIM ** -0.5
N_SHARD = 4
N_DEV = 8
LANES = 128
VMEM_LIMIT = 56 * 1024 * 1024

ADAM_LR = 0.001
ADAM_B1 = 0.9
ADAM_B2 = 0.999
ADAM_EPS = 1e-08
ADAM_WD = 0.01
ADAM_STEP = 10

_call = pl.pallas_call


def _params(sem=None, vmem=VMEM_LIMIT):
    return pltpu.CompilerParams(dimension_semantics=sem, vmem_limit_bytes=vmem)


def _nt(a, b):
    return lax.dot_general(a, b, (((1,), (1,)), ((), ())), preferred_element_type=F32)


def _tn(a, b):
    return lax.dot_general(a, b, (((0,), (0,)), ((), ())), preferred_element_type=F32)


def _nn(a, b):
    return jnp.dot(a, b, preferred_element_type=F32)


def _sigmoid(z):
    return 0.5 * jnp.tanh(0.5 * z) + 0.5


def _rope_tables(s):
    t = jnp.arange(s, dtype=jnp.int32)
    lane = jnp.arange(LANES, dtype=jnp.int32)

    def cos_sin(pos, dim):
        inv = ROPE_THETA ** (-jnp.arange(0, dim, 2, dtype=F32) / dim)
        ang = pos.astype(F32)[:, None] * inv[None, :]
        return jnp.cos(ang), jnp.sin(ang)

    cr, sr = cos_sin(t // GRID_W, HEAD_DIM // 2)
    cc, sc = cos_sin(t % GRID_W, HEAD_DIM // 2)
    cos_a = jnp.tile(jnp.concatenate([cr, cr, cc, cc], axis=1), (1, 2))
    sin_a = jnp.tile(jnp.concatenate([sr, sr, sc, sc], axis=1), (1, 2))
    first_a = (lane % 32) < 16
    c1, s1 = cos_sin(t, HEAD_DIM)
    cos_b = jnp.tile(jnp.concatenate([c1, c1], axis=1), (1, 2))
    sin_b = jnp.tile(jnp.concatenate([s1, s1], axis=1), (1, 2))
    first_b = (lane % 64) < 32
    tabs_a = (cos_a, jnp.where(first_a, -sin_a, 0.0), jnp.where(first_a, 0.0, sin_a))
    tabs_b = (cos_b, jnp.where(first_b, -sin_b, 0.0), jnp.where(first_b, 0.0, sin_b))
    return tabs_a + tabs_b


def _rope(z, cos, s_lo, s_hi, half, sign=1.0):
    up = pltpu.roll(z, LANES - half, 1)
    dn = pltpu.roll(z, half, 1)
    return z * cos + sign * (up * s_lo + dn * s_hi)


def _head_mean(z2, bd):
    hi = z2.astype(BF16)
    lo = (z2 - hi.astype(F32)).astype(BF16)
    return _nn(hi, bd) + _nn(lo, bd)


def _block_diag():
    lane = jnp.arange(LANES)
    return jnp.where((lane[:, None] // HEAD_DIM) == (lane[None, :] // HEAD_DIM), 1.0 / HEAD_DIM, 0.0).astype(BF16)


def _row_spec(tm, width):
    return pl.BlockSpec((tm, width), lambda i: (i, 0))


def _heads_spec(heads, tm):
    return pl.BlockSpec((heads, tm, HEAD_DIM), lambda i: (0, i, 0))


def _full_spec(shape):
    nd = len(shape)
    return pl.BlockSpec(shape, lambda i: (0,) * nd)


def _in_proj(x, mod6, g1, w_in_s, gq, gk, bd, tabs, tm=256):
    s = x.shape[0]

    def body(x_ref, mod_ref, g1_ref, w_ref, gq_ref, gk_ref, bd_ref, ca, la, ha, cb, lb, hb,
             h_ref, qkraw_ref, qa_ref, ka_ref, va_ref, qb_ref, kb_ref, vb_ref, gate_ref):
        xt = x_ref[...]
        r = lax.rsqrt(jnp.mean(xt * xt, axis=-1, keepdims=True) + NORM_EPS)
        h = (xt * r * g1_ref[...]) * (1.0 + mod_ref[1:2, :]) + mod_ref[0:1, :]
        hb16 = h.astype(BF16)
        h_ref[...] = hb16
        proj = jnp.concatenate([_nn(hb16, w_ref[j]) for j in range(N_SHARD)], axis=1)
        qkraw_ref[...] = proj[:, :BRANCH_W + KV_W]
        bdm = bd_ref[...]
        tab_a = (ca[...], la[...], ha[...])
        tab_b = (cb[...], lb[...], hb[...])

        def norm_rope_a(z, gain):
            zn = z * lax.rsqrt(_head_mean(z * z, bdm) + NORM_EPS) * gain
            return _rope(zn, *tab_a, 16)

        def put(ref, first, z):
            zb = z.astype(BF16)
            ref[first] = zb[:, :HEAD_DIM]
            ref[first + 1] = zb[:, HEAD_DIM:]

        for i in range(Q_HEADS // 2):
            put(qa_ref, 2 * i, norm_rope_a(proj[:, LANES * i:LANES * (i + 1)], gq_ref[...]) * Q_SCALE)
        off = BRANCH_W
        put(ka_ref, 0, norm_rope_a(proj[:, off:off + LANES], gk_ref[...]))
        off += KV_W
        def put_v(ref, z):
            zb = z.astype(BF16)
            for hd in range(KV_HEADS):
                ref[hd, :, :HEAD_DIM] = zb[:, HEAD_DIM * hd:HEAD_DIM * (hd + 1)]
                ref[hd, :, HEAD_DIM:] = jnp.ones((tm, HEAD_DIM), BF16)

        put_v(va_ref, proj[:, off:off + LANES])
        off += KV_W
        for i in range(Q_HEADS // 2):
            put(qb_ref, 2 * i, _rope(proj[:, off + LANES * i:off + LANES * (i + 1)], *tab_b, 32) * Q_SCALE)
        off += BRANCH_W
        put(kb_ref, 0, _rope(proj[:, off:off + LANES], *tab_b, 32))
        off += KV_W
        put_v(vb_ref, proj[:, off:off + LANES])
        gate_ref[...] = proj[:, QK_W:]

    tab_spec = _row_spec(tm, LANES)
    return _call(
        body, name="in_proj", grid=(s // tm,),
        in_specs=[_row_spec(tm, D_MODEL), _full_spec(mod6.shape), _full_spec(g1.shape), _full_spec(w_in_s.shape),
                  _full_spec(gq.shape), _full_spec(gk.shape), _full_spec(bd.shape)] + [tab_spec] * 6,
        out_specs=[_row_spec(tm, D_MODEL), _row_spec(tm, BRANCH_W + KV_W), _heads_spec(Q_HEADS, tm), _heads_spec(KV_HEADS, tm),
                   pl.BlockSpec((KV_HEADS, tm, LANES), lambda i: (0, i, 0)), _heads_spec(Q_HEADS, tm),
                   _heads_spec(KV_HEADS, tm), pl.BlockSpec((KV_HEADS, tm, LANES), lambda i: (0, i, 0)),
                   _row_spec(tm, 2 * D_MODEL)],
        out_shape=[jax.ShapeDtypeStruct((s, D_MODEL), BF16), jax.ShapeDtypeStruct((s, BRANCH_W + KV_W), F32),
                   jax.ShapeDtypeStruct((Q_HEADS, s, HEAD_DIM), BF16), jax.ShapeDtypeStruct((KV_HEADS, s, HEAD_DIM), BF16),
                   jax.ShapeDtypeStruct((KV_HEADS, s, LANES), BF16), jax.ShapeDtypeStruct((Q_HEADS, s, HEAD_DIM), BF16),
                   jax.ShapeDtypeStruct((KV_HEADS, s, HEAD_DIM), BF16), jax.ShapeDtypeStruct((KV_HEADS, s, LANES), BF16),
                   jax.ShapeDtypeStruct((s, 2 * D_MODEL), F32)],
        compiler_params=_params(("parallel",)),
    )(x, mod6, g1, w_in_s, gq, gk, bd, *tabs)


def _group_specs(s, tq):
    q_spec = pl.BlockSpec((None, GROUP, tq, HEAD_DIM), lambda g, i: (g, 0, i, 0))
    kv_spec = pl.BlockSpec((None, s, HEAD_DIM), lambda g, i: (g, 0, 0))
    col_spec = pl.BlockSpec((None, GROUP, tq, 1), lambda g, i: (g, 0, i, 0))
    return q_spec, kv_spec, col_spec


def _attn_a_fwd(q, k, v1, rider=None, tq=256, tk=2048):
    s = q.shape[1]
    tk = min(tk, s // 2)
    rows = GROUP * tq

    n = s // tk
    assert n >= 2 and n % 2 == 0

    def body(q_ref, k_ref, v_ref, o_ref, oh_ref, lse_ref, s0_ref, s1_ref, p0_ref, p1_ref, m_ref, a_ref, acc_ref):
        s_ref, p_ref = (s0_ref, s1_ref), (p0_ref, p1_ref)
        qq = q_ref[...].reshape(rows, HEAD_DIM)
        m_ref[...] = jnp.full((rows, 1), NEG_INF, F32)
        acc_ref[...] = jnp.zeros((rows, LANES), F32)

        def keys(i):
            return pl.ds(pl.multiple_of(i * tk, tk), tk)

        def scores(i, slot):
            s_ref[slot][...] = _nt(qq, k_ref[keys(i), :])

        def softmax(slot):
            sc = s_ref[slot][...]
            m = m_ref[...]
            mn = jnp.maximum(m, jnp.max(sc, axis=-1, keepdims=True))
            m_ref[...] = mn
            a_ref[...] = jnp.exp(m - mn)
            p_ref[slot][...] = jnp.exp(sc - mn).astype(BF16)

        def weigh(i, slot):
            acc_ref[...] = a_ref[...] * acc_ref[...] + _nn(p_ref[slot][...], v_ref[keys(i), :])

        scores(0, 0)
        softmax(0)
        scores(1, 1)

        def two_steps(j, carry):
            i = 2 * j + 1
            weigh(i - 1, 0)
            softmax(1)
            scores(i + 1, 0)
            weigh(i, 1)
            softmax(0)
            scores(i + 2, 1)
            return carry

        lax.fori_loop(0, (n - 2) // 2, two_steps, 0, unroll=True)
        weigh(n - 2, 0)
        softmax(1)
        weigh(n - 1, 1)
        l = acc_ref[:, HEAD_DIM:HEAD_DIM + 1]
        o = (acc_ref[:, :HEAD_DIM] / l).astype(BF16)
        for g in range(GROUP):
            o_ref[:, HEAD_DIM * g:HEAD_DIM * (g + 1)] = o[tq * g:tq * (g + 1)]
        oh_ref[...] = o.reshape(GROUP, tq, HEAD_DIM)
        lse_ref[...] = (m_ref[...] + jnp.log(l)).reshape(GROUP, tq, 1)

    q_spec, kv_spec, col_spec = _group_specs(s, tq)
    v_spec = pl.BlockSpec((None, s, LANES), lambda g, i: (g, 0, 0))
    return _hosted(
        body, rider, name="attn_a_fwd", grid=(KV_HEADS, s // tq),
        in_specs=[q_spec, kv_spec, v_spec],
        out_specs=[pl.BlockSpec((tq, GROUP * HEAD_DIM), lambda g, i: (i, g)), q_spec, col_spec],
        out_shape=[jax.ShapeDtypeStruct((s, BRANCH_W), BF16), jax.ShapeDtypeStruct((KV_HEADS, GROUP, s, HEAD_DIM), BF16),
                   jax.ShapeDtypeStruct((KV_HEADS, GROUP, s, 1), F32)],
        scratch_shapes=[pltpu.VMEM((rows, tk), F32), pltpu.VMEM((rows, tk), F32), pltpu.VMEM((rows, tk), BF16),
                        pltpu.VMEM((rows, tk), BF16), pltpu.VMEM((rows, 1), F32), pltpu.VMEM((rows, 1), F32),
                        pltpu.VMEM((rows, LANES), F32)],
        args=(q.reshape(KV_HEADS, GROUP, s, HEAD_DIM), k, v1), middle_at=KV_HEADS * (s // tq) // 2)


def _attn_a_bwd(q, k, v1, o, do, lse, rider=None, tq=256, tk=512):
    v = v1
    s = q.shape[1]
    tk = min(tk, s // 2)
    rows = GROUP * tq

    n = s // tk
    assert n >= 2 and n % 2 == 0

    def body(q_ref, k_ref, v_ref, o_ref, do_ref, lse_ref, dq_ref, dk_ref, dv_ref,
             s0_ref, s1_ref, dp0_ref, dp1_ref, p0_ref, p1_ref, ds0_ref, ds1_ref, dq_acc):
        s_ref, dp_ref, p_ref, ds_ref = (s0_ref, s1_ref), (dp0_ref, dp1_ref), (p0_ref, p1_ref), (ds0_ref, ds1_ref)

        @pl.when(pl.program_id(1) == 0)
        def _():
            dk_ref[...] = jnp.zeros_like(dk_ref)
            dv_ref[...] = jnp.zeros_like(dv_ref)

        qq = q_ref[...].reshape(rows, HEAD_DIM)
        dd = do_ref[...].reshape(rows, HEAD_DIM)
        ls = lse_ref[...].reshape(rows, 1)
        dl = jnp.sum(dd.astype(F32) * o_ref[...].reshape(rows, HEAD_DIM).astype(F32), axis=-1, keepdims=True)
        dq_acc[...] = jnp.zeros((rows, HEAD_DIM), F32)

        def keys(i):
            return pl.ds(pl.multiple_of(i * tk, tk), tk)

        def scores(i, slot):
            s_ref[slot][...] = _nt(qq, k_ref[keys(i), :])
            dp_ref[slot][...] = _nt(dd, v_ref[keys(i), :HEAD_DIM])

        def weights(slot):
            p = jnp.exp(s_ref[slot][...] - ls)
            p_ref[slot][...] = p.astype(BF16)
            ds_ref[slot][...] = (p * (dp_ref[slot][...] - dl)).astype(BF16)

        def grads(i, slot):
            dv_ref[keys(i), :] += _tn(p_ref[slot][...], dd)
            dk_ref[keys(i), :] += _tn(ds_ref[slot][...], qq)
            dq_acc[...] += _nn(ds_ref[slot][...], k_ref[keys(i), :])

        scores(0, 0)
        weights(0)
        scores(1, 1)

        def two_steps(j, carry):
            i = 2 * j + 1
            grads(i - 1, 0)
            weights(1)
            scores(i + 1, 0)
            grads(i, 1)
            weights(0)
            scores(i + 2, 1)
            return carry

        lax.fori_loop(0, (n - 2) // 2, two_steps, 0, unroll=True)
        grads(n - 2, 0)
        weights(1)
        grads(n - 1, 1)
        dq_ref[...] = dq_acc[...].reshape(GROUP, tq, HEAD_DIM)

    q_spec, kv_spec, col_spec = _group_specs(s, tq)
    v_spec = pl.BlockSpec((None, s, LANES), lambda g, i: (g, 0, 0))
    shape4 = (KV_HEADS, GROUP, s, HEAD_DIM)
    tile32, tile16 = pltpu.VMEM((rows, tk), F32), pltpu.VMEM((rows, tk), BF16)
    return _hosted(
        body, rider, name="attn_a_bwd", grid=(KV_HEADS, s // tq),
        in_specs=[q_spec, kv_spec, v_spec, q_spec, q_spec, col_spec],
        out_specs=[q_spec, kv_spec, kv_spec],
        out_shape=[jax.ShapeDtypeStruct(shape4, F32), jax.ShapeDtypeStruct((KV_HEADS, s, HEAD_DIM), F32),
                   jax.ShapeDtypeStruct((KV_HEADS, s, HEAD_DIM), F32)],
        scratch_shapes=[tile32] * 4 + [tile16] * 4 + [pltpu.VMEM((rows, HEAD_DIM), F32)],
        args=(q.reshape(shape4), k, v, o.reshape(shape4), do.reshape(shape4), lse))


TQ_B = WINDOW


def _win_keys(tq):
    return tq + 2 * WINDOW


def _window_bias(tq):
    r = jnp.arange(tq, dtype=jnp.int32)[:, None]
    col = jnp.arange(_win_keys(tq), dtype=jnp.int32)[None, :]
    return jnp.stack([jnp.where(jnp.abs(r - col + WINDOW * b) <= WINDOW, 0.0, NEG_INF) for b in range(3)]).astype(F32)


def _band(tq, s):
    win = _win_keys(tq)

    def window(e):
        return pl.ds(pl.multiple_of(jnp.clip(e * tq - WINDOW, 0, s - win), WINDOW), win)

    def bias_index(e):
        return jnp.where(e == 0, 0, jnp.where(e >= s // tq - 1, 2, 1))

    return window, bias_index


def _pair_specs(s, tq):
    pairs = s // (2 * tq)
    cur = lambda g, j: (g, 0, jnp.minimum(j, pairs - 1), 0)
    prev = lambda g, j: (g, 0, jnp.maximum(j - 1, 0), 0)
    tile = lambda width, index: pl.BlockSpec((None, GROUP, 2 * tq, width), index)
    kv_spec = pl.BlockSpec((None, s, HEAD_DIM), lambda g, j: (g, 0, 0))
    v_spec = pl.BlockSpec((None, s, LANES), lambda g, j: (g, 0, 0))
    sink_spec = pl.BlockSpec((None, GROUP * tq, 1), lambda g, j: (g, 0, 0))
    bias_spec = pl.BlockSpec((3, tq, _win_keys(tq)), lambda g, j: (0, 0, 0))
    return tile, cur, prev, kv_spec, v_spec, sink_spec, bias_spec


def _attn_b_fwd(q, k, v1, sink_col, bias, rider=None, tq=TQ_B):
    s = q.shape[1]
    rows = GROUP * tq
    win = _win_keys(tq)
    pairs = s // (2 * tq)
    window, bias_index = _band(tq, s)

    def body(q_ref, k_ref, v_ref, sink_ref, bias_ref, o_ref, oh_ref, lse_ref, s0_ref, s1_ref, p0_ref, p1_ref, m0_ref, m1_ref):
        s_ref, p_ref, m_ref = (s0_ref, s1_ref), (p0_ref, p1_ref), (m0_ref, m1_ref)
        j = pl.program_id(1)

        @pl.when(j == 0)
        def _():
            for ref in (s0_ref, s1_ref, p0_ref, p1_ref, m0_ref, m1_ref):
                ref[...] = jnp.zeros_like(ref)

        def scores(e, slot):
            qq = q_ref[:, pl.ds(slot * tq, tq), :].reshape(rows, HEAD_DIM)
            sc = _nt(qq, k_ref[window(e), :]).reshape(GROUP, tq, win) + bias_ref[bias_index(e)][None]
            s_ref[slot][...] = sc.reshape(rows, win)

        def softmax(slot):
            sc = s_ref[slot][...]
            m = jnp.maximum(jnp.max(sc, axis=-1, keepdims=True), sink_ref[...])
            m_ref[slot][...] = m
            p_ref[slot][...] = jnp.exp(sc - m).astype(BF16)

        def finish(e, slot):
            acc = _nn(p_ref[slot][...], v_ref[window(e), :])
            m = m_ref[slot][...]
            l = acc[:, HEAD_DIM:HEAD_DIM + 1] + jnp.exp(sink_ref[...] - m)
            o = (acc[:, :HEAD_DIM] / l).astype(BF16)
            at = pl.ds(slot * tq, tq)
            for g in range(GROUP):
                o_ref[at, HEAD_DIM * g:HEAD_DIM * (g + 1)] = o[tq * g:tq * (g + 1)]
            oh_ref[:, at, :] = o.reshape(GROUP, tq, HEAD_DIM)
            lse_ref[:, at, :] = (m + jnp.log(l)).reshape(GROUP, tq, 1)

        first = 2 * j
        finish(jnp.maximum(first - 2, 0), 0)
        softmax(1)
        scores(first, 0)
        finish(jnp.maximum(first - 1, 0), 1)
        softmax(0)
        scores(first + 1, 1)

    tile, cur, prev, kv_spec, v_spec, sink_spec, bias_spec = _pair_specs(s, tq)
    tile32, tile16, col = pltpu.VMEM((rows, win), F32), pltpu.VMEM((rows, win), BF16), pltpu.VMEM((rows, 1), F32)
    return _hosted(
        body, rider, name="attn_b_fwd", grid=(KV_HEADS, pairs + 1),
        in_specs=[tile(HEAD_DIM, cur), kv_spec, v_spec, sink_spec, bias_spec],
        out_specs=[pl.BlockSpec((2 * tq, GROUP * HEAD_DIM), lambda g, j: (jnp.maximum(j - 1, 0), g)),
                   tile(HEAD_DIM, prev), tile(1, prev)],
        out_shape=[jax.ShapeDtypeStruct((s, BRANCH_W), BF16), jax.ShapeDtypeStruct((KV_HEADS, GROUP, s, HEAD_DIM), BF16),
                   jax.ShapeDtypeStruct((KV_HEADS, GROUP, s, 1), F32)],
        scratch_shapes=[tile32, tile32, tile16, tile16, col, col],
        args=(q.reshape(KV_HEADS, GROUP, s, HEAD_DIM), k, v1, sink_col, bias))


def _attn_b_bwd(q, k, v1, o, do, lse, sink_col, bias, rider=None, tq=TQ_B):
    s = q.shape[1]
    rows = GROUP * tq
    win = _win_keys(tq)
    pairs = s // (2 * tq)
    window, bias_index = _band(tq, s)

    def body(q_ref, k_ref, v_ref, o_ref, do_ref, lse_ref, sink_ref, bias_ref, dq_ref, dk_ref, dv_ref, dsink_ref,
             s0, s1, dp0, dp1, p0, p1, ds0, ds1, q0, q1, d0, d1, ls0, ls1, dl0, dl1):
        s_ref, dp_ref, p_ref, ds_ref = (s0, s1), (dp0, dp1), (p0, p1), (ds0, ds1)
        q_keep, do_keep, lse_keep, delta_keep = (q0, q1), (d0, d1), (ls0, ls1), (dl0, dl1)
        j = pl.program_id(1)

        @pl.when(j == 0)
        def _():
            for ref in (dk_ref, dv_ref, dsink_ref, s0, s1, dp0, dp1, p0, p1, ds0, ds1, q0, q1, d0, d1, ls0, ls1, dl0, dl1):
                ref[...] = jnp.zeros_like(ref)

        def scores(e, slot):
            at = pl.ds(slot * tq, tq)
            qq = q_ref[:, at, :].reshape(rows, HEAD_DIM)
            dd = do_ref[:, at, :].reshape(rows, HEAD_DIM)
            q_keep[slot][...] = qq
            do_keep[slot][...] = dd
            lse_keep[slot][...] = lse_ref[:, at, :].reshape(rows, 1)
            delta_keep[slot][...] = jnp.sum(dd.astype(F32) * o_ref[:, at, :].reshape(rows, HEAD_DIM).astype(F32), axis=-1,
                                            keepdims=True)
            sc = _nt(qq, k_ref[window(e), :]).reshape(GROUP, tq, win) + bias_ref[bias_index(e)][None]
            s_ref[slot][...] = sc.reshape(rows, win)
            dp_ref[slot][...] = _nt(dd, v_ref[window(e), :HEAD_DIM])

        def weights(slot):
            p = jnp.exp(s_ref[slot][...] - lse_keep[slot][...])
            p_ref[slot][...] = p.astype(BF16)
            ds_ref[slot][...] = (p * (dp_ref[slot][...] - delta_keep[slot][...])).astype(BF16)

        def grads(e, slot, live):
            at = window(e)
            ds = ds_ref[slot][...]
            dv_ref[at, :] += _tn(p_ref[slot][...], do_keep[slot][...])
            dk_ref[at, :] += _tn(ds, q_keep[slot][...])
            dq_ref[:, pl.ds(slot * tq, tq), :] = _nn(ds, k_ref[at, :]).reshape(GROUP, tq, HEAD_DIM)
            dsk = jnp.exp(sink_ref[...] - lse_keep[slot][...]) * delta_keep[slot][...] * live
            for g in range(GROUP):
                dsink_ref[g:g + 1, :] -= jnp.broadcast_to(jnp.sum(dsk[tq * g:tq * (g + 1)], axis=0, keepdims=True), (1, LANES))

        first = 2 * j
        live = jnp.where(j > 0, 1.0, 0.0)
        grads(jnp.maximum(first - 2, 0), 0, live)
        weights(1)
        scores(first, 0)
        grads(jnp.maximum(first - 1, 0), 1, live)
        weights(0)
        scores(first + 1, 1)

    tile, cur, prev, kv_spec, v_spec, sink_spec, bias_spec = _pair_specs(s, tq)
    dsink_spec = pl.BlockSpec((None, ACC_ROWS, LANES), lambda g, j: (g, 0, 0))
    shape4 = (KV_HEADS, GROUP, s, HEAD_DIM)
    tile32, tile16 = pltpu.VMEM((rows, win), F32), pltpu.VMEM((rows, win), BF16)
    keep, col = pltpu.VMEM((rows, HEAD_DIM), BF16), pltpu.VMEM((rows, 1), F32)
    return _hosted(
        body, rider, name="attn_b_bwd", grid=(KV_HEADS, pairs + 1),
        in_specs=[tile(HEAD_DIM, cur), kv_spec, v_spec, tile(HEAD_DIM, cur), tile(HEAD_DIM, cur), tile(1, cur), sink_spec,
                  bias_spec],
        out_specs=[tile(HEAD_DIM, prev), kv_spec, kv_spec, dsink_spec],
        out_shape=[jax.ShapeDtypeStruct(shape4, F32), jax.ShapeDtypeStruct((KV_HEADS, s, HEAD_DIM), F32),
                   jax.ShapeDtypeStruct((KV_HEADS, s, HEAD_DIM), F32), jax.ShapeDtypeStruct((KV_HEADS, ACC_ROWS, LANES), F32)],
        scratch_shapes=[tile32] * 4 + [tile16] * 4 + [keep] * 4 + [col] * 4,
        args=(q.reshape(shape4), k, v1, o.reshape(shape4), do.reshape(shape4), lse, sink_col, bias))


def _post_attn(ya, yb, gates, x, mod6, wbr_s, w_out, tm=256):
    s = x.shape[0]

    def body(ya_ref, yb_ref, g_ref, x_ref, mod_ref, wbr_ref, wo_ref, ua_ref, ub_ref, mg_ref, o_ref, x1_ref):
        ya_t, yb_t = ya_ref[...], yb_ref[...]
        ua = jnp.concatenate([_nn(ya_t, wbr_ref[j, 0]) for j in range(N_SHARD)], axis=1)
        ub = jnp.concatenate([_nn(yb_t, wbr_ref[j, 1]) for j in range(N_SHARD)], axis=1)
        merged = (_sigmoid(g_ref[:, :D_MODEL]) * ua + _sigmoid(g_ref[:, D_MODEL:]) * ub).astype(BF16)
        o = _nn(merged, wo_ref[...])
        ua_ref[...] = ua.astype(BF16)
        ub_ref[...] = ub.astype(BF16)
        mg_ref[...] = merged
        o_ref[...] = o.astype(BF16)
        x1_ref[...] = x_ref[...] + mod_ref[2:3, :] * o

    bf = jax.ShapeDtypeStruct((s, D_MODEL), BF16)
    return _call(
        body, name="post_attn", grid=(s // tm,),
        in_specs=[_row_spec(tm, BRANCH_W), _row_spec(tm, BRANCH_W), _row_spec(tm, 2 * D_MODEL), _row_spec(tm, D_MODEL),
                  _full_spec(mod6.shape), _full_spec(wbr_s.shape), _full_spec(w_out.shape)],
        out_specs=[_row_spec(tm, D_MODEL)] * 5,
        out_shape=[bf, bf, bf, bf, jax.ShapeDtypeStruct((s, D_MODEL), F32)],
        compiler_params=_params(("parallel",)),
    )(ya, yb, gates, x, mod6, wbr_s, w_out)


def _mlp_in(x1, mod6, g2, w_mi_s, tm=256):
    s = x1.shape[0]

    def body(x_ref, mod_ref, g_ref, w_ref, h2_ref, a_ref, hid_ref):
        xt = x_ref[...]
        r = lax.rsqrt(jnp.mean(xt * xt, axis=-1, keepdims=True) + NORM_EPS)
        h2 = ((xt * r * g_ref[...]) * (1.0 + mod_ref[4:5, :]) + mod_ref[3:4, :]).astype(BF16)
        h2_ref[...] = h2
        a = jnp.concatenate([_nn(h2, w_ref[j]) for j in range(N_SHARD)], axis=1)
        a_ref[...] = a.astype(BF16)
        hid_ref[...] = jnp.square(jnp.maximum(a, 0.0)).astype(BF16)

    return _call(
        body, name="mlp_in", grid=(s // tm,),
        in_specs=[_row_spec(tm, D_MODEL), _full_spec(mod6.shape), _full_spec(g2.shape), _full_spec(w_mi_s.shape)],
        out_specs=[_row_spec(tm, D_MODEL), _row_spec(tm, D_FF), _row_spec(tm, D_FF)],
        out_shape=[jax.ShapeDtypeStruct((s, D_MODEL), BF16), jax.ShapeDtypeStruct((s, D_FF), BF16),
                   jax.ShapeDtypeStruct((s, D_FF), BF16)],
        compiler_params=_params(("parallel",)),
    )(x1, mod6, g2, w_mi_s)


ACC_ROWS = 8


def _acc_spec():
    return pl.BlockSpec((ACC_ROWS, D_MODEL), lambda i: (0, 0))


def _acc_add(acc_ref, rows):
    @pl.when(pl.program_id(0) == 0)
    def _():
        acc_ref[...] = jnp.zeros_like(acc_ref)

    for r, val in enumerate(rows):
        acc_ref[r:r + 1, :] += jnp.sum(val, axis=0, keepdims=True)


def _mlp_out_loss(hid, x1, a, target, mod6, gf, w_mo, tm=256):
    s = x1.shape[0]

    def body(hid_ref, x_ref, a_ref, t_ref, mod_ref, gf_ref, w_ref, dx2_ref, dm_ref, da_ref, acc_ref):
        m = _nn(hid_ref[...], w_ref[...])
        gate2 = mod_ref[5:6, :]
        x2 = x_ref[...] + gate2 * m
        r = lax.rsqrt(jnp.mean(x2 * x2, axis=-1, keepdims=True) + NORM_EPS)
        xn = x2 * r
        err = xn * gf_ref[...] - t_ref[...]
        dy = err * (1.0 / D_MODEL)
        dxn = dy * gf_ref[...]
        dx2 = r * (dxn - xn * jnp.mean(dxn * xn, axis=-1, keepdims=True))
        dx2_ref[...] = dx2
        dm = (dx2 * gate2).astype(BF16)
        dm_ref[...] = dm
        da_ref[...] = (_nt(dm, w_ref[...]) * (2.0 * jnp.maximum(a_ref[...].astype(F32), 0.0))).astype(BF16)
        _acc_add(acc_ref, [err * err, dy * xn, dx2 * m])

    return _call(
        body, name="mlp_out_loss", grid=(s // tm,),
        in_specs=[_row_spec(tm, D_FF), _row_spec(tm, D_MODEL), _row_spec(tm, D_FF), _row_spec(tm, D_MODEL),
                  _full_spec(mod6.shape), _full_spec(gf.shape), _full_spec(w_mo.shape)],
        out_specs=[_row_spec(tm, D_MODEL), _row_spec(tm, D_MODEL), _row_spec(tm, D_FF), _acc_spec()],
        out_shape=[jax.ShapeDtypeStruct((s, D_MODEL), F32), jax.ShapeDtypeStruct((s, D_MODEL), BF16),
                   jax.ShapeDtypeStruct((s, D_FF), BF16), jax.ShapeDtypeStruct((ACC_ROWS, D_MODEL), F32)],
        compiler_params=_params(("arbitrary",)),
    )(hid, x1, a, target, mod6, gf, w_mo)


def _norm_bwd(dh, xt, gain, scale):
    r = lax.rsqrt(jnp.mean(xt * xt, axis=-1, keepdims=True) + NORM_EPS)
    xn = xt * r
    dxn = dh * (gain * (1.0 + scale))
    dx = r * (dxn - xn * jnp.mean(dxn * xn, axis=-1, keepdims=True))
    return dx, [dh, dh * xn * gain, dh * xn * (1.0 + scale)]


def _mlp_bwd(da, x1, dx2, o, mod6, g2, w_mi_s, rider=None, tm=256):
    s = x1.shape[0]

    def body(da_ref, x_ref, dx2_ref, o_ref, mod_ref, g_ref, w_ref, dx1_ref, do_ref, acc_ref):
        dh2 = _nt(da_ref[:, :D_MODEL], w_ref[0])
        for j in range(1, N_SHARD):
            dh2 += _nt(da_ref[:, D_MODEL * j:D_MODEL * (j + 1)], w_ref[j])
        dx, sums = _norm_bwd(dh2, x_ref[...], g_ref[...], mod_ref[4:5, :])
        dx1 = dx2_ref[...] + dx
        dx1_ref[...] = dx1
        do_ref[...] = (dx1 * mod_ref[2:3, :]).astype(BF16)
        _acc_add(acc_ref, sums + [dx1 * o_ref[...].astype(F32)])

    return _hosted(
        body, rider, name="mlp_bwd", grid=(s // tm,),
        in_specs=[_row_spec(tm, D_FF), _row_spec(tm, D_MODEL), _row_spec(tm, D_MODEL), _row_spec(tm, D_MODEL),
                  _full_spec(mod6.shape), _full_spec(g2.shape), _full_spec(w_mi_s.shape)],
        out_specs=[_row_spec(tm, D_MODEL), _row_spec(tm, D_MODEL), _acc_spec()],
        out_shape=[jax.ShapeDtypeStruct((s, D_MODEL), F32), jax.ShapeDtypeStruct((s, D_MODEL), BF16),
                   jax.ShapeDtypeStruct((ACC_ROWS, D_MODEL), F32)],
        args=(da, x1, dx2, o, mod6, g2, w_mi_s))


def _merge_bwd(do, gates, ua, ub, w_out, wbr_s, rider=None, tm=256):
    s = do.shape[0]

    def body(do_ref, g_ref, ua_ref, ub_ref, wo_ref, wbr_ref, dua_ref, dub_ref, dg_ref, doa_ref, dob_ref):
        dmerged = _nt(do_ref[...], wo_ref[...])
        for b, (u_ref, du_ref, dy_ref) in enumerate(((ua_ref, dua_ref, doa_ref), (ub_ref, dub_ref, dob_ref))):
            sg = _sigmoid(g_ref[:, D_MODEL * b:D_MODEL * (b + 1)])
            du = (dmerged * sg).astype(BF16)
            du_ref[...] = du
            dg_ref[:, D_MODEL * b:D_MODEL * (b + 1)] = (dmerged * u_ref[...].astype(F32) * sg * (1.0 - sg)).astype(BF16)
            w = BRANCH_W // 2
            dy = _nt(du[:, :w], wbr_ref[0, b])
            for j in range(1, N_SHARD):
                dy += _nt(du[:, w * j:w * (j + 1)], wbr_ref[j, b])
            dyb = dy.astype(BF16)
            for h in range(Q_HEADS):
                dy_ref[h] = dyb[:, HEAD_DIM * h:HEAD_DIM * (h + 1)]

    bf = jax.ShapeDtypeStruct((s, D_MODEL), BF16)
    heads = jax.ShapeDtypeStruct((Q_HEADS, s, HEAD_DIM), BF16)
    return _hosted(
        body, rider, name="merge_bwd", grid=(s // tm,),
        in_specs=[_row_spec(tm, D_MODEL), _row_spec(tm, 2 * D_MODEL), _row_spec(tm, D_MODEL), _row_spec(tm, D_MODEL),
                  _full_spec(w_out.shape), _full_spec(wbr_s.shape)],
        out_specs=[_row_spec(tm, D_MODEL), _row_spec(tm, D_MODEL), _row_spec(tm, 2 * D_MODEL),
                   _heads_spec(Q_HEADS, tm), _heads_spec(Q_HEADS, tm)],
        out_shape=[bf, bf, jax.ShapeDtypeStruct((s, 2 * D_MODEL), BF16), heads, heads],
        args=(do, gates, ua, ub, w_out, wbr_s))


def _qk_bwd(dqa, dka, dva, dqb, dkb, dvb, qkraw, dgates, gq, gk, bd, tabs, rider=None, tm=256):
    s = qkraw.shape[0]

    def body(dqa_ref, dka_ref, dva_ref, dqb_ref, dkb_ref, dvb_ref, raw_ref, dg_ref, gq_ref, gk_ref, bd_ref,
             ca, la, ha, cb, lb, hb, dp_ref, acc_ref, pair_ref):
        bdm = bd_ref[...]
        tab_a = (ca[...], la[...], ha[...])
        tab_b = (cb[...], lb[...], hb[...])

        def pair(ref, first):
            pair_ref[:, :HEAD_DIM] = ref[first]
            pair_ref[:, HEAD_DIM:] = ref[first + 1]
            return pair_ref[...]

        def norm_rope_a_bwd(dz, raw, gain):
            dzn = _rope(dz, *tab_a, 16, sign=-1.0)
            rinv = lax.rsqrt(_head_mean(raw * raw, bdm) + NORM_EPS)
            zhat = raw * rinv
            dzhat = dzn * gain
            return rinv * (dzhat - zhat * _head_mean(dzhat * zhat, bdm)), dzn * zhat

        gq_rows = jnp.zeros((tm, LANES), F32)
        for i in range(Q_HEADS // 2):
            at = slice(LANES * i, LANES * (i + 1))
            draw, gsum = norm_rope_a_bwd(pair(dqa_ref, 2 * i) * Q_SCALE, raw_ref[:, at], gq_ref[...])
            dp_ref[:, at] = draw.astype(BF16)
            gq_rows += gsum
        off = BRANCH_W
        draw, gk_rows = norm_rope_a_bwd(pair(dka_ref, 0), raw_ref[:, off:off + LANES], gk_ref[...])
        dp_ref[:, off:off + LANES] = draw.astype(BF16)
        off += KV_W
        dp_ref[:, off:off + LANES] = pair(dva_ref, 0).astype(BF16)
        off += KV_W
        for i in range(Q_HEADS // 2):
            dz = _rope(pair(dqb_ref, 2 * i) * Q_SCALE, *tab_b, 32, sign=-1.0)
            dp_ref[:, off + LANES * i:off + LANES * (i + 1)] = dz.astype(BF16)
        off += BRANCH_W
        dp_ref[:, off:off + LANES] = _rope(pair(dkb_ref, 0), *tab_b, 32, sign=-1.0).astype(BF16)
        off += KV_W
        dp_ref[:, off:off + LANES] = pair(dvb_ref, 0).astype(BF16)
        dp_ref[:, QK_W:] = dg_ref[...]

        @pl.when(pl.program_id(0) == 0)
        def _():
            acc_ref[...] = jnp.zeros_like(acc_ref)

        acc_ref[0:1, :] += jnp.sum(gq_rows, axis=0, keepdims=True)
        acc_ref[1:2, :] += jnp.sum(gk_rows, axis=0, keepdims=True)

    tab_spec = _row_spec(tm, LANES)
    return _hosted(
        body, rider, name="qk_bwd", grid=(s // tm,),
        in_specs=[_heads_spec(Q_HEADS, tm), _heads_spec(KV_HEADS, tm), _heads_spec(KV_HEADS, tm),
                  _heads_spec(Q_HEADS, tm), _heads_spec(KV_HEADS, tm), _heads_spec(KV_HEADS, tm),
                  _row_spec(tm, BRANCH_W + KV_W), _row_spec(tm, 2 * D_MODEL),
                  _full_spec(gq.shape), _full_spec(gk.shape), _full_spec(bd.shape)] + [tab_spec] * 6,
        out_specs=[_row_spec(tm, IN_W), pl.BlockSpec((ACC_ROWS, LANES), lambda i: (0, 0))],
        out_shape=[jax.ShapeDtypeStruct((s, IN_W), BF16), jax.ShapeDtypeStruct((ACC_ROWS, LANES), F32)],
        scratch_shapes=[pltpu.VMEM((tm, LANES), F32)],
        args=(dqa, dka, dva, dqb, dkb, dvb, qkraw, dgates, gq, gk, bd, *tabs))


def _in_proj_bwd(dproj, x, dx1, mod6, g1, w_in_s, tm=256):
    s = x.shape[0]
    w = IN_W // N_SHARD

    def body(dp_ref, x_ref, dx1_ref, mod_ref, g_ref, w_ref, gx_ref, acc_ref):
        dh = _nt(dp_ref[:, :w], w_ref[0])
        for j in range(1, N_SHARD):
            dh += _nt(dp_ref[:, w * j:w * (j + 1)], w_ref[j])
        dx, sums = _norm_bwd(dh, x_ref[...], g_ref[...], mod_ref[1:2, :])
        gx_ref[...] = dx1_ref[...] + dx
        _acc_add(acc_ref, sums)

    return _call(
        body, name="in_proj_bwd", grid=(s // tm,),
        in_specs=[_row_spec(tm, IN_W), _row_spec(tm, D_MODEL), _row_spec(tm, D_MODEL),
                  _full_spec(mod6.shape), _full_spec(g1.shape), _full_spec(w_in_s.shape)],
        out_specs=[_row_spec(tm, D_MODEL), _acc_spec()],
        out_shape=[jax.ShapeDtypeStruct((s, D_MODEL), F32), jax.ShapeDtypeStruct((ACC_ROWS, D_MODEL), F32)],
        compiler_params=_params(("arbitrary",)),
    )(dproj, x, dx1, mod6, g1, w_in_s)


def _wgrad(name, a, b, out_shape, out_spec, tm, tn, tk=4096):
    s, m = a.shape
    n = b.shape[1]
    tk = min(tk, s)
    nk = s // tk

    def body(a_ref, b_ref, o_ref, acc_ref):
        k = pl.program_id(2)

        @pl.when(k == 0)
        def _():
            acc_ref[...] = jnp.zeros_like(acc_ref)

        acc_ref[...] += _tn(a_ref[...], b_ref[...])

        @pl.when(k == nk - 1)
        def _():
            o_ref[...] = acc_ref[...].reshape(o_ref.shape)

    return _call(
        body, name=name, grid=(m // tm, n // tn, nk),
        in_specs=[pl.BlockSpec((tk, tm), lambda i, j, k: (k, i)), pl.BlockSpec((tk, tn), lambda i, j, k: (k, j))],
        out_specs=out_spec, out_shape=jax.ShapeDtypeStruct(out_shape, F32),
        scratch_shapes=[pltpu.VMEM((tm, tn), F32)],
        compiler_params=_params(("parallel", "parallel", "arbitrary")),
    )(a, b)


def _wgrad_branch(ya, yb, dua, dub, tk=2048):
    s = ya.shape[0]
    tk = min(tk, s)
    nk = s // tk
    w = D_MODEL // N_SHARD

    def body(ya_ref, yb_ref, dua_ref, dub_ref, o_ref, acc_ref):
        b, k = pl.program_id(0), pl.program_id(2)

        @pl.when(k == 0)
        def _():
            acc_ref[...] = jnp.zeros_like(acc_ref)

        @pl.when(b == 0)
        def _():
            acc_ref[...] += _tn(ya_ref[...], dua_ref[...])

        @pl.when(b == 1)
        def _():
            acc_ref[...] += _tn(yb_ref[...], dub_ref[...])

        @pl.when(k == nk - 1)
        def _():
            o_ref[...] = acc_ref[...]

    y_spec = pl.BlockSpec((tk, BRANCH_W), lambda b, j, k: (k, 0))
    du_spec = pl.BlockSpec((tk, w), lambda b, j, k: (k, j))
    return _call(
        body, name="wgrad_branch", grid=(2, N_SHARD, nk),
        in_specs=[y_spec, y_spec, du_spec, du_spec],
        out_specs=pl.BlockSpec((None, None, BRANCH_W, w), lambda b, j, k: (j, b, 0, 0)),
        out_shape=jax.ShapeDtypeStruct((N_SHARD, 2, BRANCH_W, w), F32),
        scratch_shapes=[pltpu.VMEM((BRANCH_W, w), F32)],
        compiler_params=_params(("parallel", "parallel", "arbitrary")),
    )(ya, yb, dua, dub)


def _local_step(x, target, mod6, g1, g2, gf, gq2, gk2, sink, w_in_s, rest, cj=None):
    s = x.shape[0]
    dist = cj is not None
    tabs = _rope_tables(s)
    bd = _block_diag()
    sink_col = jnp.repeat(sink.reshape(KV_HEADS, GROUP, 1), TQ_B, axis=1).reshape(KV_HEADS, GROUP * TQ_B, 1)
    shard = D_MODEL // N_SHARD

    h, qkraw, qa, ka, va, qb, kb, vb, gates = _in_proj(x, mod6, g1, w_in_s, gq2, gk2, bd, tabs)
    bias = _window_bias(TQ_B)
    (yb, yb_heads, lse_b), _ = _attn_b_fwd(qb, kb, vb, sink_col, bias)
    (ya, ya_heads, lse_a), gathered = _attn_a_fwd(qa, ka, va, rider=_gather_rider(rest) if dist else None)
    wbr_s, w_out, w_mi_s, w_mo = gathered if dist else rest
    wbr_s = wbr_s.reshape(N_SHARD, 2, BRANCH_W, shard)
    w_out = w_out.reshape(D_MODEL, D_MODEL)
    w_mo = w_mo.reshape(D_FF, D_MODEL)
    ua, ub, merged, o, x1 = _post_attn(ya, yb, gates, x, mod6, wbr_s, w_out)
    h2, a, hid = _mlp_in(x1, mod6, g2, w_mi_s)
    dx2, dm, da, acc_out = _mlp_out_loss(hid, x1, a, target, mod6, gf, w_mo)

    g_w_mo = _wgrad("wgrad_mlp_out", hid, dm, (D_FF, D_MODEL), pl.BlockSpec((D_MODEL, D_MODEL), lambda i, j, k: (i, 0)),
                    D_MODEL, D_MODEL).reshape(N_SHARD, D_MODEL, D_MODEL)
    g_w_mi = _wgrad("wgrad_mlp_in", h2, da, (N_SHARD, D_MODEL, D_MODEL),
                    pl.BlockSpec((None, D_MODEL, D_MODEL), lambda i, j, k: (j, i, 0)), D_MODEL, D_MODEL)
    mlp = _Reduction(("mlp_out", "mlp_in"), (g_w_mo, g_w_mi), cj)
    (dx1, do, acc_mlp), got = _mlp_bwd(da, x1, dx2, o, mod6, g2, w_mi_s, rider=mlp.swap() if dist else None)
    (dua, dub, dgates, doa, dob), landed = _merge_bwd(do, gates, ua, ub, w_out, wbr_s, rider=mlp.add(got) if dist else None)
    g_w_out = _wgrad("wgrad_out", merged, do, (D_MODEL, D_MODEL), pl.BlockSpec((D_MODEL, D_MODEL), lambda i, j, k: (i, 0)),
                     D_MODEL, D_MODEL).reshape(N_SHARD, shard, D_MODEL)
    g_wbr = _wgrad_branch(ya, yb, dua, dub)
    out = _Reduction(("out", "branch"), (g_w_out, g_wbr.reshape(N_SHARD, 2 * BRANCH_W, shard)), cj)
    (dqa, dka, dva), landings = _attn_a_bwd(qa, ka, va, ya_heads, doa, lse_a,
                                            rider=_riders(out.swap(), mlp.total(landed)) if dist else None)
    (dqb, dkb, dvb, dsink), _ = _attn_b_bwd(qb, kb, vb, yb_heads, dob, lse_b, sink_col, bias)
    heads = (Q_HEADS, s, HEAD_DIM)
    (dproj, acc_qk), landed = _qk_bwd(dqa.reshape(heads), dka, dva, dqb.reshape(heads), dkb, dvb, qkraw, dgates, gq2, gk2, bd,
                                      tabs, rider=out.add(landings[:2]) if dist else None)
    w = IN_W // N_SHARD
    g_w_in = _wgrad("wgrad_in", h, dproj, (N_SHARD, D_MODEL, w), pl.BlockSpec((None, D_MODEL, w), lambda i, j, k: (j, i, 0)),
                    D_MODEL, w)
    grad_x, acc_in = _in_proj_bwd(dproj, x, dx1, mod6, g1, w_in_s)
    accs = (acc_out, acc_mlp, acc_in, acc_qk, dsink)
    if not dist:
        return grad_x, (g_w_in, g_wbr, g_w_out, g_w_mi, g_w_mo), accs
    first = _Reduction(("in",), (g_w_in,), cj)
    landed_in = _alone("scatter_in", first.add(_alone("swap_in", first.swap())))
    r_in, r_out, r_br = _alone("join_in_out_branch", _riders(first.total(landed_in), out.total(landed)))
    r_mo, r_mi = landings[2:]
    return grad_x, (r_in, r_br, r_out, r_mi, r_mo), accs


def _me():
    return lax.axis_index("x"), lax.axis_index("y"), lax.axis_index("c")


def _peer(d):
    x, y, c = _me()
    return (1 - x if d & 4 else x, 1 - y if d & 2 else y, 1 - c if d & 1 else c)


def _dev_index(p):
    return 4 * p[0] + 2 * p[1] + p[2]


def _chip_index(p):
    return 2 * p[0] + p[1]


def _remote(src, dst, send_sem, recv_sem, to):
    return pltpu.make_async_remote_copy(src_ref=src, dst_ref=dst, send_sem=send_sem, recv_sem=recv_sem,
                                        device_id=to, device_id_type=MESH)


SLOT_ROWS = 8


def _ada_fwd(c, w_ada, b4):
    cols = w_ada.shape[1]

    def body(c_ref, w_ref, b_ref, mod_ref, sc_ref, cbuf, pbuf, mbuf, send1, recv1, send2, recv2):
        me = _me()
        mine, chip = _dev_index(me), _chip_index(me)
        cbuf[mine] = jnp.broadcast_to(c_ref[...], (SLOT_ROWS, D_MODEL))
        gather = [_remote(cbuf.at[mine], cbuf.at[mine], send1.at[d - 1], recv1.at[d - 1], _peer(d)) for d in range(1, N_DEV)]
        for cp in gather:
            cp.start()
        for d in range(1, N_DEV):
            _remote(cbuf.at[mine], cbuf.at[_dev_index(_peer(d))], send1.at[d - 1], recv1.at[d - 1], _peer(d)).wait_recv()
        call = cbuf[...].reshape(N_DEV * SLOT_ROWS, D_MODEL)
        sc = call * _sigmoid(call)
        for s in range(N_DEV):
            sc_ref[s:s + 1, :] = sc[SLOT_ROWS * s:SLOT_ROWS * s + 1]
        part = _nn(sc.astype(BF16), w_ref[...].astype(BF16)) + b_ref[pl.ds(chip, 1), :]
        pbuf[...] = part.reshape(N_DEV, SLOT_ROWS, cols)
        mbuf[chip] = pbuf[mine]
        spread = [_remote(pbuf.at[_dev_index(_peer(d))], mbuf.at[chip], send2.at[d // 2 - 1], recv2.at[d // 2 - 1], _peer(d))
                  for d in (2, 4, 6)]
        for cp in spread:
            cp.start()
        for d in (2, 4, 6):
            _remote(pbuf.at[mine], mbuf.at[_chip_index(_peer(d))], send2.at[d // 2 - 1], recv2.at[d // 2 - 1],
                    _peer(d)).wait_recv()
        half = D_MODEL // 2
        for p in range(2 * 6):
            col = half * p
            mod_ref[p // 2:p // 2 + 1, half * (p % 2):half * (p % 2 + 1)] = mbuf[col // cols, 0:1, col % cols:col % cols + half]
        for cp in gather + spread:
            cp.wait_send()

    vm = pl.BlockSpec(memory_space=pltpu.VMEM)
    return _call(
        body, name="ada_fwd", in_specs=[vm, vm, vm], out_specs=[vm, vm],
        out_shape=[jax.ShapeDtypeStruct((6, D_MODEL), F32), jax.ShapeDtypeStruct((N_DEV, D_MODEL), F32)],
        scratch_shapes=[pltpu.VMEM((N_DEV, SLOT_ROWS, D_MODEL), F32), pltpu.VMEM((N_DEV, SLOT_ROWS, cols), F32),
                        pltpu.VMEM((N_SHARD, SLOT_ROWS, cols), F32),
                        pltpu.SemaphoreType.DMA((N_DEV - 1,)), pltpu.SemaphoreType.DMA((N_DEV - 1,)),
                        pltpu.SemaphoreType.DMA((N_SHARD - 1,)), pltpu.SemaphoreType.DMA((N_SHARD - 1,))],
        compiler_params=_params(),
    )(c, w_ada, b4)


PACK_ROWS = 16
PACK_W = 3 * D_MODEL


def _ada_bwd(acc_out, acc_mlp, acc_in, acc_qk, dsink, sc_all):
    cols = 6 * D_MODEL // N_SHARD

    def body(out_ref, mlp_ref, in_ref, qk_ref, dsink_ref, sc_ref,
             gwa_ref, gba_ref, gn1_ref, gn2_ref, gf_ref, gq_ref, gk_ref, gs_ref, blk, send, recv):
        me = _me()
        mine, chip = _dev_index(me), _chip_index(me)
        blk[mine] = jnp.zeros((PACK_ROWS, PACK_W), F32)
        dmod = (in_ref, 0), (in_ref, 1), (mlp_ref, 3), (mlp_ref, 0), (mlp_ref, 1), (out_ref, 2)
        half = D_MODEL // 2
        for p in range(2 * 6):
            ref, row = dmod[p // 2]
            col = half * p
            blk[mine, col // cols:col // cols + 1, col % cols:col % cols + half] = ref[row:row + 1, half * (p % 2):half * (p % 2 + 1)]
        blk[mine, 4:5, 0:D_MODEL] = in_ref[2:3, :]
        blk[mine, 4:5, D_MODEL:2 * D_MODEL] = mlp_ref[2:3, :]
        blk[mine, 4:5, 2 * D_MODEL:] = out_ref[1:2, :]
        blk[mine, 5:6, 0:LANES] = qk_ref[0:1, :]
        blk[mine, 5:6, LANES:2 * LANES] = qk_ref[1:2, :]
        for g in range(KV_HEADS):
            blk[mine, 8 + GROUP * g:8 + GROUP * (g + 1), 0:LANES] = dsink_ref[g, 0:GROUP, :]
        copies = [_remote(blk.at[mine], blk.at[mine], send.at[d - 1], recv.at[d - 1], _peer(d)) for d in range(1, N_DEV)]
        for cp in copies:
            cp.start()
        for d in range(1, N_DEV):
            _remote(blk.at[mine], blk.at[_dev_index(_peer(d))], send.at[d - 1], recv.at[d - 1], _peer(d)).wait_recv()
        tot = blk[0]
        for s in range(1, N_DEV):
            tot = tot + blk[s]
        for j in range(N_SHARD):
            gba_ref[:, cols * j:cols * (j + 1)] = tot[j:j + 1, :cols]
        gn1_ref[...] = tot[4:5, 0:D_MODEL]
        gn2_ref[...] = tot[4:5, D_MODEL:2 * D_MODEL]
        gf_ref[...] = tot[4:5, 2 * D_MODEL:]
        gq_ref[...] = tot[5:6, 0:HEAD_DIM] + tot[5:6, HEAD_DIM:2 * HEAD_DIM]
        gk_ref[...] = tot[5:6, LANES:LANES + HEAD_DIM] + tot[5:6, LANES + HEAD_DIM:2 * LANES]
        sq = tot[8:16, 0:Q_HEADS]
        diag = lax.broadcasted_iota(jnp.int32, sq.shape, 0) == lax.broadcasted_iota(jnp.int32, sq.shape, 1)
        gs_ref[...] = jnp.sum(jnp.where(diag, sq, 0.0), axis=0, keepdims=True)
        dm = jnp.concatenate([blk[s, pl.ds(chip, 1), pl.ds(0, cols)] for s in range(N_DEV)], axis=0)
        gwa_ref[...] = _tn(sc_ref[...], dm)
        for cp in copies:
            cp.wait_send()

    vm = pl.BlockSpec(memory_space=pltpu.VMEM)
    row = lambda n: jax.ShapeDtypeStruct((1, n), F32)
    return _call(
        body, name="ada_bwd", in_specs=[vm] * 6, out_specs=[vm] * 8,
        out_shape=[jax.ShapeDtypeStruct((D_MODEL, cols), F32), row(6 * D_MODEL), row(D_MODEL), row(D_MODEL), row(D_MODEL),
                   row(HEAD_DIM), row(HEAD_DIM), row(Q_HEADS)],
        scratch_shapes=[pltpu.VMEM((N_DEV, PACK_ROWS, PACK_W), F32),
                        pltpu.SemaphoreType.DMA((N_DEV - 1,)), pltpu.SemaphoreType.DMA((N_DEV - 1,))],
        compiler_params=_params(),
    )(acc_out, acc_mlp, acc_in, acc_qk, dsink, sc_all)


def _cast_weights(ws):
    n = len(ws)

    def body(*refs):
        src, out, tmp, sems = refs[:n], refs[n:2 * n], refs[2 * n:3 * n], refs[3 * n]
        chip = _chip_index(_me())
        copies = []
        for a in range(n):
            tmp[a][...] = src[a][...].astype(BF16)
            cp = pltpu.make_async_copy(tmp[a], out[a].at[chip], sems.at[a])
            cp.start()
            copies.append(cp)
        for cp in copies:
            cp.wait()

    vm = pl.BlockSpec(memory_space=pltpu.VMEM)
    return _call(
        body, name="cast_weights", in_specs=[vm] * n, out_specs=[ANY] * n,
        out_shape=[jax.ShapeDtypeStruct((N_SHARD,) + w.shape, BF16) for w in ws],
        scratch_shapes=[pltpu.VMEM(w.shape, BF16) for w in ws] + [pltpu.SemaphoreType.DMA((n,))],
        compiler_params=_params(),
    )(*ws)


def _half_rows(ref_rows, c):
    half = ref_rows // 2
    return pl.ds(pl.multiple_of(c * half, 8), half)


class _Rider:
    def __init__(self, inputs, out_shape, aliases, n_sems, start, finish, middle=None):
        self.inputs, self.out_shape, self.aliases, self.n_sems = list(inputs), list(out_shape), dict(aliases), n_sems
        self.start, self.finish, self.middle = start, finish, middle


def _riders(*rs):
    ins = [0]
    outs = [0]
    sems = [0]
    for r in rs:
        ins.append(ins[-1] + len(r.inputs))
        outs.append(outs[-1] + len(r.out_shape))
        sems.append(sems[-1] + r.n_sems)

    def phase(which):
        def run(in_refs, out_refs, sem):
            for k, r in enumerate(rs):
                fn = getattr(r, which)
                if fn is not None:
                    fn(in_refs[ins[k]:ins[k + 1]], out_refs[outs[k]:outs[k + 1]], lambda j, base=sems[k]: sem(base + j))
        return run

    aliases = {ins[k] + i: outs[k] + o for k, r in enumerate(rs) for i, o in r.aliases.items()}
    return _Rider([a for r in rs for a in r.inputs], [o for r in rs for o in r.out_shape], aliases, sems[-1],
                  phase("start"), phase("finish"), phase("middle") if any(r.middle for r in rs) else None)


def _hosted(body, rider, *, name, grid, in_specs, out_specs, out_shape, args, scratch_shapes=(), middle_at=None):
    sem = ("arbitrary",) * len(grid)
    if rider is None:
        res = _call(body, name=name, grid=grid, in_specs=in_specs, out_specs=out_specs, out_shape=out_shape,
                    scratch_shapes=list(scratch_shapes), compiler_params=_params(sem))(*args)
        return res, ()
    n_in, n_out, n_scr = len(in_specs), len(out_specs), len(scratch_shapes)
    r_in, r_out = len(rider.inputs), len(rider.out_shape)

    def riding(*refs):
        at = 0
        parts = []
        for size in (n_in, r_in, n_out, r_out, n_scr):
            parts.append(refs[at:at + size])
            at += size
        ins, rider_ins, outs, rider_outs, scratch = parts
        sems = refs[at]
        step = pl.program_id(0)
        for axis in range(1, len(grid)):
            step = step * grid[axis] + pl.program_id(axis)
        steps = 1
        for size in grid:
            steps *= size

        def sem_at(k):
            return sems.at[k]

        @pl.when(step == 0)
        def _():
            rider.start(rider_ins, rider_outs, sem_at)

        body(*ins, *outs, *scratch)
        if rider.middle is not None:
            @pl.when(step == middle_at)
            def _():
                rider.middle(rider_ins, rider_outs, sem_at)

        @pl.when(step == steps - 1)
        def _():
            rider.finish(rider_ins, rider_outs, sem_at)

    res = _call(
        riding, name=name, grid=grid, in_specs=list(in_specs) + [ANY] * r_in, out_specs=list(out_specs) + [ANY] * r_out,
        out_shape=list(out_shape) + rider.out_shape,
        input_output_aliases={n_in + i: n_out + o for i, o in rider.aliases.items()},
        scratch_shapes=list(scratch_shapes) + [pltpu.SemaphoreType.DMA((rider.n_sems,))],
        compiler_params=_params(sem),
    )(*args, *rider.inputs)
    return res[:n_out], res[n_out:]


def _alone(name, rider):
    n_in, n_out = len(rider.inputs), len(rider.out_shape)

    def body(*refs):
        ins, outs, sems = refs[:n_in], refs[n_in:n_in + n_out], refs[n_in + n_out]

        def sem_at(k):
            return sems.at[k]

        rider.start(ins, outs, sem_at)
        if rider.middle is not None:
            rider.middle(ins, outs, sem_at)
        rider.finish(ins, outs, sem_at)

    return _call(
        body, name=name, in_specs=[ANY] * n_in, out_specs=[ANY] * n_out, out_shape=rider.out_shape,
        input_output_aliases=rider.aliases, scratch_shapes=[pltpu.SemaphoreType.DMA((rider.n_sems,))],
    )(*rider.inputs)


OTHER_CHIPS = (2, 4, 6)


def _gather_rider(stacked):
    n = len(stacked)

    def flights(bufs, sem):
        me = _me()
        chip, sib = _chip_index(me), _peer(1)
        out = []
        for a in range(n):
            mine, theirs = (_half_rows(bufs[a].shape[1], c) for c in (me[2], 1 - me[2]))
            for j, d in enumerate(OTHER_CHIPS):
                k = 3 * a + j
                from_chip = _chip_index(_peer(d))
                own, landed, passed = bufs[a].at[chip, mine], bufs[a].at[from_chip, mine], bufs[a].at[from_chip, theirs]
                out.append((_remote(own, own, sem(k), sem(3 * n + k), _peer(d)),
                            _remote(own, landed, sem(k), sem(3 * n + k), _peer(d)),
                            _remote(landed, landed, sem(6 * n + k), sem(9 * n + k), sib),
                            _remote(passed, passed, sem(6 * n + k), sem(9 * n + k), sib)))
        return out

    def start(ins, outs, sem):
        for send, _, _, _ in flights(outs, sem):
            send.start()

    def middle(ins, outs, sem):
        for _, arrival, pass_on, _ in flights(outs, sem):
            arrival.wait_recv()
            pass_on.start()

    def finish(ins, outs, sem):
        every = flights(outs, sem)
        for _, _, _, passed_to_me in every:
            passed_to_me.wait_recv()
        for send, _, pass_on, _ in every:
            send.wait_send()
            pass_on.wait_send()

    return _Rider(stacked, [jax.ShapeDtypeStruct(w.shape, w.dtype) for w in stacked], {a: a for a in range(n)}, 12 * n,
                  start, finish, middle)


def _swap_rider(grads):
    n = len(grads)

    def copies(ins, outs, sem):
        c = _me()[2]
        return [_remote(ins[a].at[pl.ds(0, N_SHARD), _half_rows(ins[a].shape[1], 1 - c)], outs[a], sem(a), sem(n + a), _peer(1))
                for a in range(n)]

    def start(ins, outs, sem):
        for cp in copies(ins, outs, sem):
            cp.start()

    def finish(ins, outs, sem):
        for cp in copies(ins, outs, sem):
            cp.wait()

    return _Rider(grads, [jax.ShapeDtypeStruct((N_SHARD, g.shape[1] // 2, g.shape[2]), F32) for g in grads], {}, 2 * n,
                  start, finish)


def _row_tile(rows):
    return min(rows, 256)


def _add_halves(name, g, got, cj):
    _, half, cols = got.shape
    tr = _row_tile(half)
    nt = half // tr

    def body(cj_ref, g_ref, got_ref, o_ref):
        o_ref[...] = (g_ref[...] + got_ref[...]).astype(BF16)

    spec = pl.BlockSpec((None, tr, cols), lambda i, s, cj: (s, i, 0))
    return _call(
        body, name=name,
        grid_spec=pltpu.PrefetchScalarGridSpec(
            num_scalar_prefetch=1, grid=(nt, N_SHARD),
            in_specs=[pl.BlockSpec((None, tr, cols), lambda i, s, cj: (s, cj[0] * nt + i, 0)), spec], out_specs=spec),
        out_shape=jax.ShapeDtypeStruct(got.shape, BF16), compiler_params=_params(("parallel", "parallel")),
    )(cj, g, got)


def _scatter_rider(sums):
    n = len(sums)

    def flights(ins, outs, sem):
        chip = _chip_index(_me())
        out = []
        for a in range(n):
            for j, d in enumerate(OTHER_CHIPS):
                k = 3 * a + j
                other = _chip_index(_peer(d))
                out.append((_remote(ins[a].at[other], outs[a].at[chip], sem(k), sem(3 * n + k), _peer(d)),
                            _remote(ins[a].at[chip], outs[a].at[other], sem(k), sem(3 * n + k), _peer(d))))
        return out

    def start(ins, outs, sem):
        for send, _ in flights(ins, outs, sem):
            send.start()

    def finish(ins, outs, sem):
        every = flights(ins, outs, sem)
        for _, arrival in every:
            arrival.wait_recv()
        for send, _ in every:
            send.wait_send()

    return _Rider(sums, [jax.ShapeDtypeStruct(v.shape, v.dtype) for v in sums], {}, 6 * n, start, finish)


def _sum_chips(name, g, got, landed, cj):
    _, half, cols = got.shape
    tr = _row_tile(half)
    nt = half // tr

    def body(cj_ref, g_ref, got_ref, landed_ref, o_ref):
        own = g_ref[...] + got_ref[...]
        total = None
        for s in range(N_SHARD):
            term = jnp.where(cj_ref[1] == s, own, landed_ref[s].astype(F32))
            total = term if total is None else total + term
        o_ref[...] = total

    return _call(
        body, name=name,
        grid_spec=pltpu.PrefetchScalarGridSpec(
            num_scalar_prefetch=1, grid=(nt,),
            in_specs=[pl.BlockSpec((None, tr, cols), lambda i, cj: (cj[1], cj[0] * nt + i, 0)),
                      pl.BlockSpec((None, tr, cols), lambda i, cj: (cj[1], i, 0)),
                      pl.BlockSpec((N_SHARD, tr, cols), lambda i, cj: (0, i, 0))],
            out_specs=pl.BlockSpec((tr, cols), lambda i, cj: (cj[0] * nt + i, 0))),
        out_shape=jax.ShapeDtypeStruct((2 * half, cols), F32), compiler_params=_params(("parallel",)),
    )(cj, g, got, landed)


def _join_rider(shards):
    n = len(shards)

    def flights(bufs, sem):
        c = _me()[2]
        out = []
        for a in range(n):
            mine, theirs = (bufs[a].at[_half_rows(bufs[a].shape[0], cc)] for cc in (c, 1 - c))
            out.append((_remote(mine, mine, sem(a), sem(n + a), _peer(1)), _remote(theirs, theirs, sem(a), sem(n + a), _peer(1))))
        return out

    def start(ins, outs, sem):
        for send, _ in flights(outs, sem):
            send.start()

    def finish(ins, outs, sem):
        for send, arrival in flights(outs, sem):
            arrival.wait_recv()
            send.wait_send()

    return _Rider(shards, [jax.ShapeDtypeStruct(h.shape, F32) for h in shards], {a: a for a in range(n)}, 2 * n, start, finish)


class _Reduction:
    def __init__(self, names, grads, cj):
        self.names, self.grads, self.cj = names, list(grads), cj

    def swap(self):
        return _swap_rider(self.grads)

    def add(self, got):
        self.got = list(got)
        self.sums = [_add_halves("add_halves_" + nm, g, h, self.cj) for nm, g, h in zip(self.names, self.grads, self.got)]
        return _scatter_rider(self.sums)

    def total(self, landed):
        halves = [_sum_chips("sum_chips_" + nm, g, h, l, self.cj)
                  for nm, g, h, l in zip(self.names, self.grads, self.got, landed)]
        return _join_rider(halves)


def _adamw_math(w, g, m, v):
    m = ADAM_B1 * m + (1.0 - ADAM_B1) * g
    v = ADAM_B2 * v + (1.0 - ADAM_B2) * jnp.square(g)
    m_hat = m / (1.0 - ADAM_B1 ** ADAM_STEP)
    v_hat = v / (1.0 - ADAM_B2 ** ADAM_STEP)
    return -ADAM_LR * (m_hat / (jnp.sqrt(v_hat) + ADAM_EPS) + ADAM_WD * w), m, v


def _adamw(name, w, g, m, v):
    rows, cols = w.shape
    tr = _row_tile(rows)

    def body(w_ref, g_ref, m_ref, v_ref, d_ref, nm_ref, nv_ref):
        d_ref[...], nm_ref[...], nv_ref[...] = _adamw_math(w_ref[...], g_ref[...], m_ref[...], v_ref[...])

    spec = pl.BlockSpec((tr, cols), lambda i: (i, 0))
    return _call(
        body, name=name, grid=(rows // tr,), in_specs=[spec] * 4, out_specs=[spec] * 3,
        out_shape=[jax.ShapeDtypeStruct(w.shape, F32)] * 3, compiler_params=_params(("parallel",)),
    )(w, g, m, v)


def _adamw_small(ws, gs, ms, vs):
    n = len(ws)

    def body(*refs):
        ins, outs = refs[:4 * n], refs[4 * n:]
        for a in range(n):
            w, g, m, v = (ins[k * n + a][...] for k in range(4))
            outs[a][...], outs[n + a][...], outs[2 * n + a][...] = _adamw_math(w, g, m, v)

    vm = pl.BlockSpec(memory_space=pltpu.VMEM)
    res = _call(
        body, name="adamw_small", in_specs=[vm] * (4 * n), out_specs=[vm] * (3 * n),
        out_shape=[jax.ShapeDtypeStruct(w.shape, F32) for w in ws] * 3, compiler_params=_params(),
    )(*ws, *gs, *ms, *vs)
    return res[:n], res[n:2 * n], res[2 * n:]


def kernel(x, c, w_ada, b_ada, norm1_g, w_in, q_norm_a, k_norm_a, sink_b, w_branch, w_out, norm2_g, w_mlp_in, w_mlp_out, final_g, loss_target, m_w_ada, m_b_ada, m_norm1_g, m_w_in, m_q_norm_a, m_k_norm_a, m_sink_b, m_w_branch, m_w_out, m_norm2_g, m_w_mlp_in, m_w_mlp_out, m_final_g, v_w_ada, v_b_ada, v_norm1_g, v_w_in, v_q_norm_a, v_k_norm_a, v_sink_b, v_w_branch, v_w_out, v_norm2_g, v_w_mlp_in, v_w_mlp_out, v_final_g):
    xi, yi, ci = _me()
    cj = jnp.stack([ci, 2 * xi + yi]).astype(jnp.int32)
    n_cols = 6 * D_MODEL // N_SHARD

    mod6, sc_all = _ada_fwd(c, w_ada[0], b_ada.reshape(N_SHARD, n_cols))

    def rows2d(a):
        return a.reshape(-1, a.shape[-1])

    big = (w_in, w_branch, w_out, w_mlp_in, w_mlp_out)
    stacked = _cast_weights([rows2d(w) for w in big])
    w_in_s, = _alone("gather_w_in", _gather_rider(stacked[:1]))
    rest = stacked[1:]

    gq2 = jnp.tile(q_norm_a, (1, 2))
    gk2 = jnp.tile(k_norm_a, (1, 2))
    grad_x, g_big, (acc_out, acc_mlp, acc_in, acc_qk, dsink) = _local_step(
        x[0], loss_target[0], mod6, norm1_g, norm2_g, final_g.reshape(1, D_MODEL), gq2, gk2, sink_b[0], w_in_s, rest, cj)

    loss = lax.psum(0.5 * jnp.sum(acc_out[0]) / D_MODEL, ("x", "y", "c"))
    g_w_ada, g_b_ada, g_n1, g_n2, g_f, g_q, g_k, g_s = _ada_bwd(acc_out, acc_mlp, acc_in, acc_qk, dsink, sc_all)

    names = ("w_ada", "w_in", "w_branch", "w_out", "w_mlp_in", "w_mlp_out")
    big_w = [w_ada[0]] + [rows2d(w) for w in big]
    big_g = [g_w_ada] + list(g_big)
    big_m = [rows2d(m) for m in (m_w_ada, m_w_in, m_w_branch, m_w_out, m_w_mlp_in, m_w_mlp_out)]
    big_v = [rows2d(v) for v in (v_w_ada, v_w_in, v_w_branch, v_w_out, v_w_mlp_in, v_w_mlp_out)]
    big_res = {nm: _adamw("adamw_" + nm, w, g, m, v) for nm, w, g, m, v in zip(names, big_w, big_g, big_m, big_v)}

    small = ("b_ada", "norm1_g", "q_norm_a", "k_norm_a", "sink_b", "norm2_g", "final_g")
    row = lambda a: a.reshape(1, -1)
    small_w = [row(a) for a in (b_ada, norm1_g, q_norm_a, k_norm_a, sink_b, norm2_g, final_g)]
    small_g = [g_b_ada, g_n1, g_q, g_k, g_s, g_n2, g_f]
    small_m = [row(a) for a in (m_b_ada, m_norm1_g, m_q_norm_a, m_k_norm_a, m_sink_b, m_norm2_g, m_final_g)]
    small_v = [row(a) for a in (v_b_ada, v_norm1_g, v_q_norm_a, v_k_norm_a, v_sink_b, v_norm2_g, v_final_g)]
    s_d, s_m, s_v = _adamw_small(small_w, small_g, small_m, small_v)

    order = ("w_ada", "b_ada", "norm1_g", "w_in", "q_norm_a", "k_norm_a", "sink_b", "w_branch", "w_out", "norm2_g",
             "w_mlp_in", "w_mlp_out", "final_g")
    like = dict(w_ada=w_ada, b_ada=b_ada, norm1_g=norm1_g, w_in=w_in, q_norm_a=q_norm_a, k_norm_a=k_norm_a, sink_b=sink_b,
                w_branch=w_branch, w_out=w_out, norm2_g=norm2_g, w_mlp_in=w_mlp_in, w_mlp_out=w_mlp_out, final_g=final_g)
    grad, delta, new_m, new_v = {}, {}, {}, {}
    for nm, g in zip(names, big_g):
        grad[nm] = g
        delta[nm], new_m[nm], new_v[nm] = big_res[nm]
    for k, nm in enumerate(small):
        grad[nm], delta[nm], new_m[nm], new_v[nm] = small_g[k], s_d[k], s_m[k], s_v[k]
    outs = [loss, grad_x[None]]
    for group in (grad, delta, new_m, new_v):
        outs += [group[nm].reshape(like[nm].shape) for nm in order]
    return tuple(outs)
```

```python
import functools

import jax
import jax.numpy as jnp
from jax import lax
from jax.experimental import pallas as pl
from jax.experimental.pallas import tpu as pltpu

F32 = jnp.float32
BF16 = jnp.bfloat16
MESH = pl.DeviceIdType.MESH
ANY = pl.BlockSpec(memory_space=pl.ANY)

D_MODEL = 1024
HEAD_DIM = 64
Q_HEADS = 8
KV_HEADS = 2
GROUP = Q_HEADS // KV_HEADS
BRANCH_W = Q_HEADS * HEAD_DIM
KV_W = KV_HEADS * HEAD_DIM
IN_W = 2 * (BRANCH_W + 2 * KV_W) + 2 * D_MODEL
QK_W = 2 * (BRANCH_W + 2 * KV_W)
D_FF = 4 * D_MODEL
GRID_W = 64
WINDOW = 128
ROPE_THETA = 10000.0
NORM_EPS = 1e-6
NEG_INF = -1e30
Q_SCALE = HEAD_DIM ** -0.5
N_SHARD = 4
N_DEV = 8
LANES = 128
VMEM_LIMIT = 56 * 1024 * 1024

ADAM_LR = 0.001
ADAM_B1 = 0.9
ADAM_B2 = 0.999
ADAM_EPS = 1e-08
ADAM_WD = 0.01
ADAM_STEP = 10

_call = pl.pallas_call


def _params(sem=None, vmem=VMEM_LIMIT):
    return pltpu.CompilerParams(dimension_semantics=sem, vmem_limit_bytes=vmem)


def _nt(a, b):
    return lax.dot_general(a, b, (((1,), (1,)), ((), ())), preferred_element_type=F32)


def _tn(a, b):
    return lax.dot_general(a, b, (((0,), (0,)), ((), ())), preferred_element_type=F32)


def _nn(a, b):
    return jnp.dot(a, b, preferred_element_type=F32)


def _sigmoid(z):
    return 0.5 * jnp.tanh(0.5 * z) + 0.5


def _rope_tables(s):
    t = jnp.arange(s, dtype=jnp.int32)
    lane = jnp.arange(LANES, dtype=jnp.int32)

    def cos_sin(pos, dim):
        inv = ROPE_THETA ** (-jnp.arange(0, dim, 2, dtype=F32) / dim)
        ang = pos.astype(F32)[:, None] * inv[None, :]
        return jnp.cos(ang), jnp.sin(ang)

    cr, sr = cos_sin(t // GRID_W, HEAD_DIM // 2)
    cc, sc = cos_sin(t % GRID_W, HEAD_DIM // 2)
    cos_a = jnp.tile(jnp.concatenate([cr, cr, cc, cc], axis=1), (1, 2))
    sin_a = jnp.tile(jnp.concatenate([sr, sr, sc, sc], axis=1), (1, 2))
    first_a = (lane % 32) < 16
    c1, s1 = cos_sin(t, HEAD_DIM)
    cos_b = jnp.tile(jnp.concatenate([c1, c1], axis=1), (1, 2))
    sin_b = jnp.tile(jnp.concatenate([s1, s1], axis=1), (1, 2))
    first_b = (lane % 64) < 32
    tabs_a = (cos_a, jnp.where(first_a, -sin_a, 0.0), jnp.where(first_a, 0.0, sin_a))
    tabs_b = (cos_b, jnp.where(first_b, -sin_b, 0.0), jnp.where(first_b, 0.0, sin_b))
    return tabs_a + tabs_b


def _rope(z, cos, s_lo, s_hi, half, sign=1.0):
    up = pltpu.roll(z, LANES - half, 1)
    dn = pltpu.roll(z, half, 1)
    return z * cos + sign * (up * s_lo + dn * s_hi)


def _head_mean(z2, bd):
    hi = z2.astype(BF16)
    lo = (z2 - hi.astype(F32)).astype(BF16)
    return _nn(hi, bd) + _nn(lo, bd)


def _block_diag():
    lane = jnp.arange(LANES)
    return jnp.where((lane[:, None] // HEAD_DIM) == (lane[None, :] // HEAD_DIM), 1.0 / HEAD_DIM, 0.0).astype(BF16)


def _row_spec(tm, width):
    return pl.BlockSpec((tm, width), lambda i: (i, 0))


def _heads_spec(heads, tm):
    return pl.BlockSpec((heads, tm, HEAD_DIM), lambda i: (0, i, 0))


def _full_spec(shape):
    nd = len(shape)
    return pl.BlockSpec(shape, lambda i: (0,) * nd)


def _in_proj(x, mod6, g1, w_in_s, gq, gk, bd, tabs, tm=256):
    s = x.shape[0]

    def body(x_ref, mod_ref, g1_ref, w_ref, gq_ref, gk_ref, bd_ref, ca, la, ha, cb, lb, hb,
             h_ref, qkraw_ref, qa_ref, ka_ref, va_ref, qb_ref, kb_ref, vb_ref, gate_ref):
        xt = x_ref[...]
        r = lax.rsqrt(jnp.mean(xt * xt, axis=-1, keepdims=True) + NORM_EPS)
        h = (xt * r * g1_ref[...]) * (1.0 + mod_ref[1:2, :]) + mod_ref[0:1, :]
        hb16 = h.astype(BF16)
        h_ref[...] = hb16
        proj = jnp.concatenate([_nn(hb16, w_ref[j]) for j in range(N_SHARD)], axis=1)
        qkraw_ref[...] = proj[:, :BRANCH_W + KV_W]
        bdm = bd_ref[...]
        tab_a = (ca[...], la[...], ha[...])
        tab_b = (cb[...], lb[...], hb[...])

        def norm_rope_a(z, gain):
            zn = z * lax.rsqrt(_head_mean(z * z, bdm) + NORM_EPS) * gain
            return _rope(zn, *tab_a, 16)

        def put(ref, first, z):
            zb = z.astype(BF16)
            ref[first] = zb[:, :HEAD_DIM]
            ref[first + 1] = zb[:, HEAD_DIM:]

        for i in range(Q_HEADS // 2):
            put(qa_ref, 2 * i, norm_rope_a(proj[:, LANES * i:LANES * (i + 1)], gq_ref[...]) * Q_SCALE)
        off = BRANCH_W
        put(ka_ref, 0, norm_rope_a(proj[:, off:off + LANES], gk_ref[...]))
        off += KV_W
        def put_v(ref, z):
            zb = z.astype(BF16)
            for hd in range(KV_HEADS):
                ref[hd, :, :HEAD_DIM] = zb[:, HEAD_DIM * hd:HEAD_DIM * (hd + 1)]
                ref[hd, :, HEAD_DIM:] = jnp.ones((tm, HEAD_DIM), BF16)

        put_v(va_ref, proj[:, off:off + LANES])
        off += KV_W
        for i in range(Q_HEADS // 2):
            put(qb_ref, 2 * i, _rope(proj[:, off + LANES * i:off + LANES * (i + 1)], *tab_b, 32) * Q_SCALE)
        off += BRANCH_W
        put(kb_ref, 0, _rope(proj[:, off:off + LANES], *tab_b, 32))
        off += KV_W
        put_v(vb_ref, proj[:, off:off + LANES])
        gate_ref[...] = proj[:, QK_W:]

    tab_spec = _row_spec(tm, LANES)
    return _call(
        body, name="in_proj", grid=(s // tm,),
        in_specs=[_row_spec(tm, D_MODEL), _full_spec(mod6.shape), _full_spec(g1.shape), _full_spec(w_in_s.shape),
                  _full_spec(gq.shape), _full_spec(gk.shape), _full_spec(bd.shape)] + [tab_spec] * 6,
        out_specs=[_row_spec(tm, D_MODEL), _row_spec(tm, BRANCH_W + KV_W), _heads_spec(Q_HEADS, tm), _heads_spec(KV_HEADS, tm),
                   pl.BlockSpec((KV_HEADS, tm, LANES), lambda i: (0, i, 0)), _heads_spec(Q_HEADS, tm),
                   _heads_spec(KV_HEADS, tm), pl.BlockSpec((KV_HEADS, tm, LANES), lambda i: (0, i, 0)),
                   _row_spec(tm, 2 * D_MODEL)],
        out_shape=[jax.ShapeDtypeStruct((s, D_MODEL), BF16), jax.ShapeDtypeStruct((s, BRANCH_W + KV_W), F32),
                   jax.ShapeDtypeStruct((Q_HEADS, s, HEAD_DIM), BF16), jax.ShapeDtypeStruct((KV_HEADS, s, HEAD_DIM), BF16),
                   jax.ShapeDtypeStruct((KV_HEADS, s, LANES), BF16), jax.ShapeDtypeStruct((Q_HEADS, s, HEAD_DIM), BF16),
                   jax.ShapeDtypeStruct((KV_HEADS, s, HEAD_DIM), BF16), jax.ShapeDtypeStruct((KV_HEADS, s, LANES), BF16),
                   jax.ShapeDtypeStruct((s, 2 * D_MODEL), F32)],
        compiler_params=_params(("parallel",)),
    )(x, mod6, g1, w_in_s, gq, gk, bd, *tabs)


def _group_specs(s, tq):
    q_spec = pl.BlockSpec((None, GROUP, tq, HEAD_DIM), lambda g, i: (g, 0, i, 0))
    kv_spec = pl.BlockSpec((None, s, HEAD_DIM), lambda g, i: (g, 0, 0))
    col_spec = pl.BlockSpec((None, GROUP, tq, 1), lambda g, i: (g, 0, i, 0))
    return q_spec, kv_spec, col_spec


def _attn_a_fwd(q, k, v1, rider=None, tq=256, tk=2048):
    s = q.shape[1]
    tk = min(tk, s // 2)
    rows = GROUP * tq

    n = s // tk
    assert n >= 2 and n % 2 == 0

    def body(q_ref, k_ref, v_ref, o_ref, oh_ref, lse_ref, s0_ref, s1_ref, p0_ref, p1_ref, m_ref, a_ref, acc_ref):
        s_ref, p_ref = (s0_ref, s1_ref), (p0_ref, p1_ref)
        qq = q_ref[...].reshape(rows, HEAD_DIM)
        m_ref[...] = jnp.full((rows, 1), NEG_INF, F32)
        acc_ref[...] = jnp.zeros((rows, LANES), F32)

        def keys(i):
            return pl.ds(pl.multiple_of(i * tk, tk), tk)

        def scores(i, slot):
            s_ref[slot][...] = _nt(qq, k_ref[keys(i), :])

        def softmax(slot):
            sc = s_ref[slot][...]
            m = m_ref[...]
            mn = jnp.maximum(m, jnp.max(sc, axis=-1, keepdims=True))
            m_ref[...] = mn
            a_ref[...] = jnp.exp(m - mn)
            p_ref[slot][...] = jnp.exp(sc - mn).astype(BF16)

        def weigh(i, slot):
            acc_ref[...] = a_ref[...] * acc_ref[...] + _nn(p_ref[slot][...], v_ref[keys(i), :])

        scores(0, 0)
        softmax(0)
        scores(1, 1)

        def two_steps(j, carry):
            i = 2 * j + 1
            weigh(i - 1, 0)
            softmax(1)
            scores(i + 1, 0)
            weigh(i, 1)
            softmax(0)
            scores(i + 2, 1)
            return carry

        lax.fori_loop(0, (n - 2) // 2, two_steps, 0, unroll=True)
        weigh(n - 2, 0)
        softmax(1)
        weigh(n - 1, 1)
        l = acc_ref[:, HEAD_DIM:HEAD_DIM + 1]
        o = (acc_ref[:, :HEAD_DIM] / l).astype(BF16)
        for g in range(GROUP):
            o_ref[:, HEAD_DIM * g:HEAD_DIM * (g + 1)] = o[tq * g:tq * (g + 1)]
        oh_ref[...] = o.reshape(GROUP, tq, HEAD_DIM)
        lse_ref[...] = (m_ref[...] + jnp.log(l)).reshape(GROUP, tq, 1)

    q_spec, kv_spec, col_spec = _group_specs(s, tq)
    v_spec = pl.BlockSpec((None, s, LANES), lambda g, i: (g, 0, 0))
    return _hosted(
        body, rider, name="attn_a_fwd", grid=(KV_HEADS, s // tq),
        in_specs=[q_spec, kv_spec, v_spec],
        out_specs=[pl.BlockSpec((tq, GROUP * HEAD_DIM), lambda g, i: (i, g)), q_spec, col_spec],
        out_shape=[jax.ShapeDtypeStruct((s, BRANCH_W), BF16), jax.ShapeDtypeStruct((KV_HEADS, GROUP, s, HEAD_DIM), BF16),
                   jax.ShapeDtypeStruct((KV_HEADS, GROUP, s, 1), F32)],
        scratch_shapes=[pltpu.VMEM((rows, tk), F32), pltpu.VMEM((rows, tk), F32), pltpu.VMEM((rows, tk), BF16),
                        pltpu.VMEM((rows, tk), BF16), pltpu.VMEM((rows, 1), F32), pltpu.VMEM((rows, 1), F32),
                        pltpu.VMEM((rows, LANES), F32)],
        args=(q.reshape(KV_HEADS, GROUP, s, HEAD_DIM), k, v1), middle_at=KV_HEADS * (s // tq) // 2)


def _attn_a_bwd(q, k, v1, o, do, lse, rider=None, tq=256, tk=512):
    v = v1
    s = q.shape[1]
    tk = min(tk, s // 2)
    rows = GROUP * tq

    n = s // tk
    assert n >= 2 and n % 2 == 0

    def body(q_ref, k_ref, v_ref, o_ref, do_ref, lse_ref, dq_ref, dk_ref, dv_ref,
             s0_ref, s1_ref, dp0_ref, dp1_ref, p0_ref, p1_ref, ds0_ref, ds1_ref, dq_acc):
        s_ref, dp_ref, p_ref, ds_ref = (s0_ref, s1_ref), (dp0_ref, dp1_ref), (p0_ref, p1_ref), (ds0_ref, ds1_ref)

        @pl.when(pl.program_id(1) == 0)
        def _():
            dk_ref[...] = jnp.zeros_like(dk_ref)
            dv_ref[...] = jnp.zeros_like(dv_ref)

        qq = q_ref[...].reshape(rows, HEAD_DIM)
        dd = do_ref[...].reshape(rows, HEAD_DIM)
        ls = lse_ref[...].reshape(rows, 1)
        dl = jnp.sum(dd.astype(F32) * o_ref[...].reshape(rows, HEAD_DIM).astype(F32), axis=-1, keepdims=True)
        dq_acc[...] = jnp.zeros((rows, HEAD_DIM), F32)

        def keys(i):
            return pl.ds(pl.multiple_of(i * tk, tk), tk)

        def scores(i, slot):
            s_ref[slot][...] = _nt(qq, k_ref[keys(i), :])
            dp_ref[slot][...] = _nt(dd, v_ref[keys(i), :HEAD_DIM])

        def weights(slot):
            p = jnp.exp(s_ref[slot][...] - ls)
            p_ref[slot][...] = p.astype(BF16)
            ds_ref[slot][...] = (p * (dp_ref[slot][...] - dl)).astype(BF16)

        def grads(i, slot):
            dv_ref[keys(i), :] += _tn(p_ref[slot][...], dd)
            dk_ref[keys(i), :] += _tn(ds_ref[slot][...], qq)
            dq_acc[...] += _nn(ds_ref[slot][...], k_ref[keys(i), :])

        scores(0, 0)
        weights(0)
        scores(1, 1)

        def two_steps(j, carry):
            i = 2 * j + 1
            grads(i - 1, 0)
            weights(1)
            scores(i + 1, 0)
            grads(i, 1)
            weights(0)
            scores(i + 2, 1)
            return carry

        lax.fori_loop(0, (n - 2) // 2, two_steps, 0, unroll=True)
        grads(n - 2, 0)
        weights(1)
        grads(n - 1, 1)
        dq_ref[...] = dq_acc[...].reshape(GROUP, tq, HEAD_DIM)

    q_spec, kv_spec, col_spec = _group_specs(s, tq)
    v_spec = pl.BlockSpec((None, s, LANES), lambda g, i: (g, 0, 0))
    shape4 = (KV_HEADS, GROUP, s, HEAD_DIM)
    tile32, tile16 = pltpu.VMEM((rows, tk), F32), pltpu.VMEM((rows, tk), BF16)
    return _hosted(
        body, rider, name="attn_a_bwd", grid=(KV_HEADS, s // tq),
        in_specs=[q_spec, kv_spec, v_spec, q_spec, q_spec, col_spec],
        out_specs=[q_spec, kv_spec, kv_spec],
        out_shape=[jax.ShapeDtypeStruct(shape4, F32), jax.ShapeDtypeStruct((KV_HEADS, s, HEAD_DIM), F32),
                   jax.ShapeDtypeStruct((KV_HEADS, s, HEAD_DIM), F32)],
        scratch_shapes=[tile32] * 4 + [tile16] * 4 + [pltpu.VMEM((rows, HEAD_DIM), F32)],
        args=(q.reshape(shape4), k, v, o.reshape(shape4), do.reshape(shape4), lse))


TQ_B = WINDOW


def _win_keys(tq):
    return tq + 2 * WINDOW


def _window_bias(tq):
    r = jnp.arange(tq, dtype=jnp.int32)[:, None]
    col = jnp.arange(_win_keys(tq), dtype=jnp.int32)[None, :]
    return jnp.stack([jnp.where(jnp.abs(r - col + WINDOW * b) <= WINDOW, 0.0, NEG_INF) for b in range(3)]).astype(F32)


def _band(tq, s):
    win = _win_keys(tq)

    def window(e):
        return pl.ds(pl.multiple_of(jnp.clip(e * tq - WINDOW, 0, s - win), WINDOW), win)

    def bias_index(e):
        return jnp.where(e == 0, 0, jnp.where(e >= s // tq - 1, 2, 1))

    return window, bias_index


def _pair_specs(s, tq):
    pairs = s // (2 * tq)
    cur = lambda g, j: (g, 0, jnp.minimum(j, pairs - 1), 0)
    prev = lambda g, j: (g, 0, jnp.maximum(j - 1, 0), 0)
    tile = lambda width, index: pl.BlockSpec((None, GROUP, 2 * tq, width), index)
    kv_spec = pl.BlockSpec((None, s, HEAD_DIM), lambda g, j: (g, 0, 0))
    v_spec = pl.BlockSpec((None, s, LANES), lambda g, j: (g, 0, 0))
    sink_spec = pl.BlockSpec((None, GROUP * tq, 1), lambda g, j: (g, 0, 0))
    bias_spec = pl.BlockSpec((3, tq, _win_keys(tq)), lambda g, j: (0, 0, 0))
    return tile, cur, prev, kv_spec, v_spec, sink_spec, bias_spec


def _attn_b_fwd(q, k, v1, sink_col, bias, rider=None, tq=TQ_B):
    s = q.shape[1]
    rows = GROUP * tq
    win = _win_keys(tq)
    pairs = s // (2 * tq)
    window, bias_index = _band(tq, s)

    def body(q_ref, k_ref, v_ref, sink_ref, bias_ref, o_ref, oh_ref, lse_ref, s0_ref, s1_ref, p0_ref, p1_ref, m0_ref, m1_ref):
        s_ref, p_ref, m_ref = (s0_ref, s1_ref), (p0_ref, p1_ref), (m0_ref, m1_ref)
        j = pl.program_id(1)

        @pl.when(j == 0)
        def _():
            for ref in (s0_ref, s1_ref, p0_ref, p1_ref, m0_ref, m1_ref):
                ref[...] = jnp.zeros_like(ref)

        def scores(e, slot):
            qq = q_ref[:, pl.ds(slot * tq, tq), :].reshape(rows, HEAD_DIM)
            sc = _nt(qq, k_ref[window(e), :]).reshape(GROUP, tq, win) + bias_ref[bias_index(e)][None]
            s_ref[slot][...] = sc.reshape(rows, win)

        def softmax(slot):
            sc = s_ref[slot][...]
            m = jnp.maximum(jnp.max(sc, axis=-1, keepdims=True), sink_ref[...])
            m_ref[slot][...] = m
            p_ref[slot][...] = jnp.exp(sc - m).astype(BF16)

        def finish(e, slot):
            acc = _nn(p_ref[slot][...], v_ref[window(e), :])
            m = m_ref[slot][...]
            l = acc[:, HEAD_DIM:HEAD_DIM + 1] + jnp.exp(sink_ref[...] - m)
            o = (acc[:, :HEAD_DIM] / l).astype(BF16)
            at = pl.ds(slot * tq, tq)
            for g in range(GROUP):
                o_ref[at, HEAD_DIM * g:HEAD_DIM * (g + 1)] = o[tq * g:tq * (g + 1)]
            oh_ref[:, at, :] = o.reshape(GROUP, tq, HEAD_DIM)
            lse_ref[:, at, :] = (m + jnp.log(l)).reshape(GROUP, tq, 1)

        first = 2 * j
        finish(jnp.maximum(first - 2, 0), 0)
        softmax(1)
        scores(first, 0)
        finish(jnp.maximum(first - 1, 0), 1)
        softmax(0)
        scores(first + 1, 1)

    tile, cur, prev, kv_spec, v_spec, sink_spec, bias_spec = _pair_specs(s, tq)
    tile32, tile16, col = pltpu.VMEM((rows, win), F32), pltpu.VMEM((rows, win), BF16), pltpu.VMEM((rows, 1), F32)
    return _hosted(
        body, rider, name="attn_b_fwd", grid=(KV_HEADS, pairs + 1),
        in_specs=[tile(HEAD_DIM, cur), kv_spec, v_spec, sink_spec, bias_spec],
        out_specs=[pl.BlockSpec((2 * tq, GROUP * HEAD_DIM), lambda g, j: (jnp.maximum(j - 1, 0), g)),
                   tile(HEAD_DIM, prev), tile(1, prev)],
        out_shape=[jax.ShapeDtypeStruct((s, BRANCH_W), BF16), jax.ShapeDtypeStruct((KV_HEADS, GROUP, s, HEAD_DIM), BF16),
                   jax.ShapeDtypeStruct((KV_HEADS, GROUP, s, 1), F32)],
        scratch_shapes=[tile32, tile32, tile16, tile16, col, col],
        args=(q.reshape(KV_HEADS, GROUP, s, HEAD_DIM), k, v1, sink_col, bias))


def _attn_b_bwd(q, k, v1, o, do, lse, sink_col, bias, rider=None, tq=TQ_B):
    s = q.shape[1]
    rows = GROUP * tq
    win = _win_keys(tq)
    pairs = s // (2 * tq)
    window, bias_index = _band(tq, s)

    def body(q_ref, k_ref, v_ref, o_ref, do_ref, lse_ref, sink_ref, bias_ref, dq_ref, dk_ref, dv_ref, dsink_ref,
             s0, s1, dp0, dp1, p0, p1, ds0, ds1, q0, q1, d0, d1, ls0, ls1, dl0, dl1):
        s_ref, dp_ref, p_ref, ds_ref = (s0, s1), (dp0, dp1), (p0, p1), (ds0, ds1)
        q_keep, do_keep, lse_keep, delta_keep = (q0, q1), (d0, d1), (ls0, ls1), (dl0, dl1)
        j = pl.program_id(1)

        @pl.when(j == 0)
        def _():
            for ref in (dk_ref, dv_ref, dsink_ref, s0, s1, dp0, dp1, p0, p1, ds0, ds1, q0, q1, d0, d1, ls0, ls1, dl0, dl1):
                ref[...] = jnp.zeros_like(ref)

        def scores(e, slot):
            at = pl.ds(slot * tq, tq)
            qq = q_ref[:, at, :].reshape(rows, HEAD_DIM)
            dd = do_ref[:, at, :].reshape(rows, HEAD_DIM)
            q_keep[slot][...] = qq
            do_keep[slot][...] = dd
            lse_keep[slot][...] = lse_ref[:, at, :].reshape(rows, 1)
            delta_keep[slot][...] = jnp.sum(dd.astype(F32) * o_ref[:, at, :].reshape(rows, HEAD_DIM).astype(F32), axis=-1,
                                            keepdims=True)
            sc = _nt(qq, k_ref[window(e), :]).reshape(GROUP, tq, win) + bias_ref[bias_index(e)][None]
            s_ref[slot][...] = sc.reshape(rows, win)
            dp_ref[slot][...] = _nt(dd, v_ref[window(e), :HEAD_DIM])

        def weights(slot):
            p = jnp.exp(s_ref[slot][...] - lse_keep[slot][...])
            p_ref[slot][...] = p.astype(BF16)
            ds_ref[slot][...] = (p * (dp_ref[slot][...] - delta_keep[slot][...])).astype(BF16)

        def grads(e, slot, live):
            at = window(e)
            ds = ds_ref[slot][...]
            dv_ref[at, :] += _tn(p_ref[slot][...], do_keep[slot][...])
            dk_ref[at, :] += _tn(ds, q_keep[slot][...])
            dq_ref[:, pl.ds(slot * tq, tq), :] = _nn(ds, k_ref[at, :]).reshape(GROUP, tq, HEAD_DIM)
            dsk = jnp.exp(sink_ref[...] - lse_keep[slot][...]) * delta_keep[slot][...] * live
            for g in range(GROUP):
                dsink_ref[g:g + 1, :] -= jnp.broadcast_to(jnp.sum(dsk[tq * g:tq * (g + 1)], axis=0, keepdims=True), (1, LANES))

        first = 2 * j
        live = jnp.where(j > 0, 1.0, 0.0)
        grads(jnp.maximum(first - 2, 0), 0, live)
        weights(1)
        scores(first, 0)
        grads(jnp.maximum(first - 1, 0), 1, live)
        weights(0)
        scores(first + 1, 1)

    tile, cur, prev, kv_spec, v_spec, sink_spec, bias_spec = _pair_specs(s, tq)
    dsink_spec = pl.BlockSpec((None, ACC_ROWS, LANES), lambda g, j: (g, 0, 0))
    shape4 = (KV_HEADS, GROUP, s, HEAD_DIM)
    tile32, tile16 = pltpu.VMEM((rows, win), F32), pltpu.VMEM((rows, win), BF16)
    keep, col = pltpu.VMEM((rows, HEAD_DIM), BF16), pltpu.VMEM((rows, 1), F32)
    return _hosted(
        body, rider, name="attn_b_bwd", grid=(KV_HEADS, pairs + 1),
        in_specs=[tile(HEAD_DIM, cur), kv_spec, v_spec, tile(HEAD_DIM, cur), tile(HEAD_DIM, cur), tile(1, cur), sink_spec,
                  bias_spec],
        out_specs=[tile(HEAD_DIM, prev), kv_spec, kv_spec, dsink_spec],
        out_shape=[jax.ShapeDtypeStruct(shape4, F32), jax.ShapeDtypeStruct((KV_HEADS, s, HEAD_DIM), F32),
                   jax.ShapeDtypeStruct((KV_HEADS, s, HEAD_DIM), F32), jax.ShapeDtypeStruct((KV_HEADS, ACC_ROWS, LANES), F32)],
        scratch_shapes=[tile32] * 4 + [tile16] * 4 + [keep] * 4 + [col] * 4,
        args=(q.reshape(shape4), k, v1, o.reshape(shape4), do.reshape(shape4), lse, sink_col, bias))


def _post_attn(ya, yb, gates, x, mod6, wbr_s, w_out, tm=256):
    s = x.shape[0]

    def body(ya_ref, yb_ref, g_ref, x_ref, mod_ref, wbr_ref, wo_ref, ua_ref, ub_ref, mg_ref, o_ref, x1_ref):
        ya_t, yb_t = ya_ref[...], yb_ref[...]
        ua = jnp.concatenate([_nn(ya_t, wbr_ref[j, 0]) for j in range(N_SHARD)], axis=1)
        ub = jnp.concatenate([_nn(yb_t, wbr_ref[j, 1]) for j in range(N_SHARD)], axis=1)
        merged = (_sigmoid(g_ref[:, :D_MODEL]) * ua + _sigmoid(g_ref[:, D_MODEL:]) * ub).astype(BF16)
        o = _nn(merged, wo_ref[...])
        ua_ref[...] = ua.astype(BF16)
        ub_ref[...] = ub.astype(BF16)
        mg_ref[...] = merged
        o_ref[...] = o.astype(BF16)
        x1_ref[...] = x_ref[...] + mod_ref[2:3, :] * o

    bf = jax.ShapeDtypeStruct((s, D_MODEL), BF16)
    return _call(
        body, name="post_attn", grid=(s // tm,),
        in_specs=[_row_spec(tm, BRANCH_W), _row_spec(tm, BRANCH_W), _row_spec(tm, 2 * D_MODEL), _row_spec(tm, D_MODEL),
                  _full_spec(mod6.shape), _full_spec(wbr_s.shape), _full_spec(w_out.shape)],
        out_specs=[_row_spec(tm, D_MODEL)] * 5,
        out_shape=[bf, bf, bf, bf, jax.ShapeDtypeStruct((s, D_MODEL), F32)],
        compiler_params=_params(("parallel",)),
    )(ya, yb, gates, x, mod6, wbr_s, w_out)


def _mlp_in(x1, mod6, g2, w_mi_s, tm=256):
    s = x1.shape[0]

    def body(x_ref, mod_ref, g_ref, w_ref, h2_ref, a_ref, hid_ref):
        xt = x_ref[...]
        r = lax.rsqrt(jnp.mean(xt * xt, axis=-1, keepdims=True) + NORM_EPS)
        h2 = ((xt * r * g_ref[...]) * (1.0 + mod_ref[4:5, :]) + mod_ref[3:4, :]).astype(BF16)
        h2_ref[...] = h2
        a = jnp.concatenate([_nn(h2, w_ref[j]) for j in range(N_SHARD)], axis=1)
        a_ref[...] = a.astype(BF16)
        hid_ref[...] = jnp.square(jnp.maximum(a, 0.0)).astype(BF16)

    return _call(
        body, name="mlp_in", grid=(s // tm,),
        in_specs=[_row_spec(tm, D_MODEL), _full_spec(mod6.shape), _full_spec(g2.shape), _full_spec(w_mi_s.shape)],
        out_specs=[_row_spec(tm, D_MODEL), _row_spec(tm, D_FF), _row_spec(tm, D_FF)],
        out_shape=[jax.ShapeDtypeStruct((s, D_MODEL), BF16), jax.ShapeDtypeStruct((s, D_FF), BF16),
                   jax.ShapeDtypeStruct((s, D_FF), BF16)],
        compiler_params=_params(("parallel",)),
    )(x1, mod6, g2, w_mi_s)


ACC_ROWS = 8


def _acc_spec():
    return pl.BlockSpec((ACC_ROWS, D_MODEL), lambda i: (0, 0))


def _acc_add(acc_ref, rows):
    @pl.when(pl.program_id(0) == 0)
    def _():
        acc_ref[...] = jnp.zeros_like(acc_ref)

    for r, val in enumerate(rows):
        acc_ref[r:r + 1, :] += jnp.sum(val, axis=0, keepdims=True)


def _mlp_out_loss(hid, x1, a, target, mod6, gf, w_mo, tm=256):
    s = x1.shape[0]

    def body(hid_ref, x_ref, a_ref, t_ref, mod_ref, gf_ref, w_ref, dx2_ref, dm_ref, da_ref, acc_ref):
        m = _nn(hid_ref[...], w_ref[...])
        gate2 = mod_ref[5:6, :]
        x2 = x_ref[...] + gate2 * m
        r = lax.rsqrt(jnp.mean(x2 * x2, axis=-1, keepdims=True) + NORM_EPS)
        xn = x2 * r
        err = xn * gf_ref[...] - t_ref[...]
        dy = err * (1.0 / D_MODEL)
        dxn = dy * gf_ref[...]
        dx2 = r * (dxn - xn * jnp.mean(dxn * xn, axis=-1, keepdims=True))
        dx2_ref[...] = dx2
        dm = (dx2 * gate2).astype(BF16)
        dm_ref[...] = dm
        da_ref[...] = (_nt(dm, w_ref[...]) * (2.0 * jnp.maximum(a_ref[...].astype(F32), 0.0))).astype(BF16)
        _acc_add(acc_ref, [err * err, dy * xn, dx2 * m])

    return _call(
        body, name="mlp_out_loss", grid=(s // tm,),
        in_specs=[_row_spec(tm, D_FF), _row_spec(tm, D_MODEL), _row_spec(tm, D_FF), _row_spec(tm, D_MODEL),
                  _full_spec(mod6.shape), _full_spec(gf.shape), _full_spec(w_mo.shape)],
        out_specs=[_row_spec(tm, D_MODEL), _row_spec(tm, D_MODEL), _row_spec(tm, D_FF), _acc_spec()],
        out_shape=[jax.ShapeDtypeStruct((s, D_MODEL), F32), jax.ShapeDtypeStruct((s, D_MODEL), BF16),
                   jax.ShapeDtypeStruct((s, D_FF), BF16), jax.ShapeDtypeStruct((ACC_ROWS, D_MODEL), F32)],
        compiler_params=_params(("arbitrary",)),
    )(hid, x1, a, target, mod6, gf, w_mo)


def _norm_bwd(dh, xt, gain, scale):
    r = lax.rsqrt(jnp.mean(xt * xt, axis=-1, keepdims=True) + NORM_EPS)
    xn = xt * r
    dxn = dh * (gain * (1.0 + scale))
    dx = r * (dxn - xn * jnp.mean(dxn * xn, axis=-1, keepdims=True))
    return dx, [dh, dh * xn * gain, dh * xn * (1.0 + scale)]


def _mlp_bwd(da, x1, dx2, o, mod6, g2, w_mi_s, rider=None, tm=256):
    s = x1.shape[0]

    def body(da_ref, x_ref, dx2_ref, o_ref, mod_ref, g_ref, w_ref, dx1_ref, do_ref, acc_ref):
        dh2 = _nt(da_ref[:, :D_MODEL], w_ref[0])
        for j in range(1, N_SHARD):
            dh2 += _nt(da_ref[:, D_MODEL * j:D_MODEL * (j + 1)], w_ref[j])
        dx, sums = _norm_bwd(dh2, x_ref[...], g_ref[...], mod_ref[4:5, :])
        dx1 = dx2_ref[...] + dx
        dx1_ref[...] = dx1
        do_ref[...] = (dx1 * mod_ref[2:3, :]).astype(BF16)
        _acc_add(acc_ref, sums + [dx1 * o_ref[...].astype(F32)])

    return _hosted(
        body, rider, name="mlp_bwd", grid=(s // tm,),
        in_specs=[_row_spec(tm, D_FF), _row_spec(tm, D_MODEL), _row_spec(tm, D_MODEL), _row_spec(tm, D_MODEL),
                  _full_spec(mod6.shape), _full_spec(g2.shape), _full_spec(w_mi_s.shape)],
        out_specs=[_row_spec(tm, D_MODEL), _row_spec(tm, D_MODEL), _acc_spec()],
        out_shape=[jax.ShapeDtypeStruct((s, D_MODEL), F32), jax.ShapeDtypeStruct((s, D_MODEL), BF16),
                   jax.ShapeDtypeStruct((ACC_ROWS, D_MODEL), F32)],
        args=(da, x1, dx2, o, mod6, g2, w_mi_s))


def _merge_bwd(do, gates, ua, ub, w_out, wbr_s, rider=None, tm=256):
    s = do.shape[0]

    def body(do_ref, g_ref, ua_ref, ub_ref, wo_ref, wbr_ref, dua_ref, dub_ref, dg_ref, doa_ref, dob_ref):
        dmerged = _nt(do_ref[...], wo_ref[...])
        for b, (u_ref, du_ref, dy_ref) in enumerate(((ua_ref, dua_ref, doa_ref), (ub_ref, dub_ref, dob_ref))):
            sg = _sigmoid(g_ref[:, D_MODEL * b:D_MODEL * (b + 1)])
            du = (dmerged * sg).astype(BF16)
            du_ref[...] = du
            dg_ref[:, D_MODEL * b:D_MODEL * (b + 1)] = (dmerged * u_ref[...].astype(F32) * sg * (1.0 - sg)).astype(BF16)
            w = BRANCH_W // 2
            dy = _nt(du[:, :w], wbr_ref[0, b])
            for j in range(1, N_SHARD):
                dy += _nt(du[:, w * j:w * (j + 1)], wbr_ref[j, b])
            dyb = dy.astype(BF16)
            for h in range(Q_HEADS):
                dy_ref[h] = dyb[:, HEAD_DIM * h:HEAD_DIM * (h + 1)]

    bf = jax.ShapeDtypeStruct((s, D_MODEL), BF16)
    heads = jax.ShapeDtypeStruct((Q_HEADS, s, HEAD_DIM), BF16)
    return _hosted(
        body, rider, name="merge_bwd", grid=(s // tm,),
        in_specs=[_row_spec(tm, D_MODEL), _row_spec(tm, 2 * D_MODEL), _row_spec(tm, D_MODEL), _row_spec(tm, D_MODEL),
                  _full_spec(w_out.shape), _full_spec(wbr_s.shape)],
        out_specs=[_row_spec(tm, D_MODEL), _row_spec(tm, D_MODEL), _row_spec(tm, 2 * D_MODEL),
                   _heads_spec(Q_HEADS, tm), _heads_spec(Q_HEADS, tm)],
        out_shape=[bf, bf, jax.ShapeDtypeStruct((s, 2 * D_MODEL), BF16), heads, heads],
        args=(do, gates, ua, ub, w_out, wbr_s))


def _qk_bwd(dqa, dka, dva, dqb, dkb, dvb, qkraw, dgates, gq, gk, bd, tabs, rider=None, tm=256):
    s = qkraw.shape[0]

    def body(dqa_ref, dka_ref, dva_ref, dqb_ref, dkb_ref, dvb_ref, raw_ref, dg_ref, gq_ref, gk_ref, bd_ref,
             ca, la, ha, cb, lb, hb, dp_ref, acc_ref, pair_ref):
        bdm = bd_ref[...]
        tab_a = (ca[...], la[...], ha[...])
        tab_b = (cb[...], lb[...], hb[...])

        def pair(ref, first):
            pair_ref[:, :HEAD_DIM] = ref[first]
            pair_ref[:, HEAD_DIM:] = ref[first + 1]
            return pair_ref[...]

        def norm_rope_a_bwd(dz, raw, gain):
            dzn = _rope(dz, *tab_a, 16, sign=-1.0)
            rinv = lax.rsqrt(_head_mean(raw * raw, bdm) + NORM_EPS)
            zhat = raw * rinv
            dzhat = dzn * gain
            return rinv * (dzhat - zhat * _head_mean(dzhat * zhat, bdm)), dzn * zhat

        gq_rows = jnp.zeros((tm, LANES), F32)
        for i in range(Q_HEADS // 2):
            at = slice(LANES * i, LANES * (i + 1))
            draw, gsum = norm_rope_a_bwd(pair(dqa_ref, 2 * i) * Q_SCALE, raw_ref[:, at], gq_ref[...])
            dp_ref[:, at] = draw.astype(BF16)
            gq_rows += gsum
        off = BRANCH_W
        draw, gk_rows = norm_rope_a_bwd(pair(dka_ref, 0), raw_ref[:, off:off + LANES], gk_ref[...])
        dp_ref[:, off:off + LANES] = draw.astype(BF16)
        off += KV_W
        dp_ref[:, off:off + LANES] = pair(dva_ref, 0).astype(BF16)
        off += KV_W
        for i in range(Q_HEADS // 2):
            dz = _rope(pair(dqb_ref, 2 * i) * Q_SCALE, *tab_b, 32, sign=-1.0)
            dp_ref[:, off + LANES * i:off + LANES * (i + 1)] = dz.astype(BF16)
        off += BRANCH_W
        dp_ref[:, off:off + LANES] = _rope(pair(dkb_ref, 0), *tab_b, 32, sign=-1.0).astype(BF16)
        off += KV_W
        dp_ref[:, off:off + LANES] = pair(dvb_ref, 0).astype(BF16)
        dp_ref[:, QK_W:] = dg_ref[...]

        @pl.when(pl.program_id(0) == 0)
        def _():
            acc_ref[...] = jnp.zeros_like(acc_ref)

        acc_ref[0:1, :] += jnp.sum(gq_rows, axis=0, keepdims=True)
        acc_ref[1:2, :] += jnp.sum(gk_rows, axis=0, keepdims=True)

    tab_spec = _row_spec(tm, LANES)
    return _hosted(
        body, rider, name="qk_bwd", grid=(s // tm,),
        in_specs=[_heads_spec(Q_HEADS, tm), _heads_spec(KV_HEADS, tm), _heads_spec(KV_HEADS, tm),
                  _heads_spec(Q_HEADS, tm), _heads_spec(KV_HEADS, tm), _heads_spec(KV_HEADS, tm),
                  _row_spec(tm, BRANCH_W + KV_W), _row_spec(tm, 2 * D_MODEL),
                  _full_spec(gq.shape), _full_spec(gk.shape), _full_spec(bd.shape)] + [tab_spec] * 6,
        out_specs=[_row_spec(tm, IN_W), pl.BlockSpec((ACC_ROWS, LANES), lambda i: (0, 0))],
        out_shape=[jax.ShapeDtypeStruct((s, IN_W), BF16), jax.ShapeDtypeStruct((ACC_ROWS, LANES), F32)],
        scratch_shapes=[pltpu.VMEM((tm, LANES), F32)],
        args=(dqa, dka, dva, dqb, dkb, dvb, qkraw, dgates, gq, gk, bd, *tabs))


def _in_proj_bwd(dproj, x, dx1, mod6, g1, w_in_s, tm=256):
    s = x.shape[0]
    w = IN_W // N_SHARD

    def body(dp_ref, x_ref, dx1_ref, mod_ref, g_ref, w_ref, gx_ref, acc_ref):
        dh = _nt(dp_ref[:, :w], w_ref[0])
        for j in range(1, N_SHARD):
            dh += _nt(dp_ref[:, w * j:w * (j + 1)], w_ref[j])
        dx, sums = _norm_bwd(dh, x_ref[...], g_ref[...], mod_ref[1:2, :])
        gx_ref[...] = dx1_ref[...] + dx
        _acc_add(acc_ref, sums)

    return _call(
        body, name="in_proj_bwd", grid=(s // tm,),
        in_specs=[_row_spec(tm, IN_W), _row_spec(tm, D_MODEL), _row_spec(tm, D_MODEL),
                  _full_spec(mod6.shape), _full_spec(g1.shape), _full_spec(w_in_s.shape)],
        out_specs=[_row_spec(tm, D_MODEL), _acc_spec()],
        out_shape=[jax.ShapeDtypeStruct((s, D_MODEL), F32), jax.ShapeDtypeStruct((ACC_ROWS, D_MODEL), F32)],
        compiler_params=_params(("arbitrary",)),
    )(dproj, x, dx1, mod6, g1, w_in_s)


def _wgrad(name, a, b, out_shape, out_spec, tm, tn, tk=4096):
    s, m = a.shape
    n = b.shape[1]
    tk = min(tk, s)
    nk = s // tk

    def body(a_ref, b_ref, o_ref, acc_ref):
        k = pl.program_id(2)

        @pl.when(k == 0)
        def _():
            acc_ref[...] = jnp.zeros_like(acc_ref)

        acc_ref[...] += _tn(a_ref[...], b_ref[...])

        @pl.when(k == nk - 1)
        def _():
            o_ref[...] = acc_ref[...].reshape(o_ref.shape)

    return _call(
        body, name=name, grid=(m // tm, n // tn, nk),
        in_specs=[pl.BlockSpec((tk, tm), lambda i, j, k: (k, i)), pl.BlockSpec((tk, tn), lambda i, j, k: (k, j))],
        out_specs=out_spec, out_shape=jax.ShapeDtypeStruct(out_shape, F32),
        scratch_shapes=[pltpu.VMEM((tm, tn), F32)],
        compiler_params=_params(("parallel", "parallel", "arbitrary")),
    )(a, b)


def _wgrad_branch(ya, yb, dua, dub, tk=2048):
    s = ya.shape[0]
    tk = min(tk, s)
    nk = s // tk
    w = D_MODEL // N_SHARD

    def body(ya_ref, yb_ref, dua_ref, dub_ref, o_ref, acc_ref):
        b, k = pl.program_id(0), pl.program_id(1)

        @pl.when(k == 0)
        def _():
            acc_ref[...] = jnp.zeros_like(acc_ref)

        @pl.when(b == 0)
        def _():
            acc_ref[...] += _tn(ya_ref[...], dua_ref[...])

        @pl.when(b == 1)
        def _():
            acc_ref[...] += _tn(yb_ref[...], dub_ref[...])

        @pl.when(k == nk - 1)
        def _():
            for j in range(N_SHARD):
                o_ref[j] = acc_ref[:, w * j:w * (j + 1)]

    first = lambda width: pl.BlockSpec((tk, width), lambda b, k: (k * (1 - b), 0))
    second = lambda width: pl.BlockSpec((tk, width), lambda b, k: (k * b, 0))
    return _call(
        body, name="wgrad_branch", grid=(2, nk),
        in_specs=[first(BRANCH_W), second(BRANCH_W), first(D_MODEL), second(D_MODEL)],
        out_specs=pl.BlockSpec((N_SHARD, None, BRANCH_W, w), lambda b, k: (0, b, 0, 0)),
        out_shape=jax.ShapeDtypeStruct((N_SHARD, 2, BRANCH_W, w), F32),
        scratch_shapes=[pltpu.VMEM((BRANCH_W, D_MODEL), F32)],
        compiler_params=_params(("parallel", "arbitrary")),
    )(ya, yb, dua, dub)


def _local_step(x, target, mod6, g1, g2, gf, gq2, gk2, sink, w_in_s, rest, cj=None):
    s = x.shape[0]
    dist = cj is not None
    tabs = _rope_tables(s)
    bd = _block_diag()
    sink_col = jnp.repeat(sink.reshape(KV_HEADS, GROUP, 1), TQ_B, axis=1).reshape(KV_HEADS, GROUP * TQ_B, 1)
    shard = D_MODEL // N_SHARD

    h, qkraw, qa, ka, va, qb, kb, vb, gates = _in_proj(x, mod6, g1, w_in_s, gq2, gk2, bd, tabs)
    bias = _window_bias(TQ_B)
    (yb, yb_heads, lse_b), _ = _attn_b_fwd(qb, kb, vb, sink_col, bias)
    (ya, ya_heads, lse_a), gathered = _attn_a_fwd(qa, ka, va, rider=_gather_rider(rest) if dist else None)
    wbr_s, w_out, w_mi_s, w_mo = gathered if dist else rest
    wbr_s = wbr_s.reshape(N_SHARD, 2, BRANCH_W, shard)
    w_out = w_out.reshape(D_MODEL, D_MODEL)
    w_mo = w_mo.reshape(D_FF, D_MODEL)
    ua, ub, merged, o, x1 = _post_attn(ya, yb, gates, x, mod6, wbr_s, w_out)
    h2, a, hid = _mlp_in(x1, mod6, g2, w_mi_s)
    dx2, dm, da, acc_out = _mlp_out_loss(hid, x1, a, target, mod6, gf, w_mo)

    g_w_mo = _wgrad("wgrad_mlp_out", hid, dm, (D_FF, D_MODEL), pl.BlockSpec((D_MODEL, D_MODEL), lambda i, j, k: (i, 0)),
                    D_MODEL, D_MODEL).reshape(N_SHARD, D_MODEL, D_MODEL)
    g_w_mi = _wgrad("wgrad_mlp_in", h2, da, (N_SHARD, D_MODEL, D_MODEL),
                    pl.BlockSpec((None, D_MODEL, D_MODEL), lambda i, j, k: (j, i, 0)), D_MODEL, D_MODEL)
    mlp = _Reduction(("mlp_out", "mlp_in"), (g_w_mo, g_w_mi), cj)
    (dx1, do, acc_mlp), got = _mlp_bwd(da, x1, dx2, o, mod6, g2, w_mi_s, rider=mlp.swap() if dist else None)
    (dua, dub, dgates, doa, dob), landed = _merge_bwd(do, gates, ua, ub, w_out, wbr_s, rider=mlp.add(got) if dist else None)
    g_w_out = _wgrad("wgrad_out", merged, do, (D_MODEL, D_MODEL), pl.BlockSpec((D_MODEL, D_MODEL), lambda i, j, k: (i, 0)),
                     D_MODEL, D_MODEL).reshape(N_SHARD, shard, D_MODEL)
    g_wbr = _wgrad_branch(ya, yb, dua, dub)
    out = _Reduction(("out", "branch"), (g_w_out, g_wbr.reshape(N_SHARD, 2 * BRANCH_W, shard)), cj)
    (dqa, dka, dva), landings = _attn_a_bwd(qa, ka, va, ya_heads, doa, lse_a,
                                            rider=_riders(out.swap(), mlp.total(landed)) if dist else None)
    (dqb, dkb, dvb, dsink), _ = _attn_b_bwd(qb, kb, vb, yb_heads, dob, lse_b, sink_col, bias)
    heads = (Q_HEADS, s, HEAD_DIM)
    (dproj, acc_qk), landed = _qk_bwd(dqa.reshape(heads), dka, dva, dqb.reshape(heads), dkb, dvb, qkraw, dgates, gq2, gk2, bd,
                                      tabs, rider=out.add(landings[:2]) if dist else None)
    w = IN_W // N_SHARD
    g_w_in = _wgrad("wgrad_in", h, dproj, (N_SHARD, D_MODEL, w), pl.BlockSpec((None, D_MODEL, w), lambda i, j, k: (j, i, 0)),
                    D_MODEL, w)
    grad_x, acc_in = _in_proj_bwd(dproj, x, dx1, mod6, g1, w_in_s)
    accs = (acc_out, acc_mlp, acc_in, acc_qk, dsink)
    if not dist:
        return grad_x, (g_w_in, g_wbr, g_w_out, g_w_mi, g_w_mo), accs
    r_mo, r_mi = landings[2:]
    return grad_x, (_Reduction(("in",), (g_w_in,), cj), out.total(landed), r_mi, r_mo), accs


def _me():
    return lax.axis_index("x"), lax.axis_index("y"), lax.axis_index("c")


def _peer(d):
    x, y, c = _me()
    return (1 - x if d & 4 else x, 1 - y if d & 2 else y, 1 - c if d & 1 else c)


def _dev_index(p):
    return 4 * p[0] + 2 * p[1] + p[2]


def _chip_index(p):
    return 2 * p[0] + p[1]


def _remote(src, dst, send_sem, recv_sem, to):
    return pltpu.make_async_remote_copy(src_ref=src, dst_ref=dst, send_sem=send_sem, recv_sem=recv_sem,
                                        device_id=to, device_id_type=MESH)


SLOT_ROWS = 8


def _ada_fwd(c, w_ada, b4):
    cols = w_ada.shape[1]

    def body(c_ref, w_ref, b_ref, mod_ref, sc_ref, cbuf, pbuf, mbuf, send1, recv1, send2, recv2):
        me = _me()
        mine, chip = _dev_index(me), _chip_index(me)
        cbuf[mine] = jnp.broadcast_to(c_ref[...], (SLOT_ROWS, D_MODEL))
        gather = [_remote(cbuf.at[mine], cbuf.at[mine], send1.at[d - 1], recv1.at[d - 1], _peer(d)) for d in range(1, N_DEV)]
        for cp in gather:
            cp.start()
        for d in range(1, N_DEV):
            _remote(cbuf.at[mine], cbuf.at[_dev_index(_peer(d))], send1.at[d - 1], recv1.at[d - 1], _peer(d)).wait_recv()
        call = cbuf[...].reshape(N_DEV * SLOT_ROWS, D_MODEL)
        sc = call * _sigmoid(call)
        for s in range(N_DEV):
            sc_ref[s:s + 1, :] = sc[SLOT_ROWS * s:SLOT_ROWS * s + 1]
        part = _nn(sc.astype(BF16), w_ref[...].astype(BF16)) + b_ref[pl.ds(chip, 1), :]
        pbuf[...] = part.reshape(N_DEV, SLOT_ROWS, cols)
        mbuf[chip] = pbuf[mine]
        spread = [_remote(pbuf.at[_dev_index(_peer(d))], mbuf.at[chip], send2.at[d // 2 - 1], recv2.at[d // 2 - 1], _peer(d))
                  for d in (2, 4, 6)]
        for cp in spread:
            cp.start()
        for d in (2, 4, 6):
            _remote(pbuf.at[mine], mbuf.at[_chip_index(_peer(d))], send2.at[d // 2 - 1], recv2.at[d // 2 - 1],
                    _peer(d)).wait_recv()
        half = D_MODEL // 2
        for p in range(2 * 6):
            col = half * p
            mod_ref[p // 2:p // 2 + 1, half * (p % 2):half * (p % 2 + 1)] = mbuf[col // cols, 0:1, col % cols:col % cols + half]
        for cp in gather + spread:
            cp.wait_send()

    vm = pl.BlockSpec(memory_space=pltpu.VMEM)
    return _call(
        body, name="ada_fwd", in_specs=[vm, vm, vm], out_specs=[vm, vm],
        out_shape=[jax.ShapeDtypeStruct((6, D_MODEL), F32), jax.ShapeDtypeStruct((N_DEV, D_MODEL), F32)],
        scratch_shapes=[pltpu.VMEM((N_DEV, SLOT_ROWS, D_MODEL), F32), pltpu.VMEM((N_DEV, SLOT_ROWS, cols), F32),
                        pltpu.VMEM((N_SHARD, SLOT_ROWS, cols), F32),
                        pltpu.SemaphoreType.DMA((N_DEV - 1,)), pltpu.SemaphoreType.DMA((N_DEV - 1,)),
                        pltpu.SemaphoreType.DMA((N_SHARD - 1,)), pltpu.SemaphoreType.DMA((N_SHARD - 1,))],
        compiler_params=_params(),
    )(c, w_ada, b4)


PACK_ROWS = 16
PACK_W = 3 * D_MODEL


def _ada_bwd(acc_out, acc_mlp, acc_in, acc_qk, dsink, sc_all, rider=None):
    cols = 6 * D_MODEL // N_SHARD

    def body(out_ref, mlp_ref, in_ref, qk_ref, dsink_ref, sc_ref,
             gwa_ref, gba_ref, gn1_ref, gn2_ref, gf_ref, gq_ref, gk_ref, gs_ref, loss_ref, blk, send, recv):
        me = _me()
        mine, chip = _dev_index(me), _chip_index(me)
        blk[mine] = jnp.zeros((PACK_ROWS, PACK_W), F32)
        dmod = (in_ref, 0), (in_ref, 1), (mlp_ref, 3), (mlp_ref, 0), (mlp_ref, 1), (out_ref, 2)
        half = D_MODEL // 2
        for p in range(2 * 6):
            ref, row = dmod[p // 2]
            col = half * p
            blk[mine, col // cols:col // cols + 1, col % cols:col % cols + half] = ref[row:row + 1, half * (p % 2):half * (p % 2 + 1)]
        blk[mine, 4:5, 0:D_MODEL] = in_ref[2:3, :]
        blk[mine, 4:5, D_MODEL:2 * D_MODEL] = mlp_ref[2:3, :]
        blk[mine, 4:5, 2 * D_MODEL:] = out_ref[1:2, :]
        blk[mine, 5:6, 0:LANES] = qk_ref[0:1, :]
        blk[mine, 5:6, LANES:2 * LANES] = qk_ref[1:2, :]
        blk[mine, 6:7, 0:D_MODEL] = out_ref[0:1, :]
        for g in range(KV_HEADS):
            blk[mine, 8 + GROUP * g:8 + GROUP * (g + 1), 0:LANES] = dsink_ref[g, 0:GROUP, :]
        copies = [_remote(blk.at[mine], blk.at[mine], send.at[d - 1], recv.at[d - 1], _peer(d)) for d in range(1, N_DEV)]
        for cp in copies:
            cp.start()
        for d in range(1, N_DEV):
            _remote(blk.at[mine], blk.at[_dev_index(_peer(d))], send.at[d - 1], recv.at[d - 1], _peer(d)).wait_recv()
        tot = blk[0]
        for s in range(1, N_DEV):
            tot = tot + blk[s]
        for j in range(N_SHARD):
            gba_ref[:, cols * j:cols * (j + 1)] = tot[j:j + 1, :cols]
        gn1_ref[...] = tot[4:5, 0:D_MODEL]
        gn2_ref[...] = tot[4:5, D_MODEL:2 * D_MODEL]
        gf_ref[...] = tot[4:5, 2 * D_MODEL:]
        gq_ref[...] = tot[5:6, 0:HEAD_DIM] + tot[5:6, HEAD_DIM:2 * HEAD_DIM]
        gk_ref[...] = tot[5:6, LANES:LANES + HEAD_DIM] + tot[5:6, LANES + HEAD_DIM:2 * LANES]
        sq = tot[8:16, 0:Q_HEADS]
        diag = lax.broadcasted_iota(jnp.int32, sq.shape, 0) == lax.broadcasted_iota(jnp.int32, sq.shape, 1)
        gs_ref[...] = jnp.sum(jnp.where(diag, sq, 0.0), axis=0, keepdims=True)
        half_mse = (0.5 / D_MODEL) * jnp.sum(tot[6:7, 0:D_MODEL], axis=-1, keepdims=True)
        loss_ref[...] = jnp.broadcast_to(half_mse, (1, LANES))
        dm = jnp.concatenate([blk[s, pl.ds(chip, 1), pl.ds(0, cols)] for s in range(N_DEV)], axis=0)
        gwa_ref[...] = _tn(sc_ref[...], dm)
        for cp in copies:
            cp.wait_send()

    vm = pl.BlockSpec(memory_space=pltpu.VMEM)
    row = lambda n: jax.ShapeDtypeStruct((1, n), F32)
    return _hosted(
        body, rider, name="ada_bwd", grid=(), in_specs=[vm] * 6, out_specs=[vm] * 9,
        out_shape=[jax.ShapeDtypeStruct((D_MODEL, cols), F32), row(6 * D_MODEL), row(D_MODEL), row(D_MODEL), row(D_MODEL),
                   row(HEAD_DIM), row(HEAD_DIM), row(Q_HEADS), row(LANES)],
        scratch_shapes=[pltpu.VMEM((N_DEV, PACK_ROWS, PACK_W), F32),
                        pltpu.SemaphoreType.DMA((N_DEV - 1,)), pltpu.SemaphoreType.DMA((N_DEV - 1,))],
        args=(acc_out, acc_mlp, acc_in, acc_qk, dsink, sc_all))


def _cast_weights(ws):
    n = len(ws)

    def body(*refs):
        src, out, tmp, sems = refs[:n], refs[n:2 * n], refs[2 * n:3 * n], refs[3 * n]
        chip = _chip_index(_me())
        copies = []
        for a in range(n):
            tmp[a][...] = src[a][...].astype(BF16)
            cp = pltpu.make_async_copy(tmp[a], out[a].at[chip], sems.at[a])
            cp.start()
            copies.append(cp)
        for cp in copies:
            cp.wait()

    vm = pl.BlockSpec(memory_space=pltpu.VMEM)
    return _call(
        body, name="cast_weights", in_specs=[vm] * n, out_specs=[ANY] * n,
        out_shape=[jax.ShapeDtypeStruct((N_SHARD,) + w.shape, BF16) for w in ws],
        scratch_shapes=[pltpu.VMEM(w.shape, BF16) for w in ws] + [pltpu.SemaphoreType.DMA((n,))],
        compiler_params=_params(),
    )(*ws)


def _half_rows(ref_rows, c):
    half = ref_rows // 2
    return pl.ds(pl.multiple_of(c * half, 8), half)


class _Rider:
    def __init__(self, inputs, out_shape, aliases, n_sems, start, finish, middle=None):
        self.inputs, self.out_shape, self.aliases, self.n_sems = list(inputs), list(out_shape), dict(aliases), n_sems
        self.start, self.finish, self.middle = start, finish, middle


def _riders(*rs):
    ins = [0]
    outs = [0]
    sems = [0]
    for r in rs:
        ins.append(ins[-1] + len(r.inputs))
        outs.append(outs[-1] + len(r.out_shape))
        sems.append(sems[-1] + r.n_sems)

    def phase(which):
        def run(in_refs, out_refs, sem):
            for k, r in enumerate(rs):
                fn = getattr(r, which)
                if fn is not None:
                    fn(in_refs[ins[k]:ins[k + 1]], out_refs[outs[k]:outs[k + 1]], lambda j, base=sems[k]: sem(base + j))
        return run

    aliases = {ins[k] + i: outs[k] + o for k, r in enumerate(rs) for i, o in r.aliases.items()}
    return _Rider([a for r in rs for a in r.inputs], [o for r in rs for o in r.out_shape], aliases, sems[-1],
                  phase("start"), phase("finish"), phase("middle") if any(r.middle for r in rs) else None)


def _hosted(body, rider, *, name, grid, in_specs, out_specs, out_shape, args, scratch_shapes=(), middle_at=None):
    where = dict(grid=grid, compiler_params=_params(("arbitrary",) * len(grid))) if grid else dict(compiler_params=_params())
    if rider is None:
        res = _call(body, name=name, in_specs=in_specs, out_specs=out_specs, out_shape=out_shape,
                    scratch_shapes=list(scratch_shapes), **where)(*args)
        return res, ()
    n_in, n_out, n_scr = len(in_specs), len(out_specs), len(scratch_shapes)
    r_in, r_out = len(rider.inputs), len(rider.out_shape)

    def riding(*refs):
        at = 0
        parts = []
        for size in (n_in, r_in, n_out, r_out, n_scr):
            parts.append(refs[at:at + size])
            at += size
        ins, rider_ins, outs, rider_outs, scratch = parts
        sems = refs[at]

        def sem_at(k):
            return sems.at[k]

        if not grid:
            rider.start(rider_ins, rider_outs, sem_at)
            body(*ins, *outs, *scratch)
            if rider.middle is not None:
                rider.middle(rider_ins, rider_outs, sem_at)
            rider.finish(rider_ins, rider_outs, sem_at)
            return
        step = pl.program_id(0)
        for axis in range(1, len(grid)):
            step = step * grid[axis] + pl.program_id(axis)
        steps = 1
        for size in grid:
            steps *= size

        @pl.when(step == 0)
        def _():
            rider.start(rider_ins, rider_outs, sem_at)

        body(*ins, *outs, *scratch)
        if rider.middle is not None:
            @pl.when(step == middle_at)
            def _():
                rider.middle(rider_ins, rider_outs, sem_at)

        @pl.when(step == steps - 1)
        def _():
            rider.finish(rider_ins, rider_outs, sem_at)

    res = _call(
        riding, name=name, in_specs=list(in_specs) + [ANY] * r_in, out_specs=list(out_specs) + [ANY] * r_out,
        out_shape=list(out_shape) + rider.out_shape,
        input_output_aliases={n_in + i: n_out + o for i, o in rider.aliases.items()},
        scratch_shapes=list(scratch_shapes) + [pltpu.SemaphoreType.DMA((rider.n_sems,))], **where,
    )(*args, *rider.inputs)
    return res[:n_out], res[n_out:]


def _alone(name, rider):
    n_in, n_out = len(rider.inputs), len(rider.out_shape)

    def body(*refs):
        ins, outs, sems = refs[:n_in], refs[n_in:n_in + n_out], refs[n_in + n_out]

        def sem_at(k):
            return sems.at[k]

        rider.start(ins, outs, sem_at)
        if rider.middle is not None:
            rider.middle(ins, outs, sem_at)
        rider.finish(ins, outs, sem_at)

    return _call(
        body, name=name, in_specs=[ANY] * n_in, out_specs=[ANY] * n_out, out_shape=rider.out_shape,
        input_output_aliases=rider.aliases, scratch_shapes=[pltpu.SemaphoreType.DMA((rider.n_sems,))],
    )(*rider.inputs)


OTHER_CHIPS = (2, 4, 6)


def _gather_rider(stacked):
    n = len(stacked)

    def flights(bufs, sem):
        me = _me()
        chip, sib = _chip_index(me), _peer(1)
        out = []
        for a in range(n):
            mine, theirs = (_half_rows(bufs[a].shape[1], c) for c in (me[2], 1 - me[2]))
            for j, d in enumerate(OTHER_CHIPS):
                k = 3 * a + j
                from_chip = _chip_index(_peer(d))
                own, landed, passed = bufs[a].at[chip, mine], bufs[a].at[from_chip, mine], bufs[a].at[from_chip, theirs]
                out.append((_remote(own, own, sem(k), sem(3 * n + k), _peer(d)),
                            _remote(own, landed, sem(k), sem(3 * n + k), _peer(d)),
                            _remote(landed, landed, sem(6 * n + k), sem(9 * n + k), sib),
                            _remote(passed, passed, sem(6 * n + k), sem(9 * n + k), sib)))
        return out

    def start(ins, outs, sem):
        for send, _, _, _ in flights(outs, sem):
            send.start()

    def middle(ins, outs, sem):
        for _, arrival, pass_on, _ in flights(outs, sem):
            arrival.wait_recv()
            pass_on.start()

    def finish(ins, outs, sem):
        every = flights(outs, sem)
        for _, _, _, passed_to_me in every:
            passed_to_me.wait_recv()
        for send, _, pass_on, _ in every:
            send.wait_send()
            pass_on.wait_send()

    return _Rider(stacked, [jax.ShapeDtypeStruct(w.shape, w.dtype) for w in stacked], {a: a for a in range(n)}, 12 * n,
                  start, finish, middle)


def _swap_rider(grads):
    n = len(grads)

    def copies(ins, outs, sem):
        c = _me()[2]
        return [_remote(ins[a].at[pl.ds(0, N_SHARD), _half_rows(ins[a].shape[1], 1 - c)], outs[a], sem(a), sem(n + a), _peer(1))
                for a in range(n)]

    def start(ins, outs, sem):
        for cp in copies(ins, outs, sem):
            cp.start()

    def finish(ins, outs, sem):
        for cp in copies(ins, outs, sem):
            cp.wait()

    return _Rider(grads, [jax.ShapeDtypeStruct((N_SHARD, g.shape[1] // 2, g.shape[2]), F32) for g in grads], {}, 2 * n,
                  start, finish)


def _row_tile(rows):
    return min(rows, 256)


def _add_halves(name, g, got, cj):
    _, half, cols = got.shape
    tr = _row_tile(half)
    nt = half // tr

    def body(cj_ref, g_ref, got_ref, o_ref):
        o_ref[...] = (g_ref[...] + got_ref[...]).astype(BF16)

    spec = pl.BlockSpec((None, tr, cols), lambda i, s, cj: (s, i, 0))
    return _call(
        body, name=name,
        grid_spec=pltpu.PrefetchScalarGridSpec(
            num_scalar_prefetch=1, grid=(nt, N_SHARD),
            in_specs=[pl.BlockSpec((None, tr, cols), lambda i, s, cj: (s, cj[0] * nt + i, 0)), spec], out_specs=spec),
        out_shape=jax.ShapeDtypeStruct(got.shape, BF16), compiler_params=_params(("parallel", "parallel")),
    )(cj, g, got)


def _scatter_rider(sums):
    n = len(sums)

    def flights(ins, outs, sem):
        chip = _chip_index(_me())
        out = []
        for a in range(n):
            for j, d in enumerate(OTHER_CHIPS):
                k = 3 * a + j
                other = _chip_index(_peer(d))
                out.append((_remote(ins[a].at[other], outs[a].at[chip], sem(k), sem(3 * n + k), _peer(d)),
                            _remote(ins[a].at[chip], outs[a].at[other], sem(k), sem(3 * n + k), _peer(d))))
        return out

    def start(ins, outs, sem):
        for send, _ in flights(ins, outs, sem):
            send.start()

    def finish(ins, outs, sem):
        every = flights(ins, outs, sem)
        for _, arrival in every:
            arrival.wait_recv()
        for send, _ in every:
            send.wait_send()

    return _Rider(sums, [jax.ShapeDtypeStruct(v.shape, v.dtype) for v in sums], {}, 6 * n, start, finish)


def _sum_chips(name, g, got, landed, cj):
    _, half, cols = got.shape
    tr = _row_tile(half)
    nt = half // tr

    def body(cj_ref, g_ref, got_ref, landed_ref, o_ref):
        own = g_ref[...] + got_ref[...]
        total = None
        for s in range(N_SHARD):
            term = jnp.where(cj_ref[1] == s, own, landed_ref[s].astype(F32))
            total = term if total is None else total + term
        o_ref[...] = total

    return _call(
        body, name=name,
        grid_spec=pltpu.PrefetchScalarGridSpec(
            num_scalar_prefetch=1, grid=(nt,),
            in_specs=[pl.BlockSpec((None, tr, cols), lambda i, cj: (cj[1], cj[0] * nt + i, 0)),
                      pl.BlockSpec((None, tr, cols), lambda i, cj: (cj[1], i, 0)),
                      pl.BlockSpec((N_SHARD, tr, cols), lambda i, cj: (0, i, 0))],
            out_specs=pl.BlockSpec((tr, cols), lambda i, cj: (cj[0] * nt + i, 0))),
        out_shape=jax.ShapeDtypeStruct((2 * half, cols), F32), compiler_params=_params(("parallel",)),
    )(cj, g, got, landed)


def _join_rider(shards):
    n = len(shards)

    def flights(bufs, sem):
        c = _me()[2]
        out = []
        for a in range(n):
            mine, theirs = (bufs[a].at[_half_rows(bufs[a].shape[0], cc)] for cc in (c, 1 - c))
            out.append((_remote(mine, mine, sem(a), sem(n + a), _peer(1)), _remote(theirs, theirs, sem(a), sem(n + a), _peer(1))))
        return out

    def start(ins, outs, sem):
        for send, _ in flights(outs, sem):
            send.start()

    def finish(ins, outs, sem):
        for send, arrival in flights(outs, sem):
            arrival.wait_recv()
            send.wait_send()

    return _Rider(shards, [jax.ShapeDtypeStruct(h.shape, F32) for h in shards], {a: a for a in range(n)}, 2 * n, start, finish)


class _Reduction:
    def __init__(self, names, grads, cj):
        self.names, self.grads, self.cj = names, list(grads), cj

    def swap(self):
        return _swap_rider(self.grads)

    def add(self, got):
        self.got = list(got)
        self.sums = [_add_halves("add_halves_" + nm, g, h, self.cj) for nm, g, h in zip(self.names, self.grads, self.got)]
        return _scatter_rider(self.sums)

    def total(self, landed):
        halves = [_sum_chips("sum_chips_" + nm, g, h, l, self.cj)
                  for nm, g, h, l in zip(self.names, self.grads, self.got, landed)]
        return _join_rider(halves)


def _adamw_math(w, g, m, v):
    m = ADAM_B1 * m + (1.0 - ADAM_B1) * g
    v = ADAM_B2 * v + (1.0 - ADAM_B2) * jnp.square(g)
    m_hat = m / (1.0 - ADAM_B1 ** ADAM_STEP)
    v_hat = v / (1.0 - ADAM_B2 ** ADAM_STEP)
    return -ADAM_LR * (m_hat / (jnp.sqrt(v_hat) + ADAM_EPS) + ADAM_WD * w), m, v


def _adamw(name, ws, gs, ms, vs, rider=None):
    n = len(ws)
    rows = ws[0].shape[0]
    tr = _row_tile(rows)

    def body(*refs):
        ins, outs = refs[:4 * n], refs[4 * n:]
        for a in range(n):
            w, g, m, v = (ins[k * n + a][...] for k in range(4))
            outs[a][...], outs[n + a][...], outs[2 * n + a][...] = _adamw_math(w, g, m, v)

    specs = [pl.BlockSpec((tr, w.shape[1]), lambda i: (i, 0)) for w in ws]
    res, riding = _hosted(
        body, rider, name=name, grid=(rows // tr,), in_specs=specs * 4, out_specs=specs * 3,
        out_shape=[jax.ShapeDtypeStruct(w.shape, F32) for w in ws] * 3, args=(*ws, *gs, *ms, *vs))
    return (res[:n], res[n:2 * n], res[2 * n:]), riding


def _adamw_small(ws, gs, ms, vs):
    n = len(ws)

    def body(*refs):
        ins, outs = refs[:4 * n], refs[4 * n:]
        for a in range(n):
            w, g, m, v = (ins[k * n + a][...] for k in range(4))
            outs[a][...], outs[n + a][...], outs[2 * n + a][...] = _adamw_math(w, g, m, v)

    vm = pl.BlockSpec(memory_space=pltpu.VMEM)
    res = _call(
        body, name="adamw_small", in_specs=[vm] * (4 * n), out_specs=[vm] * (3 * n),
        out_shape=[jax.ShapeDtypeStruct(w.shape, F32) for w in ws] * 3, compiler_params=_params(),
    )(*ws, *gs, *ms, *vs)
    return res[:n], res[n:2 * n], res[2 * n:]


def kernel(x, c, w_ada, b_ada, norm1_g, w_in, q_norm_a, k_norm_a, sink_b, w_branch, w_out, norm2_g, w_mlp_in, w_mlp_out, final_g, loss_target, m_w_ada, m_b_ada, m_norm1_g, m_w_in, m_q_norm_a, m_k_norm_a, m_sink_b, m_w_branch, m_w_out, m_norm2_g, m_w_mlp_in, m_w_mlp_out, m_final_g, v_w_ada, v_b_ada, v_norm1_g, v_w_in, v_q_norm_a, v_k_norm_a, v_sink_b, v_w_branch, v_w_out, v_norm2_g, v_w_mlp_in, v_w_mlp_out, v_final_g):
    xi, yi, ci = _me()
    cj = jnp.stack([ci, 2 * xi + yi]).astype(jnp.int32)
    n_cols = 6 * D_MODEL // N_SHARD

    mod6, sc_all = _ada_fwd(c, w_ada[0], b_ada.reshape(N_SHARD, n_cols))

    def rows2d(a):
        return a.reshape(-1, a.shape[-1])

    big = (w_in, w_branch, w_out, w_mlp_in, w_mlp_out)
    stacked = _cast_weights([rows2d(w) for w in big])
    w_in_s, = _alone("gather_w_in", _gather_rider(stacked[:1]))
    rest = stacked[1:]

    gq2 = jnp.tile(q_norm_a, (1, 2))
    gk2 = jnp.tile(k_norm_a, (1, 2))
    grad_x, (w_in_red, join_out, g_mi, g_mo), accs = _local_step(
        x[0], loss_target[0], mod6, norm1_g, norm2_g, final_g.reshape(1, D_MODEL), gq2, gk2, sink_b[0], w_in_s, rest, cj)

    (g_w_ada, g_b_ada, g_n1, g_n2, g_f, g_q, g_k, g_s, loss_row), got_in = _ada_bwd(*accs, sc_all, rider=w_in_red.swap())
    loss = loss_row[0, 0]
    moments = dict(w_ada=(m_w_ada, v_w_ada), w_in=(m_w_in, v_w_in), w_branch=(m_w_branch, v_w_branch), w_out=(m_w_out, v_w_out),
                   w_mlp_in=(m_w_mlp_in, v_w_mlp_in), w_mlp_out=(m_w_mlp_out, v_w_mlp_out))
    weights = dict(w_ada=w_ada, w_in=w_in, w_branch=w_branch, w_out=w_out, w_mlp_in=w_mlp_in, w_mlp_out=w_mlp_out)

    def adamw(call, names, grads, rider=None):
        (d, m, v), riding = _adamw(call, [rows2d(weights[nm]) for nm in names], grads,
                                   [rows2d(moments[nm][0]) for nm in names], [rows2d(moments[nm][1]) for nm in names], rider)
        return {nm: (grads[k], d[k], m[k], v[k]) for k, nm in enumerate(names)}, riding

    big_res, landed_in = adamw("adamw_ada_mlp", ("w_ada", "w_mlp_in", "w_mlp_out"), [g_w_ada, g_mi, g_mo], w_in_red.add(got_in))
    g_in, g_out, g_br = _alone("join_in_out_branch", _riders(w_in_red.total(landed_in), join_out))
    big_res.update(adamw("adamw_in_branch", ("w_in", "w_branch"), [g_in, g_br])[0])

    small = ("b_ada", "norm1_g", "q_norm_a", "k_norm_a", "sink_b", "norm2_g", "final_g", "w_out")
    row = lambda a: a.reshape(1, -1)
    small_w = [row(a) for a in (b_ada, norm1_g, q_norm_a, k_norm_a, sink_b, norm2_g, final_g)] + [w_out[0]]
    small_g = [g_b_ada, g_n1, g_q, g_k, g_s, g_n2, g_f, g_out]
    small_m = [row(a) for a in (m_b_ada, m_norm1_g, m_q_norm_a, m_k_norm_a, m_sink_b, m_norm2_g, m_final_g)] + [m_w_out[0]]
    small_v = [row(a) for a in (v_b_ada, v_norm1_g, v_q_norm_a, v_k_norm_a, v_sink_b, v_norm2_g, v_final_g)] + [v_w_out[0]]
    s_d, s_m, s_v = _adamw_small(small_w, small_g, small_m, small_v)

    order = ("w_ada", "b_ada", "norm1_g", "w_in", "q_norm_a", "k_norm_a", "sink_b", "w_branch", "w_out", "norm2_g",
             "w_mlp_in", "w_mlp_out", "final_g")
    like = dict(w_ada=w_ada, b_ada=b_ada, norm1_g=norm1_g, w_in=w_in, q_norm_a=q_norm_a, k_norm_a=k_norm_a, sink_b=sink_b,
                w_branch=w_branch, w_out=w_out, norm2_g=norm2_g, w_mlp_in=w_mlp_in, w_mlp_out=w_mlp_out, final_g=final_g)
    grad, delta, new_m, new_v = {}, {}, {}, {}
    for nm, res in big_res.items():
        grad[nm], delta[nm], new_m[nm], new_v[nm] = res
    for k, nm in enumerate(small):
        grad[nm], delta[nm], new_m[nm], new_v[nm] = small_g[k], s_d[k], s_m[k], s_v[k]
    outs = [loss, grad_x[None]]
    for group in (grad, delta, new_m, new_v):
        outs += [group[nm].reshape(like[nm].shape) for nm in order]
    return tuple(outs)
```

```python
import functools

import jax
import jax.numpy as jnp
from jax import lax
from jax.experimental import pallas as pl
from jax.experimental.pallas import tpu as pltpu

F32 = jnp.float32
BF16 = jnp.bfloat16
MESH = pl.DeviceIdType.MESH
ANY = pl.BlockSpec(memory_space=pl.ANY)

D_MODEL = 1024
HEAD_DIM = 64
Q_HEADS = 8
KV_HEADS = 2
GROUP = Q_HEADS // KV_HEADS
BRANCH_W = Q_HEADS * HEAD_DIM
KV_W = KV_HEADS * HEAD_DIM
IN_W = 2 * (BRANCH_W + 2 * KV_W) + 2 * D_MODEL
QK_W = 2 * (BRANCH_W + 2 * KV_W)
D_FF = 4 * D_MODEL
GRID_W = 64
WINDOW = 128
ROPE_THETA = 10000.0
NORM_EPS = 1e-6
NEG_INF = -1e30
Q_SCALE = HEAD_DIM ** -0.5
N_SHARD = 4
N_DEV = 8
LANES = 128
VMEM_LIMIT = 56 * 1024 * 1024

ADAM_LR = 0.001
ADAM_B1 = 0.9
ADAM_B2 = 0.999
ADAM_EPS = 1e-08
ADAM_WD = 0.01
ADAM_STEP = 10

_call = pl.pallas_call


def _params(sem=None, vmem=VMEM_LIMIT):
    return pltpu.CompilerParams(dimension_semantics=sem, vmem_limit_bytes=vmem)


def _nt(a, b):
    return lax.dot_general(a, b, (((1,), (1,)), ((), ())), preferred_element_type=F32)


def _tn(a, b):
    return lax.dot_general(a, b, (((0,), (0,)), ((), ())), preferred_element_type=F32)


def _nn(a, b):
    return jnp.dot(a, b, preferred_element_type=F32)


def _sigmoid(z):
    return 0.5 * jnp.tanh(0.5 * z) + 0.5


def _rope_tables(s):
    t = jnp.arange(s, dtype=jnp.int32)
    lane = jnp.arange(LANES, dtype=jnp.int32)

    def cos_sin(pos, dim):
        inv = ROPE_THETA ** (-jnp.arange(0, dim, 2, dtype=F32) / dim)
        ang = pos.astype(F32)[:, None] * inv[None, :]
        return jnp.cos(ang), jnp.sin(ang)

    cr, sr = cos_sin(t // GRID_W, HEAD_DIM // 2)
    cc, sc = cos_sin(t % GRID_W, HEAD_DIM // 2)
    cos_a = jnp.tile(jnp.concatenate([cr, cr, cc, cc], axis=1), (1, 2))
    sin_a = jnp.tile(jnp.concatenate([sr, sr, sc, sc], axis=1), (1, 2))
    first_a = (lane % 32) < 16
    c1, s1 = cos_sin(t, HEAD_DIM)
    cos_b = jnp.tile(jnp.concatenate([c1, c1], axis=1), (1, 2))
    sin_b = jnp.tile(jnp.concatenate([s1, s1], axis=1), (1, 2))
    first_b = (lane % 64) < 32
    tabs_a = (cos_a, jnp.where(first_a, -sin_a, 0.0), jnp.where(first_a, 0.0, sin_a))
    tabs_b = (cos_b, jnp.where(first_b, -sin_b, 0.0), jnp.where(first_b, 0.0, sin_b))
    return tabs_a + tabs_b


def _rope(z, cos, s_lo, s_hi, half, sign=1.0):
    up = pltpu.roll(z, LANES - half, 1)
    dn = pltpu.roll(z, half, 1)
    return z * cos + sign * (up * s_lo + dn * s_hi)


def _head_mean(z2, bd):
    hi = z2.astype(BF16)
    lo = (z2 - hi.astype(F32)).astype(BF16)
    return _nn(hi, bd) + _nn(lo, bd)


def _block_diag():
    lane = jnp.arange(LANES)
    return jnp.where((lane[:, None] // HEAD_DIM) == (lane[None, :] // HEAD_DIM), 1.0 / HEAD_DIM, 0.0).astype(BF16)


def _row_spec(tm, width):
    return pl.BlockSpec((tm, width), lambda i: (i, 0))


def _heads_spec(heads, tm):
    return pl.BlockSpec((heads, tm, HEAD_DIM), lambda i: (0, i, 0))


def _full_spec(shape):
    nd = len(shape)
    return pl.BlockSpec(shape, lambda i: (0,) * nd)


def _in_proj(x, mod6, g1, w_in_s, gq, gk, bd, tabs, tm=256):
    s = x.shape[0]

    def body(x_ref, mod_ref, g1_ref, w_ref, gq_ref, gk_ref, bd_ref, ca, la, ha, cb, lb, hb,
             h_ref, qkraw_ref, qa_ref, ka_ref, va_ref, qb_ref, kb_ref, vb_ref, gate_ref):
        xt = x_ref[...]
        r = lax.rsqrt(jnp.mean(xt * xt, axis=-1, keepdims=True) + NORM_EPS)
        h = (xt * r * g1_ref[...]) * (1.0 + mod_ref[1:2, :]) + mod_ref[0:1, :]
        hb16 = h.astype(BF16)
        h_ref[...] = hb16
        proj = jnp.concatenate([_nn(hb16, w_ref[j]) for j in range(N_SHARD)], axis=1)
        qkraw_ref[...] = proj[:, :BRANCH_W + KV_W]
        bdm = bd_ref[...]
        tab_a = (ca[...], la[...], ha[...])
        tab_b = (cb[...], lb[...], hb[...])

        def norm_rope_a(z, gain):
            zn = z * lax.rsqrt(_head_mean(z * z, bdm) + NORM_EPS) * gain
            return _rope(zn, *tab_a, 16)

        def put(ref, first, z):
            zb = z.astype(BF16)
            ref[first] = zb[:, :HEAD_DIM]
            ref[first + 1] = zb[:, HEAD_DIM:]

        for i in range(Q_HEADS // 2):
            put(qa_ref, 2 * i, norm_rope_a(proj[:, LANES * i:LANES * (i + 1)], gq_ref[...]) * Q_SCALE)
        off = BRANCH_W
        put(ka_ref, 0, norm_rope_a(proj[:, off:off + LANES], gk_ref[...]))
        off += KV_W
        def put_v(ref, z):
            zb = z.astype(BF16)
            for hd in range(KV_HEADS):
                ref[hd, :, :HEAD_DIM] = zb[:, HEAD_DIM * hd:HEAD_DIM * (hd + 1)]
                ref[hd, :, HEAD_DIM:] = jnp.ones((tm, HEAD_DIM), BF16)

        put_v(va_ref, proj[:, off:off + LANES])
        off += KV_W
        for i in range(Q_HEADS // 2):
            put(qb_ref, 2 * i, _rope(proj[:, off + LANES * i:off + LANES * (i + 1)], *tab_b, 32) * Q_SCALE)
        off += BRANCH_W
        put(kb_ref, 0, _rope(proj[:, off:off + LANES], *tab_b, 32))
        off += KV_W
        put_v(vb_ref, proj[:, off:off + LANES])
        gate_ref[...] = proj[:, QK_W:].astype(BF16)

    tab_spec = _row_spec(tm, LANES)
    return _call(
        body, name="in_proj", grid=(s // tm,),
        in_specs=[_row_spec(tm, D_MODEL), _full_spec(mod6.shape), _full_spec(g1.shape), _full_spec(w_in_s.shape),
                  _full_spec(gq.shape), _full_spec(gk.shape), _full_spec(bd.shape)] + [tab_spec] * 6,
        out_specs=[_row_spec(tm, D_MODEL), _row_spec(tm, BRANCH_W + KV_W), _heads_spec(Q_HEADS, tm), _heads_spec(KV_HEADS, tm),
                   pl.BlockSpec((KV_HEADS, tm, LANES), lambda i: (0, i, 0)), _heads_spec(Q_HEADS, tm),
                   _heads_spec(KV_HEADS, tm), pl.BlockSpec((KV_HEADS, tm, LANES), lambda i: (0, i, 0)),
                   _row_spec(tm, 2 * D_MODEL)],
        out_shape=[jax.ShapeDtypeStruct((s, D_MODEL), BF16), jax.ShapeDtypeStruct((s, BRANCH_W + KV_W), F32),
                   jax.ShapeDtypeStruct((Q_HEADS, s, HEAD_DIM), BF16), jax.ShapeDtypeStruct((KV_HEADS, s, HEAD_DIM), BF16),
                   jax.ShapeDtypeStruct((KV_HEADS, s, LANES), BF16), jax.ShapeDtypeStruct((Q_HEADS, s, HEAD_DIM), BF16),
                   jax.ShapeDtypeStruct((KV_HEADS, s, HEAD_DIM), BF16), jax.ShapeDtypeStruct((KV_HEADS, s, LANES), BF16),
                   jax.ShapeDtypeStruct((s, 2 * D_MODEL), BF16)],
        compiler_params=_params(("parallel",)),
    )(x, mod6, g1, w_in_s, gq, gk, bd, *tabs)


def _group_specs(s, tq):
    q_spec = pl.BlockSpec((None, GROUP, tq, HEAD_DIM), lambda g, i: (g, 0, i, 0))
    kv_spec = pl.BlockSpec((None, s, HEAD_DIM), lambda g, i: (g, 0, 0))
    col_spec = pl.BlockSpec((None, GROUP, tq, 1), lambda g, i: (g, 0, i, 0))
    return q_spec, kv_spec, col_spec


def _attn_a_fwd(q, k, v1, rider=None, tq=256, tk=2048):
    s = q.shape[1]
    tk = min(tk, s // 2)
    rows = GROUP * tq

    n = s // tk
    assert n >= 2 and n % 2 == 0

    def body(q_ref, k_ref, v_ref, o_ref, oh_ref, lse_ref, s0_ref, s1_ref, p0_ref, p1_ref, m_ref, a_ref, acc_ref):
        s_ref, p_ref = (s0_ref, s1_ref), (p0_ref, p1_ref)
        qq = q_ref[...].reshape(rows, HEAD_DIM)
        m_ref[...] = jnp.full((rows, 1), NEG_INF, F32)
        acc_ref[...] = jnp.zeros((rows, LANES), F32)

        def keys(i):
            return pl.ds(pl.multiple_of(i * tk, tk), tk)

        def scores(i, slot):
            s_ref[slot][...] = _nt(qq, k_ref[keys(i), :])

        def softmax(slot):
            sc = s_ref[slot][...]
            m = m_ref[...]
            mn = jnp.maximum(m, jnp.max(sc, axis=-1, keepdims=True))
            m_ref[...] = mn
            a_ref[...] = jnp.exp(m - mn)
            p_ref[slot][...] = jnp.exp(sc - mn).astype(BF16)

        def weigh(i, slot):
            acc_ref[...] = a_ref[...] * acc_ref[...] + _nn(p_ref[slot][...], v_ref[keys(i), :])

        scores(0, 0)
        softmax(0)
        scores(1, 1)

        def two_steps(j, carry):
            i = 2 * j + 1
            weigh(i - 1, 0)
            softmax(1)
            scores(i + 1, 0)
            weigh(i, 1)
            softmax(0)
            scores(i + 2, 1)
            return carry

        lax.fori_loop(0, (n - 2) // 2, two_steps, 0, unroll=True)
        weigh(n - 2, 0)
        softmax(1)
        weigh(n - 1, 1)
        l = acc_ref[:, HEAD_DIM:HEAD_DIM + 1]
        o = (acc_ref[:, :HEAD_DIM] / l).astype(BF16)
        for g in range(GROUP):
            o_ref[:, HEAD_DIM * g:HEAD_DIM * (g + 1)] = o[tq * g:tq * (g + 1)]
        oh_ref[...] = o.reshape(GROUP, tq, HEAD_DIM)
        lse_ref[...] = (m_ref[...] + jnp.log(l)).reshape(GROUP, tq, 1)

    q_spec, kv_spec, col_spec = _group_specs(s, tq)
    v_spec = pl.BlockSpec((None, s, LANES), lambda g, i: (g, 0, 0))
    return _hosted(
        body, rider, name="attn_a_fwd", grid=(KV_HEADS, s // tq),
        in_specs=[q_spec, kv_spec, v_spec],
        out_specs=[pl.BlockSpec((tq, GROUP * HEAD_DIM), lambda g, i: (i, g)), q_spec, col_spec],
        out_shape=[jax.ShapeDtypeStruct((s, BRANCH_W), BF16), jax.ShapeDtypeStruct((KV_HEADS, GROUP, s, HEAD_DIM), BF16),
                   jax.ShapeDtypeStruct((KV_HEADS, GROUP, s, 1), F32)],
        scratch_shapes=[pltpu.VMEM((rows, tk), F32), pltpu.VMEM((rows, tk), F32), pltpu.VMEM((rows, tk), BF16),
                        pltpu.VMEM((rows, tk), BF16), pltpu.VMEM((rows, 1), F32), pltpu.VMEM((rows, 1), F32),
                        pltpu.VMEM((rows, LANES), F32)],
        args=(q.reshape(KV_HEADS, GROUP, s, HEAD_DIM), k, v1), middle_at=KV_HEADS * (s // tq) // 2)


def _attn_a_bwd(q, k, v1, o, do, lse, rider=None, tq=256, tk=512):
    v = v1
    s = q.shape[1]
    tk = min(tk, s // 2)
    rows = GROUP * tq

    n = s // tk
    assert n >= 2 and n % 2 == 0

    def body(q_ref, k_ref, v_ref, o_ref, do_ref, lse_ref, dq_ref, dk_ref, dv_ref,
             s0_ref, s1_ref, dp0_ref, dp1_ref, p0_ref, p1_ref, ds0_ref, ds1_ref, dq_acc):
        s_ref, dp_ref, p_ref, ds_ref = (s0_ref, s1_ref), (dp0_ref, dp1_ref), (p0_ref, p1_ref), (ds0_ref, ds1_ref)

        @pl.when(pl.program_id(1) == 0)
        def _():
            dk_ref[...] = jnp.zeros_like(dk_ref)
            dv_ref[...] = jnp.zeros_like(dv_ref)

        qq = q_ref[...].reshape(rows, HEAD_DIM)
        dd = do_ref[...].reshape(rows, HEAD_DIM)
        ls = lse_ref[...].reshape(rows, 1)
        dl = jnp.sum(dd.astype(F32) * o_ref[...].reshape(rows, HEAD_DIM).astype(F32), axis=-1, keepdims=True)
        dq_acc[...] = jnp.zeros((rows, HEAD_DIM), F32)

        def keys(i):
            return pl.ds(pl.multiple_of(i * tk, tk), tk)

        def scores(i, slot):
            s_ref[slot][...] = _nt(qq, k_ref[keys(i), :])
            dp_ref[slot][...] = _nt(dd, v_ref[keys(i), :HEAD_DIM])

        def weights(slot):
            p = jnp.exp(s_ref[slot][...] - ls)
            p_ref[slot][...] = p.astype(BF16)
            ds_ref[slot][...] = (p * (dp_ref[slot][...] - dl)).astype(BF16)

        def grads(i, slot):
            dv_ref[keys(i), :] += _tn(p_ref[slot][...], dd)
            dk_ref[keys(i), :] += _tn(ds_ref[slot][...], qq)
            dq_acc[...] += _nn(ds_ref[slot][...], k_ref[keys(i), :])

        scores(0, 0)
        weights(0)
        scores(1, 1)

        def two_steps(j, carry):
            i = 2 * j + 1
            grads(i - 1, 0)
            weights(1)
            scores(i + 1, 0)
            grads(i, 1)
            weights(0)
            scores(i + 2, 1)
            return carry

        lax.fori_loop(0, (n - 2) // 2, two_steps, 0, unroll=True)
        grads(n - 2, 0)
        weights(1)
        grads(n - 1, 1)
        dq_ref[...] = dq_acc[...].astype(BF16).reshape(GROUP, tq, HEAD_DIM)

    q_spec, kv_spec, col_spec = _group_specs(s, tq)
    v_spec = pl.BlockSpec((None, s, LANES), lambda g, i: (g, 0, 0))
    shape4 = (KV_HEADS, GROUP, s, HEAD_DIM)
    tile32, tile16 = pltpu.VMEM((rows, tk), F32), pltpu.VMEM((rows, tk), BF16)
    return _hosted(
        body, rider, name="attn_a_bwd", grid=(KV_HEADS, s // tq),
        in_specs=[q_spec, kv_spec, v_spec, q_spec, q_spec, col_spec],
        out_specs=[q_spec, kv_spec, kv_spec],
        out_shape=[jax.ShapeDtypeStruct(shape4, BF16), jax.ShapeDtypeStruct((KV_HEADS, s, HEAD_DIM), F32),
                   jax.ShapeDtypeStruct((KV_HEADS, s, HEAD_DIM), F32)],
        scratch_shapes=[tile32] * 4 + [tile16] * 4 + [pltpu.VMEM((rows, HEAD_DIM), F32)],
        args=(q.reshape(shape4), k, v, o.reshape(shape4), do.reshape(shape4), lse))


TQ_B = WINDOW


def _win_keys(tq):
    return tq + 2 * WINDOW


def _window_bias(tq):
    r = jnp.arange(tq, dtype=jnp.int32)[:, None]
    col = jnp.arange(_win_keys(tq), dtype=jnp.int32)[None, :]
    return jnp.stack([jnp.where(jnp.abs(r - col + WINDOW * b) <= WINDOW, 0.0, NEG_INF) for b in range(3)]).astype(F32)


def _band(tq, s):
    win = _win_keys(tq)

    def window(e):
        return pl.ds(pl.multiple_of(jnp.clip(e * tq - WINDOW, 0, s - win), WINDOW), win)

    def bias_index(e):
        return jnp.where(e == 0, 0, jnp.where(e >= s // tq - 1, 2, 1))

    return window, bias_index


def _pair_specs(s, tq):
    pairs = s // (2 * tq)
    cur = lambda g, j: (g, 0, jnp.minimum(j, pairs - 1), 0)
    prev = lambda g, j: (g, 0, jnp.maximum(j - 1, 0), 0)
    tile = lambda width, index: pl.BlockSpec((None, GROUP, 2 * tq, width), index)
    kv_spec = pl.BlockSpec((None, s, HEAD_DIM), lambda g, j: (g, 0, 0))
    v_spec = pl.BlockSpec((None, s, LANES), lambda g, j: (g, 0, 0))
    sink_spec = pl.BlockSpec((None, GROUP * tq, 1), lambda g, j: (g, 0, 0))
    bias_spec = pl.BlockSpec((3, tq, _win_keys(tq)), lambda g, j: (0, 0, 0))
    return tile, cur, prev, kv_spec, v_spec, sink_spec, bias_spec


def _attn_b_fwd(q, k, v1, sink_col, bias, rider=None, tq=TQ_B):
    s = q.shape[1]
    rows = GROUP * tq
    win = _win_keys(tq)
    pairs = s // (2 * tq)
    window, bias_index = _band(tq, s)

    def body(q_ref, k_ref, v_ref, sink_ref, bias_ref, o_ref, oh_ref, lse_ref, s0_ref, s1_ref, p0_ref, p1_ref, m0_ref, m1_ref):
        s_ref, p_ref, m_ref = (s0_ref, s1_ref), (p0_ref, p1_ref), (m0_ref, m1_ref)
        j = pl.program_id(1)

        @pl.when(j == 0)
        def _():
            for ref in (s0_ref, s1_ref, p0_ref, p1_ref, m0_ref, m1_ref):
                ref[...] = jnp.zeros_like(ref)

        def scores(e, slot):
            qq = q_ref[:, pl.ds(slot * tq, tq), :].reshape(rows, HEAD_DIM)
            sc = _nt(qq, k_ref[window(e), :]).reshape(GROUP, tq, win) + bias_ref[bias_index(e)][None]
            s_ref[slot][...] = sc.reshape(rows, win)

        def softmax(slot):
            sc = s_ref[slot][...]
            m = jnp.maximum(jnp.max(sc, axis=-1, keepdims=True), sink_ref[...])
            m_ref[slot][...] = m
            p_ref[slot][...] = jnp.exp(sc - m).astype(BF16)

        def finish(e, slot):
            acc = _nn(p_ref[slot][...], v_ref[window(e), :])
            m = m_ref[slot][...]
            l = acc[:, HEAD_DIM:HEAD_DIM + 1] + jnp.exp(sink_ref[...] - m)
            o = (acc[:, :HEAD_DIM] / l).astype(BF16)
            at = pl.ds(slot * tq, tq)
            for g in range(GROUP):
                o_ref[at, HEAD_DIM * g:HEAD_DIM * (g + 1)] = o[tq * g:tq * (g + 1)]
            oh_ref[:, at, :] = o.reshape(GROUP, tq, HEAD_DIM)
            lse_ref[:, at, :] = (m + jnp.log(l)).reshape(GROUP, tq, 1)

        first = 2 * j
        finish(jnp.maximum(first - 2, 0), 0)
        softmax(1)
        scores(first, 0)
        finish(jnp.maximum(first - 1, 0), 1)
        softmax(0)
        scores(first + 1, 1)

    tile, cur, prev, kv_spec, v_spec, sink_spec, bias_spec = _pair_specs(s, tq)
    tile32, tile16, col = pltpu.VMEM((rows, win), F32), pltpu.VMEM((rows, win), BF16), pltpu.VMEM((rows, 1), F32)
    return _hosted(
        body, rider, name="attn_b_fwd", grid=(KV_HEADS, pairs + 1),
        in_specs=[tile(HEAD_DIM, cur), kv_spec, v_spec, sink_spec, bias_spec],
        out_specs=[pl.BlockSpec((2 * tq, GROUP * HEAD_DIM), lambda g, j: (jnp.maximum(j - 1, 0), g)),
                   tile(HEAD_DIM, prev), tile(1, prev)],
        out_shape=[jax.ShapeDtypeStruct((s, BRANCH_W), BF16), jax.ShapeDtypeStruct((KV_HEADS, GROUP, s, HEAD_DIM), BF16),
                   jax.ShapeDtypeStruct((KV_HEADS, GROUP, s, 1), F32)],
        scratch_shapes=[tile32, tile32, tile16, tile16, col, col],
        args=(q.reshape(KV_HEADS, GROUP, s, HEAD_DIM), k, v1, sink_col, bias))


def _attn_b_bwd(q, k, v1, o, do, lse, sink_col, bias, rider=None, tq=TQ_B):
    s = q.shape[1]
    rows = GROUP * tq
    win = _win_keys(tq)
    pairs = s // (2 * tq)
    window, bias_index = _band(tq, s)

    def body(q_ref, k_ref, v_ref, o_ref, do_ref, lse_ref, sink_ref, bias_ref, dq_ref, dk_ref, dv_ref, dsink_ref,
             s0, s1, dp0, dp1, p0, p1, ds0, ds1, q0, q1, d0, d1, ls0, ls1, dl0, dl1):
        s_ref, dp_ref, p_ref, ds_ref = (s0, s1), (dp0, dp1), (p0, p1), (ds0, ds1)
        q_keep, do_keep, lse_keep, delta_keep = (q0, q1), (d0, d1), (ls0, ls1), (dl0, dl1)
        j = pl.program_id(1)

        @pl.when(j == 0)
        def _():
            for ref in (dk_ref, dv_ref, dsink_ref, s0, s1, dp0, dp1, p0, p1, ds0, ds1, q0, q1, d0, d1, ls0, ls1, dl0, dl1):
                ref[...] = jnp.zeros_like(ref)

        def scores(e, slot):
            at = pl.ds(slot * tq, tq)
            qq = q_ref[:, at, :].reshape(rows, HEAD_DIM)
            dd = do_ref[:, at, :].reshape(rows, HEAD_DIM)
            q_keep[slot][...] = qq
            do_keep[slot][...] = dd
            lse_keep[slot][...] = lse_ref[:, at, :].reshape(rows, 1)
            delta_keep[slot][...] = jnp.sum(dd.astype(F32) * o_ref[:, at, :].reshape(rows, HEAD_DIM).astype(F32), axis=-1,
                                            keepdims=True)
            sc = _nt(qq, k_ref[window(e), :]).reshape(GROUP, tq, win) + bias_ref[bias_index(e)][None]
            s_ref[slot][...] = sc.reshape(rows, win)
            dp_ref[slot][...] = _nt(dd, v_ref[window(e), :HEAD_DIM])

        def weights(slot):
            p = jnp.exp(s_ref[slot][...] - lse_keep[slot][...])
            p_ref[slot][...] = p.astype(BF16)
            ds_ref[slot][...] = (p * (dp_ref[slot][...] - delta_keep[slot][...])).astype(BF16)

        def grads(e, slot, live):
            at = window(e)
            ds = ds_ref[slot][...]
            dv_ref[at, :] += _tn(p_ref[slot][...], do_keep[slot][...])
            dk_ref[at, :] += _tn(ds, q_keep[slot][...])
            dq_ref[:, pl.ds(slot * tq, tq), :] = _nn(ds, k_ref[at, :]).astype(BF16).reshape(GROUP, tq, HEAD_DIM)
            dsk = jnp.exp(sink_ref[...] - lse_keep[slot][...]) * delta_keep[slot][...] * live
            for g in range(GROUP):
                dsink_ref[g:g + 1, :] -= jnp.broadcast_to(jnp.sum(dsk[tq * g:tq * (g + 1)], axis=0, keepdims=True), (1, LANES))

        first = 2 * j
        live = jnp.where(j > 0, 1.0, 0.0)
        grads(jnp.maximum(first - 2, 0), 0, live)
        weights(1)
        scores(first, 0)
        grads(jnp.maximum(first - 1, 0), 1, live)
        weights(0)
        scores(first + 1, 1)

    tile, cur, prev, kv_spec, v_spec, sink_spec, bias_spec = _pair_specs(s, tq)
    dsink_spec = pl.BlockSpec((None, ACC_ROWS, LANES), lambda g, j: (g, 0, 0))
    shape4 = (KV_HEADS, GROUP, s, HEAD_DIM)
    tile32, tile16 = pltpu.VMEM((rows, win), F32), pltpu.VMEM((rows, win), BF16)
    keep, col = pltpu.VMEM((rows, HEAD_DIM), BF16), pltpu.VMEM((rows, 1), F32)
    return _hosted(
        body, rider, name="attn_b_bwd", grid=(KV_HEADS, pairs + 1),
        in_specs=[tile(HEAD_DIM, cur), kv_spec, v_spec, tile(HEAD_DIM, cur), tile(HEAD_DIM, cur), tile(1, cur), sink_spec,
                  bias_spec],
        out_specs=[tile(HEAD_DIM, prev), kv_spec, kv_spec, dsink_spec],
        out_shape=[jax.ShapeDtypeStruct(shape4, BF16), jax.ShapeDtypeStruct((KV_HEADS, s, HEAD_DIM), F32),
                   jax.ShapeDtypeStruct((KV_HEADS, s, HEAD_DIM), F32), jax.ShapeDtypeStruct((KV_HEADS, ACC_ROWS, LANES), F32)],
        scratch_shapes=[tile32] * 4 + [tile16] * 4 + [keep] * 4 + [col] * 4,
        args=(q.reshape(shape4), k, v1, o.reshape(shape4), do.reshape(shape4), lse, sink_col, bias))


def _post_attn(ya, yb, gates, x, mod6, wbr_s, w_out, tm=512):
    s = x.shape[0]

    def body(ya_ref, yb_ref, g_ref, x_ref, mod_ref, wbr_ref, wo_ref, ua_ref, ub_ref, mg_ref, o_ref, x1_ref):
        ya_t, yb_t = ya_ref[...], yb_ref[...]
        ua = jnp.concatenate([_nn(ya_t, wbr_ref[j, 0]) for j in range(N_SHARD)], axis=1)
        ub = jnp.concatenate([_nn(yb_t, wbr_ref[j, 1]) for j in range(N_SHARD)], axis=1)
        ga, gb = g_ref[:, :D_MODEL].astype(F32), g_ref[:, D_MODEL:].astype(F32)
        merged = (_sigmoid(ga) * ua + _sigmoid(gb) * ub).astype(BF16)
        o = _nn(merged, wo_ref[...])
        ua_ref[...] = ua.astype(BF16)
        ub_ref[...] = ub.astype(BF16)
        mg_ref[...] = merged
        o_ref[...] = o.astype(BF16)
        x1_ref[...] = x_ref[...] + mod_ref[2:3, :] * o

    bf = jax.ShapeDtypeStruct((s, D_MODEL), BF16)
    return _call(
        body, name="post_attn", grid=(s // tm,),
        in_specs=[_row_spec(tm, BRANCH_W), _row_spec(tm, BRANCH_W), _row_spec(tm, 2 * D_MODEL), _row_spec(tm, D_MODEL),
                  _full_spec(mod6.shape), _full_spec(wbr_s.shape), _full_spec(w_out.shape)],
        out_specs=[_row_spec(tm, D_MODEL)] * 5,
        out_shape=[bf, bf, bf, bf, jax.ShapeDtypeStruct((s, D_MODEL), F32)],
        compiler_params=_params(("parallel",)),
    )(ya, yb, gates, x, mod6, wbr_s, w_out)


def _mlp_in(x1, mod6, g2, w_mi_s, tm=256):
    s = x1.shape[0]

    def body(x_ref, mod_ref, g_ref, w_ref, h2_ref, a_ref, hid_ref):
        xt = x_ref[...]
        r = lax.rsqrt(jnp.mean(xt * xt, axis=-1, keepdims=True) + NORM_EPS)
        h2 = ((xt * r * g_ref[...]) * (1.0 + mod_ref[4:5, :]) + mod_ref[3:4, :]).astype(BF16)
        h2_ref[...] = h2
        a = jnp.concatenate([_nn(h2, w_ref[j]) for j in range(N_SHARD)], axis=1)
        a_ref[...] = a.astype(BF16)
        hid_ref[...] = jnp.square(jnp.maximum(a, 0.0)).astype(BF16)

    return _call(
        body, name="mlp_in", grid=(s // tm,),
        in_specs=[_row_spec(tm, D_MODEL), _full_spec(mod6.shape), _full_spec(g2.shape), _full_spec(w_mi_s.shape)],
        out_specs=[_row_spec(tm, D_MODEL), _row_spec(tm, D_FF), _row_spec(tm, D_FF)],
        out_shape=[jax.ShapeDtypeStruct((s, D_MODEL), BF16), jax.ShapeDtypeStruct((s, D_FF), BF16),
                   jax.ShapeDtypeStruct((s, D_FF), BF16)],
        compiler_params=_params(("parallel",)),
    )(x1, mod6, g2, w_mi_s)


ACC_ROWS = 8


def _acc_spec():
    return pl.BlockSpec((ACC_ROWS, D_MODEL), lambda i: (0, 0))


def _acc_add(acc_ref, rows):
    @pl.when(pl.program_id(0) == 0)
    def _():
        acc_ref[...] = jnp.zeros_like(acc_ref)

    for r, val in enumerate(rows):
        acc_ref[r:r + 1, :] += jnp.sum(val, axis=0, keepdims=True)


def _mlp_out_loss(hid, x1, a, target, mod6, gf, w_mo, tm=256):
    s = x1.shape[0]

    def body(hid_ref, x_ref, a_ref, t_ref, mod_ref, gf_ref, w_ref, dx2_ref, dm_ref, da_ref, acc_ref):
        m = _nn(hid_ref[...], w_ref[...])
        gate2 = mod_ref[5:6, :]
        x2 = x_ref[...] + gate2 * m
        r = lax.rsqrt(jnp.mean(x2 * x2, axis=-1, keepdims=True) + NORM_EPS)
        xn = x2 * r
        err = xn * gf_ref[...] - t_ref[...]
        dy = err * (1.0 / D_MODEL)
        dxn = dy * gf_ref[...]
        dx2 = r * (dxn - xn * jnp.mean(dxn * xn, axis=-1, keepdims=True))
        dx2_ref[...] = dx2
        dm = (dx2 * gate2).astype(BF16)
        dm_ref[...] = dm
        da_ref[...] = (_nt(dm, w_ref[...]) * (2.0 * jnp.maximum(a_ref[...].astype(F32), 0.0))).astype(BF16)
        _acc_add(acc_ref, [err * err, dy * xn, dx2 * m])

    return _call(
        body, name="mlp_out_loss", grid=(s // tm,),
        in_specs=[_row_spec(tm, D_FF), _row_spec(tm, D_MODEL), _row_spec(tm, D_FF), _row_spec(tm, D_MODEL),
                  _full_spec(mod6.shape), _full_spec(gf.shape), _full_spec(w_mo.shape)],
        out_specs=[_row_spec(tm, D_MODEL), _row_spec(tm, D_MODEL), _row_spec(tm, D_FF), _acc_spec()],
        out_shape=[jax.ShapeDtypeStruct((s, D_MODEL), F32), jax.ShapeDtypeStruct((s, D_MODEL), BF16),
                   jax.ShapeDtypeStruct((s, D_FF), BF16), jax.ShapeDtypeStruct((ACC_ROWS, D_MODEL), F32)],
        compiler_params=_params(("arbitrary",)),
    )(hid, x1, a, target, mod6, gf, w_mo)


def _norm_bwd(dh, xt, gain, scale):
    r = lax.rsqrt(jnp.mean(xt * xt, axis=-1, keepdims=True) + NORM_EPS)
    xn = xt * r
    dxn = dh * (gain * (1.0 + scale))
    dx = r * (dxn - xn * jnp.mean(dxn * xn, axis=-1, keepdims=True))
    return dx, [dh, dh * xn * gain, dh * xn * (1.0 + scale)]


def _mlp_bwd(da, x1, dx2, o, mod6, g2, w_mi_s, rider=None, tm=256):
    s = x1.shape[0]

    def body(da_ref, x_ref, dx2_ref, o_ref, mod_ref, g_ref, w_ref, dx1_ref, do_ref, acc_ref):
        dh2 = _nt(da_ref[:, :D_MODEL], w_ref[0])
        for j in range(1, N_SHARD):
            dh2 += _nt(da_ref[:, D_MODEL * j:D_MODEL * (j + 1)], w_ref[j])
        dx, sums = _norm_bwd(dh2, x_ref[...], g_ref[...], mod_ref[4:5, :])
        dx1 = dx2_ref[...] + dx
        dx1_ref[...] = dx1
        do_ref[...] = (dx1 * mod_ref[2:3, :]).astype(BF16)
        _acc_add(acc_ref, sums + [dx1 * o_ref[...].astype(F32)])

    return _hosted(
        body, rider, name="mlp_bwd", grid=(s // tm,),
        in_specs=[_row_spec(tm, D_FF), _row_spec(tm, D_MODEL), _row_spec(tm, D_MODEL), _row_spec(tm, D_MODEL),
                  _full_spec(mod6.shape), _full_spec(g2.shape), _full_spec(w_mi_s.shape)],
        out_specs=[_row_spec(tm, D_MODEL), _row_spec(tm, D_MODEL), _acc_spec()],
        out_shape=[jax.ShapeDtypeStruct((s, D_MODEL), F32), jax.ShapeDtypeStruct((s, D_MODEL), BF16),
                   jax.ShapeDtypeStruct((ACC_ROWS, D_MODEL), F32)],
        args=(da, x1, dx2, o, mod6, g2, w_mi_s))


def _merge_bwd(do, gates, ua, ub, w_out, wbr_s, rider=None, tm=256):
    s = do.shape[0]

    def body(do_ref, g_ref, ua_ref, ub_ref, wo_ref, wbr_ref, dua_ref, dub_ref, dg_ref, doa_ref, dob_ref):
        dmerged = _nt(do_ref[...], wo_ref[...])
        for b, (u_ref, du_ref, dy_ref) in enumerate(((ua_ref, dua_ref, doa_ref), (ub_ref, dub_ref, dob_ref))):
            sg = _sigmoid(g_ref[:, D_MODEL * b:D_MODEL * (b + 1)].astype(F32))
            du = (dmerged * sg).astype(BF16)
            du_ref[...] = du
            dg_ref[:, D_MODEL * b:D_MODEL * (b + 1)] = (dmerged * u_ref[...].astype(F32) * sg * (1.0 - sg)).astype(BF16)
            w = BRANCH_W // 2
            dy = _nt(du[:, :w], wbr_ref[0, b])
            for j in range(1, N_SHARD):
                dy += _nt(du[:, w * j:w * (j + 1)], wbr_ref[j, b])
            dyb = dy.astype(BF16)
            for h in range(Q_HEADS):
                dy_ref[h] = dyb[:, HEAD_DIM * h:HEAD_DIM * (h + 1)]

    bf = jax.ShapeDtypeStruct((s, D_MODEL), BF16)
    heads = jax.ShapeDtypeStruct((Q_HEADS, s, HEAD_DIM), BF16)
    return _hosted(
        body, rider, name="merge_bwd", grid=(s // tm,),
        in_specs=[_row_spec(tm, D_MODEL), _row_spec(tm, 2 * D_MODEL), _row_spec(tm, D_MODEL), _row_spec(tm, D_MODEL),
                  _full_spec(w_out.shape), _full_spec(wbr_s.shape)],
        out_specs=[_row_spec(tm, D_MODEL), _row_spec(tm, D_MODEL), _row_spec(tm, 2 * D_MODEL),
                   _heads_spec(Q_HEADS, tm), _heads_spec(Q_HEADS, tm)],
        out_shape=[bf, bf, jax.ShapeDtypeStruct((s, 2 * D_MODEL), BF16), heads, heads],
        args=(do, gates, ua, ub, w_out, wbr_s))


def _qk_bwd(dqa, dka, dva, dqb, dkb, dvb, qkraw, dgates, gq, gk, bd, tabs, rider=None, tm=256):
    s = qkraw.shape[0]

    def body(dqa_ref, dka_ref, dva_ref, dqb_ref, dkb_ref, dvb_ref, raw_ref, dg_ref, gq_ref, gk_ref, bd_ref,
             ca, la, ha, cb, lb, hb, dp_ref, acc_ref, pair_ref):
        bdm = bd_ref[...]
        tab_a = (ca[...], la[...], ha[...])
        tab_b = (cb[...], lb[...], hb[...])

        def pair(ref, first):
            pair_ref[:, :HEAD_DIM] = ref[first].astype(F32)
            pair_ref[:, HEAD_DIM:] = ref[first + 1].astype(F32)
            return pair_ref[...]

        def norm_rope_a_bwd(dz, raw, gain):
            dzn = _rope(dz, *tab_a, 16, sign=-1.0)
            rinv = lax.rsqrt(_head_mean(raw * raw, bdm) + NORM_EPS)
            zhat = raw * rinv
            dzhat = dzn * gain
            return rinv * (dzhat - zhat * _head_mean(dzhat * zhat, bdm)), dzn * zhat

        gq_rows = jnp.zeros((tm, LANES), F32)
        for i in range(Q_HEADS // 2):
            at = slice(LANES * i, LANES * (i + 1))
            draw, gsum = norm_rope_a_bwd(pair(dqa_ref, 2 * i) * Q_SCALE, raw_ref[:, at], gq_ref[...])
            dp_ref[:, at] = draw.astype(BF16)
            gq_rows += gsum
        off = BRANCH_W
        draw, gk_rows = norm_rope_a_bwd(pair(dka_ref, 0), raw_ref[:, off:off + LANES], gk_ref[...])
        dp_ref[:, off:off + LANES] = draw.astype(BF16)
        off += KV_W
        dp_ref[:, off:off + LANES] = pair(dva_ref, 0).astype(BF16)
        off += KV_W
        for i in range(Q_HEADS // 2):
            dz = _rope(pair(dqb_ref, 2 * i) * Q_SCALE, *tab_b, 32, sign=-1.0)
            dp_ref[:, off + LANES * i:off + LANES * (i + 1)] = dz.astype(BF16)
        off += BRANCH_W
        dp_ref[:, off:off + LANES] = _rope(pair(dkb_ref, 0), *tab_b, 32, sign=-1.0).astype(BF16)
        off += KV_W
        dp_ref[:, off:off + LANES] = pair(dvb_ref, 0).astype(BF16)
        dp_ref[:, QK_W:] = dg_ref[...]

        @pl.when(pl.program_id(0) == 0)
        def _():
            acc_ref[...] = jnp.zeros_like(acc_ref)

        acc_ref[0:1, :] += jnp.sum(gq_rows, axis=0, keepdims=True)
        acc_ref[1:2, :] += jnp.sum(gk_rows, axis=0, keepdims=True)

    tab_spec = _row_spec(tm, LANES)
    return _hosted(
        body, rider, name="qk_bwd", grid=(s // tm,),
        in_specs=[_heads_spec(Q_HEADS, tm), _heads_spec(KV_HEADS, tm), _heads_spec(KV_HEADS, tm),
                  _heads_spec(Q_HEADS, tm), _heads_spec(KV_HEADS, tm), _heads_spec(KV_HEADS, tm),
                  _row_spec(tm, BRANCH_W + KV_W), _row_spec(tm, 2 * D_MODEL),
                  _full_spec(gq.shape), _full_spec(gk.shape), _full_spec(bd.shape)] + [tab_spec] * 6,
        out_specs=[_row_spec(tm, IN_W), pl.BlockSpec((ACC_ROWS, LANES), lambda i: (0, 0))],
        out_shape=[jax.ShapeDtypeStruct((s, IN_W), BF16), jax.ShapeDtypeStruct((ACC_ROWS, LANES), F32)],
        scratch_shapes=[pltpu.VMEM((tm, LANES), F32)],
        args=(dqa, dka, dva, dqb, dkb, dvb, qkraw, dgates, gq, gk, bd, *tabs))


def _in_proj_bwd(dproj, x, dx1, mod6, g1, w_in_s, tm=256):
    s = x.shape[0]
    w = IN_W // N_SHARD

    def body(dp_ref, x_ref, dx1_ref, mod_ref, g_ref, w_ref, gx_ref, acc_ref):
        dh = _nt(dp_ref[:, :w], w_ref[0])
        for j in range(1, N_SHARD):
            dh += _nt(dp_ref[:, w * j:w * (j + 1)], w_ref[j])
        dx, sums = _norm_bwd(dh, x_ref[...], g_ref[...], mod_ref[1:2, :])
        gx_ref[...] = dx1_ref[...] + dx
        _acc_add(acc_ref, sums)

    return _call(
        body, name="in_proj_bwd", grid=(s // tm,),
        in_specs=[_row_spec(tm, IN_W), _row_spec(tm, D_MODEL), _row_spec(tm, D_MODEL),
                  _full_spec(mod6.shape), _full_spec(g1.shape), _full_spec(w_in_s.shape)],
        out_specs=[_row_spec(tm, D_MODEL), _acc_spec()],
        out_shape=[jax.ShapeDtypeStruct((s, D_MODEL), F32), jax.ShapeDtypeStruct((ACC_ROWS, D_MODEL), F32)],
        compiler_params=_params(("arbitrary",)),
    )(dproj, x, dx1, mod6, g1, w_in_s)


def _wgrad(name, a, b, out_shape, out_spec, tm, tn, tk=4096):
    s, m = a.shape
    n = b.shape[1]
    tk = min(tk, s)
    nk = s // tk

    def body(a_ref, b_ref, o_ref, acc_ref):
        k = pl.program_id(2)

        @pl.when(k == 0)
        def _():
            acc_ref[...] = jnp.zeros_like(acc_ref)

        acc_ref[...] += _tn(a_ref[...], b_ref[...])

        @pl.when(k == nk - 1)
        def _():
            o_ref[...] = acc_ref[...].reshape(o_ref.shape)

    return _call(
        body, name=name, grid=(m // tm, n // tn, nk),
        in_specs=[pl.BlockSpec((tk, tm), lambda i, j, k: (k, i)), pl.BlockSpec((tk, tn), lambda i, j, k: (k, j))],
        out_specs=out_spec, out_shape=jax.ShapeDtypeStruct(out_shape, F32),
        scratch_shapes=[pltpu.VMEM((tm, tn), F32)],
        compiler_params=_params(("parallel", "parallel", "arbitrary")),
    )(a, b)


def _wgrad_branch(ya, yb, dua, dub, tk=2048):
    s = ya.shape[0]
    tk = min(tk, s)
    nk = s // tk
    w = D_MODEL // N_SHARD

    def body(ya_ref, yb_ref, dua_ref, dub_ref, o_ref, acc_ref):
        b, k = pl.program_id(0), pl.program_id(1)

        @pl.when(k == 0)
        def _():
            acc_ref[...] = jnp.zeros_like(acc_ref)

        @pl.when(b == 0)
        def _():
            acc_ref[...] += _tn(ya_ref[...], dua_ref[...])

        @pl.when(b == 1)
        def _():
            acc_ref[...] += _tn(yb_ref[...], dub_ref[...])

        @pl.when(k == nk - 1)
        def _():
            for j in range(N_SHARD):
                o_ref[j] = acc_ref[:, w * j:w * (j + 1)]

    first = lambda width: pl.BlockSpec((tk, width), lambda b, k: (k * (1 - b), 0))
    second = lambda width: pl.BlockSpec((tk, width), lambda b, k: (k * b, 0))
    return _call(
        body, name="wgrad_branch", grid=(2, nk),
        in_specs=[first(BRANCH_W), second(BRANCH_W), first(D_MODEL), second(D_MODEL)],
        out_specs=pl.BlockSpec((N_SHARD, None, BRANCH_W, w), lambda b, k: (0, b, 0, 0)),
        out_shape=jax.ShapeDtypeStruct((N_SHARD, 2, BRANCH_W, w), F32),
        scratch_shapes=[pltpu.VMEM((BRANCH_W, D_MODEL), F32)],
        compiler_params=_params(("parallel", "arbitrary")),
    )(ya, yb, dua, dub)


def _local_step(x, target, mod6, g1, g2, gf, gq2, gk2, sink, w_in_s, rest, cj=None):
    s = x.shape[0]
    dist = cj is not None
    tabs = _rope_tables(s)
    bd = _block_diag()
    sink_col = jnp.repeat(sink.reshape(KV_HEADS, GROUP, 1), TQ_B, axis=1).reshape(KV_HEADS, GROUP * TQ_B, 1)
    shard = D_MODEL // N_SHARD

    h, qkraw, qa, ka, va, qb, kb, vb, gates = _in_proj(x, mod6, g1, w_in_s, gq2, gk2, bd, tabs)
    bias = _window_bias(TQ_B)
    (yb, yb_heads, lse_b), _ = _attn_b_fwd(qb, kb, vb, sink_col, bias)
    (ya, ya_heads, lse_a), gathered = _attn_a_fwd(qa, ka, va, rider=_gather_rider(rest) if dist else None)
    wbr_s, w_out, w_mi_s, w_mo = gathered if dist else rest
    wbr_s = wbr_s.reshape(N_SHARD, 2, BRANCH_W, shard)
    w_out = w_out.reshape(D_MODEL, D_MODEL)
    w_mo = w_mo.reshape(D_FF, D_MODEL)
    ua, ub, merged, o, x1 = _post_attn(ya, yb, gates, x, mod6, wbr_s, w_out)
    h2, a, hid = _mlp_in(x1, mod6, g2, w_mi_s)
    dx2, dm, da, acc_out = _mlp_out_loss(hid, x1, a, target, mod6, gf, w_mo)

    g_w_mo = _wgrad("wgrad_mlp_out", hid, dm, (D_FF, D_MODEL), pl.BlockSpec((D_MODEL, D_MODEL), lambda i, j, k: (i, 0)),
                    D_MODEL, D_MODEL).reshape(N_SHARD, D_MODEL, D_MODEL)
    g_w_mi = _wgrad("wgrad_mlp_in", h2, da, (N_SHARD, D_MODEL, D_MODEL),
                    pl.BlockSpec((None, D_MODEL, D_MODEL), lambda i, j, k: (j, i, 0)), D_MODEL, D_MODEL)
    mlp = _Reduction(("mlp_out", "mlp_in"), (g_w_mo, g_w_mi), cj)
    (dx1, do, acc_mlp), got = _mlp_bwd(da, x1, dx2, o, mod6, g2, w_mi_s, rider=mlp.swap() if dist else None)
    (dua, dub, dgates, doa, dob), landed = _merge_bwd(do, gates, ua, ub, w_out, wbr_s, rider=mlp.add(got) if dist else None)
    g_w_out = _wgrad("wgrad_out", merged, do, (D_MODEL, D_MODEL), pl.BlockSpec((D_MODEL, D_MODEL), lambda i, j, k: (i, 0)),
                     D_MODEL, D_MODEL).reshape(N_SHARD, shard, D_MODEL)
    g_wbr = _wgrad_branch(ya, yb, dua, dub)
    out = _Reduction(("out", "branch"), (g_w_out, g_wbr.reshape(N_SHARD, 2 * BRANCH_W, shard)), cj)
    (dqa, dka, dva), landings = _attn_a_bwd(qa, ka, va, ya_heads, doa, lse_a,
                                            rider=_riders(out.swap(), mlp.total(landed)) if dist else None)
    (dqb, dkb, dvb, dsink), _ = _attn_b_bwd(qb, kb, vb, yb_heads, dob, lse_b, sink_col, bias)
    heads = (Q_HEADS, s, HEAD_DIM)
    (dproj, acc_qk), landed = _qk_bwd(dqa.reshape(heads), dka, dva, dqb.reshape(heads), dkb, dvb, qkraw, dgates, gq2, gk2, bd,
                                      tabs, rider=out.add(landings[:2]) if dist else None)
    w = IN_W // N_SHARD
    g_w_in = _wgrad("wgrad_in", h, dproj, (N_SHARD, D_MODEL, w), pl.BlockSpec((None, D_MODEL, w), lambda i, j, k: (j, i, 0)),
                    D_MODEL, w)
    grad_x, acc_in = _in_proj_bwd(dproj, x, dx1, mod6, g1, w_in_s)
    accs = (acc_out, acc_mlp, acc_in, acc_qk, dsink)
    if not dist:
        return grad_x, (g_w_in, g_wbr, g_w_out, g_w_mi, g_w_mo), accs
    r_mo, r_mi = landings[2:]
    return grad_x, (_Reduction(("in",), (g_w_in,), cj), out.total(landed), r_mi, r_mo), accs


def _me():
    return lax.axis_index("x"), lax.axis_index("y"), lax.axis_index("c")


def _peer(d):
    x, y, c = _me()
    return (1 - x if d & 4 else x, 1 - y if d & 2 else y, 1 - c if d & 1 else c)


def _dev_index(p):
    return 4 * p[0] + 2 * p[1] + p[2]


def _chip_index(p):
    return 2 * p[0] + p[1]


def _remote(src, dst, send_sem, recv_sem, to):
    return pltpu.make_async_remote_copy(src_ref=src, dst_ref=dst, send_sem=send_sem, recv_sem=recv_sem,
                                        device_id=to, device_id_type=MESH)


SLOT_ROWS = 8


def _ada_fwd(c, w_ada, b4, rider=None):
    cols = w_ada.shape[1]

    def body(c_ref, w_ref, b_ref, mod_ref, sc_ref, cbuf, pbuf, mbuf, send1, recv1, send2, recv2, launch=None):
        me = _me()
        mine, chip = _dev_index(me), _chip_index(me)
        cbuf[mine] = jnp.broadcast_to(c_ref[...], (SLOT_ROWS, D_MODEL))
        gather = [_remote(cbuf.at[mine], cbuf.at[mine], send1.at[d - 1], recv1.at[d - 1], _peer(d)) for d in range(1, N_DEV)]
        for cp in gather:
            cp.start()
        if launch is not None:
            launch()
        for d in range(1, N_DEV):
            _remote(cbuf.at[mine], cbuf.at[_dev_index(_peer(d))], send1.at[d - 1], recv1.at[d - 1], _peer(d)).wait_recv()
        call = cbuf[...].reshape(N_DEV * SLOT_ROWS, D_MODEL)
        sc = call * _sigmoid(call)
        for s in range(N_DEV):
            sc_ref[s:s + 1, :] = sc[SLOT_ROWS * s:SLOT_ROWS * s + 1]
        part = _nn(sc.astype(BF16), w_ref[...].astype(BF16)) + b_ref[pl.ds(chip, 1), :]
        pbuf[...] = part.reshape(N_DEV, SLOT_ROWS, cols)
        mbuf[chip] = pbuf[mine]
        spread = [_remote(pbuf.at[_dev_index(_peer(d))], mbuf.at[chip], send2.at[d // 2 - 1], recv2.at[d // 2 - 1], _peer(d))
                  for d in (2, 4, 6)]
        for cp in spread:
            cp.start()
        for d in (2, 4, 6):
            _remote(pbuf.at[mine], mbuf.at[_chip_index(_peer(d))], send2.at[d // 2 - 1], recv2.at[d // 2 - 1],
                    _peer(d)).wait_recv()
        half = D_MODEL // 2
        for p in range(2 * 6):
            col = half * p
            mod_ref[p // 2:p // 2 + 1, half * (p % 2):half * (p % 2 + 1)] = mbuf[col // cols, 0:1, col % cols:col % cols + half]
        for cp in gather + spread:
            cp.wait_send()

    vm = pl.BlockSpec(memory_space=pltpu.VMEM)
    return _hosted(
        body, rider, name="ada_fwd", grid=(), in_specs=[vm, vm, vm], out_specs=[vm, vm],
        out_shape=[jax.ShapeDtypeStruct((6, D_MODEL), F32), jax.ShapeDtypeStruct((N_DEV, D_MODEL), F32)],
        scratch_shapes=[pltpu.VMEM((N_DEV, SLOT_ROWS, D_MODEL), F32), pltpu.VMEM((N_DEV, SLOT_ROWS, cols), F32),
                        pltpu.VMEM((N_SHARD, SLOT_ROWS, cols), F32),
                        pltpu.SemaphoreType.DMA((N_DEV - 1,)), pltpu.SemaphoreType.DMA((N_DEV - 1,)),
                        pltpu.SemaphoreType.DMA((N_SHARD - 1,)), pltpu.SemaphoreType.DMA((N_SHARD - 1,))],
        args=(c, w_ada, b4))


PACK_ROWS = 16
PACK_W = 3 * D_MODEL


def _ada_bwd(acc_out, acc_mlp, acc_in, acc_qk, dsink, sc_all, rider=None):
    cols = 6 * D_MODEL // N_SHARD

    def body(out_ref, mlp_ref, in_ref, qk_ref, dsink_ref, sc_ref,
             gwa_ref, gba_ref, gn1_ref, gn2_ref, gf_ref, gq_ref, gk_ref, gs_ref, loss_ref, blk, send, recv, launch=None):
        me = _me()
        mine, chip = _dev_index(me), _chip_index(me)
        blk[mine] = jnp.zeros((PACK_ROWS, PACK_W), F32)
        dmod = (in_ref, 0), (in_ref, 1), (mlp_ref, 3), (mlp_ref, 0), (mlp_ref, 1), (out_ref, 2)
        half = D_MODEL // 2
        for p in range(2 * 6):
            ref, row = dmod[p // 2]
            col = half * p
            blk[mine, col // cols:col // cols + 1, col % cols:col % cols + half] = ref[row:row + 1, half * (p % 2):half * (p % 2 + 1)]
        blk[mine, 4:5, 0:D_MODEL] = in_ref[2:3, :]
        blk[mine, 4:5, D_MODEL:2 * D_MODEL] = mlp_ref[2:3, :]
        blk[mine, 4:5, 2 * D_MODEL:] = out_ref[1:2, :]
        blk[mine, 5:6, 0:LANES] = qk_ref[0:1, :]
        blk[mine, 5:6, LANES:2 * LANES] = qk_ref[1:2, :]
        blk[mine, 6:7, 0:D_MODEL] = out_ref[0:1, :]
        for g in range(KV_HEADS):
            blk[mine, 8 + GROUP * g:8 + GROUP * (g + 1), 0:LANES] = dsink_ref[g, 0:GROUP, :]
        copies = [_remote(blk.at[mine], blk.at[mine], send.at[d - 1], recv.at[d - 1], _peer(d)) for d in range(1, N_DEV)]
        for cp in copies:
            cp.start()
        if launch is not None:
            launch()
        for d in range(1, N_DEV):
            _remote(blk.at[mine], blk.at[_dev_index(_peer(d))], send.at[d - 1], recv.at[d - 1], _peer(d)).wait_recv()
        tot = blk[0]
        for s in range(1, N_DEV):
            tot = tot + blk[s]
        for j in range(N_SHARD):
            gba_ref[:, cols * j:cols * (j + 1)] = tot[j:j + 1, :cols]
        gn1_ref[...] = tot[4:5, 0:D_MODEL]
        gn2_ref[...] = tot[4:5, D_MODEL:2 * D_MODEL]
        gf_ref[...] = tot[4:5, 2 * D_MODEL:]
        gq_ref[...] = tot[5:6, 0:HEAD_DIM] + tot[5:6, HEAD_DIM:2 * HEAD_DIM]
        gk_ref[...] = tot[5:6, LANES:LANES + HEAD_DIM] + tot[5:6, LANES + HEAD_DIM:2 * LANES]
        sq = tot[8:16, 0:Q_HEADS]
        diag = lax.broadcasted_iota(jnp.int32, sq.shape, 0) == lax.broadcasted_iota(jnp.int32, sq.shape, 1)
        gs_ref[...] = jnp.sum(jnp.where(diag, sq, 0.0), axis=0, keepdims=True)
        half_mse = (0.5 / D_MODEL) * jnp.sum(tot[6:7, 0:D_MODEL], axis=-1, keepdims=True)
        loss_ref[...] = jnp.broadcast_to(half_mse, (1, LANES))
        dm = jnp.concatenate([blk[s, pl.ds(chip, 1), pl.ds(0, cols)] for s in range(N_DEV)], axis=0)
        gwa_ref[...] = _tn(sc_ref[...], dm)
        for cp in copies:
            cp.wait_send()

    vm = pl.BlockSpec(memory_space=pltpu.VMEM)
    row = lambda n: jax.ShapeDtypeStruct((1, n), F32)
    return _hosted(
        body, rider, name="ada_bwd", grid=(), in_specs=[vm] * 6, out_specs=[vm] * 9,
        out_shape=[jax.ShapeDtypeStruct((D_MODEL, cols), F32), row(6 * D_MODEL), row(D_MODEL), row(D_MODEL), row(D_MODEL),
                   row(HEAD_DIM), row(HEAD_DIM), row(Q_HEADS), row(LANES)],
        scratch_shapes=[pltpu.VMEM((N_DEV, PACK_ROWS, PACK_W), F32),
                        pltpu.SemaphoreType.DMA((N_DEV - 1,)), pltpu.SemaphoreType.DMA((N_DEV - 1,))],
        args=(acc_out, acc_mlp, acc_in, acc_qk, dsink, sc_all))


def _cast_weights(ws):
    n = len(ws)

    def body(*refs):
        src, out, tmp, sems = refs[:n], refs[n:2 * n], refs[2 * n:3 * n], refs[3 * n]
        chip = _chip_index(_me())
        copies = []
        for a in range(n):
            tmp[a][...] = src[a][...].astype(BF16)
            cp = pltpu.make_async_copy(tmp[a], out[a].at[chip], sems.at[a])
            cp.start()
            copies.append(cp)
        for cp in copies:
            cp.wait()

    vm = pl.BlockSpec(memory_space=pltpu.VMEM)
    return _call(
        body, name="cast_weights", in_specs=[vm] * n, out_specs=[ANY] * n,
        out_shape=[jax.ShapeDtypeStruct((N_SHARD,) + w.shape, BF16) for w in ws],
        scratch_shapes=[pltpu.VMEM(w.shape, BF16) for w in ws] + [pltpu.SemaphoreType.DMA((n,))],
        compiler_params=_params(),
    )(*ws)


def _half_rows(ref_rows, c):
    half = ref_rows // 2
    return pl.ds(pl.multiple_of(c * half, 8), half)


class _Rider:
    def __init__(self, inputs, out_shape, aliases, n_sems, start, finish, middle=None):
        self.inputs, self.out_shape, self.aliases, self.n_sems = list(inputs), list(out_shape), dict(aliases), n_sems
        self.start, self.finish, self.middle = start, finish, middle


def _riders(*rs):
    ins = [0]
    outs = [0]
    sems = [0]
    for r in rs:
        ins.append(ins[-1] + len(r.inputs))
        outs.append(outs[-1] + len(r.out_shape))
        sems.append(sems[-1] + r.n_sems)

    def phase(which):
        def run(in_refs, out_refs, sem):
            for k, r in enumerate(rs):
                fn = getattr(r, which)
                if fn is not None:
                    fn(in_refs[ins[k]:ins[k + 1]], out_refs[outs[k]:outs[k + 1]], lambda j, base=sems[k]: sem(base + j))
        return run

    aliases = {ins[k] + i: outs[k] + o for k, r in enumerate(rs) for i, o in r.aliases.items()}
    return _Rider([a for r in rs for a in r.inputs], [o for r in rs for o in r.out_shape], aliases, sems[-1],
                  phase("start"), phase("finish"), phase("middle") if any(r.middle for r in rs) else None)


def _hosted(body, rider, *, name, grid, in_specs, out_specs, out_shape, args, scratch_shapes=(), middle_at=None):
    where = dict(grid=grid, compiler_params=_params(("arbitrary",) * len(grid))) if grid else dict(compiler_params=_params())
    if rider is None:
        res = _call(body, name=name, in_specs=in_specs, out_specs=out_specs, out_shape=out_shape,
                    scratch_shapes=list(scratch_shapes), **where)(*args)
        return res, ()
    n_in, n_out, n_scr = len(in_specs), len(out_specs), len(scratch_shapes)
    r_in, r_out = len(rider.inputs), len(rider.out_shape)

    def riding(*refs):
        at = 0
        parts = []
        for size in (n_in, r_in, n_out, r_out, n_scr):
            parts.append(refs[at:at + size])
            at += size
        ins, rider_ins, outs, rider_outs, scratch = parts
        sems = refs[at]

        def sem_at(k):
            return sems.at[k]

        if not grid:
            body(*ins, *outs, *scratch, launch=lambda: rider.start(rider_ins, rider_outs, sem_at))
            if rider.middle is not None:
                rider.middle(rider_ins, rider_outs, sem_at)
            rider.finish(rider_ins, rider_outs, sem_at)
            return
        step = pl.program_id(0)
        for axis in range(1, len(grid)):
            step = step * grid[axis] + pl.program_id(axis)
        steps = 1
        for size in grid:
            steps *= size

        @pl.when(step == 0)
        def _():
            rider.start(rider_ins, rider_outs, sem_at)

        body(*ins, *outs, *scratch)
        if rider.middle is not None:
            @pl.when(step == middle_at)
            def _():
                rider.middle(rider_ins, rider_outs, sem_at)

        @pl.when(step == steps - 1)
        def _():
            rider.finish(rider_ins, rider_outs, sem_at)

    res = _call(
        riding, name=name, in_specs=list(in_specs) + [ANY] * r_in, out_specs=list(out_specs) + [ANY] * r_out,
        out_shape=list(out_shape) + rider.out_shape,
        input_output_aliases={n_in + i: n_out + o for i, o in rider.aliases.items()},
        scratch_shapes=list(scratch_shapes) + [pltpu.SemaphoreType.DMA((rider.n_sems,))], **where,
    )(*args, *rider.inputs)
    return res[:n_out], res[n_out:]


def _alone(name, rider):
    n_in, n_out = len(rider.inputs), len(rider.out_shape)

    def body(*refs):
        ins, outs, sems = refs[:n_in], refs[n_in:n_in + n_out], refs[n_in + n_out]

        def sem_at(k):
            return sems.at[k]

        rider.start(ins, outs, sem_at)
        if rider.middle is not None:
            rider.middle(ins, outs, sem_at)
        rider.finish(ins, outs, sem_at)

    return _call(
        body, name=name, in_specs=[ANY] * n_in, out_specs=[ANY] * n_out, out_shape=rider.out_shape,
        input_output_aliases=rider.aliases, scratch_shapes=[pltpu.SemaphoreType.DMA((rider.n_sems,))],
    )(*rider.inputs)


OTHER_CHIPS = (2, 4, 6)


def _gather_rider(stacked):
    n = len(stacked)

    def flights(bufs, sem):
        me = _me()
        chip, sib = _chip_index(me), _peer(1)
        out = []
        for a in range(n):
            mine, theirs = (_half_rows(bufs[a].shape[1], c) for c in (me[2], 1 - me[2]))
            for j, d in enumerate(OTHER_CHIPS):
                k = 3 * a + j
                from_chip = _chip_index(_peer(d))
                own, landed, passed = bufs[a].at[chip, mine], bufs[a].at[from_chip, mine], bufs[a].at[from_chip, theirs]
                out.append((_remote(own, own, sem(k), sem(3 * n + k), _peer(d)),
                            _remote(own, landed, sem(k), sem(3 * n + k), _peer(d)),
                            _remote(landed, landed, sem(6 * n + k), sem(9 * n + k), sib),
                            _remote(passed, passed, sem(6 * n + k), sem(9 * n + k), sib)))
        return out

    def start(ins, outs, sem):
        for send, _, _, _ in flights(outs, sem):
            send.start()

    def middle(ins, outs, sem):
        for _, arrival, pass_on, _ in flights(outs, sem):
            arrival.wait_recv()
            pass_on.start()

    def finish(ins, outs, sem):
        every = flights(outs, sem)
        for _, _, _, passed_to_me in every:
            passed_to_me.wait_recv()
        for send, _, pass_on, _ in every:
            send.wait_send()
            pass_on.wait_send()

    return _Rider(stacked, [jax.ShapeDtypeStruct(w.shape, w.dtype) for w in stacked], {a: a for a in range(n)}, 12 * n,
                  start, finish, middle)


def _swap_rider(grads):
    n = len(grads)

    def copies(ins, outs, sem):
        c = _me()[2]
        return [_remote(ins[a].at[pl.ds(0, N_SHARD), _half_rows(ins[a].shape[1], 1 - c)], outs[a], sem(a), sem(n + a), _peer(1))
                for a in range(n)]

    def start(ins, outs, sem):
        for cp in copies(ins, outs, sem):
            cp.start()

    def finish(ins, outs, sem):
        for cp in copies(ins, outs, sem):
            cp.wait()

    return _Rider(grads, [jax.ShapeDtypeStruct((N_SHARD, g.shape[1] // 2, g.shape[2]), F32) for g in grads], {}, 2 * n,
                  start, finish)


def _row_tile(rows):
    return min(rows, 256)


def _add_halves(name, g, got, cj):
    _, half, cols = got.shape
    tr = _row_tile(half)
    nt = half // tr

    def body(cj_ref, g_ref, got_ref, o_ref):
        o_ref[...] = (g_ref[...] + got_ref[...]).astype(BF16)

    spec = pl.BlockSpec((None, tr, cols), lambda i, s, cj: (s, i, 0))
    return _call(
        body, name=name,
        grid_spec=pltpu.PrefetchScalarGridSpec(
            num_scalar_prefetch=1, grid=(nt, N_SHARD),
            in_specs=[pl.BlockSpec((None, tr, cols), lambda i, s, cj: (s, cj[0] * nt + i, 0)), spec], out_specs=spec),
        out_shape=jax.ShapeDtypeStruct(got.shape, BF16), compiler_params=_params(("parallel", "parallel")),
    )(cj, g, got)


def _scatter_rider(sums):
    n = len(sums)

    def flights(ins, outs, sem):
        chip = _chip_index(_me())
        out = []
        for a in range(n):
            for j, d in enumerate(OTHER_CHIPS):
                k = 3 * a + j
                other = _chip_index(_peer(d))
                out.append((_remote(ins[a].at[other], outs[a].at[chip], sem(k), sem(3 * n + k), _peer(d)),
                            _remote(ins[a].at[chip], outs[a].at[other], sem(k), sem(3 * n + k), _peer(d))))
        return out

    def start(ins, outs, sem):
        for send, _ in flights(ins, outs, sem):
            send.start()

    def finish(ins, outs, sem):
        every = flights(ins, outs, sem)
        for _, arrival in every:
            arrival.wait_recv()
        for send, _ in every:
            send.wait_send()

    return _Rider(sums, [jax.ShapeDtypeStruct(v.shape, v.dtype) for v in sums], {}, 6 * n, start, finish)


def _sum_chips(name, g, got, landed, cj):
    _, half, cols = got.shape
    tr = _row_tile(half)
    nt = half // tr

    def body(cj_ref, g_ref, got_ref, landed_ref, o_ref):
        own = g_ref[...] + got_ref[...]
        total = None
        for s in range(N_SHARD):
            term = jnp.where(cj_ref[1] == s, own, landed_ref[s].astype(F32))
            total = term if total is None else total + term
        o_ref[...] = total

    return _call(
        body, name=name,
        grid_spec=pltpu.PrefetchScalarGridSpec(
            num_scalar_prefetch=1, grid=(nt,),
            in_specs=[pl.BlockSpec((None, tr, cols), lambda i, cj: (cj[1], cj[0] * nt + i, 0)),
                      pl.BlockSpec((None, tr, cols), lambda i, cj: (cj[1], i, 0)),
                      pl.BlockSpec((N_SHARD, tr, cols), lambda i, cj: (0, i, 0))],
            out_specs=pl.BlockSpec((tr, cols), lambda i, cj: (cj[0] * nt + i, 0))),
        out_shape=jax.ShapeDtypeStruct((2 * half, cols), F32), compiler_params=_params(("parallel",)),
    )(cj, g, got, landed)


def _join_rider(shards):
    n = len(shards)

    def flights(bufs, sem):
        c = _me()[2]
        out = []
        for a in range(n):
            mine, theirs = (bufs[a].at[_half_rows(bufs[a].shape[0], cc)] for cc in (c, 1 - c))
            out.append((_remote(mine, mine, sem(a), sem(n + a), _peer(1)), _remote(theirs, theirs, sem(a), sem(n + a), _peer(1))))
        return out

    def start(ins, outs, sem):
        for send, _ in flights(outs, sem):
            send.start()

    def finish(ins, outs, sem):
        for send, arrival in flights(outs, sem):
            arrival.wait_recv()
            send.wait_send()

    return _Rider(shards, [jax.ShapeDtypeStruct(h.shape, F32) for h in shards], {a: a for a in range(n)}, 2 * n, start, finish)


class _Reduction:
    def __init__(self, names, grads, cj):
        self.names, self.grads, self.cj = names, list(grads), cj

    def swap(self):
        return _swap_rider(self.grads)

    def add(self, got):
        self.got = list(got)
        self.sums = [_add_halves("add_halves_" + nm, g, h, self.cj) for nm, g, h in zip(self.names, self.grads, self.got)]
        return _scatter_rider(self.sums)

    def total(self, landed):
        halves = [_sum_chips("sum_chips_" + nm, g, h, l, self.cj)
                  for nm, g, h, l in zip(self.names, self.grads, self.got, landed)]
        return _join_rider(halves)


def _adamw_math(w, g, m, v):
    m = ADAM_B1 * m + (1.0 - ADAM_B1) * g
    v = ADAM_B2 * v + (1.0 - ADAM_B2) * jnp.square(g)
    m_hat = m / (1.0 - ADAM_B1 ** ADAM_STEP)
    v_hat = v / (1.0 - ADAM_B2 ** ADAM_STEP)
    return -ADAM_LR * (m_hat / (jnp.sqrt(v_hat) + ADAM_EPS) + ADAM_WD * w), m, v


def _adamw(name, ws, gs, ms, vs, rider=None):
    n = len(ws)
    rows = ws[0].shape[0]
    tr = _row_tile(rows)

    def body(*refs):
        ins, outs = refs[:4 * n], refs[4 * n:]
        for a in range(n):
            w, g, m, v = (ins[k * n + a][...] for k in range(4))
            outs[a][...], outs[n + a][...], outs[2 * n + a][...] = _adamw_math(w, g, m, v)

    specs = [pl.BlockSpec((tr, w.shape[1]), lambda i: (i, 0)) for w in ws]
    res, riding = _hosted(
        body, rider, name=name, grid=(rows // tr,), in_specs=specs * 4, out_specs=specs * 3,
        out_shape=[jax.ShapeDtypeStruct(w.shape, F32) for w in ws] * 3, args=(*ws, *gs, *ms, *vs))
    return (res[:n], res[n:2 * n], res[2 * n:]), riding


def _adamw_small(ws, gs, ms, vs):
    n = len(ws)

    def body(*refs):
        ins, outs = refs[:4 * n], refs[4 * n:]
        for a in range(n):
            w, g, m, v = (ins[k * n + a][...] for k in range(4))
            outs[a][...], outs[n + a][...], outs[2 * n + a][...] = _adamw_math(w, g, m, v)

    vm = pl.BlockSpec(memory_space=pltpu.VMEM)
    res = _call(
        body, name="adamw_small", in_specs=[vm] * (4 * n), out_specs=[vm] * (3 * n),
        out_shape=[jax.ShapeDtypeStruct(w.shape, F32) for w in ws] * 3, compiler_params=_params(),
    )(*ws, *gs, *ms, *vs)
    return res[:n], res[n:2 * n], res[2 * n:]


def kernel(x, c, w_ada, b_ada, norm1_g, w_in, q_norm_a, k_norm_a, sink_b, w_branch, w_out, norm2_g, w_mlp_in, w_mlp_out, final_g, loss_target, m_w_ada, m_b_ada, m_norm1_g, m_w_in, m_q_norm_a, m_k_norm_a, m_sink_b, m_w_branch, m_w_out, m_norm2_g, m_w_mlp_in, m_w_mlp_out, m_final_g, v_w_ada, v_b_ada, v_norm1_g, v_w_in, v_q_norm_a, v_k_norm_a, v_sink_b, v_w_branch, v_w_out, v_norm2_g, v_w_mlp_in, v_w_mlp_out, v_final_g):
    xi, yi, ci = _me()
    cj = jnp.stack([ci, 2 * xi + yi]).astype(jnp.int32)
    n_cols = 6 * D_MODEL // N_SHARD

    def rows2d(a):
        return a.reshape(-1, a.shape[-1])

    big = (w_in, w_branch, w_out, w_mlp_in, w_mlp_out)
    stacked = _cast_weights([rows2d(w) for w in big])
    (mod6, sc_all), (w_in_s,) = _ada_fwd(c, w_ada[0], b_ada.reshape(N_SHARD, n_cols), rider=_gather_rider(stacked[:1]))
    rest = stacked[1:]

    gq2 = jnp.tile(q_norm_a, (1, 2))
    gk2 = jnp.tile(k_norm_a, (1, 2))
    grad_x, (w_in_red, join_out, g_mi, g_mo), accs = _local_step(
        x[0], loss_target[0], mod6, norm1_g, norm2_g, final_g.reshape(1, D_MODEL), gq2, gk2, sink_b[0], w_in_s, rest, cj)

    (g_w_ada, g_b_ada, g_n1, g_n2, g_f, g_q, g_k, g_s, loss_row), got_in = _ada_bwd(*accs, sc_all, rider=w_in_red.swap())
    loss = loss_row[0, 0]
    moments = dict(w_ada=(m_w_ada, v_w_ada), w_in=(m_w_in, v_w_in), w_branch=(m_w_branch, v_w_branch), w_out=(m_w_out, v_w_out),
                   w_mlp_in=(m_w_mlp_in, v_w_mlp_in), w_mlp_out=(m_w_mlp_out, v_w_mlp_out))
    weights = dict(w_ada=w_ada, w_in=w_in, w_branch=w_branch, w_out=w_out, w_mlp_in=w_mlp_in, w_mlp_out=w_mlp_out)

    def adamw(call, names, grads, rider=None):
        (d, m, v), riding = _adamw(call, [rows2d(weights[nm]) for nm in names], grads,
                                   [rows2d(moments[nm][0]) for nm in names], [rows2d(moments[nm][1]) for nm in names], rider)
        return {nm: (grads[k], d[k], m[k], v[k]) for k, nm in enumerate(names)}, riding

    big_res, landed_in = adamw("adamw_ada_mlp", ("w_ada", "w_mlp_in", "w_mlp_out"), [g_w_ada, g_mi, g_mo], w_in_red.add(got_in))
    g_in, g_out, g_br = _alone("join_in_out_branch", _riders(w_in_red.total(landed_in), join_out))
    big_res.update(adamw("adamw_in_branch", ("w_in", "w_branch"), [g_in, g_br])[0])

    small = ("b_ada", "norm1_g", "q_norm_a", "k_norm_a", "sink_b", "norm2_g", "final_g", "w_out")
    row = lambda a: a.reshape(1, -1)
    small_w = [row(a) for a in (b_ada, norm1_g, q_norm_a, k_norm_a, sink_b, norm2_g, final_g)] + [w_out[0]]
    small_g = [g_b_ada, g_n1, g_q, g_k, g_s, g_n2, g_f, g_out]
    small_m = [row(a) for a in (m_b_ada, m_norm1_g, m_q_norm_a, m_k_norm_a, m_sink_b, m_norm2_g, m_final_g)] + [m_w_out[0]]
    small_v = [row(a) for a in (v_b_ada, v_norm1_g, v_q_norm_a, v_k_norm_a, v_sink_b, v_norm2_g, v_final_g)] + [v_w_out[0]]
    s_d, s_m, s_v = _adamw_small(small_w, small_g, small_m, small_v)

    order = ("w_ada", "b_ada", "norm1_g", "w_in", "q_norm_a", "k_norm_a", "sink_b", "w_branch", "w_out", "norm2_g",
             "w_mlp_in", "w_mlp_out", "final_g")
    like = dict(w_ada=w_ada, b_ada=b_ada, norm1_g=norm1_g, w_in=w_in, q_norm_a=q_norm_a, k_norm_a=k_norm_a, sink_b=sink_b,
                w_branch=w_branch, w_out=w_out, norm2_g=norm2_g, w_mlp_in=w_mlp_in, w_mlp_out=w_mlp_out, final_g=final_g)
    grad, delta, new_m, new_v = {}, {}, {}, {}
    for nm, res in big_res.items():
        grad[nm], delta[nm], new_m[nm], new_v[nm] = res
    for k, nm in enumerate(small):
        grad[nm], delta[nm], new_m[nm], new_v[nm] = small_g[k], s_d[k], s_m[k], s_v[k]
    outs = [loss, grad_x[None]]
    for group in (grad, delta, new_m, new_v):
        outs += [group[nm].reshape(like[nm].shape) for nm in order]
    return tuple(outs)
```

```python
import jax
import jax.numpy as jnp
from jax import lax
from jax.experimental import pallas as pl
from jax.experimental.pallas import tpu as pltpu

F32 = jnp.float32
BF16 = jnp.bfloat16
MESH = pl.DeviceIdType.MESH
ANY = pl.BlockSpec(memory_space=pl.ANY)

D_MODEL = 1024
HEAD_DIM = 64
Q_HEADS = 8
KV_HEADS = 2
GROUP = Q_HEADS // KV_HEADS
BRANCH_W = Q_HEADS * HEAD_DIM
KV_W = KV_HEADS * HEAD_DIM
IN_W = 2 * (BRANCH_W + 2 * KV_W) + 2 * D_MODEL
QK_W = 2 * (BRANCH_W + 2 * KV_W)
D_FF = 4 * D_MODEL
GRID_W = 64
WINDOW = 128
ROPE_THETA = 10000.0
NORM_EPS = 1e-6
NEG_INF = -1e30
Q_SCALE = HEAD_DIM ** -0.5
N_SHARD = 4
N_DEV = 8
LANES = 128
VMEM_LIMIT = 56 * 1024 * 1024

ADAM_LR = 0.001
ADAM_B1 = 0.9
ADAM_B2 = 0.999
ADAM_EPS = 1e-08
ADAM_WD = 0.01
ADAM_STEP = 10

_call = pl.pallas_call


def _params(sem=None, vmem=VMEM_LIMIT):
    return pltpu.CompilerParams(dimension_semantics=sem, vmem_limit_bytes=vmem)


def _nt(a, b):
    return lax.dot_general(a, b, (((1,), (1,)), ((), ())), preferred_element_type=F32)


def _tn(a, b):
    return lax.dot_general(a, b, (((0,), (0,)), ((), ())), preferred_element_type=F32)


def _nn(a, b):
    return jnp.dot(a, b, preferred_element_type=F32)


def _sigmoid(z):
    return 0.5 * jnp.tanh(0.5 * z) + 0.5


def _rope_tables(s):
    t = jnp.arange(s, dtype=jnp.int32)
    lane = jnp.arange(LANES, dtype=jnp.int32)

    def cos_sin(pos, dim):
        inv = ROPE_THETA ** (-jnp.arange(0, dim, 2, dtype=F32) / dim)
        ang = pos.astype(F32)[:, None] * inv[None, :]
        return jnp.cos(ang), jnp.sin(ang)

    cr, sr = cos_sin(t // GRID_W, HEAD_DIM // 2)
    cc, sc = cos_sin(t % GRID_W, HEAD_DIM // 2)
    cos_a = jnp.tile(jnp.concatenate([cr, cr, cc, cc], axis=1), (1, 2))
    sin_a = jnp.tile(jnp.concatenate([sr, sr, sc, sc], axis=1), (1, 2))
    first_a = (lane % 32) < 16
    c1, s1 = cos_sin(t, HEAD_DIM)
    cos_b = jnp.tile(jnp.concatenate([c1, c1], axis=1), (1, 2))
    sin_b = jnp.tile(jnp.concatenate([s1, s1], axis=1), (1, 2))
    first_b = (lane % 64) < 32
    tabs_a = (cos_a, jnp.where(first_a, -sin_a, 0.0), jnp.where(first_a, 0.0, sin_a))
    tabs_b = (cos_b, jnp.where(first_b, -sin_b, 0.0), jnp.where(first_b, 0.0, sin_b))
    return tabs_a + tabs_b


def _rope(z, cos, s_lo, s_hi, half, sign=1.0):
    up = pltpu.roll(z, LANES - half, 1)
    dn = pltpu.roll(z, half, 1)
    return z * cos + sign * (up * s_lo + dn * s_hi)


def _head_mean(z2, bd):
    hi = z2.astype(BF16)
    lo = (z2 - hi.astype(F32)).astype(BF16)
    return _nn(hi, bd) + _nn(lo, bd)


def _block_diag():
    lane = jnp.arange(LANES)
    return jnp.where((lane[:, None] // HEAD_DIM) == (lane[None, :] // HEAD_DIM), 1.0 / HEAD_DIM, 0.0).astype(BF16)


def _row_spec(tm, width):
    return pl.BlockSpec((tm, width), lambda i: (i, 0))


def _heads_spec(heads, tm):
    return pl.BlockSpec((heads, tm, HEAD_DIM), lambda i: (0, i, 0))


def _full_spec(shape):
    nd = len(shape)
    return pl.BlockSpec(shape, lambda i: (0,) * nd)


def _in_proj(x, mod6, g1, w_in_s, gq, gk, bd, tabs, tm=256):
    s = x.shape[0]

    def body(x_ref, mod_ref, g1_ref, w_ref, gq_ref, gk_ref, bd_ref, ca, la, ha, cb, lb, hb,
             h_ref, qkraw_ref, qa_ref, ka_ref, va_ref, qb_ref, kb_ref, vb_ref, gate_ref):
        xt = x_ref[...]
        r = lax.rsqrt(jnp.mean(xt * xt, axis=-1, keepdims=True) + NORM_EPS)
        h = (xt * r * g1_ref[...]) * (1.0 + mod_ref[1:2, :]) + mod_ref[0:1, :]
        hb16 = h.astype(BF16)
        h_ref[...] = hb16
        proj = jnp.concatenate([_nn(hb16, w_ref[j]) for j in range(N_SHARD)], axis=1)
        qkraw_ref[...] = proj[:, :BRANCH_W + KV_W]
        bdm = bd_ref[...]
        tab_a = (ca[...], la[...], ha[...])
        tab_b = (cb[...], lb[...], hb[...])

        def norm_rope_a(z, gain):
            zn = z * lax.rsqrt(_head_mean(z * z, bdm) + NORM_EPS) * gain
            return _rope(zn, *tab_a, 16)

        def put(ref, first, z):
            zb = z.astype(BF16)
            ref[first] = zb[:, :HEAD_DIM]
            ref[first + 1] = zb[:, HEAD_DIM:]

        for i in range(Q_HEADS // 2):
            put(qa_ref, 2 * i, norm_rope_a(proj[:, LANES * i:LANES * (i + 1)], gq_ref[...]) * Q_SCALE)
        off = BRANCH_W
        put(ka_ref, 0, norm_rope_a(proj[:, off:off + LANES], gk_ref[...]))
        off += KV_W
        def put_v(ref, z):
            zb = z.astype(BF16)
            for hd in range(KV_HEADS):
                ref[hd, :, :HEAD_DIM] = zb[:, HEAD_DIM * hd:HEAD_DIM * (hd + 1)]
                ref[hd, :, HEAD_DIM:] = jnp.ones((tm, HEAD_DIM), BF16)

        put_v(va_ref, proj[:, off:off + LANES])
        off += KV_W
        for i in range(Q_HEADS // 2):
            put(qb_ref, 2 * i, _rope(proj[:, off + LANES * i:off + LANES * (i + 1)], *tab_b, 32) * Q_SCALE)
        off += BRANCH_W
        put(kb_ref, 0, _rope(proj[:, off:off + LANES], *tab_b, 32))
        off += KV_W
        put_v(vb_ref, proj[:, off:off + LANES])
        gate_ref[...] = proj[:, QK_W:].astype(BF16)

    tab_spec = _row_spec(tm, LANES)
    return _call(
        body, name="in_proj", grid=(s // tm,),
        in_specs=[_row_spec(tm, D_MODEL), _full_spec(mod6.shape), _full_spec(g1.shape), _full_spec(w_in_s.shape),
                  _full_spec(gq.shape), _full_spec(gk.shape), _full_spec(bd.shape)] + [tab_spec] * 6,
        out_specs=[_row_spec(tm, D_MODEL), _row_spec(tm, BRANCH_W + KV_W), _heads_spec(Q_HEADS, tm), _heads_spec(KV_HEADS, tm),
                   pl.BlockSpec((KV_HEADS, tm, LANES), lambda i: (0, i, 0)), _heads_spec(Q_HEADS, tm),
                   _heads_spec(KV_HEADS, tm), pl.BlockSpec((KV_HEADS, tm, LANES), lambda i: (0, i, 0)),
                   _row_spec(tm, 2 * D_MODEL)],
        out_shape=[jax.ShapeDtypeStruct((s, D_MODEL), BF16), jax.ShapeDtypeStruct((s, BRANCH_W + KV_W), F32),
                   jax.ShapeDtypeStruct((Q_HEADS, s, HEAD_DIM), BF16), jax.ShapeDtypeStruct((KV_HEADS, s, HEAD_DIM), BF16),
                   jax.ShapeDtypeStruct((KV_HEADS, s, LANES), BF16), jax.ShapeDtypeStruct((Q_HEADS, s, HEAD_DIM), BF16),
                   jax.ShapeDtypeStruct((KV_HEADS, s, HEAD_DIM), BF16), jax.ShapeDtypeStruct((KV_HEADS, s, LANES), BF16),
                   jax.ShapeDtypeStruct((s, 2 * D_MODEL), BF16)],
        compiler_params=_params(("parallel",)),
    )(x, mod6, g1, w_in_s, gq, gk, bd, *tabs)


def _group_specs(s, tq):
    q_spec = pl.BlockSpec((None, GROUP, tq, HEAD_DIM), lambda g, i: (g, 0, i, 0))
    kv_spec = pl.BlockSpec((None, s, HEAD_DIM), lambda g, i: (g, 0, 0))
    col_spec = pl.BlockSpec((None, GROUP, tq, 1), lambda g, i: (g, 0, i, 0))
    return q_spec, kv_spec, col_spec


def _attn_a_fwd(q, k, v1, rider=None, tq=256, tk=2048):
    s = q.shape[1]
    tk = min(tk, s // 2)
    rows = GROUP * tq

    n = s // tk
    assert n >= 2 and n % 2 == 0

    def body(q_ref, k_ref, v_ref, o_ref, oh_ref, lse_ref, s0_ref, s1_ref, p0_ref, p1_ref, m_ref, a_ref, acc_ref):
        s_ref, p_ref = (s0_ref, s1_ref), (p0_ref, p1_ref)
        qq = q_ref[...].reshape(rows, HEAD_DIM)
        m_ref[...] = jnp.full((rows, 1), NEG_INF, F32)
        acc_ref[...] = jnp.zeros((rows, LANES), F32)

        def keys(i):
            return pl.ds(pl.multiple_of(i * tk, tk), tk)

        def scores(i, slot):
            s_ref[slot][...] = _nt(qq, k_ref[keys(i), :])

        def softmax(slot):
            sc = s_ref[slot][...]
            m = m_ref[...]
            mn = jnp.maximum(m, jnp.max(sc, axis=-1, keepdims=True))
            m_ref[...] = mn
            a_ref[...] = jnp.exp(m - mn)
            p_ref[slot][...] = jnp.exp(sc - mn).astype(BF16)

        def weigh(i, slot):
            acc_ref[...] = a_ref[...] * acc_ref[...] + _nn(p_ref[slot][...], v_ref[keys(i), :])

        scores(0, 0)
        softmax(0)
        scores(1, 1)

        def two_steps(j, carry):
            i = 2 * j + 1
            weigh(i - 1, 0)
            softmax(1)
            scores(i + 1, 0)
            weigh(i, 1)
            softmax(0)
            scores(i + 2, 1)
            return carry

        lax.fori_loop(0, (n - 2) // 2, two_steps, 0, unroll=True)
        weigh(n - 2, 0)
        softmax(1)
        weigh(n - 1, 1)
        l = acc_ref[:, HEAD_DIM:HEAD_DIM + 1]
        o = (acc_ref[:, :HEAD_DIM] / l).astype(BF16)
        for g in range(GROUP):
            o_ref[:, HEAD_DIM * g:HEAD_DIM * (g + 1)] = o[tq * g:tq * (g + 1)]
        oh_ref[...] = o.reshape(GROUP, tq, HEAD_DIM)
        lse_ref[...] = (m_ref[...] + jnp.log(l)).reshape(GROUP, tq, 1)

    q_spec, kv_spec, col_spec = _group_specs(s, tq)
    v_spec = pl.BlockSpec((None, s, LANES), lambda g, i: (g, 0, 0))
    return _hosted(
        body, rider, name="attn_a_fwd", grid=(KV_HEADS, s // tq),
        in_specs=[q_spec, kv_spec, v_spec],
        out_specs=[pl.BlockSpec((tq, GROUP * HEAD_DIM), lambda g, i: (i, g)), q_spec, col_spec],
        out_shape=[jax.ShapeDtypeStruct((s, BRANCH_W), BF16), jax.ShapeDtypeStruct((KV_HEADS, GROUP, s, HEAD_DIM), BF16),
                   jax.ShapeDtypeStruct((KV_HEADS, GROUP, s, 1), F32)],
        scratch_shapes=[pltpu.VMEM((rows, tk), F32), pltpu.VMEM((rows, tk), F32), pltpu.VMEM((rows, tk), BF16),
                        pltpu.VMEM((rows, tk), BF16), pltpu.VMEM((rows, 1), F32), pltpu.VMEM((rows, 1), F32),
                        pltpu.VMEM((rows, LANES), F32)],
        args=(q.reshape(KV_HEADS, GROUP, s, HEAD_DIM), k, v1), middle_at=KV_HEADS * (s // tq) // 2)


def _attn_a_bwd(q, k, v1, o, do, lse, rider=None, tq=256, tk=512):
    s = q.shape[1]
    tk = min(tk, s // 2)
    rows = GROUP * tq

    n = s // tk
    assert n >= 2 and n % 2 == 0

    def body(q_ref, k_ref, v_ref, o_ref, do_ref, lse_ref, dq_ref, dk_ref, dv_ref,
             s0_ref, s1_ref, dp0_ref, dp1_ref, p0_ref, p1_ref, ds0_ref, ds1_ref, dq_acc):
        s_ref, dp_ref, p_ref, ds_ref = (s0_ref, s1_ref), (dp0_ref, dp1_ref), (p0_ref, p1_ref), (ds0_ref, ds1_ref)

        @pl.when(pl.program_id(1) == 0)
        def _():
            dk_ref[...] = jnp.zeros_like(dk_ref)
            dv_ref[...] = jnp.zeros_like(dv_ref)

        qq = q_ref[...].reshape(rows, HEAD_DIM)
        dd = do_ref[...].reshape(rows, HEAD_DIM)
        ls = lse_ref[...].reshape(rows, 1)
        dl = jnp.sum(dd.astype(F32) * o_ref[...].reshape(rows, HEAD_DIM).astype(F32), axis=-1, keepdims=True)
        dq_acc[...] = jnp.zeros((rows, HEAD_DIM), F32)

        def keys(i):
            return pl.ds(pl.multiple_of(i * tk, tk), tk)

        def scores(i, slot):
            s_ref[slot][...] = _nt(qq, k_ref[keys(i), :])
            dp_ref[slot][...] = _nt(dd, v_ref[keys(i), :HEAD_DIM])

        def weights(slot):
            p = jnp.exp(s_ref[slot][...] - ls)
            p_ref[slot][...] = p.astype(BF16)
            ds_ref[slot][...] = (p * (dp_ref[slot][...] - dl)).astype(BF16)

        def grads(i, slot):
            dv_ref[keys(i), :] += _tn(p_ref[slot][...], dd)
            dk_ref[keys(i), :] += _tn(ds_ref[slot][...], qq)
            dq_acc[...] += _nn(ds_ref[slot][...], k_ref[keys(i), :])

        scores(0, 0)
        weights(0)
        scores(1, 1)

        def two_steps(j, carry):
            i = 2 * j + 1
            grads(i - 1, 0)
            weights(1)
            scores(i + 1, 0)
            grads(i, 1)
            weights(0)
            scores(i + 2, 1)
            return carry

        lax.fori_loop(0, (n - 2) // 2, two_steps, 0, unroll=True)
        grads(n - 2, 0)
        weights(1)
        grads(n - 1, 1)
        dq_ref[...] = dq_acc[...].astype(BF16).reshape(GROUP, tq, HEAD_DIM)

    q_spec, kv_spec, col_spec = _group_specs(s, tq)
    v_spec = pl.BlockSpec((None, s, LANES), lambda g, i: (g, 0, 0))
    shape4 = (KV_HEADS, GROUP, s, HEAD_DIM)
    tile32, tile16 = pltpu.VMEM((rows, tk), F32), pltpu.VMEM((rows, tk), BF16)
    return _hosted(
        body, rider, name="attn_a_bwd", grid=(KV_HEADS, s // tq),
        in_specs=[q_spec, kv_spec, v_spec, q_spec, q_spec, col_spec],
        out_specs=[q_spec, kv_spec, kv_spec],
        out_shape=[jax.ShapeDtypeStruct(shape4, BF16), jax.ShapeDtypeStruct((KV_HEADS, s, HEAD_DIM), F32),
                   jax.ShapeDtypeStruct((KV_HEADS, s, HEAD_DIM), F32)],
        scratch_shapes=[tile32] * 4 + [tile16] * 4 + [pltpu.VMEM((rows, HEAD_DIM), F32)],
        args=(q.reshape(shape4), k, v1, o.reshape(shape4), do.reshape(shape4), lse))


TQ_B = WINDOW


def _win_keys(tq):
    return tq + 2 * WINDOW


def _window_bias(tq):
    r = jnp.arange(tq, dtype=jnp.int32)[:, None]
    col = jnp.arange(_win_keys(tq), dtype=jnp.int32)[None, :]
    return jnp.stack([jnp.where(jnp.abs(r - col + WINDOW * b) <= WINDOW, 0.0, NEG_INF) for b in range(3)]).astype(F32)


def _band(tq, s):
    win = _win_keys(tq)

    def window(e):
        return pl.ds(pl.multiple_of(jnp.clip(e * tq - WINDOW, 0, s - win), WINDOW), win)

    def bias_index(e):
        return jnp.where(e == 0, 0, jnp.where(e >= s // tq - 1, 2, 1))

    return window, bias_index


def _pair_specs(s, tq):
    pairs = s // (2 * tq)
    cur = lambda g, j: (g, 0, jnp.minimum(j, pairs - 1), 0)
    prev = lambda g, j: (g, 0, jnp.maximum(j - 1, 0), 0)
    tile = lambda width, index: pl.BlockSpec((None, GROUP, 2 * tq, width), index)
    kv_spec = pl.BlockSpec((None, s, HEAD_DIM), lambda g, j: (g, 0, 0))
    v_spec = pl.BlockSpec((None, s, LANES), lambda g, j: (g, 0, 0))
    sink_spec = pl.BlockSpec((None, GROUP * tq, 1), lambda g, j: (g, 0, 0))
    bias_spec = pl.BlockSpec((3, tq, _win_keys(tq)), lambda g, j: (0, 0, 0))
    return tile, cur, prev, kv_spec, v_spec, sink_spec, bias_spec


def _attn_b_fwd(q, k, v1, sink_col, bias, rider=None, tq=TQ_B):
    s = q.shape[1]
    rows = GROUP * tq
    win = _win_keys(tq)
    pairs = s // (2 * tq)
    window, bias_index = _band(tq, s)

    def body(q_ref, k_ref, v_ref, sink_ref, bias_ref, o_ref, oh_ref, lse_ref, s0_ref, s1_ref, p0_ref, p1_ref, m0_ref, m1_ref):
        s_ref, p_ref, m_ref = (s0_ref, s1_ref), (p0_ref, p1_ref), (m0_ref, m1_ref)
        j = pl.program_id(1)

        @pl.when(j == 0)
        def _():
            for ref in (s0_ref, s1_ref, p0_ref, p1_ref, m0_ref, m1_ref):
                ref[...] = jnp.zeros_like(ref)

        def scores(e, slot):
            qq = q_ref[:, pl.ds(slot * tq, tq), :].reshape(rows, HEAD_DIM)
            sc = _nt(qq, k_ref[window(e), :]).reshape(GROUP, tq, win) + bias_ref[bias_index(e)][None]
            s_ref[slot][...] = sc.reshape(rows, win)

        def softmax(slot):
            sc = s_ref[slot][...]
            m = jnp.maximum(jnp.max(sc, axis=-1, keepdims=True), sink_ref[...])
            m_ref[slot][...] = m
            p_ref[slot][...] = jnp.exp(sc - m).astype(BF16)

        def finish(e, slot):
            acc = _nn(p_ref[slot][...], v_ref[window(e), :])
            m = m_ref[slot][...]
            l = acc[:, HEAD_DIM:HEAD_DIM + 1] + jnp.exp(sink_ref[...] - m)
            o = (acc[:, :HEAD_DIM] / l).astype(BF16)
            at = pl.ds(slot * tq, tq)
            for g in range(GROUP):
                o_ref[at, HEAD_DIM * g:HEAD_DIM * (g + 1)] = o[tq * g:tq * (g + 1)]
            oh_ref[:, at, :] = o.reshape(GROUP, tq, HEAD_DIM)
            lse_ref[:, at, :] = (m + jnp.log(l)).reshape(GROUP, tq, 1)

        first = 2 * j
        finish(jnp.maximum(first - 2, 0), 0)
        softmax(1)
        scores(first, 0)
        finish(jnp.maximum(first - 1, 0), 1)
        softmax(0)
        scores(first + 1, 1)

    tile, cur, prev, kv_spec, v_spec, sink_spec, bias_spec = _pair_specs(s, tq)
    tile32, tile16, col = pltpu.VMEM((rows, win), F32), pltpu.VMEM((rows, win), BF16), pltpu.VMEM((rows, 1), F32)
    return _hosted(
        body, rider, name="attn_b_fwd", grid=(KV_HEADS, pairs + 1),
        in_specs=[tile(HEAD_DIM, cur), kv_spec, v_spec, sink_spec, bias_spec],
        out_specs=[pl.BlockSpec((2 * tq, GROUP * HEAD_DIM), lambda g, j: (jnp.maximum(j - 1, 0), g)),
                   tile(HEAD_DIM, prev), tile(1, prev)],
        out_shape=[jax.ShapeDtypeStruct((s, BRANCH_W), BF16), jax.ShapeDtypeStruct((KV_HEADS, GROUP, s, HEAD_DIM), BF16),
                   jax.ShapeDtypeStruct((KV_HEADS, GROUP, s, 1), F32)],
        scratch_shapes=[tile32, tile32, tile16, tile16, col, col],
        args=(q.reshape(KV_HEADS, GROUP, s, HEAD_DIM), k, v1, sink_col, bias))


def _attn_b_bwd(q, k, v1, o, do, lse, sink_col, bias, rider=None, tq=TQ_B):
    s = q.shape[1]
    rows = GROUP * tq
    win = _win_keys(tq)
    pairs = s // (2 * tq)
    window, bias_index = _band(tq, s)

    def body(q_ref, k_ref, v_ref, o_ref, do_ref, lse_ref, sink_ref, bias_ref, dq_ref, dk_ref, dv_ref, dsink_ref,
             s0, s1, dp0, dp1, p0, p1, ds0, ds1, q0, q1, d0, d1, ls0, ls1, dl0, dl1):
        s_ref, dp_ref, p_ref, ds_ref = (s0, s1), (dp0, dp1), (p0, p1), (ds0, ds1)
        q_keep, do_keep, lse_keep, delta_keep = (q0, q1), (d0, d1), (ls0, ls1), (dl0, dl1)
        j = pl.program_id(1)

        @pl.when(j == 0)
        def _():
            for ref in (dk_ref, dv_ref, dsink_ref, s0, s1, dp0, dp1, p0, p1, ds0, ds1, q0, q1, d0, d1, ls0, ls1, dl0, dl1):
                ref[...] = jnp.zeros_like(ref)

        def scores(e, slot):
            at = pl.ds(slot * tq, tq)
            qq = q_ref[:, at, :].reshape(rows, HEAD_DIM)
            dd = do_ref[:, at, :].reshape(rows, HEAD_DIM)
            q_keep[slot][...] = qq
            do_keep[slot][...] = dd
            lse_keep[slot][...] = lse_ref[:, at, :].reshape(rows, 1)
            delta_keep[slot][...] = jnp.sum(dd.astype(F32) * o_ref[:, at, :].reshape(rows, HEAD_DIM).astype(F32), axis=-1,
                                            keepdims=True)
            sc = _nt(qq, k_ref[window(e), :]).reshape(GROUP, tq, win) + bias_ref[bias_index(e)][None]
            s_ref[slot][...] = sc.reshape(rows, win)
            dp_ref[slot][...] = _nt(dd, v_ref[window(e), :HEAD_DIM])

        def weights(slot):
            p = jnp.exp(s_ref[slot][...] - lse_keep[slot][...])
            p_ref[slot][...] = p.astype(BF16)
            ds_ref[slot][...] = (p * (dp_ref[slot][...] - delta_keep[slot][...])).astype(BF16)

        def grads(e, slot, live):
            at = window(e)
            ds = ds_ref[slot][...]
            dv_ref[at, :] += _tn(p_ref[slot][...], do_keep[slot][...])
            dk_ref[at, :] += _tn(ds, q_keep[slot][...])
            dq_ref[:, pl.ds(slot * tq, tq), :] = _nn(ds, k_ref[at, :]).astype(BF16).reshape(GROUP, tq, HEAD_DIM)
            dsk = jnp.exp(sink_ref[...] - lse_keep[slot][...]) * delta_keep[slot][...] * live
            for g in range(GROUP):
                dsink_ref[g:g + 1, :] -= jnp.broadcast_to(jnp.sum(dsk[tq * g:tq * (g + 1)], axis=0, keepdims=True), (1, LANES))

        first = 2 * j
        live = jnp.where(j > 0, 1.0, 0.0)
        grads(jnp.maximum(first - 2, 0), 0, live)
        weights(1)
        scores(first, 0)
        grads(jnp.maximum(first - 1, 0), 1, live)
        weights(0)
        scores(first + 1, 1)

    tile, cur, prev, kv_spec, v_spec, sink_spec, bias_spec = _pair_specs(s, tq)
    dsink_spec = pl.BlockSpec((None, ACC_ROWS, LANES), lambda g, j: (g, 0, 0))
    shape4 = (KV_HEADS, GROUP, s, HEAD_DIM)
    tile32, tile16 = pltpu.VMEM((rows, win), F32), pltpu.VMEM((rows, win), BF16)
    keep, col = pltpu.VMEM((rows, HEAD_DIM), BF16), pltpu.VMEM((rows, 1), F32)
    return _hosted(
        body, rider, name="attn_b_bwd", grid=(KV_HEADS, pairs + 1),
        in_specs=[tile(HEAD_DIM, cur), kv_spec, v_spec, tile(HEAD_DIM, cur), tile(HEAD_DIM, cur), tile(1, cur), sink_spec,
                  bias_spec],
        out_specs=[tile(HEAD_DIM, prev), kv_spec, kv_spec, dsink_spec],
        out_shape=[jax.ShapeDtypeStruct(shape4, BF16), jax.ShapeDtypeStruct((KV_HEADS, s, HEAD_DIM), F32),
                   jax.ShapeDtypeStruct((KV_HEADS, s, HEAD_DIM), F32), jax.ShapeDtypeStruct((KV_HEADS, ACC_ROWS, LANES), F32)],
        scratch_shapes=[tile32] * 4 + [tile16] * 4 + [keep] * 4 + [col] * 4,
        args=(q.reshape(shape4), k, v1, o.reshape(shape4), do.reshape(shape4), lse, sink_col, bias))


def _post_attn(ya, yb, gates, x, mod6, wbr_s, w_out, tm=512):
    s = x.shape[0]

    def body(ya_ref, yb_ref, g_ref, x_ref, mod_ref, wbr_ref, wo_ref, ua_ref, ub_ref, mg_ref, o_ref, x1_ref):
        ya_t, yb_t = ya_ref[...], yb_ref[...]
        ua = jnp.concatenate([_nn(ya_t, wbr_ref[j, 0]) for j in range(N_SHARD)], axis=1)
        ub = jnp.concatenate([_nn(yb_t, wbr_ref[j, 1]) for j in range(N_SHARD)], axis=1)
        ga, gb = g_ref[:, :D_MODEL].astype(F32), g_ref[:, D_MODEL:].astype(F32)
        merged = (_sigmoid(ga) * ua + _sigmoid(gb) * ub).astype(BF16)
        o = _nn(merged, wo_ref[...])
        ua_ref[...] = ua.astype(BF16)
        ub_ref[...] = ub.astype(BF16)
        mg_ref[...] = merged
        o_ref[...] = o.astype(BF16)
        x1_ref[...] = x_ref[...] + mod_ref[2:3, :] * o

    bf = jax.ShapeDtypeStruct((s, D_MODEL), BF16)
    return _call(
        body, name="post_attn", grid=(s // tm,),
        in_specs=[_row_spec(tm, BRANCH_W), _row_spec(tm, BRANCH_W), _row_spec(tm, 2 * D_MODEL), _row_spec(tm, D_MODEL),
                  _full_spec(mod6.shape), _full_spec(wbr_s.shape), _full_spec(w_out.shape)],
        out_specs=[_row_spec(tm, D_MODEL)] * 5,
        out_shape=[bf, bf, bf, bf, jax.ShapeDtypeStruct((s, D_MODEL), F32)],
        compiler_params=_params(("parallel",)),
    )(ya, yb, gates, x, mod6, wbr_s, w_out)


def _mlp_in(x1, mod6, g2, w_mi_s, tm=512):
    s = x1.shape[0]

    def body(x_ref, mod_ref, g_ref, w_ref, h2_ref, a_ref, hid_ref):
        xt = x_ref[...]
        r = lax.rsqrt(jnp.mean(xt * xt, axis=-1, keepdims=True) + NORM_EPS)
        h2 = ((xt * r * g_ref[...]) * (1.0 + mod_ref[4:5, :]) + mod_ref[3:4, :]).astype(BF16)
        h2_ref[...] = h2
        a = jnp.concatenate([_nn(h2, w_ref[j]) for j in range(N_SHARD)], axis=1)
        a_ref[...] = a.astype(BF16)
        hid_ref[...] = jnp.square(jnp.maximum(a, 0.0)).astype(BF16)

    return _call(
        body, name="mlp_in", grid=(s // tm,),
        in_specs=[_row_spec(tm, D_MODEL), _full_spec(mod6.shape), _full_spec(g2.shape), _full_spec(w_mi_s.shape)],
        out_specs=[_row_spec(tm, D_MODEL), _row_spec(tm, D_FF), _row_spec(tm, D_FF)],
        out_shape=[jax.ShapeDtypeStruct((s, D_MODEL), BF16), jax.ShapeDtypeStruct((s, D_FF), BF16),
                   jax.ShapeDtypeStruct((s, D_FF), BF16)],
        compiler_params=_params(("parallel",)),
    )(x1, mod6, g2, w_mi_s)


ACC_ROWS = 8


def _acc_spec():
    return pl.BlockSpec((ACC_ROWS, D_MODEL), lambda i: (0, 0))


def _acc_add(acc_ref, rows):
    @pl.when(pl.program_id(0) == 0)
    def _():
        acc_ref[...] = jnp.zeros_like(acc_ref)

    for r, val in enumerate(rows):
        acc_ref[r:r + 1, :] += jnp.sum(val, axis=0, keepdims=True)


def _mlp_out_loss(hid, x1, a, target, mod6, gf, w_mo, tm=256):
    s = x1.shape[0]

    def body(hid_ref, x_ref, a_ref, t_ref, mod_ref, gf_ref, w_ref, dx2_ref, dm_ref, da_ref, acc_ref):
        m = _nn(hid_ref[...], w_ref[...])
        gate2 = mod_ref[5:6, :]
        x2 = x_ref[...] + gate2 * m
        r = lax.rsqrt(jnp.mean(x2 * x2, axis=-1, keepdims=True) + NORM_EPS)
        xn = x2 * r
        err = xn * gf_ref[...] - t_ref[...]
        dy = err * (1.0 / D_MODEL)
        dxn = dy * gf_ref[...]
        dx2 = r * (dxn - xn * jnp.mean(dxn * xn, axis=-1, keepdims=True))
        dx2_ref[...] = dx2
        dm = (dx2 * gate2).astype(BF16)
        dm_ref[...] = dm
        da_ref[...] = (_nt(dm, w_ref[...]) * (2.0 * jnp.maximum(a_ref[...].astype(F32), 0.0))).astype(BF16)
        _acc_add(acc_ref, [err * err, dy * xn, dx2 * m])

    return _call(
        body, name="mlp_out_loss", grid=(s // tm,),
        in_specs=[_row_spec(tm, D_FF), _row_spec(tm, D_MODEL), _row_spec(tm, D_FF), _row_spec(tm, D_MODEL),
                  _full_spec(mod6.shape), _full_spec(gf.shape), _full_spec(w_mo.shape)],
        out_specs=[_row_spec(tm, D_MODEL), _row_spec(tm, D_MODEL), _row_spec(tm, D_FF), _acc_spec()],
        out_shape=[jax.ShapeDtypeStruct((s, D_MODEL), F32), jax.ShapeDtypeStruct((s, D_MODEL), BF16),
                   jax.ShapeDtypeStruct((s, D_FF), BF16), jax.ShapeDtypeStruct((ACC_ROWS, D_MODEL), F32)],
        compiler_params=_params(("arbitrary",)),
    )(hid, x1, a, target, mod6, gf, w_mo)


def _norm_bwd(dh, xt, gain, scale):
    r = lax.rsqrt(jnp.mean(xt * xt, axis=-1, keepdims=True) + NORM_EPS)
    xn = xt * r
    dxn = dh * (gain * (1.0 + scale))
    dx = r * (dxn - xn * jnp.mean(dxn * xn, axis=-1, keepdims=True))
    return dx, [dh, dh * xn * gain, dh * xn * (1.0 + scale)]


def _mlp_bwd(da, x1, dx2, o, mod6, g2, w_mi_s, rider=None, tm=512):
    s = x1.shape[0]

    def body(da_ref, x_ref, dx2_ref, o_ref, mod_ref, g_ref, w_ref, dx1_ref, do_ref, acc_ref):
        dh2 = _nt(da_ref[:, :D_MODEL], w_ref[0])
        for j in range(1, N_SHARD):
            dh2 += _nt(da_ref[:, D_MODEL * j:D_MODEL * (j + 1)], w_ref[j])
        dx, sums = _norm_bwd(dh2, x_ref[...], g_ref[...], mod_ref[4:5, :])
        dx1 = dx2_ref[...] + dx
        dx1_ref[...] = dx1
        do_ref[...] = (dx1 * mod_ref[2:3, :]).astype(BF16)
        _acc_add(acc_ref, sums + [dx1 * o_ref[...].astype(F32)])

    return _hosted(
        body, rider, name="mlp_bwd", grid=(s // tm,),
        in_specs=[_row_spec(tm, D_FF), _row_spec(tm, D_MODEL), _row_spec(tm, D_MODEL), _row_spec(tm, D_MODEL),
                  _full_spec(mod6.shape), _full_spec(g2.shape), _full_spec(w_mi_s.shape)],
        out_specs=[_row_spec(tm, D_MODEL), _row_spec(tm, D_MODEL), _acc_spec()],
        out_shape=[jax.ShapeDtypeStruct((s, D_MODEL), F32), jax.ShapeDtypeStruct((s, D_MODEL), BF16),
                   jax.ShapeDtypeStruct((ACC_ROWS, D_MODEL), F32)],
        args=(da, x1, dx2, o, mod6, g2, w_mi_s))


def _merge_bwd(do, gates, ua, ub, w_out, wbr_s, rider=None, tm=512):
    s = do.shape[0]

    def body(do_ref, g_ref, ua_ref, ub_ref, wo_ref, wbr_ref, dua_ref, dub_ref, dg_ref, doa_ref, dob_ref):
        dmerged = _nt(do_ref[...], wo_ref[...])
        for b, (u_ref, du_ref, dy_ref) in enumerate(((ua_ref, dua_ref, doa_ref), (ub_ref, dub_ref, dob_ref))):
            sg = _sigmoid(g_ref[:, D_MODEL * b:D_MODEL * (b + 1)].astype(F32))
            du = (dmerged * sg).astype(BF16)
            du_ref[...] = du
            dg_ref[:, D_MODEL * b:D_MODEL * (b + 1)] = (dmerged * u_ref[...].astype(F32) * sg * (1.0 - sg)).astype(BF16)
            w = BRANCH_W // 2
            dy = _nt(du[:, :w], wbr_ref[0, b])
            for j in range(1, N_SHARD):
                dy += _nt(du[:, w * j:w * (j + 1)], wbr_ref[j, b])
            dyb = dy.astype(BF16)
            for h in range(Q_HEADS):
                dy_ref[h] = dyb[:, HEAD_DIM * h:HEAD_DIM * (h + 1)]

    bf = jax.ShapeDtypeStruct((s, D_MODEL), BF16)
    heads = jax.ShapeDtypeStruct((Q_HEADS, s, HEAD_DIM), BF16)
    return _hosted(
        body, rider, name="merge_bwd", grid=(s // tm,),
        in_specs=[_row_spec(tm, D_MODEL), _row_spec(tm, 2 * D_MODEL), _row_spec(tm, D_MODEL), _row_spec(tm, D_MODEL),
                  _full_spec(w_out.shape), _full_spec(wbr_s.shape)],
        out_specs=[_row_spec(tm, D_MODEL), _row_spec(tm, D_MODEL), _row_spec(tm, 2 * D_MODEL),
                   _heads_spec(Q_HEADS, tm), _heads_spec(Q_HEADS, tm)],
        out_shape=[bf, bf, jax.ShapeDtypeStruct((s, 2 * D_MODEL), BF16), heads, heads],
        args=(do, gates, ua, ub, w_out, wbr_s))


def _qk_bwd(dqa, dka, dva, dqb, dkb, dvb, qkraw, dgates, gq, gk, bd, tabs, rider=None, tm=512):
    s = qkraw.shape[0]

    def body(dqa_ref, dka_ref, dva_ref, dqb_ref, dkb_ref, dvb_ref, raw_ref, dg_ref, gq_ref, gk_ref, bd_ref,
             ca, la, ha, cb, lb, hb, dp_ref, acc_ref, pair_ref):
        bdm = bd_ref[...]
        tab_a = (ca[...], la[...], ha[...])
        tab_b = (cb[...], lb[...], hb[...])

        def pair(ref, first):
            pair_ref[:, :HEAD_DIM] = ref[first].astype(F32)
            pair_ref[:, HEAD_DIM:] = ref[first + 1].astype(F32)
            return pair_ref[...]

        def norm_rope_a_bwd(dz, raw, gain):
            dzn = _rope(dz, *tab_a, 16, sign=-1.0)
            rinv = lax.rsqrt(_head_mean(raw * raw, bdm) + NORM_EPS)
            zhat = raw * rinv
            dzhat = dzn * gain
            return rinv * (dzhat - zhat * _head_mean(dzhat * zhat, bdm)), dzn * zhat

        gq_rows = jnp.zeros((tm, LANES), F32)
        for i in range(Q_HEADS // 2):
            at = slice(LANES * i, LANES * (i + 1))
            draw, gsum = norm_rope_a_bwd(pair(dqa_ref, 2 * i) * Q_SCALE, raw_ref[:, at], gq_ref[...])
            dp_ref[:, at] = draw.astype(BF16)
            gq_rows += gsum
        off = BRANCH_W
        draw, gk_rows = norm_rope_a_bwd(pair(dka_ref, 0), raw_ref[:, off:off + LANES], gk_ref[...])
        dp_ref[:, off:off + LANES] = draw.astype(BF16)
        off += KV_W
        dp_ref[:, off:off + LANES] = pair(dva_ref, 0).astype(BF16)
        off += KV_W
        for i in range(Q_HEADS // 2):
            dz = _rope(pair(dqb_ref, 2 * i) * Q_SCALE, *tab_b, 32, sign=-1.0)
            dp_ref[:, off + LANES * i:off + LANES * (i + 1)] = dz.astype(BF16)
        off += BRANCH_W
        dp_ref[:, off:off + LANES] = _rope(pair(dkb_ref, 0), *tab_b, 32, sign=-1.0).astype(BF16)
        off += KV_W
        dp_ref[:, off:off + LANES] = pair(dvb_ref, 0).astype(BF16)
        dp_ref[:, QK_W:] = dg_ref[...]

        @pl.when(pl.program_id(0) == 0)
        def _():
            acc_ref[...] = jnp.zeros_like(acc_ref)

        acc_ref[0:1, :] += jnp.sum(gq_rows, axis=0, keepdims=True)
        acc_ref[1:2, :] += jnp.sum(gk_rows, axis=0, keepdims=True)

    tab_spec = _row_spec(tm, LANES)
    return _hosted(
        body, rider, name="qk_bwd", grid=(s // tm,),
        in_specs=[_heads_spec(Q_HEADS, tm), _heads_spec(KV_HEADS, tm), _heads_spec(KV_HEADS, tm),
                  _heads_spec(Q_HEADS, tm), _heads_spec(KV_HEADS, tm), _heads_spec(KV_HEADS, tm),
                  _row_spec(tm, BRANCH_W + KV_W), _row_spec(tm, 2 * D_MODEL),
                  _full_spec(gq.shape), _full_spec(gk.shape), _full_spec(bd.shape)] + [tab_spec] * 6,
        out_specs=[_row_spec(tm, IN_W), pl.BlockSpec((ACC_ROWS, LANES), lambda i: (0, 0))],
        out_shape=[jax.ShapeDtypeStruct((s, IN_W), BF16), jax.ShapeDtypeStruct((ACC_ROWS, LANES), F32)],
        scratch_shapes=[pltpu.VMEM((tm, LANES), F32)],
        args=(dqa, dka, dva, dqb, dkb, dvb, qkraw, dgates, gq, gk, bd, *tabs))


def _in_proj_bwd(dproj, x, dx1, mod6, g1, w_in_s, tm=512):
    s = x.shape[0]
    w = IN_W // N_SHARD

    def body(dp_ref, x_ref, dx1_ref, mod_ref, g_ref, w_ref, gx_ref, acc_ref):
        dh = _nt(dp_ref[:, :w], w_ref[0])
        for j in range(1, N_SHARD):
            dh += _nt(dp_ref[:, w * j:w * (j + 1)], w_ref[j])
        dx, sums = _norm_bwd(dh, x_ref[...], g_ref[...], mod_ref[1:2, :])
        gx_ref[...] = dx1_ref[...] + dx
        _acc_add(acc_ref, sums)

    return _call(
        body, name="in_proj_bwd", grid=(s // tm,),
        in_specs=[_row_spec(tm, IN_W), _row_spec(tm, D_MODEL), _row_spec(tm, D_MODEL),
                  _full_spec(mod6.shape), _full_spec(g1.shape), _full_spec(w_in_s.shape)],
        out_specs=[_row_spec(tm, D_MODEL), _acc_spec()],
        out_shape=[jax.ShapeDtypeStruct((s, D_MODEL), F32), jax.ShapeDtypeStruct((ACC_ROWS, D_MODEL), F32)],
        compiler_params=_params(("arbitrary",)),
    )(dproj, x, dx1, mod6, g1, w_in_s)


def _wgrad(name, a, b, out_shape, out_spec, tm, tn, tk=4096):
    s, m = a.shape
    n = b.shape[1]
    tk = min(tk, s)
    nk = s // tk

    def body(a_ref, b_ref, o_ref, acc_ref):
        k = pl.program_id(2)

        @pl.when(k == 0)
        def _():
            acc_ref[...] = jnp.zeros_like(acc_ref)

        acc_ref[...] += _tn(a_ref[...], b_ref[...])

        @pl.when(k == nk - 1)
        def _():
            o_ref[...] = acc_ref[...].reshape(o_ref.shape)

    return _call(
        body, name=name, grid=(m // tm, n // tn, nk),
        in_specs=[pl.BlockSpec((tk, tm), lambda i, j, k: (k, i)), pl.BlockSpec((tk, tn), lambda i, j, k: (k, j))],
        out_specs=out_spec, out_shape=jax.ShapeDtypeStruct(out_shape, F32),
        scratch_shapes=[pltpu.VMEM((tm, tn), F32)],
        compiler_params=_params(("parallel", "parallel", "arbitrary")),
    )(a, b)


def _wgrad_branch(ya, yb, dua, dub, tk=2048):
    s = ya.shape[0]
    tk = min(tk, s)
    nk = s // tk
    w = D_MODEL // N_SHARD

    def body(ya_ref, yb_ref, dua_ref, dub_ref, o_ref, acc_ref):
        b, k = pl.program_id(0), pl.program_id(1)

        @pl.when(k == 0)
        def _():
            acc_ref[...] = jnp.zeros_like(acc_ref)

        @pl.when(b == 0)
        def _():
            acc_ref[...] += _tn(ya_ref[...], dua_ref[...])

        @pl.when(b == 1)
        def _():
            acc_ref[...] += _tn(yb_ref[...], dub_ref[...])

        @pl.when(k == nk - 1)
        def _():
            for j in range(N_SHARD):
                o_ref[j] = acc_ref[:, w * j:w * (j + 1)]

    first = lambda width: pl.BlockSpec((tk, width), lambda b, k: (k * (1 - b), 0))
    second = lambda width: pl.BlockSpec((tk, width), lambda b, k: (k * b, 0))
    return _call(
        body, name="wgrad_branch", grid=(2, nk),
        in_specs=[first(BRANCH_W), second(BRANCH_W), first(D_MODEL), second(D_MODEL)],
        out_specs=pl.BlockSpec((N_SHARD, None, BRANCH_W, w), lambda b, k: (0, b, 0, 0)),
        out_shape=jax.ShapeDtypeStruct((N_SHARD, 2, BRANCH_W, w), F32),
        scratch_shapes=[pltpu.VMEM((BRANCH_W, D_MODEL), F32)],
        compiler_params=_params(("parallel", "arbitrary")),
    )(ya, yb, dua, dub)


def _local_step(x, target, mod6, g1, g2, gf, gq2, gk2, sink, w_in_s, rest, cj=None):
    s = x.shape[0]
    dist = cj is not None
    tabs = _rope_tables(s)
    bd = _block_diag()
    sink_col = jnp.repeat(sink.reshape(KV_HEADS, GROUP, 1), TQ_B, axis=1).reshape(KV_HEADS, GROUP * TQ_B, 1)
    shard = D_MODEL // N_SHARD

    h, qkraw, qa, ka, va, qb, kb, vb, gates = _in_proj(x, mod6, g1, w_in_s, gq2, gk2, bd, tabs)
    bias = _window_bias(TQ_B)
    (yb, yb_heads, lse_b), _ = _attn_b_fwd(qb, kb, vb, sink_col, bias)
    (ya, ya_heads, lse_a), gathered = _attn_a_fwd(qa, ka, va, rider=_gather_rider(rest) if dist else None)
    wbr_s, w_out, w_mi_s, w_mo = gathered if dist else rest
    wbr_s = wbr_s.reshape(N_SHARD, 2, BRANCH_W, shard)
    w_out = w_out.reshape(D_MODEL, D_MODEL)
    w_mo = w_mo.reshape(D_FF, D_MODEL)
    ua, ub, merged, o, x1 = _post_attn(ya, yb, gates, x, mod6, wbr_s, w_out)
    h2, a, hid = _mlp_in(x1, mod6, g2, w_mi_s)
    dx2, dm, da, acc_out = _mlp_out_loss(hid, x1, a, target, mod6, gf, w_mo)

    g_w_mo = _wgrad("wgrad_mlp_out", hid, dm, (D_FF, D_MODEL), pl.BlockSpec((D_MODEL, D_MODEL), lambda i, j, k: (i, 0)),
                    D_MODEL, D_MODEL).reshape(N_SHARD, D_MODEL, D_MODEL)
    g_w_mi = _wgrad("wgrad_mlp_in", h2, da, (N_SHARD, D_MODEL, D_MODEL),
                    pl.BlockSpec((None, D_MODEL, D_MODEL), lambda i, j, k: (j, i, 0)), D_MODEL, D_MODEL)
    mlp = _Reduction(("mlp_out", "mlp_in"), (g_w_mo, g_w_mi), cj)
    (dx1, do, acc_mlp), got = _mlp_bwd(da, x1, dx2, o, mod6, g2, w_mi_s, rider=mlp.swap() if dist else None)
    (dua, dub, dgates, doa, dob), landed = _merge_bwd(do, gates, ua, ub, w_out, wbr_s, rider=mlp.add(got) if dist else None)
    g_w_out = _wgrad("wgrad_out", merged, do, (D_MODEL, D_MODEL), pl.BlockSpec((D_MODEL, D_MODEL), lambda i, j, k: (i, 0)),
                     D_MODEL, D_MODEL).reshape(N_SHARD, shard, D_MODEL)
    g_wbr = _wgrad_branch(ya, yb, dua, dub)
    out = _Reduction(("out", "branch"), (g_w_out, g_wbr.reshape(N_SHARD, 2 * BRANCH_W, shard)), cj)
    (dqa, dka, dva), landings = _attn_a_bwd(qa, ka, va, ya_heads, doa, lse_a,
                                            rider=_riders(out.swap(), mlp.total(landed)) if dist else None)
    (dqb, dkb, dvb, dsink), _ = _attn_b_bwd(qb, kb, vb, yb_heads, dob, lse_b, sink_col, bias)
    heads = (Q_HEADS, s, HEAD_DIM)
    (dproj, acc_qk), landed = _qk_bwd(dqa.reshape(heads), dka, dva, dqb.reshape(heads), dkb, dvb, qkraw, dgates, gq2, gk2, bd,
                                      tabs, rider=out.add(landings[:2]) if dist else None)
    w = IN_W // N_SHARD
    g_w_in = _wgrad("wgrad_in", h, dproj, (N_SHARD, D_MODEL, w), pl.BlockSpec((None, D_MODEL, w), lambda i, j, k: (j, i, 0)),
                    D_MODEL, w)
    grad_x, acc_in = _in_proj_bwd(dproj, x, dx1, mod6, g1, w_in_s)
    accs = (acc_out, acc_mlp, acc_in, acc_qk, dsink)
    if not dist:
        return grad_x, (g_w_in, g_wbr, g_w_out, g_w_mi, g_w_mo), accs
    r_mo, r_mi = landings[2:]
    return grad_x, (_Reduction(("in",), (g_w_in,), cj), out.total(landed), r_mi, r_mo), accs


def _me():
    return lax.axis_index("x"), lax.axis_index("y"), lax.axis_index("c")


def _peer(d):
    x, y, c = _me()
    return (1 - x if d & 4 else x, 1 - y if d & 2 else y, 1 - c if d & 1 else c)


def _dev_index(p):
    return 4 * p[0] + 2 * p[1] + p[2]


def _chip_index(p):
    return 2 * p[0] + p[1]


def _remote(src, dst, send_sem, recv_sem, to):
    return pltpu.make_async_remote_copy(src_ref=src, dst_ref=dst, send_sem=send_sem, recv_sem=recv_sem,
                                        device_id=to, device_id_type=MESH)


SLOT_ROWS = 8


def _ada_fwd(c, w_ada, b4, rider=None):
    cols = w_ada.shape[1]

    def body(c_ref, w_ref, b_ref, mod_ref, sc_ref, cbuf, pbuf, mbuf, send1, recv1, send2, recv2, launch=None):
        me = _me()
        mine, chip = _dev_index(me), _chip_index(me)
        cbuf[mine] = jnp.broadcast_to(c_ref[...], (SLOT_ROWS, D_MODEL))
        gather = [_remote(cbuf.at[mine], cbuf.at[mine], send1.at[d - 1], recv1.at[d - 1], _peer(d)) for d in range(1, N_DEV)]
        for cp in gather:
            cp.start()
        if launch is not None:
            launch()
        for d in range(1, N_DEV):
            _remote(cbuf.at[mine], cbuf.at[_dev_index(_peer(d))], send1.at[d - 1], recv1.at[d - 1], _peer(d)).wait_recv()
        call = cbuf[...].reshape(N_DEV * SLOT_ROWS, D_MODEL)
        sc = call * _sigmoid(call)
        for s in range(N_DEV):
            sc_ref[s:s + 1, :] = sc[SLOT_ROWS * s:SLOT_ROWS * s + 1]
        part = _nn(sc.astype(BF16), w_ref[...].astype(BF16)) + b_ref[pl.ds(chip, 1), :]
        pbuf[...] = part.reshape(N_DEV, SLOT_ROWS, cols)
        mbuf[chip] = pbuf[mine]
        spread = [_remote(pbuf.at[_dev_index(_peer(d))], mbuf.at[chip], send2.at[d // 2 - 1], recv2.at[d // 2 - 1], _peer(d))
                  for d in (2, 4, 6)]
        for cp in spread:
            cp.start()
        for d in (2, 4, 6):
            _remote(pbuf.at[mine], mbuf.at[_chip_index(_peer(d))], send2.at[d // 2 - 1], recv2.at[d // 2 - 1],
                    _peer(d)).wait_recv()
        half = D_MODEL // 2
        for p in range(2 * 6):
            col = half * p
            mod_ref[p // 2:p // 2 + 1, half * (p % 2):half * (p % 2 + 1)] = mbuf[col // cols, 0:1, col % cols:col % cols + half]
        for cp in gather + spread:
            cp.wait_send()

    vm = pl.BlockSpec(memory_space=pltpu.VMEM)
    return _hosted(
        body, rider, name="ada_fwd", grid=(), in_specs=[vm, vm, vm], out_specs=[vm, vm],
        out_shape=[jax.ShapeDtypeStruct((6, D_MODEL), F32), jax.ShapeDtypeStruct((N_DEV, D_MODEL), F32)],
        scratch_shapes=[pltpu.VMEM((N_DEV, SLOT_ROWS, D_MODEL), F32), pltpu.VMEM((N_DEV, SLOT_ROWS, cols), F32),
                        pltpu.VMEM((N_SHARD, SLOT_ROWS, cols), F32),
                        pltpu.SemaphoreType.DMA((N_DEV - 1,)), pltpu.SemaphoreType.DMA((N_DEV - 1,)),
                        pltpu.SemaphoreType.DMA((N_SHARD - 1,)), pltpu.SemaphoreType.DMA((N_SHARD - 1,))],
        args=(c, w_ada, b4))


PACK_ROWS = 16
PACK_W = 3 * D_MODEL


def _ada_bwd(acc_out, acc_mlp, acc_in, acc_qk, dsink, sc_all, rider=None):
    cols = 6 * D_MODEL // N_SHARD

    def body(out_ref, mlp_ref, in_ref, qk_ref, dsink_ref, sc_ref,
             gwa_ref, gba_ref, gn1_ref, gn2_ref, gf_ref, gq_ref, gk_ref, gs_ref, loss_ref, blk, send, recv, launch=None):
        me = _me()
        mine, chip = _dev_index(me), _chip_index(me)
        blk[mine] = jnp.zeros((PACK_ROWS, PACK_W), F32)
        dmod = (in_ref, 0), (in_ref, 1), (mlp_ref, 3), (mlp_ref, 0), (mlp_ref, 1), (out_ref, 2)
        half = D_MODEL // 2
        for p in range(2 * 6):
            ref, row = dmod[p // 2]
            col = half * p
            blk[mine, col // cols:col // cols + 1, col % cols:col % cols + half] = ref[row:row + 1, half * (p % 2):half * (p % 2 + 1)]
        blk[mine, 4:5, 0:D_MODEL] = in_ref[2:3, :]
        blk[mine, 4:5, D_MODEL:2 * D_MODEL] = mlp_ref[2:3, :]
        blk[mine, 4:5, 2 * D_MODEL:] = out_ref[1:2, :]
        blk[mine, 5:6, 0:LANES] = qk_ref[0:1, :]
        blk[mine, 5:6, LANES:2 * LANES] = qk_ref[1:2, :]
        blk[mine, 6:7, 0:D_MODEL] = out_ref[0:1, :]
        for g in range(KV_HEADS):
            blk[mine, 8 + GROUP * g:8 + GROUP * (g + 1), 0:LANES] = dsink_ref[g, 0:GROUP, :]
        copies = [_remote(blk.at[mine], blk.at[mine], send.at[d - 1], recv.at[d - 1], _peer(d)) for d in range(1, N_DEV)]
        for cp in copies:
            cp.start()
        if launch is not None:
            launch()
        for d in range(1, N_DEV):
            _remote(blk.at[mine], blk.at[_dev_index(_peer(d))], send.at[d - 1], recv.at[d - 1], _peer(d)).wait_recv()
        tot = blk[0]
        for s in range(1, N_DEV):
            tot = tot + blk[s]
        for j in range(N_SHARD):
            gba_ref[:, cols * j:cols * (j + 1)] = tot[j:j + 1, :cols]
        gn1_ref[...] = tot[4:5, 0:D_MODEL]
        gn2_ref[...] = tot[4:5, D_MODEL:2 * D_MODEL]
        gf_ref[...] = tot[4:5, 2 * D_MODEL:]
        gq_ref[...] = tot[5:6, 0:HEAD_DIM] + tot[5:6, HEAD_DIM:2 * HEAD_DIM]
        gk_ref[...] = tot[5:6, LANES:LANES + HEAD_DIM] + tot[5:6, LANES + HEAD_DIM:2 * LANES]
        sq = tot[8:16, 0:Q_HEADS]
        diag = lax.broadcasted_iota(jnp.int32, sq.shape, 0) == lax.broadcasted_iota(jnp.int32, sq.shape, 1)
        gs_ref[...] = jnp.sum(jnp.where(diag, sq, 0.0), axis=0, keepdims=True)
        half_mse = (0.5 / D_MODEL) * jnp.sum(tot[6:7, 0:D_MODEL], axis=-1, keepdims=True)
        loss_ref[...] = jnp.broadcast_to(half_mse, (1, LANES))
        dm = jnp.concatenate([blk[s, pl.ds(chip, 1), pl.ds(0, cols)] for s in range(N_DEV)], axis=0)
        gwa_ref[...] = _tn(sc_ref[...], dm)
        for cp in copies:
            cp.wait_send()

    vm = pl.BlockSpec(memory_space=pltpu.VMEM)
    row = lambda n: jax.ShapeDtypeStruct((1, n), F32)
    return _hosted(
        body, rider, name="ada_bwd", grid=(), in_specs=[vm] * 6, out_specs=[vm] * 9,
        out_shape=[jax.ShapeDtypeStruct((D_MODEL, cols), F32), row(6 * D_MODEL), row(D_MODEL), row(D_MODEL), row(D_MODEL),
                   row(HEAD_DIM), row(HEAD_DIM), row(Q_HEADS), row(LANES)],
        scratch_shapes=[pltpu.VMEM((N_DEV, PACK_ROWS, PACK_W), F32),
                        pltpu.SemaphoreType.DMA((N_DEV - 1,)), pltpu.SemaphoreType.DMA((N_DEV - 1,))],
        args=(acc_out, acc_mlp, acc_in, acc_qk, dsink, sc_all))


def _cast_weights(ws):
    n = len(ws)

    def body(*refs):
        src, out, tmp, sems = refs[:n], refs[n:2 * n], refs[2 * n:3 * n], refs[3 * n]
        chip = _chip_index(_me())
        copies = []
        for a in range(n):
            tmp[a][...] = src[a][...].astype(BF16)
            cp = pltpu.make_async_copy(tmp[a], out[a].at[chip], sems.at[a])
            cp.start()
            copies.append(cp)
        for cp in copies:
            cp.wait()

    vm = pl.BlockSpec(memory_space=pltpu.VMEM)
    return _call(
        body, name="cast_weights", in_specs=[vm] * n, out_specs=[ANY] * n,
        out_shape=[jax.ShapeDtypeStruct((N_SHARD,) + w.shape, BF16) for w in ws],
        scratch_shapes=[pltpu.VMEM(w.shape, BF16) for w in ws] + [pltpu.SemaphoreType.DMA((n,))],
        compiler_params=_params(),
    )(*ws)


def _half_rows(ref_rows, c):
    half = ref_rows // 2
    return pl.ds(pl.multiple_of(c * half, 8), half)


class _Rider:
    def __init__(self, inputs, out_shape, aliases, n_sems, start, finish, middle=None):
        self.inputs, self.out_shape, self.aliases, self.n_sems = list(inputs), list(out_shape), dict(aliases), n_sems
        self.start, self.finish, self.middle = start, finish, middle


def _riders(*rs):
    ins = [0]
    outs = [0]
    sems = [0]
    for r in rs:
        ins.append(ins[-1] + len(r.inputs))
        outs.append(outs[-1] + len(r.out_shape))
        sems.append(sems[-1] + r.n_sems)

    def phase(which):
        def run(in_refs, out_refs, sem):
            for k, r in enumerate(rs):
                fn = getattr(r, which)
                if fn is not None:
                    fn(in_refs[ins[k]:ins[k + 1]], out_refs[outs[k]:outs[k + 1]], lambda j, base=sems[k]: sem(base + j))
        return run

    aliases = {ins[k] + i: outs[k] + o for k, r in enumerate(rs) for i, o in r.aliases.items()}
    return _Rider([a for r in rs for a in r.inputs], [o for r in rs for o in r.out_shape], aliases, sems[-1],
                  phase("start"), phase("finish"), phase("middle") if any(r.middle for r in rs) else None)


def _hosted(body, rider, *, name, grid, in_specs, out_specs, out_shape, args, scratch_shapes=(), middle_at=None):
    where = dict(grid=grid, compiler_params=_params(("arbitrary",) * len(grid))) if grid else dict(compiler_params=_params())
    if rider is None:
        res = _call(body, name=name, in_specs=in_specs, out_specs=out_specs, out_shape=out_shape,
                    scratch_shapes=list(scratch_shapes), **where)(*args)
        return res, ()
    n_in, n_out, n_scr = len(in_specs), len(out_specs), len(scratch_shapes)
    r_in, r_out = len(rider.inputs), len(rider.out_shape)

    def riding(*refs):
        at = 0
        parts = []
        for size in (n_in, r_in, n_out, r_out, n_scr):
            parts.append(refs[at:at + size])
            at += size
        ins, rider_ins, outs, rider_outs, scratch = parts
        sems = refs[at]

        def sem_at(k):
            return sems.at[k]

        if not grid:
            body(*ins, *outs, *scratch, launch=lambda: rider.start(rider_ins, rider_outs, sem_at))
            if rider.middle is not None:
                rider.middle(rider_ins, rider_outs, sem_at)
            rider.finish(rider_ins, rider_outs, sem_at)
            return
        step = pl.program_id(0)
        for axis in range(1, len(grid)):
            step = step * grid[axis] + pl.program_id(axis)
        steps = 1
        for size in grid:
            steps *= size

        @pl.when(step == 0)
        def _():
            rider.start(rider_ins, rider_outs, sem_at)

        body(*ins, *outs, *scratch)
        if rider.middle is not None:
            @pl.when(step == middle_at)
            def _():
                rider.middle(rider_ins, rider_outs, sem_at)

        @pl.when(step == steps - 1)
        def _():
            rider.finish(rider_ins, rider_outs, sem_at)

    res = _call(
        riding, name=name, in_specs=list(in_specs) + [ANY] * r_in, out_specs=list(out_specs) + [ANY] * r_out,
        out_shape=list(out_shape) + rider.out_shape,
        input_output_aliases={n_in + i: n_out + o for i, o in rider.aliases.items()},
        scratch_shapes=list(scratch_shapes) + [pltpu.SemaphoreType.DMA((rider.n_sems,))], **where,
    )(*args, *rider.inputs)
    return res[:n_out], res[n_out:]


def _alone(name, rider):
    n_in, n_out = len(rider.inputs), len(rider.out_shape)

    def body(*refs):
        ins, outs, sems = refs[:n_in], refs[n_in:n_in + n_out], refs[n_in + n_out]

        def sem_at(k):
            return sems.at[k]

        rider.start(ins, outs, sem_at)
        if rider.middle is not None:
            rider.middle(ins, outs, sem_at)
        rider.finish(ins, outs, sem_at)

    return _call(
        body, name=name, in_specs=[ANY] * n_in, out_specs=[ANY] * n_out, out_shape=rider.out_shape,
        input_output_aliases=rider.aliases, scratch_shapes=[pltpu.SemaphoreType.DMA((rider.n_sems,))],
    )(*rider.inputs)


OTHER_CHIPS = (2, 4, 6)


def _gather_rider(stacked):
    n = len(stacked)

    def flights(bufs, sem):
        me = _me()
        chip, sib = _chip_index(me), _peer(1)
        out = []
        for a in range(n):
            mine, theirs = (_half_rows(bufs[a].shape[1], c) for c in (me[2], 1 - me[2]))
            for j, d in enumerate(OTHER_CHIPS):
                k = 3 * a + j
                from_chip = _chip_index(_peer(d))
                own, landed, passed = bufs[a].at[chip, mine], bufs[a].at[from_chip, mine], bufs[a].at[from_chip, theirs]
                out.append((_remote(own, own, sem(k), sem(3 * n + k), _peer(d)),
                            _remote(own, landed, sem(k), sem(3 * n + k), _peer(d)),
                            _remote(landed, landed, sem(6 * n + k), sem(9 * n + k), sib),
                            _remote(passed, passed, sem(6 * n + k), sem(9 * n + k), sib)))
        return out

    def start(ins, outs, sem):
        for send, _, _, _ in flights(outs, sem):
            send.start()

    def middle(ins, outs, sem):
        for _, arrival, pass_on, _ in flights(outs, sem):
            arrival.wait_recv()
            pass_on.start()

    def finish(ins, outs, sem):
        every = flights(outs, sem)
        for _, _, _, passed_to_me in every:
            passed_to_me.wait_recv()
        for send, _, pass_on, _ in every:
            send.wait_send()
            pass_on.wait_send()

    return _Rider(stacked, [jax.ShapeDtypeStruct(w.shape, w.dtype) for w in stacked], {a: a for a in range(n)}, 12 * n,
                  start, finish, middle)


def _swap_rider(grads):
    n = len(grads)

    def copies(ins, outs, sem):
        c = _me()[2]
        return [_remote(ins[a].at[pl.ds(0, N_SHARD), _half_rows(ins[a].shape[1], 1 - c)], outs[a], sem(a), sem(n + a), _peer(1))
                for a in range(n)]

    def start(ins, outs, sem):
        for cp in copies(ins, outs, sem):
            cp.start()

    def finish(ins, outs, sem):
        for cp in copies(ins, outs, sem):
            cp.wait()

    return _Rider(grads, [jax.ShapeDtypeStruct((N_SHARD, g.shape[1] // 2, g.shape[2]), F32) for g in grads], {}, 2 * n,
                  start, finish)


def _row_tile(rows):
    return min(rows, 256)


def _add_halves(name, g, got, cj):
    _, half, cols = got.shape
    tr = _row_tile(half)
    nt = half // tr

    def body(cj_ref, g_ref, got_ref, o_ref):
        o_ref[...] = (g_ref[...] + got_ref[...]).astype(BF16)

    spec = pl.BlockSpec((None, tr, cols), lambda i, s, cj: (s, i, 0))
    return _call(
        body, name=name,
        grid_spec=pltpu.PrefetchScalarGridSpec(
            num_scalar_prefetch=1, grid=(nt, N_SHARD),
            in_specs=[pl.BlockSpec((None, tr, cols), lambda i, s, cj: (s, cj[0] * nt + i, 0)), spec], out_specs=spec),
        out_shape=jax.ShapeDtypeStruct(got.shape, BF16), compiler_params=_params(("parallel", "parallel")),
    )(cj, g, got)


def _scatter_rider(sums):
    n = len(sums)

    def flights(ins, outs, sem):
        chip = _chip_index(_me())
        out = []
        for a in range(n):
            for j, d in enumerate(OTHER_CHIPS):
                k = 3 * a + j
                other = _chip_index(_peer(d))
                out.append((_remote(ins[a].at[other], outs[a].at[chip], sem(k), sem(3 * n + k), _peer(d)),
                            _remote(ins[a].at[chip], outs[a].at[other], sem(k), sem(3 * n + k), _peer(d))))
        return out

    def start(ins, outs, sem):
        for send, _ in flights(ins, outs, sem):
            send.start()

    def finish(ins, outs, sem):
        every = flights(ins, outs, sem)
        for _, arrival in every:
            arrival.wait_recv()
        for send, _ in every:
            send.wait_send()

    return _Rider(sums, [jax.ShapeDtypeStruct(v.shape, v.dtype) for v in sums], {}, 6 * n, start, finish)


def _sum_chips(name, g, got, landed, cj):
    _, half, cols = got.shape
    tr = _row_tile(half)
    nt = half // tr

    def body(cj_ref, g_ref, got_ref, landed_ref, o_ref):
        own = g_ref[...] + got_ref[...]
        total = None
        for s in range(N_SHARD):
            term = jnp.where(cj_ref[1] == s, own, landed_ref[s].astype(F32))
            total = term if total is None else total + term
        o_ref[...] = total

    return _call(
        body, name=name,
        grid_spec=pltpu.PrefetchScalarGridSpec(
            num_scalar_prefetch=1, grid=(nt,),
            in_specs=[pl.BlockSpec((None, tr, cols), lambda i, cj: (cj[1], cj[0] * nt + i, 0)),
                      pl.BlockSpec((None, tr, cols), lambda i, cj: (cj[1], i, 0)),
                      pl.BlockSpec((N_SHARD, tr, cols), lambda i, cj: (0, i, 0))],
            out_specs=pl.BlockSpec((tr, cols), lambda i, cj: (cj[0] * nt + i, 0))),
        out_shape=jax.ShapeDtypeStruct((2 * half, cols), F32), compiler_params=_params(("parallel",)),
    )(cj, g, got, landed)


def _join_rider(shards):
    n = len(shards)

    def flights(bufs, sem):
        c = _me()[2]
        out = []
        for a in range(n):
            mine, theirs = (bufs[a].at[_half_rows(bufs[a].shape[0], cc)] for cc in (c, 1 - c))
            out.append((_remote(mine, mine, sem(a), sem(n + a), _peer(1)), _remote(theirs, theirs, sem(a), sem(n + a), _peer(1))))
        return out

    def start(ins, outs, sem):
        for send, _ in flights(outs, sem):
            send.start()

    def finish(ins, outs, sem):
        for send, arrival in flights(outs, sem):
            arrival.wait_recv()
            send.wait_send()

    return _Rider(shards, [jax.ShapeDtypeStruct(h.shape, F32) for h in shards], {a: a for a in range(n)}, 2 * n, start, finish)


class _Reduction:
    def __init__(self, names, grads, cj):
        self.names, self.grads, self.cj = names, list(grads), cj

    def swap(self):
        return _swap_rider(self.grads)

    def add(self, got):
        self.got = list(got)
        self.sums = [_add_halves("add_halves_" + nm, g, h, self.cj) for nm, g, h in zip(self.names, self.grads, self.got)]
        return _scatter_rider(self.sums)

    def total(self, landed):
        halves = [_sum_chips("sum_chips_" + nm, g, h, l, self.cj)
                  for nm, g, h, l in zip(self.names, self.grads, self.got, landed)]
        return _join_rider(halves)


def _adamw_math(w, g, m, v):
    m = ADAM_B1 * m + (1.0 - ADAM_B1) * g
    v = ADAM_B2 * v + (1.0 - ADAM_B2) * jnp.square(g)
    m_hat = m / (1.0 - ADAM_B1 ** ADAM_STEP)
    v_hat = v / (1.0 - ADAM_B2 ** ADAM_STEP)
    return -ADAM_LR * (m_hat / (jnp.sqrt(v_hat) + ADAM_EPS) + ADAM_WD * w), m, v


def _adamw(name, ws, gs, ms, vs, rider=None):
    n = len(ws)
    rows = ws[0].shape[0]
    tr = _row_tile(rows)

    def body(*refs):
        ins, outs = refs[:4 * n], refs[4 * n:]
        for a in range(n):
            w, g, m, v = (ins[k * n + a][...] for k in range(4))
            outs[a][...], outs[n + a][...], outs[2 * n + a][...] = _adamw_math(w, g, m, v)

    specs = [pl.BlockSpec((tr, w.shape[1]), lambda i: (i, 0)) for w in ws]
    res, riding = _hosted(
        body, rider, name=name, grid=(rows // tr,), in_specs=specs * 4, out_specs=specs * 3,
        out_shape=[jax.ShapeDtypeStruct(w.shape, F32) for w in ws] * 3, args=(*ws, *gs, *ms, *vs))
    return (res[:n], res[n:2 * n], res[2 * n:]), riding


def _adamw_small(ws, gs, ms, vs):
    n = len(ws)

    def body(*refs):
        ins, outs = refs[:4 * n], refs[4 * n:]
        for a in range(n):
            w, g, m, v = (ins[k * n + a][...] for k in range(4))
            outs[a][...], outs[n + a][...], outs[2 * n + a][...] = _adamw_math(w, g, m, v)

    vm = pl.BlockSpec(memory_space=pltpu.VMEM)
    res = _call(
        body, name="adamw_small", in_specs=[vm] * (4 * n), out_specs=[vm] * (3 * n),
        out_shape=[jax.ShapeDtypeStruct(w.shape, F32) for w in ws] * 3, compiler_params=_params(),
    )(*ws, *gs, *ms, *vs)
    return res[:n], res[n:2 * n], res[2 * n:]


def kernel(x, c, w_ada, b_ada, norm1_g, w_in, q_norm_a, k_norm_a, sink_b, w_branch, w_out, norm2_g, w_mlp_in, w_mlp_out, final_g, loss_target, m_w_ada, m_b_ada, m_norm1_g, m_w_in, m_q_norm_a, m_k_norm_a, m_sink_b, m_w_branch, m_w_out, m_norm2_g, m_w_mlp_in, m_w_mlp_out, m_final_g, v_w_ada, v_b_ada, v_norm1_g, v_w_in, v_q_norm_a, v_k_norm_a, v_sink_b, v_w_branch, v_w_out, v_norm2_g, v_w_mlp_in, v_w_mlp_out, v_final_g):
    xi, yi, ci = _me()
    cj = jnp.stack([ci, 2 * xi + yi]).astype(jnp.int32)
    n_cols = 6 * D_MODEL // N_SHARD

    def rows2d(a):
        return a.reshape(-1, a.shape[-1])

    big = (w_in, w_branch, w_out, w_mlp_in, w_mlp_out)
    stacked = _cast_weights([rows2d(w) for w in big])
    (mod6, sc_all), (w_in_s,) = _ada_fwd(c, w_ada[0], b_ada.reshape(N_SHARD, n_cols), rider=_gather_rider(stacked[:1]))
    rest = stacked[1:]

    gq2 = jnp.tile(q_norm_a, (1, 2))
    gk2 = jnp.tile(k_norm_a, (1, 2))
    grad_x, (w_in_red, join_out, g_mi, g_mo), accs = _local_step(
        x[0], loss_target[0], mod6, norm1_g, norm2_g, final_g.reshape(1, D_MODEL), gq2, gk2, sink_b[0], w_in_s, rest, cj)

    (g_w_ada, g_b_ada, g_n1, g_n2, g_f, g_q, g_k, g_s, loss_row), got_in = _ada_bwd(*accs, sc_all, rider=w_in_red.swap())
    loss = loss_row[0, 0]
    moments = dict(w_ada=(m_w_ada, v_w_ada), w_in=(m_w_in, v_w_in), w_branch=(m_w_branch, v_w_branch), w_out=(m_w_out, v_w_out),
                   w_mlp_in=(m_w_mlp_in, v_w_mlp_in), w_mlp_out=(m_w_mlp_out, v_w_mlp_out))
    weights = dict(w_ada=w_ada, w_in=w_in, w_branch=w_branch, w_out=w_out, w_mlp_in=w_mlp_in, w_mlp_out=w_mlp_out)

    def adamw(call, names, grads, rider=None):
        (d, m, v), riding = _adamw(call, [rows2d(weights[nm]) for nm in names], grads,
                                   [rows2d(moments[nm][0]) for nm in names], [rows2d(moments[nm][1]) for nm in names], rider)
        return {nm: (grads[k], d[k], m[k], v[k]) for k, nm in enumerate(names)}, riding

    big_res, landed_in = adamw("adamw_ada_mlp", ("w_ada", "w_mlp_in", "w_mlp_out"), [g_w_ada, g_mi, g_mo], w_in_red.add(got_in))
    g_in, g_out, g_br = _alone("join_in_out_branch", _riders(w_in_red.total(landed_in), join_out))
    big_res.update(adamw("adamw_in_branch", ("w_in", "w_branch"), [g_in, g_br])[0])

    small = ("b_ada", "norm1_g", "q_norm_a", "k_norm_a", "sink_b", "norm2_g", "final_g", "w_out")
    row = lambda a: a.reshape(1, -1)
    small_w = [row(a) for a in (b_ada, norm1_g, q_norm_a, k_norm_a, sink_b, norm2_g, final_g)] + [w_out[0]]
    small_g = [g_b_ada, g_n1, g_q, g_k, g_s, g_n2, g_f, g_out]
    small_m = [row(a) for a in (m_b_ada, m_norm1_g, m_q_norm_a, m_k_norm_a, m_sink_b, m_norm2_g, m_final_g)] + [m_w_out[0]]
    small_v = [row(a) for a in (v_b_ada, v_norm1_g, v_q_norm_a, v_k_norm_a, v_sink_b, v_norm2_g, v_final_g)] + [v_w_out[0]]
    s_d, s_m, s_v = _adamw_small(small_w, small_g, small_m, small_v)

    order = ("w_ada", "b_ada", "norm1_g", "w_in", "q_norm_a", "k_norm_a", "sink_b", "w_branch", "w_out", "norm2_g",
             "w_mlp_in", "w_mlp_out", "final_g")
    like = dict(w_ada=w_ada, b_ada=b_ada, norm1_g=norm1_g, w_in=w_in, q_norm_a=q_norm_a, k_norm_a=k_norm_a, sink_b=sink_b,
                w_branch=w_branch, w_out=w_out, norm2_g=norm2_g, w_mlp_in=w_mlp_in, w_mlp_out=w_mlp_out, final_g=final_g)
    grad, delta, new_m, new_v = {}, {}, {}, {}
    for nm, res in big_res.items():
        grad[nm], delta[nm], new_m[nm], new_v[nm] = res
    for k, nm in enumerate(small):
        grad[nm], delta[nm], new_m[nm], new_v[nm] = small_g[k], s_d[k], s_m[k], s_v[k]
    outs = [loss, grad_x[None]]
    for group in (grad, delta, new_m, new_v):
        outs += [group[nm].reshape(like[nm].shape) for nm in order]
    return tuple(outs)
```

```python
import jax
import jax.numpy as jnp
from jax import lax
from jax.experimental import pallas as pl
from jax.experimental.pallas import tpu as pltpu

F32 = jnp.float32
BF16 = jnp.bfloat16
MESH = pl.DeviceIdType.MESH
ANY = pl.BlockSpec(memory_space=pl.ANY)

D_MODEL = 1024
HEAD_DIM = 64
Q_HEADS = 8
KV_HEADS = 2
GROUP = Q_HEADS // KV_HEADS
BRANCH_W = Q_HEADS * HEAD_DIM
KV_W = KV_HEADS * HEAD_DIM
IN_W = 2 * (BRANCH_W + 2 * KV_W) + 2 * D_MODEL
QK_W = 2 * (BRANCH_W + 2 * KV_W)
D_FF = 4 * D_MODEL
GRID_W = 64
WINDOW = 128
ROPE_THETA = 10000.0
NORM_EPS = 1e-6
NEG_INF = -1e30
Q_SCALE = HEAD_DIM ** -0.5
N_SHARD = 4
N_DEV = 8
LANES = 128
VMEM_LIMIT = 56 * 1024 * 1024

ADAM_LR = 0.001
ADAM_B1 = 0.9
ADAM_B2 = 0.999
ADAM_EPS = 1e-08
ADAM_WD = 0.01
ADAM_STEP = 10

_call = pl.pallas_call


def _params(sem=None, vmem=VMEM_LIMIT):
    return pltpu.CompilerParams(dimension_semantics=sem, vmem_limit_bytes=vmem)


def _nt(a, b):
    return lax.dot_general(a, b, (((1,), (1,)), ((), ())), preferred_element_type=F32)


def _tn(a, b):
    return lax.dot_general(a, b, (((0,), (0,)), ((), ())), preferred_element_type=F32)


def _nn(a, b):
    return jnp.dot(a, b, preferred_element_type=F32)


def _sigmoid(z):
    return 0.5 * jnp.tanh(0.5 * z) + 0.5


def _rope_tables(s):
    t = jnp.arange(s, dtype=jnp.int32)
    lane = jnp.arange(LANES, dtype=jnp.int32)

    def cos_sin(pos, dim):
        inv = ROPE_THETA ** (-jnp.arange(0, dim, 2, dtype=F32) / dim)
        ang = pos.astype(F32)[:, None] * inv[None, :]
        return jnp.cos(ang), jnp.sin(ang)

    cr, sr = cos_sin(t // GRID_W, HEAD_DIM // 2)
    cc, sc = cos_sin(t % GRID_W, HEAD_DIM // 2)
    cos_a = jnp.tile(jnp.concatenate([cr, cr, cc, cc], axis=1), (1, 2))
    sin_a = jnp.tile(jnp.concatenate([sr, sr, sc, sc], axis=1), (1, 2))
    first_a = (lane % 32) < 16
    c1, s1 = cos_sin(t, HEAD_DIM)
    cos_b = jnp.tile(jnp.concatenate([c1, c1], axis=1), (1, 2))
    sin_b = jnp.tile(jnp.concatenate([s1, s1], axis=1), (1, 2))
    first_b = (lane % 64) < 32
    tabs_a = (cos_a, jnp.where(first_a, -sin_a, 0.0), jnp.where(first_a, 0.0, sin_a))
    tabs_b = (cos_b, jnp.where(first_b, -sin_b, 0.0), jnp.where(first_b, 0.0, sin_b))
    return tabs_a + tabs_b


def _rope(z, cos, s_lo, s_hi, half, sign=1.0):
    up = pltpu.roll(z, LANES - half, 1)
    dn = pltpu.roll(z, half, 1)
    return z * cos + sign * (up * s_lo + dn * s_hi)


def _head_mean(z2, bd):
    hi = z2.astype(BF16)
    lo = (z2 - hi.astype(F32)).astype(BF16)
    return _nn(hi, bd) + _nn(lo, bd)


def _block_diag():
    lane = jnp.arange(LANES)
    return jnp.where((lane[:, None] // HEAD_DIM) == (lane[None, :] // HEAD_DIM), 1.0 / HEAD_DIM, 0.0).astype(BF16)


def _row_spec(tm, width):
    return pl.BlockSpec((tm, width), lambda i: (i, 0))


def _heads_spec(heads, tm):
    return pl.BlockSpec((heads, tm, HEAD_DIM), lambda i: (0, i, 0))


def _full_spec(shape):
    nd = len(shape)
    return pl.BlockSpec(shape, lambda i: (0,) * nd)


def _in_proj(x, mod6, g1, w_in_s, gq, gk, bd, tabs, tm=256):
    s = x.shape[0]

    def body(x_ref, mod_ref, g1_ref, w_ref, gq_ref, gk_ref, bd_ref, ca, la, ha, cb, lb, hb,
             h_ref, qkraw_ref, qa_ref, ka_ref, va_ref, qb_ref, kb_ref, vb_ref, gate_ref):
        xt = x_ref[...]
        r = lax.rsqrt(jnp.mean(xt * xt, axis=-1, keepdims=True) + NORM_EPS)
        h = (xt * r * g1_ref[...]) * (1.0 + mod_ref[1:2, :]) + mod_ref[0:1, :]
        hb16 = h.astype(BF16)
        h_ref[...] = hb16
        proj = jnp.concatenate([_nn(hb16, w_ref[j]) for j in range(N_SHARD)], axis=1)
        qkraw_ref[...] = proj[:, :BRANCH_W + KV_W]
        bdm = bd_ref[...]
        tab_a = (ca[...], la[...], ha[...])
        tab_b = (cb[...], lb[...], hb[...])

        def norm_rope_a(z, gain):
            zn = z * lax.rsqrt(_head_mean(z * z, bdm) + NORM_EPS) * gain
            return _rope(zn, *tab_a, 16)

        def put(ref, first, z):
            zb = z.astype(BF16)
            ref[first] = zb[:, :HEAD_DIM]
            ref[first + 1] = zb[:, HEAD_DIM:]

        for i in range(Q_HEADS // 2):
            put(qa_ref, 2 * i, norm_rope_a(proj[:, LANES * i:LANES * (i + 1)], gq_ref[...]) * Q_SCALE)
        off = BRANCH_W
        put(ka_ref, 0, norm_rope_a(proj[:, off:off + LANES], gk_ref[...]))
        off += KV_W
        def put_v(ref, z):
            zb = z.astype(BF16)
            for hd in range(KV_HEADS):
                ref[hd, :, :HEAD_DIM] = zb[:, HEAD_DIM * hd:HEAD_DIM * (hd + 1)]
                ref[hd, :, HEAD_DIM:] = jnp.ones((tm, HEAD_DIM), BF16)

        put_v(va_ref, proj[:, off:off + LANES])
        off += KV_W
        for i in range(Q_HEADS // 2):
            put(qb_ref, 2 * i, _rope(proj[:, off + LANES * i:off + LANES * (i + 1)], *tab_b, 32) * Q_SCALE)
        off += BRANCH_W
        put(kb_ref, 0, _rope(proj[:, off:off + LANES], *tab_b, 32))
        off += KV_W
        put_v(vb_ref, proj[:, off:off + LANES])
        gate_ref[...] = proj[:, QK_W:].astype(BF16)

    tab_spec = _row_spec(tm, LANES)
    return _call(
        body, name="in_proj", grid=(s // tm,),
        in_specs=[_row_spec(tm, D_MODEL), _full_spec(mod6.shape), _full_spec(g1.shape), _full_spec(w_in_s.shape),
                  _full_spec(gq.shape), _full_spec(gk.shape), _full_spec(bd.shape)] + [tab_spec] * 6,
        out_specs=[_row_spec(tm, D_MODEL), _row_spec(tm, BRANCH_W + KV_W), _heads_spec(Q_HEADS, tm), _heads_spec(KV_HEADS, tm),
                   pl.BlockSpec((KV_HEADS, tm, LANES), lambda i: (0, i, 0)), _heads_spec(Q_HEADS, tm),
                   _heads_spec(KV_HEADS, tm), pl.BlockSpec((KV_HEADS, tm, LANES), lambda i: (0, i, 0)),
                   _row_spec(tm, 2 * D_MODEL)],
        out_shape=[jax.ShapeDtypeStruct((s, D_MODEL), BF16), jax.ShapeDtypeStruct((s, BRANCH_W + KV_W), F32),
                   jax.ShapeDtypeStruct((Q_HEADS, s, HEAD_DIM), BF16), jax.ShapeDtypeStruct((KV_HEADS, s, HEAD_DIM), BF16),
                   jax.ShapeDtypeStruct((KV_HEADS, s, LANES), BF16), jax.ShapeDtypeStruct((Q_HEADS, s, HEAD_DIM), BF16),
                   jax.ShapeDtypeStruct((KV_HEADS, s, HEAD_DIM), BF16), jax.ShapeDtypeStruct((KV_HEADS, s, LANES), BF16),
                   jax.ShapeDtypeStruct((s, 2 * D_MODEL), BF16)],
        compiler_params=_params(("parallel",)),
    )(x, mod6, g1, w_in_s, gq, gk, bd, *tabs)


def _group_specs(s, tq):
    q_spec = pl.BlockSpec((None, GROUP, tq, HEAD_DIM), lambda g, i: (g, 0, i, 0))
    kv_spec = pl.BlockSpec((None, s, HEAD_DIM), lambda g, i: (g, 0, 0))
    col_spec = pl.BlockSpec((None, GROUP, tq, 1), lambda g, i: (g, 0, i, 0))
    return q_spec, kv_spec, col_spec


def _attn_a_fwd(q, k, v1, rider=None, tq=256, tk=2048):
    s = q.shape[1]
    tk = min(tk, s // 2)
    rows = GROUP * tq

    n = s // tk
    assert n >= 2 and n % 2 == 0

    def body(q_ref, k_ref, v_ref, o_ref, oh_ref, lse_ref, s0_ref, s1_ref, p0_ref, p1_ref, m_ref, a_ref, acc_ref):
        s_ref, p_ref = (s0_ref, s1_ref), (p0_ref, p1_ref)
        qq = q_ref[...].reshape(rows, HEAD_DIM)
        m_ref[...] = jnp.full((rows, 1), NEG_INF, F32)
        acc_ref[...] = jnp.zeros((rows, LANES), F32)

        def keys(i):
            return pl.ds(pl.multiple_of(i * tk, tk), tk)

        def scores(i, slot):
            s_ref[slot][...] = _nt(qq, k_ref[keys(i), :])

        def softmax(slot):
            sc = s_ref[slot][...]
            m = m_ref[...]
            mn = jnp.maximum(m, jnp.max(sc, axis=-1, keepdims=True))
            m_ref[...] = mn
            a_ref[...] = jnp.exp(m - mn)
            p_ref[slot][...] = jnp.exp(sc - mn).astype(BF16)

        def weigh(i, slot):
            acc_ref[...] = a_ref[...] * acc_ref[...] + _nn(p_ref[slot][...], v_ref[keys(i), :])

        scores(0, 0)
        softmax(0)
        scores(1, 1)

        def two_steps(j, carry):
            i = 2 * j + 1
            weigh(i - 1, 0)
            softmax(1)
            scores(i + 1, 0)
            weigh(i, 1)
            softmax(0)
            scores(i + 2, 1)
            return carry

        lax.fori_loop(0, (n - 2) // 2, two_steps, 0, unroll=True)
        weigh(n - 2, 0)
        softmax(1)
        weigh(n - 1, 1)
        l = acc_ref[:, HEAD_DIM:HEAD_DIM + 1]
        o = (acc_ref[:, :HEAD_DIM] / l).astype(BF16)
        for g in range(GROUP):
            o_ref[:, HEAD_DIM * g:HEAD_DIM * (g + 1)] = o[tq * g:tq * (g + 1)]
        oh_ref[...] = o.reshape(GROUP, tq, HEAD_DIM)
        lse_ref[...] = (m_ref[...] + jnp.log(l)).reshape(GROUP, tq, 1)

    q_spec, kv_spec, col_spec = _group_specs(s, tq)
    v_spec = pl.BlockSpec((None, s, LANES), lambda g, i: (g, 0, 0))
    return _hosted(
        body, rider, name="attn_a_fwd", grid=(KV_HEADS, s // tq),
        in_specs=[q_spec, kv_spec, v_spec],
        out_specs=[pl.BlockSpec((tq, GROUP * HEAD_DIM), lambda g, i: (i, g)), q_spec, col_spec],
        out_shape=[jax.ShapeDtypeStruct((s, BRANCH_W), BF16), jax.ShapeDtypeStruct((KV_HEADS, GROUP, s, HEAD_DIM), BF16),
                   jax.ShapeDtypeStruct((KV_HEADS, GROUP, s, 1), F32)],
        scratch_shapes=[pltpu.VMEM((rows, tk), F32), pltpu.VMEM((rows, tk), F32), pltpu.VMEM((rows, tk), BF16),
                        pltpu.VMEM((rows, tk), BF16), pltpu.VMEM((rows, 1), F32), pltpu.VMEM((rows, 1), F32),
                        pltpu.VMEM((rows, LANES), F32)],
        args=(q.reshape(KV_HEADS, GROUP, s, HEAD_DIM), k, v1), middle_at=KV_HEADS * (s // tq) // 2)


def _attn_a_bwd(q, k, v1, o, do, lse, rider=None, tq=256, tk=512):
    s = q.shape[1]
    tk = min(tk, s // 2)
    rows = GROUP * tq

    n = s // tk
    assert n >= 2 and n % 2 == 0

    def body(q_ref, k_ref, v_ref, o_ref, do_ref, lse_ref, dq_ref, dk_ref, dv_ref,
             s0_ref, s1_ref, dp0_ref, dp1_ref, p0_ref, p1_ref, ds0_ref, ds1_ref, dq_acc):
        s_ref, dp_ref, p_ref, ds_ref = (s0_ref, s1_ref), (dp0_ref, dp1_ref), (p0_ref, p1_ref), (ds0_ref, ds1_ref)

        @pl.when(pl.program_id(1) == 0)
        def _():
            dk_ref[...] = jnp.zeros_like(dk_ref)
            dv_ref[...] = jnp.zeros_like(dv_ref)

        qq = q_ref[...].reshape(rows, HEAD_DIM)
        dd = do_ref[...].reshape(rows, HEAD_DIM)
        ls = lse_ref[...].reshape(rows, 1)
        dl = jnp.sum(dd.astype(F32) * o_ref[...].reshape(rows, HEAD_DIM).astype(F32), axis=-1, keepdims=True)
        dq_acc[...] = jnp.zeros((rows, HEAD_DIM), F32)

        def keys(i):
            return pl.ds(pl.multiple_of(i * tk, tk), tk)

        def scores(i, slot):
            s_ref[slot][...] = _nt(qq, k_ref[keys(i), :])
            dp_ref[slot][...] = _nt(dd, v_ref[keys(i), :HEAD_DIM])

        def weights(slot):
            p = jnp.exp(s_ref[slot][...] - ls)
            p_ref[slot][...] = p.astype(BF16)
            ds_ref[slot][...] = (p * (dp_ref[slot][...] - dl)).astype(BF16)

        def grads(i, slot):
            dv_ref[keys(i), :] += _tn(p_ref[slot][...], dd)
            dk_ref[keys(i), :] += _tn(ds_ref[slot][...], qq)
            dq_acc[...] += _nn(ds_ref[slot][...], k_ref[keys(i), :])

        scores(0, 0)
        weights(0)
        scores(1, 1)

        def two_steps(j, carry):
            i = 2 * j + 1
            grads(i - 1, 0)
            weights(1)
            scores(i + 1, 0)
            grads(i, 1)
            weights(0)
            scores(i + 2, 1)
            return carry

        lax.fori_loop(0, (n - 2) // 2, two_steps, 0, unroll=True)
        grads(n - 2, 0)
        weights(1)
        grads(n - 1, 1)
        dq_ref[...] = dq_acc[...].astype(BF16).reshape(GROUP, tq, HEAD_DIM)

    q_spec, kv_spec, col_spec = _group_specs(s, tq)
    v_spec = pl.BlockSpec((None, s, LANES), lambda g, i: (g, 0, 0))
    shape4 = (KV_HEADS, GROUP, s, HEAD_DIM)
    tile32, tile16 = pltpu.VMEM((rows, tk), F32), pltpu.VMEM((rows, tk), BF16)
    return _hosted(
        body, rider, name="attn_a_bwd", grid=(KV_HEADS, s // tq),
        in_specs=[q_spec, kv_spec, v_spec, q_spec, q_spec, col_spec],
        out_specs=[q_spec, kv_spec, kv_spec],
        out_shape=[jax.ShapeDtypeStruct(shape4, BF16), jax.ShapeDtypeStruct((KV_HEADS, s, HEAD_DIM), F32),
                   jax.ShapeDtypeStruct((KV_HEADS, s, HEAD_DIM), F32)],
        scratch_shapes=[tile32] * 4 + [tile16] * 4 + [pltpu.VMEM((rows, HEAD_DIM), F32)],
        args=(q.reshape(shape4), k, v1, o.reshape(shape4), do.reshape(shape4), lse))


TQ_B = WINDOW


def _win_keys(tq):
    return tq + 2 * WINDOW


def _window_bias(tq):
    r = jnp.arange(tq, dtype=jnp.int32)[:, None]
    col = jnp.arange(_win_keys(tq), dtype=jnp.int32)[None, :]
    return jnp.stack([jnp.where(jnp.abs(r - col + WINDOW * b) <= WINDOW, 0.0, NEG_INF) for b in range(3)]).astype(F32)


def _band(tq, s):
    win = _win_keys(tq)

    def window(e):
        return pl.ds(pl.multiple_of(jnp.clip(e * tq - WINDOW, 0, s - win), WINDOW), win)

    def bias_index(e):
        return jnp.where(e == 0, 0, jnp.where(e >= s // tq - 1, 2, 1))

    return window, bias_index


def _pair_specs(s, tq):
    pairs = s // (2 * tq)
    cur = lambda g, j: (g, 0, jnp.minimum(j, pairs - 1), 0)
    prev = lambda g, j: (g, 0, jnp.maximum(j - 1, 0), 0)
    tile = lambda width, index: pl.BlockSpec((None, GROUP, 2 * tq, width), index)
    kv_spec = pl.BlockSpec((None, s, HEAD_DIM), lambda g, j: (g, 0, 0))
    v_spec = pl.BlockSpec((None, s, LANES), lambda g, j: (g, 0, 0))
    sink_spec = pl.BlockSpec((None, GROUP * tq, 1), lambda g, j: (g, 0, 0))
    bias_spec = pl.BlockSpec((3, tq, _win_keys(tq)), lambda g, j: (0, 0, 0))
    return tile, cur, prev, kv_spec, v_spec, sink_spec, bias_spec


def _attn_b_fwd(q, k, v1, sink_col, bias, rider=None, tq=TQ_B):
    s = q.shape[1]
    rows = GROUP * tq
    win = _win_keys(tq)
    pairs = s // (2 * tq)
    window, bias_index = _band(tq, s)

    def body(q_ref, k_ref, v_ref, sink_ref, bias_ref, o_ref, oh_ref, lse_ref, s0_ref, s1_ref, p0_ref, p1_ref, m0_ref, m1_ref):
        s_ref, p_ref, m_ref = (s0_ref, s1_ref), (p0_ref, p1_ref), (m0_ref, m1_ref)
        j = pl.program_id(1)

        @pl.when(j == 0)
        def _():
            for ref in (s0_ref, s1_ref, p0_ref, p1_ref, m0_ref, m1_ref):
                ref[...] = jnp.zeros_like(ref)

        def scores(e, slot):
            qq = q_ref[:, pl.ds(slot * tq, tq), :].reshape(rows, HEAD_DIM)
            sc = _nt(qq, k_ref[window(e), :]).reshape(GROUP, tq, win) + bias_ref[bias_index(e)][None]
            s_ref[slot][...] = sc.reshape(rows, win)

        def softmax(slot):
            sc = s_ref[slot][...]
            m = jnp.maximum(jnp.max(sc, axis=-1, keepdims=True), sink_ref[...])
            m_ref[slot][...] = m
            p_ref[slot][...] = jnp.exp(sc - m).astype(BF16)

        def finish(e, slot):
            acc = _nn(p_ref[slot][...], v_ref[window(e), :])
            m = m_ref[slot][...]
            l = acc[:, HEAD_DIM:HEAD_DIM + 1] + jnp.exp(sink_ref[...] - m)
            o = (acc[:, :HEAD_DIM] / l).astype(BF16)
            at = pl.ds(slot * tq, tq)
            for g in range(GROUP):
                o_ref[at, HEAD_DIM * g:HEAD_DIM * (g + 1)] = o[tq * g:tq * (g + 1)]
            oh_ref[:, at, :] = o.reshape(GROUP, tq, HEAD_DIM)
            lse_ref[:, at, :] = (m + jnp.log(l)).reshape(GROUP, tq, 1)

        first = 2 * j
        finish(jnp.maximum(first - 2, 0), 0)
        softmax(1)
        scores(first, 0)
        finish(jnp.maximum(first - 1, 0), 1)
        softmax(0)
        scores(first + 1, 1)

    tile, cur, prev, kv_spec, v_spec, sink_spec, bias_spec = _pair_specs(s, tq)
    tile32, tile16, col = pltpu.VMEM((rows, win), F32), pltpu.VMEM((rows, win), BF16), pltpu.VMEM((rows, 1), F32)
    return _hosted(
        body, rider, name="attn_b_fwd", grid=(KV_HEADS, pairs + 1),
        in_specs=[tile(HEAD_DIM, cur), kv_spec, v_spec, sink_spec, bias_spec],
        out_specs=[pl.BlockSpec((2 * tq, GROUP * HEAD_DIM), lambda g, j: (jnp.maximum(j - 1, 0), g)),
                   tile(HEAD_DIM, prev), tile(1, prev)],
        out_shape=[jax.ShapeDtypeStruct((s, BRANCH_W), BF16), jax.ShapeDtypeStruct((KV_HEADS, GROUP, s, HEAD_DIM), BF16),
                   jax.ShapeDtypeStruct((KV_HEADS, GROUP, s, 1), F32)],
        scratch_shapes=[tile32, tile32, tile16, tile16, col, col],
        args=(q.reshape(KV_HEADS, GROUP, s, HEAD_DIM), k, v1, sink_col, bias))


def _attn_b_bwd(q, k, v1, o, do, lse, sink_col, bias, rider=None, tq=TQ_B):
    s = q.shape[1]
    rows = GROUP * tq
    win = _win_keys(tq)
    pairs = s // (2 * tq)
    window, bias_index = _band(tq, s)

    def body(q_ref, k_ref, v_ref, o_ref, do_ref, lse_ref, sink_ref, bias_ref, dq_ref, dk_ref, dv_ref, dsink_ref,
             s0, s1, dp0, dp1, p0, p1, ds0, ds1, q0, q1, d0, d1, ls0, ls1, dl0, dl1):
        s_ref, dp_ref, p_ref, ds_ref = (s0, s1), (dp0, dp1), (p0, p1), (ds0, ds1)
        q_keep, do_keep, lse_keep, delta_keep = (q0, q1), (d0, d1), (ls0, ls1), (dl0, dl1)
        j = pl.program_id(1)

        @pl.when(j == 0)
        def _():
            for ref in (dk_ref, dv_ref, dsink_ref, s0, s1, dp0, dp1, p0, p1, ds0, ds1, q0, q1, d0, d1, ls0, ls1, dl0, dl1):
                ref[...] = jnp.zeros_like(ref)

        def scores(e, slot):
            at = pl.ds(slot * tq, tq)
            qq = q_ref[:, at, :].reshape(rows, HEAD_DIM)
            dd = do_ref[:, at, :].reshape(rows, HEAD_DIM)
            q_keep[slot][...] = qq
            do_keep[slot][...] = dd
            lse_keep[slot][...] = lse_ref[:, at, :].reshape(rows, 1)
            delta_keep[slot][...] = jnp.sum(dd.astype(F32) * o_ref[:, at, :].reshape(rows, HEAD_DIM).astype(F32), axis=-1,
                                            keepdims=True)
            sc = _nt(qq, k_ref[window(e), :]).reshape(GROUP, tq, win) + bias_ref[bias_index(e)][None]
            s_ref[slot][...] = sc.reshape(rows, win)
            dp_ref[slot][...] = _nt(dd, v_ref[window(e), :HEAD_DIM])

        def weights(slot):
            p = jnp.exp(s_ref[slot][...] - lse_keep[slot][...])
            p_ref[slot][...] = p.astype(BF16)
            ds_ref[slot][...] = (p * (dp_ref[slot][...] - delta_keep[slot][...])).astype(BF16)

        def grads(e, slot, live):
            at = window(e)
            ds = ds_ref[slot][...]
            dv_ref[at, :] += _tn(p_ref[slot][...], do_keep[slot][...])
            dk_ref[at, :] += _tn(ds, q_keep[slot][...])
            dq_ref[:, pl.ds(slot * tq, tq), :] = _nn(ds, k_ref[at, :]).astype(BF16).reshape(GROUP, tq, HEAD_DIM)
            dsk = jnp.exp(sink_ref[...] - lse_keep[slot][...]) * delta_keep[slot][...] * live
            for g in range(GROUP):
                dsink_ref[g:g + 1, :] -= jnp.broadcast_to(jnp.sum(dsk[tq * g:tq * (g + 1)], axis=0, keepdims=True), (1, LANES))

        first = 2 * j
        live = jnp.where(j > 0, 1.0, 0.0)
        grads(jnp.maximum(first - 2, 0), 0, live)
        weights(1)
        scores(first, 0)
        grads(jnp.maximum(first - 1, 0), 1, live)
        weights(0)
        scores(first + 1, 1)

    tile, cur, prev, kv_spec, v_spec, sink_spec, bias_spec = _pair_specs(s, tq)
    dsink_spec = pl.BlockSpec((None, ACC_ROWS, LANES), lambda g, j: (g, 0, 0))
    shape4 = (KV_HEADS, GROUP, s, HEAD_DIM)
    tile32, tile16 = pltpu.VMEM((rows, win), F32), pltpu.VMEM((rows, win), BF16)
    keep, col = pltpu.VMEM((rows, HEAD_DIM), BF16), pltpu.VMEM((rows, 1), F32)
    return _hosted(
        body, rider, name="attn_b_bwd", grid=(KV_HEADS, pairs + 1),
        in_specs=[tile(HEAD_DIM, cur), kv_spec, v_spec, tile(HEAD_DIM, cur), tile(HEAD_DIM, cur), tile(1, cur), sink_spec,
                  bias_spec],
        out_specs=[tile(HEAD_DIM, prev), kv_spec, kv_spec, dsink_spec],
        out_shape=[jax.ShapeDtypeStruct(shape4, BF16), jax.ShapeDtypeStruct((KV_HEADS, s, HEAD_DIM), F32),
                   jax.ShapeDtypeStruct((KV_HEADS, s, HEAD_DIM), F32), jax.ShapeDtypeStruct((KV_HEADS, ACC_ROWS, LANES), F32)],
        scratch_shapes=[tile32] * 4 + [tile16] * 4 + [keep] * 4 + [col] * 4,
        args=(q.reshape(shape4), k, v1, o.reshape(shape4), do.reshape(shape4), lse, sink_col, bias))


def _post_attn(ya, yb, gates, x, mod6, wbr_s, w_out, tm=512):
    s = x.shape[0]

    def body(ya_ref, yb_ref, g_ref, x_ref, mod_ref, wbr_ref, wo_ref, ua_ref, ub_ref, mg_ref, o_ref, x1_ref):
        ya_t, yb_t = ya_ref[...], yb_ref[...]
        ua = jnp.concatenate([_nn(ya_t, wbr_ref[j, 0]) for j in range(N_SHARD)], axis=1)
        ub = jnp.concatenate([_nn(yb_t, wbr_ref[j, 1]) for j in range(N_SHARD)], axis=1)
        ga, gb = g_ref[:, :D_MODEL].astype(F32), g_ref[:, D_MODEL:].astype(F32)
        merged = (_sigmoid(ga) * ua + _sigmoid(gb) * ub).astype(BF16)
        o = _nn(merged, wo_ref[...])
        ua_ref[...] = ua.astype(BF16)
        ub_ref[...] = ub.astype(BF16)
        mg_ref[...] = merged
        o_ref[...] = o.astype(BF16)
        x1_ref[...] = x_ref[...] + mod_ref[2:3, :] * o

    bf = jax.ShapeDtypeStruct((s, D_MODEL), BF16)
    return _call(
        body, name="post_attn", grid=(s // tm,),
        in_specs=[_row_spec(tm, BRANCH_W), _row_spec(tm, BRANCH_W), _row_spec(tm, 2 * D_MODEL), _row_spec(tm, D_MODEL),
                  _full_spec(mod6.shape), _full_spec(wbr_s.shape), _full_spec(w_out.shape)],
        out_specs=[_row_spec(tm, D_MODEL)] * 5,
        out_shape=[bf, bf, bf, bf, jax.ShapeDtypeStruct((s, D_MODEL), F32)],
        compiler_params=_params(("parallel",)),
    )(ya, yb, gates, x, mod6, wbr_s, w_out)


def _mlp_in(x1, mod6, g2, w_mi_s, tm=512):
    s = x1.shape[0]

    def body(x_ref, mod_ref, g_ref, w_ref, h2_ref, a_ref, hid_ref):
        xt = x_ref[...]
        r = lax.rsqrt(jnp.mean(xt * xt, axis=-1, keepdims=True) + NORM_EPS)
        h2 = ((xt * r * g_ref[...]) * (1.0 + mod_ref[4:5, :]) + mod_ref[3:4, :]).astype(BF16)
        h2_ref[...] = h2
        a = jnp.concatenate([_nn(h2, w_ref[j]) for j in range(N_SHARD)], axis=1)
        a_ref[...] = a.astype(BF16)
        hid_ref[...] = jnp.square(jnp.maximum(a, 0.0)).astype(BF16)

    return _call(
        body, name="mlp_in", grid=(s // tm,),
        in_specs=[_row_spec(tm, D_MODEL), _full_spec(mod6.shape), _full_spec(g2.shape), _full_spec(w_mi_s.shape)],
        out_specs=[_row_spec(tm, D_MODEL), _row_spec(tm, D_FF), _row_spec(tm, D_FF)],
        out_shape=[jax.ShapeDtypeStruct((s, D_MODEL), BF16), jax.ShapeDtypeStruct((s, D_FF), BF16),
                   jax.ShapeDtypeStruct((s, D_FF), BF16)],
        compiler_params=_params(("parallel",)),
    )(x1, mod6, g2, w_mi_s)


ACC_ROWS = 8


def _acc_spec():
    return pl.BlockSpec((ACC_ROWS, D_MODEL), lambda i: (0, 0))


def _acc_add(acc_ref, rows):
    @pl.when(pl.program_id(0) == 0)
    def _():
        acc_ref[...] = jnp.zeros_like(acc_ref)

    for r, val in enumerate(rows):
        acc_ref[r:r + 1, :] += jnp.sum(val, axis=0, keepdims=True)


def _mlp_out_loss(hid, x1, a, target, mod6, gf, w_mo, tm=256):
    s = x1.shape[0]

    def body(hid_ref, x_ref, a_ref, t_ref, mod_ref, gf_ref, w_ref, dx2_ref, dm_ref, da_ref, acc_ref):
        m = _nn(hid_ref[...], w_ref[...])
        gate2 = mod_ref[5:6, :]
        x2 = x_ref[...] + gate2 * m
        r = lax.rsqrt(jnp.mean(x2 * x2, axis=-1, keepdims=True) + NORM_EPS)
        xn = x2 * r
        err = xn * gf_ref[...] - t_ref[...]
        dy = err * (1.0 / D_MODEL)
        dxn = dy * gf_ref[...]
        dx2 = r * (dxn - xn * jnp.mean(dxn * xn, axis=-1, keepdims=True))
        dx2_ref[...] = dx2
        dm = (dx2 * gate2).astype(BF16)
        dm_ref[...] = dm
        da_ref[...] = (_nt(dm, w_ref[...]) * (2.0 * jnp.maximum(a_ref[...].astype(F32), 0.0))).astype(BF16)
        _acc_add(acc_ref, [err * err, dy * xn, dx2 * m])

    return _call(
        body, name="mlp_out_loss", grid=(s // tm,),
        in_specs=[_row_spec(tm, D_FF), _row_spec(tm, D_MODEL), _row_spec(tm, D_FF), _row_spec(tm, D_MODEL),
                  _full_spec(mod6.shape), _full_spec(gf.shape), _full_spec(w_mo.shape)],
        out_specs=[_row_spec(tm, D_MODEL), _row_spec(tm, D_MODEL), _row_spec(tm, D_FF), _acc_spec()],
        out_shape=[jax.ShapeDtypeStruct((s, D_MODEL), F32), jax.ShapeDtypeStruct((s, D_MODEL), BF16),
                   jax.ShapeDtypeStruct((s, D_FF), BF16), jax.ShapeDtypeStruct((ACC_ROWS, D_MODEL), F32)],
        compiler_params=_params(("arbitrary",)),
    )(hid, x1, a, target, mod6, gf, w_mo)


def _norm_bwd(dh, xt, gain, scale):
    r = lax.rsqrt(jnp.mean(xt * xt, axis=-1, keepdims=True) + NORM_EPS)
    xn = xt * r
    dxn = dh * (gain * (1.0 + scale))
    dx = r * (dxn - xn * jnp.mean(dxn * xn, axis=-1, keepdims=True))
    return dx, [dh, dh * xn * gain, dh * xn * (1.0 + scale)]


def _mlp_bwd(da, x1, dx2, o, mod6, g2, w_mi_s, rider=None, tm=512):
    s = x1.shape[0]

    def body(da_ref, x_ref, dx2_ref, o_ref, mod_ref, g_ref, w_ref, dx1_ref, do_ref, acc_ref):
        dh2 = _nt(da_ref[:, :D_MODEL], w_ref[0])
        for j in range(1, N_SHARD):
            dh2 += _nt(da_ref[:, D_MODEL * j:D_MODEL * (j + 1)], w_ref[j])
        dx, sums = _norm_bwd(dh2, x_ref[...], g_ref[...], mod_ref[4:5, :])
        dx1 = dx2_ref[...] + dx
        dx1_ref[...] = dx1
        do_ref[...] = (dx1 * mod_ref[2:3, :]).astype(BF16)
        _acc_add(acc_ref, sums + [dx1 * o_ref[...].astype(F32)])

    return _hosted(
        body, rider, name="mlp_bwd", grid=(s // tm,),
        in_specs=[_row_spec(tm, D_FF), _row_spec(tm, D_MODEL), _row_spec(tm, D_MODEL), _row_spec(tm, D_MODEL),
                  _full_spec(mod6.shape), _full_spec(g2.shape), _full_spec(w_mi_s.shape)],
        out_specs=[_row_spec(tm, D_MODEL), _row_spec(tm, D_MODEL), _acc_spec()],
        out_shape=[jax.ShapeDtypeStruct((s, D_MODEL), F32), jax.ShapeDtypeStruct((s, D_MODEL), BF16),
                   jax.ShapeDtypeStruct((ACC_ROWS, D_MODEL), F32)],
        args=(da, x1, dx2, o, mod6, g2, w_mi_s))


def _merge_bwd(do, gates, ua, ub, w_out, wbr_s, rider=None, tm=512):
    s = do.shape[0]

    def body(do_ref, g_ref, ua_ref, ub_ref, wo_ref, wbr_ref, dua_ref, dub_ref, dg_ref, doa_ref, dob_ref):
        dmerged = _nt(do_ref[...], wo_ref[...])
        for b, (u_ref, du_ref, dy_ref) in enumerate(((ua_ref, dua_ref, doa_ref), (ub_ref, dub_ref, dob_ref))):
            sg = _sigmoid(g_ref[:, D_MODEL * b:D_MODEL * (b + 1)].astype(F32))
            du = (dmerged * sg).astype(BF16)
            du_ref[...] = du
            dg_ref[:, D_MODEL * b:D_MODEL * (b + 1)] = (dmerged * u_ref[...].astype(F32) * sg * (1.0 - sg)).astype(BF16)
            w = BRANCH_W // 2
            dy = _nt(du[:, :w], wbr_ref[0, b])
            for j in range(1, N_SHARD):
                dy += _nt(du[:, w * j:w * (j + 1)], wbr_ref[j, b])
            dyb = dy.astype(BF16)
            for h in range(Q_HEADS):
                dy_ref[h] = dyb[:, HEAD_DIM * h:HEAD_DIM * (h + 1)]

    bf = jax.ShapeDtypeStruct((s, D_MODEL), BF16)
    heads = jax.ShapeDtypeStruct((Q_HEADS, s, HEAD_DIM), BF16)
    return _hosted(
        body, rider, name="merge_bwd", grid=(s // tm,),
        in_specs=[_row_spec(tm, D_MODEL), _row_spec(tm, 2 * D_MODEL), _row_spec(tm, D_MODEL), _row_spec(tm, D_MODEL),
                  _full_spec(w_out.shape), _full_spec(wbr_s.shape)],
        out_specs=[_row_spec(tm, D_MODEL), _row_spec(tm, D_MODEL), _row_spec(tm, 2 * D_MODEL),
                   _heads_spec(Q_HEADS, tm), _heads_spec(Q_HEADS, tm)],
        out_shape=[bf, bf, jax.ShapeDtypeStruct((s, 2 * D_MODEL), BF16), heads, heads],
        args=(do, gates, ua, ub, w_out, wbr_s))


def _qk_bwd(dqa, dka, dva, dqb, dkb, dvb, qkraw, dgates, gq, gk, bd, tabs, rider=None, tm=512):
    s = qkraw.shape[0]

    def body(dqa_ref, dka_ref, dva_ref, dqb_ref, dkb_ref, dvb_ref, raw_ref, dg_ref, gq_ref, gk_ref, bd_ref,
             ca, la, ha, cb, lb, hb, dp_ref, acc_ref, pair_ref):
        bdm = bd_ref[...]
        tab_a = (ca[...], la[...], ha[...])
        tab_b = (cb[...], lb[...], hb[...])

        def pair(ref, first):
            pair_ref[:, :HEAD_DIM] = ref[first].astype(F32)
            pair_ref[:, HEAD_DIM:] = ref[first + 1].astype(F32)
            return pair_ref[...]

        def norm_rope_a_bwd(dz, raw, gain):
            dzn = _rope(dz, *tab_a, 16, sign=-1.0)
            rinv = lax.rsqrt(_head_mean(raw * raw, bdm) + NORM_EPS)
            zhat = raw * rinv
            dzhat = dzn * gain
            return rinv * (dzhat - zhat * _head_mean(dzhat * zhat, bdm)), dzn * zhat

        gq_rows = jnp.zeros((tm, LANES), F32)
        for i in range(Q_HEADS // 2):
            at = slice(LANES * i, LANES * (i + 1))
            draw, gsum = norm_rope_a_bwd(pair(dqa_ref, 2 * i) * Q_SCALE, raw_ref[:, at], gq_ref[...])
            dp_ref[:, at] = draw.astype(BF16)
            gq_rows += gsum
        off = BRANCH_W
        draw, gk_rows = norm_rope_a_bwd(pair(dka_ref, 0), raw_ref[:, off:off + LANES], gk_ref[...])
        dp_ref[:, off:off + LANES] = draw.astype(BF16)
        off += KV_W
        dp_ref[:, off:off + LANES] = pair(dva_ref, 0).astype(BF16)
        off += KV_W
        for i in range(Q_HEADS // 2):
            dz = _rope(pair(dqb_ref, 2 * i) * Q_SCALE, *tab_b, 32, sign=-1.0)
            dp_ref[:, off + LANES * i:off + LANES * (i + 1)] = dz.astype(BF16)
        off += BRANCH_W
        dp_ref[:, off:off + LANES] = _rope(pair(dkb_ref, 0), *tab_b, 32, sign=-1.0).astype(BF16)
        off += KV_W
        dp_ref[:, off:off + LANES] = pair(dvb_ref, 0).astype(BF16)
        dp_ref[:, QK_W:] = dg_ref[...]

        @pl.when(pl.program_id(0) == 0)
        def _():
            acc_ref[...] = jnp.zeros_like(acc_ref)

        acc_ref[0:1, :] += jnp.sum(gq_rows, axis=0, keepdims=True)
        acc_ref[1:2, :] += jnp.sum(gk_rows, axis=0, keepdims=True)

    tab_spec = _row_spec(tm, LANES)
    return _hosted(
        body, rider, name="qk_bwd", grid=(s // tm,),
        in_specs=[_heads_spec(Q_HEADS, tm), _heads_spec(KV_HEADS, tm), _heads_spec(KV_HEADS, tm),
                  _heads_spec(Q_HEADS, tm), _heads_spec(KV_HEADS, tm), _heads_spec(KV_HEADS, tm),
                  _row_spec(tm, BRANCH_W + KV_W), _row_spec(tm, 2 * D_MODEL),
                  _full_spec(gq.shape), _full_spec(gk.shape), _full_spec(bd.shape)] + [tab_spec] * 6,
        out_specs=[_row_spec(tm, IN_W), pl.BlockSpec((ACC_ROWS, LANES), lambda i: (0, 0))],
        out_shape=[jax.ShapeDtypeStruct((s, IN_W), BF16), jax.ShapeDtypeStruct((ACC_ROWS, LANES), F32)],
        scratch_shapes=[pltpu.VMEM((tm, LANES), F32)],
        args=(dqa, dka, dva, dqb, dkb, dvb, qkraw, dgates, gq, gk, bd, *tabs))


def _in_proj_bwd(dproj, x, dx1, mod6, g1, w_in_s, tm=512):
    s = x.shape[0]
    w = IN_W // N_SHARD

    def body(dp_ref, x_ref, dx1_ref, mod_ref, g_ref, w_ref, gx_ref, acc_ref):
        dh = _nt(dp_ref[:, :w], w_ref[0])
        for j in range(1, N_SHARD):
            dh += _nt(dp_ref[:, w * j:w * (j + 1)], w_ref[j])
        dx, sums = _norm_bwd(dh, x_ref[...], g_ref[...], mod_ref[1:2, :])
        gx_ref[...] = dx1_ref[...] + dx
        _acc_add(acc_ref, sums)

    return _call(
        body, name="in_proj_bwd", grid=(s // tm,),
        in_specs=[_row_spec(tm, IN_W), _row_spec(tm, D_MODEL), _row_spec(tm, D_MODEL),
                  _full_spec(mod6.shape), _full_spec(g1.shape), _full_spec(w_in_s.shape)],
        out_specs=[_row_spec(tm, D_MODEL), _acc_spec()],
        out_shape=[jax.ShapeDtypeStruct((s, D_MODEL), F32), jax.ShapeDtypeStruct((ACC_ROWS, D_MODEL), F32)],
        compiler_params=_params(("arbitrary",)),
    )(dproj, x, dx1, mod6, g1, w_in_s)


def _wgrad(name, a, b, out_shape, out_spec, tm, tn, tk=4096):
    s, m = a.shape
    n = b.shape[1]
    tk = min(tk, s)
    nk = s // tk

    def body(a_ref, b_ref, o_ref, acc_ref):
        k = pl.program_id(2)

        @pl.when(k == 0)
        def _():
            acc_ref[...] = jnp.zeros_like(acc_ref)

        acc_ref[...] += _tn(a_ref[...], b_ref[...])

        @pl.when(k == nk - 1)
        def _():
            o_ref[...] = acc_ref[...].reshape(o_ref.shape)

    return _call(
        body, name=name, grid=(m // tm, n // tn, nk),
        in_specs=[pl.BlockSpec((tk, tm), lambda i, j, k: (k, i)), pl.BlockSpec((tk, tn), lambda i, j, k: (k, j))],
        out_specs=out_spec, out_shape=jax.ShapeDtypeStruct(out_shape, F32),
        scratch_shapes=[pltpu.VMEM((tm, tn), F32)],
        compiler_params=_params(("parallel", "parallel", "arbitrary")),
    )(a, b)


def _wgrad_branch(ya, yb, dua, dub, tk=2048):
    s = ya.shape[0]
    tk = min(tk, s)
    nk = s // tk
    w = D_MODEL // N_SHARD

    def body(ya_ref, yb_ref, dua_ref, dub_ref, o_ref, acc_ref):
        b, k = pl.program_id(0), pl.program_id(1)

        @pl.when(k == 0)
        def _():
            acc_ref[...] = jnp.zeros_like(acc_ref)

        @pl.when(b == 0)
        def _():
            acc_ref[...] += _tn(ya_ref[...], dua_ref[...])

        @pl.when(b == 1)
        def _():
            acc_ref[...] += _tn(yb_ref[...], dub_ref[...])

        @pl.when(k == nk - 1)
        def _():
            for j in range(N_SHARD):
                o_ref[j] = acc_ref[:, w * j:w * (j + 1)]

    first = lambda width: pl.BlockSpec((tk, width), lambda b, k: (k * (1 - b), 0))
    second = lambda width: pl.BlockSpec((tk, width), lambda b, k: (k * b, 0))
    return _call(
        body, name="wgrad_branch", grid=(2, nk),
        in_specs=[first(BRANCH_W), second(BRANCH_W), first(D_MODEL), second(D_MODEL)],
        out_specs=pl.BlockSpec((N_SHARD, None, BRANCH_W, w), lambda b, k: (0, b, 0, 0)),
        out_shape=jax.ShapeDtypeStruct((N_SHARD, 2, BRANCH_W, w), F32),
        scratch_shapes=[pltpu.VMEM((BRANCH_W, D_MODEL), F32)],
        compiler_params=_params(("parallel", "arbitrary")),
    )(ya, yb, dua, dub)


def _local_step(x, target, mod6, g1, g2, gf, gq2, gk2, sink, w_in_s, rest, cj=None):
    s = x.shape[0]
    dist = cj is not None
    tabs = _rope_tables(s)
    bd = _block_diag()
    sink_col = jnp.repeat(sink.reshape(KV_HEADS, GROUP, 1), TQ_B, axis=1).reshape(KV_HEADS, GROUP * TQ_B, 1)
    shard = D_MODEL // N_SHARD

    h, qkraw, qa, ka, va, qb, kb, vb, gates = _in_proj(x, mod6, g1, w_in_s, gq2, gk2, bd, tabs)
    bias = _window_bias(TQ_B)
    (yb, yb_heads, lse_b), _ = _attn_b_fwd(qb, kb, vb, sink_col, bias)
    (ya, ya_heads, lse_a), gathered = _attn_a_fwd(qa, ka, va, rider=_gather_rider(rest) if dist else None)
    wbr_s, w_out, w_mi_s, w_mo = gathered if dist else rest
    wbr_s = wbr_s.reshape(N_SHARD, 2, BRANCH_W, shard)
    w_out = w_out.reshape(D_MODEL, D_MODEL)
    w_mo = w_mo.reshape(D_FF, D_MODEL)
    ua, ub, merged, o, x1 = _post_attn(ya, yb, gates, x, mod6, wbr_s, w_out)
    h2, a, hid = _mlp_in(x1, mod6, g2, w_mi_s)
    dx2, dm, da, acc_out = _mlp_out_loss(hid, x1, a, target, mod6, gf, w_mo)

    g_w_mo = _wgrad("wgrad_mlp_out", hid, dm, (D_FF, D_MODEL), pl.BlockSpec((D_MODEL, D_MODEL), lambda i, j, k: (i, 0)),
                    D_MODEL, D_MODEL).reshape(N_SHARD, D_MODEL, D_MODEL)
    g_w_mi = _wgrad("wgrad_mlp_in", h2, da, (N_SHARD, D_MODEL, D_MODEL),
                    pl.BlockSpec((None, D_MODEL, D_MODEL), lambda i, j, k: (j, i, 0)), D_MODEL, D_MODEL)
    mlp = _Reduction(("mlp_out", "mlp_in"), (g_w_mo, g_w_mi), cj)
    (dx1, do, acc_mlp), got = _mlp_bwd(da, x1, dx2, o, mod6, g2, w_mi_s, rider=mlp.swap() if dist else None)
    (dua, dub, dgates, doa, dob), _ = _merge_bwd(do, gates, ua, ub, w_out, wbr_s)
    g_w_out = _wgrad("wgrad_out", merged, do, (D_MODEL, D_MODEL), pl.BlockSpec((D_MODEL, D_MODEL), lambda i, j, k: (i, 0)),
                     D_MODEL, D_MODEL).reshape(N_SHARD, shard, D_MODEL)
    g_wbr = _wgrad_branch(ya, yb, dua, dub)
    out = _Reduction(("out", "branch"), (g_w_out, g_wbr.reshape(N_SHARD, 2 * BRANCH_W, shard)), cj)
    (dqa, dka, dva), landings = _attn_a_bwd(qa, ka, va, ya_heads, doa, lse_a,
                                            rider=_riders(mlp.add(got), out.swap()) if dist else None)
    (dqb, dkb, dvb, dsink), joined = _attn_b_bwd(qb, kb, vb, yb_heads, dob, lse_b, sink_col, bias,
                                                 rider=mlp.total(landings[:2]) if dist else None)
    heads = (Q_HEADS, s, HEAD_DIM)
    (dproj, acc_qk), landed = _qk_bwd(dqa.reshape(heads), dka, dva, dqb.reshape(heads), dkb, dvb, qkraw, dgates, gq2, gk2, bd,
                                      tabs, rider=out.add(landings[2:]) if dist else None)
    w = IN_W // N_SHARD
    g_w_in = _wgrad("wgrad_in", h, dproj, (N_SHARD, D_MODEL, w), pl.BlockSpec((None, D_MODEL, w), lambda i, j, k: (j, i, 0)),
                    D_MODEL, w)
    grad_x, acc_in = _in_proj_bwd(dproj, x, dx1, mod6, g1, w_in_s)
    accs = (acc_out, acc_mlp, acc_in, acc_qk, dsink)
    if not dist:
        return grad_x, (g_w_in, g_wbr, g_w_out, g_w_mi, g_w_mo), accs
    r_mo, r_mi = joined
    return grad_x, (_Reduction(("in",), (g_w_in,), cj), out.total(landed), r_mi, r_mo), accs


def _me():
    return lax.axis_index("x"), lax.axis_index("y"), lax.axis_index("c")


def _peer(d):
    x, y, c = _me()
    return (1 - x if d & 4 else x, 1 - y if d & 2 else y, 1 - c if d & 1 else c)


def _dev_index(p):
    return 4 * p[0] + 2 * p[1] + p[2]


def _chip_index(p):
    return 2 * p[0] + p[1]


def _remote(src, dst, send_sem, recv_sem, to):
    return pltpu.make_async_remote_copy(src_ref=src, dst_ref=dst, send_sem=send_sem, recv_sem=recv_sem,
                                        device_id=to, device_id_type=MESH)


SLOT_ROWS = 8


def _ada_fwd(c, w_ada, b4, rider=None):
    cols = w_ada.shape[1]

    def body(c_ref, w_ref, b_ref, mod_ref, sc_ref, cbuf, pbuf, mbuf, send1, recv1, send2, recv2, launch=None):
        me = _me()
        mine, chip = _dev_index(me), _chip_index(me)
        cbuf[mine] = jnp.broadcast_to(c_ref[...], (SLOT_ROWS, D_MODEL))
        gather = [_remote(cbuf.at[mine], cbuf.at[mine], send1.at[d - 1], recv1.at[d - 1], _peer(d)) for d in range(1, N_DEV)]
        for cp in gather:
            cp.start()
        if launch is not None:
            launch()
        for d in range(1, N_DEV):
            _remote(cbuf.at[mine], cbuf.at[_dev_index(_peer(d))], send1.at[d - 1], recv1.at[d - 1], _peer(d)).wait_recv()
        call = cbuf[...].reshape(N_DEV * SLOT_ROWS, D_MODEL)
        sc = call * _sigmoid(call)
        for s in range(N_DEV):
            sc_ref[s:s + 1, :] = sc[SLOT_ROWS * s:SLOT_ROWS * s + 1]
        part = _nn(sc.astype(BF16), w_ref[...].astype(BF16)) + b_ref[pl.ds(chip, 1), :]
        pbuf[...] = part.reshape(N_DEV, SLOT_ROWS, cols)
        mbuf[chip] = pbuf[mine]
        spread = [_remote(pbuf.at[_dev_index(_peer(d))], mbuf.at[chip], send2.at[d // 2 - 1], recv2.at[d // 2 - 1], _peer(d))
                  for d in (2, 4, 6)]
        for cp in spread:
            cp.start()
        for d in (2, 4, 6):
            _remote(pbuf.at[mine], mbuf.at[_chip_index(_peer(d))], send2.at[d // 2 - 1], recv2.at[d // 2 - 1],
                    _peer(d)).wait_recv()
        half = D_MODEL // 2
        for p in range(2 * 6):
            col = half * p
            mod_ref[p // 2:p // 2 + 1, half * (p % 2):half * (p % 2 + 1)] = mbuf[col // cols, 0:1, col % cols:col % cols + half]
        for cp in gather + spread:
            cp.wait_send()

    vm = pl.BlockSpec(memory_space=pltpu.VMEM)
    return _hosted(
        body, rider, name="ada_fwd", grid=(), in_specs=[vm, vm, vm], out_specs=[vm, vm],
        out_shape=[jax.ShapeDtypeStruct((6, D_MODEL), F32), jax.ShapeDtypeStruct((N_DEV, D_MODEL), F32)],
        scratch_shapes=[pltpu.VMEM((N_DEV, SLOT_ROWS, D_MODEL), F32), pltpu.VMEM((N_DEV, SLOT_ROWS, cols), F32),
                        pltpu.VMEM((N_SHARD, SLOT_ROWS, cols), F32),
                        pltpu.SemaphoreType.DMA((N_DEV - 1,)), pltpu.SemaphoreType.DMA((N_DEV - 1,)),
                        pltpu.SemaphoreType.DMA((N_SHARD - 1,)), pltpu.SemaphoreType.DMA((N_SHARD - 1,))],
        args=(c, w_ada, b4))


PACK_ROWS = 16
PACK_W = 3 * D_MODEL


def _ada_bwd(acc_out, acc_mlp, acc_in, acc_qk, dsink, sc_all, rider=None):
    cols = 6 * D_MODEL // N_SHARD

    def body(out_ref, mlp_ref, in_ref, qk_ref, dsink_ref, sc_ref,
             gwa_ref, gba_ref, gn1_ref, gn2_ref, gf_ref, gq_ref, gk_ref, gs_ref, loss_ref, blk, send, recv, launch=None):
        me = _me()
        mine, chip = _dev_index(me), _chip_index(me)
        blk[mine] = jnp.zeros((PACK_ROWS, PACK_W), F32)
        dmod = (in_ref, 0), (in_ref, 1), (mlp_ref, 3), (mlp_ref, 0), (mlp_ref, 1), (out_ref, 2)
        half = D_MODEL // 2
        for p in range(2 * 6):
            ref, row = dmod[p // 2]
            col = half * p
            blk[mine, col // cols:col // cols + 1, col % cols:col % cols + half] = ref[row:row + 1, half * (p % 2):half * (p % 2 + 1)]
        blk[mine, 4:5, 0:D_MODEL] = in_ref[2:3, :]
        blk[mine, 4:5, D_MODEL:2 * D_MODEL] = mlp_ref[2:3, :]
        blk[mine, 4:5, 2 * D_MODEL:] = out_ref[1:2, :]
        blk[mine, 5:6, 0:LANES] = qk_ref[0:1, :]
        blk[mine, 5:6, LANES:2 * LANES] = qk_ref[1:2, :]
        blk[mine, 6:7, 0:D_MODEL] = out_ref[0:1, :]
        for g in range(KV_HEADS):
            blk[mine, 8 + GROUP * g:8 + GROUP * (g + 1), 0:LANES] = dsink_ref[g, 0:GROUP, :]
        copies = [_remote(blk.at[mine], blk.at[mine], send.at[d - 1], recv.at[d - 1], _peer(d)) for d in range(1, N_DEV)]
        for cp in copies:
            cp.start()
        if launch is not None:
            launch()
        for d in range(1, N_DEV):
            _remote(blk.at[mine], blk.at[_dev_index(_peer(d))], send.at[d - 1], recv.at[d - 1], _peer(d)).wait_recv()
        tot = blk[0]
        for s in range(1, N_DEV):
            tot = tot + blk[s]
        for j in range(N_SHARD):
            gba_ref[:, cols * j:cols * (j + 1)] = tot[j:j + 1, :cols]
        gn1_ref[...] = tot[4:5, 0:D_MODEL]
        gn2_ref[...] = tot[4:5, D_MODEL:2 * D_MODEL]
        gf_ref[...] = tot[4:5, 2 * D_MODEL:]
        gq_ref[...] = tot[5:6, 0:HEAD_DIM] + tot[5:6, HEAD_DIM:2 * HEAD_DIM]
        gk_ref[...] = tot[5:6, LANES:LANES + HEAD_DIM] + tot[5:6, LANES + HEAD_DIM:2 * LANES]
        sq = tot[8:16, 0:Q_HEADS]
        diag = lax.broadcasted_iota(jnp.int32, sq.shape, 0) == lax.broadcasted_iota(jnp.int32, sq.shape, 1)
        gs_ref[...] = jnp.sum(jnp.where(diag, sq, 0.0), axis=0, keepdims=True)
        half_mse = (0.5 / D_MODEL) * jnp.sum(tot[6:7, 0:D_MODEL], axis=-1, keepdims=True)
        loss_ref[...] = jnp.broadcast_to(half_mse, (1, LANES))
        dm = jnp.concatenate([blk[s, pl.ds(chip, 1), pl.ds(0, cols)] for s in range(N_DEV)], axis=0)
        gwa_ref[...] = _tn(sc_ref[...], dm)
        for cp in copies:
            cp.wait_send()

    vm = pl.BlockSpec(memory_space=pltpu.VMEM)
    row = lambda n: jax.ShapeDtypeStruct((1, n), F32)
    return _hosted(
        body, rider, name="ada_bwd", grid=(), in_specs=[vm] * 6, out_specs=[vm] * 9,
        out_shape=[jax.ShapeDtypeStruct((D_MODEL, cols), F32), row(6 * D_MODEL), row(D_MODEL), row(D_MODEL), row(D_MODEL),
                   row(HEAD_DIM), row(HEAD_DIM), row(Q_HEADS), row(LANES)],
        scratch_shapes=[pltpu.VMEM((N_DEV, PACK_ROWS, PACK_W), F32),
                        pltpu.SemaphoreType.DMA((N_DEV - 1,)), pltpu.SemaphoreType.DMA((N_DEV - 1,))],
        args=(acc_out, acc_mlp, acc_in, acc_qk, dsink, sc_all))


def _cast_weights(ws):
    n = len(ws)

    def body(*refs):
        src, out, tmp, sems = refs[:n], refs[n:2 * n], refs[2 * n:3 * n], refs[3 * n]
        chip = _chip_index(_me())
        copies = []
        for a in range(n):
            tmp[a][...] = src[a][...].astype(BF16)
            cp = pltpu.make_async_copy(tmp[a], out[a].at[chip], sems.at[a])
            cp.start()
            copies.append(cp)
        for cp in copies:
            cp.wait()

    vm = pl.BlockSpec(memory_space=pltpu.VMEM)
    return _call(
        body, name="cast_weights", in_specs=[vm] * n, out_specs=[ANY] * n,
        out_shape=[jax.ShapeDtypeStruct((N_SHARD,) + w.shape, BF16) for w in ws],
        scratch_shapes=[pltpu.VMEM(w.shape, BF16) for w in ws] + [pltpu.SemaphoreType.DMA((n,))],
        compiler_params=_params(),
    )(*ws)


def _half_rows(ref_rows, c):
    half = ref_rows // 2
    return pl.ds(pl.multiple_of(c * half, 8), half)


class _Rider:
    def __init__(self, inputs, out_shape, aliases, n_sems, start, finish, middle=None):
        self.inputs, self.out_shape, self.aliases, self.n_sems = list(inputs), list(out_shape), dict(aliases), n_sems
        self.start, self.finish, self.middle = start, finish, middle


def _riders(*rs):
    ins = [0]
    outs = [0]
    sems = [0]
    for r in rs:
        ins.append(ins[-1] + len(r.inputs))
        outs.append(outs[-1] + len(r.out_shape))
        sems.append(sems[-1] + r.n_sems)

    def phase(which):
        def run(in_refs, out_refs, sem):
            for k, r in enumerate(rs):
                fn = getattr(r, which)
                if fn is not None:
                    fn(in_refs[ins[k]:ins[k + 1]], out_refs[outs[k]:outs[k + 1]], lambda j, base=sems[k]: sem(base + j))
        return run

    aliases = {ins[k] + i: outs[k] + o for k, r in enumerate(rs) for i, o in r.aliases.items()}
    return _Rider([a for r in rs for a in r.inputs], [o for r in rs for o in r.out_shape], aliases, sems[-1],
                  phase("start"), phase("finish"), phase("middle") if any(r.middle for r in rs) else None)


def _hosted(body, rider, *, name, grid, in_specs, out_specs, out_shape, args, scratch_shapes=(), middle_at=None):
    where = dict(grid=grid, compiler_params=_params(("arbitrary",) * len(grid))) if grid else dict(compiler_params=_params())
    if rider is None:
        res = _call(body, name=name, in_specs=in_specs, out_specs=out_specs, out_shape=out_shape,
                    scratch_shapes=list(scratch_shapes), **where)(*args)
        return res, ()
    n_in, n_out, n_scr = len(in_specs), len(out_specs), len(scratch_shapes)
    r_in, r_out = len(rider.inputs), len(rider.out_shape)

    def riding(*refs):
        at = 0
        parts = []
        for size in (n_in, r_in, n_out, r_out, n_scr):
            parts.append(refs[at:at + size])
            at += size
        ins, rider_ins, outs, rider_outs, scratch = parts
        sems = refs[at]

        def sem_at(k):
            return sems.at[k]

        if not grid:
            body(*ins, *outs, *scratch, launch=lambda: rider.start(rider_ins, rider_outs, sem_at))
            if rider.middle is not None:
                rider.middle(rider_ins, rider_outs, sem_at)
            rider.finish(rider_ins, rider_outs, sem_at)
            return
        step = pl.program_id(0)
        for axis in range(1, len(grid)):
            step = step * grid[axis] + pl.program_id(axis)
        steps = 1
        for size in grid:
            steps *= size

        @pl.when(step == 0)
        def _():
            rider.start(rider_ins, rider_outs, sem_at)

        body(*ins, *outs, *scratch)
        if rider.middle is not None:
            @pl.when(step == middle_at)
            def _():
                rider.middle(rider_ins, rider_outs, sem_at)

        @pl.when(step == steps - 1)
        def _():
            rider.finish(rider_ins, rider_outs, sem_at)

    res = _call(
        riding, name=name, in_specs=list(in_specs) + [ANY] * r_in, out_specs=list(out_specs) + [ANY] * r_out,
        out_shape=list(out_shape) + rider.out_shape,
        input_output_aliases={n_in + i: n_out + o for i, o in rider.aliases.items()},
        scratch_shapes=list(scratch_shapes) + [pltpu.SemaphoreType.DMA((rider.n_sems,))], **where,
    )(*args, *rider.inputs)
    return res[:n_out], res[n_out:]


def _alone(name, rider):
    n_in, n_out = len(rider.inputs), len(rider.out_shape)

    def body(*refs):
        ins, outs, sems = refs[:n_in], refs[n_in:n_in + n_out], refs[n_in + n_out]

        def sem_at(k):
            return sems.at[k]

        rider.start(ins, outs, sem_at)
        if rider.middle is not None:
            rider.middle(ins, outs, sem_at)
        rider.finish(ins, outs, sem_at)

    return _call(
        body, name=name, in_specs=[ANY] * n_in, out_specs=[ANY] * n_out, out_shape=rider.out_shape,
        input_output_aliases=rider.aliases, scratch_shapes=[pltpu.SemaphoreType.DMA((rider.n_sems,))],
    )(*rider.inputs)


OTHER_CHIPS = (2, 4, 6)


def _gather_rider(stacked):
    n = len(stacked)

    def flights(bufs, sem):
        me = _me()
        chip, sib = _chip_index(me), _peer(1)
        out = []
        for a in range(n):
            mine, theirs = (_half_rows(bufs[a].shape[1], c) for c in (me[2], 1 - me[2]))
            for j, d in enumerate(OTHER_CHIPS):
                k = 3 * a + j
                from_chip = _chip_index(_peer(d))
                own, landed, passed = bufs[a].at[chip, mine], bufs[a].at[from_chip, mine], bufs[a].at[from_chip, theirs]
                out.append((_remote(own, own, sem(k), sem(3 * n + k), _peer(d)),
                            _remote(own, landed, sem(k), sem(3 * n + k), _peer(d)),
                            _remote(landed, landed, sem(6 * n + k), sem(9 * n + k), sib),
                            _remote(passed, passed, sem(6 * n + k), sem(9 * n + k), sib)))
        return out

    def start(ins, outs, sem):
        for send, _, _, _ in flights(outs, sem):
            send.start()

    def middle(ins, outs, sem):
        for _, arrival, pass_on, _ in flights(outs, sem):
            arrival.wait_recv()
            pass_on.start()

    def finish(ins, outs, sem):
        every = flights(outs, sem)
        for _, _, _, passed_to_me in every:
            passed_to_me.wait_recv()
        for send, _, pass_on, _ in every:
            send.wait_send()
            pass_on.wait_send()

    return _Rider(stacked, [jax.ShapeDtypeStruct(w.shape, w.dtype) for w in stacked], {a: a for a in range(n)}, 12 * n,
                  start, finish, middle)


def _swap_rider(grads):
    n = len(grads)

    def copies(ins, outs, sem):
        c = _me()[2]
        return [_remote(ins[a].at[pl.ds(0, N_SHARD), _half_rows(ins[a].shape[1], 1 - c)], outs[a], sem(a), sem(n + a), _peer(1))
                for a in range(n)]

    def start(ins, outs, sem):
        for cp in copies(ins, outs, sem):
            cp.start()

    def finish(ins, outs, sem):
        for cp in copies(ins, outs, sem):
            cp.wait()

    return _Rider(grads, [jax.ShapeDtypeStruct((N_SHARD, g.shape[1] // 2, g.shape[2]), F32) for g in grads], {}, 2 * n,
                  start, finish)


def _row_tile(rows):
    return min(rows, 256)


def _add_halves(name, g, got, cj):
    _, half, cols = got.shape
    tr = _row_tile(half)
    nt = half // tr

    def body(cj_ref, g_ref, got_ref, o_ref):
        o_ref[...] = (g_ref[...] + got_ref[...]).astype(BF16)

    spec = pl.BlockSpec((None, tr, cols), lambda i, s, cj: (s, i, 0))
    return _call(
        body, name=name,
        grid_spec=pltpu.PrefetchScalarGridSpec(
            num_scalar_prefetch=1, grid=(nt, N_SHARD),
            in_specs=[pl.BlockSpec((None, tr, cols), lambda i, s, cj: (s, cj[0] * nt + i, 0)), spec], out_specs=spec),
        out_shape=jax.ShapeDtypeStruct(got.shape, BF16), compiler_params=_params(("parallel", "parallel")),
    )(cj, g, got)


def _scatter_rider(sums):
    n = len(sums)

    def flights(ins, outs, sem):
        chip = _chip_index(_me())
        out = []
        for a in range(n):
            for j, d in enumerate(OTHER_CHIPS):
                k = 3 * a + j
                other = _chip_index(_peer(d))
                out.append((_remote(ins[a].at[other], outs[a].at[chip], sem(k), sem(3 * n + k), _peer(d)),
                            _remote(ins[a].at[chip], outs[a].at[other], sem(k), sem(3 * n + k), _peer(d))))
        return out

    def start(ins, outs, sem):
        for send, _ in flights(ins, outs, sem):
            send.start()

    def finish(ins, outs, sem):
        every = flights(ins, outs, sem)
        for _, arrival in every:
            arrival.wait_recv()
        for send, _ in every:
            send.wait_send()

    return _Rider(sums, [jax.ShapeDtypeStruct(v.shape, v.dtype) for v in sums], {}, 6 * n, start, finish)


def _sum_chips(name, g, got, landed, cj):
    _, half, cols = got.shape
    tr = _row_tile(half)
    nt = half // tr

    def body(cj_ref, g_ref, got_ref, landed_ref, o_ref):
        own = g_ref[...] + got_ref[...]
        total = None
        for s in range(N_SHARD):
            term = jnp.where(cj_ref[1] == s, own, landed_ref[s].astype(F32))
            total = term if total is None else total + term
        o_ref[...] = total

    return _call(
        body, name=name,
        grid_spec=pltpu.PrefetchScalarGridSpec(
            num_scalar_prefetch=1, grid=(nt,),
            in_specs=[pl.BlockSpec((None, tr, cols), lambda i, cj: (cj[1], cj[0] * nt + i, 0)),
                      pl.BlockSpec((None, tr, cols), lambda i, cj: (cj[1], i, 0)),
                      pl.BlockSpec((N_SHARD, tr, cols), lambda i, cj: (0, i, 0))],
            out_specs=pl.BlockSpec((tr, cols), lambda i, cj: (cj[0] * nt + i, 0))),
        out_shape=jax.ShapeDtypeStruct((2 * half, cols), F32), compiler_params=_params(("parallel",)),
    )(cj, g, got, landed)


def _join_rider(shards):
    n = len(shards)

    def flights(bufs, sem):
        c = _me()[2]
        out = []
        for a in range(n):
            mine, theirs = (bufs[a].at[_half_rows(bufs[a].shape[0], cc)] for cc in (c, 1 - c))
            out.append((_remote(mine, mine, sem(a), sem(n + a), _peer(1)), _remote(theirs, theirs, sem(a), sem(n + a), _peer(1))))
        return out

    def start(ins, outs, sem):
        for send, _ in flights(outs, sem):
            send.start()

    def finish(ins, outs, sem):
        for send, arrival in flights(outs, sem):
            arrival.wait_recv()
            send.wait_send()

    return _Rider(shards, [jax.ShapeDtypeStruct(h.shape, F32) for h in shards], {a: a for a in range(n)}, 2 * n, start, finish)


class _Reduction:
    def __init__(self, names, grads, cj):
        self.names, self.grads, self.cj = names, list(grads), cj

    def swap(self):
        return _swap_rider(self.grads)

    def add(self, got):
        self.got = list(got)
        self.sums = [_add_halves("add_halves_" + nm, g, h, self.cj) for nm, g, h in zip(self.names, self.grads, self.got)]
        return _scatter_rider(self.sums)

    def total(self, landed):
        halves = [_sum_chips("sum_chips_" + nm, g, h, l, self.cj)
                  for nm, g, h, l in zip(self.names, self.grads, self.got, landed)]
        return _join_rider(halves)


def _adamw_math(w, g, m, v):
    m = ADAM_B1 * m + (1.0 - ADAM_B1) * g
    v = ADAM_B2 * v + (1.0 - ADAM_B2) * jnp.square(g)
    m_hat = m / (1.0 - ADAM_B1 ** ADAM_STEP)
    v_hat = v / (1.0 - ADAM_B2 ** ADAM_STEP)
    return -ADAM_LR * (m_hat / (jnp.sqrt(v_hat) + ADAM_EPS) + ADAM_WD * w), m, v


def _adamw(name, ws, gs, ms, vs, rider=None):
    n = len(ws)
    rows = ws[0].shape[0]
    tr = _row_tile(rows)

    def body(*refs):
        ins, outs = refs[:4 * n], refs[4 * n:]
        for a in range(n):
            w, g, m, v = (ins[k * n + a][...] for k in range(4))
            outs[a][...], outs[n + a][...], outs[2 * n + a][...] = _adamw_math(w, g, m, v)

    specs = [pl.BlockSpec((tr, w.shape[1]), lambda i: (i, 0)) for w in ws]
    res, riding = _hosted(
        body, rider, name=name, grid=(rows // tr,), in_specs=specs * 4, out_specs=specs * 3,
        out_shape=[jax.ShapeDtypeStruct(w.shape, F32) for w in ws] * 3, args=(*ws, *gs, *ms, *vs))
    return (res[:n], res[n:2 * n], res[2 * n:]), riding


def _adamw_small(ws, gs, ms, vs):
    n = len(ws)

    def body(*refs):
        ins, outs = refs[:4 * n], refs[4 * n:]
        for a in range(n):
            w, g, m, v = (ins[k * n + a][...] for k in range(4))
            outs[a][...], outs[n + a][...], outs[2 * n + a][...] = _adamw_math(w, g, m, v)

    vm = pl.BlockSpec(memory_space=pltpu.VMEM)
    res = _call(
        body, name="adamw_small", in_specs=[vm] * (4 * n), out_specs=[vm] * (3 * n),
        out_shape=[jax.ShapeDtypeStruct(w.shape, F32) for w in ws] * 3, compiler_params=_params(),
    )(*ws, *gs, *ms, *vs)
    return res[:n], res[n:2 * n], res[2 * n:]


def kernel(x, c, w_ada, b_ada, norm1_g, w_in, q_norm_a, k_norm_a, sink_b, w_branch, w_out, norm2_g, w_mlp_in, w_mlp_out, final_g, loss_target, m_w_ada, m_b_ada, m_norm1_g, m_w_in, m_q_norm_a, m_k_norm_a, m_sink_b, m_w_branch, m_w_out, m_norm2_g, m_w_mlp_in, m_w_mlp_out, m_final_g, v_w_ada, v_b_ada, v_norm1_g, v_w_in, v_q_norm_a, v_k_norm_a, v_sink_b, v_w_branch, v_w_out, v_norm2_g, v_w_mlp_in, v_w_mlp_out, v_final_g):
    xi, yi, ci = _me()
    cj = jnp.stack([ci, 2 * xi + yi]).astype(jnp.int32)
    n_cols = 6 * D_MODEL // N_SHARD

    def rows2d(a):
        return a.reshape(-1, a.shape[-1])

    big = (w_in, w_branch, w_out, w_mlp_in, w_mlp_out)
    stacked = _cast_weights([rows2d(w) for w in big])
    (mod6, sc_all), (w_in_s,) = _ada_fwd(c, w_ada[0], b_ada.reshape(N_SHARD, n_cols), rider=_gather_rider(stacked[:1]))
    rest = stacked[1:]

    gq2 = jnp.tile(q_norm_a, (1, 2))
    gk2 = jnp.tile(k_norm_a, (1, 2))
    grad_x, (w_in_red, join_out, g_mi, g_mo), accs = _local_step(
        x[0], loss_target[0], mod6, norm1_g, norm2_g, final_g.reshape(1, D_MODEL), gq2, gk2, sink_b[0], w_in_s, rest, cj)

    (g_w_ada, g_b_ada, g_n1, g_n2, g_f, g_q, g_k, g_s, loss_row), got_in = _ada_bwd(*accs, sc_all, rider=w_in_red.swap())
    loss = loss_row[0, 0]
    moments = dict(w_ada=(m_w_ada, v_w_ada), w_in=(m_w_in, v_w_in), w_branch=(m_w_branch, v_w_branch), w_out=(m_w_out, v_w_out),
                   w_mlp_in=(m_w_mlp_in, v_w_mlp_in), w_mlp_out=(m_w_mlp_out, v_w_mlp_out))
    weights = dict(w_ada=w_ada, w_in=w_in, w_branch=w_branch, w_out=w_out, w_mlp_in=w_mlp_in, w_mlp_out=w_mlp_out)

    def adamw(call, names, grads, rider=None):
        (d, m, v), riding = _adamw(call, [rows2d(weights[nm]) for nm in names], grads,
                                   [rows2d(moments[nm][0]) for nm in names], [rows2d(moments[nm][1]) for nm in names], rider)
        return {nm: (grads[k], d[k], m[k], v[k]) for k, nm in enumerate(names)}, riding

    big_res, landed_in = adamw("adamw_ada_mlp", ("w_ada", "w_mlp_in", "w_mlp_out"), [g_w_ada, g_mi, g_mo], w_in_red.add(got_in))
    g_in, g_out, g_br = _alone("join_in_out_branch", _riders(w_in_red.total(landed_in), join_out))
    big_res.update(adamw("adamw_in_branch", ("w_in", "w_branch"), [g_in, g_br])[0])

    small = ("b_ada", "norm1_g", "q_norm_a", "k_norm_a", "sink_b", "norm2_g", "final_g", "w_out")
    row = lambda a: a.reshape(1, -1)
    small_w = [row(a) for a in (b_ada, norm1_g, q_norm_a, k_norm_a, sink_b, norm2_g, final_g)] + [w_out[0]]
    small_g = [g_b_ada, g_n1, g_q, g_k, g_s, g_n2, g_f, g_out]
    small_m = [row(a) for a in (m_b_ada, m_norm1_g, m_q_norm_a, m_k_norm_a, m_sink_b, m_norm2_g, m_final_g)] + [m_w_out[0]]
    small_v = [row(a) for a in (v_b_ada, v_norm1_g, v_q_norm_a, v_k_norm_a, v_sink_b, v_norm2_g, v_final_g)] + [v_w_out[0]]
    s_d, s_m, s_v = _adamw_small(small_w, small_g, small_m, small_v)

    order = ("w_ada", "b_ada", "norm1_g", "w_in", "q_norm_a", "k_norm_a", "sink_b", "w_branch", "w_out", "norm2_g",
             "w_mlp_in", "w_mlp_out", "final_g")
    like = dict(w_ada=w_ada, b_ada=b_ada, norm1_g=norm1_g, w_in=w_in, q_norm_a=q_norm_a, k_norm_a=k_norm_a, sink_b=sink_b,
                w_branch=w_branch, w_out=w_out, norm2_g=norm2_g, w_mlp_in=w_mlp_in, w_mlp_out=w_mlp_out, final_g=final_g)
    grad, delta, new_m, new_v = {}, {}, {}, {}
    for nm, res in big_res.items():
        grad[nm], delta[nm], new_m[nm], new_v[nm] = res
    for k, nm in enumerate(small):
        grad[nm], delta[nm], new_m[nm], new_v[nm] = small_g[k], s_d[k], s_m[k], s_v[k]
    outs = [loss, grad_x[None]]
    for group in (grad, delta, new_m, new_v):
        outs += [group[nm].reshape(like[nm].shape) for nm in order]
    return tuple(outs)
```

```python
import jax
import jax.numpy as jnp
from jax import lax
from jax.experimental import pallas as pl
from jax.experimental.pallas import tpu as pltpu

F32 = jnp.float32
BF16 = jnp.bfloat16
MESH = pl.DeviceIdType.MESH
ANY = pl.BlockSpec(memory_space=pl.ANY)

D_MODEL = 1024
HEAD_DIM = 64
Q_HEADS = 8
KV_HEADS = 2
GROUP = Q_HEADS // KV_HEADS
BRANCH_W = Q_HEADS * HEAD_DIM
KV_W = KV_HEADS * HEAD_DIM
IN_W = 2 * (BRANCH_W + 2 * KV_W) + 2 * D_MODEL
QK_W = 2 * (BRANCH_W + 2 * KV_W)
D_FF = 4 * D_MODEL
GRID_W = 64
WINDOW = 128
ROPE_THETA = 10000.0
NORM_EPS = 1e-6
NEG_INF = -1e30
Q_SCALE = HEAD_DIM ** -0.5
N_SHARD = 4
N_DEV = 8
LANES = 128
VMEM_LIMIT = 56 * 1024 * 1024

ADAM_LR = 0.001
ADAM_B1 = 0.9
ADAM_B2 = 0.999
ADAM_EPS = 1e-08
ADAM_WD = 0.01
ADAM_STEP = 10

_call = pl.pallas_call


def _params(sem=None, vmem=VMEM_LIMIT):
    return pltpu.CompilerParams(dimension_semantics=sem, vmem_limit_bytes=vmem)


def _nt(a, b):
    return lax.dot_general(a, b, (((1,), (1,)), ((), ())), preferred_element_type=F32)


def _tn(a, b):
    return lax.dot_general(a, b, (((0,), (0,)), ((), ())), preferred_element_type=F32)


def _nn(a, b):
    return jnp.dot(a, b, preferred_element_type=F32)


def _sigmoid(z):
    return 0.5 * jnp.tanh(0.5 * z) + 0.5


def _rope_tables(s):
    t = jnp.arange(s, dtype=jnp.int32)
    lane = jnp.arange(LANES, dtype=jnp.int32)

    def cos_sin(pos, dim):
        inv = ROPE_THETA ** (-jnp.arange(0, dim, 2, dtype=F32) / dim)
        ang = pos.astype(F32)[:, None] * inv[None, :]
        return jnp.cos(ang), jnp.sin(ang)

    cr, sr = cos_sin(t // GRID_W, HEAD_DIM // 2)
    cc, sc = cos_sin(t % GRID_W, HEAD_DIM // 2)
    cos_a = jnp.tile(jnp.concatenate([cr, cr, cc, cc], axis=1), (1, 2))
    sin_a = jnp.tile(jnp.concatenate([sr, sr, sc, sc], axis=1), (1, 2))
    first_a = (lane % 32) < 16
    c1, s1 = cos_sin(t, HEAD_DIM)
    cos_b = jnp.tile(jnp.concatenate([c1, c1], axis=1), (1, 2))
    sin_b = jnp.tile(jnp.concatenate([s1, s1], axis=1), (1, 2))
    first_b = (lane % 64) < 32
    tabs_a = (cos_a, jnp.where(first_a, -sin_a, 0.0), jnp.where(first_a, 0.0, sin_a))
    tabs_b = (cos_b, jnp.where(first_b, -sin_b, 0.0), jnp.where(first_b, 0.0, sin_b))
    return tabs_a + tabs_b


def _rope(z, cos, s_lo, s_hi, half, sign=1.0):
    up = pltpu.roll(z, LANES - half, 1)
    dn = pltpu.roll(z, half, 1)
    return z * cos + sign * (up * s_lo + dn * s_hi)


def _head_mean(z2, bd):
    hi = z2.astype(BF16)
    lo = (z2 - hi.astype(F32)).astype(BF16)
    return _nn(hi, bd) + _nn(lo, bd)


def _block_diag():
    lane = jnp.arange(LANES)
    return jnp.where((lane[:, None] // HEAD_DIM) == (lane[None, :] // HEAD_DIM), 1.0 / HEAD_DIM, 0.0).astype(BF16)


def _row_spec(tm, width):
    return pl.BlockSpec((tm, width), lambda i: (i, 0))


def _heads_spec(heads, tm):
    return pl.BlockSpec((heads, tm, HEAD_DIM), lambda i: (0, i, 0))


def _full_spec(shape):
    nd = len(shape)
    return pl.BlockSpec(shape, lambda i: (0,) * nd)


def _in_proj(x, mod6, g1, w_in_s, gq, gk, bd, tabs, tm=512):
    s = x.shape[0]

    def body(x_ref, mod_ref, g1_ref, w_ref, gq_ref, gk_ref, bd_ref, ca, la, ha, cb, lb, hb,
             h_ref, qkraw_ref, qa_ref, ka_ref, va_ref, qb_ref, kb_ref, vb_ref, gate_ref):
        xt = x_ref[...]
        r = lax.rsqrt(jnp.mean(xt * xt, axis=-1, keepdims=True) + NORM_EPS)
        h = (xt * r * g1_ref[...]) * (1.0 + mod_ref[1:2, :]) + mod_ref[0:1, :]
        hb16 = h.astype(BF16)
        h_ref[...] = hb16
        proj = jnp.concatenate([_nn(hb16, w_ref[j]) for j in range(N_SHARD)], axis=1)
        qkraw_ref[...] = proj[:, :BRANCH_W + KV_W]
        bdm = bd_ref[...]
        tab_a = (ca[...], la[...], ha[...])
        tab_b = (cb[...], lb[...], hb[...])

        def norm_rope_a(z, gain):
            zn = z * lax.rsqrt(_head_mean(z * z, bdm) + NORM_EPS) * gain
            return _rope(zn, *tab_a, 16)

        def put(ref, first, z):
            zb = z.astype(BF16)
            ref[first] = zb[:, :HEAD_DIM]
            ref[first + 1] = zb[:, HEAD_DIM:]

        for i in range(Q_HEADS // 2):
            put(qa_ref, 2 * i, norm_rope_a(proj[:, LANES * i:LANES * (i + 1)], gq_ref[...]) * Q_SCALE)
        off = BRANCH_W
        put(ka_ref, 0, norm_rope_a(proj[:, off:off + LANES], gk_ref[...]))
        off += KV_W
        def put_v(ref, z):
            zb = z.astype(BF16)
            for hd in range(KV_HEADS):
                ref[hd, :, :HEAD_DIM] = zb[:, HEAD_DIM * hd:HEAD_DIM * (hd + 1)]
                ref[hd, :, HEAD_DIM:] = jnp.ones((tm, HEAD_DIM), BF16)

        put_v(va_ref, proj[:, off:off + LANES])
        off += KV_W
        for i in range(Q_HEADS // 2):
            put(qb_ref, 2 * i, _rope(proj[:, off + LANES * i:off + LANES * (i + 1)], *tab_b, 32) * Q_SCALE)
        off += BRANCH_W
        put(kb_ref, 0, _rope(proj[:, off:off + LANES], *tab_b, 32))
        off += KV_W
        put_v(vb_ref, proj[:, off:off + LANES])
        gate_ref[...] = proj[:, QK_W:].astype(BF16)

    tab_spec = _row_spec(tm, LANES)
    return _call(
        body, name="in_proj", grid=(s // tm,),
        in_specs=[_row_spec(tm, D_MODEL), _full_spec(mod6.shape), _full_spec(g1.shape), _full_spec(w_in_s.shape),
                  _full_spec(gq.shape), _full_spec(gk.shape), _full_spec(bd.shape)] + [tab_spec] * 6,
        out_specs=[_row_spec(tm, D_MODEL), _row_spec(tm, BRANCH_W + KV_W), _heads_spec(Q_HEADS, tm), _heads_spec(KV_HEADS, tm),
                   pl.BlockSpec((KV_HEADS, tm, LANES), lambda i: (0, i, 0)), _heads_spec(Q_HEADS, tm),
                   _heads_spec(KV_HEADS, tm), pl.BlockSpec((KV_HEADS, tm, LANES), lambda i: (0, i, 0)),
                   _row_spec(tm, 2 * D_MODEL)],
        out_shape=[jax.ShapeDtypeStruct((s, D_MODEL), BF16), jax.ShapeDtypeStruct((s, BRANCH_W + KV_W), F32),
                   jax.ShapeDtypeStruct((Q_HEADS, s, HEAD_DIM), BF16), jax.ShapeDtypeStruct((KV_HEADS, s, HEAD_DIM), BF16),
                   jax.ShapeDtypeStruct((KV_HEADS, s, LANES), BF16), jax.ShapeDtypeStruct((Q_HEADS, s, HEAD_DIM), BF16),
                   jax.ShapeDtypeStruct((KV_HEADS, s, HEAD_DIM), BF16), jax.ShapeDtypeStruct((KV_HEADS, s, LANES), BF16),
                   jax.ShapeDtypeStruct((s, 2 * D_MODEL), BF16)],
        compiler_params=_params(("parallel",)),
    )(x, mod6, g1, w_in_s, gq, gk, bd, *tabs)


def _group_specs(s, tq):
    q_spec = pl.BlockSpec((None, GROUP, tq, HEAD_DIM), lambda g, i: (g, 0, i, 0))
    kv_spec = pl.BlockSpec((None, s, HEAD_DIM), lambda g, i: (g, 0, 0))
    col_spec = pl.BlockSpec((None, GROUP, tq, 1), lambda g, i: (g, 0, i, 0))
    return q_spec, kv_spec, col_spec


def _attn_a_fwd(q, k, v1, rider=None, tq=256, tk=2048):
    s = q.shape[1]
    tk = min(tk, s // 2)
    rows = GROUP * tq

    n = s // tk
    assert n >= 2 and n % 2 == 0

    def body(q_ref, k_ref, v_ref, o_ref, oh_ref, lse_ref, s0_ref, s1_ref, p0_ref, p1_ref, m_ref, a_ref, acc_ref):
        s_ref, p_ref = (s0_ref, s1_ref), (p0_ref, p1_ref)
        qq = q_ref[...].reshape(rows, HEAD_DIM)
        m_ref[...] = jnp.full((rows, 1), NEG_INF, F32)
        acc_ref[...] = jnp.zeros((rows, LANES), F32)

        def keys(i):
            return pl.ds(pl.multiple_of(i * tk, tk), tk)

        def scores(i, slot):
            s_ref[slot][...] = _nt(qq, k_ref[keys(i), :])

        def softmax(slot):
            sc = s_ref[slot][...]
            m = m_ref[...]
            mn = jnp.maximum(m, jnp.max(sc, axis=-1, keepdims=True))
            m_ref[...] = mn
            a_ref[...] = jnp.exp(m - mn)
            p_ref[slot][...] = jnp.exp(sc - mn).astype(BF16)

        def weigh(i, slot):
            acc_ref[...] = a_ref[...] * acc_ref[...] + _nn(p_ref[slot][...], v_ref[keys(i), :])

        scores(0, 0)
        softmax(0)
        scores(1, 1)

        def two_steps(j, carry):
            i = 2 * j + 1
            weigh(i - 1, 0)
            softmax(1)
            scores(i + 1, 0)
            weigh(i, 1)
            softmax(0)
            scores(i + 2, 1)
            return carry

        lax.fori_loop(0, (n - 2) // 2, two_steps, 0, unroll=True)
        weigh(n - 2, 0)
        softmax(1)
        weigh(n - 1, 1)
        l = acc_ref[:, HEAD_DIM:HEAD_DIM + 1]
        o = (acc_ref[:, :HEAD_DIM] / l).astype(BF16)
        for g in range(GROUP):
            o_ref[:, HEAD_DIM * g:HEAD_DIM * (g + 1)] = o[tq * g:tq * (g + 1)]
        oh_ref[...] = o.reshape(GROUP, tq, HEAD_DIM)
        lse_ref[...] = (m_ref[...] + jnp.log(l)).reshape(GROUP, tq, 1)

    q_spec, kv_spec, col_spec = _group_specs(s, tq)
    v_spec = pl.BlockSpec((None, s, LANES), lambda g, i: (g, 0, 0))
    return _hosted(
        body, rider, name="attn_a_fwd", grid=(KV_HEADS, s // tq),
        in_specs=[q_spec, kv_spec, v_spec],
        out_specs=[pl.BlockSpec((tq, GROUP * HEAD_DIM), lambda g, i: (i, g)), q_spec, col_spec],
        out_shape=[jax.ShapeDtypeStruct((s, BRANCH_W), BF16), jax.ShapeDtypeStruct((KV_HEADS, GROUP, s, HEAD_DIM), BF16),
                   jax.ShapeDtypeStruct((KV_HEADS, GROUP, s, 1), F32)],
        scratch_shapes=[pltpu.VMEM((rows, tk), F32), pltpu.VMEM((rows, tk), F32), pltpu.VMEM((rows, tk), BF16),
                        pltpu.VMEM((rows, tk), BF16), pltpu.VMEM((rows, 1), F32), pltpu.VMEM((rows, 1), F32),
                        pltpu.VMEM((rows, LANES), F32)],
        args=(q.reshape(KV_HEADS, GROUP, s, HEAD_DIM), k, v1), middle_at=KV_HEADS * (s // tq) // 2)


def _attn_a_bwd(q, k, v1, o, do, lse, rider=None, tq=256, tk=512):
    s = q.shape[1]
    tk = min(tk, s // 2)
    rows = GROUP * tq

    n = s // tk
    assert n >= 2 and n % 2 == 0

    def body(q_ref, k_ref, v_ref, o_ref, do_ref, lse_ref, dq_ref, dk_ref, dv_ref,
             s0_ref, s1_ref, dp0_ref, dp1_ref, p0_ref, p1_ref, ds0_ref, ds1_ref, dq_acc):
        s_ref, dp_ref, p_ref, ds_ref = (s0_ref, s1_ref), (dp0_ref, dp1_ref), (p0_ref, p1_ref), (ds0_ref, ds1_ref)

        @pl.when(pl.program_id(1) == 0)
        def _():
            dk_ref[...] = jnp.zeros_like(dk_ref)
            dv_ref[...] = jnp.zeros_like(dv_ref)

        qq = q_ref[...].reshape(rows, HEAD_DIM)
        dd = do_ref[...].reshape(rows, HEAD_DIM)
        ls = lse_ref[...].reshape(rows, 1)
        dl = jnp.sum(dd.astype(F32) * o_ref[...].reshape(rows, HEAD_DIM).astype(F32), axis=-1, keepdims=True)
        dq_acc[...] = jnp.zeros((rows, HEAD_DIM), F32)

        def keys(i):
            return pl.ds(pl.multiple_of(i * tk, tk), tk)

        def scores(i, slot):
            s_ref[slot][...] = _nt(qq, k_ref[keys(i), :])
            dp_ref[slot][...] = _nt(dd, v_ref[keys(i), :HEAD_DIM])

        def weights(slot):
            p = jnp.exp(s_ref[slot][...] - ls)
            p_ref[slot][...] = p.astype(BF16)
            ds_ref[slot][...] = (p * (dp_ref[slot][...] - dl)).astype(BF16)

        def grads(i, slot):
            dv_ref[keys(i), :] += _tn(p_ref[slot][...], dd)
            dk_ref[keys(i), :] += _tn(ds_ref[slot][...], qq)
            dq_acc[...] += _nn(ds_ref[slot][...], k_ref[keys(i), :])

        scores(0, 0)
        weights(0)
        scores(1, 1)

        def two_steps(j, carry):
            i = 2 * j + 1
            grads(i - 1, 0)
            weights(1)
            scores(i + 1, 0)
            grads(i, 1)
            weights(0)
            scores(i + 2, 1)
            return carry

        lax.fori_loop(0, (n - 2) // 2, two_steps, 0, unroll=True)
        grads(n - 2, 0)
        weights(1)
        grads(n - 1, 1)
        dq_ref[...] = dq_acc[...].astype(BF16).reshape(GROUP, tq, HEAD_DIM)

    q_spec, kv_spec, col_spec = _group_specs(s, tq)
    v_spec = pl.BlockSpec((None, s, LANES), lambda g, i: (g, 0, 0))
    shape4 = (KV_HEADS, GROUP, s, HEAD_DIM)
    tile32, tile16 = pltpu.VMEM((rows, tk), F32), pltpu.VMEM((rows, tk), BF16)
    return _hosted(
        body, rider, name="attn_a_bwd", grid=(KV_HEADS, s // tq),
        in_specs=[q_spec, kv_spec, v_spec, q_spec, q_spec, col_spec],
        out_specs=[q_spec, kv_spec, kv_spec],
        out_shape=[jax.ShapeDtypeStruct(shape4, BF16), jax.ShapeDtypeStruct((KV_HEADS, s, HEAD_DIM), F32),
                   jax.ShapeDtypeStruct((KV_HEADS, s, HEAD_DIM), F32)],
        scratch_shapes=[tile32] * 4 + [tile16] * 4 + [pltpu.VMEM((rows, HEAD_DIM), F32)],
        args=(q.reshape(shape4), k, v1, o.reshape(shape4), do.reshape(shape4), lse))


TQ_B = WINDOW


def _win_keys(tq):
    return tq + 2 * WINDOW


def _window_bias(tq):
    r = jnp.arange(tq, dtype=jnp.int32)[:, None]
    col = jnp.arange(_win_keys(tq), dtype=jnp.int32)[None, :]
    return jnp.stack([jnp.where(jnp.abs(r - col + WINDOW * b) <= WINDOW, 0.0, NEG_INF) for b in range(3)]).astype(F32)


def _band(tq, s):
    win = _win_keys(tq)

    def window(e):
        return pl.ds(pl.multiple_of(jnp.clip(e * tq - WINDOW, 0, s - win), WINDOW), win)

    def bias_index(e):
        return jnp.where(e == 0, 0, jnp.where(e >= s // tq - 1, 2, 1))

    return window, bias_index


def _pair_specs(s, tq):
    pairs = s // (2 * tq)
    cur = lambda g, j: (g, 0, jnp.minimum(j, pairs - 1), 0)
    prev = lambda g, j: (g, 0, jnp.maximum(j - 1, 0), 0)
    tile = lambda width, index: pl.BlockSpec((None, GROUP, 2 * tq, width), index)
    kv_spec = pl.BlockSpec((None, s, HEAD_DIM), lambda g, j: (g, 0, 0))
    v_spec = pl.BlockSpec((None, s, LANES), lambda g, j: (g, 0, 0))
    sink_spec = pl.BlockSpec((None, GROUP * tq, 1), lambda g, j: (g, 0, 0))
    bias_spec = pl.BlockSpec((3, tq, _win_keys(tq)), lambda g, j: (0, 0, 0))
    return tile, cur, prev, kv_spec, v_spec, sink_spec, bias_spec


def _attn_b_fwd(q, k, v1, sink_col, bias, rider=None, tq=TQ_B):
    s = q.shape[1]
    rows = GROUP * tq
    win = _win_keys(tq)
    pairs = s // (2 * tq)
    window, bias_index = _band(tq, s)

    def body(q_ref, k_ref, v_ref, sink_ref, bias_ref, o_ref, oh_ref, lse_ref, s0_ref, s1_ref, p0_ref, p1_ref, m0_ref, m1_ref):
        s_ref, p_ref, m_ref = (s0_ref, s1_ref), (p0_ref, p1_ref), (m0_ref, m1_ref)
        j = pl.program_id(1)

        @pl.when(j == 0)
        def _():
            for ref in (s0_ref, s1_ref, p0_ref, p1_ref, m0_ref, m1_ref):
                ref[...] = jnp.zeros_like(ref)

        def scores(e, slot):
            qq = q_ref[:, pl.ds(slot * tq, tq), :].reshape(rows, HEAD_DIM)
            sc = _nt(qq, k_ref[window(e), :]).reshape(GROUP, tq, win) + bias_ref[bias_index(e)][None]
            s_ref[slot][...] = sc.reshape(rows, win)

        def softmax(slot):
            sc = s_ref[slot][...]
            m = jnp.maximum(jnp.max(sc, axis=-1, keepdims=True), sink_ref[...])
            m_ref[slot][...] = m
            p_ref[slot][...] = jnp.exp(sc - m).astype(BF16)

        def finish(e, slot):
            acc = _nn(p_ref[slot][...], v_ref[window(e), :])
            m = m_ref[slot][...]
            l = acc[:, HEAD_DIM:HEAD_DIM + 1] + jnp.exp(sink_ref[...] - m)
            o = (acc[:, :HEAD_DIM] / l).astype(BF16)
            at = pl.ds(slot * tq, tq)
            for g in range(GROUP):
                o_ref[at, HEAD_DIM * g:HEAD_DIM * (g + 1)] = o[tq * g:tq * (g + 1)]
            oh_ref[:, at, :] = o.reshape(GROUP, tq, HEAD_DIM)
            lse_ref[:, at, :] = (m + jnp.log(l)).reshape(GROUP, tq, 1)

        first = 2 * j
        finish(jnp.maximum(first - 2, 0), 0)
        softmax(1)
        scores(first, 0)
        finish(jnp.maximum(first - 1, 0), 1)
        softmax(0)
        scores(first + 1, 1)

    tile, cur, prev, kv_spec, v_spec, sink_spec, bias_spec = _pair_specs(s, tq)
    tile32, tile16, col = pltpu.VMEM((rows, win), F32), pltpu.VMEM((rows, win), BF16), pltpu.VMEM((rows, 1), F32)
    return _hosted(
        body, rider, name="attn_b_fwd", grid=(KV_HEADS, pairs + 1),
        in_specs=[tile(HEAD_DIM, cur), kv_spec, v_spec, sink_spec, bias_spec],
        out_specs=[pl.BlockSpec((2 * tq, GROUP * HEAD_DIM), lambda g, j: (jnp.maximum(j - 1, 0), g)),
                   tile(HEAD_DIM, prev), tile(1, prev)],
        out_shape=[jax.ShapeDtypeStruct((s, BRANCH_W), BF16), jax.ShapeDtypeStruct((KV_HEADS, GROUP, s, HEAD_DIM), BF16),
                   jax.ShapeDtypeStruct((KV_HEADS, GROUP, s, 1), F32)],
        scratch_shapes=[tile32, tile32, tile16, tile16, col, col],
        args=(q.reshape(KV_HEADS, GROUP, s, HEAD_DIM), k, v1, sink_col, bias))


def _attn_b_bwd(q, k, v1, o, do, lse, sink_col, bias, rider=None, tq=TQ_B):
    s = q.shape[1]
    rows = GROUP * tq
    win = _win_keys(tq)
    pairs = s // (2 * tq)
    window, bias_index = _band(tq, s)

    def body(q_ref, k_ref, v_ref, o_ref, do_ref, lse_ref, sink_ref, bias_ref, dq_ref, dk_ref, dv_ref, dsink_ref,
             s0, s1, dp0, dp1, p0, p1, ds0, ds1, q0, q1, d0, d1, ls0, ls1, dl0, dl1):
        s_ref, dp_ref, p_ref, ds_ref = (s0, s1), (dp0, dp1), (p0, p1), (ds0, ds1)
        q_keep, do_keep, lse_keep, delta_keep = (q0, q1), (d0, d1), (ls0, ls1), (dl0, dl1)
        j = pl.program_id(1)

        @pl.when(j == 0)
        def _():
            for ref in (dk_ref, dv_ref, dsink_ref, s0, s1, dp0, dp1, p0, p1, ds0, ds1, q0, q1, d0, d1, ls0, ls1, dl0, dl1):
                ref[...] = jnp.zeros_like(ref)

        def scores(e, slot):
            at = pl.ds(slot * tq, tq)
            qq = q_ref[:, at, :].reshape(rows, HEAD_DIM)
            dd = do_ref[:, at, :].reshape(rows, HEAD_DIM)
            q_keep[slot][...] = qq
            do_keep[slot][...] = dd
            lse_keep[slot][...] = lse_ref[:, at, :].reshape(rows, 1)
            delta_keep[slot][...] = jnp.sum(dd.astype(F32) * o_ref[:, at, :].reshape(rows, HEAD_DIM).astype(F32), axis=-1,
                                            keepdims=True)
            sc = _nt(qq, k_ref[window(e), :]).reshape(GROUP, tq, win) + bias_ref[bias_index(e)][None]
            s_ref[slot][...] = sc.reshape(rows, win)
            dp_ref[slot][...] = _nt(dd, v_ref[window(e), :HEAD_DIM])

        def weights(slot):
            p = jnp.exp(s_ref[slot][...] - lse_keep[slot][...])
            p_ref[slot][...] = p.astype(BF16)
            ds_ref[slot][...] = (p * (dp_ref[slot][...] - delta_keep[slot][...])).astype(BF16)

        def grads(e, slot, live):
            at = window(e)
            ds = ds_ref[slot][...]
            dv_ref[at, :] += _tn(p_ref[slot][...], do_keep[slot][...])
            dk_ref[at, :] += _tn(ds, q_keep[slot][...])
            dq_ref[:, pl.ds(slot * tq, tq), :] = _nn(ds, k_ref[at, :]).astype(BF16).reshape(GROUP, tq, HEAD_DIM)
            dsk = jnp.exp(sink_ref[...] - lse_keep[slot][...]) * delta_keep[slot][...] * live
            for g in range(GROUP):
                dsink_ref[g:g + 1, :] -= jnp.broadcast_to(jnp.sum(dsk[tq * g:tq * (g + 1)], axis=0, keepdims=True), (1, LANES))

        first = 2 * j
        live = jnp.where(j > 0, 1.0, 0.0)
        grads(jnp.maximum(first - 2, 0), 0, live)
        weights(1)
        scores(first, 0)
        grads(jnp.maximum(first - 1, 0), 1, live)
        weights(0)
        scores(first + 1, 1)

    tile, cur, prev, kv_spec, v_spec, sink_spec, bias_spec = _pair_specs(s, tq)
    dsink_spec = pl.BlockSpec((None, ACC_ROWS, LANES), lambda g, j: (g, 0, 0))
    shape4 = (KV_HEADS, GROUP, s, HEAD_DIM)
    tile32, tile16 = pltpu.VMEM((rows, win), F32), pltpu.VMEM((rows, win), BF16)
    keep, col = pltpu.VMEM((rows, HEAD_DIM), BF16), pltpu.VMEM((rows, 1), F32)
    return _hosted(
        body, rider, name="attn_b_bwd", grid=(KV_HEADS, pairs + 1),
        in_specs=[tile(HEAD_DIM, cur), kv_spec, v_spec, tile(HEAD_DIM, cur), tile(HEAD_DIM, cur), tile(1, cur), sink_spec,
                  bias_spec],
        out_specs=[tile(HEAD_DIM, prev), kv_spec, kv_spec, dsink_spec],
        out_shape=[jax.ShapeDtypeStruct(shape4, BF16), jax.ShapeDtypeStruct((KV_HEADS, s, HEAD_DIM), F32),
                   jax.ShapeDtypeStruct((KV_HEADS, s, HEAD_DIM), F32), jax.ShapeDtypeStruct((KV_HEADS, ACC_ROWS, LANES), F32)],
        scratch_shapes=[tile32] * 4 + [tile16] * 4 + [keep] * 4 + [col] * 4,
        args=(q.reshape(shape4), k, v1, o.reshape(shape4), do.reshape(shape4), lse, sink_col, bias))


def _post_attn(ya, yb, gates, x, mod6, wbr_s, w_out, tm=512):
    s = x.shape[0]

    def body(ya_ref, yb_ref, g_ref, x_ref, mod_ref, wbr_ref, wo_ref, ua_ref, ub_ref, mg_ref, o_ref, x1_ref):
        ya_t, yb_t = ya_ref[...], yb_ref[...]
        ua = jnp.concatenate([_nn(ya_t, wbr_ref[j, 0]) for j in range(N_SHARD)], axis=1)
        ub = jnp.concatenate([_nn(yb_t, wbr_ref[j, 1]) for j in range(N_SHARD)], axis=1)
        ga, gb = g_ref[:, :D_MODEL].astype(F32), g_ref[:, D_MODEL:].astype(F32)
        merged = (_sigmoid(ga) * ua + _sigmoid(gb) * ub).astype(BF16)
        o = _nn(merged, wo_ref[...])
        ua_ref[...] = ua.astype(BF16)
        ub_ref[...] = ub.astype(BF16)
        mg_ref[...] = merged
        o_ref[...] = o.astype(BF16)
        x1_ref[...] = x_ref[...] + mod_ref[2:3, :] * o

    bf = jax.ShapeDtypeStruct((s, D_MODEL), BF16)
    return _call(
        body, name="post_attn", grid=(s // tm,),
        in_specs=[_row_spec(tm, BRANCH_W), _row_spec(tm, BRANCH_W), _row_spec(tm, 2 * D_MODEL), _row_spec(tm, D_MODEL),
                  _full_spec(mod6.shape), _full_spec(wbr_s.shape), _full_spec(w_out.shape)],
        out_specs=[_row_spec(tm, D_MODEL)] * 5,
        out_shape=[bf, bf, bf, bf, jax.ShapeDtypeStruct((s, D_MODEL), F32)],
        compiler_params=_params(("parallel",)),
    )(ya, yb, gates, x, mod6, wbr_s, w_out)


def _mlp_in(x1, mod6, g2, w_mi_s, tm=512):
    s = x1.shape[0]

    def body(x_ref, mod_ref, g_ref, w_ref, h2_ref, a_ref, hid_ref):
        xt = x_ref[...]
        r = lax.rsqrt(jnp.mean(xt * xt, axis=-1, keepdims=True) + NORM_EPS)
        h2 = ((xt * r * g_ref[...]) * (1.0 + mod_ref[4:5, :]) + mod_ref[3:4, :]).astype(BF16)
        h2_ref[...] = h2
        a = jnp.concatenate([_nn(h2, w_ref[j]) for j in range(N_SHARD)], axis=1)
        a_ref[...] = a.astype(BF16)
        hid_ref[...] = jnp.square(jnp.maximum(a, 0.0)).astype(BF16)

    return _call(
        body, name="mlp_in", grid=(s // tm,),
        in_specs=[_row_spec(tm, D_MODEL), _full_spec(mod6.shape), _full_spec(g2.shape), _full_spec(w_mi_s.shape)],
        out_specs=[_row_spec(tm, D_MODEL), _row_spec(tm, D_FF), _row_spec(tm, D_FF)],
        out_shape=[jax.ShapeDtypeStruct((s, D_MODEL), BF16), jax.ShapeDtypeStruct((s, D_FF), BF16),
                   jax.ShapeDtypeStruct((s, D_FF), BF16)],
        compiler_params=_params(("parallel",)),
    )(x1, mod6, g2, w_mi_s)


ACC_ROWS = 8


def _acc_spec():
    return pl.BlockSpec((ACC_ROWS, D_MODEL), lambda i: (0, 0))


def _acc_add(acc_ref, rows):
    @pl.when(pl.program_id(0) == 0)
    def _():
        acc_ref[...] = jnp.zeros_like(acc_ref)

    for r, val in enumerate(rows):
        acc_ref[r:r + 1, :] += jnp.sum(val, axis=0, keepdims=True)


def _mlp_out_loss(hid, x1, a, target, mod6, gf, w_mo, tm=512):
    s = x1.shape[0]

    def body(hid_ref, x_ref, a_ref, t_ref, mod_ref, gf_ref, w_ref, dx2_ref, dm_ref, da_ref, acc_ref):
        m = _nn(hid_ref[...], w_ref[...])
        gate2 = mod_ref[5:6, :]
        x2 = x_ref[...] + gate2 * m
        r = lax.rsqrt(jnp.mean(x2 * x2, axis=-1, keepdims=True) + NORM_EPS)
        xn = x2 * r
        err = xn * gf_ref[...] - t_ref[...]
        dy = err * (1.0 / D_MODEL)
        dxn = dy * gf_ref[...]
        dx2 = r * (dxn - xn * jnp.mean(dxn * xn, axis=-1, keepdims=True))
        dx2_ref[...] = dx2
        dm = (dx2 * gate2).astype(BF16)
        dm_ref[...] = dm
        da_ref[...] = (_nt(dm, w_ref[...]) * (2.0 * jnp.maximum(a_ref[...].astype(F32), 0.0))).astype(BF16)
        _acc_add(acc_ref, [err * err, dy * xn, dx2 * m])

    return _call(
        body, name="mlp_out_loss", grid=(s // tm,),
        in_specs=[_row_spec(tm, D_FF), _row_spec(tm, D_MODEL), _row_spec(tm, D_FF), _row_spec(tm, D_MODEL),
                  _full_spec(mod6.shape), _full_spec(gf.shape), _full_spec(w_mo.shape)],
        out_specs=[_row_spec(tm, D_MODEL), _row_spec(tm, D_MODEL), _row_spec(tm, D_FF), _acc_spec()],
        out_shape=[jax.ShapeDtypeStruct((s, D_MODEL), F32), jax.ShapeDtypeStruct((s, D_MODEL), BF16),
                   jax.ShapeDtypeStruct((s, D_FF), BF16), jax.ShapeDtypeStruct((ACC_ROWS, D_MODEL), F32)],
        compiler_params=_params(("arbitrary",)),
    )(hid, x1, a, target, mod6, gf, w_mo)


def _norm_bwd(dh, xt, gain, scale):
    r = lax.rsqrt(jnp.mean(xt * xt, axis=-1, keepdims=True) + NORM_EPS)
    xn = xt * r
    dxn = dh * (gain * (1.0 + scale))
    dx = r * (dxn - xn * jnp.mean(dxn * xn, axis=-1, keepdims=True))
    return dx, [dh, dh * xn * gain, dh * xn * (1.0 + scale)]


def _mlp_bwd(da, x1, dx2, o, mod6, g2, w_mi_s, rider=None, tm=512):
    s = x1.shape[0]

    def body(da_ref, x_ref, dx2_ref, o_ref, mod_ref, g_ref, w_ref, dx1_ref, do_ref, acc_ref):
        dh2 = _nt(da_ref[:, :D_MODEL], w_ref[0])
        for j in range(1, N_SHARD):
            dh2 += _nt(da_ref[:, D_MODEL * j:D_MODEL * (j + 1)], w_ref[j])
        dx, sums = _norm_bwd(dh2, x_ref[...], g_ref[...], mod_ref[4:5, :])
        dx1 = dx2_ref[...] + dx
        dx1_ref[...] = dx1
        do_ref[...] = (dx1 * mod_ref[2:3, :]).astype(BF16)
        _acc_add(acc_ref, sums + [dx1 * o_ref[...].astype(F32)])

    return _hosted(
        body, rider, name="mlp_bwd", grid=(s // tm,),
        in_specs=[_row_spec(tm, D_FF), _row_spec(tm, D_MODEL), _row_spec(tm, D_MODEL), _row_spec(tm, D_MODEL),
                  _full_spec(mod6.shape), _full_spec(g2.shape), _full_spec(w_mi_s.shape)],
        out_specs=[_row_spec(tm, D_MODEL), _row_spec(tm, D_MODEL), _acc_spec()],
        out_shape=[jax.ShapeDtypeStruct((s, D_MODEL), F32), jax.ShapeDtypeStruct((s, D_MODEL), BF16),
                   jax.ShapeDtypeStruct((ACC_ROWS, D_MODEL), F32)],
        args=(da, x1, dx2, o, mod6, g2, w_mi_s))


def _merge_bwd(do, gates, ua, ub, w_out, wbr_s, rider=None, tm=512):
    s = do.shape[0]

    def body(do_ref, g_ref, ua_ref, ub_ref, wo_ref, wbr_ref, dua_ref, dub_ref, dg_ref, doa_ref, dob_ref):
        dmerged = _nt(do_ref[...], wo_ref[...])
        for b, (u_ref, du_ref, dy_ref) in enumerate(((ua_ref, dua_ref, doa_ref), (ub_ref, dub_ref, dob_ref))):
            sg = _sigmoid(g_ref[:, D_MODEL * b:D_MODEL * (b + 1)].astype(F32))
            du = (dmerged * sg).astype(BF16)
            du_ref[...] = du
            dg_ref[:, D_MODEL * b:D_MODEL * (b + 1)] = (dmerged * u_ref[...].astype(F32) * sg * (1.0 - sg)).astype(BF16)
            w = BRANCH_W // 2
            dy = _nt(du[:, :w], wbr_ref[0, b])
            for j in range(1, N_SHARD):
                dy += _nt(du[:, w * j:w * (j + 1)], wbr_ref[j, b])
            dyb = dy.astype(BF16)
            for h in range(Q_HEADS):
                dy_ref[h] = dyb[:, HEAD_DIM * h:HEAD_DIM * (h + 1)]

    bf = jax.ShapeDtypeStruct((s, D_MODEL), BF16)
    heads = jax.ShapeDtypeStruct((Q_HEADS, s, HEAD_DIM), BF16)
    return _hosted(
        body, rider, name="merge_bwd", grid=(s // tm,),
        in_specs=[_row_spec(tm, D_MODEL), _row_spec(tm, 2 * D_MODEL), _row_spec(tm, D_MODEL), _row_spec(tm, D_MODEL),
                  _full_spec(w_out.shape), _full_spec(wbr_s.shape)],
        out_specs=[_row_spec(tm, D_MODEL), _row_spec(tm, D_MODEL), _row_spec(tm, 2 * D_MODEL),
                   _heads_spec(Q_HEADS, tm), _heads_spec(Q_HEADS, tm)],
        out_shape=[bf, bf, jax.ShapeDtypeStruct((s, 2 * D_MODEL), BF16), heads, heads],
        args=(do, gates, ua, ub, w_out, wbr_s))


def _qk_bwd(dqa, dka, dva, dqb, dkb, dvb, qkraw, dgates, gq, gk, bd, tabs, rider=None, tm=512):
    s = qkraw.shape[0]

    def body(dqa_ref, dka_ref, dva_ref, dqb_ref, dkb_ref, dvb_ref, raw_ref, dg_ref, gq_ref, gk_ref, bd_ref,
             ca, la, ha, cb, lb, hb, dp_ref, acc_ref, pair_ref):
        bdm = bd_ref[...]
        tab_a = (ca[...], la[...], ha[...])
        tab_b = (cb[...], lb[...], hb[...])

        def pair(ref, first):
            pair_ref[:, :HEAD_DIM] = ref[first].astype(F32)
            pair_ref[:, HEAD_DIM:] = ref[first + 1].astype(F32)
            return pair_ref[...]

        def norm_rope_a_bwd(dz, raw, gain):
            dzn = _rope(dz, *tab_a, 16, sign=-1.0)
            rinv = lax.rsqrt(_head_mean(raw * raw, bdm) + NORM_EPS)
            zhat = raw * rinv
            dzhat = dzn * gain
            return rinv * (dzhat - zhat * _head_mean(dzhat * zhat, bdm)), dzn * zhat

        gq_rows = jnp.zeros((tm, LANES), F32)
        for i in range(Q_HEADS // 2):
            at = slice(LANES * i, LANES * (i + 1))
            draw, gsum = norm_rope_a_bwd(pair(dqa_ref, 2 * i) * Q_SCALE, raw_ref[:, at], gq_ref[...])
            dp_ref[:, at] = draw.astype(BF16)
            gq_rows += gsum
        off = BRANCH_W
        draw, gk_rows = norm_rope_a_bwd(pair(dka_ref, 0), raw_ref[:, off:off + LANES], gk_ref[...])
        dp_ref[:, off:off + LANES] = draw.astype(BF16)
        off += KV_W
        dp_ref[:, off:off + LANES] = pair(dva_ref, 0).astype(BF16)
        off += KV_W
        for i in range(Q_HEADS // 2):
            dz = _rope(pair(dqb_ref, 2 * i) * Q_SCALE, *tab_b, 32, sign=-1.0)
            dp_ref[:, off + LANES * i:off + LANES * (i + 1)] = dz.astype(BF16)
        off += BRANCH_W
        dp_ref[:, off:off + LANES] = _rope(pair(dkb_ref, 0), *tab_b, 32, sign=-1.0).astype(BF16)
        off += KV_W
        dp_ref[:, off:off + LANES] = pair(dvb_ref, 0).astype(BF16)
        dp_ref[:, QK_W:] = dg_ref[...]

        @pl.when(pl.program_id(0) == 0)
        def _():
            acc_ref[...] = jnp.zeros_like(acc_ref)

        acc_ref[0:1, :] += jnp.sum(gq_rows, axis=0, keepdims=True)
        acc_ref[1:2, :] += jnp.sum(gk_rows, axis=0, keepdims=True)

    tab_spec = _row_spec(tm, LANES)
    return _hosted(
        body, rider, name="qk_bwd", grid=(s // tm,),
        in_specs=[_heads_spec(Q_HEADS, tm), _heads_spec(KV_HEADS, tm), _heads_spec(KV_HEADS, tm),
                  _heads_spec(Q_HEADS, tm), _heads_spec(KV_HEADS, tm), _heads_spec(KV_HEADS, tm),
                  _row_spec(tm, BRANCH_W + KV_W), _row_spec(tm, 2 * D_MODEL),
                  _full_spec(gq.shape), _full_spec(gk.shape), _full_spec(bd.shape)] + [tab_spec] * 6,
        out_specs=[_row_spec(tm, IN_W), pl.BlockSpec((ACC_ROWS, LANES), lambda i: (0, 0))],
        out_shape=[jax.ShapeDtypeStruct((s, IN_W), BF16), jax.ShapeDtypeStruct((ACC_ROWS, LANES), F32)],
        scratch_shapes=[pltpu.VMEM((tm, LANES), F32)],
        args=(dqa, dka, dva, dqb, dkb, dvb, qkraw, dgates, gq, gk, bd, *tabs))


def _in_proj_bwd(dproj, x, dx1, mod6, g1, w_in_s, tm=512):
    s = x.shape[0]
    w = IN_W // N_SHARD

    def body(dp_ref, x_ref, dx1_ref, mod_ref, g_ref, w_ref, gx_ref, acc_ref):
        dh = _nt(dp_ref[:, :w], w_ref[0])
        for j in range(1, N_SHARD):
            dh += _nt(dp_ref[:, w * j:w * (j + 1)], w_ref[j])
        dx, sums = _norm_bwd(dh, x_ref[...], g_ref[...], mod_ref[1:2, :])
        gx_ref[...] = dx1_ref[...] + dx
        _acc_add(acc_ref, sums)

    return _call(
        body, name="in_proj_bwd", grid=(s // tm,),
        in_specs=[_row_spec(tm, IN_W), _row_spec(tm, D_MODEL), _row_spec(tm, D_MODEL),
                  _full_spec(mod6.shape), _full_spec(g1.shape), _full_spec(w_in_s.shape)],
        out_specs=[_row_spec(tm, D_MODEL), _acc_spec()],
        out_shape=[jax.ShapeDtypeStruct((s, D_MODEL), F32), jax.ShapeDtypeStruct((ACC_ROWS, D_MODEL), F32)],
        compiler_params=_params(("arbitrary",)),
    )(dproj, x, dx1, mod6, g1, w_in_s)


def _wgrad(name, a, b, out_shape, out_spec, tm, tn, tk=4096):
    s, m = a.shape
    n = b.shape[1]
    tk = min(tk, s)
    nk = s // tk

    def body(a_ref, b_ref, o_ref, acc_ref):
        k = pl.program_id(2)

        @pl.when(k == 0)
        def _():
            acc_ref[...] = jnp.zeros_like(acc_ref)

        acc_ref[...] += _tn(a_ref[...], b_ref[...])

        @pl.when(k == nk - 1)
        def _():
            o_ref[...] = acc_ref[...].reshape(o_ref.shape)

    return _call(
        body, name=name, grid=(m // tm, n // tn, nk),
        in_specs=[pl.BlockSpec((tk, tm), lambda i, j, k: (k, i)), pl.BlockSpec((tk, tn), lambda i, j, k: (k, j))],
        out_specs=out_spec, out_shape=jax.ShapeDtypeStruct(out_shape, F32),
        scratch_shapes=[pltpu.VMEM((tm, tn), F32)],
        compiler_params=_params(("parallel", "parallel", "arbitrary")),
    )(a, b)


def _wgrad_branch(ya, yb, dua, dub, tk=2048):
    s = ya.shape[0]
    tk = min(tk, s)
    nk = s // tk
    w = D_MODEL // N_SHARD

    def body(ya_ref, yb_ref, dua_ref, dub_ref, o_ref, acc_ref):
        b, k = pl.program_id(0), pl.program_id(1)

        @pl.when(k == 0)
        def _():
            acc_ref[...] = jnp.zeros_like(acc_ref)

        @pl.when(b == 0)
        def _():
            acc_ref[...] += _tn(ya_ref[...], dua_ref[...])

        @pl.when(b == 1)
        def _():
            acc_ref[...] += _tn(yb_ref[...], dub_ref[...])

        @pl.when(k == nk - 1)
        def _():
            for j in range(N_SHARD):
                o_ref[j] = acc_ref[:, w * j:w * (j + 1)]

    first = lambda width: pl.BlockSpec((tk, width), lambda b, k: (k * (1 - b), 0))
    second = lambda width: pl.BlockSpec((tk, width), lambda b, k: (k * b, 0))
    return _call(
        body, name="wgrad_branch", grid=(2, nk),
        in_specs=[first(BRANCH_W), second(BRANCH_W), first(D_MODEL), second(D_MODEL)],
        out_specs=pl.BlockSpec((N_SHARD, None, BRANCH_W, w), lambda b, k: (0, b, 0, 0)),
        out_shape=jax.ShapeDtypeStruct((N_SHARD, 2, BRANCH_W, w), F32),
        scratch_shapes=[pltpu.VMEM((BRANCH_W, D_MODEL), F32)],
        compiler_params=_params(("parallel", "arbitrary")),
    )(ya, yb, dua, dub)


def _local_step(x, target, mod6, g1, g2, gf, gq2, gk2, sink, w_in_s, rest, cj=None):
    s = x.shape[0]
    dist = cj is not None
    tabs = _rope_tables(s)
    bd = _block_diag()
    sink_col = jnp.repeat(sink.reshape(KV_HEADS, GROUP, 1), TQ_B, axis=1).reshape(KV_HEADS, GROUP * TQ_B, 1)
    shard = D_MODEL // N_SHARD

    h, qkraw, qa, ka, va, qb, kb, vb, gates = _in_proj(x, mod6, g1, w_in_s, gq2, gk2, bd, tabs)
    bias = _window_bias(TQ_B)
    (yb, yb_heads, lse_b), _ = _attn_b_fwd(qb, kb, vb, sink_col, bias)
    (ya, ya_heads, lse_a), gathered = _attn_a_fwd(qa, ka, va, rider=_gather_rider(rest) if dist else None)
    wbr_s, w_out, w_mi_s, w_mo = gathered if dist else rest
    wbr_s = wbr_s.reshape(N_SHARD, 2, BRANCH_W, shard)
    w_out = w_out.reshape(D_MODEL, D_MODEL)
    w_mo = w_mo.reshape(D_FF, D_MODEL)
    ua, ub, merged, o, x1 = _post_attn(ya, yb, gates, x, mod6, wbr_s, w_out)
    h2, a, hid = _mlp_in(x1, mod6, g2, w_mi_s)
    dx2, dm, da, acc_out = _mlp_out_loss(hid, x1, a, target, mod6, gf, w_mo)

    g_w_mo = _wgrad("wgrad_mlp_out", hid, dm, (D_FF, D_MODEL), pl.BlockSpec((D_MODEL, D_MODEL), lambda i, j, k: (i, 0)),
                    D_MODEL, D_MODEL).reshape(N_SHARD, D_MODEL, D_MODEL)
    g_w_mi = _wgrad("wgrad_mlp_in", h2, da, (N_SHARD, D_MODEL, D_MODEL),
                    pl.BlockSpec((None, D_MODEL, D_MODEL), lambda i, j, k: (j, i, 0)), D_MODEL, D_MODEL)
    mlp = _Reduction(("mlp_out", "mlp_in"), (g_w_mo, g_w_mi), cj)
    (dx1, do, acc_mlp), got = _mlp_bwd(da, x1, dx2, o, mod6, g2, w_mi_s, rider=mlp.swap() if dist else None)
    (dua, dub, dgates, doa, dob), _ = _merge_bwd(do, gates, ua, ub, w_out, wbr_s)
    g_w_out = _wgrad("wgrad_out", merged, do, (D_MODEL, D_MODEL), pl.BlockSpec((D_MODEL, D_MODEL), lambda i, j, k: (i, 0)),
                     D_MODEL, D_MODEL).reshape(N_SHARD, shard, D_MODEL)
    g_wbr = _wgrad_branch(ya, yb, dua, dub)
    out = _Reduction(("out", "branch"), (g_w_out, g_wbr.reshape(N_SHARD, 2 * BRANCH_W, shard)), cj)
    (dqa, dka, dva), landings = _attn_a_bwd(qa, ka, va, ya_heads, doa, lse_a,
                                            rider=_riders(mlp.add(got), out.swap()) if dist else None)
    (dqb, dkb, dvb, dsink), joined = _attn_b_bwd(qb, kb, vb, yb_heads, dob, lse_b, sink_col, bias,
                                                 rider=mlp.total(landings[:2]) if dist else None)
    heads = (Q_HEADS, s, HEAD_DIM)
    (dproj, acc_qk), landed = _qk_bwd(dqa.reshape(heads), dka, dva, dqb.reshape(heads), dkb, dvb, qkraw, dgates, gq2, gk2, bd,
                                      tabs, rider=out.add(landings[2:]) if dist else None)
    w = IN_W // N_SHARD
    g_w_in = _wgrad("wgrad_in", h, dproj, (N_SHARD, D_MODEL, w), pl.BlockSpec((None, D_MODEL, w), lambda i, j, k: (j, i, 0)),
                    D_MODEL, w)
    grad_x, acc_in = _in_proj_bwd(dproj, x, dx1, mod6, g1, w_in_s)
    accs = (acc_out, acc_mlp, acc_in, acc_qk, dsink)
    if not dist:
        return grad_x, (g_w_in, g_wbr, g_w_out, g_w_mi, g_w_mo), accs
    r_mo, r_mi = joined
    return grad_x, (_Reduction(("in",), (g_w_in,), cj), out.total(landed), r_mi, r_mo), accs


def _me():
    return lax.axis_index("x"), lax.axis_index("y"), lax.axis_index("c")


def _peer(d):
    x, y, c = _me()
    return (1 - x if d & 4 else x, 1 - y if d & 2 else y, 1 - c if d & 1 else c)


def _dev_index(p):
    return 4 * p[0] + 2 * p[1] + p[2]


def _chip_index(p):
    return 2 * p[0] + p[1]


def _remote(src, dst, send_sem, recv_sem, to):
    return pltpu.make_async_remote_copy(src_ref=src, dst_ref=dst, send_sem=send_sem, recv_sem=recv_sem,
                                        device_id=to, device_id_type=MESH)


SLOT_ROWS = 8


def _ada_fwd(c, w_ada, b4, rider=None):
    cols = w_ada.shape[1]

    def body(c_ref, w_ref, b_ref, mod_ref, sc_ref, cbuf, pbuf, mbuf, send1, recv1, send2, recv2, launch=None):
        me = _me()
        mine, chip = _dev_index(me), _chip_index(me)
        cbuf[mine] = jnp.broadcast_to(c_ref[...], (SLOT_ROWS, D_MODEL))
        gather = [_remote(cbuf.at[mine], cbuf.at[mine], send1.at[d - 1], recv1.at[d - 1], _peer(d)) for d in range(1, N_DEV)]
        for cp in gather:
            cp.start()
        if launch is not None:
            launch()
        for d in range(1, N_DEV):
            _remote(cbuf.at[mine], cbuf.at[_dev_index(_peer(d))], send1.at[d - 1], recv1.at[d - 1], _peer(d)).wait_recv()
        call = cbuf[...].reshape(N_DEV * SLOT_ROWS, D_MODEL)
        sc = call * _sigmoid(call)
        for s in range(N_DEV):
            sc_ref[s:s + 1, :] = sc[SLOT_ROWS * s:SLOT_ROWS * s + 1]
        part = _nn(sc.astype(BF16), w_ref[...].astype(BF16)) + b_ref[pl.ds(chip, 1), :]
        pbuf[...] = part.reshape(N_DEV, SLOT_ROWS, cols)
        mbuf[chip] = pbuf[mine]
        spread = [_remote(pbuf.at[_dev_index(_peer(d))], mbuf.at[chip], send2.at[d // 2 - 1], recv2.at[d // 2 - 1], _peer(d))
                  for d in (2, 4, 6)]
        for cp in spread:
            cp.start()
        for d in (2, 4, 6):
            _remote(pbuf.at[mine], mbuf.at[_chip_index(_peer(d))], send2.at[d // 2 - 1], recv2.at[d // 2 - 1],
                    _peer(d)).wait_recv()
        half = D_MODEL // 2
        for p in range(2 * 6):
            col = half * p
            mod_ref[p // 2:p // 2 + 1, half * (p % 2):half * (p % 2 + 1)] = mbuf[col // cols, 0:1, col % cols:col % cols + half]
        for cp in gather + spread:
            cp.wait_send()

    vm = pl.BlockSpec(memory_space=pltpu.VMEM)
    return _hosted(
        body, rider, name="ada_fwd", grid=(), in_specs=[vm, vm, vm], out_specs=[vm, vm],
        out_shape=[jax.ShapeDtypeStruct((6, D_MODEL), F32), jax.ShapeDtypeStruct((N_DEV, D_MODEL), F32)],
        scratch_shapes=[pltpu.VMEM((N_DEV, SLOT_ROWS, D_MODEL), F32), pltpu.VMEM((N_DEV, SLOT_ROWS, cols), F32),
                        pltpu.VMEM((N_SHARD, SLOT_ROWS, cols), F32),
                        pltpu.SemaphoreType.DMA((N_DEV - 1,)), pltpu.SemaphoreType.DMA((N_DEV - 1,)),
                        pltpu.SemaphoreType.DMA((N_SHARD - 1,)), pltpu.SemaphoreType.DMA((N_SHARD - 1,))],
        args=(c, w_ada, b4))


PACK_ROWS = 16
PACK_W = 3 * D_MODEL


def _ada_bwd(acc_out, acc_mlp, acc_in, acc_qk, dsink, sc_all, rider=None):
    cols = 6 * D_MODEL // N_SHARD

    def body(out_ref, mlp_ref, in_ref, qk_ref, dsink_ref, sc_ref,
             gwa_ref, gba_ref, gn1_ref, gn2_ref, gf_ref, gq_ref, gk_ref, gs_ref, loss_ref, blk, send, recv, launch=None):
        me = _me()
        mine, chip = _dev_index(me), _chip_index(me)
        blk[mine] = jnp.zeros((PACK_ROWS, PACK_W), F32)
        dmod = (in_ref, 0), (in_ref, 1), (mlp_ref, 3), (mlp_ref, 0), (mlp_ref, 1), (out_ref, 2)
        half = D_MODEL // 2
        for p in range(2 * 6):
            ref, row = dmod[p // 2]
            col = half * p
            blk[mine, col // cols:col // cols + 1, col % cols:col % cols + half] = ref[row:row + 1, half * (p % 2):half * (p % 2 + 1)]
        blk[mine, 4:5, 0:D_MODEL] = in_ref[2:3, :]
        blk[mine, 4:5, D_MODEL:2 * D_MODEL] = mlp_ref[2:3, :]
        blk[mine, 4:5, 2 * D_MODEL:] = out_ref[1:2, :]
        blk[mine, 5:6, 0:LANES] = qk_ref[0:1, :]
        blk[mine, 5:6, LANES:2 * LANES] = qk_ref[1:2, :]
        blk[mine, 6:7, 0:D_MODEL] = out_ref[0:1, :]
        for g in range(KV_HEADS):
            blk[mine, 8 + GROUP * g:8 + GROUP * (g + 1), 0:LANES] = dsink_ref[g, 0:GROUP, :]
        copies = [_remote(blk.at[mine], blk.at[mine], send.at[d - 1], recv.at[d - 1], _peer(d)) for d in range(1, N_DEV)]
        for cp in copies:
            cp.start()
        if launch is not None:
            launch()
        for d in range(1, N_DEV):
            _remote(blk.at[mine], blk.at[_dev_index(_peer(d))], send.at[d - 1], recv.at[d - 1], _peer(d)).wait_recv()
        tot = blk[0]
        for s in range(1, N_DEV):
            tot = tot + blk[s]
        for j in range(N_SHARD):
            gba_ref[:, cols * j:cols * (j + 1)] = tot[j:j + 1, :cols]
        gn1_ref[...] = tot[4:5, 0:D_MODEL]
        gn2_ref[...] = tot[4:5, D_MODEL:2 * D_MODEL]
        gf_ref[...] = tot[4:5, 2 * D_MODEL:]
        gq_ref[...] = tot[5:6, 0:HEAD_DIM] + tot[5:6, HEAD_DIM:2 * HEAD_DIM]
        gk_ref[...] = tot[5:6, LANES:LANES + HEAD_DIM] + tot[5:6, LANES + HEAD_DIM:2 * LANES]
        sq = tot[8:16, 0:Q_HEADS]
        diag = lax.broadcasted_iota(jnp.int32, sq.shape, 0) == lax.broadcasted_iota(jnp.int32, sq.shape, 1)
        gs_ref[...] = jnp.sum(jnp.where(diag, sq, 0.0), axis=0, keepdims=True)
        half_mse = (0.5 / D_MODEL) * jnp.sum(tot[6:7, 0:D_MODEL], axis=-1, keepdims=True)
        loss_ref[...] = jnp.broadcast_to(half_mse, (1, LANES))
        dm = jnp.concatenate([blk[s, pl.ds(chip, 1), pl.ds(0, cols)] for s in range(N_DEV)], axis=0)
        gwa_ref[...] = _tn(sc_ref[...], dm)
        for cp in copies:
            cp.wait_send()

    vm = pl.BlockSpec(memory_space=pltpu.VMEM)
    row = lambda n: jax.ShapeDtypeStruct((1, n), F32)
    return _hosted(
        body, rider, name="ada_bwd", grid=(), in_specs=[vm] * 6, out_specs=[vm] * 9,
        out_shape=[jax.ShapeDtypeStruct((D_MODEL, cols), F32), row(6 * D_MODEL), row(D_MODEL), row(D_MODEL), row(D_MODEL),
                   row(HEAD_DIM), row(HEAD_DIM), row(Q_HEADS), row(LANES)],
        scratch_shapes=[pltpu.VMEM((N_DEV, PACK_ROWS, PACK_W), F32),
                        pltpu.SemaphoreType.DMA((N_DEV - 1,)), pltpu.SemaphoreType.DMA((N_DEV - 1,))],
        args=(acc_out, acc_mlp, acc_in, acc_qk, dsink, sc_all))


def _cast_weights(ws):
    n = len(ws)

    def body(*refs):
        src, out, tmp, sems = refs[:n], refs[n:2 * n], refs[2 * n:3 * n], refs[3 * n]
        chip = _chip_index(_me())
        copies = []
        for a in range(n):
            tmp[a][...] = src[a][...].astype(BF16)
            cp = pltpu.make_async_copy(tmp[a], out[a].at[chip], sems.at[a])
            cp.start()
            copies.append(cp)
        for cp in copies:
            cp.wait()

    vm = pl.BlockSpec(memory_space=pltpu.VMEM)
    return _call(
        body, name="cast_weights", in_specs=[vm] * n, out_specs=[ANY] * n,
        out_shape=[jax.ShapeDtypeStruct((N_SHARD,) + w.shape, BF16) for w in ws],
        scratch_shapes=[pltpu.VMEM(w.shape, BF16) for w in ws] + [pltpu.SemaphoreType.DMA((n,))],
        compiler_params=_params(),
    )(*ws)


def _half_rows(ref_rows, c):
    half = ref_rows // 2
    return pl.ds(pl.multiple_of(c * half, 8), half)


class _Rider:
    def __init__(self, inputs, out_shape, aliases, n_sems, start, finish, middle=None):
        self.inputs, self.out_shape, self.aliases, self.n_sems = list(inputs), list(out_shape), dict(aliases), n_sems
        self.start, self.finish, self.middle = start, finish, middle


def _riders(*rs):
    ins = [0]
    outs = [0]
    sems = [0]
    for r in rs:
        ins.append(ins[-1] + len(r.inputs))
        outs.append(outs[-1] + len(r.out_shape))
        sems.append(sems[-1] + r.n_sems)

    def phase(which):
        def run(in_refs, out_refs, sem):
            for k, r in enumerate(rs):
                fn = getattr(r, which)
                if fn is not None:
                    fn(in_refs[ins[k]:ins[k + 1]], out_refs[outs[k]:outs[k + 1]], lambda j, base=sems[k]: sem(base + j))
        return run

    aliases = {ins[k] + i: outs[k] + o for k, r in enumerate(rs) for i, o in r.aliases.items()}
    return _Rider([a for r in rs for a in r.inputs], [o for r in rs for o in r.out_shape], aliases, sems[-1],
                  phase("start"), phase("finish"), phase("middle") if any(r.middle for r in rs) else None)


def _hosted(body, rider, *, name, grid, in_specs, out_specs, out_shape, args, scratch_shapes=(), middle_at=None):
    where = dict(grid=grid, compiler_params=_params(("arbitrary",) * len(grid))) if grid else dict(compiler_params=_params())
    if rider is None:
        res = _call(body, name=name, in_specs=in_specs, out_specs=out_specs, out_shape=out_shape,
                    scratch_shapes=list(scratch_shapes), **where)(*args)
        return res, ()
    n_in, n_out, n_scr = len(in_specs), len(out_specs), len(scratch_shapes)
    r_in, r_out = len(rider.inputs), len(rider.out_shape)

    def riding(*refs):
        at = 0
        parts = []
        for size in (n_in, r_in, n_out, r_out, n_scr):
            parts.append(refs[at:at + size])
            at += size
        ins, rider_ins, outs, rider_outs, scratch = parts
        sems = refs[at]

        def sem_at(k):
            return sems.at[k]

        if not grid:
            body(*ins, *outs, *scratch, launch=lambda: rider.start(rider_ins, rider_outs, sem_at))
            if rider.middle is not None:
                rider.middle(rider_ins, rider_outs, sem_at)
            rider.finish(rider_ins, rider_outs, sem_at)
            return
        step = pl.program_id(0)
        for axis in range(1, len(grid)):
            step = step * grid[axis] + pl.program_id(axis)
        steps = 1
        for size in grid:
            steps *= size

        @pl.when(step == 0)
        def _():
            rider.start(rider_ins, rider_outs, sem_at)

        body(*ins, *outs, *scratch)
        if rider.middle is not None:
            @pl.when(step == middle_at)
            def _():
                rider.middle(rider_ins, rider_outs, sem_at)

        @pl.when(step == steps - 1)
        def _():
            rider.finish(rider_ins, rider_outs, sem_at)

    res = _call(
        riding, name=name, in_specs=list(in_specs) + [ANY] * r_in, out_specs=list(out_specs) + [ANY] * r_out,
        out_shape=list(out_shape) + rider.out_shape,
        input_output_aliases={n_in + i: n_out + o for i, o in rider.aliases.items()},
        scratch_shapes=list(scratch_shapes) + [pltpu.SemaphoreType.DMA((rider.n_sems,))], **where,
    )(*args, *rider.inputs)
    return res[:n_out], res[n_out:]


def _alone(name, rider):
    n_in, n_out = len(rider.inputs), len(rider.out_shape)

    def body(*refs):
        ins, outs, sems = refs[:n_in], refs[n_in:n_in + n_out], refs[n_in + n_out]

        def sem_at(k):
            return sems.at[k]

        rider.start(ins, outs, sem_at)
        if rider.middle is not None:
            rider.middle(ins, outs, sem_at)
        rider.finish(ins, outs, sem_at)

    return _call(
        body, name=name, in_specs=[ANY] * n_in, out_specs=[ANY] * n_out, out_shape=rider.out_shape,
        input_output_aliases=rider.aliases, scratch_shapes=[pltpu.SemaphoreType.DMA((rider.n_sems,))],
    )(*rider.inputs)


OTHER_CHIPS = (2, 4, 6)


def _gather_rider(stacked):
    n = len(stacked)

    def flights(bufs, sem):
        me = _me()
        chip, sib = _chip_index(me), _peer(1)
        out = []
        for a in range(n):
            mine, theirs = (_half_rows(bufs[a].shape[1], c) for c in (me[2], 1 - me[2]))
            for j, d in enumerate(OTHER_CHIPS):
                k = 3 * a + j
                from_chip = _chip_index(_peer(d))
                own, landed, passed = bufs[a].at[chip, mine], bufs[a].at[from_chip, mine], bufs[a].at[from_chip, theirs]
                out.append((_remote(own, own, sem(k), sem(3 * n + k), _peer(d)),
                            _remote(own, landed, sem(k), sem(3 * n + k), _peer(d)),
                            _remote(landed, landed, sem(6 * n + k), sem(9 * n + k), sib),
                            _remote(passed, passed, sem(6 * n + k), sem(9 * n + k), sib)))
        return out

    def start(ins, outs, sem):
        for send, _, _, _ in flights(outs, sem):
            send.start()

    def middle(ins, outs, sem):
        for _, arrival, pass_on, _ in flights(outs, sem):
            arrival.wait_recv()
            pass_on.start()

    def finish(ins, outs, sem):
        every = flights(outs, sem)
        for _, _, _, passed_to_me in every:
            passed_to_me.wait_recv()
        for send, _, pass_on, _ in every:
            send.wait_send()
            pass_on.wait_send()

    return _Rider(stacked, [jax.ShapeDtypeStruct(w.shape, w.dtype) for w in stacked], {a: a for a in range(n)}, 12 * n,
                  start, finish, middle)


def _swap_rider(grads):
    n = len(grads)

    def copies(ins, outs, sem):
        c = _me()[2]
        return [_remote(ins[a].at[pl.ds(0, N_SHARD), _half_rows(ins[a].shape[1], 1 - c)], outs[a], sem(a), sem(n + a), _peer(1))
                for a in range(n)]

    def start(ins, outs, sem):
        for cp in copies(ins, outs, sem):
            cp.start()

    def finish(ins, outs, sem):
        for cp in copies(ins, outs, sem):
            cp.wait()

    return _Rider(grads, [jax.ShapeDtypeStruct((N_SHARD, g.shape[1] // 2, g.shape[2]), F32) for g in grads], {}, 2 * n,
                  start, finish)


def _row_tile(rows):
    return min(rows, 256)


def _add_halves(name, g, got, cj):
    _, half, cols = got.shape
    tr = _row_tile(half)
    nt = half // tr

    def body(cj_ref, g_ref, got_ref, o_ref):
        o_ref[...] = (g_ref[...] + got_ref[...]).astype(BF16)

    spec = pl.BlockSpec((None, tr, cols), lambda i, s, cj: (s, i, 0))
    return _call(
        body, name=name,
        grid_spec=pltpu.PrefetchScalarGridSpec(
            num_scalar_prefetch=1, grid=(nt, N_SHARD),
            in_specs=[pl.BlockSpec((None, tr, cols), lambda i, s, cj: (s, cj[0] * nt + i, 0)), spec], out_specs=spec),
        out_shape=jax.ShapeDtypeStruct(got.shape, BF16), compiler_params=_params(("parallel", "parallel")),
    )(cj, g, got)


def _scatter_rider(sums):
    n = len(sums)

    def flights(ins, outs, sem):
        chip = _chip_index(_me())
        out = []
        for a in range(n):
            for j, d in enumerate(OTHER_CHIPS):
                k = 3 * a + j
                other = _chip_index(_peer(d))
                out.append((_remote(ins[a].at[other], outs[a].at[chip], sem(k), sem(3 * n + k), _peer(d)),
                            _remote(ins[a].at[chip], outs[a].at[other], sem(k), sem(3 * n + k), _peer(d))))
        return out

    def start(ins, outs, sem):
        for send, _ in flights(ins, outs, sem):
            send.start()

    def finish(ins, outs, sem):
        every = flights(ins, outs, sem)
        for _, arrival in every:
            arrival.wait_recv()
        for send, _ in every:
            send.wait_send()

    return _Rider(sums, [jax.ShapeDtypeStruct(v.shape, v.dtype) for v in sums], {}, 6 * n, start, finish)


def _sum_chips(name, g, got, landed, cj):
    _, half, cols = got.shape
    tr = _row_tile(half)
    nt = half // tr

    def body(cj_ref, g_ref, got_ref, landed_ref, o_ref):
        own = g_ref[...] + got_ref[...]
        total = None
        for s in range(N_SHARD):
            term = jnp.where(cj_ref[1] == s, own, landed_ref[s].astype(F32))
            total = term if total is None else total + term
        o_ref[...] = total

    return _call(
        body, name=name,
        grid_spec=pltpu.PrefetchScalarGridSpec(
            num_scalar_prefetch=1, grid=(nt,),
            in_specs=[pl.BlockSpec((None, tr, cols), lambda i, cj: (cj[1], cj[0] * nt + i, 0)),
                      pl.BlockSpec((None, tr, cols), lambda i, cj: (cj[1], i, 0)),
                      pl.BlockSpec((N_SHARD, tr, cols), lambda i, cj: (0, i, 0))],
            out_specs=pl.BlockSpec((tr, cols), lambda i, cj: (cj[0] * nt + i, 0))),
        out_shape=jax.ShapeDtypeStruct((2 * half, cols), F32), compiler_params=_params(("parallel",)),
    )(cj, g, got, landed)


def _join_rider(shards):
    n = len(shards)

    def flights(bufs, sem):
        c = _me()[2]
        out = []
        for a in range(n):
            mine, theirs = (bufs[a].at[_half_rows(bufs[a].shape[0], cc)] for cc in (c, 1 - c))
            out.append((_remote(mine, mine, sem(a), sem(n + a), _peer(1)), _remote(theirs, theirs, sem(a), sem(n + a), _peer(1))))
        return out

    def start(ins, outs, sem):
        for send, _ in flights(outs, sem):
            send.start()

    def finish(ins, outs, sem):
        for send, arrival in flights(outs, sem):
            arrival.wait_recv()
            send.wait_send()

    return _Rider(shards, [jax.ShapeDtypeStruct(h.shape, F32) for h in shards], {a: a for a in range(n)}, 2 * n, start, finish)


class _Reduction:
    def __init__(self, names, grads, cj):
        self.names, self.grads, self.cj = names, list(grads), cj

    def swap(self):
        return _swap_rider(self.grads)

    def add(self, got):
        self.got = list(got)
        self.sums = [_add_halves("add_halves_" + nm, g, h, self.cj) for nm, g, h in zip(self.names, self.grads, self.got)]
        return _scatter_rider(self.sums)

    def total(self, landed):
        halves = [_sum_chips("sum_chips_" + nm, g, h, l, self.cj)
                  for nm, g, h, l in zip(self.names, self.grads, self.got, landed)]
        return _join_rider(halves)


def _adamw_math(w, g, m, v):
    m = ADAM_B1 * m + (1.0 - ADAM_B1) * g
    v = ADAM_B2 * v + (1.0 - ADAM_B2) * jnp.square(g)
    m_hat = m / (1.0 - ADAM_B1 ** ADAM_STEP)
    v_hat = v / (1.0 - ADAM_B2 ** ADAM_STEP)
    return -ADAM_LR * (m_hat / (jnp.sqrt(v_hat) + ADAM_EPS) + ADAM_WD * w), m, v


def _adamw(name, ws, gs, ms, vs, rider=None):
    n = len(ws)
    rows = ws[0].shape[0]
    tr = _row_tile(rows)

    def body(*refs):
        ins, outs = refs[:4 * n], refs[4 * n:]
        for a in range(n):
            w, g, m, v = (ins[k * n + a][...] for k in range(4))
            outs[a][...], outs[n + a][...], outs[2 * n + a][...] = _adamw_math(w, g, m, v)

    specs = [pl.BlockSpec((tr, w.shape[1]), lambda i: (i, 0)) for w in ws]
    res, riding = _hosted(
        body, rider, name=name, grid=(rows // tr,), in_specs=specs * 4, out_specs=specs * 3,
        out_shape=[jax.ShapeDtypeStruct(w.shape, F32) for w in ws] * 3, args=(*ws, *gs, *ms, *vs))
    return (res[:n], res[n:2 * n], res[2 * n:]), riding


def _adamw_small(ws, gs, ms, vs):
    n = len(ws)

    def body(*refs):
        ins, outs = refs[:4 * n], refs[4 * n:]
        for a in range(n):
            w, g, m, v = (ins[k * n + a][...] for k in range(4))
            outs[a][...], outs[n + a][...], outs[2 * n + a][...] = _adamw_math(w, g, m, v)

    vm = pl.BlockSpec(memory_space=pltpu.VMEM)
    res = _call(
        body, name="adamw_small", in_specs=[vm] * (4 * n), out_specs=[vm] * (3 * n),
        out_shape=[jax.ShapeDtypeStruct(w.shape, F32) for w in ws] * 3, compiler_params=_params(),
    )(*ws, *gs, *ms, *vs)
    return res[:n], res[n:2 * n], res[2 * n:]


def kernel(x, c, w_ada, b_ada, norm1_g, w_in, q_norm_a, k_norm_a, sink_b, w_branch, w_out, norm2_g, w_mlp_in, w_mlp_out, final_g, loss_target, m_w_ada, m_b_ada, m_norm1_g, m_w_in, m_q_norm_a, m_k_norm_a, m_sink_b, m_w_branch, m_w_out, m_norm2_g, m_w_mlp_in, m_w_mlp_out, m_final_g, v_w_ada, v_b_ada, v_norm1_g, v_w_in, v_q_norm_a, v_k_norm_a, v_sink_b, v_w_branch, v_w_out, v_norm2_g, v_w_mlp_in, v_w_mlp_out, v_final_g):
    xi, yi, ci = _me()
    cj = jnp.stack([ci, 2 * xi + yi]).astype(jnp.int32)
    n_cols = 6 * D_MODEL // N_SHARD

    def rows2d(a):
        return a.reshape(-1, a.shape[-1])

    big = (w_in, w_branch, w_out, w_mlp_in, w_mlp_out)
    stacked = _cast_weights([rows2d(w) for w in big])
    (mod6, sc_all), (w_in_s,) = _ada_fwd(c, w_ada[0], b_ada.reshape(N_SHARD, n_cols), rider=_gather_rider(stacked[:1]))
    rest = stacked[1:]

    gq2 = jnp.tile(q_norm_a, (1, 2))
    gk2 = jnp.tile(k_norm_a, (1, 2))
    grad_x, (w_in_red, join_out, g_mi, g_mo), accs = _local_step(
        x[0], loss_target[0], mod6, norm1_g, norm2_g, final_g.reshape(1, D_MODEL), gq2, gk2, sink_b[0], w_in_s, rest, cj)

    (g_w_ada, g_b_ada, g_n1, g_n2, g_f, g_q, g_k, g_s, loss_row), got_in = _ada_bwd(*accs, sc_all, rider=w_in_red.swap())
    loss = loss_row[0, 0]
    moments = dict(w_ada=(m_w_ada, v_w_ada), w_in=(m_w_in, v_w_in), w_branch=(m_w_branch, v_w_branch), w_out=(m_w_out, v_w_out),
                   w_mlp_in=(m_w_mlp_in, v_w_mlp_in), w_mlp_out=(m_w_mlp_out, v_w_mlp_out))
    weights = dict(w_ada=w_ada, w_in=w_in, w_branch=w_branch, w_out=w_out, w_mlp_in=w_mlp_in, w_mlp_out=w_mlp_out)

    def adamw(call, names, grads, rider=None):
        (d, m, v), riding = _adamw(call, [rows2d(weights[nm]) for nm in names], grads,
                                   [rows2d(moments[nm][0]) for nm in names], [rows2d(moments[nm][1]) for nm in names], rider)
        return {nm: (grads[k], d[k], m[k], v[k]) for k, nm in enumerate(names)}, riding

    big_res, landed_in = adamw("adamw_ada_mlp", ("w_ada", "w_mlp_in", "w_mlp_out"), [g_w_ada, g_mi, g_mo], w_in_red.add(got_in))
    g_in, g_out, g_br = _alone("join_in_out_branch", _riders(w_in_red.total(landed_in), join_out))
    big_res.update(adamw("adamw_in_branch", ("w_in", "w_branch"), [g_in, g_br])[0])

    small = ("b_ada", "norm1_g", "q_norm_a", "k_norm_a", "sink_b", "norm2_g", "final_g", "w_out")
    row = lambda a: a.reshape(1, -1)
    small_w = [row(a) for a in (b_ada, norm1_g, q_norm_a, k_norm_a, sink_b, norm2_g, final_g)] + [w_out[0]]
    small_g = [g_b_ada, g_n1, g_q, g_k, g_s, g_n2, g_f, g_out]
    small_m = [row(a) for a in (m_b_ada, m_norm1_g, m_q_norm_a, m_k_norm_a, m_sink_b, m_norm2_g, m_final_g)] + [m_w_out[0]]
    small_v = [row(a) for a in (v_b_ada, v_norm1_g, v_q_norm_a, v_k_norm_a, v_sink_b, v_norm2_g, v_final_g)] + [v_w_out[0]]
    s_d, s_m, s_v = _adamw_small(small_w, small_g, small_m, small_v)

    order = ("w_ada", "b_ada", "norm1_g", "w_in", "q_norm_a", "k_norm_a", "sink_b", "w_branch", "w_out", "norm2_g",
             "w_mlp_in", "w_mlp_out", "final_g")
    like = dict(w_ada=w_ada, b_ada=b_ada, norm1_g=norm1_g, w_in=w_in, q_norm_a=q_norm_a, k_norm_a=k_norm_a, sink_b=sink_b,
                w_branch=w_branch, w_out=w_out, norm2_g=norm2_g, w_mlp_in=w_mlp_in, w_mlp_out=w_mlp_out, final_g=final_g)
    grad, delta, new_m, new_v = {}, {}, {}, {}
    for nm, res in big_res.items():
        grad[nm], delta[nm], new_m[nm], new_v[nm] = res
    for k, nm in enumerate(small):
        grad[nm], delta[nm], new_m[nm], new_v[nm] = small_g[k], s_d[k], s_m[k], s_v[k]
    outs = [loss, grad_x[None]]
    for group in (grad, delta, new_m, new_v):
        outs += [group[nm].reshape(like[nm].shape) for nm in order]
    return tuple(outs)
```

```python
import jax
import jax.numpy as jnp
from jax import lax
from jax.experimental import pallas as pl
from jax.experimental.pallas import tpu as pltpu

F32 = jnp.float32
BF16 = jnp.bfloat16
MESH = pl.DeviceIdType.MESH
ANY = pl.BlockSpec(memory_space=pl.ANY)

D_MODEL = 1024
HEAD_DIM = 64
Q_HEADS = 8
KV_HEADS = 2
GROUP = Q_HEADS // KV_HEADS
BRANCH_W = Q_HEADS * HEAD_DIM
KV_W = KV_HEADS * HEAD_DIM
IN_W = 2 * (BRANCH_W + 2 * KV_W) + 2 * D_MODEL
QK_W = 2 * (BRANCH_W + 2 * KV_W)
D_FF = 4 * D_MODEL
GRID_W = 64
WINDOW = 128
ROPE_THETA = 10000.0
NORM_EPS = 1e-6
NEG_INF = -1e30
Q_SCALE = HEAD_DIM ** -0.5
N_SHARD = 4
N_DEV = 8
LANES = 128
VMEM_LIMIT = 56 * 1024 * 1024

ADAM_LR = 0.001
ADAM_B1 = 0.9
ADAM_B2 = 0.999
ADAM_EPS = 1e-08
ADAM_WD = 0.01
ADAM_STEP = 10

_call = pl.pallas_call


def _params(sem=None, vmem=VMEM_LIMIT):
    return pltpu.CompilerParams(dimension_semantics=sem, vmem_limit_bytes=vmem)


def _nt(a, b):
    return lax.dot_general(a, b, (((1,), (1,)), ((), ())), preferred_element_type=F32)


def _tn(a, b):
    return lax.dot_general(a, b, (((0,), (0,)), ((), ())), preferred_element_type=F32)


def _nn(a, b):
    return jnp.dot(a, b, preferred_element_type=F32)


def _sigmoid(z):
    return 0.5 * jnp.tanh(0.5 * z) + 0.5


def _rope_tables(s):
    t = jnp.arange(s, dtype=jnp.int32)
    lane = jnp.arange(LANES, dtype=jnp.int32)

    def cos_sin(pos, dim):
        inv = ROPE_THETA ** (-jnp.arange(0, dim, 2, dtype=F32) / dim)
        ang = pos.astype(F32)[:, None] * inv[None, :]
        return jnp.cos(ang), jnp.sin(ang)

    cr, sr = cos_sin(t // GRID_W, HEAD_DIM // 2)
    cc, sc = cos_sin(t % GRID_W, HEAD_DIM // 2)
    cos_a = jnp.tile(jnp.concatenate([cr, cr, cc, cc], axis=1), (1, 2))
    sin_a = jnp.tile(jnp.concatenate([sr, sr, sc, sc], axis=1), (1, 2))
    first_a = (lane % 32) < 16
    c1, s1 = cos_sin(t, HEAD_DIM)
    cos_b = jnp.tile(jnp.concatenate([c1, c1], axis=1), (1, 2))
    sin_b = jnp.tile(jnp.concatenate([s1, s1], axis=1), (1, 2))
    first_b = (lane % 64) < 32
    tabs_a = (cos_a, jnp.where(first_a, -sin_a, 0.0), jnp.where(first_a, 0.0, sin_a))
    tabs_b = (cos_b, jnp.where(first_b, -sin_b, 0.0), jnp.where(first_b, 0.0, sin_b))
    return tabs_a + tabs_b


def _rope(z, cos, s_lo, s_hi, half, sign=1.0):
    up = pltpu.roll(z, LANES - half, 1)
    dn = pltpu.roll(z, half, 1)
    return z * cos + sign * (up * s_lo + dn * s_hi)


def _head_mean(z2, bd):
    hi = z2.astype(BF16)
    lo = (z2 - hi.astype(F32)).astype(BF16)
    return _nn(hi, bd) + _nn(lo, bd)


def _block_diag():
    lane = jnp.arange(LANES)
    return jnp.where((lane[:, None] // HEAD_DIM) == (lane[None, :] // HEAD_DIM), 1.0 / HEAD_DIM, 0.0).astype(BF16)


def _row_spec(tm, width):
    return pl.BlockSpec((tm, width), lambda i: (i, 0))


def _heads_spec(heads, tm):
    return pl.BlockSpec((heads, tm, HEAD_DIM), lambda i: (0, i, 0))


def _full_spec(shape):
    nd = len(shape)
    return pl.BlockSpec(shape, lambda i: (0,) * nd)


def _in_proj(x, mod6, g1, w_in_s, gq, gk, bd, tabs, tm=512):
    s = x.shape[0]

    def body(x_ref, mod_ref, g1_ref, w_ref, gq_ref, gk_ref, bd_ref, ca, la, ha, cb, lb, hb,
             h_ref, qkraw_ref, qa_ref, ka_ref, va_ref, qb_ref, kb_ref, vb_ref, gate_ref):
        xt = x_ref[...]
        r = lax.rsqrt(jnp.mean(xt * xt, axis=-1, keepdims=True) + NORM_EPS)
        h = (xt * r * g1_ref[...]) * (1.0 + mod_ref[1:2, :]) + mod_ref[0:1, :]
        hb16 = h.astype(BF16)
        h_ref[...] = hb16
        proj = jnp.concatenate([_nn(hb16, w_ref[j]) for j in range(N_SHARD)], axis=1)
        qkraw_ref[...] = proj[:, :BRANCH_W + KV_W]
        bdm = bd_ref[...]
        tab_a = (ca[...], la[...], ha[...])
        tab_b = (cb[...], lb[...], hb[...])

        def norm_rope_a(z, gain):
            zn = z * lax.rsqrt(_head_mean(z * z, bdm) + NORM_EPS) * gain
            return _rope(zn, *tab_a, 16)

        def put(ref, first, z):
            zb = z.astype(BF16)
            ref[first] = zb[:, :HEAD_DIM]
            ref[first + 1] = zb[:, HEAD_DIM:]

        for i in range(Q_HEADS // 2):
            put(qa_ref, 2 * i, norm_rope_a(proj[:, LANES * i:LANES * (i + 1)], gq_ref[...]) * Q_SCALE)
        off = BRANCH_W
        put(ka_ref, 0, norm_rope_a(proj[:, off:off + LANES], gk_ref[...]))
        off += KV_W
        def put_v(ref, z):
            zb = z.astype(BF16)
            for hd in range(KV_HEADS):
                ref[hd, :, :HEAD_DIM] = zb[:, HEAD_DIM * hd:HEAD_DIM * (hd + 1)]
                ref[hd, :, HEAD_DIM:] = jnp.ones((tm, HEAD_DIM), BF16)

        put_v(va_ref, proj[:, off:off + LANES])
        off += KV_W
        for i in range(Q_HEADS // 2):
            put(qb_ref, 2 * i, _rope(proj[:, off + LANES * i:off + LANES * (i + 1)], *tab_b, 32) * Q_SCALE)
        off += BRANCH_W
        put(kb_ref, 0, _rope(proj[:, off:off + LANES], *tab_b, 32))
        off += KV_W
        put_v(vb_ref, proj[:, off:off + LANES])
        gate_ref[...] = proj[:, QK_W:].astype(BF16)

    tab_spec = _row_spec(tm, LANES)
    return _call(
        body, name="in_proj", grid=(s // tm,),
        in_specs=[_row_spec(tm, D_MODEL), _full_spec(mod6.shape), _full_spec(g1.shape), _full_spec(w_in_s.shape),
                  _full_spec(gq.shape), _full_spec(gk.shape), _full_spec(bd.shape)] + [tab_spec] * 6,
        out_specs=[_row_spec(tm, D_MODEL), _row_spec(tm, BRANCH_W + KV_W), _heads_spec(Q_HEADS, tm), _heads_spec(KV_HEADS, tm),
                   pl.BlockSpec((KV_HEADS, tm, LANES), lambda i: (0, i, 0)), _heads_spec(Q_HEADS, tm),
                   _heads_spec(KV_HEADS, tm), pl.BlockSpec((KV_HEADS, tm, LANES), lambda i: (0, i, 0)),
                   _row_spec(tm, 2 * D_MODEL)],
        out_shape=[jax.ShapeDtypeStruct((s, D_MODEL), BF16), jax.ShapeDtypeStruct((s, BRANCH_W + KV_W), F32),
                   jax.ShapeDtypeStruct((Q_HEADS, s, HEAD_DIM), BF16), jax.ShapeDtypeStruct((KV_HEADS, s, HEAD_DIM), BF16),
                   jax.ShapeDtypeStruct((KV_HEADS, s, LANES), BF16), jax.ShapeDtypeStruct((Q_HEADS, s, HEAD_DIM), BF16),
                   jax.ShapeDtypeStruct((KV_HEADS, s, HEAD_DIM), BF16), jax.ShapeDtypeStruct((KV_HEADS, s, LANES), BF16),
                   jax.ShapeDtypeStruct((s, 2 * D_MODEL), BF16)],
        compiler_params=_params(("parallel",)),
    )(x, mod6, g1, w_in_s, gq, gk, bd, *tabs)


def _group_specs(s, tq):
    q_spec = pl.BlockSpec((None, GROUP, tq, HEAD_DIM), lambda g, i: (g, 0, i, 0))
    kv_spec = pl.BlockSpec((None, s, HEAD_DIM), lambda g, i: (g, 0, 0))
    col_spec = pl.BlockSpec((None, GROUP, tq, 1), lambda g, i: (g, 0, i, 0))
    return q_spec, kv_spec, col_spec


def _attn_a_fwd(q, k, v1, rider=None, tq=256, tk=2048):
    s = q.shape[1]
    tk = min(tk, s // 2)
    rows = GROUP * tq

    n = s // tk
    assert n >= 2 and n % 2 == 0

    def body(q_ref, k_ref, v_ref, o_ref, oh_ref, lse_ref, s0_ref, s1_ref, p0_ref, p1_ref, m_ref, a_ref, acc_ref):
        s_ref, p_ref = (s0_ref, s1_ref), (p0_ref, p1_ref)
        qq = q_ref[...].reshape(rows, HEAD_DIM)
        m_ref[...] = jnp.full((rows, 1), NEG_INF, F32)
        acc_ref[...] = jnp.zeros((rows, LANES), F32)

        def keys(i):
            return pl.ds(pl.multiple_of(i * tk, tk), tk)

        def scores(i, slot):
            s_ref[slot][...] = _nt(qq, k_ref[keys(i), :])

        def softmax(slot):
            sc = s_ref[slot][...]
            m = m_ref[...]
            mn = jnp.maximum(m, jnp.max(sc, axis=-1, keepdims=True))
            m_ref[...] = mn
            a_ref[...] = jnp.exp(m - mn)
            p_ref[slot][...] = jnp.exp(sc - mn).astype(BF16)

        def weigh(i, slot):
            acc_ref[...] = a_ref[...] * acc_ref[...] + _nn(p_ref[slot][...], v_ref[keys(i), :])

        scores(0, 0)
        softmax(0)
        scores(1, 1)

        def two_steps(j, carry):
            i = 2 * j + 1
            weigh(i - 1, 0)
            softmax(1)
            scores(i + 1, 0)
            weigh(i, 1)
            softmax(0)
            scores(i + 2, 1)
            return carry

        lax.fori_loop(0, (n - 2) // 2, two_steps, 0, unroll=True)
        weigh(n - 2, 0)
        softmax(1)
        weigh(n - 1, 1)
        l = acc_ref[:, HEAD_DIM:HEAD_DIM + 1]
        o = (acc_ref[:, :HEAD_DIM] / l).astype(BF16)
        for g in range(GROUP):
            o_ref[:, HEAD_DIM * g:HEAD_DIM * (g + 1)] = o[tq * g:tq * (g + 1)]
        oh_ref[...] = o.reshape(GROUP, tq, HEAD_DIM)
        lse_ref[...] = (m_ref[...] + jnp.log(l)).reshape(GROUP, tq, 1)

    q_spec, kv_spec, col_spec = _group_specs(s, tq)
    v_spec = pl.BlockSpec((None, s, LANES), lambda g, i: (g, 0, 0))
    return _hosted(
        body, rider, name="attn_a_fwd", grid=(KV_HEADS, s // tq),
        in_specs=[q_spec, kv_spec, v_spec],
        out_specs=[pl.BlockSpec((tq, GROUP * HEAD_DIM), lambda g, i: (i, g)), q_spec, col_spec],
        out_shape=[jax.ShapeDtypeStruct((s, BRANCH_W), BF16), jax.ShapeDtypeStruct((KV_HEADS, GROUP, s, HEAD_DIM), BF16),
                   jax.ShapeDtypeStruct((KV_HEADS, GROUP, s, 1), F32)],
        scratch_shapes=[pltpu.VMEM((rows, tk), F32), pltpu.VMEM((rows, tk), F32), pltpu.VMEM((rows, tk), BF16),
                        pltpu.VMEM((rows, tk), BF16), pltpu.VMEM((rows, 1), F32), pltpu.VMEM((rows, 1), F32),
                        pltpu.VMEM((rows, LANES), F32)],
        args=(q.reshape(KV_HEADS, GROUP, s, HEAD_DIM), k, v1), middle_at=KV_HEADS * (s // tq) // 2)


def _attn_a_bwd(q, k, v1, o, do, lse, rider=None, tq=256, tk=512):
    s = q.shape[1]
    tk = min(tk, s // 2)
    rows = GROUP * tq

    n = s // tk
    assert n >= 2 and n % 2 == 0

    def body(q_ref, k_ref, v_ref, o_ref, do_ref, lse_ref, dq_ref, dk_ref, dv_ref,
             s0_ref, s1_ref, dp0_ref, dp1_ref, p0_ref, p1_ref, ds0_ref, ds1_ref, dq_acc):
        s_ref, dp_ref, p_ref, ds_ref = (s0_ref, s1_ref), (dp0_ref, dp1_ref), (p0_ref, p1_ref), (ds0_ref, ds1_ref)

        @pl.when(pl.program_id(1) == 0)
        def _():
            dk_ref[...] = jnp.zeros_like(dk_ref)
            dv_ref[...] = jnp.zeros_like(dv_ref)

        qq = q_ref[...].reshape(rows, HEAD_DIM)
        dd = do_ref[...].reshape(rows, HEAD_DIM)
        ls = lse_ref[...].reshape(rows, 1)
        dl = jnp.sum(dd.astype(F32) * o_ref[...].reshape(rows, HEAD_DIM).astype(F32), axis=-1, keepdims=True)
        dq_acc[...] = jnp.zeros((rows, HEAD_DIM), F32)

        def keys(i):
            return pl.ds(pl.multiple_of(i * tk, tk), tk)

        def scores(i, slot):
            s_ref[slot][...] = _nt(qq, k_ref[keys(i), :])
            dp_ref[slot][...] = _nt(dd, v_ref[keys(i), :HEAD_DIM])

        def weights(slot):
            p = jnp.exp(s_ref[slot][...] - ls)
            p_ref[slot][...] = p.astype(BF16)
            ds_ref[slot][...] = (p * (dp_ref[slot][...] - dl)).astype(BF16)

        def grads(i, slot):
            dv_ref[keys(i), :] += _tn(p_ref[slot][...], dd)
            dk_ref[keys(i), :] += _tn(ds_ref[slot][...], qq)
            dq_acc[...] += _nn(ds_ref[slot][...], k_ref[keys(i), :])

        scores(0, 0)
        weights(0)
        scores(1, 1)

        def two_steps(j, carry):
            i = 2 * j + 1
            grads(i - 1, 0)
            weights(1)
            scores(i + 1, 0)
            grads(i, 1)
            weights(0)
            scores(i + 2, 1)
            return carry

        lax.fori_loop(0, (n - 2) // 2, two_steps, 0, unroll=True)
        grads(n - 2, 0)
        weights(1)
        grads(n - 1, 1)
        dq_ref[...] = dq_acc[...].astype(BF16).reshape(GROUP, tq, HEAD_DIM)

    q_spec, kv_spec, col_spec = _group_specs(s, tq)
    v_spec = pl.BlockSpec((None, s, LANES), lambda g, i: (g, 0, 0))
    shape4 = (KV_HEADS, GROUP, s, HEAD_DIM)
    tile32, tile16 = pltpu.VMEM((rows, tk), F32), pltpu.VMEM((rows, tk), BF16)
    return _hosted(
        body, rider, name="attn_a_bwd", grid=(KV_HEADS, s // tq),
        in_specs=[q_spec, kv_spec, v_spec, q_spec, q_spec, col_spec],
        out_specs=[q_spec, kv_spec, kv_spec],
        out_shape=[jax.ShapeDtypeStruct(shape4, BF16), jax.ShapeDtypeStruct((KV_HEADS, s, HEAD_DIM), F32),
                   jax.ShapeDtypeStruct((KV_HEADS, s, HEAD_DIM), F32)],
        scratch_shapes=[tile32] * 4 + [tile16] * 4 + [pltpu.VMEM((rows, HEAD_DIM), F32)],
        args=(q.reshape(shape4), k, v1, o.reshape(shape4), do.reshape(shape4), lse))


TQ_B = WINDOW


def _win_keys(tq):
    return tq + 2 * WINDOW


def _window_bias(tq):
    r = jnp.arange(tq, dtype=jnp.int32)[:, None]
    col = jnp.arange(_win_keys(tq), dtype=jnp.int32)[None, :]
    return jnp.stack([jnp.where(jnp.abs(r - col + WINDOW * b) <= WINDOW, 0.0, NEG_INF) for b in range(3)]).astype(F32)


def _band(tq, s):
    win = _win_keys(tq)

    def window(e):
        return pl.ds(pl.multiple_of(jnp.clip(e * tq - WINDOW, 0, s - win), WINDOW), win)

    def bias_index(e):
        return jnp.where(e == 0, 0, jnp.where(e >= s // tq - 1, 2, 1))

    return window, bias_index


def _pair_specs(s, tq):
    pairs = s // (2 * tq)
    cur = lambda g, j: (g, 0, jnp.minimum(j, pairs - 1), 0)
    prev = lambda g, j: (g, 0, jnp.maximum(j - 1, 0), 0)
    tile = lambda width, index: pl.BlockSpec((None, GROUP, 2 * tq, width), index)
    kv_spec = pl.BlockSpec((None, s, HEAD_DIM), lambda g, j: (g, 0, 0))
    v_spec = pl.BlockSpec((None, s, LANES), lambda g, j: (g, 0, 0))
    sink_spec = pl.BlockSpec((None, GROUP * tq, 1), lambda g, j: (g, 0, 0))
    bias_spec = pl.BlockSpec((3, tq, _win_keys(tq)), lambda g, j: (0, 0, 0))
    return tile, cur, prev, kv_spec, v_spec, sink_spec, bias_spec


def _attn_b_fwd(q, k, v1, sink_col, bias, rider=None, tq=TQ_B):
    s = q.shape[1]
    rows = GROUP * tq
    win = _win_keys(tq)
    pairs = s // (2 * tq)
    window, bias_index = _band(tq, s)

    def body(q_ref, k_ref, v_ref, sink_ref, bias_ref, o_ref, oh_ref, lse_ref, s0_ref, s1_ref, p0_ref, p1_ref, m0_ref, m1_ref):
        s_ref, p_ref, m_ref = (s0_ref, s1_ref), (p0_ref, p1_ref), (m0_ref, m1_ref)
        j = pl.program_id(1)

        @pl.when(j == 0)
        def _():
            for ref in (s0_ref, s1_ref, p0_ref, p1_ref, m0_ref, m1_ref):
                ref[...] = jnp.zeros_like(ref)

        def scores(e, slot):
            qq = q_ref[:, pl.ds(slot * tq, tq), :].reshape(rows, HEAD_DIM)
            sc = _nt(qq, k_ref[window(e), :]).reshape(GROUP, tq, win) + bias_ref[bias_index(e)][None]
            s_ref[slot][...] = sc.reshape(rows, win)

        def softmax(slot):
            sc = s_ref[slot][...]
            m = jnp.maximum(jnp.max(sc, axis=-1, keepdims=True), sink_ref[...])
            m_ref[slot][...] = m
            p_ref[slot][...] = jnp.exp(sc - m).astype(BF16)

        def finish(e, slot):
            acc = _nn(p_ref[slot][...], v_ref[window(e), :])
            m = m_ref[slot][...]
            l = acc[:, HEAD_DIM:HEAD_DIM + 1] + jnp.exp(sink_ref[...] - m)
            o = (acc[:, :HEAD_DIM] / l).astype(BF16)
            at = pl.ds(slot * tq, tq)
            for g in range(GROUP):
                o_ref[at, HEAD_DIM * g:HEAD_DIM * (g + 1)] = o[tq * g:tq * (g + 1)]
            oh_ref[:, at, :] = o.reshape(GROUP, tq, HEAD_DIM)
            lse_ref[:, at, :] = (m + jnp.log(l)).reshape(GROUP, tq, 1)

        first = 2 * j
        finish(jnp.maximum(first - 2, 0), 0)
        softmax(1)
        scores(first, 0)
        finish(jnp.maximum(first - 1, 0), 1)
        softmax(0)
        scores(first + 1, 1)

    tile, cur, prev, kv_spec, v_spec, sink_spec, bias_spec = _pair_specs(s, tq)
    tile32, tile16, col = pltpu.VMEM((rows, win), F32), pltpu.VMEM((rows, win), BF16), pltpu.VMEM((rows, 1), F32)
    return _hosted(
        body, rider, name="attn_b_fwd", grid=(KV_HEADS, pairs + 1),
        in_specs=[tile(HEAD_DIM, cur), kv_spec, v_spec, sink_spec, bias_spec],
        out_specs=[pl.BlockSpec((2 * tq, GROUP * HEAD_DIM), lambda g, j: (jnp.maximum(j - 1, 0), g)),
                   tile(HEAD_DIM, prev), tile(1, prev)],
        out_shape=[jax.ShapeDtypeStruct((s, BRANCH_W), BF16), jax.ShapeDtypeStruct((KV_HEADS, GROUP, s, HEAD_DIM), BF16),
                   jax.ShapeDtypeStruct((KV_HEADS, GROUP, s, 1), F32)],
        scratch_shapes=[tile32, tile32, tile16, tile16, col, col],
        args=(q.reshape(KV_HEADS, GROUP, s, HEAD_DIM), k, v1, sink_col, bias))


def _attn_b_bwd(q, k, v1, o, do, lse, sink_col, bias, rider=None, tq=TQ_B):
    s = q.shape[1]
    rows = GROUP * tq
    win = _win_keys(tq)
    pairs = s // (2 * tq)
    window, bias_index = _band(tq, s)

    def body(q_ref, k_ref, v_ref, o_ref, do_ref, lse_ref, sink_ref, bias_ref, dq_ref, dk_ref, dv_ref, dsink_ref,
             s0, s1, dp0, dp1, p0, p1, ds0, ds1, q0, q1, d0, d1, ls0, ls1, dl0, dl1):
        s_ref, dp_ref, p_ref, ds_ref = (s0, s1), (dp0, dp1), (p0, p1), (ds0, ds1)
        q_keep, do_keep, lse_keep, delta_keep = (q0, q1), (d0, d1), (ls0, ls1), (dl0, dl1)
        j = pl.program_id(1)

        @pl.when(j == 0)
        def _():
            for ref in (dk_ref, dv_ref, dsink_ref, s0, s1, dp0, dp1, p0, p1, ds0, ds1, q0, q1, d0, d1, ls0, ls1, dl0, dl1):
                ref[...] = jnp.zeros_like(ref)

        def scores(e, slot):
            at = pl.ds(slot * tq, tq)
            qq = q_ref[:, at, :].reshape(rows, HEAD_DIM)
            dd = do_ref[:, at, :].reshape(rows, HEAD_DIM)
            q_keep[slot][...] = qq
            do_keep[slot][...] = dd
            lse_keep[slot][...] = lse_ref[:, at, :].reshape(rows, 1)
            delta_keep[slot][...] = jnp.sum(dd.astype(F32) * o_ref[:, at, :].reshape(rows, HEAD_DIM).astype(F32), axis=-1,
                                            keepdims=True)
            sc = _nt(qq, k_ref[window(e), :]).reshape(GROUP, tq, win) + bias_ref[bias_index(e)][None]
            s_ref[slot][...] = sc.reshape(rows, win)
            dp_ref[slot][...] = _nt(dd, v_ref[window(e), :HEAD_DIM])

        def weights(slot):
            p = jnp.exp(s_ref[slot][...] - lse_keep[slot][...])
            p_ref[slot][...] = p.astype(BF16)
            ds_ref[slot][...] = (p * (dp_ref[slot][...] - delta_keep[slot][...])).astype(BF16)

        def grads(e, slot, live):
            at = window(e)
            ds = ds_ref[slot][...]
            dv_ref[at, :] += _tn(p_ref[slot][...], do_keep[slot][...])
            dk_ref[at, :] += _tn(ds, q_keep[slot][...])
            dq_ref[:, pl.ds(slot * tq, tq), :] = _nn(ds, k_ref[at, :]).astype(BF16).reshape(GROUP, tq, HEAD_DIM)
            dsk = jnp.exp(sink_ref[...] - lse_keep[slot][...]) * delta_keep[slot][...] * live
            for g in range(GROUP):
                dsink_ref[g:g + 1, :] -= jnp.broadcast_to(jnp.sum(dsk[tq * g:tq * (g + 1)], axis=0, keepdims=True), (1, LANES))

        first = 2 * j
        live = jnp.where(j > 0, 1.0, 0.0)
        grads(jnp.maximum(first - 2, 0), 0, live)
        weights(1)
        scores(first, 0)
        grads(jnp.maximum(first - 1, 0), 1, live)
        weights(0)
        scores(first + 1, 1)

    tile, cur, prev, kv_spec, v_spec, sink_spec, bias_spec = _pair_specs(s, tq)
    dsink_spec = pl.BlockSpec((None, ACC_ROWS, LANES), lambda g, j: (g, 0, 0))
    shape4 = (KV_HEADS, GROUP, s, HEAD_DIM)
    tile32, tile16 = pltpu.VMEM((rows, win), F32), pltpu.VMEM((rows, win), BF16)
    keep, col = pltpu.VMEM((rows, HEAD_DIM), BF16), pltpu.VMEM((rows, 1), F32)
    return _hosted(
        body, rider, name="attn_b_bwd", grid=(KV_HEADS, pairs + 1),
        in_specs=[tile(HEAD_DIM, cur), kv_spec, v_spec, tile(HEAD_DIM, cur), tile(HEAD_DIM, cur), tile(1, cur), sink_spec,
                  bias_spec],
        out_specs=[tile(HEAD_DIM, prev), kv_spec, kv_spec, dsink_spec],
        out_shape=[jax.ShapeDtypeStruct(shape4, BF16), jax.ShapeDtypeStruct((KV_HEADS, s, HEAD_DIM), F32),
                   jax.ShapeDtypeStruct((KV_HEADS, s, HEAD_DIM), F32), jax.ShapeDtypeStruct((KV_HEADS, ACC_ROWS, LANES), F32)],
        scratch_shapes=[tile32] * 4 + [tile16] * 4 + [keep] * 4 + [col] * 4,
        args=(q.reshape(shape4), k, v1, o.reshape(shape4), do.reshape(shape4), lse, sink_col, bias))


def _post_attn(ya, yb, gates, x, mod6, wbr_s, w_out, tm=512):
    s = x.shape[0]

    def body(ya_ref, yb_ref, g_ref, x_ref, mod_ref, wbr_ref, wo_ref, ua_ref, ub_ref, mg_ref, o_ref, x1_ref):
        ya_t, yb_t = ya_ref[...], yb_ref[...]
        ua = jnp.concatenate([_nn(ya_t, wbr_ref[j, 0]) for j in range(N_SHARD)], axis=1)
        ub = jnp.concatenate([_nn(yb_t, wbr_ref[j, 1]) for j in range(N_SHARD)], axis=1)
        ga, gb = g_ref[:, :D_MODEL].astype(F32), g_ref[:, D_MODEL:].astype(F32)
        merged = (_sigmoid(ga) * ua + _sigmoid(gb) * ub).astype(BF16)
        o = _nn(merged, wo_ref[...])
        ua_ref[...] = ua.astype(BF16)
        ub_ref[...] = ub.astype(BF16)
        mg_ref[...] = merged
        o_ref[...] = o.astype(BF16)
        x1_ref[...] = x_ref[...] + mod_ref[2:3, :] * o

    bf = jax.ShapeDtypeStruct((s, D_MODEL), BF16)
    return _call(
        body, name="post_attn", grid=(s // tm,),
        in_specs=[_row_spec(tm, BRANCH_W), _row_spec(tm, BRANCH_W), _row_spec(tm, 2 * D_MODEL), _row_spec(tm, D_MODEL),
                  _full_spec(mod6.shape), _full_spec(wbr_s.shape), _full_spec(w_out.shape)],
        out_specs=[_row_spec(tm, D_MODEL)] * 5,
        out_shape=[bf, bf, bf, bf, jax.ShapeDtypeStruct((s, D_MODEL), F32)],
        compiler_params=_params(("parallel",)),
    )(ya, yb, gates, x, mod6, wbr_s, w_out)


def _mlp_in(x1, mod6, g2, w_mi_s, tm=512):
    s = x1.shape[0]

    def body(x_ref, mod_ref, g_ref, w_ref, h2_ref, a_ref, hid_ref):
        xt = x_ref[...]
        r = lax.rsqrt(jnp.mean(xt * xt, axis=-1, keepdims=True) + NORM_EPS)
        h2 = ((xt * r * g_ref[...]) * (1.0 + mod_ref[4:5, :]) + mod_ref[3:4, :]).astype(BF16)
        h2_ref[...] = h2
        a = jnp.concatenate([_nn(h2, w_ref[j]) for j in range(N_SHARD)], axis=1)
        a_ref[...] = a.astype(BF16)
        hid_ref[...] = jnp.square(jnp.maximum(a, 0.0)).astype(BF16)

    return _call(
        body, name="mlp_in", grid=(s // tm,),
        in_specs=[_row_spec(tm, D_MODEL), _full_spec(mod6.shape), _full_spec(g2.shape), _full_spec(w_mi_s.shape)],
        out_specs=[_row_spec(tm, D_MODEL), _row_spec(tm, D_FF), _row_spec(tm, D_FF)],
        out_shape=[jax.ShapeDtypeStruct((s, D_MODEL), BF16), jax.ShapeDtypeStruct((s, D_FF), BF16),
                   jax.ShapeDtypeStruct((s, D_FF), BF16)],
        compiler_params=_params(("parallel",)),
    )(x1, mod6, g2, w_mi_s)


ACC_ROWS = 8


def _acc_spec():
    return pl.BlockSpec((ACC_ROWS, D_MODEL), lambda i: (0, 0))


def _acc_add(acc_ref, rows):
    @pl.when(pl.program_id(0) == 0)
    def _():
        acc_ref[...] = jnp.zeros_like(acc_ref)

    for r, val in enumerate(rows):
        acc_ref[r:r + 1, :] += jnp.sum(val, axis=0, keepdims=True)


def _mlp_out_loss(hid, x1, a, target, mod6, gf, w_mo, tm=512):
    s = x1.shape[0]

    def body(hid_ref, x_ref, a_ref, t_ref, mod_ref, gf_ref, w_ref, dx2_ref, dm_ref, da_ref, acc_ref):
        m = _nn(hid_ref[...], w_ref[...])
        gate2 = mod_ref[5:6, :]
        x2 = x_ref[...] + gate2 * m
        r = lax.rsqrt(jnp.mean(x2 * x2, axis=-1, keepdims=True) + NORM_EPS)
        xn = x2 * r
        err = xn * gf_ref[...] - t_ref[...]
        dy = err * (1.0 / D_MODEL)
        dxn = dy * gf_ref[...]
        dx2 = r * (dxn - xn * jnp.mean(dxn * xn, axis=-1, keepdims=True))
        dx2_ref[...] = dx2
        dm = (dx2 * gate2).astype(BF16)
        dm_ref[...] = dm
        da_ref[...] = (_nt(dm, w_ref[...]) * (2.0 * jnp.maximum(a_ref[...].astype(F32), 0.0))).astype(BF16)
        _acc_add(acc_ref, [err * err, dy * xn, dx2 * m])

    return _call(
        body, name="mlp_out_loss", grid=(s // tm,),
        in_specs=[_row_spec(tm, D_FF), _row_spec(tm, D_MODEL), _row_spec(tm, D_FF), _row_spec(tm, D_MODEL),
                  _full_spec(mod6.shape), _full_spec(gf.shape), _full_spec(w_mo.shape)],
        out_specs=[_row_spec(tm, D_MODEL), _row_spec(tm, D_MODEL), _row_spec(tm, D_FF), _acc_spec()],
        out_shape=[jax.ShapeDtypeStruct((s, D_MODEL), F32), jax.ShapeDtypeStruct((s, D_MODEL), BF16),
                   jax.ShapeDtypeStruct((s, D_FF), BF16), jax.ShapeDtypeStruct((ACC_ROWS, D_MODEL), F32)],
        compiler_params=_params(("arbitrary",)),
    )(hid, x1, a, target, mod6, gf, w_mo)


def _norm_bwd(dh, xt, gain, scale):
    r = lax.rsqrt(jnp.mean(xt * xt, axis=-1, keepdims=True) + NORM_EPS)
    xn = xt * r
    dxn = dh * (gain * (1.0 + scale))
    dx = r * (dxn - xn * jnp.mean(dxn * xn, axis=-1, keepdims=True))
    return dx, [dh, dh * xn * gain, dh * xn * (1.0 + scale)]


def _mlp_bwd(da, x1, dx2, o, mod6, g2, w_mi_s, rider=None, tm=512):
    s = x1.shape[0]

    def body(da_ref, x_ref, dx2_ref, o_ref, mod_ref, g_ref, w_ref, dx1_ref, do_ref, acc_ref):
        dh2 = _nt(da_ref[:, :D_MODEL], w_ref[0])
        for j in range(1, N_SHARD):
            dh2 += _nt(da_ref[:, D_MODEL * j:D_MODEL * (j + 1)], w_ref[j])
        dx, sums = _norm_bwd(dh2, x_ref[...], g_ref[...], mod_ref[4:5, :])
        dx1 = dx2_ref[...] + dx
        dx1_ref[...] = dx1
        do_ref[...] = (dx1 * mod_ref[2:3, :]).astype(BF16)
        _acc_add(acc_ref, sums + [dx1 * o_ref[...].astype(F32)])

    return _hosted(
        body, rider, name="mlp_bwd", grid=(s // tm,),
        in_specs=[_row_spec(tm, D_FF), _row_spec(tm, D_MODEL), _row_spec(tm, D_MODEL), _row_spec(tm, D_MODEL),
                  _full_spec(mod6.shape), _full_spec(g2.shape), _full_spec(w_mi_s.shape)],
        out_specs=[_row_spec(tm, D_MODEL), _row_spec(tm, D_MODEL), _acc_spec()],
        out_shape=[jax.ShapeDtypeStruct((s, D_MODEL), F32), jax.ShapeDtypeStruct((s, D_MODEL), BF16),
                   jax.ShapeDtypeStruct((ACC_ROWS, D_MODEL), F32)],
        args=(da, x1, dx2, o, mod6, g2, w_mi_s))


def _merge_bwd(do, gates, ua, ub, w_out, wbr_s, rider=None, tm=512):
    s = do.shape[0]

    def body(do_ref, g_ref, ua_ref, ub_ref, wo_ref, wbr_ref, dua_ref, dub_ref, dg_ref, doa_ref, dob_ref):
        dmerged = _nt(do_ref[...], wo_ref[...])
        for b, (u_ref, du_ref, dy_ref) in enumerate(((ua_ref, dua_ref, doa_ref), (ub_ref, dub_ref, dob_ref))):
            sg = _sigmoid(g_ref[:, D_MODEL * b:D_MODEL * (b + 1)].astype(F32))
            du = (dmerged * sg).astype(BF16)
            du_ref[...] = du
            dg_ref[:, D_MODEL * b:D_MODEL * (b + 1)] = (dmerged * u_ref[...].astype(F32) * sg * (1.0 - sg)).astype(BF16)
            w = BRANCH_W // 2
            dy = _nt(du[:, :w], wbr_ref[0, b])
            for j in range(1, N_SHARD):
                dy += _nt(du[:, w * j:w * (j + 1)], wbr_ref[j, b])
            dyb = dy.astype(BF16)
            for h in range(Q_HEADS):
                dy_ref[h] = dyb[:, HEAD_DIM * h:HEAD_DIM * (h + 1)]

    bf = jax.ShapeDtypeStruct((s, D_MODEL), BF16)
    heads = jax.ShapeDtypeStruct((Q_HEADS, s, HEAD_DIM), BF16)
    return _hosted(
        body, rider, name="merge_bwd", grid=(s // tm,),
        in_specs=[_row_spec(tm, D_MODEL), _row_spec(tm, 2 * D_MODEL), _row_spec(tm, D_MODEL), _row_spec(tm, D_MODEL),
                  _full_spec(w_out.shape), _full_spec(wbr_s.shape)],
        out_specs=[_row_spec(tm, D_MODEL), _row_spec(tm, D_MODEL), _row_spec(tm, 2 * D_MODEL),
                   _heads_spec(Q_HEADS, tm), _heads_spec(Q_HEADS, tm)],
        out_shape=[bf, bf, jax.ShapeDtypeStruct((s, 2 * D_MODEL), BF16), heads, heads],
        args=(do, gates, ua, ub, w_out, wbr_s))


def _qk_bwd(dqa, dka, dva, dqb, dkb, dvb, qkraw, dgates, gq, gk, bd, tabs, rider=None, tm=512):
    s = qkraw.shape[0]

    def body(dqa_ref, dka_ref, dva_ref, dqb_ref, dkb_ref, dvb_ref, raw_ref, dg_ref, gq_ref, gk_ref, bd_ref,
             ca, la, ha, cb, lb, hb, dp_ref, acc_ref, pair_ref):
        bdm = bd_ref[...]
        tab_a = (ca[...], la[...], ha[...])
        tab_b = (cb[...], lb[...], hb[...])

        def pair(ref, first):
            pair_ref[:, :HEAD_DIM] = ref[first].astype(F32)
            pair_ref[:, HEAD_DIM:] = ref[first + 1].astype(F32)
            return pair_ref[...]

        def norm_rope_a_bwd(dz, raw, gain):
            dzn = _rope(dz, *tab_a, 16, sign=-1.0)
            rinv = lax.rsqrt(_head_mean(raw * raw, bdm) + NORM_EPS)
            zhat = raw * rinv
            dzhat = dzn * gain
            return rinv * (dzhat - zhat * _head_mean(dzhat * zhat, bdm)), dzn * zhat

        gq_rows = jnp.zeros((tm, LANES), F32)
        for i in range(Q_HEADS // 2):
            at = slice(LANES * i, LANES * (i + 1))
            draw, gsum = norm_rope_a_bwd(pair(dqa_ref, 2 * i) * Q_SCALE, raw_ref[:, at], gq_ref[...])
            dp_ref[:, at] = draw.astype(BF16)
            gq_rows += gsum
        off = BRANCH_W
        draw, gk_rows = norm_rope_a_bwd(pair(dka_ref, 0), raw_ref[:, off:off + LANES], gk_ref[...])
        dp_ref[:, off:off + LANES] = draw.astype(BF16)
        off += KV_W
        dp_ref[:, off:off + LANES] = pair(dva_ref, 0).astype(BF16)
        off += KV_W
        for i in range(Q_HEADS // 2):
            dz = _rope(pair(dqb_ref, 2 * i) * Q_SCALE, *tab_b, 32, sign=-1.0)
            dp_ref[:, off + LANES * i:off + LANES * (i + 1)] = dz.astype(BF16)
        off += BRANCH_W
        dp_ref[:, off:off + LANES] = _rope(pair(dkb_ref, 0), *tab_b, 32, sign=-1.0).astype(BF16)
        off += KV_W
        dp_ref[:, off:off + LANES] = pair(dvb_ref, 0).astype(BF16)
        dp_ref[:, QK_W:] = dg_ref[...]

        @pl.when(pl.program_id(0) == 0)
        def _():
            acc_ref[...] = jnp.zeros_like(acc_ref)

        acc_ref[0:1, :] += jnp.sum(gq_rows, axis=0, keepdims=True)
        acc_ref[1:2, :] += jnp.sum(gk_rows, axis=0, keepdims=True)

    tab_spec = _row_spec(tm, LANES)
    return _hosted(
        body, rider, name="qk_bwd", grid=(s // tm,),
        in_specs=[_heads_spec(Q_HEADS, tm), _heads_spec(KV_HEADS, tm), _heads_spec(KV_HEADS, tm),
                  _heads_spec(Q_HEADS, tm), _heads_spec(KV_HEADS, tm), _heads_spec(KV_HEADS, tm),
                  _row_spec(tm, BRANCH_W + KV_W), _row_spec(tm, 2 * D_MODEL),
                  _full_spec(gq.shape), _full_spec(gk.shape), _full_spec(bd.shape)] + [tab_spec] * 6,
        out_specs=[_row_spec(tm, IN_W), pl.BlockSpec((ACC_ROWS, LANES), lambda i: (0, 0))],
        out_shape=[jax.ShapeDtypeStruct((s, IN_W), BF16), jax.ShapeDtypeStruct((ACC_ROWS, LANES), F32)],
        scratch_shapes=[pltpu.VMEM((tm, LANES), F32)],
        args=(dqa, dka, dva, dqb, dkb, dvb, qkraw, dgates, gq, gk, bd, *tabs))


def _in_proj_bwd(dproj, x, dx1, mod6, g1, w_in_s, tm=512):
    s = x.shape[0]
    w = IN_W // N_SHARD

    def body(dp_ref, x_ref, dx1_ref, mod_ref, g_ref, w_ref, gx_ref, acc_ref):
        dh = _nt(dp_ref[:, :w], w_ref[0])
        for j in range(1, N_SHARD):
            dh += _nt(dp_ref[:, w * j:w * (j + 1)], w_ref[j])
        dx, sums = _norm_bwd(dh, x_ref[...], g_ref[...], mod_ref[1:2, :])
        gx_ref[...] = dx1_ref[...] + dx
        _acc_add(acc_ref, sums)

    return _call(
        body, name="in_proj_bwd", grid=(s // tm,),
        in_specs=[_row_spec(tm, IN_W), _row_spec(tm, D_MODEL), _row_spec(tm, D_MODEL),
                  _full_spec(mod6.shape), _full_spec(g1.shape), _full_spec(w_in_s.shape)],
        out_specs=[_row_spec(tm, D_MODEL), _acc_spec()],
        out_shape=[jax.ShapeDtypeStruct((s, D_MODEL), F32), jax.ShapeDtypeStruct((ACC_ROWS, D_MODEL), F32)],
        compiler_params=_params(("arbitrary",)),
    )(dproj, x, dx1, mod6, g1, w_in_s)


def _wgrad(name, a, b, out_shape, out_spec, tm, tn, tk=4096):
    s, m = a.shape
    n = b.shape[1]
    tk = min(tk, s)
    nk = s // tk

    def body(a_ref, b_ref, o_ref, acc_ref):
        k = pl.program_id(2)

        @pl.when(k == 0)
        def _():
            acc_ref[...] = jnp.zeros_like(acc_ref)

        acc_ref[...] += _tn(a_ref[...], b_ref[...])

        @pl.when(k == nk - 1)
        def _():
            o_ref[...] = acc_ref[...].reshape(o_ref.shape)

    return _call(
        body, name=name, grid=(m // tm, n // tn, nk),
        in_specs=[pl.BlockSpec((tk, tm), lambda i, j, k: (k, i)), pl.BlockSpec((tk, tn), lambda i, j, k: (k, j))],
        out_specs=out_spec, out_shape=jax.ShapeDtypeStruct(out_shape, F32),
        scratch_shapes=[pltpu.VMEM((tm, tn), F32)],
        compiler_params=_params(("parallel", "parallel", "arbitrary")),
    )(a, b)


def _wgrad_branch(ya, yb, dua, dub, tk=2048):
    s = ya.shape[0]
    tk = min(tk, s)
    nk = s // tk
    w = D_MODEL // N_SHARD

    def body(ya_ref, yb_ref, dua_ref, dub_ref, o_ref, acc_ref):
        b, k = pl.program_id(0), pl.program_id(1)

        @pl.when(k == 0)
        def _():
            acc_ref[...] = jnp.zeros_like(acc_ref)

        @pl.when(b == 0)
        def _():
            acc_ref[...] += _tn(ya_ref[...], dua_ref[...])

        @pl.when(b == 1)
        def _():
            acc_ref[...] += _tn(yb_ref[...], dub_ref[...])

        @pl.when(k == nk - 1)
        def _():
            for j in range(N_SHARD):
                o_ref[j] = acc_ref[:, w * j:w * (j + 1)]

    first = lambda width: pl.BlockSpec((tk, width), lambda b, k: (k * (1 - b), 0))
    second = lambda width: pl.BlockSpec((tk, width), lambda b, k: (k * b, 0))
    return _call(
        body, name="wgrad_branch", grid=(2, nk),
        in_specs=[first(BRANCH_W), second(BRANCH_W), first(D_MODEL), second(D_MODEL)],
        out_specs=pl.BlockSpec((N_SHARD, None, BRANCH_W, w), lambda b, k: (0, b, 0, 0)),
        out_shape=jax.ShapeDtypeStruct((N_SHARD, 2, BRANCH_W, w), F32),
        scratch_shapes=[pltpu.VMEM((BRANCH_W, D_MODEL), F32)],
        compiler_params=_params(("parallel", "arbitrary")),
    )(ya, yb, dua, dub)


def _local_step(x, target, mod6, g1, g2, gf, gq2, gk2, sink, w_in_s, rest, cj=None, tabs=None):
    s = x.shape[0]
    dist = cj is not None
    tabs = _rope_tables(s) if tabs is None else tabs
    bd = _block_diag()
    sink_col = jnp.repeat(sink.reshape(KV_HEADS, GROUP, 1), TQ_B, axis=1).reshape(KV_HEADS, GROUP * TQ_B, 1)
    shard = D_MODEL // N_SHARD

    h, qkraw, qa, ka, va, qb, kb, vb, gates = _in_proj(x, mod6, g1, w_in_s, gq2, gk2, bd, tabs)
    bias = _window_bias(TQ_B)
    (yb, yb_heads, lse_b), _ = _attn_b_fwd(qb, kb, vb, sink_col, bias)
    (ya, ya_heads, lse_a), gathered = _attn_a_fwd(qa, ka, va, rider=_gather_rider(rest) if dist else None)
    wbr_s, w_out, w_mi_s, w_mo = gathered if dist else rest
    wbr_s = wbr_s.reshape(N_SHARD, 2, BRANCH_W, shard)
    w_out = w_out.reshape(D_MODEL, D_MODEL)
    w_mo = w_mo.reshape(D_FF, D_MODEL)
    ua, ub, merged, o, x1 = _post_attn(ya, yb, gates, x, mod6, wbr_s, w_out)
    h2, a, hid = _mlp_in(x1, mod6, g2, w_mi_s)
    dx2, dm, da, acc_out = _mlp_out_loss(hid, x1, a, target, mod6, gf, w_mo)

    g_w_mo = _wgrad("wgrad_mlp_out", hid, dm, (D_FF, D_MODEL), pl.BlockSpec((D_MODEL, D_MODEL), lambda i, j, k: (i, 0)),
                    D_MODEL, D_MODEL).reshape(N_SHARD, D_MODEL, D_MODEL)
    g_w_mi = _wgrad("wgrad_mlp_in", h2, da, (N_SHARD, D_MODEL, D_MODEL),
                    pl.BlockSpec((None, D_MODEL, D_MODEL), lambda i, j, k: (j, i, 0)), D_MODEL, D_MODEL)
    mlp = _Reduction(("mlp_out", "mlp_in"), (g_w_mo, g_w_mi), cj)
    (dx1, do, acc_mlp), got = _mlp_bwd(da, x1, dx2, o, mod6, g2, w_mi_s, rider=mlp.swap() if dist else None)
    (dua, dub, dgates, doa, dob), _ = _merge_bwd(do, gates, ua, ub, w_out, wbr_s)
    g_w_out = _wgrad("wgrad_out", merged, do, (D_MODEL, D_MODEL), pl.BlockSpec((D_MODEL, D_MODEL), lambda i, j, k: (i, 0)),
                     D_MODEL, D_MODEL).reshape(N_SHARD, shard, D_MODEL)
    g_wbr = _wgrad_branch(ya, yb, dua, dub)
    out = _Reduction(("out", "branch"), (g_w_out, g_wbr.reshape(N_SHARD, 2 * BRANCH_W, shard)), cj)
    (dqa, dka, dva), landings = _attn_a_bwd(qa, ka, va, ya_heads, doa, lse_a,
                                            rider=_riders(mlp.add(got), out.swap()) if dist else None)
    (dqb, dkb, dvb, dsink), joined = _attn_b_bwd(qb, kb, vb, yb_heads, dob, lse_b, sink_col, bias,
                                                 rider=mlp.total(landings[:2]) if dist else None)
    heads = (Q_HEADS, s, HEAD_DIM)
    (dproj, acc_qk), landed = _qk_bwd(dqa.reshape(heads), dka, dva, dqb.reshape(heads), dkb, dvb, qkraw, dgates, gq2, gk2, bd,
                                      tabs, rider=out.add(landings[2:]) if dist else None)
    w = IN_W // N_SHARD
    g_w_in = _wgrad("wgrad_in", h, dproj, (N_SHARD, D_MODEL, w), pl.BlockSpec((None, D_MODEL, w), lambda i, j, k: (j, i, 0)),
                    D_MODEL, w)
    grad_x, acc_in = _in_proj_bwd(dproj, x, dx1, mod6, g1, w_in_s)
    accs = (acc_out, acc_mlp, acc_in, acc_qk, dsink)
    if not dist:
        return grad_x, (g_w_in, g_wbr, g_w_out, g_w_mi, g_w_mo), accs
    r_mo, r_mi = joined
    return grad_x, (_Reduction(("in",), (g_w_in,), cj), out.total(landed), r_mi, r_mo), accs


def _me():
    return lax.axis_index("x"), lax.axis_index("y"), lax.axis_index("c")


def _peer(d):
    x, y, c = _me()
    return (1 - x if d & 4 else x, 1 - y if d & 2 else y, 1 - c if d & 1 else c)


def _dev_index(p):
    return 4 * p[0] + 2 * p[1] + p[2]


def _chip_index(p):
    return 2 * p[0] + p[1]


def _remote(src, dst, send_sem, recv_sem, to):
    return pltpu.make_async_remote_copy(src_ref=src, dst_ref=dst, send_sem=send_sem, recv_sem=recv_sem,
                                        device_id=to, device_id_type=MESH)


SLOT_ROWS = 8


def _ada_fwd(c, w_ada, b4, inv, s, rider=None):
    cols = w_ada.shape[1]
    chunk = 1024

    def tables(inv_ref, tab_refs):
        lane = lax.broadcasted_iota(jnp.int32, (chunk, LANES), 1)
        is_col = (lane & (HEAD_DIM - 1)) >= HEAD_DIM // 2
        first_a = (lane & 31) < 16
        first_b = (lane & (HEAD_DIM - 1)) < HEAD_DIM // 2

        def rows(i, carry):
            at = pl.ds(pl.multiple_of(i * chunk, chunk), chunk)
            t = i * chunk + lax.broadcasted_iota(jnp.int32, (chunk, LANES), 0)
            pos_a = jnp.where(is_col, t & (GRID_W - 1), t // GRID_W).astype(F32)
            for pos, row, first, out in ((pos_a, 0, first_a, tab_refs[:3]), (t.astype(F32), 1, first_b, tab_refs[3:])):
                ang = pos * inv_ref[row:row + 1, :]
                sin = jnp.sin(ang)
                out[0][at, :] = jnp.cos(ang)
                out[1][at, :] = jnp.where(first, -sin, 0.0)
                out[2][at, :] = jnp.where(first, 0.0, sin)
            return carry

        lax.fori_loop(0, s // chunk, rows, 0)

    def body(c_ref, w_ref, b_ref, inv_ref, mod_ref, sc_ref, ca, la, ha, cb, lb, hb, cbuf, pbuf, mbuf, send1, recv1, send2, recv2,
             launch=None):
        me = _me()
        mine, chip = _dev_index(me), _chip_index(me)
        cbuf[mine] = jnp.broadcast_to(c_ref[...], (SLOT_ROWS, D_MODEL))
        gather = [_remote(cbuf.at[mine], cbuf.at[mine], send1.at[d - 1], recv1.at[d - 1], _peer(d)) for d in range(1, N_DEV)]
        for cp in gather:
            cp.start()
        if launch is not None:
            launch()
        tables(inv_ref, (ca, la, ha, cb, lb, hb))
        for d in range(1, N_DEV):
            _remote(cbuf.at[mine], cbuf.at[_dev_index(_peer(d))], send1.at[d - 1], recv1.at[d - 1], _peer(d)).wait_recv()
        call = cbuf[...].reshape(N_DEV * SLOT_ROWS, D_MODEL)
        sc = call * _sigmoid(call)
        for s in range(N_DEV):
            sc_ref[s:s + 1, :] = sc[SLOT_ROWS * s:SLOT_ROWS * s + 1]
        part = _nn(sc.astype(BF16), w_ref[...].astype(BF16)) + b_ref[pl.ds(chip, 1), :]
        pbuf[...] = part.reshape(N_DEV, SLOT_ROWS, cols)
        mbuf[chip] = pbuf[mine]
        spread = [_remote(pbuf.at[_dev_index(_peer(d))], mbuf.at[chip], send2.at[d // 2 - 1], recv2.at[d // 2 - 1], _peer(d))
                  for d in (2, 4, 6)]
        for cp in spread:
            cp.start()
        for d in (2, 4, 6):
            _remote(pbuf.at[mine], mbuf.at[_chip_index(_peer(d))], send2.at[d // 2 - 1], recv2.at[d // 2 - 1],
                    _peer(d)).wait_recv()
        half = D_MODEL // 2
        for p in range(2 * 6):
            col = half * p
            mod_ref[p // 2:p // 2 + 1, half * (p % 2):half * (p % 2 + 1)] = mbuf[col // cols, 0:1, col % cols:col % cols + half]
        for cp in gather + spread:
            cp.wait_send()

    vm = pl.BlockSpec(memory_space=pltpu.VMEM)
    return _hosted(
        body, rider, name="ada_fwd", grid=(), in_specs=[vm] * 4, out_specs=[vm] * 8,
        out_shape=[jax.ShapeDtypeStruct((6, D_MODEL), F32), jax.ShapeDtypeStruct((N_DEV, D_MODEL), F32)]
        + [jax.ShapeDtypeStruct((s, LANES), F32)] * 6,
        scratch_shapes=[pltpu.VMEM((N_DEV, SLOT_ROWS, D_MODEL), F32), pltpu.VMEM((N_DEV, SLOT_ROWS, cols), F32),
                        pltpu.VMEM((N_SHARD, SLOT_ROWS, cols), F32),
                        pltpu.SemaphoreType.DMA((N_DEV - 1,)), pltpu.SemaphoreType.DMA((N_DEV - 1,)),
                        pltpu.SemaphoreType.DMA((N_SHARD - 1,)), pltpu.SemaphoreType.DMA((N_SHARD - 1,))],
        args=(c, w_ada, b4, inv))


PACK_ROWS = 16
PACK_W = 3 * D_MODEL


def _ada_bwd(acc_out, acc_mlp, acc_in, acc_qk, dsink, sc_all, rider=None):
    cols = 6 * D_MODEL // N_SHARD

    def body(out_ref, mlp_ref, in_ref, qk_ref, dsink_ref, sc_ref,
             gwa_ref, gba_ref, gn1_ref, gn2_ref, gf_ref, gq_ref, gk_ref, gs_ref, loss_ref, blk, send, recv, launch=None):
        me = _me()
        mine, chip = _dev_index(me), _chip_index(me)
        blk[mine] = jnp.zeros((PACK_ROWS, PACK_W), F32)
        dmod = (in_ref, 0), (in_ref, 1), (mlp_ref, 3), (mlp_ref, 0), (mlp_ref, 1), (out_ref, 2)
        half = D_MODEL // 2
        for p in range(2 * 6):
            ref, row = dmod[p // 2]
            col = half * p
            blk[mine, col // cols:col // cols + 1, col % cols:col % cols + half] = ref[row:row + 1, half * (p % 2):half * (p % 2 + 1)]
        blk[mine, 4:5, 0:D_MODEL] = in_ref[2:3, :]
        blk[mine, 4:5, D_MODEL:2 * D_MODEL] = mlp_ref[2:3, :]
        blk[mine, 4:5, 2 * D_MODEL:] = out_ref[1:2, :]
        blk[mine, 5:6, 0:LANES] = qk_ref[0:1, :]
        blk[mine, 5:6, LANES:2 * LANES] = qk_ref[1:2, :]
        blk[mine, 6:7, 0:D_MODEL] = out_ref[0:1, :]
        for g in range(KV_HEADS):
            blk[mine, 8 + GROUP * g:8 + GROUP * (g + 1), 0:LANES] = dsink_ref[g, 0:GROUP, :]
        copies = [_remote(blk.at[mine], blk.at[mine], send.at[d - 1], recv.at[d - 1], _peer(d)) for d in range(1, N_DEV)]
        for cp in copies:
            cp.start()
        if launch is not None:
            launch()
        for d in range(1, N_DEV):
            _remote(blk.at[mine], blk.at[_dev_index(_peer(d))], send.at[d - 1], recv.at[d - 1], _peer(d)).wait_recv()
        tot = blk[0]
        for s in range(1, N_DEV):
            tot = tot + blk[s]
        for j in range(N_SHARD):
            gba_ref[:, cols * j:cols * (j + 1)] = tot[j:j + 1, :cols]
        gn1_ref[...] = tot[4:5, 0:D_MODEL]
        gn2_ref[...] = tot[4:5, D_MODEL:2 * D_MODEL]
        gf_ref[...] = tot[4:5, 2 * D_MODEL:]
        gq_ref[...] = tot[5:6, 0:HEAD_DIM] + tot[5:6, HEAD_DIM:2 * HEAD_DIM]
        gk_ref[...] = tot[5:6, LANES:LANES + HEAD_DIM] + tot[5:6, LANES + HEAD_DIM:2 * LANES]
        sq = tot[8:16, 0:Q_HEADS]
        diag = lax.broadcasted_iota(jnp.int32, sq.shape, 0) == lax.broadcasted_iota(jnp.int32, sq.shape, 1)
        gs_ref[...] = jnp.sum(jnp.where(diag, sq, 0.0), axis=0, keepdims=True)
        half_mse = (0.5 / D_MODEL) * jnp.sum(tot[6:7, 0:D_MODEL], axis=-1, keepdims=True)
        loss_ref[...] = jnp.broadcast_to(half_mse, (1, LANES))
        dm = jnp.concatenate([blk[s, pl.ds(chip, 1), pl.ds(0, cols)] for s in range(N_DEV)], axis=0)
        gwa_ref[...] = _tn(sc_ref[...], dm)
        for cp in copies:
            cp.wait_send()

    vm = pl.BlockSpec(memory_space=pltpu.VMEM)
    row = lambda n: jax.ShapeDtypeStruct((1, n), F32)
    return _hosted(
        body, rider, name="ada_bwd", grid=(), in_specs=[vm] * 6, out_specs=[vm] * 9,
        out_shape=[jax.ShapeDtypeStruct((D_MODEL, cols), F32), row(6 * D_MODEL), row(D_MODEL), row(D_MODEL), row(D_MODEL),
                   row(HEAD_DIM), row(HEAD_DIM), row(Q_HEADS), row(LANES)],
        scratch_shapes=[pltpu.VMEM((N_DEV, PACK_ROWS, PACK_W), F32),
                        pltpu.SemaphoreType.DMA((N_DEV - 1,)), pltpu.SemaphoreType.DMA((N_DEV - 1,))],
        args=(acc_out, acc_mlp, acc_in, acc_qk, dsink, sc_all))


def _cast_weights(ws):
    n = len(ws)

    def body(*refs):
        src, out, tmp, sems = refs[:n], refs[n:2 * n], refs[2 * n:3 * n], refs[3 * n]
        chip = _chip_index(_me())
        copies = []
        for a in range(n):
            tmp[a][...] = src[a][...].astype(BF16)
            cp = pltpu.make_async_copy(tmp[a], out[a].at[chip], sems.at[a])
            cp.start()
            copies.append(cp)
        for cp in copies:
            cp.wait()

    vm = pl.BlockSpec(memory_space=pltpu.VMEM)
    return _call(
        body, name="cast_weights", in_specs=[vm] * n, out_specs=[ANY] * n,
        out_shape=[jax.ShapeDtypeStruct((N_SHARD,) + w.shape, BF16) for w in ws],
        scratch_shapes=[pltpu.VMEM(w.shape, BF16) for w in ws] + [pltpu.SemaphoreType.DMA((n,))],
        compiler_params=_params(),
    )(*ws)


def _half_rows(ref_rows, c):
    half = ref_rows // 2
    return pl.ds(pl.multiple_of(c * half, 8), half)


class _Rider:
    def __init__(self, inputs, out_shape, aliases, n_sems, start, finish, middle=None):
        self.inputs, self.out_shape, self.aliases, self.n_sems = list(inputs), list(out_shape), dict(aliases), n_sems
        self.start, self.finish, self.middle = start, finish, middle


def _riders(*rs):
    ins = [0]
    outs = [0]
    sems = [0]
    for r in rs:
        ins.append(ins[-1] + len(r.inputs))
        outs.append(outs[-1] + len(r.out_shape))
        sems.append(sems[-1] + r.n_sems)

    def phase(which):
        def run(in_refs, out_refs, sem):
            for k, r in enumerate(rs):
                fn = getattr(r, which)
                if fn is not None:
                    fn(in_refs[ins[k]:ins[k + 1]], out_refs[outs[k]:outs[k + 1]], lambda j, base=sems[k]: sem(base + j))
        return run

    aliases = {ins[k] + i: outs[k] + o for k, r in enumerate(rs) for i, o in r.aliases.items()}
    return _Rider([a for r in rs for a in r.inputs], [o for r in rs for o in r.out_shape], aliases, sems[-1],
                  phase("start"), phase("finish"), phase("middle") if any(r.middle for r in rs) else None)


def _hosted(body, rider, *, name, grid, in_specs, out_specs, out_shape, args, scratch_shapes=(), middle_at=None):
    where = dict(grid=grid, compiler_params=_params(("arbitrary",) * len(grid))) if grid else dict(compiler_params=_params())
    if rider is None:
        res = _call(body, name=name, in_specs=in_specs, out_specs=out_specs, out_shape=out_shape,
                    scratch_shapes=list(scratch_shapes), **where)(*args)
        return res, ()
    n_in, n_out, n_scr = len(in_specs), len(out_specs), len(scratch_shapes)
    r_in, r_out = len(rider.inputs), len(rider.out_shape)

    def riding(*refs):
        at = 0
        parts = []
        for size in (n_in, r_in, n_out, r_out, n_scr):
            parts.append(refs[at:at + size])
            at += size
        ins, rider_ins, outs, rider_outs, scratch = parts
        sems = refs[at]

        def sem_at(k):
            return sems.at[k]

        if not grid:
            body(*ins, *outs, *scratch, launch=lambda: rider.start(rider_ins, rider_outs, sem_at))
            if rider.middle is not None:
                rider.middle(rider_ins, rider_outs, sem_at)
            rider.finish(rider_ins, rider_outs, sem_at)
            return
        step = pl.program_id(0)
        for axis in range(1, len(grid)):
            step = step * grid[axis] + pl.program_id(axis)
        steps = 1
        for size in grid:
            steps *= size

        @pl.when(step == 0)
        def _():
            rider.start(rider_ins, rider_outs, sem_at)

        body(*ins, *outs, *scratch)
        if rider.middle is not None:
            @pl.when(step == middle_at)
            def _():
                rider.middle(rider_ins, rider_outs, sem_at)

        @pl.when(step == steps - 1)
        def _():
            rider.finish(rider_ins, rider_outs, sem_at)

    res = _call(
        riding, name=name, in_specs=list(in_specs) + [ANY] * r_in, out_specs=list(out_specs) + [ANY] * r_out,
        out_shape=list(out_shape) + rider.out_shape,
        input_output_aliases={n_in + i: n_out + o for i, o in rider.aliases.items()},
        scratch_shapes=list(scratch_shapes) + [pltpu.SemaphoreType.DMA((rider.n_sems,))], **where,
    )(*args, *rider.inputs)
    return res[:n_out], res[n_out:]


def _alone(name, rider):
    n_in, n_out = len(rider.inputs), len(rider.out_shape)

    def body(*refs):
        ins, outs, sems = refs[:n_in], refs[n_in:n_in + n_out], refs[n_in + n_out]

        def sem_at(k):
            return sems.at[k]

        rider.start(ins, outs, sem_at)
        if rider.middle is not None:
            rider.middle(ins, outs, sem_at)
        rider.finish(ins, outs, sem_at)

    return _call(
        body, name=name, in_specs=[ANY] * n_in, out_specs=[ANY] * n_out, out_shape=rider.out_shape,
        input_output_aliases=rider.aliases, scratch_shapes=[pltpu.SemaphoreType.DMA((rider.n_sems,))],
    )(*rider.inputs)


OTHER_CHIPS = (2, 4, 6)


def _gather_rider(stacked):
    n = len(stacked)

    def flights(bufs, sem):
        me = _me()
        chip, sib = _chip_index(me), _peer(1)
        out = []
        for a in range(n):
            mine, theirs = (_half_rows(bufs[a].shape[1], c) for c in (me[2], 1 - me[2]))
            for j, d in enumerate(OTHER_CHIPS):
                k = 3 * a + j
                from_chip = _chip_index(_peer(d))
                own, landed, passed = bufs[a].at[chip, mine], bufs[a].at[from_chip, mine], bufs[a].at[from_chip, theirs]
                out.append((_remote(own, own, sem(k), sem(3 * n + k), _peer(d)),
                            _remote(own, landed, sem(k), sem(3 * n + k), _peer(d)),
                            _remote(landed, landed, sem(6 * n + k), sem(9 * n + k), sib),
                            _remote(passed, passed, sem(6 * n + k), sem(9 * n + k), sib)))
        return out

    def start(ins, outs, sem):
        for send, _, _, _ in flights(outs, sem):
            send.start()

    def middle(ins, outs, sem):
        for _, arrival, pass_on, _ in flights(outs, sem):
            arrival.wait_recv()
            pass_on.start()

    def finish(ins, outs, sem):
        every = flights(outs, sem)
        for _, _, _, passed_to_me in every:
            passed_to_me.wait_recv()
        for send, _, pass_on, _ in every:
            send.wait_send()
            pass_on.wait_send()

    return _Rider(stacked, [jax.ShapeDtypeStruct(w.shape, w.dtype) for w in stacked], {a: a for a in range(n)}, 12 * n,
                  start, finish, middle)


def _swap_rider(grads):
    n = len(grads)

    def copies(ins, outs, sem):
        c = _me()[2]
        return [_remote(ins[a].at[pl.ds(0, N_SHARD), _half_rows(ins[a].shape[1], 1 - c)], outs[a], sem(a), sem(n + a), _peer(1))
                for a in range(n)]

    def start(ins, outs, sem):
        for cp in copies(ins, outs, sem):
            cp.start()

    def finish(ins, outs, sem):
        for cp in copies(ins, outs, sem):
            cp.wait()

    return _Rider(grads, [jax.ShapeDtypeStruct((N_SHARD, g.shape[1] // 2, g.shape[2]), F32) for g in grads], {}, 2 * n,
                  start, finish)


def _row_tile(rows):
    return min(rows, 256)


def _add_halves(name, g, got, cj):
    _, half, cols = got.shape
    tr = _row_tile(half)
    nt = half // tr

    def body(cj_ref, g_ref, got_ref, o_ref):
        o_ref[...] = (g_ref[...] + got_ref[...]).astype(BF16)

    spec = pl.BlockSpec((None, tr, cols), lambda i, s, cj: (s, i, 0))
    return _call(
        body, name=name,
        grid_spec=pltpu.PrefetchScalarGridSpec(
            num_scalar_prefetch=1, grid=(nt, N_SHARD),
            in_specs=[pl.BlockSpec((None, tr, cols), lambda i, s, cj: (s, cj[0] * nt + i, 0)), spec], out_specs=spec),
        out_shape=jax.ShapeDtypeStruct(got.shape, BF16), compiler_params=_params(("parallel", "parallel")),
    )(cj, g, got)


def _scatter_rider(sums):
    n = len(sums)

    def flights(ins, outs, sem):
        chip = _chip_index(_me())
        out = []
        for a in range(n):
            for j, d in enumerate(OTHER_CHIPS):
                k = 3 * a + j
                other = _chip_index(_peer(d))
                out.append((_remote(ins[a].at[other], outs[a].at[chip], sem(k), sem(3 * n + k), _peer(d)),
                            _remote(ins[a].at[chip], outs[a].at[other], sem(k), sem(3 * n + k), _peer(d))))
        return out

    def start(ins, outs, sem):
        for send, _ in flights(ins, outs, sem):
            send.start()

    def finish(ins, outs, sem):
        every = flights(ins, outs, sem)
        for _, arrival in every:
            arrival.wait_recv()
        for send, _ in every:
            send.wait_send()

    return _Rider(sums, [jax.ShapeDtypeStruct(v.shape, v.dtype) for v in sums], {}, 6 * n, start, finish)


def _sum_chips(name, g, got, landed, cj):
    _, half, cols = got.shape
    tr = _row_tile(half)
    nt = half // tr

    def body(cj_ref, g_ref, got_ref, landed_ref, o_ref):
        own = g_ref[...] + got_ref[...]
        total = None
        for s in range(N_SHARD):
            term = jnp.where(cj_ref[1] == s, own, landed_ref[s].astype(F32))
            total = term if total is None else total + term
        o_ref[...] = total

    return _call(
        body, name=name,
        grid_spec=pltpu.PrefetchScalarGridSpec(
            num_scalar_prefetch=1, grid=(nt,),
            in_specs=[pl.BlockSpec((None, tr, cols), lambda i, cj: (cj[1], cj[0] * nt + i, 0)),
                      pl.BlockSpec((None, tr, cols), lambda i, cj: (cj[1], i, 0)),
                      pl.BlockSpec((N_SHARD, tr, cols), lambda i, cj: (0, i, 0))],
            out_specs=pl.BlockSpec((tr, cols), lambda i, cj: (cj[0] * nt + i, 0))),
        out_shape=jax.ShapeDtypeStruct((2 * half, cols), F32), compiler_params=_params(("parallel",)),
    )(cj, g, got, landed)


def _join_rider(shards):
    n = len(shards)

    def flights(bufs, sem):
        c = _me()[2]
        out = []
        for a in range(n):
            mine, theirs = (bufs[a].at[_half_rows(bufs[a].shape[0], cc)] for cc in (c, 1 - c))
            out.append((_remote(mine, mine, sem(a), sem(n + a), _peer(1)), _remote(theirs, theirs, sem(a), sem(n + a), _peer(1))))
        return out

    def start(ins, outs, sem):
        for send, _ in flights(outs, sem):
            send.start()

    def finish(ins, outs, sem):
        for send, arrival in flights(outs, sem):
            arrival.wait_recv()
            send.wait_send()

    return _Rider(shards, [jax.ShapeDtypeStruct(h.shape, F32) for h in shards], {a: a for a in range(n)}, 2 * n, start, finish)


class _Reduction:
    def __init__(self, names, grads, cj):
        self.names, self.grads, self.cj = names, list(grads), cj

    def swap(self):
        return _swap_rider(self.grads)

    def add(self, got):
        self.got = list(got)
        self.sums = [_add_halves("add_halves_" + nm, g, h, self.cj) for nm, g, h in zip(self.names, self.grads, self.got)]
        return _scatter_rider(self.sums)

    def total(self, landed):
        halves = [_sum_chips("sum_chips_" + nm, g, h, l, self.cj)
                  for nm, g, h, l in zip(self.names, self.grads, self.got, landed)]
        return _join_rider(halves)


def _adamw_math(w, g, m, v):
    m = ADAM_B1 * m + (1.0 - ADAM_B1) * g
    v = ADAM_B2 * v + (1.0 - ADAM_B2) * jnp.square(g)
    m_hat = m / (1.0 - ADAM_B1 ** ADAM_STEP)
    v_hat = v / (1.0 - ADAM_B2 ** ADAM_STEP)
    return -ADAM_LR * (m_hat / (jnp.sqrt(v_hat) + ADAM_EPS) + ADAM_WD * w), m, v


def _adamw(name, ws, gs, ms, vs, rider=None):
    n = len(ws)
    rows = ws[0].shape[0]
    tr = _row_tile(rows)

    def body(*refs):
        ins, outs = refs[:4 * n], refs[4 * n:]
        for a in range(n):
            w, g, m, v = (ins[k * n + a][...] for k in range(4))
            outs[a][...], outs[n + a][...], outs[2 * n + a][...] = _adamw_math(w, g, m, v)

    specs = [pl.BlockSpec((tr, w.shape[1]), lambda i: (i, 0)) for w in ws]
    res, riding = _hosted(
        body, rider, name=name, grid=(rows // tr,), in_specs=specs * 4, out_specs=specs * 3,
        out_shape=[jax.ShapeDtypeStruct(w.shape, F32) for w in ws] * 3, args=(*ws, *gs, *ms, *vs))
    return (res[:n], res[n:2 * n], res[2 * n:]), riding


def _adamw_small(ws, gs, ms, vs):
    n = len(ws)

    def body(*refs):
        ins, outs = refs[:4 * n], refs[4 * n:]
        for a in range(n):
            w, g, m, v = (ins[k * n + a][...] for k in range(4))
            outs[a][...], outs[n + a][...], outs[2 * n + a][...] = _adamw_math(w, g, m, v)

    vm = pl.BlockSpec(memory_space=pltpu.VMEM)
    res = _call(
        body, name="adamw_small", in_specs=[vm] * (4 * n), out_specs=[vm] * (3 * n),
        out_shape=[jax.ShapeDtypeStruct(w.shape, F32) for w in ws] * 3, compiler_params=_params(),
    )(*ws, *gs, *ms, *vs)
    return res[:n], res[n:2 * n], res[2 * n:]


def kernel(x, c, w_ada, b_ada, norm1_g, w_in, q_norm_a, k_norm_a, sink_b, w_branch, w_out, norm2_g, w_mlp_in, w_mlp_out, final_g, loss_target, m_w_ada, m_b_ada, m_norm1_g, m_w_in, m_q_norm_a, m_k_norm_a, m_sink_b, m_w_branch, m_w_out, m_norm2_g, m_w_mlp_in, m_w_mlp_out, m_final_g, v_w_ada, v_b_ada, v_norm1_g, v_w_in, v_q_norm_a, v_k_norm_a, v_sink_b, v_w_branch, v_w_out, v_norm2_g, v_w_mlp_in, v_w_mlp_out, v_final_g):
    xi, yi, ci = _me()
    cj = jnp.stack([ci, 2 * xi + yi]).astype(jnp.int32)
    n_cols = 6 * D_MODEL // N_SHARD

    def rows2d(a):
        return a.reshape(-1, a.shape[-1])

    big = (w_in, w_branch, w_out, w_mlp_in, w_mlp_out)
    stacked = _cast_weights([rows2d(w) for w in big])
    inv_a = ROPE_THETA ** (-jnp.arange(0, HEAD_DIM // 2, 2, dtype=F32) / (HEAD_DIM // 2))
    inv_b = ROPE_THETA ** (-jnp.arange(0, HEAD_DIM, 2, dtype=F32) / HEAD_DIM)
    inv = jnp.stack([jnp.tile(inv_a, LANES // inv_a.shape[0]), jnp.tile(inv_b, LANES // inv_b.shape[0])])
    (mod6, sc_all, *tabs), (w_in_s,) = _ada_fwd(c, w_ada[0], b_ada.reshape(N_SHARD, n_cols), inv, x.shape[1],
                                                rider=_gather_rider(stacked[:1]))
    rest = stacked[1:]

    gq2 = jnp.tile(q_norm_a, (1, 2))
    gk2 = jnp.tile(k_norm_a, (1, 2))
    grad_x, (w_in_red, join_out, g_mi, g_mo), accs = _local_step(
        x[0], loss_target[0], mod6, norm1_g, norm2_g, final_g.reshape(1, D_MODEL), gq2, gk2, sink_b[0], w_in_s, rest, cj, tabs)

    (g_w_ada, g_b_ada, g_n1, g_n2, g_f, g_q, g_k, g_s, loss_row), got_in = _ada_bwd(*accs, sc_all, rider=w_in_red.swap())
    loss = loss_row[0, 0]
    moments = dict(w_ada=(m_w_ada, v_w_ada), w_in=(m_w_in, v_w_in), w_branch=(m_w_branch, v_w_branch), w_out=(m_w_out, v_w_out),
                   w_mlp_in=(m_w_mlp_in, v_w_mlp_in), w_mlp_out=(m_w_mlp_out, v_w_mlp_out))
    weights = dict(w_ada=w_ada, w_in=w_in, w_branch=w_branch, w_out=w_out, w_mlp_in=w_mlp_in, w_mlp_out=w_mlp_out)

    def adamw(call, names, grads, rider=None):
        (d, m, v), riding = _adamw(call, [rows2d(weights[nm]) for nm in names], grads,
                                   [rows2d(moments[nm][0]) for nm in names], [rows2d(moments[nm][1]) for nm in names], rider)
        return {nm: (grads[k], d[k], m[k], v[k]) for k, nm in enumerate(names)}, riding

    big_res, landed_in = adamw("adamw_ada_mlp", ("w_ada", "w_mlp_in", "w_mlp_out"), [g_w_ada, g_mi, g_mo], w_in_red.add(got_in))
    g_in, g_out, g_br = _alone("join_in_out_branch", _riders(w_in_red.total(landed_in), join_out))
    big_res.update(adamw("adamw_in_branch", ("w_in", "w_branch"), [g_in, g_br])[0])

    small = ("b_ada", "norm1_g", "q_norm_a", "k_norm_a", "sink_b", "norm2_g", "final_g", "w_out")
    row = lambda a: a.reshape(1, -1)
    small_w = [row(a) for a in (b_ada, norm1_g, q_norm_a, k_norm_a, sink_b, norm2_g, final_g)] + [w_out[0]]
    small_g = [g_b_ada, g_n1, g_q, g_k, g_s, g_n2, g_f, g_out]
    small_m = [row(a) for a in (m_b_ada, m_norm1_g, m_q_norm_a, m_k_norm_a, m_sink_b, m_norm2_g, m_final_g)] + [m_w_out[0]]
    small_v = [row(a) for a in (v_b_ada, v_norm1_g, v_q_norm_a, v_k_norm_a, v_sink_b, v_norm2_g, v_final_g)] + [v_w_out[0]]
    s_d, s_m, s_v = _adamw_small(small_w, small_g, small_m, small_v)

    order = ("w_ada", "b_ada", "norm1_g", "w_in", "q_norm_a", "k_norm_a", "sink_b", "w_branch", "w_out", "norm2_g",
             "w_mlp_in", "w_mlp_out", "final_g")
    like = dict(w_ada=w_ada, b_ada=b_ada, norm1_g=norm1_g, w_in=w_in, q_norm_a=q_norm_a, k_norm_a=k_norm_a, sink_b=sink_b,
                w_branch=w_branch, w_out=w_out, norm2_g=norm2_g, w_mlp_in=w_mlp_in, w_mlp_out=w_mlp_out, final_g=final_g)
    grad, delta, new_m, new_v = {}, {}, {}, {}
    for nm, res in big_res.items():
        grad[nm], delta[nm], new_m[nm], new_v[nm] = res
    for k, nm in enumerate(small):
        grad[nm], delta[nm], new_m[nm], new_v[nm] = small_g[k], s_d[k], s_m[k], s_v[k]
    outs = [loss, grad_x[None]]
    for group in (grad, delta, new_m, new_v):
        outs += [group[nm].reshape(like[nm].shape) for nm in order]
    return tuple(outs)
```

```python
import jax
import jax.numpy as jnp
from jax import lax
from jax.experimental import pallas as pl
from jax.experimental.pallas import tpu as pltpu

F32 = jnp.float32
BF16 = jnp.bfloat16
MESH = pl.DeviceIdType.MESH
ANY = pl.BlockSpec(memory_space=pl.ANY)

D_MODEL = 1024
HEAD_DIM = 64
Q_HEADS = 8
KV_HEADS = 2
GROUP = Q_HEADS // KV_HEADS
BRANCH_W = Q_HEADS * HEAD_DIM
KV_W = KV_HEADS * HEAD_DIM
IN_W = 2 * (BRANCH_W + 2 * KV_W) + 2 * D_MODEL
QK_W = 2 * (BRANCH_W + 2 * KV_W)
D_FF = 4 * D_MODEL
GRID_W = 64
WINDOW = 128
ROPE_THETA = 10000.0
NORM_EPS = 1e-6
NEG_INF = -1e30
Q_SCALE = HEAD_DIM ** -0.5
N_SHARD = 4
N_DEV = 8
LANES = 128
VMEM_LIMIT = 56 * 1024 * 1024

ADAM_LR = 0.001
ADAM_B1 = 0.9
ADAM_B2 = 0.999
ADAM_EPS = 1e-08
ADAM_WD = 0.01
ADAM_STEP = 10

_call = pl.pallas_call


def _params(sem=None, vmem=VMEM_LIMIT):
    return pltpu.CompilerParams(dimension_semantics=sem, vmem_limit_bytes=vmem)


def _nt(a, b):
    return lax.dot_general(a, b, (((1,), (1,)), ((), ())), preferred_element_type=F32)


def _tn(a, b):
    return lax.dot_general(a, b, (((0,), (0,)), ((), ())), preferred_element_type=F32)


def _nn(a, b):
    return jnp.dot(a, b, preferred_element_type=F32)


def _sigmoid(z):
    return 0.5 * jnp.tanh(0.5 * z) + 0.5


def _rope_tables(s):
    t = jnp.arange(s, dtype=jnp.int32)
    lane = jnp.arange(LANES, dtype=jnp.int32)

    def cos_sin(pos, dim):
        inv = ROPE_THETA ** (-jnp.arange(0, dim, 2, dtype=F32) / dim)
        ang = pos.astype(F32)[:, None] * inv[None, :]
        return jnp.cos(ang), jnp.sin(ang)

    cr, sr = cos_sin(t // GRID_W, HEAD_DIM // 2)
    cc, sc = cos_sin(t % GRID_W, HEAD_DIM // 2)
    cos_a = jnp.tile(jnp.concatenate([cr, cr, cc, cc], axis=1), (1, 2))
    sin_a = jnp.tile(jnp.concatenate([sr, sr, sc, sc], axis=1), (1, 2))
    first_a = (lane % 32) < 16
    c1, s1 = cos_sin(t, HEAD_DIM)
    cos_b = jnp.tile(jnp.concatenate([c1, c1], axis=1), (1, 2))
    sin_b = jnp.tile(jnp.concatenate([s1, s1], axis=1), (1, 2))
    first_b = (lane % 64) < 32
    tabs_a = (cos_a, jnp.where(first_a, -sin_a, 0.0), jnp.where(first_a, 0.0, sin_a))
    tabs_b = (cos_b, jnp.where(first_b, -sin_b, 0.0), jnp.where(first_b, 0.0, sin_b))
    return tabs_a + tabs_b


def _rope(z, cos, s_lo, s_hi, half, sign=1.0):
    up = pltpu.roll(z, LANES - half, 1)
    dn = pltpu.roll(z, half, 1)
    return z * cos + sign * (up * s_lo + dn * s_hi)


def _head_mean(z2, bd):
    hi = z2.astype(BF16)
    lo = (z2 - hi.astype(F32)).astype(BF16)
    return _nn(hi, bd) + _nn(lo, bd)


def _block_diag():
    lane = jnp.arange(LANES)
    return jnp.where((lane[:, None] // HEAD_DIM) == (lane[None, :] // HEAD_DIM), 1.0 / HEAD_DIM, 0.0).astype(BF16)


def _row_spec(tm, width):
    return pl.BlockSpec((tm, width), lambda i: (i, 0))


def _heads_spec(heads, tm):
    return pl.BlockSpec((heads, tm, HEAD_DIM), lambda i: (0, i, 0))


def _full_spec(shape):
    nd = len(shape)
    return pl.BlockSpec(shape, lambda i: (0,) * nd)


def _in_proj(x, mod6, g1, w_in_s, gq, gk, bd, tabs, tm=512):
    s = x.shape[0]

    def body(x_ref, mod_ref, g1_ref, w_ref, gq_ref, gk_ref, bd_ref, ca, la, ha, cb, lb, hb,
             h_ref, qkraw_ref, qa_ref, ka_ref, va_ref, qb_ref, kb_ref, vb_ref, gate_ref):
        xt = x_ref[...]
        r = lax.rsqrt(jnp.mean(xt * xt, axis=-1, keepdims=True) + NORM_EPS)
        h = (xt * r * g1_ref[...]) * (1.0 + mod_ref[1:2, :]) + mod_ref[0:1, :]
        hb16 = h.astype(BF16)
        h_ref[...] = hb16
        proj = jnp.concatenate([_nn(hb16, w_ref[j]) for j in range(N_SHARD)], axis=1)
        qkraw_ref[...] = proj[:, :BRANCH_W + KV_W]
        bdm = bd_ref[...]
        tab_a = (ca[...], la[...], ha[...])
        tab_b = (cb[...], lb[...], hb[...])

        def norm_rope_a(z, gain):
            zn = z * lax.rsqrt(_head_mean(z * z, bdm) + NORM_EPS) * gain
            return _rope(zn, *tab_a, 16)

        def put(ref, first, z):
            zb = z.astype(BF16)
            ref[first] = zb[:, :HEAD_DIM]
            ref[first + 1] = zb[:, HEAD_DIM:]

        for i in range(Q_HEADS // 2):
            put(qa_ref, 2 * i, norm_rope_a(proj[:, LANES * i:LANES * (i + 1)], gq_ref[...]) * Q_SCALE)
        off = BRANCH_W
        put(ka_ref, 0, norm_rope_a(proj[:, off:off + LANES], gk_ref[...]))
        off += KV_W
        def put_v(ref, z):
            zb = z.astype(BF16)
            for hd in range(KV_HEADS):
                ref[hd, :, :HEAD_DIM] = zb[:, HEAD_DIM * hd:HEAD_DIM * (hd + 1)]
                ref[hd, :, HEAD_DIM:] = jnp.ones((tm, HEAD_DIM), BF16)

        put_v(va_ref, proj[:, off:off + LANES])
        off += KV_W
        for i in range(Q_HEADS // 2):
            put(qb_ref, 2 * i, _rope(proj[:, off + LANES * i:off + LANES * (i + 1)], *tab_b, 32) * Q_SCALE)
        off += BRANCH_W
        put(kb_ref, 0, _rope(proj[:, off:off + LANES], *tab_b, 32))
        off += KV_W
        put_v(vb_ref, proj[:, off:off + LANES])
        gate_ref[...] = proj[:, QK_W:].astype(BF16)

    tab_spec = _row_spec(tm, LANES)
    return _call(
        body, name="in_proj", grid=(s // tm,),
        in_specs=[_row_spec(tm, D_MODEL), _full_spec(mod6.shape), _full_spec(g1.shape), _full_spec(w_in_s.shape),
                  _full_spec(gq.shape), _full_spec(gk.shape), _full_spec(bd.shape)] + [tab_spec] * 6,
        out_specs=[_row_spec(tm, D_MODEL), _row_spec(tm, BRANCH_W + KV_W), _heads_spec(Q_HEADS, tm), _heads_spec(KV_HEADS, tm),
                   pl.BlockSpec((KV_HEADS, tm, LANES), lambda i: (0, i, 0)), _heads_spec(Q_HEADS, tm),
                   _heads_spec(KV_HEADS, tm), pl.BlockSpec((KV_HEADS, tm, LANES), lambda i: (0, i, 0)),
                   _row_spec(tm, 2 * D_MODEL)],
        out_shape=[jax.ShapeDtypeStruct((s, D_MODEL), BF16), jax.ShapeDtypeStruct((s, BRANCH_W + KV_W), F32),
                   jax.ShapeDtypeStruct((Q_HEADS, s, HEAD_DIM), BF16), jax.ShapeDtypeStruct((KV_HEADS, s, HEAD_DIM), BF16),
                   jax.ShapeDtypeStruct((KV_HEADS, s, LANES), BF16), jax.ShapeDtypeStruct((Q_HEADS, s, HEAD_DIM), BF16),
                   jax.ShapeDtypeStruct((KV_HEADS, s, HEAD_DIM), BF16), jax.ShapeDtypeStruct((KV_HEADS, s, LANES), BF16),
                   jax.ShapeDtypeStruct((s, 2 * D_MODEL), BF16)],
        compiler_params=_params(("parallel",)),
    )(x, mod6, g1, w_in_s, gq, gk, bd, *tabs)


def _group_specs(s, tq):
    q_spec = pl.BlockSpec((None, GROUP, tq, HEAD_DIM), lambda g, i: (g, 0, i, 0))
    kv_spec = pl.BlockSpec((None, s, HEAD_DIM), lambda g, i: (g, 0, 0))
    col_spec = pl.BlockSpec((None, GROUP, tq, 1), lambda g, i: (g, 0, i, 0))
    return q_spec, kv_spec, col_spec


def _attn_a_fwd(q, k, v1, rider=None, tq=256, tk=2048):
    s = q.shape[1]
    tk = min(tk, s // 2)
    rows = GROUP * tq

    n = s // tk
    assert n >= 2 and n % 2 == 0

    def body(q_ref, k_ref, v_ref, o_ref, oh_ref, lse_ref, s0_ref, s1_ref, p0_ref, p1_ref, m_ref, a_ref, acc_ref):
        s_ref, p_ref = (s0_ref, s1_ref), (p0_ref, p1_ref)
        qq = q_ref[...].reshape(rows, HEAD_DIM)
        m_ref[...] = jnp.full((rows, 1), NEG_INF, F32)
        acc_ref[...] = jnp.zeros((rows, LANES), F32)

        def keys(i):
            return pl.ds(pl.multiple_of(i * tk, tk), tk)

        def scores(i, slot):
            s_ref[slot][...] = _nt(qq, k_ref[keys(i), :])

        def softmax(slot):
            sc = s_ref[slot][...]
            m = m_ref[...]
            mn = jnp.maximum(m, jnp.max(sc, axis=-1, keepdims=True))
            m_ref[...] = mn
            a_ref[...] = jnp.exp(m - mn)
            p_ref[slot][...] = jnp.exp(sc - mn).astype(BF16)

        def weigh(i, slot):
            acc_ref[...] = a_ref[...] * acc_ref[...] + _nn(p_ref[slot][...], v_ref[keys(i), :])

        scores(0, 0)
        softmax(0)
        scores(1, 1)

        def two_steps(j, carry):
            i = 2 * j + 1
            weigh(i - 1, 0)
            softmax(1)
            scores(i + 1, 0)
            weigh(i, 1)
            softmax(0)
            scores(i + 2, 1)
            return carry

        lax.fori_loop(0, (n - 2) // 2, two_steps, 0, unroll=True)
        weigh(n - 2, 0)
        softmax(1)
        weigh(n - 1, 1)
        l = acc_ref[:, HEAD_DIM:HEAD_DIM + 1]
        o = (acc_ref[:, :HEAD_DIM] / l).astype(BF16)
        for g in range(GROUP):
            o_ref[:, HEAD_DIM * g:HEAD_DIM * (g + 1)] = o[tq * g:tq * (g + 1)]
        oh_ref[...] = o.reshape(GROUP, tq, HEAD_DIM)
        lse_ref[...] = (m_ref[...] + jnp.log(l)).reshape(GROUP, tq, 1)

    q_spec, kv_spec, col_spec = _group_specs(s, tq)
    v_spec = pl.BlockSpec((None, s, LANES), lambda g, i: (g, 0, 0))
    return _hosted(
        body, rider, name="attn_a_fwd", grid=(KV_HEADS, s // tq),
        in_specs=[q_spec, kv_spec, v_spec],
        out_specs=[pl.BlockSpec((tq, GROUP * HEAD_DIM), lambda g, i: (i, g)), q_spec, col_spec],
        out_shape=[jax.ShapeDtypeStruct((s, BRANCH_W), BF16), jax.ShapeDtypeStruct((KV_HEADS, GROUP, s, HEAD_DIM), BF16),
                   jax.ShapeDtypeStruct((KV_HEADS, GROUP, s, 1), F32)],
        scratch_shapes=[pltpu.VMEM((rows, tk), F32), pltpu.VMEM((rows, tk), F32), pltpu.VMEM((rows, tk), BF16),
                        pltpu.VMEM((rows, tk), BF16), pltpu.VMEM((rows, 1), F32), pltpu.VMEM((rows, 1), F32),
                        pltpu.VMEM((rows, LANES), F32)],
        args=(q.reshape(KV_HEADS, GROUP, s, HEAD_DIM), k, v1), middle_at=KV_HEADS * (s // tq) // 2)


def _attn_a_bwd(q, k, v1, o, do, lse, rider=None, tq=256, tk=512):
    s = q.shape[1]
    tk = min(tk, s // 2)
    rows = GROUP * tq

    n = s // tk
    assert n >= 2 and n % 2 == 0

    def body(q_ref, k_ref, v_ref, o_ref, do_ref, lse_ref, dq_ref, dk_ref, dv_ref,
             s0_ref, s1_ref, dp0_ref, dp1_ref, p0_ref, p1_ref, ds0_ref, ds1_ref, dq_acc):
        s_ref, dp_ref, p_ref, ds_ref = (s0_ref, s1_ref), (dp0_ref, dp1_ref), (p0_ref, p1_ref), (ds0_ref, ds1_ref)

        @pl.when(pl.program_id(1) == 0)
        def _():
            dk_ref[...] = jnp.zeros_like(dk_ref)
            dv_ref[...] = jnp.zeros_like(dv_ref)

        qq = q_ref[...].reshape(rows, HEAD_DIM)
        dd = do_ref[...].reshape(rows, HEAD_DIM)
        ls = lse_ref[...].reshape(rows, 1)
        dl = jnp.sum(dd.astype(F32) * o_ref[...].reshape(rows, HEAD_DIM).astype(F32), axis=-1, keepdims=True)
        dq_acc[...] = jnp.zeros((rows, HEAD_DIM), F32)

        def keys(i):
            return pl.ds(pl.multiple_of(i * tk, tk), tk)

        def scores(i, slot):
            s_ref[slot][...] = _nt(qq, k_ref[keys(i), :])
            dp_ref[slot][...] = _nt(dd, v_ref[keys(i), :HEAD_DIM])

        def weights(slot):
            p = jnp.exp(s_ref[slot][...] - ls)
            p_ref[slot][...] = p.astype(BF16)
            ds_ref[slot][...] = (p * (dp_ref[slot][...] - dl)).astype(BF16)

        def grads(i, slot):
            dv_ref[keys(i), :] += _tn(p_ref[slot][...], dd)
            dk_ref[keys(i), :] += _tn(ds_ref[slot][...], qq)
            dq_acc[...] += _nn(ds_ref[slot][...], k_ref[keys(i), :])

        scores(0, 0)
        weights(0)
        scores(1, 1)

        def two_steps(j, carry):
            i = 2 * j + 1
            grads(i - 1, 0)
            weights(1)
            scores(i + 1, 0)
            grads(i, 1)
            weights(0)
            scores(i + 2, 1)
            return carry

        lax.fori_loop(0, (n - 2) // 2, two_steps, 0, unroll=True)
        grads(n - 2, 0)
        weights(1)
        grads(n - 1, 1)
        dq_ref[...] = dq_acc[...].astype(BF16).reshape(GROUP, tq, HEAD_DIM)

    q_spec, kv_spec, col_spec = _group_specs(s, tq)
    v_spec = pl.BlockSpec((None, s, LANES), lambda g, i: (g, 0, 0))
    shape4 = (KV_HEADS, GROUP, s, HEAD_DIM)
    tile32, tile16 = pltpu.VMEM((rows, tk), F32), pltpu.VMEM((rows, tk), BF16)
    return _hosted(
        body, rider, name="attn_a_bwd", grid=(KV_HEADS, s // tq),
        in_specs=[q_spec, kv_spec, v_spec, q_spec, q_spec, col_spec],
        out_specs=[q_spec, kv_spec, kv_spec],
        out_shape=[jax.ShapeDtypeStruct(shape4, BF16), jax.ShapeDtypeStruct((KV_HEADS, s, HEAD_DIM), F32),
                   jax.ShapeDtypeStruct((KV_HEADS, s, HEAD_DIM), F32)],
        scratch_shapes=[tile32] * 4 + [tile16] * 4 + [pltpu.VMEM((rows, HEAD_DIM), F32)],
        args=(q.reshape(shape4), k, v1, o.reshape(shape4), do.reshape(shape4), lse))


TQ_B = WINDOW


def _win_keys(tq):
    return tq + 2 * WINDOW


def _window_bias(tq):
    r = jnp.arange(tq, dtype=jnp.int32)[:, None]
    col = jnp.arange(_win_keys(tq), dtype=jnp.int32)[None, :]
    return jnp.stack([jnp.where(jnp.abs(r - col + WINDOW * b) <= WINDOW, 0.0, NEG_INF) for b in range(3)]).astype(F32)


def _band(tq, s):
    win = _win_keys(tq)

    def window(e):
        return pl.ds(pl.multiple_of(jnp.clip(e * tq - WINDOW, 0, s - win), WINDOW), win)

    def bias_index(e):
        return jnp.where(e == 0, 0, jnp.where(e >= s // tq - 1, 2, 1))

    return window, bias_index


def _pair_specs(s, tq):
    pairs = s // (2 * tq)
    cur = lambda g, j: (g, 0, jnp.minimum(j, pairs - 1), 0)
    prev = lambda g, j: (g, 0, jnp.maximum(j - 1, 0), 0)
    tile = lambda width, index: pl.BlockSpec((None, GROUP, 2 * tq, width), index)
    kv_spec = pl.BlockSpec((None, s, HEAD_DIM), lambda g, j: (g, 0, 0))
    v_spec = pl.BlockSpec((None, s, LANES), lambda g, j: (g, 0, 0))
    sink_spec = pl.BlockSpec((None, GROUP * tq, 1), lambda g, j: (g, 0, 0))
    bias_spec = pl.BlockSpec((3, tq, _win_keys(tq)), lambda g, j: (0, 0, 0))
    return tile, cur, prev, kv_spec, v_spec, sink_spec, bias_spec


def _attn_b_fwd(q, k, v1, sink_col, bias, rider=None, tq=TQ_B):
    s = q.shape[1]
    rows = GROUP * tq
    win = _win_keys(tq)
    pairs = s // (2 * tq)
    window, bias_index = _band(tq, s)

    def body(q_ref, k_ref, v_ref, sink_ref, bias_ref, o_ref, oh_ref, lse_ref, s0_ref, s1_ref, p0_ref, p1_ref, m0_ref, m1_ref):
        s_ref, p_ref, m_ref = (s0_ref, s1_ref), (p0_ref, p1_ref), (m0_ref, m1_ref)
        j = pl.program_id(1)

        @pl.when(j == 0)
        def _():
            for ref in (s0_ref, s1_ref, p0_ref, p1_ref, m0_ref, m1_ref):
                ref[...] = jnp.zeros_like(ref)

        def scores(e, slot):
            qq = q_ref[:, pl.ds(slot * tq, tq), :].reshape(rows, HEAD_DIM)
            sc = _nt(qq, k_ref[window(e), :]).reshape(GROUP, tq, win) + bias_ref[bias_index(e)][None]
            s_ref[slot][...] = sc.reshape(rows, win)

        def softmax(slot):
            sc = s_ref[slot][...]
            m = jnp.maximum(jnp.max(sc, axis=-1, keepdims=True), sink_ref[...])
            m_ref[slot][...] = m
            p_ref[slot][...] = jnp.exp(sc - m).astype(BF16)

        def finish(e, slot):
            acc = _nn(p_ref[slot][...], v_ref[window(e), :])
            m = m_ref[slot][...]
            l = acc[:, HEAD_DIM:HEAD_DIM + 1] + jnp.exp(sink_ref[...] - m)
            o = (acc[:, :HEAD_DIM] / l).astype(BF16)
            at = pl.ds(slot * tq, tq)
            for g in range(GROUP):
                o_ref[at, HEAD_DIM * g:HEAD_DIM * (g + 1)] = o[tq * g:tq * (g + 1)]
            oh_ref[:, at, :] = o.reshape(GROUP, tq, HEAD_DIM)
            lse_ref[:, at, :] = (m + jnp.log(l)).reshape(GROUP, tq, 1)

        first = 2 * j
        finish(jnp.maximum(first - 2, 0), 0)
        softmax(1)
        scores(first, 0)
        finish(jnp.maximum(first - 1, 0), 1)
        softmax(0)
        scores(first + 1, 1)

    tile, cur, prev, kv_spec, v_spec, sink_spec, bias_spec = _pair_specs(s, tq)
    tile32, tile16, col = pltpu.VMEM((rows, win), F32), pltpu.VMEM((rows, win), BF16), pltpu.VMEM((rows, 1), F32)
    return _hosted(
        body, rider, name="attn_b_fwd", grid=(KV_HEADS, pairs + 1),
        in_specs=[tile(HEAD_DIM, cur), kv_spec, v_spec, sink_spec, bias_spec],
        out_specs=[pl.BlockSpec((2 * tq, GROUP * HEAD_DIM), lambda g, j: (jnp.maximum(j - 1, 0), g)),
                   tile(HEAD_DIM, prev), tile(1, prev)],
        out_shape=[jax.ShapeDtypeStruct((s, BRANCH_W), BF16), jax.ShapeDtypeStruct((KV_HEADS, GROUP, s, HEAD_DIM), BF16),
                   jax.ShapeDtypeStruct((KV_HEADS, GROUP, s, 1), F32)],
        scratch_shapes=[tile32, tile32, tile16, tile16, col, col],
        args=(q.reshape(KV_HEADS, GROUP, s, HEAD_DIM), k, v1, sink_col, bias))


def _attn_b_bwd(q, k, v1, o, do, lse, sink_col, bias, rider=None, tq=TQ_B):
    s = q.shape[1]
    rows = GROUP * tq
    win = _win_keys(tq)
    pairs = s // (2 * tq)
    window, bias_index = _band(tq, s)

    def body(q_ref, k_ref, v_ref, o_ref, do_ref, lse_ref, sink_ref, bias_ref, dq_ref, dk_ref, dv_ref, dsink_ref,
             s0, s1, dp0, dp1, p0, p1, ds0, ds1, q0, q1, d0, d1, ls0, ls1, dl0, dl1):
        s_ref, dp_ref, p_ref, ds_ref = (s0, s1), (dp0, dp1), (p0, p1), (ds0, ds1)
        q_keep, do_keep, lse_keep, delta_keep = (q0, q1), (d0, d1), (ls0, ls1), (dl0, dl1)
        j = pl.program_id(1)

        @pl.when(j == 0)
        def _():
            for ref in (dk_ref, dv_ref, dsink_ref, s0, s1, dp0, dp1, p0, p1, ds0, ds1, q0, q1, d0, d1, ls0, ls1, dl0, dl1):
                ref[...] = jnp.zeros_like(ref)

        def scores(e, slot):
            at = pl.ds(slot * tq, tq)
            qq = q_ref[:, at, :].reshape(rows, HEAD_DIM)
            dd = do_ref[:, at, :].reshape(rows, HEAD_DIM)
            q_keep[slot][...] = qq
            do_keep[slot][...] = dd
            lse_keep[slot][...] = lse_ref[:, at, :].reshape(rows, 1)
            delta_keep[slot][...] = jnp.sum(dd.astype(F32) * o_ref[:, at, :].reshape(rows, HEAD_DIM).astype(F32), axis=-1,
                                            keepdims=True)
            sc = _nt(qq, k_ref[window(e), :]).reshape(GROUP, tq, win) + bias_ref[bias_index(e)][None]
            s_ref[slot][...] = sc.reshape(rows, win)
            dp_ref[slot][...] = _nt(dd, v_ref[window(e), :HEAD_DIM])

        def weights(slot):
            p = jnp.exp(s_ref[slot][...] - lse_keep[slot][...])
            p_ref[slot][...] = p.astype(BF16)
            ds_ref[slot][...] = (p * (dp_ref[slot][...] - delta_keep[slot][...])).astype(BF16)

        def grads(e, slot, live):
            at = window(e)
            ds = ds_ref[slot][...]
            dv_ref[at, :] += _tn(p_ref[slot][...], do_keep[slot][...])
            dk_ref[at, :] += _tn(ds, q_keep[slot][...])
            dq_ref[:, pl.ds(slot * tq, tq), :] = _nn(ds, k_ref[at, :]).astype(BF16).reshape(GROUP, tq, HEAD_DIM)
            dsk = jnp.exp(sink_ref[...] - lse_keep[slot][...]) * delta_keep[slot][...] * live
            for g in range(GROUP):
                dsink_ref[g:g + 1, :] -= jnp.broadcast_to(jnp.sum(dsk[tq * g:tq * (g + 1)], axis=0, keepdims=True), (1, LANES))

        first = 2 * j
        live = jnp.where(j > 0, 1.0, 0.0)
        grads(jnp.maximum(first - 2, 0), 0, live)
        weights(1)
        scores(first, 0)
        grads(jnp.maximum(first - 1, 0), 1, live)
        weights(0)
        scores(first + 1, 1)

    tile, cur, prev, kv_spec, v_spec, sink_spec, bias_spec = _pair_specs(s, tq)
    dsink_spec = pl.BlockSpec((None, ACC_ROWS, LANES), lambda g, j: (g, 0, 0))
    shape4 = (KV_HEADS, GROUP, s, HEAD_DIM)
    tile32, tile16 = pltpu.VMEM((rows, win), F32), pltpu.VMEM((rows, win), BF16)
    keep, col = pltpu.VMEM((rows, HEAD_DIM), BF16), pltpu.VMEM((rows, 1), F32)
    return _hosted(
        body, rider, name="attn_b_bwd", grid=(KV_HEADS, pairs + 1),
        in_specs=[tile(HEAD_DIM, cur), kv_spec, v_spec, tile(HEAD_DIM, cur), tile(HEAD_DIM, cur), tile(1, cur), sink_spec,
                  bias_spec],
        out_specs=[tile(HEAD_DIM, prev), kv_spec, kv_spec, dsink_spec],
        out_shape=[jax.ShapeDtypeStruct(shape4, BF16), jax.ShapeDtypeStruct((KV_HEADS, s, HEAD_DIM), F32),
                   jax.ShapeDtypeStruct((KV_HEADS, s, HEAD_DIM), F32), jax.ShapeDtypeStruct((KV_HEADS, ACC_ROWS, LANES), F32)],
        scratch_shapes=[tile32] * 4 + [tile16] * 4 + [keep] * 4 + [col] * 4,
        args=(q.reshape(shape4), k, v1, o.reshape(shape4), do.reshape(shape4), lse, sink_col, bias))


def _post_attn(ya, yb, gates, x, mod6, wbr_s, w_out, tm=512):
    s = x.shape[0]

    def body(ya_ref, yb_ref, g_ref, x_ref, mod_ref, wbr_ref, wo_ref, ua_ref, ub_ref, mg_ref, o_ref, x1_ref):
        ya_t, yb_t = ya_ref[...], yb_ref[...]
        ua = jnp.concatenate([_nn(ya_t, wbr_ref[j, 0]) for j in range(N_SHARD)], axis=1)
        ub = jnp.concatenate([_nn(yb_t, wbr_ref[j, 1]) for j in range(N_SHARD)], axis=1)
        ga, gb = g_ref[:, :D_MODEL].astype(F32), g_ref[:, D_MODEL:].astype(F32)
        merged = (_sigmoid(ga) * ua + _sigmoid(gb) * ub).astype(BF16)
        o = _nn(merged, wo_ref[...])
        ua_ref[...] = ua.astype(BF16)
        ub_ref[...] = ub.astype(BF16)
        mg_ref[...] = merged
        o_ref[...] = o.astype(BF16)
        x1_ref[...] = x_ref[...] + mod_ref[2:3, :] * o

    bf = jax.ShapeDtypeStruct((s, D_MODEL), BF16)
    return _call(
        body, name="post_attn", grid=(s // tm,),
        in_specs=[_row_spec(tm, BRANCH_W), _row_spec(tm, BRANCH_W), _row_spec(tm, 2 * D_MODEL), _row_spec(tm, D_MODEL),
                  _full_spec(mod6.shape), _full_spec(wbr_s.shape), _full_spec(w_out.shape)],
        out_specs=[_row_spec(tm, D_MODEL)] * 5,
        out_shape=[bf, bf, bf, bf, jax.ShapeDtypeStruct((s, D_MODEL), F32)],
        compiler_params=_params(("parallel",)),
    )(ya, yb, gates, x, mod6, wbr_s, w_out)


def _mlp_in(x1, mod6, g2, w_mi_s, tm=512):
    s = x1.shape[0]

    def body(x_ref, mod_ref, g_ref, w_ref, h2_ref, a_ref, hid_ref):
        xt = x_ref[...]
        r = lax.rsqrt(jnp.mean(xt * xt, axis=-1, keepdims=True) + NORM_EPS)
        h2 = ((xt * r * g_ref[...]) * (1.0 + mod_ref[4:5, :]) + mod_ref[3:4, :]).astype(BF16)
        h2_ref[...] = h2
        a = jnp.concatenate([_nn(h2, w_ref[j]) for j in range(N_SHARD)], axis=1)
        a_ref[...] = a.astype(BF16)
        hid_ref[...] = jnp.square(jnp.maximum(a, 0.0)).astype(BF16)

    return _call(
        body, name="mlp_in", grid=(s // tm,),
        in_specs=[_row_spec(tm, D_MODEL), _full_spec(mod6.shape), _full_spec(g2.shape), _full_spec(w_mi_s.shape)],
        out_specs=[_row_spec(tm, D_MODEL), _row_spec(tm, D_FF), _row_spec(tm, D_FF)],
        out_shape=[jax.ShapeDtypeStruct((s, D_MODEL), BF16), jax.ShapeDtypeStruct((s, D_FF), BF16),
                   jax.ShapeDtypeStruct((s, D_FF), BF16)],
        compiler_params=_params(("parallel",)),
    )(x1, mod6, g2, w_mi_s)


ACC_ROWS = 8


def _acc_spec():
    return pl.BlockSpec((ACC_ROWS, D_MODEL), lambda i: (0, 0))


def _acc_add(acc_ref, rows):
    @pl.when(pl.program_id(0) == 0)
    def _():
        acc_ref[...] = jnp.zeros_like(acc_ref)

    for r, val in enumerate(rows):
        acc_ref[r:r + 1, :] += jnp.sum(val, axis=0, keepdims=True)


def _mlp_out_loss(hid, x1, a, target, mod6, gf, w_mo, tm=512):
    s = x1.shape[0]

    def body(hid_ref, x_ref, a_ref, t_ref, mod_ref, gf_ref, w_ref, dx2_ref, dm_ref, da_ref, acc_ref):
        m = _nn(hid_ref[...], w_ref[...])
        gate2 = mod_ref[5:6, :]
        x2 = x_ref[...] + gate2 * m
        r = lax.rsqrt(jnp.mean(x2 * x2, axis=-1, keepdims=True) + NORM_EPS)
        xn = x2 * r
        err = xn * gf_ref[...] - t_ref[...]
        dy = err * (1.0 / D_MODEL)
        dxn = dy * gf_ref[...]
        dx2 = r * (dxn - xn * jnp.mean(dxn * xn, axis=-1, keepdims=True))
        dx2_ref[...] = dx2
        dm = (dx2 * gate2).astype(BF16)
        dm_ref[...] = dm
        da_ref[...] = (_nt(dm, w_ref[...]) * (2.0 * jnp.maximum(a_ref[...].astype(F32), 0.0))).astype(BF16)
        _acc_add(acc_ref, [err * err, dy * xn, dx2 * m])

    return _call(
        body, name="mlp_out_loss", grid=(s // tm,),
        in_specs=[_row_spec(tm, D_FF), _row_spec(tm, D_MODEL), _row_spec(tm, D_FF), _row_spec(tm, D_MODEL),
                  _full_spec(mod6.shape), _full_spec(gf.shape), _full_spec(w_mo.shape)],
        out_specs=[_row_spec(tm, D_MODEL), _row_spec(tm, D_MODEL), _row_spec(tm, D_FF), _acc_spec()],
        out_shape=[jax.ShapeDtypeStruct((s, D_MODEL), F32), jax.ShapeDtypeStruct((s, D_MODEL), BF16),
                   jax.ShapeDtypeStruct((s, D_FF), BF16), jax.ShapeDtypeStruct((ACC_ROWS, D_MODEL), F32)],
        compiler_params=_params(("arbitrary",)),
    )(hid, x1, a, target, mod6, gf, w_mo)


def _norm_bwd(dh, xt, gain, scale):
    r = lax.rsqrt(jnp.mean(xt * xt, axis=-1, keepdims=True) + NORM_EPS)
    xn = xt * r
    dxn = dh * (gain * (1.0 + scale))
    dx = r * (dxn - xn * jnp.mean(dxn * xn, axis=-1, keepdims=True))
    return dx, [dh, dh * xn * gain, dh * xn * (1.0 + scale)]


def _mlp_bwd(da, x1, dx2, o, mod6, g2, w_mi_s, rider=None, tm=512):
    s = x1.shape[0]

    def body(da_ref, x_ref, dx2_ref, o_ref, mod_ref, g_ref, w_ref, dx1_ref, do_ref, acc_ref):
        dh2 = _nt(da_ref[:, :D_MODEL], w_ref[0])
        for j in range(1, N_SHARD):
            dh2 += _nt(da_ref[:, D_MODEL * j:D_MODEL * (j + 1)], w_ref[j])
        dx, sums = _norm_bwd(dh2, x_ref[...], g_ref[...], mod_ref[4:5, :])
        dx1 = dx2_ref[...] + dx
        dx1_ref[...] = dx1
        do_ref[...] = (dx1 * mod_ref[2:3, :]).astype(BF16)
        _acc_add(acc_ref, sums + [dx1 * o_ref[...].astype(F32)])

    return _hosted(
        body, rider, name="mlp_bwd", grid=(s // tm,),
        in_specs=[_row_spec(tm, D_FF), _row_spec(tm, D_MODEL), _row_spec(tm, D_MODEL), _row_spec(tm, D_MODEL),
                  _full_spec(mod6.shape), _full_spec(g2.shape), _full_spec(w_mi_s.shape)],
        out_specs=[_row_spec(tm, D_MODEL), _row_spec(tm, D_MODEL), _acc_spec()],
        out_shape=[jax.ShapeDtypeStruct((s, D_MODEL), F32), jax.ShapeDtypeStruct((s, D_MODEL), BF16),
                   jax.ShapeDtypeStruct((ACC_ROWS, D_MODEL), F32)],
        args=(da, x1, dx2, o, mod6, g2, w_mi_s))


def _merge_bwd(do, gates, ua, ub, w_out, wbr_s, rider=None, tm=512):
    s = do.shape[0]

    def body(do_ref, g_ref, ua_ref, ub_ref, wo_ref, wbr_ref, dua_ref, dub_ref, dg_ref, doa_ref, dob_ref):
        dmerged = _nt(do_ref[...], wo_ref[...])
        for b, (u_ref, du_ref, dy_ref) in enumerate(((ua_ref, dua_ref, doa_ref), (ub_ref, dub_ref, dob_ref))):
            sg = _sigmoid(g_ref[:, D_MODEL * b:D_MODEL * (b + 1)].astype(F32))
            du = (dmerged * sg).astype(BF16)
            du_ref[...] = du
            dg_ref[:, D_MODEL * b:D_MODEL * (b + 1)] = (dmerged * u_ref[...].astype(F32) * sg * (1.0 - sg)).astype(BF16)
            w = BRANCH_W // 2
            dy = _nt(du[:, :w], wbr_ref[0, b])
            for j in range(1, N_SHARD):
                dy += _nt(du[:, w * j:w * (j + 1)], wbr_ref[j, b])
            dyb = dy.astype(BF16)
            for h in range(Q_HEADS):
                dy_ref[h] = dyb[:, HEAD_DIM * h:HEAD_DIM * (h + 1)]

    bf = jax.ShapeDtypeStruct((s, D_MODEL), BF16)
    heads = jax.ShapeDtypeStruct((Q_HEADS, s, HEAD_DIM), BF16)
    return _hosted(
        body, rider, name="merge_bwd", grid=(s // tm,),
        in_specs=[_row_spec(tm, D_MODEL), _row_spec(tm, 2 * D_MODEL), _row_spec(tm, D_MODEL), _row_spec(tm, D_MODEL),
                  _full_spec(w_out.shape), _full_spec(wbr_s.shape)],
        out_specs=[_row_spec(tm, D_MODEL), _row_spec(tm, D_MODEL), _row_spec(tm, 2 * D_MODEL),
                   _heads_spec(Q_HEADS, tm), _heads_spec(Q_HEADS, tm)],
        out_shape=[bf, bf, jax.ShapeDtypeStruct((s, 2 * D_MODEL), BF16), heads, heads],
        args=(do, gates, ua, ub, w_out, wbr_s))


def _qk_bwd(dqa, dka, dva, dqb, dkb, dvb, qkraw, dgates, gq, gk, bd, tabs, rider=None, tm=512):
    s = qkraw.shape[0]

    def body(dqa_ref, dka_ref, dva_ref, dqb_ref, dkb_ref, dvb_ref, raw_ref, dg_ref, gq_ref, gk_ref, bd_ref,
             ca, la, ha, cb, lb, hb, dp_ref, acc_ref, pair_ref):
        bdm = bd_ref[...]
        tab_a = (ca[...], la[...], ha[...])
        tab_b = (cb[...], lb[...], hb[...])

        def pair(ref, first):
            pair_ref[:, :HEAD_DIM] = ref[first].astype(F32)
            pair_ref[:, HEAD_DIM:] = ref[first + 1].astype(F32)
            return pair_ref[...]

        def norm_rope_a_bwd(dz, raw, gain):
            dzn = _rope(dz, *tab_a, 16, sign=-1.0)
            rinv = lax.rsqrt(_head_mean(raw * raw, bdm) + NORM_EPS)
            zhat = raw * rinv
            dzhat = dzn * gain
            return rinv * (dzhat - zhat * _head_mean(dzhat * zhat, bdm)), dzn * zhat

        gq_rows = jnp.zeros((tm, LANES), F32)
        for i in range(Q_HEADS // 2):
            at = slice(LANES * i, LANES * (i + 1))
            draw, gsum = norm_rope_a_bwd(pair(dqa_ref, 2 * i) * Q_SCALE, raw_ref[:, at], gq_ref[...])
            dp_ref[:, at] = draw.astype(BF16)
            gq_rows += gsum
        off = BRANCH_W
        draw, gk_rows = norm_rope_a_bwd(pair(dka_ref, 0), raw_ref[:, off:off + LANES], gk_ref[...])
        dp_ref[:, off:off + LANES] = draw.astype(BF16)
        off += KV_W
        dp_ref[:, off:off + LANES] = pair(dva_ref, 0).astype(BF16)
        off += KV_W
        for i in range(Q_HEADS // 2):
            dz = _rope(pair(dqb_ref, 2 * i) * Q_SCALE, *tab_b, 32, sign=-1.0)
            dp_ref[:, off + LANES * i:off + LANES * (i + 1)] = dz.astype(BF16)
        off += BRANCH_W
        dp_ref[:, off:off + LANES] = _rope(pair(dkb_ref, 0), *tab_b, 32, sign=-1.0).astype(BF16)
        off += KV_W
        dp_ref[:, off:off + LANES] = pair(dvb_ref, 0).astype(BF16)
        dp_ref[:, QK_W:] = dg_ref[...]

        @pl.when(pl.program_id(0) == 0)
        def _():
            acc_ref[...] = jnp.zeros_like(acc_ref)

        acc_ref[0:1, :] += jnp.sum(gq_rows, axis=0, keepdims=True)
        acc_ref[1:2, :] += jnp.sum(gk_rows, axis=0, keepdims=True)

    tab_spec = _row_spec(tm, LANES)
    return _hosted(
        body, rider, name="qk_bwd", grid=(s // tm,),
        in_specs=[_heads_spec(Q_HEADS, tm), _heads_spec(KV_HEADS, tm), _heads_spec(KV_HEADS, tm),
                  _heads_spec(Q_HEADS, tm), _heads_spec(KV_HEADS, tm), _heads_spec(KV_HEADS, tm),
                  _row_spec(tm, BRANCH_W + KV_W), _row_spec(tm, 2 * D_MODEL),
                  _full_spec(gq.shape), _full_spec(gk.shape), _full_spec(bd.shape)] + [tab_spec] * 6,
        out_specs=[_row_spec(tm, IN_W), pl.BlockSpec((ACC_ROWS, LANES), lambda i: (0, 0))],
        out_shape=[jax.ShapeDtypeStruct((s, IN_W), BF16), jax.ShapeDtypeStruct((ACC_ROWS, LANES), F32)],
        scratch_shapes=[pltpu.VMEM((tm, LANES), F32)],
        args=(dqa, dka, dva, dqb, dkb, dvb, qkraw, dgates, gq, gk, bd, *tabs))


def _in_proj_bwd(dproj, x, dx1, mod6, g1, w_in_s, tm=512):
    s = x.shape[0]
    w = IN_W // N_SHARD

    def body(dp_ref, x_ref, dx1_ref, mod_ref, g_ref, w_ref, gx_ref, acc_ref):
        dh = _nt(dp_ref[:, :w], w_ref[0])
        for j in range(1, N_SHARD):
            dh += _nt(dp_ref[:, w * j:w * (j + 1)], w_ref[j])
        dx, sums = _norm_bwd(dh, x_ref[...], g_ref[...], mod_ref[1:2, :])
        gx_ref[...] = dx1_ref[...] + dx
        _acc_add(acc_ref, sums)

    return _call(
        body, name="in_proj_bwd", grid=(s // tm,),
        in_specs=[_row_spec(tm, IN_W), _row_spec(tm, D_MODEL), _row_spec(tm, D_MODEL),
                  _full_spec(mod6.shape), _full_spec(g1.shape), _full_spec(w_in_s.shape)],
        out_specs=[_row_spec(tm, D_MODEL), _acc_spec()],
        out_shape=[jax.ShapeDtypeStruct((s, D_MODEL), F32), jax.ShapeDtypeStruct((ACC_ROWS, D_MODEL), F32)],
        compiler_params=_params(("arbitrary",)),
    )(dproj, x, dx1, mod6, g1, w_in_s)


def _wgrad(name, a, b, out_shape, out_spec, tm, tn, tk=4096, whole_b=False):
    s, m = a.shape
    n = b.shape[1]
    tk = s if whole_b else min(tk, s)
    nk = s // tk
    assert not whole_b or tn == n
    b_mode = dict(pipeline_mode=pl.Buffered(1)) if whole_b else {}

    def body(a_ref, b_ref, o_ref, acc_ref):
        k = pl.program_id(2)

        @pl.when(k == 0)
        def _():
            acc_ref[...] = jnp.zeros_like(acc_ref)

        acc_ref[...] += _tn(a_ref[...], b_ref[...])

        @pl.when(k == nk - 1)
        def _():
            o_ref[...] = acc_ref[...].reshape(o_ref.shape)

    return _call(
        body, name=name, grid=(m // tm, n // tn, nk),
        in_specs=[pl.BlockSpec((tk, tm), lambda i, j, k: (k, i)), pl.BlockSpec((tk, tn), lambda i, j, k: (k, j), **b_mode)],
        out_specs=out_spec, out_shape=jax.ShapeDtypeStruct(out_shape, F32),
        scratch_shapes=[pltpu.VMEM((tm, tn), F32)],
        compiler_params=_params(("parallel", "parallel", "arbitrary")),
    )(a, b)


def _wgrad_branch(ya, yb, dua, dub, tk=2048):
    s = ya.shape[0]
    tk = min(tk, s)
    nk = s // tk
    w = D_MODEL // N_SHARD

    def body(ya_ref, yb_ref, dua_ref, dub_ref, o_ref, acc_ref):
        b, k = pl.program_id(0), pl.program_id(1)

        @pl.when(k == 0)
        def _():
            acc_ref[...] = jnp.zeros_like(acc_ref)

        @pl.when(b == 0)
        def _():
            acc_ref[...] += _tn(ya_ref[...], dua_ref[...])

        @pl.when(b == 1)
        def _():
            acc_ref[...] += _tn(yb_ref[...], dub_ref[...])

        @pl.when(k == nk - 1)
        def _():
            for j in range(N_SHARD):
                o_ref[j] = acc_ref[:, w * j:w * (j + 1)]

    first = lambda width: pl.BlockSpec((tk, width), lambda b, k: (k * (1 - b), 0))
    second = lambda width: pl.BlockSpec((tk, width), lambda b, k: (k * b, 0))
    return _call(
        body, name="wgrad_branch", grid=(2, nk),
        in_specs=[first(BRANCH_W), second(BRANCH_W), first(D_MODEL), second(D_MODEL)],
        out_specs=pl.BlockSpec((N_SHARD, None, BRANCH_W, w), lambda b, k: (0, b, 0, 0)),
        out_shape=jax.ShapeDtypeStruct((N_SHARD, 2, BRANCH_W, w), F32),
        scratch_shapes=[pltpu.VMEM((BRANCH_W, D_MODEL), F32)],
        compiler_params=_params(("parallel", "arbitrary")),
    )(ya, yb, dua, dub)


def _local_step(x, target, mod6, g1, g2, gf, gq2, gk2, sink, w_in_s, rest, cj=None, tabs=None):
    s = x.shape[0]
    dist = cj is not None
    tabs = _rope_tables(s) if tabs is None else tabs
    bd = _block_diag()
    sink_col = jnp.repeat(sink.reshape(KV_HEADS, GROUP, 1), TQ_B, axis=1).reshape(KV_HEADS, GROUP * TQ_B, 1)
    shard = D_MODEL // N_SHARD

    h, qkraw, qa, ka, va, qb, kb, vb, gates = _in_proj(x, mod6, g1, w_in_s, gq2, gk2, bd, tabs)
    bias = _window_bias(TQ_B)
    (yb, yb_heads, lse_b), _ = _attn_b_fwd(qb, kb, vb, sink_col, bias)
    (ya, ya_heads, lse_a), gathered = _attn_a_fwd(qa, ka, va, rider=_gather_rider(rest) if dist else None)
    wbr_s, w_out, w_mi_s, w_mo = gathered if dist else rest
    wbr_s = wbr_s.reshape(N_SHARD, 2, BRANCH_W, shard)
    w_out = w_out.reshape(D_MODEL, D_MODEL)
    w_mo = w_mo.reshape(D_FF, D_MODEL)
    ua, ub, merged, o, x1 = _post_attn(ya, yb, gates, x, mod6, wbr_s, w_out)
    h2, a, hid = _mlp_in(x1, mod6, g2, w_mi_s)
    dx2, dm, da, acc_out = _mlp_out_loss(hid, x1, a, target, mod6, gf, w_mo)

    g_w_mo = _wgrad("wgrad_mlp_out", hid, dm, (D_FF, D_MODEL), pl.BlockSpec((512, D_MODEL), lambda i, j, k: (i, 0)),
                    512, D_MODEL, whole_b=True).reshape(N_SHARD, D_MODEL, D_MODEL)
    g_w_mi = _wgrad("wgrad_mlp_in", h2, da, (N_SHARD, D_MODEL, D_MODEL),
                    pl.BlockSpec((None, D_MODEL, D_MODEL), lambda i, j, k: (j, i, 0)), D_MODEL, D_MODEL)
    mlp = _Reduction(("mlp_out", "mlp_in"), (g_w_mo, g_w_mi), cj)
    (dx1, do, acc_mlp), got = _mlp_bwd(da, x1, dx2, o, mod6, g2, w_mi_s, rider=mlp.swap() if dist else None)
    (dua, dub, dgates, doa, dob), _ = _merge_bwd(do, gates, ua, ub, w_out, wbr_s)
    g_w_out = _wgrad("wgrad_out", merged, do, (D_MODEL, D_MODEL), pl.BlockSpec((512, D_MODEL), lambda i, j, k: (i, 0)),
                     512, D_MODEL, whole_b=True).reshape(N_SHARD, shard, D_MODEL)
    g_wbr = _wgrad_branch(ya, yb, dua, dub)
    out = _Reduction(("out", "branch"), (g_w_out, g_wbr.reshape(N_SHARD, 2 * BRANCH_W, shard)), cj)
    (dqa, dka, dva), landings = _attn_a_bwd(qa, ka, va, ya_heads, doa, lse_a,
                                            rider=_riders(mlp.add(got), out.swap()) if dist else None)
    (dqb, dkb, dvb, dsink), joined = _attn_b_bwd(qb, kb, vb, yb_heads, dob, lse_b, sink_col, bias,
                                                 rider=mlp.total(landings[:2]) if dist else None)
    heads = (Q_HEADS, s, HEAD_DIM)
    (dproj, acc_qk), landed = _qk_bwd(dqa.reshape(heads), dka, dva, dqb.reshape(heads), dkb, dvb, qkraw, dgates, gq2, gk2, bd,
                                      tabs, rider=out.add(landings[2:]) if dist else None)
    w = IN_W // N_SHARD
    g_w_in = _wgrad("wgrad_in", h, dproj, (N_SHARD, D_MODEL, w), pl.BlockSpec((None, D_MODEL, w), lambda i, j, k: (j, i, 0)),
                    D_MODEL, w)
    grad_x, acc_in = _in_proj_bwd(dproj, x, dx1, mod6, g1, w_in_s)
    accs = (acc_out, acc_mlp, acc_in, acc_qk, dsink)
    if not dist:
        return grad_x, (g_w_in, g_wbr, g_w_out, g_w_mi, g_w_mo), accs
    r_mo, r_mi = joined
    return grad_x, (_Reduction(("in",), (g_w_in,), cj), out.total(landed), r_mi, r_mo), accs


def _me():
    return lax.axis_index("x"), lax.axis_index("y"), lax.axis_index("c")


def _peer(d):
    x, y, c = _me()
    return (1 - x if d & 4 else x, 1 - y if d & 2 else y, 1 - c if d & 1 else c)


def _dev_index(p):
    return 4 * p[0] + 2 * p[1] + p[2]


def _chip_index(p):
    return 2 * p[0] + p[1]


def _remote(src, dst, send_sem, recv_sem, to):
    return pltpu.make_async_remote_copy(src_ref=src, dst_ref=dst, send_sem=send_sem, recv_sem=recv_sem,
                                        device_id=to, device_id_type=MESH)


SLOT_ROWS = 8


def _ada_fwd(c, w_ada, b4, inv, s, rider=None):
    cols = w_ada.shape[1]
    chunk = 1024

    def tables(inv_ref, tab_refs):
        lane = lax.broadcasted_iota(jnp.int32, (chunk, LANES), 1)
        is_col = (lane & (HEAD_DIM - 1)) >= HEAD_DIM // 2
        first_a = (lane & 31) < 16
        first_b = (lane & (HEAD_DIM - 1)) < HEAD_DIM // 2

        def rows(i, carry):
            at = pl.ds(pl.multiple_of(i * chunk, chunk), chunk)
            t = i * chunk + lax.broadcasted_iota(jnp.int32, (chunk, LANES), 0)
            pos_a = jnp.where(is_col, t & (GRID_W - 1), t // GRID_W).astype(F32)
            for pos, row, first, out in ((pos_a, 0, first_a, tab_refs[:3]), (t.astype(F32), 1, first_b, tab_refs[3:])):
                ang = pos * inv_ref[row:row + 1, :]
                sin = jnp.sin(ang)
                out[0][at, :] = jnp.cos(ang)
                out[1][at, :] = jnp.where(first, -sin, 0.0)
                out[2][at, :] = jnp.where(first, 0.0, sin)
            return carry

        lax.fori_loop(0, s // chunk, rows, 0)

    def body(c_ref, w_ref, b_ref, inv_ref, mod_ref, sc_ref, ca, la, ha, cb, lb, hb, cbuf, pbuf, mbuf, send1, recv1, send2, recv2,
             launch=None):
        me = _me()
        mine, chip = _dev_index(me), _chip_index(me)
        cbuf[mine] = jnp.broadcast_to(c_ref[...], (SLOT_ROWS, D_MODEL))
        gather = [_remote(cbuf.at[mine], cbuf.at[mine], send1.at[d - 1], recv1.at[d - 1], _peer(d)) for d in range(1, N_DEV)]
        for cp in gather:
            cp.start()
        if launch is not None:
            launch()
        tables(inv_ref, (ca, la, ha, cb, lb, hb))
        for d in range(1, N_DEV):
            _remote(cbuf.at[mine], cbuf.at[_dev_index(_peer(d))], send1.at[d - 1], recv1.at[d - 1], _peer(d)).wait_recv()
        call = cbuf[...].reshape(N_DEV * SLOT_ROWS, D_MODEL)
        sc = call * _sigmoid(call)
        for s in range(N_DEV):
            sc_ref[s:s + 1, :] = sc[SLOT_ROWS * s:SLOT_ROWS * s + 1]
        part = _nn(sc.astype(BF16), w_ref[...].astype(BF16)) + b_ref[pl.ds(chip, 1), :]
        pbuf[...] = part.reshape(N_DEV, SLOT_ROWS, cols)
        mbuf[chip] = pbuf[mine]
        spread = [_remote(pbuf.at[_dev_index(_peer(d))], mbuf.at[chip], send2.at[d // 2 - 1], recv2.at[d // 2 - 1], _peer(d))
                  for d in (2, 4, 6)]
        for cp in spread:
            cp.start()
        for d in (2, 4, 6):
            _remote(pbuf.at[mine], mbuf.at[_chip_index(_peer(d))], send2.at[d // 2 - 1], recv2.at[d // 2 - 1],
                    _peer(d)).wait_recv()
        half = D_MODEL // 2
        for p in range(2 * 6):
            col = half * p
            mod_ref[p // 2:p // 2 + 1, half * (p % 2):half * (p % 2 + 1)] = mbuf[col // cols, 0:1, col % cols:col % cols + half]
        for cp in gather + spread:
            cp.wait_send()

    vm = pl.BlockSpec(memory_space=pltpu.VMEM)
    return _hosted(
        body, rider, name="ada_fwd", grid=(), in_specs=[vm] * 4, out_specs=[vm] * 8,
        out_shape=[jax.ShapeDtypeStruct((6, D_MODEL), F32), jax.ShapeDtypeStruct((N_DEV, D_MODEL), F32)]
        + [jax.ShapeDtypeStruct((s, LANES), F32)] * 6,
        scratch_shapes=[pltpu.VMEM((N_DEV, SLOT_ROWS, D_MODEL), F32), pltpu.VMEM((N_DEV, SLOT_ROWS, cols), F32),
                        pltpu.VMEM((N_SHARD, SLOT_ROWS, cols), F32),
                        pltpu.SemaphoreType.DMA((N_DEV - 1,)), pltpu.SemaphoreType.DMA((N_DEV - 1,)),
                        pltpu.SemaphoreType.DMA((N_SHARD - 1,)), pltpu.SemaphoreType.DMA((N_SHARD - 1,))],
        args=(c, w_ada, b4, inv))


PACK_ROWS = 16
PACK_W = 3 * D_MODEL


def _ada_bwd(acc_out, acc_mlp, acc_in, acc_qk, dsink, sc_all, rider=None):
    cols = 6 * D_MODEL // N_SHARD

    def body(out_ref, mlp_ref, in_ref, qk_ref, dsink_ref, sc_ref,
             gwa_ref, gba_ref, gn1_ref, gn2_ref, gf_ref, gq_ref, gk_ref, gs_ref, loss_ref, blk, send, recv, launch=None):
        me = _me()
        mine, chip = _dev_index(me), _chip_index(me)
        blk[mine] = jnp.zeros((PACK_ROWS, PACK_W), F32)
        dmod = (in_ref, 0), (in_ref, 1), (mlp_ref, 3), (mlp_ref, 0), (mlp_ref, 1), (out_ref, 2)
        half = D_MODEL // 2
        for p in range(2 * 6):
            ref, row = dmod[p // 2]
            col = half * p
            blk[mine, col // cols:col // cols + 1, col % cols:col % cols + half] = ref[row:row + 1, half * (p % 2):half * (p % 2 + 1)]
        blk[mine, 4:5, 0:D_MODEL] = in_ref[2:3, :]
        blk[mine, 4:5, D_MODEL:2 * D_MODEL] = mlp_ref[2:3, :]
        blk[mine, 4:5, 2 * D_MODEL:] = out_ref[1:2, :]
        blk[mine, 5:6, 0:LANES] = qk_ref[0:1, :]
        blk[mine, 5:6, LANES:2 * LANES] = qk_ref[1:2, :]
        blk[mine, 6:7, 0:D_MODEL] = out_ref[0:1, :]
        for g in range(KV_HEADS):
            blk[mine, 8 + GROUP * g:8 + GROUP * (g + 1), 0:LANES] = dsink_ref[g, 0:GROUP, :]
        copies = [_remote(blk.at[mine], blk.at[mine], send.at[d - 1], recv.at[d - 1], _peer(d)) for d in range(1, N_DEV)]
        for cp in copies:
            cp.start()
        if launch is not None:
            launch()
        for d in range(1, N_DEV):
            _remote(blk.at[mine], blk.at[_dev_index(_peer(d))], send.at[d - 1], recv.at[d - 1], _peer(d)).wait_recv()
        tot = blk[0]
        for s in range(1, N_DEV):
            tot = tot + blk[s]
        for j in range(N_SHARD):
            gba_ref[:, cols * j:cols * (j + 1)] = tot[j:j + 1, :cols]
        gn1_ref[...] = tot[4:5, 0:D_MODEL]
        gn2_ref[...] = tot[4:5, D_MODEL:2 * D_MODEL]
        gf_ref[...] = tot[4:5, 2 * D_MODEL:]
        gq_ref[...] = tot[5:6, 0:HEAD_DIM] + tot[5:6, HEAD_DIM:2 * HEAD_DIM]
        gk_ref[...] = tot[5:6, LANES:LANES + HEAD_DIM] + tot[5:6, LANES + HEAD_DIM:2 * LANES]
        sq = tot[8:16, 0:Q_HEADS]
        diag = lax.broadcasted_iota(jnp.int32, sq.shape, 0) == lax.broadcasted_iota(jnp.int32, sq.shape, 1)
        gs_ref[...] = jnp.sum(jnp.where(diag, sq, 0.0), axis=0, keepdims=True)
        half_mse = (0.5 / D_MODEL) * jnp.sum(tot[6:7, 0:D_MODEL], axis=-1, keepdims=True)
        loss_ref[...] = jnp.broadcast_to(half_mse, (1, LANES))
        dm = jnp.concatenate([blk[s, pl.ds(chip, 1), pl.ds(0, cols)] for s in range(N_DEV)], axis=0)
        gwa_ref[...] = _tn(sc_ref[...], dm)
        for cp in copies:
            cp.wait_send()

    vm = pl.BlockSpec(memory_space=pltpu.VMEM)
    row = lambda n: jax.ShapeDtypeStruct((1, n), F32)
    return _hosted(
        body, rider, name="ada_bwd", grid=(), in_specs=[vm] * 6, out_specs=[vm] * 9,
        out_shape=[jax.ShapeDtypeStruct((D_MODEL, cols), F32), row(6 * D_MODEL), row(D_MODEL), row(D_MODEL), row(D_MODEL),
                   row(HEAD_DIM), row(HEAD_DIM), row(Q_HEADS), row(LANES)],
        scratch_shapes=[pltpu.VMEM((N_DEV, PACK_ROWS, PACK_W), F32),
                        pltpu.SemaphoreType.DMA((N_DEV - 1,)), pltpu.SemaphoreType.DMA((N_DEV - 1,))],
        args=(acc_out, acc_mlp, acc_in, acc_qk, dsink, sc_all))


def _cast_weights(ws):
    n = len(ws)

    def body(*refs):
        src, out, tmp, sems = refs[:n], refs[n:2 * n], refs[2 * n:3 * n], refs[3 * n]
        chip = _chip_index(_me())
        copies = []
        for a in range(n):
            tmp[a][...] = src[a][...].astype(BF16)
            cp = pltpu.make_async_copy(tmp[a], out[a].at[chip], sems.at[a])
            cp.start()
            copies.append(cp)
        for cp in copies:
            cp.wait()

    vm = pl.BlockSpec(memory_space=pltpu.VMEM)
    return _call(
        body, name="cast_weights", in_specs=[vm] * n, out_specs=[ANY] * n,
        out_shape=[jax.ShapeDtypeStruct((N_SHARD,) + w.shape, BF16) for w in ws],
        scratch_shapes=[pltpu.VMEM(w.shape, BF16) for w in ws] + [pltpu.SemaphoreType.DMA((n,))],
        compiler_params=_params(),
    )(*ws)


def _half_rows(ref_rows, c):
    half = ref_rows // 2
    return pl.ds(pl.multiple_of(c * half, 8), half)


class _Rider:
    def __init__(self, inputs, out_shape, aliases, n_sems, start, finish, middle=None):
        self.inputs, self.out_shape, self.aliases, self.n_sems = list(inputs), list(out_shape), dict(aliases), n_sems
        self.start, self.finish, self.middle = start, finish, middle


def _riders(*rs):
    ins = [0]
    outs = [0]
    sems = [0]
    for r in rs:
        ins.append(ins[-1] + len(r.inputs))
        outs.append(outs[-1] + len(r.out_shape))
        sems.append(sems[-1] + r.n_sems)

    def phase(which):
        def run(in_refs, out_refs, sem):
            for k, r in enumerate(rs):
                fn = getattr(r, which)
                if fn is not None:
                    fn(in_refs[ins[k]:ins[k + 1]], out_refs[outs[k]:outs[k + 1]], lambda j, base=sems[k]: sem(base + j))
        return run

    aliases = {ins[k] + i: outs[k] + o for k, r in enumerate(rs) for i, o in r.aliases.items()}
    return _Rider([a for r in rs for a in r.inputs], [o for r in rs for o in r.out_shape], aliases, sems[-1],
                  phase("start"), phase("finish"), phase("middle") if any(r.middle for r in rs) else None)


def _hosted(body, rider, *, name, grid, in_specs, out_specs, out_shape, args, scratch_shapes=(), middle_at=None):
    where = dict(grid=grid, compiler_params=_params(("arbitrary",) * len(grid))) if grid else dict(compiler_params=_params())
    if rider is None:
        res = _call(body, name=name, in_specs=in_specs, out_specs=out_specs, out_shape=out_shape,
                    scratch_shapes=list(scratch_shapes), **where)(*args)
        return res, ()
    n_in, n_out, n_scr = len(in_specs), len(out_specs), len(scratch_shapes)
    r_in, r_out = len(rider.inputs), len(rider.out_shape)

    def riding(*refs):
        at = 0
        parts = []
        for size in (n_in, r_in, n_out, r_out, n_scr):
            parts.append(refs[at:at + size])
            at += size
        ins, rider_ins, outs, rider_outs, scratch = parts
        sems = refs[at]

        def sem_at(k):
            return sems.at[k]

        if not grid:
            body(*ins, *outs, *scratch, launch=lambda: rider.start(rider_ins, rider_outs, sem_at))
            if rider.middle is not None:
                rider.middle(rider_ins, rider_outs, sem_at)
            rider.finish(rider_ins, rider_outs, sem_at)
            return
        step = pl.program_id(0)
        for axis in range(1, len(grid)):
            step = step * grid[axis] + pl.program_id(axis)
        steps = 1
        for size in grid:
            steps *= size

        @pl.when(step == 0)
        def _():
            rider.start(rider_ins, rider_outs, sem_at)

        body(*ins, *outs, *scratch)
        if rider.middle is not None:
            @pl.when(step == middle_at)
            def _():
                rider.middle(rider_ins, rider_outs, sem_at)

        @pl.when(step == steps - 1)
        def _():
            rider.finish(rider_ins, rider_outs, sem_at)

    res = _call(
        riding, name=name, in_specs=list(in_specs) + [ANY] * r_in, out_specs=list(out_specs) + [ANY] * r_out,
        out_shape=list(out_shape) + rider.out_shape,
        input_output_aliases={n_in + i: n_out + o for i, o in rider.aliases.items()},
        scratch_shapes=list(scratch_shapes) + [pltpu.SemaphoreType.DMA((rider.n_sems,))], **where,
    )(*args, *rider.inputs)
    return res[:n_out], res[n_out:]


def _alone(name, rider):
    n_in, n_out = len(rider.inputs), len(rider.out_shape)

    def body(*refs):
        ins, outs, sems = refs[:n_in], refs[n_in:n_in + n_out], refs[n_in + n_out]

        def sem_at(k):
            return sems.at[k]

        rider.start(ins, outs, sem_at)
        if rider.middle is not None:
            rider.middle(ins, outs, sem_at)
        rider.finish(ins, outs, sem_at)

    return _call(
        body, name=name, in_specs=[ANY] * n_in, out_specs=[ANY] * n_out, out_shape=rider.out_shape,
        input_output_aliases=rider.aliases, scratch_shapes=[pltpu.SemaphoreType.DMA((rider.n_sems,))],
    )(*rider.inputs)


OTHER_CHIPS = (2, 4, 6)


def _gather_rider(stacked):
    n = len(stacked)

    def flights(bufs, sem):
        me = _me()
        chip, sib = _chip_index(me), _peer(1)
        out = []
        for a in range(n):
            mine, theirs = (_half_rows(bufs[a].shape[1], c) for c in (me[2], 1 - me[2]))
            for j, d in enumerate(OTHER_CHIPS):
                k = 3 * a + j
                from_chip = _chip_index(_peer(d))
                own, landed, passed = bufs[a].at[chip, mine], bufs[a].at[from_chip, mine], bufs[a].at[from_chip, theirs]
                out.append((_remote(own, own, sem(k), sem(3 * n + k), _peer(d)),
                            _remote(own, landed, sem(k), sem(3 * n + k), _peer(d)),
                            _remote(landed, landed, sem(6 * n + k), sem(9 * n + k), sib),
                            _remote(passed, passed, sem(6 * n + k), sem(9 * n + k), sib)))
        return out

    def start(ins, outs, sem):
        for send, _, _, _ in flights(outs, sem):
            send.start()

    def middle(ins, outs, sem):
        for _, arrival, pass_on, _ in flights(outs, sem):
            arrival.wait_recv()
            pass_on.start()

    def finish(ins, outs, sem):
        every = flights(outs, sem)
        for _, _, _, passed_to_me in every:
            passed_to_me.wait_recv()
        for send, _, pass_on, _ in every:
            send.wait_send()
            pass_on.wait_send()

    return _Rider(stacked, [jax.ShapeDtypeStruct(w.shape, w.dtype) for w in stacked], {a: a for a in range(n)}, 12 * n,
                  start, finish, middle)


def _swap_rider(grads):
    n = len(grads)

    def copies(ins, outs, sem):
        c = _me()[2]
        return [_remote(ins[a].at[pl.ds(0, N_SHARD), _half_rows(ins[a].shape[1], 1 - c)], outs[a], sem(a), sem(n + a), _peer(1))
                for a in range(n)]

    def start(ins, outs, sem):
        for cp in copies(ins, outs, sem):
            cp.start()

    def finish(ins, outs, sem):
        for cp in copies(ins, outs, sem):
            cp.wait()

    return _Rider(grads, [jax.ShapeDtypeStruct((N_SHARD, g.shape[1] // 2, g.shape[2]), F32) for g in grads], {}, 2 * n,
                  start, finish)


def _row_tile(rows):
    return min(rows, 256)


def _add_halves(name, g, got, cj):
    _, half, cols = got.shape
    tr = _row_tile(half)
    nt = half // tr

    def body(cj_ref, g_ref, got_ref, o_ref):
        o_ref[...] = (g_ref[...] + got_ref[...]).astype(BF16)

    spec = pl.BlockSpec((None, tr, cols), lambda i, s, cj: (s, i, 0))
    return _call(
        body, name=name,
        grid_spec=pltpu.PrefetchScalarGridSpec(
            num_scalar_prefetch=1, grid=(nt, N_SHARD),
            in_specs=[pl.BlockSpec((None, tr, cols), lambda i, s, cj: (s, cj[0] * nt + i, 0)), spec], out_specs=spec),
        out_shape=jax.ShapeDtypeStruct(got.shape, BF16), compiler_params=_params(("parallel", "parallel")),
    )(cj, g, got)


def _scatter_rider(sums):
    n = len(sums)

    def flights(ins, outs, sem):
        chip = _chip_index(_me())
        out = []
        for a in range(n):
            for j, d in enumerate(OTHER_CHIPS):
                k = 3 * a + j
                other = _chip_index(_peer(d))
                out.append((_remote(ins[a].at[other], outs[a].at[chip], sem(k), sem(3 * n + k), _peer(d)),
                            _remote(ins[a].at[chip], outs[a].at[other], sem(k), sem(3 * n + k), _peer(d))))
        return out

    def start(ins, outs, sem):
        for send, _ in flights(ins, outs, sem):
            send.start()

    def finish(ins, outs, sem):
        every = flights(ins, outs, sem)
        for _, arrival in every:
            arrival.wait_recv()
        for send, _ in every:
            send.wait_send()

    return _Rider(sums, [jax.ShapeDtypeStruct(v.shape, v.dtype) for v in sums], {}, 6 * n, start, finish)


def _sum_chips(name, g, got, landed, cj):
    _, half, cols = got.shape
    tr = _row_tile(half)
    nt = half // tr

    def body(cj_ref, g_ref, got_ref, landed_ref, o_ref):
        own = g_ref[...] + got_ref[...]
        total = None
        for s in range(N_SHARD):
            term = jnp.where(cj_ref[1] == s, own, landed_ref[s].astype(F32))
            total = term if total is None else total + term
        o_ref[...] = total

    return _call(
        body, name=name,
        grid_spec=pltpu.PrefetchScalarGridSpec(
            num_scalar_prefetch=1, grid=(nt,),
            in_specs=[pl.BlockSpec((None, tr, cols), lambda i, cj: (cj[1], cj[0] * nt + i, 0)),
                      pl.BlockSpec((None, tr, cols), lambda i, cj: (cj[1], i, 0)),
                      pl.BlockSpec((N_SHARD, tr, cols), lambda i, cj: (0, i, 0))],
            out_specs=pl.BlockSpec((tr, cols), lambda i, cj: (cj[0] * nt + i, 0))),
        out_shape=jax.ShapeDtypeStruct((2 * half, cols), F32), compiler_params=_params(("parallel",)),
    )(cj, g, got, landed)


def _join_rider(shards):
    n = len(shards)

    def flights(bufs, sem):
        c = _me()[2]
        out = []
        for a in range(n):
            mine, theirs = (bufs[a].at[_half_rows(bufs[a].shape[0], cc)] for cc in (c, 1 - c))
            out.append((_remote(mine, mine, sem(a), sem(n + a), _peer(1)), _remote(theirs, theirs, sem(a), sem(n + a), _peer(1))))
        return out

    def start(ins, outs, sem):
        for send, _ in flights(outs, sem):
            send.start()

    def finish(ins, outs, sem):
        for send, arrival in flights(outs, sem):
            arrival.wait_recv()
            send.wait_send()

    return _Rider(shards, [jax.ShapeDtypeStruct(h.shape, F32) for h in shards], {a: a for a in range(n)}, 2 * n, start, finish)


class _Reduction:
    def __init__(self, names, grads, cj):
        self.names, self.grads, self.cj = names, list(grads), cj

    def swap(self):
        return _swap_rider(self.grads)

    def add(self, got):
        self.got = list(got)
        self.sums = [_add_halves("add_halves_" + nm, g, h, self.cj) for nm, g, h in zip(self.names, self.grads, self.got)]
        return _scatter_rider(self.sums)

    def total(self, landed):
        halves = [_sum_chips("sum_chips_" + nm, g, h, l, self.cj)
                  for nm, g, h, l in zip(self.names, self.grads, self.got, landed)]
        return _join_rider(halves)


def _adamw_math(w, g, m, v):
    m = ADAM_B1 * m + (1.0 - ADAM_B1) * g
    v = ADAM_B2 * v + (1.0 - ADAM_B2) * jnp.square(g)
    m_hat = m / (1.0 - ADAM_B1 ** ADAM_STEP)
    v_hat = v / (1.0 - ADAM_B2 ** ADAM_STEP)
    return -ADAM_LR * (m_hat / (jnp.sqrt(v_hat) + ADAM_EPS) + ADAM_WD * w), m, v


def _adamw(name, ws, gs, ms, vs, rider=None):
    n = len(ws)
    rows = ws[0].shape[0]
    tr = _row_tile(rows)

    def body(*refs):
        ins, outs = refs[:4 * n], refs[4 * n:]
        for a in range(n):
            w, g, m, v = (ins[k * n + a][...] for k in range(4))
            outs[a][...], outs[n + a][...], outs[2 * n + a][...] = _adamw_math(w, g, m, v)

    specs = [pl.BlockSpec((tr, w.shape[1]), lambda i: (i, 0)) for w in ws]
    res, riding = _hosted(
        body, rider, name=name, grid=(rows // tr,), in_specs=specs * 4, out_specs=specs * 3,
        out_shape=[jax.ShapeDtypeStruct(w.shape, F32) for w in ws] * 3, args=(*ws, *gs, *ms, *vs))
    return (res[:n], res[n:2 * n], res[2 * n:]), riding


def _adamw_small(ws, gs, ms, vs):
    n = len(ws)

    def body(*refs):
        ins, outs = refs[:4 * n], refs[4 * n:]
        for a in range(n):
            w, g, m, v = (ins[k * n + a][...] for k in range(4))
            outs[a][...], outs[n + a][...], outs[2 * n + a][...] = _adamw_math(w, g, m, v)

    vm = pl.BlockSpec(memory_space=pltpu.VMEM)
    res = _call(
        body, name="adamw_small", in_specs=[vm] * (4 * n), out_specs=[vm] * (3 * n),
        out_shape=[jax.ShapeDtypeStruct(w.shape, F32) for w in ws] * 3, compiler_params=_params(),
    )(*ws, *gs, *ms, *vs)
    return res[:n], res[n:2 * n], res[2 * n:]


def kernel(x, c, w_ada, b_ada, norm1_g, w_in, q_norm_a, k_norm_a, sink_b, w_branch, w_out, norm2_g, w_mlp_in, w_mlp_out, final_g, loss_target, m_w_ada, m_b_ada, m_norm1_g, m_w_in, m_q_norm_a, m_k_norm_a, m_sink_b, m_w_branch, m_w_out, m_norm2_g, m_w_mlp_in, m_w_mlp_out, m_final_g, v_w_ada, v_b_ada, v_norm1_g, v_w_in, v_q_norm_a, v_k_norm_a, v_sink_b, v_w_branch, v_w_out, v_norm2_g, v_w_mlp_in, v_w_mlp_out, v_final_g):
    xi, yi, ci = _me()
    cj = jnp.stack([ci, 2 * xi + yi]).astype(jnp.int32)
    n_cols = 6 * D_MODEL // N_SHARD

    def rows2d(a):
        return a.reshape(-1, a.shape[-1])

    big = (w_in, w_branch, w_out, w_mlp_in, w_mlp_out)
    stacked = _cast_weights([rows2d(w) for w in big])
    inv_a = ROPE_THETA ** (-jnp.arange(0, HEAD_DIM // 2, 2, dtype=F32) / (HEAD_DIM // 2))
    inv_b = ROPE_THETA ** (-jnp.arange(0, HEAD_DIM, 2, dtype=F32) / HEAD_DIM)
    inv = jnp.stack([jnp.tile(inv_a, LANES // inv_a.shape[0]), jnp.tile(inv_b, LANES // inv_b.shape[0])])
    (mod6, sc_all, *tabs), (w_in_s,) = _ada_fwd(c, w_ada[0], b_ada.reshape(N_SHARD, n_cols), inv, x.shape[1],
                                                rider=_gather_rider(stacked[:1]))
    rest = stacked[1:]

    gq2 = jnp.tile(q_norm_a, (1, 2))
    gk2 = jnp.tile(k_norm_a, (1, 2))
    grad_x, (w_in_red, join_out, g_mi, g_mo), accs = _local_step(
        x[0], loss_target[0], mod6, norm1_g, norm2_g, final_g.reshape(1, D_MODEL), gq2, gk2, sink_b[0], w_in_s, rest, cj, tabs)

    (g_w_ada, g_b_ada, g_n1, g_n2, g_f, g_q, g_k, g_s, loss_row), got_in = _ada_bwd(*accs, sc_all, rider=w_in_red.swap())
    loss = loss_row[0, 0]
    moments = dict(w_ada=(m_w_ada, v_w_ada), w_in=(m_w_in, v_w_in), w_branch=(m_w_branch, v_w_branch), w_out=(m_w_out, v_w_out),
                   w_mlp_in=(m_w_mlp_in, v_w_mlp_in), w_mlp_out=(m_w_mlp_out, v_w_mlp_out))
    weights = dict(w_ada=w_ada, w_in=w_in, w_branch=w_branch, w_out=w_out, w_mlp_in=w_mlp_in, w_mlp_out=w_mlp_out)

    def adamw(call, names, grads, rider=None):
        (d, m, v), riding = _adamw(call, [rows2d(weights[nm]) for nm in names], grads,
                                   [rows2d(moments[nm][0]) for nm in names], [rows2d(moments[nm][1]) for nm in names], rider)
        return {nm: (grads[k], d[k], m[k], v[k]) for k, nm in enumerate(names)}, riding

    big_res, landed_in = adamw("adamw_ada_mlp", ("w_ada", "w_mlp_in", "w_mlp_out"), [g_w_ada, g_mi, g_mo], w_in_red.add(got_in))
    g_in, g_out, g_br = _alone("join_in_out_branch", _riders(w_in_red.total(landed_in), join_out))
    big_res.update(adamw("adamw_in_branch", ("w_in", "w_branch"), [g_in, g_br])[0])

    small = ("b_ada", "norm1_g", "q_norm_a", "k_norm_a", "sink_b", "norm2_g", "final_g", "w_out")
    row = lambda a: a.reshape(1, -1)
    small_w = [row(a) for a in (b_ada, norm1_g, q_norm_a, k_norm_a, sink_b, norm2_g, final_g)] + [w_out[0]]
    small_g = [g_b_ada, g_n1, g_q, g_k, g_s, g_n2, g_f, g_out]
    small_m = [row(a) for a in (m_b_ada, m_norm1_g, m_q_norm_a, m_k_norm_a, m_sink_b, m_norm2_g, m_final_g)] + [m_w_out[0]]
    small_v = [row(a) for a in (v_b_ada, v_norm1_g, v_q_norm_a, v_k_norm_a, v_sink_b, v_norm2_g, v_final_g)] + [v_w_out[0]]
    s_d, s_m, s_v = _adamw_small(small_w, small_g, small_m, small_v)

    order = ("w_ada", "b_ada", "norm1_g", "w_in", "q_norm_a", "k_norm_a", "sink_b", "w_branch", "w_out", "norm2_g",
             "w_mlp_in", "w_mlp_out", "final_g")
    like = dict(w_ada=w_ada, b_ada=b_ada, norm1_g=norm1_g, w_in=w_in, q_norm_a=q_norm_a, k_norm_a=k_norm_a, sink_b=sink_b,
                w_branch=w_branch, w_out=w_out, norm2_g=norm2_g, w_mlp_in=w_mlp_in, w_mlp_out=w_mlp_out, final_g=final_g)
    grad, delta, new_m, new_v = {}, {}, {}, {}
    for nm, res in big_res.items():
        grad[nm], delta[nm], new_m[nm], new_v[nm] = res
    for k, nm in enumerate(small):
        grad[nm], delta[nm], new_m[nm], new_v[nm] = small_g[k], s_d[k], s_m[k], s_v[k]
    outs = [loss, grad_x[None]]
    for group in (grad, delta, new_m, new_v):
        outs += [group[nm].reshape(like[nm].shape) for nm in order]
    return tuple(outs)
```

```python
import jax
import jax.numpy as jnp
from jax import lax
from jax.experimental import pallas as pl
from jax.experimental.pallas import tpu as pltpu

F32 = jnp.float32
BF16 = jnp.bfloat16
MESH = pl.DeviceIdType.MESH
ANY = pl.BlockSpec(memory_space=pl.ANY)

D_MODEL = 1024
HEAD_DIM = 64
Q_HEADS = 8
KV_HEADS = 2
GROUP = Q_HEADS // KV_HEADS
BRANCH_W = Q_HEADS * HEAD_DIM
KV_W = KV_HEADS * HEAD_DIM
IN_W = 2 * (BRANCH_W + 2 * KV_W) + 2 * D_MODEL
QK_W = 2 * (BRANCH_W + 2 * KV_W)
D_FF = 4 * D_MODEL
GRID_W = 64
WINDOW = 128
ROPE_THETA = 10000.0
NORM_EPS = 1e-6
NEG_INF = -1e30
Q_SCALE = HEAD_DIM ** -0.5
N_SHARD = 4
N_DEV = 8
LANES = 128
VMEM_LIMIT = 56 * 1024 * 1024

ADAM_LR = 0.001
ADAM_B1 = 0.9
ADAM_B2 = 0.999
ADAM_EPS = 1e-08
ADAM_WD = 0.01
ADAM_STEP = 10

_call = pl.pallas_call


def _params(sem=None, vmem=VMEM_LIMIT):
    return pltpu.CompilerParams(dimension_semantics=sem, vmem_limit_bytes=vmem)


def _nt(a, b):
    return lax.dot_general(a, b, (((1,), (1,)), ((), ())), preferred_element_type=F32)


def _tn(a, b):
    return lax.dot_general(a, b, (((0,), (0,)), ((), ())), preferred_element_type=F32)


def _nn(a, b):
    return jnp.dot(a, b, preferred_element_type=F32)


def _sigmoid(z):
    return 0.5 * jnp.tanh(0.5 * z) + 0.5


def _rope_tables(s):
    t = jnp.arange(s, dtype=jnp.int32)
    lane = jnp.arange(LANES, dtype=jnp.int32)

    def cos_sin(pos, dim):
        inv = ROPE_THETA ** (-jnp.arange(0, dim, 2, dtype=F32) / dim)
        ang = pos.astype(F32)[:, None] * inv[None, :]
        return jnp.cos(ang), jnp.sin(ang)

    cr, sr = cos_sin(t // GRID_W, HEAD_DIM // 2)
    cc, sc = cos_sin(t % GRID_W, HEAD_DIM // 2)
    cos_a = jnp.tile(jnp.concatenate([cr, cr, cc, cc], axis=1), (1, 2))
    sin_a = jnp.tile(jnp.concatenate([sr, sr, sc, sc], axis=1), (1, 2))
    first_a = (lane % 32) < 16
    c1, s1 = cos_sin(t, HEAD_DIM)
    cos_b = jnp.tile(jnp.concatenate([c1, c1], axis=1), (1, 2))
    sin_b = jnp.tile(jnp.concatenate([s1, s1], axis=1), (1, 2))
    first_b = (lane % 64) < 32
    tabs_a = (cos_a, jnp.where(first_a, -sin_a, 0.0), jnp.where(first_a, 0.0, sin_a))
    tabs_b = (cos_b, jnp.where(first_b, -sin_b, 0.0), jnp.where(first_b, 0.0, sin_b))
    return tabs_a + tabs_b


def _rope(z, cos, s_lo, s_hi, half, sign=1.0):
    up = pltpu.roll(z, LANES - half, 1)
    dn = pltpu.roll(z, half, 1)
    return z * cos + sign * (up * s_lo + dn * s_hi)


def _head_mean(z2, bd):
    hi = z2.astype(BF16)
    lo = (z2 - hi.astype(F32)).astype(BF16)
    return _nn(hi, bd) + _nn(lo, bd)


def _block_diag():
    lane = jnp.arange(LANES)
    return jnp.where((lane[:, None] // HEAD_DIM) == (lane[None, :] // HEAD_DIM), 1.0 / HEAD_DIM, 0.0).astype(BF16)


def _row_spec(tm, width):
    return pl.BlockSpec((tm, width), lambda i: (i, 0))


def _heads_spec(heads, tm):
    return pl.BlockSpec((heads, tm, HEAD_DIM), lambda i: (0, i, 0))


def _full_spec(shape):
    nd = len(shape)
    return pl.BlockSpec(shape, lambda i: (0,) * nd)


def _in_proj(x, mod6, g1, w_in_s, gq, gk, bd, tabs, tm=512):
    s = x.shape[0]

    def body(x_ref, mod_ref, g1_ref, w_ref, gq_ref, gk_ref, bd_ref, ca, la, ha, cb, lb, hb,
             h_ref, qkraw_ref, qa_ref, ka_ref, va_ref, qb_ref, kb_ref, vb_ref, gate_ref):
        xt = x_ref[...]
        r = lax.rsqrt(jnp.mean(xt * xt, axis=-1, keepdims=True) + NORM_EPS)
        h = (xt * r * g1_ref[...]) * (1.0 + mod_ref[1:2, :]) + mod_ref[0:1, :]
        hb16 = h.astype(BF16)
        h_ref[...] = hb16
        proj = jnp.concatenate([_nn(hb16, w_ref[j]) for j in range(N_SHARD)], axis=1)
        qkraw_ref[...] = proj[:, :BRANCH_W + KV_W]
        bdm = bd_ref[...]
        tab_a = (ca[...], la[...], ha[...])
        tab_b = (cb[...], lb[...], hb[...])

        def norm_rope_a(z, gain):
            zn = z * lax.rsqrt(_head_mean(z * z, bdm) + NORM_EPS) * gain
            return _rope(zn, *tab_a, 16)

        def put(ref, first, z):
            zb = z.astype(BF16)
            ref[first] = zb[:, :HEAD_DIM]
            ref[first + 1] = zb[:, HEAD_DIM:]

        for i in range(Q_HEADS // 2):
            put(qa_ref, 2 * i, norm_rope_a(proj[:, LANES * i:LANES * (i + 1)], gq_ref[...]) * Q_SCALE)
        off = BRANCH_W
        put(ka_ref, 0, norm_rope_a(proj[:, off:off + LANES], gk_ref[...]))
        off += KV_W
        def put_v(ref, z):
            zb = z.astype(BF16)
            for hd in range(KV_HEADS):
                ref[hd, :, :HEAD_DIM] = zb[:, HEAD_DIM * hd:HEAD_DIM * (hd + 1)]
                ref[hd, :, HEAD_DIM:] = jnp.ones((tm, HEAD_DIM), BF16)

        put_v(va_ref, proj[:, off:off + LANES])
        off += KV_W
        for i in range(Q_HEADS // 2):
            put(qb_ref, 2 * i, _rope(proj[:, off + LANES * i:off + LANES * (i + 1)], *tab_b, 32) * Q_SCALE)
        off += BRANCH_W
        put(kb_ref, 0, _rope(proj[:, off:off + LANES], *tab_b, 32))
        off += KV_W
        put_v(vb_ref, proj[:, off:off + LANES])
        gate_ref[...] = proj[:, QK_W:].astype(BF16)

    tab_spec = _row_spec(tm, LANES)
    return _call(
        body, name="in_proj", grid=(s // tm,),
        in_specs=[_row_spec(tm, D_MODEL), _full_spec(mod6.shape), _full_spec(g1.shape), _full_spec(w_in_s.shape),
                  _full_spec(gq.shape), _full_spec(gk.shape), _full_spec(bd.shape)] + [tab_spec] * 6,
        out_specs=[_row_spec(tm, D_MODEL), _row_spec(tm, BRANCH_W + KV_W), _heads_spec(Q_HEADS, tm), _heads_spec(KV_HEADS, tm),
                   pl.BlockSpec((KV_HEADS, tm, LANES), lambda i: (0, i, 0)), _heads_spec(Q_HEADS, tm),
                   _heads_spec(KV_HEADS, tm), pl.BlockSpec((KV_HEADS, tm, LANES), lambda i: (0, i, 0)),
                   _row_spec(tm, 2 * D_MODEL)],
        out_shape=[jax.ShapeDtypeStruct((s, D_MODEL), BF16), jax.ShapeDtypeStruct((s, BRANCH_W + KV_W), F32),
                   jax.ShapeDtypeStruct((Q_HEADS, s, HEAD_DIM), BF16), jax.ShapeDtypeStruct((KV_HEADS, s, HEAD_DIM), BF16),
                   jax.ShapeDtypeStruct((KV_HEADS, s, LANES), BF16), jax.ShapeDtypeStruct((Q_HEADS, s, HEAD_DIM), BF16),
                   jax.ShapeDtypeStruct((KV_HEADS, s, HEAD_DIM), BF16), jax.ShapeDtypeStruct((KV_HEADS, s, LANES), BF16),
                   jax.ShapeDtypeStruct((s, 2 * D_MODEL), BF16)],
        compiler_params=_params(("parallel",)),
    )(x, mod6, g1, w_in_s, gq, gk, bd, *tabs)


def _group_specs(s, tq):
    q_spec = pl.BlockSpec((None, GROUP, tq, HEAD_DIM), lambda g, i: (g, 0, i, 0))
    kv_spec = pl.BlockSpec((None, s, HEAD_DIM), lambda g, i: (g, 0, 0))
    col_spec = pl.BlockSpec((None, GROUP, tq, 1), lambda g, i: (g, 0, i, 0))
    return q_spec, kv_spec, col_spec


def _attn_a_fwd(q, k, v1, rider=None, tq=256, tk=2048):
    s = q.shape[1]
    tk = min(tk, s // 2)
    rows = GROUP * tq

    n = s // tk
    assert n >= 2 and n % 2 == 0

    def body(q_ref, k_ref, v_ref, o_ref, oh_ref, lse_ref, s0_ref, s1_ref, p0_ref, p1_ref, m_ref, a_ref, acc_ref):
        s_ref, p_ref = (s0_ref, s1_ref), (p0_ref, p1_ref)
        qq = q_ref[...].reshape(rows, HEAD_DIM)
        m_ref[...] = jnp.full((rows, 1), NEG_INF, F32)
        acc_ref[...] = jnp.zeros((rows, LANES), F32)

        def keys(i):
            return pl.ds(pl.multiple_of(i * tk, tk), tk)

        def scores(i, slot):
            s_ref[slot][...] = _nt(qq, k_ref[keys(i), :])

        def softmax(slot):
            sc = s_ref[slot][...]
            m = m_ref[...]
            mn = jnp.maximum(m, jnp.max(sc, axis=-1, keepdims=True))
            m_ref[...] = mn
            a_ref[...] = jnp.exp(m - mn)
            p_ref[slot][...] = jnp.exp(sc - mn).astype(BF16)

        def weigh(i, slot):
            acc_ref[...] = a_ref[...] * acc_ref[...] + _nn(p_ref[slot][...], v_ref[keys(i), :])

        scores(0, 0)
        softmax(0)
        scores(1, 1)

        def two_steps(j, carry):
            i = 2 * j + 1
            weigh(i - 1, 0)
            softmax(1)
            scores(i + 1, 0)
            weigh(i, 1)
            softmax(0)
            scores(i + 2, 1)
            return carry

        lax.fori_loop(0, (n - 2) // 2, two_steps, 0, unroll=True)
        weigh(n - 2, 0)
        softmax(1)
        weigh(n - 1, 1)
        l = acc_ref[:, HEAD_DIM:HEAD_DIM + 1]
        o = (acc_ref[:, :HEAD_DIM] / l).astype(BF16)
        for g in range(GROUP):
            o_ref[:, HEAD_DIM * g:HEAD_DIM * (g + 1)] = o[tq * g:tq * (g + 1)]
        oh_ref[...] = o.reshape(GROUP, tq, HEAD_DIM)
        lse_ref[...] = (m_ref[...] + jnp.log(l)).reshape(GROUP, tq, 1)

    q_spec, kv_spec, col_spec = _group_specs(s, tq)
    v_spec = pl.BlockSpec((None, s, LANES), lambda g, i: (g, 0, 0))
    return _hosted(
        body, rider, name="attn_a_fwd", grid=(KV_HEADS, s // tq),
        in_specs=[q_spec, kv_spec, v_spec],
        out_specs=[pl.BlockSpec((tq, GROUP * HEAD_DIM), lambda g, i: (i, g)), q_spec, col_spec],
        out_shape=[jax.ShapeDtypeStruct((s, BRANCH_W), BF16), jax.ShapeDtypeStruct((KV_HEADS, GROUP, s, HEAD_DIM), BF16),
                   jax.ShapeDtypeStruct((KV_HEADS, GROUP, s, 1), F32)],
        scratch_shapes=[pltpu.VMEM((rows, tk), F32), pltpu.VMEM((rows, tk), F32), pltpu.VMEM((rows, tk), BF16),
                        pltpu.VMEM((rows, tk), BF16), pltpu.VMEM((rows, 1), F32), pltpu.VMEM((rows, 1), F32),
                        pltpu.VMEM((rows, LANES), F32)],
        args=(q.reshape(KV_HEADS, GROUP, s, HEAD_DIM), k, v1), middle_at=KV_HEADS * (s // tq) // 2)


def _attn_a_bwd(q, k, v1, o, do, lse, rider=None, tq=256, tk=512):
    s = q.shape[1]
    tk = min(tk, s // 2)
    rows = GROUP * tq

    n = s // tk
    assert n >= 2 and n % 2 == 0

    def body(q_ref, k_ref, v_ref, o_ref, do_ref, lse_ref, dq_ref, dk_ref, dv_ref,
             s0_ref, s1_ref, dp0_ref, dp1_ref, p0_ref, p1_ref, ds0_ref, ds1_ref, dq_acc):
        s_ref, dp_ref, p_ref, ds_ref = (s0_ref, s1_ref), (dp0_ref, dp1_ref), (p0_ref, p1_ref), (ds0_ref, ds1_ref)

        @pl.when(pl.program_id(1) == 0)
        def _():
            dk_ref[...] = jnp.zeros_like(dk_ref)
            dv_ref[...] = jnp.zeros_like(dv_ref)

        qq = q_ref[...].reshape(rows, HEAD_DIM)
        dd = do_ref[...].reshape(rows, HEAD_DIM)
        ls = lse_ref[...].reshape(rows, 1)
        dl = jnp.sum(dd.astype(F32) * o_ref[...].reshape(rows, HEAD_DIM).astype(F32), axis=-1, keepdims=True)
        dq_acc[...] = jnp.zeros((rows, HEAD_DIM), F32)

        def keys(i):
            return pl.ds(pl.multiple_of(i * tk, tk), tk)

        def scores(i, slot):
            s_ref[slot][...] = _nt(qq, k_ref[keys(i), :])
            dp_ref[slot][...] = _nt(dd, v_ref[keys(i), :HEAD_DIM])

        def weights(slot):
            p = jnp.exp(s_ref[slot][...] - ls)
            p_ref[slot][...] = p.astype(BF16)
            ds_ref[slot][...] = (p * (dp_ref[slot][...] - dl)).astype(BF16)

        def grads(i, slot):
            dv_ref[keys(i), :] += _tn(p_ref[slot][...], dd)
            dk_ref[keys(i), :] += _tn(ds_ref[slot][...], qq)
            dq_acc[...] += _nn(ds_ref[slot][...], k_ref[keys(i), :])

        scores(0, 0)
        weights(0)
        scores(1, 1)

        def two_steps(j, carry):
            i = 2 * j + 1
            grads(i - 1, 0)
            weights(1)
            scores(i + 1, 0)
            grads(i, 1)
            weights(0)
            scores(i + 2, 1)
            return carry

        lax.fori_loop(0, (n - 2) // 2, two_steps, 0, unroll=True)
        grads(n - 2, 0)
        weights(1)
        grads(n - 1, 1)
        dq_ref[...] = dq_acc[...].astype(BF16).reshape(GROUP, tq, HEAD_DIM)

    q_spec, kv_spec, col_spec = _group_specs(s, tq)
    v_spec = pl.BlockSpec((None, s, LANES), lambda g, i: (g, 0, 0))
    shape4 = (KV_HEADS, GROUP, s, HEAD_DIM)
    tile32, tile16 = pltpu.VMEM((rows, tk), F32), pltpu.VMEM((rows, tk), BF16)
    return _hosted(
        body, rider, name="attn_a_bwd", grid=(KV_HEADS, s // tq),
        in_specs=[q_spec, kv_spec, v_spec, q_spec, q_spec, col_spec],
        out_specs=[q_spec, kv_spec, kv_spec],
        out_shape=[jax.ShapeDtypeStruct(shape4, BF16), jax.ShapeDtypeStruct((KV_HEADS, s, HEAD_DIM), F32),
                   jax.ShapeDtypeStruct((KV_HEADS, s, HEAD_DIM), F32)],
        scratch_shapes=[tile32] * 4 + [tile16] * 4 + [pltpu.VMEM((rows, HEAD_DIM), F32)],
        args=(q.reshape(shape4), k, v1, o.reshape(shape4), do.reshape(shape4), lse))


TQ_B = WINDOW


def _win_keys(tq):
    return tq + 2 * WINDOW


def _window_bias(tq):
    r = jnp.arange(tq, dtype=jnp.int32)[:, None]
    col = jnp.arange(_win_keys(tq), dtype=jnp.int32)[None, :]
    return jnp.stack([jnp.where(jnp.abs(r - col + WINDOW * b) <= WINDOW, 0.0, NEG_INF) for b in range(3)]).astype(F32)


def _band(tq, s):
    win = _win_keys(tq)

    def window(e):
        return pl.ds(pl.multiple_of(jnp.clip(e * tq - WINDOW, 0, s - win), WINDOW), win)

    def bias_index(e):
        return jnp.where(e == 0, 0, jnp.where(e >= s // tq - 1, 2, 1))

    return window, bias_index


def _pair_specs(s, tq):
    pairs = s // (2 * tq)
    cur = lambda g, j: (g, 0, jnp.minimum(j, pairs - 1), 0)
    prev = lambda g, j: (g, 0, jnp.maximum(j - 1, 0), 0)
    tile = lambda width, index: pl.BlockSpec((None, GROUP, 2 * tq, width), index)
    kv_spec = pl.BlockSpec((None, s, HEAD_DIM), lambda g, j: (g, 0, 0))
    v_spec = pl.BlockSpec((None, s, LANES), lambda g, j: (g, 0, 0))
    sink_spec = pl.BlockSpec((None, GROUP * tq, 1), lambda g, j: (g, 0, 0))
    bias_spec = pl.BlockSpec((3, tq, _win_keys(tq)), lambda g, j: (0, 0, 0))
    return tile, cur, prev, kv_spec, v_spec, sink_spec, bias_spec


def _attn_b_fwd(q, k, v1, sink_col, bias, rider=None, tq=TQ_B):
    s = q.shape[1]
    rows = GROUP * tq
    win = _win_keys(tq)
    pairs = s // (2 * tq)
    window, bias_index = _band(tq, s)

    def body(q_ref, k_ref, v_ref, sink_ref, bias_ref, o_ref, oh_ref, lse_ref, s0_ref, s1_ref, p0_ref, p1_ref, m0_ref, m1_ref):
        s_ref, p_ref, m_ref = (s0_ref, s1_ref), (p0_ref, p1_ref), (m0_ref, m1_ref)
        j = pl.program_id(1)

        @pl.when(j == 0)
        def _():
            for ref in (s0_ref, s1_ref, p0_ref, p1_ref, m0_ref, m1_ref):
                ref[...] = jnp.zeros_like(ref)

        def scores(e, slot):
            qq = q_ref[:, pl.ds(slot * tq, tq), :].reshape(rows, HEAD_DIM)
            sc = _nt(qq, k_ref[window(e), :]).reshape(GROUP, tq, win) + bias_ref[bias_index(e)][None]
            s_ref[slot][...] = sc.reshape(rows, win)

        def softmax(slot):
            sc = s_ref[slot][...]
            m = jnp.maximum(jnp.max(sc, axis=-1, keepdims=True), sink_ref[...])
            m_ref[slot][...] = m
            p_ref[slot][...] = jnp.exp(sc - m).astype(BF16)

        def finish(e, slot):
            acc = _nn(p_ref[slot][...], v_ref[window(e), :])
            m = m_ref[slot][...]
            l = acc[:, HEAD_DIM:HEAD_DIM + 1] + jnp.exp(sink_ref[...] - m)
            o = (acc[:, :HEAD_DIM] / l).astype(BF16)
            at = pl.ds(slot * tq, tq)
            for g in range(GROUP):
                o_ref[at, HEAD_DIM * g:HEAD_DIM * (g + 1)] = o[tq * g:tq * (g + 1)]
            oh_ref[:, at, :] = o.reshape(GROUP, tq, HEAD_DIM)
            lse_ref[:, at, :] = (m + jnp.log(l)).reshape(GROUP, tq, 1)

        first = 2 * j
        finish(jnp.maximum(first - 2, 0), 0)
        softmax(1)
        scores(first, 0)
        finish(jnp.maximum(first - 1, 0), 1)
        softmax(0)
        scores(first + 1, 1)

    tile, cur, prev, kv_spec, v_spec, sink_spec, bias_spec = _pair_specs(s, tq)
    tile32, tile16, col = pltpu.VMEM((rows, win), F32), pltpu.VMEM((rows, win), BF16), pltpu.VMEM((rows, 1), F32)
    return _hosted(
        body, rider, name="attn_b_fwd", grid=(KV_HEADS, pairs + 1),
        in_specs=[tile(HEAD_DIM, cur), kv_spec, v_spec, sink_spec, bias_spec],
        out_specs=[pl.BlockSpec((2 * tq, GROUP * HEAD_DIM), lambda g, j: (jnp.maximum(j - 1, 0), g)),
                   tile(HEAD_DIM, prev), tile(1, prev)],
        out_shape=[jax.ShapeDtypeStruct((s, BRANCH_W), BF16), jax.ShapeDtypeStruct((KV_HEADS, GROUP, s, HEAD_DIM), BF16),
                   jax.ShapeDtypeStruct((KV_HEADS, GROUP, s, 1), F32)],
        scratch_shapes=[tile32, tile32, tile16, tile16, col, col],
        args=(q.reshape(KV_HEADS, GROUP, s, HEAD_DIM), k, v1, sink_col, bias))


def _attn_b_bwd(q, k, v1, o, do, lse, sink_col, bias, rider=None, tq=TQ_B):
    s = q.shape[1]
    rows = GROUP * tq
    win = _win_keys(tq)
    pairs = s // (2 * tq)
    window, bias_index = _band(tq, s)

    def body(q_ref, k_ref, v_ref, o_ref, do_ref, lse_ref, sink_ref, bias_ref, dq_ref, dk_ref, dv_ref, dsink_ref,
             s0, s1, dp0, dp1, p0, p1, ds0, ds1, q0, q1, d0, d1, ls0, ls1, dl0, dl1):
        s_ref, dp_ref, p_ref, ds_ref = (s0, s1), (dp0, dp1), (p0, p1), (ds0, ds1)
        q_keep, do_keep, lse_keep, delta_keep = (q0, q1), (d0, d1), (ls0, ls1), (dl0, dl1)
        j = pl.program_id(1)

        @pl.when(j == 0)
        def _():
            for ref in (dk_ref, dv_ref, dsink_ref, s0, s1, dp0, dp1, p0, p1, ds0, ds1, q0, q1, d0, d1, ls0, ls1, dl0, dl1):
                ref[...] = jnp.zeros_like(ref)

        def scores(e, slot):
            at = pl.ds(slot * tq, tq)
            qq = q_ref[:, at, :].reshape(rows, HEAD_DIM)
            dd = do_ref[:, at, :].reshape(rows, HEAD_DIM)
            q_keep[slot][...] = qq
            do_keep[slot][...] = dd
            lse_keep[slot][...] = lse_ref[:, at, :].reshape(rows, 1)
            delta_keep[slot][...] = jnp.sum(dd.astype(F32) * o_ref[:, at, :].reshape(rows, HEAD_DIM).astype(F32), axis=-1,
                                            keepdims=True)
            sc = _nt(qq, k_ref[window(e), :]).reshape(GROUP, tq, win) + bias_ref[bias_index(e)][None]
            s_ref[slot][...] = sc.reshape(rows, win)
            dp_ref[slot][...] = _nt(dd, v_ref[window(e), :HEAD_DIM])

        def weights(slot):
            p = jnp.exp(s_ref[slot][...] - lse_keep[slot][...])
            p_ref[slot][...] = p.astype(BF16)
            ds_ref[slot][...] = (p * (dp_ref[slot][...] - delta_keep[slot][...])).astype(BF16)

        def grads(e, slot, live):
            at = window(e)
            ds = ds_ref[slot][...]
            dv_ref[at, :] += _tn(p_ref[slot][...], do_keep[slot][...])
            dk_ref[at, :] += _tn(ds, q_keep[slot][...])
            dq_ref[:, pl.ds(slot * tq, tq), :] = _nn(ds, k_ref[at, :]).astype(BF16).reshape(GROUP, tq, HEAD_DIM)
            dsk = jnp.exp(sink_ref[...] - lse_keep[slot][...]) * delta_keep[slot][...] * live
            for g in range(GROUP):
                dsink_ref[g:g + 1, :] -= jnp.broadcast_to(jnp.sum(dsk[tq * g:tq * (g + 1)], axis=0, keepdims=True), (1, LANES))

        first = 2 * j
        live = jnp.where(j > 0, 1.0, 0.0)
        grads(jnp.maximum(first - 2, 0), 0, live)
        weights(1)
        scores(first, 0)
        grads(jnp.maximum(first - 1, 0), 1, live)
        weights(0)
        scores(first + 1, 1)

    tile, cur, prev, kv_spec, v_spec, sink_spec, bias_spec = _pair_specs(s, tq)
    dsink_spec = pl.BlockSpec((None, ACC_ROWS, LANES), lambda g, j: (g, 0, 0))
    shape4 = (KV_HEADS, GROUP, s, HEAD_DIM)
    tile32, tile16 = pltpu.VMEM((rows, win), F32), pltpu.VMEM((rows, win), BF16)
    keep, col = pltpu.VMEM((rows, HEAD_DIM), BF16), pltpu.VMEM((rows, 1), F32)
    return _hosted(
        body, rider, name="attn_b_bwd", grid=(KV_HEADS, pairs + 1),
        in_specs=[tile(HEAD_DIM, cur), kv_spec, v_spec, tile(HEAD_DIM, cur), tile(HEAD_DIM, cur), tile(1, cur), sink_spec,
                  bias_spec],
        out_specs=[tile(HEAD_DIM, prev), kv_spec, kv_spec, dsink_spec],
        out_shape=[jax.ShapeDtypeStruct(shape4, BF16), jax.ShapeDtypeStruct((KV_HEADS, s, HEAD_DIM), F32),
                   jax.ShapeDtypeStruct((KV_HEADS, s, HEAD_DIM), F32), jax.ShapeDtypeStruct((KV_HEADS, ACC_ROWS, LANES), F32)],
        scratch_shapes=[tile32] * 4 + [tile16] * 4 + [keep] * 4 + [col] * 4,
        args=(q.reshape(shape4), k, v1, o.reshape(shape4), do.reshape(shape4), lse, sink_col, bias))


def _post_attn(ya, yb, gates, x, mod6, wbr_s, w_out, tm=512):
    s = x.shape[0]

    def body(ya_ref, yb_ref, g_ref, x_ref, mod_ref, wbr_ref, wo_ref, ua_ref, ub_ref, mg_ref, o_ref, x1_ref):
        ya_t, yb_t = ya_ref[...], yb_ref[...]
        ua = jnp.concatenate([_nn(ya_t, wbr_ref[j, 0]) for j in range(N_SHARD)], axis=1)
        ub = jnp.concatenate([_nn(yb_t, wbr_ref[j, 1]) for j in range(N_SHARD)], axis=1)
        ga, gb = g_ref[:, :D_MODEL].astype(F32), g_ref[:, D_MODEL:].astype(F32)
        merged = (_sigmoid(ga) * ua + _sigmoid(gb) * ub).astype(BF16)
        o = _nn(merged, wo_ref[...])
        ua_ref[...] = ua.astype(BF16)
        ub_ref[...] = ub.astype(BF16)
        mg_ref[...] = merged
        o_ref[...] = o.astype(BF16)
        x1_ref[...] = x_ref[...] + mod_ref[2:3, :] * o

    bf = jax.ShapeDtypeStruct((s, D_MODEL), BF16)
    return _call(
        body, name="post_attn", grid=(s // tm,),
        in_specs=[_row_spec(tm, BRANCH_W), _row_spec(tm, BRANCH_W), _row_spec(tm, 2 * D_MODEL), _row_spec(tm, D_MODEL),
                  _full_spec(mod6.shape), _full_spec(wbr_s.shape), _full_spec(w_out.shape)],
        out_specs=[_row_spec(tm, D_MODEL)] * 5,
        out_shape=[bf, bf, bf, bf, jax.ShapeDtypeStruct((s, D_MODEL), F32)],
        compiler_params=_params(("parallel",)),
    )(ya, yb, gates, x, mod6, wbr_s, w_out)


def _mlp_in(x1, mod6, g2, w_mi_s, tm=512):
    s = x1.shape[0]

    def body(x_ref, mod_ref, g_ref, w_ref, h2_ref, a_ref, hid_ref):
        xt = x_ref[...]
        r = lax.rsqrt(jnp.mean(xt * xt, axis=-1, keepdims=True) + NORM_EPS)
        h2 = ((xt * r * g_ref[...]) * (1.0 + mod_ref[4:5, :]) + mod_ref[3:4, :]).astype(BF16)
        h2_ref[...] = h2
        a = jnp.concatenate([_nn(h2, w_ref[j]) for j in range(N_SHARD)], axis=1)
        a_ref[...] = a.astype(BF16)
        hid_ref[...] = jnp.square(jnp.maximum(a, 0.0)).astype(BF16)

    return _call(
        body, name="mlp_in", grid=(s // tm,),
        in_specs=[_row_spec(tm, D_MODEL), _full_spec(mod6.shape), _full_spec(g2.shape), _full_spec(w_mi_s.shape)],
        out_specs=[_row_spec(tm, D_MODEL), _row_spec(tm, D_FF), _row_spec(tm, D_FF)],
        out_shape=[jax.ShapeDtypeStruct((s, D_MODEL), BF16), jax.ShapeDtypeStruct((s, D_FF), BF16),
                   jax.ShapeDtypeStruct((s, D_FF), BF16)],
        compiler_params=_params(("parallel",)),
    )(x1, mod6, g2, w_mi_s)


ACC_ROWS = 8


def _acc_spec():
    return pl.BlockSpec((ACC_ROWS, D_MODEL), lambda i: (0, 0))


def _acc_add(acc_ref, rows):
    @pl.when(pl.program_id(0) == 0)
    def _():
        acc_ref[...] = jnp.zeros_like(acc_ref)

    for r, val in enumerate(rows):
        acc_ref[r:r + 1, :] += jnp.sum(val, axis=0, keepdims=True)


def _mlp_out_loss(hid, x1, a, target, mod6, gf, w_mo, tm=512):
    s = x1.shape[0]

    def body(hid_ref, x_ref, a_ref, t_ref, mod_ref, gf_ref, w_ref, dx2_ref, dm_ref, da_ref, acc_ref):
        m = _nn(hid_ref[...], w_ref[...])
        gate2 = mod_ref[5:6, :]
        x2 = x_ref[...] + gate2 * m
        r = lax.rsqrt(jnp.mean(x2 * x2, axis=-1, keepdims=True) + NORM_EPS)
        xn = x2 * r
        err = xn * gf_ref[...] - t_ref[...]
        dy = err * (1.0 / D_MODEL)
        dxn = dy * gf_ref[...]
        dx2 = r * (dxn - xn * jnp.mean(dxn * xn, axis=-1, keepdims=True))
        dx2_ref[...] = dx2
        dm = (dx2 * gate2).astype(BF16)
        dm_ref[...] = dm
        da_ref[...] = (_nt(dm, w_ref[...]) * (2.0 * jnp.maximum(a_ref[...].astype(F32), 0.0))).astype(BF16)
        _acc_add(acc_ref, [err * err, dy * xn, dx2 * m])

    return _call(
        body, name="mlp_out_loss", grid=(s // tm,),
        in_specs=[_row_spec(tm, D_FF), _row_spec(tm, D_MODEL), _row_spec(tm, D_FF), _row_spec(tm, D_MODEL),
                  _full_spec(mod6.shape), _full_spec(gf.shape), _full_spec(w_mo.shape)],
        out_specs=[_row_spec(tm, D_MODEL), _row_spec(tm, D_MODEL), _row_spec(tm, D_FF), _acc_spec()],
        out_shape=[jax.ShapeDtypeStruct((s, D_MODEL), F32), jax.ShapeDtypeStruct((s, D_MODEL), BF16),
                   jax.ShapeDtypeStruct((s, D_FF), BF16), jax.ShapeDtypeStruct((ACC_ROWS, D_MODEL), F32)],
        compiler_params=_params(("arbitrary",)),
    )(hid, x1, a, target, mod6, gf, w_mo)


def _norm_bwd(dh, xt, gain, scale):
    r = lax.rsqrt(jnp.mean(xt * xt, axis=-1, keepdims=True) + NORM_EPS)
    xn = xt * r
    dxn = dh * (gain * (1.0 + scale))
    dx = r * (dxn - xn * jnp.mean(dxn * xn, axis=-1, keepdims=True))
    return dx, [dh, dh * xn * gain, dh * xn * (1.0 + scale)]


def _mlp_bwd(da, x1, dx2, o, mod6, g2, w_mi_s, rider=None, tm=512):
    s = x1.shape[0]

    def body(da_ref, x_ref, dx2_ref, o_ref, mod_ref, g_ref, w_ref, dx1_ref, do_ref, acc_ref):
        dh2 = _nt(da_ref[:, :D_MODEL], w_ref[0])
        for j in range(1, N_SHARD):
            dh2 += _nt(da_ref[:, D_MODEL * j:D_MODEL * (j + 1)], w_ref[j])
        dx, sums = _norm_bwd(dh2, x_ref[...], g_ref[...], mod_ref[4:5, :])
        dx1 = dx2_ref[...] + dx
        dx1_ref[...] = dx1
        do_ref[...] = (dx1 * mod_ref[2:3, :]).astype(BF16)
        _acc_add(acc_ref, sums + [dx1 * o_ref[...].astype(F32)])

    return _hosted(
        body, rider, name="mlp_bwd", grid=(s // tm,),
        in_specs=[_row_spec(tm, D_FF), _row_spec(tm, D_MODEL), _row_spec(tm, D_MODEL), _row_spec(tm, D_MODEL),
                  _full_spec(mod6.shape), _full_spec(g2.shape), _full_spec(w_mi_s.shape)],
        out_specs=[_row_spec(tm, D_MODEL), _row_spec(tm, D_MODEL), _acc_spec()],
        out_shape=[jax.ShapeDtypeStruct((s, D_MODEL), F32), jax.ShapeDtypeStruct((s, D_MODEL), BF16),
                   jax.ShapeDtypeStruct((ACC_ROWS, D_MODEL), F32)],
        args=(da, x1, dx2, o, mod6, g2, w_mi_s))


def _merge_bwd(do, gates, ua, ub, w_out, wbr_s, rider=None, tm=512):
    s = do.shape[0]

    def body(do_ref, g_ref, ua_ref, ub_ref, wo_ref, wbr_ref, dua_ref, dub_ref, dg_ref, doa_ref, dob_ref):
        dmerged = _nt(do_ref[...], wo_ref[...])
        for b, (u_ref, du_ref, dy_ref) in enumerate(((ua_ref, dua_ref, doa_ref), (ub_ref, dub_ref, dob_ref))):
            sg = _sigmoid(g_ref[:, D_MODEL * b:D_MODEL * (b + 1)].astype(F32))
            du = (dmerged * sg).astype(BF16)
            du_ref[...] = du
            dg_ref[:, D_MODEL * b:D_MODEL * (b + 1)] = (dmerged * u_ref[...].astype(F32) * sg * (1.0 - sg)).astype(BF16)
            w = BRANCH_W // 2
            dy = _nt(du[:, :w], wbr_ref[0, b])
            for j in range(1, N_SHARD):
                dy += _nt(du[:, w * j:w * (j + 1)], wbr_ref[j, b])
            dyb = dy.astype(BF16)
            for h in range(Q_HEADS):
                dy_ref[h] = dyb[:, HEAD_DIM * h:HEAD_DIM * (h + 1)]

    bf = jax.ShapeDtypeStruct((s, D_MODEL), BF16)
    heads = jax.ShapeDtypeStruct((Q_HEADS, s, HEAD_DIM), BF16)
    return _hosted(
        body, rider, name="merge_bwd", grid=(s // tm,),
        in_specs=[_row_spec(tm, D_MODEL), _row_spec(tm, 2 * D_MODEL), _row_spec(tm, D_MODEL), _row_spec(tm, D_MODEL),
                  _full_spec(w_out.shape), _full_spec(wbr_s.shape)],
        out_specs=[_row_spec(tm, D_MODEL), _row_spec(tm, D_MODEL), _row_spec(tm, 2 * D_MODEL),
                   _heads_spec(Q_HEADS, tm), _heads_spec(Q_HEADS, tm)],
        out_shape=[bf, bf, jax.ShapeDtypeStruct((s, 2 * D_MODEL), BF16), heads, heads],
        args=(do, gates, ua, ub, w_out, wbr_s))


def _qk_bwd(dqa, dka, dva, dqb, dkb, dvb, qkraw, dgates, gq, gk, bd, tabs, rider=None, tm=512):
    s = qkraw.shape[0]

    def body(dqa_ref, dka_ref, dva_ref, dqb_ref, dkb_ref, dvb_ref, raw_ref, dg_ref, gq_ref, gk_ref, bd_ref,
             ca, la, ha, cb, lb, hb, dp_ref, acc_ref, pair_ref):
        bdm = bd_ref[...]
        tab_a = (ca[...], la[...], ha[...])
        tab_b = (cb[...], lb[...], hb[...])

        def pair(ref, first):
            pair_ref[:, :HEAD_DIM] = ref[first].astype(F32)
            pair_ref[:, HEAD_DIM:] = ref[first + 1].astype(F32)
            return pair_ref[...]

        def norm_rope_a_bwd(dz, raw, gain):
            dzn = _rope(dz, *tab_a, 16, sign=-1.0)
            rinv = lax.rsqrt(_head_mean(raw * raw, bdm) + NORM_EPS)
            zhat = raw * rinv
            dzhat = dzn * gain
            return rinv * (dzhat - zhat * _head_mean(dzhat * zhat, bdm)), dzn * zhat

        gq_rows = jnp.zeros((tm, LANES), F32)
        for i in range(Q_HEADS // 2):
            at = slice(LANES * i, LANES * (i + 1))
            draw, gsum = norm_rope_a_bwd(pair(dqa_ref, 2 * i) * Q_SCALE, raw_ref[:, at], gq_ref[...])
            dp_ref[:, at] = draw.astype(BF16)
            gq_rows += gsum
        off = BRANCH_W
        draw, gk_rows = norm_rope_a_bwd(pair(dka_ref, 0), raw_ref[:, off:off + LANES], gk_ref[...])
        dp_ref[:, off:off + LANES] = draw.astype(BF16)
        off += KV_W
        dp_ref[:, off:off + LANES] = pair(dva_ref, 0).astype(BF16)
        off += KV_W
        for i in range(Q_HEADS // 2):
            dz = _rope(pair(dqb_ref, 2 * i) * Q_SCALE, *tab_b, 32, sign=-1.0)
            dp_ref[:, off + LANES * i:off + LANES * (i + 1)] = dz.astype(BF16)
        off += BRANCH_W
        dp_ref[:, off:off + LANES] = _rope(pair(dkb_ref, 0), *tab_b, 32, sign=-1.0).astype(BF16)
        off += KV_W
        dp_ref[:, off:off + LANES] = pair(dvb_ref, 0).astype(BF16)
        dp_ref[:, QK_W:] = dg_ref[...]

        @pl.when(pl.program_id(0) == 0)
        def _():
            acc_ref[...] = jnp.zeros_like(acc_ref)

        acc_ref[0:1, :] += jnp.sum(gq_rows, axis=0, keepdims=True)
        acc_ref[1:2, :] += jnp.sum(gk_rows, axis=0, keepdims=True)

    tab_spec = _row_spec(tm, LANES)
    return _hosted(
        body, rider, name="qk_bwd", grid=(s // tm,),
        in_specs=[_heads_spec(Q_HEADS, tm), _heads_spec(KV_HEADS, tm), _heads_spec(KV_HEADS, tm),
                  _heads_spec(Q_HEADS, tm), _heads_spec(KV_HEADS, tm), _heads_spec(KV_HEADS, tm),
                  _row_spec(tm, BRANCH_W + KV_W), _row_spec(tm, 2 * D_MODEL),
                  _full_spec(gq.shape), _full_spec(gk.shape), _full_spec(bd.shape)] + [tab_spec] * 6,
        out_specs=[_row_spec(tm, IN_W), pl.BlockSpec((ACC_ROWS, LANES), lambda i: (0, 0))],
        out_shape=[jax.ShapeDtypeStruct((s, IN_W), BF16), jax.ShapeDtypeStruct((ACC_ROWS, LANES), F32)],
        scratch_shapes=[pltpu.VMEM((tm, LANES), F32)],
        args=(dqa, dka, dva, dqb, dkb, dvb, qkraw, dgates, gq, gk, bd, *tabs))


def _in_proj_bwd(dproj, x, dx1, mod6, g1, w_in_s, tm=512):
    s = x.shape[0]
    w = IN_W // N_SHARD

    def body(dp_ref, x_ref, dx1_ref, mod_ref, g_ref, w_ref, gx_ref, acc_ref):
        dh = _nt(dp_ref[:, :w], w_ref[0])
        for j in range(1, N_SHARD):
            dh += _nt(dp_ref[:, w * j:w * (j + 1)], w_ref[j])
        dx, sums = _norm_bwd(dh, x_ref[...], g_ref[...], mod_ref[1:2, :])
        gx_ref[...] = dx1_ref[...] + dx
        _acc_add(acc_ref, sums)

    return _call(
        body, name="in_proj_bwd", grid=(s // tm,),
        in_specs=[_row_spec(tm, IN_W), _row_spec(tm, D_MODEL), _row_spec(tm, D_MODEL),
                  _full_spec(mod6.shape), _full_spec(g1.shape), _full_spec(w_in_s.shape)],
        out_specs=[_row_spec(tm, D_MODEL), _acc_spec()],
        out_shape=[jax.ShapeDtypeStruct((s, D_MODEL), F32), jax.ShapeDtypeStruct((ACC_ROWS, D_MODEL), F32)],
        compiler_params=_params(("arbitrary",)),
    )(dproj, x, dx1, mod6, g1, w_in_s)


def _wgrad(name, a, b, out_shape, out_spec, tm, tn, tk=4096):
    s, m = a.shape
    n = b.shape[1]
    tk = min(tk, s)
    nk = s // tk

    def body(a_ref, b_ref, o_ref, acc_ref):
        k = pl.program_id(2)

        @pl.when(k == 0)
        def _():
            acc_ref[...] = jnp.zeros_like(acc_ref)

        acc_ref[...] += _tn(a_ref[...], b_ref[...])

        @pl.when(k == nk - 1)
        def _():
            o_ref[...] = acc_ref[...].reshape(o_ref.shape)

    return _call(
        body, name=name, grid=(m // tm, n // tn, nk),
        in_specs=[pl.BlockSpec((tk, tm), lambda i, j, k: (k, i)), pl.BlockSpec((tk, tn), lambda i, j, k: (k, j))],
        out_specs=out_spec, out_shape=jax.ShapeDtypeStruct(out_shape, F32),
        scratch_shapes=[pltpu.VMEM((tm, tn), F32)],
        compiler_params=_params(("parallel", "parallel", "arbitrary")),
    )(a, b)


def _wgrad_branch(ya, yb, dua, dub, tk=2048):
    s = ya.shape[0]
    tk = min(tk, s)
    nk = s // tk
    w = D_MODEL // N_SHARD

    def body(ya_ref, yb_ref, dua_ref, dub_ref, o_ref, acc_ref):
        b, k = pl.program_id(0), pl.program_id(1)

        @pl.when(k == 0)
        def _():
            acc_ref[...] = jnp.zeros_like(acc_ref)

        @pl.when(b == 0)
        def _():
            acc_ref[...] += _tn(ya_ref[...], dua_ref[...])

        @pl.when(b == 1)
        def _():
            acc_ref[...] += _tn(yb_ref[...], dub_ref[...])

        @pl.when(k == nk - 1)
        def _():
            for j in range(N_SHARD):
                o_ref[j] = acc_ref[:, w * j:w * (j + 1)]

    first = lambda width: pl.BlockSpec((tk, width), lambda b, k: (k * (1 - b), 0))
    second = lambda width: pl.BlockSpec((tk, width), lambda b, k: (k * b, 0))
    return _call(
        body, name="wgrad_branch", grid=(2, nk),
        in_specs=[first(BRANCH_W), second(BRANCH_W), first(D_MODEL), second(D_MODEL)],
        out_specs=pl.BlockSpec((N_SHARD, None, BRANCH_W, w), lambda b, k: (0, b, 0, 0)),
        out_shape=jax.ShapeDtypeStruct((N_SHARD, 2, BRANCH_W, w), F32),
        scratch_shapes=[pltpu.VMEM((BRANCH_W, D_MODEL), F32)],
        compiler_params=_params(("parallel", "arbitrary")),
    )(ya, yb, dua, dub)


def _local_step(x, target, mod6, g1, g2, gf, gq2, gk2, sink, w_in_s, rest, cj=None, tabs=None):
    s = x.shape[0]
    dist = cj is not None
    tabs = _rope_tables(s) if tabs is None else tabs
    bd = _block_diag()
    sink_col = jnp.repeat(sink.reshape(KV_HEADS, GROUP, 1), TQ_B, axis=1).reshape(KV_HEADS, GROUP * TQ_B, 1)
    shard = D_MODEL // N_SHARD

    h, qkraw, qa, ka, va, qb, kb, vb, gates = _in_proj(x, mod6, g1, w_in_s, gq2, gk2, bd, tabs)
    bias = _window_bias(TQ_B)
    (yb, yb_heads, lse_b), _ = _attn_b_fwd(qb, kb, vb, sink_col, bias)
    (ya, ya_heads, lse_a), gathered = _attn_a_fwd(qa, ka, va, rider=_gather_rider(rest) if dist else None)
    wbr_s, w_out, w_mi_s, w_mo = gathered if dist else rest
    wbr_s = wbr_s.reshape(N_SHARD, 2, BRANCH_W, shard)
    w_out = w_out.reshape(D_MODEL, D_MODEL)
    w_mo = w_mo.reshape(D_FF, D_MODEL)
    ua, ub, merged, o, x1 = _post_attn(ya, yb, gates, x, mod6, wbr_s, w_out)
    h2, a, hid = _mlp_in(x1, mod6, g2, w_mi_s)
    dx2, dm, da, acc_out = _mlp_out_loss(hid, x1, a, target, mod6, gf, w_mo)

    g_w_mo = _wgrad("wgrad_mlp_out", hid, dm, (D_FF, D_MODEL), pl.BlockSpec((D_MODEL, D_MODEL), lambda i, j, k: (i, 0)),
                    D_MODEL, D_MODEL).reshape(N_SHARD, D_MODEL, D_MODEL)
    g_w_mi = _wgrad("wgrad_mlp_in", h2, da, (N_SHARD, D_MODEL, D_MODEL),
                    pl.BlockSpec((None, D_MODEL, D_MODEL), lambda i, j, k: (j, i, 0)), D_MODEL, D_MODEL)
    mlp = _Reduction(("mlp_out", "mlp_in"), (g_w_mo, g_w_mi), cj)
    (dx1, do, acc_mlp), _ = _mlp_bwd(da, x1, dx2, o, mod6, g2, w_mi_s)
    (dua, dub, dgates, doa, dob), got = _merge_bwd(do, gates, ua, ub, w_out, wbr_s, rider=mlp.swap() if dist else None)
    g_w_out = _wgrad("wgrad_out", merged, do, (D_MODEL, D_MODEL), pl.BlockSpec((D_MODEL, D_MODEL), lambda i, j, k: (i, 0)),
                     D_MODEL, D_MODEL).reshape(N_SHARD, shard, D_MODEL)
    g_wbr = _wgrad_branch(ya, yb, dua, dub)
    out = _Reduction(("out", "branch"), (g_w_out, g_wbr.reshape(N_SHARD, 2 * BRANCH_W, shard)), cj)
    (dqa, dka, dva), landings = _attn_a_bwd(qa, ka, va, ya_heads, doa, lse_a,
                                            rider=_riders(mlp.add(got), out.swap()) if dist else None)
    (dqb, dkb, dvb, dsink), joined = _attn_b_bwd(qb, kb, vb, yb_heads, dob, lse_b, sink_col, bias,
                                                 rider=mlp.total(landings[:2]) if dist else None)
    heads = (Q_HEADS, s, HEAD_DIM)
    (dproj, acc_qk), landed = _qk_bwd(dqa.reshape(heads), dka, dva, dqb.reshape(heads), dkb, dvb, qkraw, dgates, gq2, gk2, bd,
                                      tabs, rider=out.add(landings[2:]) if dist else None)
    w = IN_W // N_SHARD
    g_w_in = _wgrad("wgrad_in", h, dproj, (N_SHARD, D_MODEL, w), pl.BlockSpec((None, D_MODEL, w), lambda i, j, k: (j, i, 0)),
                    D_MODEL, w)
    grad_x, acc_in = _in_proj_bwd(dproj, x, dx1, mod6, g1, w_in_s)
    accs = (acc_out, acc_mlp, acc_in, acc_qk, dsink)
    if not dist:
        return grad_x, (g_w_in, g_wbr, g_w_out, g_w_mi, g_w_mo), accs
    r_mo, r_mi = joined
    return grad_x, (_Reduction(("in",), (g_w_in,), cj), out.total(landed), r_mi, r_mo), accs


def _me():
    return lax.axis_index("x"), lax.axis_index("y"), lax.axis_index("c")


def _peer(d):
    x, y, c = _me()
    return (1 - x if d & 4 else x, 1 - y if d & 2 else y, 1 - c if d & 1 else c)


def _dev_index(p):
    return 4 * p[0] + 2 * p[1] + p[2]


def _chip_index(p):
    return 2 * p[0] + p[1]


def _remote(src, dst, send_sem, recv_sem, to):
    return pltpu.make_async_remote_copy(src_ref=src, dst_ref=dst, send_sem=send_sem, recv_sem=recv_sem,
                                        device_id=to, device_id_type=MESH)


SLOT_ROWS = 8


def _ada_fwd(c, w_ada, b4, inv, s, rider=None):
    cols = w_ada.shape[1]
    chunk = 1024

    def tables(inv_ref, tab_refs):
        lane = lax.broadcasted_iota(jnp.int32, (chunk, LANES), 1)
        is_col = (lane & (HEAD_DIM - 1)) >= HEAD_DIM // 2
        first_a = (lane & 31) < 16
        first_b = (lane & (HEAD_DIM - 1)) < HEAD_DIM // 2

        def rows(i, carry):
            at = pl.ds(pl.multiple_of(i * chunk, chunk), chunk)
            t = i * chunk + lax.broadcasted_iota(jnp.int32, (chunk, LANES), 0)
            pos_a = jnp.where(is_col, t & (GRID_W - 1), t // GRID_W).astype(F32)
            for pos, row, first, out in ((pos_a, 0, first_a, tab_refs[:3]), (t.astype(F32), 1, first_b, tab_refs[3:])):
                ang = pos * inv_ref[row:row + 1, :]
                sin = jnp.sin(ang)
                out[0][at, :] = jnp.cos(ang)
                out[1][at, :] = jnp.where(first, -sin, 0.0)
                out[2][at, :] = jnp.where(first, 0.0, sin)
            return carry

        lax.fori_loop(0, s // chunk, rows, 0)

    def body(c_ref, w_ref, b_ref, inv_ref, mod_ref, sc_ref, ca, la, ha, cb, lb, hb, cbuf, pbuf, mbuf, send1, recv1, send2, recv2,
             launch=None):
        me = _me()
        mine, chip = _dev_index(me), _chip_index(me)
        cbuf[mine] = jnp.broadcast_to(c_ref[...], (SLOT_ROWS, D_MODEL))
        gather = [_remote(cbuf.at[mine], cbuf.at[mine], send1.at[d - 1], recv1.at[d - 1], _peer(d)) for d in range(1, N_DEV)]
        for cp in gather:
            cp.start()
        if launch is not None:
            launch()
        tables(inv_ref, (ca, la, ha, cb, lb, hb))
        for d in range(1, N_DEV):
            _remote(cbuf.at[mine], cbuf.at[_dev_index(_peer(d))], send1.at[d - 1], recv1.at[d - 1], _peer(d)).wait_recv()
        call = cbuf[...].reshape(N_DEV * SLOT_ROWS, D_MODEL)
        sc = call * _sigmoid(call)
        for s in range(N_DEV):
            sc_ref[s:s + 1, :] = sc[SLOT_ROWS * s:SLOT_ROWS * s + 1]
        part = _nn(sc.astype(BF16), w_ref[...].astype(BF16)) + b_ref[pl.ds(chip, 1), :]
        pbuf[...] = part.reshape(N_DEV, SLOT_ROWS, cols)
        mbuf[chip] = pbuf[mine]
        spread = [_remote(pbuf.at[_dev_index(_peer(d))], mbuf.at[chip], send2.at[d // 2 - 1], recv2.at[d // 2 - 1], _peer(d))
                  for d in (2, 4, 6)]
        for cp in spread:
            cp.start()
        for d in (2, 4, 6):
            _remote(pbuf.at[mine], mbuf.at[_chip_index(_peer(d))], send2.at[d // 2 - 1], recv2.at[d // 2 - 1],
                    _peer(d)).wait_recv()
        half = D_MODEL // 2
        for p in range(2 * 6):
            col = half * p
            mod_ref[p // 2:p // 2 + 1, half * (p % 2):half * (p % 2 + 1)] = mbuf[col // cols, 0:1, col % cols:col % cols + half]
        for cp in gather + spread:
            cp.wait_send()

    vm = pl.BlockSpec(memory_space=pltpu.VMEM)
    return _hosted(
        body, rider, name="ada_fwd", grid=(), in_specs=[vm] * 4, out_specs=[vm] * 8,
        out_shape=[jax.ShapeDtypeStruct((6, D_MODEL), F32), jax.ShapeDtypeStruct((N_DEV, D_MODEL), F32)]
        + [jax.ShapeDtypeStruct((s, LANES), F32)] * 6,
        scratch_shapes=[pltpu.VMEM((N_DEV, SLOT_ROWS, D_MODEL), F32), pltpu.VMEM((N_DEV, SLOT_ROWS, cols), F32),
                        pltpu.VMEM((N_SHARD, SLOT_ROWS, cols), F32),
                        pltpu.SemaphoreType.DMA((N_DEV - 1,)), pltpu.SemaphoreType.DMA((N_DEV - 1,)),
                        pltpu.SemaphoreType.DMA((N_SHARD - 1,)), pltpu.SemaphoreType.DMA((N_SHARD - 1,))],
        args=(c, w_ada, b4, inv))


PACK_ROWS = 16
PACK_W = 3 * D_MODEL


def _ada_bwd(acc_out, acc_mlp, acc_in, acc_qk, dsink, sc_all, rider=None):
    cols = 6 * D_MODEL // N_SHARD

    def body(out_ref, mlp_ref, in_ref, qk_ref, dsink_ref, sc_ref,
             gwa_ref, gba_ref, gn1_ref, gn2_ref, gf_ref, gq_ref, gk_ref, gs_ref, loss_ref, blk, send, recv, launch=None):
        me = _me()
        mine, chip = _dev_index(me), _chip_index(me)
        blk[mine] = jnp.zeros((PACK_ROWS, PACK_W), F32)
        dmod = (in_ref, 0), (in_ref, 1), (mlp_ref, 3), (mlp_ref, 0), (mlp_ref, 1), (out_ref, 2)
        half = D_MODEL // 2
        for p in range(2 * 6):
            ref, row = dmod[p // 2]
            col = half * p
            blk[mine, col // cols:col // cols + 1, col % cols:col % cols + half] = ref[row:row + 1, half * (p % 2):half * (p % 2 + 1)]
        blk[mine, 4:5, 0:D_MODEL] = in_ref[2:3, :]
        blk[mine, 4:5, D_MODEL:2 * D_MODEL] = mlp_ref[2:3, :]
        blk[mine, 4:5, 2 * D_MODEL:] = out_ref[1:2, :]
        blk[mine, 5:6, 0:LANES] = qk_ref[0:1, :]
        blk[mine, 5:6, LANES:2 * LANES] = qk_ref[1:2, :]
        blk[mine, 6:7, 0:D_MODEL] = out_ref[0:1, :]
        for g in range(KV_HEADS):
            blk[mine, 8 + GROUP * g:8 + GROUP * (g + 1), 0:LANES] = dsink_ref[g, 0:GROUP, :]
        copies = [_remote(blk.at[mine], blk.at[mine], send.at[d - 1], recv.at[d - 1], _peer(d)) for d in range(1, N_DEV)]
        for cp in copies:
            cp.start()
        if launch is not None:
            launch()
        for d in range(1, N_DEV):
            _remote(blk.at[mine], blk.at[_dev_index(_peer(d))], send.at[d - 1], recv.at[d - 1], _peer(d)).wait_recv()
        tot = blk[0]
        for s in range(1, N_DEV):
            tot = tot + blk[s]
        for j in range(N_SHARD):
            gba_ref[:, cols * j:cols * (j + 1)] = tot[j:j + 1, :cols]
        gn1_ref[...] = tot[4:5, 0:D_MODEL]
        gn2_ref[...] = tot[4:5, D_MODEL:2 * D_MODEL]
        gf_ref[...] = tot[4:5, 2 * D_MODEL:]
        gq_ref[...] = tot[5:6, 0:HEAD_DIM] + tot[5:6, HEAD_DIM:2 * HEAD_DIM]
        gk_ref[...] = tot[5:6, LANES:LANES + HEAD_DIM] + tot[5:6, LANES + HEAD_DIM:2 * LANES]
        sq = tot[8:16, 0:Q_HEADS]
        diag = lax.broadcasted_iota(jnp.int32, sq.shape, 0) == lax.broadcasted_iota(jnp.int32, sq.shape, 1)
        gs_ref[...] = jnp.sum(jnp.where(diag, sq, 0.0), axis=0, keepdims=True)
        half_mse = (0.5 / D_MODEL) * jnp.sum(tot[6:7, 0:D_MODEL], axis=-1, keepdims=True)
        loss_ref[...] = jnp.broadcast_to(half_mse, (1, LANES))
        dm = jnp.concatenate([blk[s, pl.ds(chip, 1), pl.ds(0, cols)] for s in range(N_DEV)], axis=0)
        gwa_ref[...] = _tn(sc_ref[...], dm)
        for cp in copies:
            cp.wait_send()

    vm = pl.BlockSpec(memory_space=pltpu.VMEM)
    row = lambda n: jax.ShapeDtypeStruct((1, n), F32)
    return _hosted(
        body, rider, name="ada_bwd", grid=(), in_specs=[vm] * 6, out_specs=[vm] * 9,
        out_shape=[jax.ShapeDtypeStruct((D_MODEL, cols), F32), row(6 * D_MODEL), row(D_MODEL), row(D_MODEL), row(D_MODEL),
                   row(HEAD_DIM), row(HEAD_DIM), row(Q_HEADS), row(LANES)],
        scratch_shapes=[pltpu.VMEM((N_DEV, PACK_ROWS, PACK_W), F32),
                        pltpu.SemaphoreType.DMA((N_DEV - 1,)), pltpu.SemaphoreType.DMA((N_DEV - 1,))],
        args=(acc_out, acc_mlp, acc_in, acc_qk, dsink, sc_all))


def _cast_weights(ws):
    n = len(ws)

    def body(*refs):
        src, out, tmp, sems = refs[:n], refs[n:2 * n], refs[2 * n:3 * n], refs[3 * n]
        chip = _chip_index(_me())
        copies = []
        for a in range(n):
            tmp[a][...] = src[a][...].astype(BF16)
            cp = pltpu.make_async_copy(tmp[a], out[a].at[chip], sems.at[a])
            cp.start()
            copies.append(cp)
        for cp in copies:
            cp.wait()

    vm = pl.BlockSpec(memory_space=pltpu.VMEM)
    return _call(
        body, name="cast_weights", in_specs=[vm] * n, out_specs=[ANY] * n,
        out_shape=[jax.ShapeDtypeStruct((N_SHARD,) + w.shape, BF16) for w in ws],
        scratch_shapes=[pltpu.VMEM(w.shape, BF16) for w in ws] + [pltpu.SemaphoreType.DMA((n,))],
        compiler_params=_params(),
    )(*ws)


def _half_rows(ref_rows, c):
    half = ref_rows // 2
    return pl.ds(pl.multiple_of(c * half, 8), half)


class _Rider:
    def __init__(self, inputs, out_shape, aliases, n_sems, start, finish, middle=None):
        self.inputs, self.out_shape, self.aliases, self.n_sems = list(inputs), list(out_shape), dict(aliases), n_sems
        self.start, self.finish, self.middle = start, finish, middle


def _riders(*rs):
    ins = [0]
    outs = [0]
    sems = [0]
    for r in rs:
        ins.append(ins[-1] + len(r.inputs))
        outs.append(outs[-1] + len(r.out_shape))
        sems.append(sems[-1] + r.n_sems)

    def phase(which):
        def run(in_refs, out_refs, sem):
            for k, r in enumerate(rs):
                fn = getattr(r, which)
                if fn is not None:
                    fn(in_refs[ins[k]:ins[k + 1]], out_refs[outs[k]:outs[k + 1]], lambda j, base=sems[k]: sem(base + j))
        return run

    aliases = {ins[k] + i: outs[k] + o for k, r in enumerate(rs) for i, o in r.aliases.items()}
    return _Rider([a for r in rs for a in r.inputs], [o for r in rs for o in r.out_shape], aliases, sems[-1],
                  phase("start"), phase("finish"), phase("middle") if any(r.middle for r in rs) else None)


def _hosted(body, rider, *, name, grid, in_specs, out_specs, out_shape, args, scratch_shapes=(), middle_at=None):
    where = dict(grid=grid, compiler_params=_params(("arbitrary",) * len(grid))) if grid else dict(compiler_params=_params())
    if rider is None:
        res = _call(body, name=name, in_specs=in_specs, out_specs=out_specs, out_shape=out_shape,
                    scratch_shapes=list(scratch_shapes), **where)(*args)
        return res, ()
    n_in, n_out, n_scr = len(in_specs), len(out_specs), len(scratch_shapes)
    r_in, r_out = len(rider.inputs), len(rider.out_shape)

    def riding(*refs):
        at = 0
        parts = []
        for size in (n_in, r_in, n_out, r_out, n_scr):
            parts.append(refs[at:at + size])
            at += size
        ins, rider_ins, outs, rider_outs, scratch = parts
        sems = refs[at]

        def sem_at(k):
            return sems.at[k]

        if not grid:
            body(*ins, *outs, *scratch, launch=lambda: rider.start(rider_ins, rider_outs, sem_at))
            if rider.middle is not None:
                rider.middle(rider_ins, rider_outs, sem_at)
            rider.finish(rider_ins, rider_outs, sem_at)
            return
        step = pl.program_id(0)
        for axis in range(1, len(grid)):
            step = step * grid[axis] + pl.program_id(axis)
        steps = 1
        for size in grid:
            steps *= size

        @pl.when(step == 0)
        def _():
            rider.start(rider_ins, rider_outs, sem_at)

        body(*ins, *outs, *scratch)
        if rider.middle is not None:
            @pl.when(step == middle_at)
            def _():
                rider.middle(rider_ins, rider_outs, sem_at)

        @pl.when(step == steps - 1)
        def _():
            rider.finish(rider_ins, rider_outs, sem_at)

    res = _call(
        riding, name=name, in_specs=list(in_specs) + [ANY] * r_in, out_specs=list(out_specs) + [ANY] * r_out,
        out_shape=list(out_shape) + rider.out_shape,
        input_output_aliases={n_in + i: n_out + o for i, o in rider.aliases.items()},
        scratch_shapes=list(scratch_shapes) + [pltpu.SemaphoreType.DMA((rider.n_sems,))], **where,
    )(*args, *rider.inputs)
    return res[:n_out], res[n_out:]


def _alone(name, rider):
    n_in, n_out = len(rider.inputs), len(rider.out_shape)

    def body(*refs):
        ins, outs, sems = refs[:n_in], refs[n_in:n_in + n_out], refs[n_in + n_out]

        def sem_at(k):
            return sems.at[k]

        rider.start(ins, outs, sem_at)
        if rider.middle is not None:
            rider.middle(ins, outs, sem_at)
        rider.finish(ins, outs, sem_at)

    return _call(
        body, name=name, in_specs=[ANY] * n_in, out_specs=[ANY] * n_out, out_shape=rider.out_shape,
        input_output_aliases=rider.aliases, scratch_shapes=[pltpu.SemaphoreType.DMA((rider.n_sems,))],
    )(*rider.inputs)


OTHER_CHIPS = (2, 4, 6)


def _gather_rider(stacked):
    n = len(stacked)

    def flights(bufs, sem):
        me = _me()
        chip, sib = _chip_index(me), _peer(1)
        out = []
        for a in range(n):
            mine, theirs = (_half_rows(bufs[a].shape[1], c) for c in (me[2], 1 - me[2]))
            for j, d in enumerate(OTHER_CHIPS):
                k = 3 * a + j
                from_chip = _chip_index(_peer(d))
                own, landed, passed = bufs[a].at[chip, mine], bufs[a].at[from_chip, mine], bufs[a].at[from_chip, theirs]
                out.append((_remote(own, own, sem(k), sem(3 * n + k), _peer(d)),
                            _remote(own, landed, sem(k), sem(3 * n + k), _peer(d)),
                            _remote(landed, landed, sem(6 * n + k), sem(9 * n + k), sib),
                            _remote(passed, passed, sem(6 * n + k), sem(9 * n + k), sib)))
        return out

    def start(ins, outs, sem):
        for send, _, _, _ in flights(outs, sem):
            send.start()

    def middle(ins, outs, sem):
        for _, arrival, pass_on, _ in flights(outs, sem):
            arrival.wait_recv()
            pass_on.start()

    def finish(ins, outs, sem):
        every = flights(outs, sem)
        for _, _, _, passed_to_me in every:
            passed_to_me.wait_recv()
        for send, _, pass_on, _ in every:
            send.wait_send()
            pass_on.wait_send()

    return _Rider(stacked, [jax.ShapeDtypeStruct(w.shape, w.dtype) for w in stacked], {a: a for a in range(n)}, 12 * n,
                  start, finish, middle)


def _swap_rider(grads):
    n = len(grads)

    def copies(ins, outs, sem):
        c = _me()[2]
        return [_remote(ins[a].at[pl.ds(0, N_SHARD), _half_rows(ins[a].shape[1], 1 - c)], outs[a], sem(a), sem(n + a), _peer(1))
                for a in range(n)]

    def start(ins, outs, sem):
        for cp in copies(ins, outs, sem):
            cp.start()

    def finish(ins, outs, sem):
        for cp in copies(ins, outs, sem):
            cp.wait()

    return _Rider(grads, [jax.ShapeDtypeStruct((N_SHARD, g.shape[1] // 2, g.shape[2]), F32) for g in grads], {}, 2 * n,
                  start, finish)


def _row_tile(rows):
    return min(rows, 256)


def _add_halves(name, g, got, cj):
    _, half, cols = got.shape
    tr = _row_tile(half)
    nt = half // tr

    def body(cj_ref, g_ref, got_ref, o_ref):
        o_ref[...] = (g_ref[...] + got_ref[...]).astype(BF16)

    spec = pl.BlockSpec((None, tr, cols), lambda i, s, cj: (s, i, 0))
    return _call(
        body, name=name,
        grid_spec=pltpu.PrefetchScalarGridSpec(
            num_scalar_prefetch=1, grid=(nt, N_SHARD),
            in_specs=[pl.BlockSpec((None, tr, cols), lambda i, s, cj: (s, cj[0] * nt + i, 0)), spec], out_specs=spec),
        out_shape=jax.ShapeDtypeStruct(got.shape, BF16), compiler_params=_params(("parallel", "parallel")),
    )(cj, g, got)


def _scatter_rider(sums):
    n = len(sums)

    def flights(ins, outs, sem):
        chip = _chip_index(_me())
        out = []
        for a in range(n):
            for j, d in enumerate(OTHER_CHIPS):
                k = 3 * a + j
                other = _chip_index(_peer(d))
                out.append((_remote(ins[a].at[other], outs[a].at[chip], sem(k), sem(3 * n + k), _peer(d)),
                            _remote(ins[a].at[chip], outs[a].at[other], sem(k), sem(3 * n + k), _peer(d))))
        return out

    def start(ins, outs, sem):
        for send, _ in flights(ins, outs, sem):
            send.start()

    def finish(ins, outs, sem):
        every = flights(ins, outs, sem)
        for _, arrival in every:
            arrival.wait_recv()
        for send, _ in every:
            send.wait_send()

    return _Rider(sums, [jax.ShapeDtypeStruct(v.shape, v.dtype) for v in sums], {}, 6 * n, start, finish)


def _sum_chips(name, g, got, landed, cj):
    _, half, cols = got.shape
    tr = _row_tile(half)
    nt = half // tr

    def body(cj_ref, g_ref, got_ref, landed_ref, o_ref):
        own = g_ref[...] + got_ref[...]
        total = None
        for s in range(N_SHARD):
            term = jnp.where(cj_ref[1] == s, own, landed_ref[s].astype(F32))
            total = term if total is None else total + term
        o_ref[...] = total

    return _call(
        body, name=name,
        grid_spec=pltpu.PrefetchScalarGridSpec(
            num_scalar_prefetch=1, grid=(nt,),
            in_specs=[pl.BlockSpec((None, tr, cols), lambda i, cj: (cj[1], cj[0] * nt + i, 0)),
                      pl.BlockSpec((None, tr, cols), lambda i, cj: (cj[1], i, 0)),
                      pl.BlockSpec((N_SHARD, tr, cols), lambda i, cj: (0, i, 0))],
            out_specs=pl.BlockSpec((tr, cols), lambda i, cj: (cj[0] * nt + i, 0))),
        out_shape=jax.ShapeDtypeStruct((2 * half, cols), F32), compiler_params=_params(("parallel",)),
    )(cj, g, got, landed)


def _join_rider(shards):
    n = len(shards)

    def flights(bufs, sem):
        c = _me()[2]
        out = []
        for a in range(n):
            mine, theirs = (bufs[a].at[_half_rows(bufs[a].shape[0], cc)] for cc in (c, 1 - c))
            out.append((_remote(mine, mine, sem(a), sem(n + a), _peer(1)), _remote(theirs, theirs, sem(a), sem(n + a), _peer(1))))
        return out

    def start(ins, outs, sem):
        for send, _ in flights(outs, sem):
            send.start()

    def finish(ins, outs, sem):
        for send, arrival in flights(outs, sem):
            arrival.wait_recv()
            send.wait_send()

    return _Rider(shards, [jax.ShapeDtypeStruct(h.shape, F32) for h in shards], {a: a for a in range(n)}, 2 * n, start, finish)


class _Reduction:
    def __init__(self, names, grads, cj):
        self.names, self.grads, self.cj = names, list(grads), cj

    def swap(self):
        return _swap_rider(self.grads)

    def add(self, got):
        self.got = list(got)
        self.sums = [_add_halves("add_halves_" + nm, g, h, self.cj) for nm, g, h in zip(self.names, self.grads, self.got)]
        return _scatter_rider(self.sums)

    def total(self, landed):
        halves = [_sum_chips("sum_chips_" + nm, g, h, l, self.cj)
                  for nm, g, h, l in zip(self.names, self.grads, self.got, landed)]
        return _join_rider(halves)


def _adamw_math(w, g, m, v):
    m = ADAM_B1 * m + (1.0 - ADAM_B1) * g
    v = ADAM_B2 * v + (1.0 - ADAM_B2) * jnp.square(g)
    m_hat = m / (1.0 - ADAM_B1 ** ADAM_STEP)
    v_hat = v / (1.0 - ADAM_B2 ** ADAM_STEP)
    return -ADAM_LR * (m_hat / (jnp.sqrt(v_hat) + ADAM_EPS) + ADAM_WD * w), m, v


def _adamw(name, ws, gs, ms, vs, rider=None):
    n = len(ws)
    rows = ws[0].shape[0]
    tr = _row_tile(rows)

    def body(*refs):
        ins, outs = refs[:4 * n], refs[4 * n:]
        for a in range(n):
            w, g, m, v = (ins[k * n + a][...] for k in range(4))
            outs[a][...], outs[n + a][...], outs[2 * n + a][...] = _adamw_math(w, g, m, v)

    specs = [pl.BlockSpec((tr, w.shape[1]), lambda i: (i, 0)) for w in ws]
    res, riding = _hosted(
        body, rider, name=name, grid=(rows // tr,), in_specs=specs * 4, out_specs=specs * 3,
        out_shape=[jax.ShapeDtypeStruct(w.shape, F32) for w in ws] * 3, args=(*ws, *gs, *ms, *vs))
    return (res[:n], res[n:2 * n], res[2 * n:]), riding


def _adamw_small(ws, gs, ms, vs):
    n = len(ws)

    def body(*refs):
        ins, outs = refs[:4 * n], refs[4 * n:]
        for a in range(n):
            w, g, m, v = (ins[k * n + a][...] for k in range(4))
            outs[a][...], outs[n + a][...], outs[2 * n + a][...] = _adamw_math(w, g, m, v)

    vm = pl.BlockSpec(memory_space=pltpu.VMEM)
    res = _call(
        body, name="adamw_small", in_specs=[vm] * (4 * n), out_specs=[vm] * (3 * n),
        out_shape=[jax.ShapeDtypeStruct(w.shape, F32) for w in ws] * 3, compiler_params=_params(),
    )(*ws, *gs, *ms, *vs)
    return res[:n], res[n:2 * n], res[2 * n:]


def kernel(x, c, w_ada, b_ada, norm1_g, w_in, q_norm_a, k_norm_a, sink_b, w_branch, w_out, norm2_g, w_mlp_in, w_mlp_out, final_g, loss_target, m_w_ada, m_b_ada, m_norm1_g, m_w_in, m_q_norm_a, m_k_norm_a, m_sink_b, m_w_branch, m_w_out, m_norm2_g, m_w_mlp_in, m_w_mlp_out, m_final_g, v_w_ada, v_b_ada, v_norm1_g, v_w_in, v_q_norm_a, v_k_norm_a, v_sink_b, v_w_branch, v_w_out, v_norm2_g, v_w_mlp_in, v_w_mlp_out, v_final_g):
    xi, yi, ci = _me()
    cj = jnp.stack([ci, 2 * xi + yi]).astype(jnp.int32)
    n_cols = 6 * D_MODEL // N_SHARD

    def rows2d(a):
        return a.reshape(-1, a.shape[-1])

    big = (w_in, w_branch, w_out, w_mlp_in, w_mlp_out)
    stacked = _cast_weights([rows2d(w) for w in big])
    inv_a = ROPE_THETA ** (-jnp.arange(0, HEAD_DIM // 2, 2, dtype=F32) / (HEAD_DIM // 2))
    inv_b = ROPE_THETA ** (-jnp.arange(0, HEAD_DIM, 2, dtype=F32) / HEAD_DIM)
    inv = jnp.stack([jnp.tile(inv_a, LANES // inv_a.shape[0]), jnp.tile(inv_b, LANES // inv_b.shape[0])])
    (mod6, sc_all, *tabs), (w_in_s,) = _ada_fwd(c, w_ada[0], b_ada.reshape(N_SHARD, n_cols), inv, x.shape[1],
                                                rider=_gather_rider(stacked[:1]))
    rest = stacked[1:]

    gq2 = jnp.tile(q_norm_a, (1, 2))
    gk2 = jnp.tile(k_norm_a, (1, 2))
    grad_x, (w_in_red, join_out, g_mi, g_mo), accs = _local_step(
        x[0], loss_target[0], mod6, norm1_g, norm2_g, final_g.reshape(1, D_MODEL), gq2, gk2, sink_b[0], w_in_s, rest, cj, tabs)

    (g_w_ada, g_b_ada, g_n1, g_n2, g_f, g_q, g_k, g_s, loss_row), got_in = _ada_bwd(*accs, sc_all, rider=w_in_red.swap())
    loss = loss_row[0, 0]
    moments = dict(w_ada=(m_w_ada, v_w_ada), w_in=(m_w_in, v_w_in), w_branch=(m_w_branch, v_w_branch), w_out=(m_w_out, v_w_out),
                   w_mlp_in=(m_w_mlp_in, v_w_mlp_in), w_mlp_out=(m_w_mlp_out, v_w_mlp_out))
    weights = dict(w_ada=w_ada, w_in=w_in, w_branch=w_branch, w_out=w_out, w_mlp_in=w_mlp_in, w_mlp_out=w_mlp_out)

    def adamw(call, names, grads, rider=None):
        (d, m, v), riding = _adamw(call, [rows2d(weights[nm]) for nm in names], grads,
                                   [rows2d(moments[nm][0]) for nm in names], [rows2d(moments[nm][1]) for nm in names], rider)
        return {nm: (grads[k], d[k], m[k], v[k]) for k, nm in enumerate(names)}, riding

    big_res, landed_in = adamw("adamw_ada_mlp", ("w_ada", "w_mlp_in", "w_mlp_out"), [g_w_ada, g_mi, g_mo], w_in_red.add(got_in))
    g_in, g_out, g_br = _alone("join_in_out_branch", _riders(w_in_red.total(landed_in), join_out))
    big_res.update(adamw("adamw_in_branch", ("w_in", "w_branch"), [g_in, g_br])[0])

    small = ("b_ada", "norm1_g", "q_norm_a", "k_norm_a", "sink_b", "norm2_g", "final_g", "w_out")
    row = lambda a: a.reshape(1, -1)
    small_w = [row(a) for a in (b_ada, norm1_g, q_norm_a, k_norm_a, sink_b, norm2_g, final_g)] + [w_out[0]]
    small_g = [g_b_ada, g_n1, g_q, g_k, g_s, g_n2, g_f, g_out]
    small_m = [row(a) for a in (m_b_ada, m_norm1_g, m_q_norm_a, m_k_norm_a, m_sink_b, m_norm2_g, m_final_g)] + [m_w_out[0]]
    small_v = [row(a) for a in (v_b_ada, v_norm1_g, v_q_norm_a, v_k_norm_a, v_sink_b, v_norm2_g, v_final_g)] + [v_w_out[0]]
    s_d, s_m, s_v = _adamw_small(small_w, small_g, small_m, small_v)

    order = ("w_ada", "b_ada", "norm1_g", "w_in", "q_norm_a", "k_norm_a", "sink_b", "w_branch", "w_out", "norm2_g",
             "w_mlp_in", "w_mlp_out", "final_g")
    like = dict(w_ada=w_ada, b_ada=b_ada, norm1_g=norm1_g, w_in=w_in, q_norm_a=q_norm_a, k_norm_a=k_norm_a, sink_b=sink_b,
                w_branch=w_branch, w_out=w_out, norm2_g=norm2_g, w_mlp_in=w_mlp_in, w_mlp_out=w_mlp_out, final_g=final_g)
    grad, delta, new_m, new_v = {}, {}, {}, {}
    for nm, res in big_res.items():
        grad[nm], delta[nm], new_m[nm], new_v[nm] = res
    for k, nm in enumerate(small):
        grad[nm], delta[nm], new_m[nm], new_v[nm] = small_g[k], s_d[k], s_m[k], s_v[k]
    outs = [loss, grad_x[None]]
    for group in (grad, delta, new_m, new_v):
        outs += [group[nm].reshape(like[nm].shape) for nm in order]
    return tuple(outs)
```
